```python
import math
import jax, jax.numpy as jnp
from jax import lax
import numpy as np

D_MODEL = 1024
BATCH = 8
SEQ = 2048
DEPTH = 2

N_BRANCH = 4
BRANCH_WIDTH = D_MODEL // N_BRANCH
GLA_HEADS = 4
GLA_DK = BRANCH_WIDTH // (2 * GLA_HEADS)
GLA_DV = BRANCH_WIDTH // GLA_HEADS
GLA_GATE_RANK = 16
GLA_GATE_NORMALIZER = 16.0
GLA_CHUNK = 32
DIFF_HEADS = 4
DIFF_DQK = BRANCH_WIDTH // (2 * DIFF_HEADS)
DIFF_DV = 2 * DIFF_DQK
DIFF_QBLOCK = 128
ROPE_THETA = 10000.0
CONV_CH = BRANCH_WIDTH
CONV_WIDTH = 31
RWKV_HEADS = 4
RWKV_HEAD = BRANCH_WIDTH // RWKV_HEADS
RWKV_DECAY_RANK = 64
RWKV_AAA_RANK = 64
RWKV_GATE_RANK = 128
RWKV_SIZES = (BRANCH_WIDTH, BRANCH_WIDTH, BRANCH_WIDTH, RWKV_DECAY_RANK, RWKV_AAA_RANK, RWKV_GATE_RANK)
RWKV_COLS = sum(RWKV_SIZES)
IN_SIZES = (GLA_HEADS * GLA_DK, GLA_HEADS * GLA_DK, GLA_HEADS * GLA_DV, GLA_HEADS * GLA_DV, GLA_GATE_RANK,
            DIFF_HEADS * 2 * DIFF_DQK, DIFF_HEADS * 2 * DIFF_DQK, DIFF_HEADS * DIFF_DV,
            CONV_CH, CONV_CH,
            RWKV_COLS,
            N_BRANCH * D_MODEL)
N_IN = sum(IN_SIZES)
D_FF = ((8 * D_MODEL // 3 + 255) // 256) * 256
N_EXPERTS = 8
TOP_K = 2
D_EXPERT = 7 * D_MODEL // 2
N_DENSE = (DEPTH + 1) // 2
N_MOE = DEPTH // 2
RMS_EPS = 1e-6
LN_EPS = 1e-5
RWKV_GN_EPS = 64e-5

kernel_name = "hybrid_gated_gla_diff_conv_rwkv7_moe_block"

F32 = jnp.float32


def _split(t, sizes):
    idx = [int(i) for i in np.cumsum(sizes)[:-1]]
    return jnp.split(t, idx, axis=-1)


def rms_norm(x, g, eps=RMS_EPS):
    xf = x.astype(F32)
    y = xf * lax.rsqrt(jnp.mean(xf * xf, axis=-1, keepdims=True) + eps)
    return (y * g.astype(F32)).astype(x.dtype)


def layer_norm(x, g, b, eps=LN_EPS):
    xf = x.astype(F32)
    mu = jnp.mean(xf, axis=-1, keepdims=True)
    xc = xf - mu
    var = jnp.mean(xc * xc, axis=-1, keepdims=True)
    return (xc * lax.rsqrt(var + eps) * g.astype(F32) + b.astype(F32)).astype(x.dtype)


def rope_tables(positions):
    inv = 1.0 / (ROPE_THETA ** (jnp.arange(0, DIFF_DQK, 2, dtype=F32) / DIFF_DQK))
    ang = positions.astype(F32)[..., None] * inv
    return jnp.cos(ang), jnp.sin(ang)


def apply_rope(t, cos, sin):
    tf = t.astype(F32)
    t1, t2 = jnp.split(tf, 2, axis=-1)
    c = cos[:, :, None, None, :]
    s = sin[:, :, None, None, :]
    return jnp.concatenate([t1 * c - t2 * s, t1 * s + t2 * c], axis=-1)


def gla_mixer(q, k, v, og, gz, gate_w2, gate_b, norm_g):
    B, T, _ = q.shape
    n = T // GLA_CHUNK

    def heads(t, d):
        return t.astype(F32).reshape(B, n, GLA_CHUNK, GLA_HEADS, d).transpose(0, 3, 1, 2, 4)

    gk = jax.nn.log_sigmoid(gz.astype(F32) @ gate_w2.astype(F32) + gate_b.astype(F32)) / GLA_GATE_NORMALIZER
    qh = heads(q, GLA_DK) * (GLA_DK ** -0.5)
    kh = heads(k, GLA_DK)
    vh = heads(v, GLA_DV)
    G = jnp.cumsum(heads(gk, GLA_DK), axis=3)
    idx = jnp.arange(GLA_CHUNK)
    causal = idx[:, None] >= idx[None, :]
    gdiff = G[..., :, None, :] - G[..., None, :, :]
    decay = jnp.exp(jnp.where(causal[..., None], gdiff, -jnp.inf))
    A = jnp.einsum('bhnid,bhnjd,bhnijd->bhnij', qh, kh, decay)
    o_intra = jnp.einsum('bhnij,bhnjv->bhniv', A, vh)
    G_last = G[..., -1:, :]
    q_dec = qh * jnp.exp(G)
    k_dec = kh * jnp.exp(G_last - G)
    chunk_decay = jnp.exp(G_last[..., 0, :])

    def step(S, inp):
        qd, kd, vc, cd = inp
        o = jnp.einsum('bhid,bhdv->bhiv', qd, S)
        S = cd[..., None] * S + jnp.einsum('bhjd,bhjv->bhdv', kd, vc)
        return S, o

    S0 = jnp.zeros((B, GLA_HEADS, GLA_DK, GLA_DV), F32)
    mv = lambda t: jnp.moveaxis(t, 2, 0)
    _, o_inter = lax.scan(step, S0, (mv(q_dec), mv(k_dec), mv(vh), mv(chunk_decay)))
    o = o_intra + jnp.moveaxis(o_inter, 0, 2)
    o = o.transpose(0, 2, 3, 1, 4).reshape(B, T, GLA_HEADS, GLA_DV)
    o = rms_norm(o, norm_g) * jax.nn.silu(og.astype(F32)).reshape(B, T, GLA_HEADS, GLA_DV)
    return o.reshape(B, T, GLA_HEADS * GLA_DV).astype(q.dtype)


def diff_attention(q, k, v, cos, sin, lam_params, subln_g, layer_idx):
    B, T, _ = q.shape
    H, d = DIFF_HEADS, DIFF_DQK
    qh = apply_rope(q.reshape(B, T, H, 2, d), cos, sin).transpose(0, 2, 3, 1, 4) * (d ** -0.5)
    kh = apply_rope(k.reshape(B, T, H, 2, d), cos, sin).transpose(0, 2, 3, 1, 4)
    vh = v.astype(F32).reshape(B, T, H, DIFF_DV).transpose(0, 2, 1, 3)
    lam_init = 0.8 - 0.6 * math.exp(-0.3 * layer_idx)
    lp = lam_params.astype(F32)
    lam = jnp.exp(jnp.sum(lp[0] * lp[1])) - jnp.exp(jnp.sum(lp[2] * lp[3])) + lam_init
    outs = []
    for blk in range(T // DIFF_QBLOCK):
        start = blk * DIFF_QBLOCK
        end = start + DIFF_QBLOCK
        s = jnp.einsum('bhcqd,bhckd->bhcqk', qh[:, :, :, start:end], kh[:, :, :, :end])
        qpos = start + jnp.arange(DIFF_QBLOCK)
        kpos = jnp.arange(end)
        s = jnp.where(kpos[None, :] <= qpos[:, None], s, -jnp.inf)
        p = jax.nn.softmax(s, axis=-1)
        attn = p[:, :, 0] - lam * p[:, :, 1]
        outs.append(jnp.einsum('bhqk,bhkv->bhqv', attn, vh[:, :, :end]))
    o = jnp.concatenate(outs, axis=2)
    o = rms_norm(o, subln_g) * (1.0 - lam_init)
    return o.transpose(0, 2, 1, 3).reshape(B, T, H * DIFF_DV).astype(v.dtype)


def conformer_conv(a, b, conv_w, conv_b, ln_g, ln_b):
    u = a.astype(F32) * jax.nn.sigmoid(b.astype(F32))
    y = lax.conv_general_dilated(u, conv_w.astype(F32)[:, None, :], window_strides=(1,),
                                 padding=[(CONV_WIDTH - 1, 0)],
                                 dimension_numbers=('NWC', 'WIO', 'NWC'),
                                 feature_group_count=CONV_CH) + conv_b.astype(F32)
    y = layer_norm(y, ln_g, ln_b)
    return jax.nn.silu(y).astype(a.dtype)


def rwkv7_time_mix(cols, mu, w0, w2, a0, a2, g2, k_k, k_a, r_k, ln_g, ln_b):
    B, T, _ = cols.shape
    H, N = RWKV_HEADS, RWKV_HEAD
    xf = cols.astype(F32)
    prev = jnp.pad(xf, ((0, 0), (1, 0), (0, 0)))[:, :-1]
    xm = xf + (prev - xf) * mu.astype(F32)
    r, k, v, zw, za, zg = _split(xm, RWKV_SIZES)
    w = -jax.nn.softplus(-(w0.astype(F32) + jnp.tanh(zw) @ w2.astype(F32))) - 0.5
    decay = jnp.exp(-jnp.exp(w))
    a = jax.nn.sigmoid(a0.astype(F32) + za @ a2.astype(F32))
    g = jax.nn.sigmoid(zg) @ g2.astype(F32)
    hd = lambda t: t.reshape(B, T, H, N)
    kk = hd(k * k_k.astype(F32))
    kk = kk / jnp.maximum(jnp.sqrt(jnp.sum(kk * kk, axis=-1, keepdims=True)), 1e-12)
    k = k * (1.0 + (a - 1.0) * k_a.astype(F32))
    rh, wh, kh, vh, ah = hd(r), hd(decay), hd(k), hd(v), hd(a)

    def step(S, inp):
        rt, wt, kt, vt, kkt, at = inp
        sa = jnp.einsum('bhvk,bhk->bhv', S, -kkt)
        S = S * wt[:, :, None, :] + sa[..., None] * (kkt * at)[:, :, None, :] + vt[..., None] * kt[:, :, None, :]
        return S, jnp.einsum('bhvk,bhk->bhv', S, rt)

    seq_first = tuple(jnp.moveaxis(t, 1, 0) for t in (rh, wh, kh, vh, kk, ah))
    S0 = jnp.zeros((B, H, N, N), F32)
    _, y = lax.scan(step, S0, seq_first)
    y = jnp.moveaxis(y, 0, 1)
    y = layer_norm(y, ln_g.reshape(H, N), ln_b.reshape(H, N), eps=RWKV_GN_EPS)
    bonus = jnp.sum(rh * kh * r_k.astype(F32), axis=-1, keepdims=True) * vh
    y = (y + bonus).reshape(B, T, H * N) * g
    return y.astype(cols.dtype)


def swiglu(h, wg, wu, wd):
    return (jax.nn.silu(h @ wg) * (h @ wu)) @ wd


def moe_swiglu(h, router_w, router_b, wg, wu, wd):
    logits = h.astype(F32) @ router_w.astype(F32) + router_b.astype(F32)
    top_v, top_i = lax.top_k(logits, TOP_K)
    top_p = jax.nn.softmax(top_v, axis=-1)
    combine = jnp.sum(jax.nn.one_hot(top_i, N_EXPERTS, dtype=F32) * top_p[..., None], axis=-2)
    out = jnp.zeros(h.shape, F32)
    for e in range(N_EXPERTS):
        out = out + combine[..., e:e + 1] * swiglu(h, wg[e], wu[e], wd[e]).astype(F32)
    return out.astype(h.dtype)


def setup_inputs(seed: int = 0) -> dict:
    key = jax.random.key(seed)
    keys = jax.random.split(key, 48)
    counter = [0]

    def nk():
        kk = keys[counter[0]]
        counter[0] += 1
        return kk

    def nrm(shape, scale):
        return jax.random.normal(nk(), shape, F32) * scale

    def gain(shape):
        return 1.0 + nrm(shape, 0.02)

    D, L = D_MODEL, DEPTH
    W = BRANCH_WIDTH
    x = nrm((BATCH, SEQ, D), 1.0)
    c = nrm((BATCH, D), 1.0)
    positions = (jax.random.randint(nk(), (BATCH, 1), 0, 1024, dtype=jnp.int32)
                 + jnp.arange(SEQ, dtype=jnp.int32)[None, :])
    return {
        "x": x,
        "c": c,
        "positions": positions,
        "ada_w": nrm((L, D, 6 * D), 0.5 * D ** -0.5),
        "ada_b": nrm((L, 6 * D), 0.02),
        "norm_mix_pre": gain((L, D)),
        "norm_mix_post": gain((L, D)),
        "norm_ffn_pre": gain((L, D)),
        "norm_ffn_post": gain((L, D)),
        "w_in": nrm((L, D, N_IN), D ** -0.5),
        "gla_gate_w2": nrm((L, GLA_GATE_RANK, GLA_HEADS * GLA_DK), GLA_GATE_RANK ** -0.5),
        "gla_gate_b": 1.0 + nrm((L, GLA_HEADS * GLA_DK), 0.5),
        "gla_norm": gain((L, GLA_DV)),
        "diff_lambda": nrm((L, 4, DIFF_DQK), 0.1),
        "diff_subln": gain((L, DIFF_DV)),
        "conv_w": nrm((L, CONV_WIDTH, CONV_CH), CONV_WIDTH ** -0.5),
        "conv_b": nrm((L, CONV_CH), 0.02),
        "conv_ln_g": gain((L, CONV_CH)),
        "conv_ln_b": nrm((L, CONV_CH), 0.02),
        "rwkv_mu": jax.random.uniform(nk(), (L, RWKV_COLS), F32),
        "rwkv_w0": jax.random.uniform(nk(), (L, W), F32, minval=-3.0, maxval=1.0),
        "rwkv_w2": nrm((L, RWKV_DECAY_RANK, W), 0.5 * RWKV_DECAY_RANK ** -0.5),
        "rwkv_a0": nrm((L, W), 0.5),
        "rwkv_a2": nrm((L, RWKV_AAA_RANK, W), 0.5 * RWKV_AAA_RANK ** -0.5),
        "rwkv_g2": nrm((L, RWKV_GATE_RANK, W), RWKV_GATE_RANK ** -0.5),
        "rwkv_k_k": 0.85 + nrm((L, W), 0.05),
        "rwkv_k_a": 1.0 + nrm((L, W), 0.05),
        "rwkv_r_k": nrm((L, RWKV_HEADS, RWKV_HEAD), 0.1),
        "rwkv_ln_g": gain((L, W)),
        "rwkv_ln_b": nrm((L, W), 0.02),
        "w_branch": nrm((L, N_BRANCH, W, D), W ** -0.5),
        "w_out": nrm((L, D, D), D ** -0.5),
        "ffn_w_gate": nrm((N_DENSE, D, D_FF), D ** -0.5),
        "ffn_w_up": nrm((N_DENSE, D, D_FF), D ** -0.5),
        "ffn_w_down": nrm((N_DENSE, D_FF, D), D_FF ** -0.5),
        "router_w": nrm((N_MOE, D, N_EXPERTS), D ** -0.5),
        "router_b": nrm((N_MOE, N_EXPERTS), 0.01),
        "moe_w_gate": nrm((N_MOE, N_EXPERTS, D, D_EXPERT), D ** -0.5),
        "moe_w_up": nrm((N_MOE, N_EXPERTS, D, D_EXPERT), D ** -0.5),
        "moe_w_down": nrm((N_MOE, N_EXPERTS, D_EXPERT, D), D_EXPERT ** -0.5),
    }


def reference(x, c, positions, ada_w, ada_b, norm_mix_pre, norm_mix_post, norm_ffn_pre, norm_ffn_post,
              w_in, gla_gate_w2, gla_gate_b, gla_norm, diff_lambda, diff_subln,
              conv_w, conv_b, conv_ln_g, conv_ln_b,
              rwkv_mu, rwkv_w0, rwkv_w2, rwkv_a0, rwkv_a2, rwkv_g2, rwkv_k_k, rwkv_k_a, rwkv_r_k,
              rwkv_ln_g, rwkv_ln_b, w_branch, w_out,
              ffn_w_gate, ffn_w_up, ffn_w_down,
              router_w, router_b, moe_w_gate, moe_w_up, moe_w_down):
    B, T, D = x.shape
    cos, sin = rope_tables(positions)
    c_act = jax.nn.silu(c)
    for l in range(DEPTH):
        mod = (c_act @ ada_w[l] + ada_b[l])[:, None, :]
        sh_m, sc_m, gt_m, sh_f, sc_f, gt_f = jnp.split(mod, 6, axis=-1)

        h = rms_norm(x, norm_mix_pre[l]) * (1.0 + sc_m) + sh_m
        proj = h @ w_in[l]
        (g_q, g_k, g_v, g_o, g_z, d_q, d_k, d_v, c_a, c_b, r_cols, gate_cols) = _split(proj, IN_SIZES)
        o_gla = gla_mixer(g_q, g_k, g_v, g_o, g_z, gla_gate_w2[l], gla_gate_b[l], gla_norm[l])
        o_diff = diff_attention(d_q, d_k, d_v, cos, sin, diff_lambda[l], diff_subln[l], l)
        o_conv = conformer_conv(c_a, c_b, conv_w[l], conv_b[l], conv_ln_g[l], conv_ln_b[l])
        o_rwkv = rwkv7_time_mix(r_cols, rwkv_mu[l], rwkv_w0[l], rwkv_w2[l], rwkv_a0[l], rwkv_a2[l],
                                rwkv_g2[l], rwkv_k_k[l], rwkv_k_a[l], rwkv_r_k[l],
                                rwkv_ln_g[l], rwkv_ln_b[l])
        branches = jnp.stack([o_gla.astype(h.dtype), o_diff.astype(h.dtype),
                              o_conv.astype(h.dtype), o_rwkv.astype(h.dtype)], axis=2)
        branch_d = jnp.einsum('btgc,gcd->btgd', branches, w_branch[l])
        gates = jax.nn.sigmoid(gate_cols.reshape(B, T, N_BRANCH, D))
        merged = jnp.sum(gates * branch_d, axis=2)
        y = merged @ w_out[l]
        x = (x + gt_m * rms_norm(y, norm_mix_post[l])).astype(x.dtype)

        h = rms_norm(x, norm_ffn_pre[l]) * (1.0 + sc_f) + sh_f
        if l % 2 == 0:
            i = l // 2
            y = swiglu(h, ffn_w_gate[i], ffn_w_up[i], ffn_w_down[i])
        else:
            i = l // 2
            y = moe_swiglu(h, router_w[i], router_b[i], moe_w_gate[i], moe_w_up[i], moe_w_down[i])
        x = (x + gt_f * rms_norm(y, norm_ffn_post[l])).astype(x.dtype)
    return x
```

```python
import functools
import math

import jax
import jax.numpy as jnp
from jax import lax
from jax.experimental import pallas as pl
from jax.experimental.pallas import tpu as pltpu

F32 = jnp.float32
BF16 = jnp.bfloat16
HIGHEST = lax.Precision.HIGHEST

N_BRANCH = 4
GLA_HEADS = 4
GLA_GATE_RANK = 16
GLA_GATE_NORMALIZER = 16.0
GLA_CHUNK = 32
DIFF_HEADS = 4
ROPE_THETA = 10000.0
CONV_WIDTH = 31
RWKV_HEADS = 4
RWKV_CHUNK = 64
RWKV_SUB = 16
N_EXPERTS = 8
RMS_EPS = 1e-6
LN_EPS = 1e-5
RWKV_GN_EPS = 64e-5
LANES = 128
VMEM_LIMIT = 56 * 1024 * 1024


def _cparams(sem):
    return pltpu.CompilerParams(dimension_semantics=sem, vmem_limit_bytes=VMEM_LIMIT)


def _mm(a, b):
    return jnp.dot(a.astype(BF16), b.astype(BF16), preferred_element_type=F32)


def _mm_nt(a, b):
    return lax.dot_general(a.astype(BF16), b.astype(BF16), (((1,), (1,)), ((), ())),
                           preferred_element_type=F32)


def _mm_tn(a, b):
    return lax.dot_general(a.astype(BF16), b.astype(BF16), (((0,), (0,)), ((), ())),
                           preferred_element_type=F32)


def _bmm(a, b):
    return lax.dot_general(a.astype(BF16), b.astype(BF16), (((2,), (1,)), ((0,), (0,))),
                           preferred_element_type=F32)


def _mm_f32(a, b):
    return jnp.dot(a, b, precision=HIGHEST, preferred_element_type=F32)


def _sigmoid(x):
    return 1.0 / (1.0 + jnp.exp(-x))


def _silu(x):
    return x * _sigmoid(x)


def _softplus(x):
    return jnp.maximum(x, 0.0) + jnp.log(1.0 + jnp.exp(-jnp.abs(x)))


def _group_matrix(n, group):
    r = lax.broadcasted_iota(jnp.int32, (n, n), 0) // group
    c = lax.broadcasted_iota(jnp.int32, (n, n), 1) // group
    return r == c


def _rms_mod(x, gain, scale, shift):
    y = x * lax.rsqrt(jnp.mean(x * x, axis=-1, keepdims=True) + RMS_EPS)
    return y * gain * (1.0 + scale) + shift


def _mod_kernel(c_ref, w_ref, b_ref, o_ref):
    o_ref[0] = _mm_f32(_silu(c_ref[...]), w_ref[0]) + b_ref[0]


def _modulation(c, ada_w, ada_b):
    L, D, M = ada_w.shape
    B = c.shape[0]
    tn = M // 4
    return pl.pallas_call(
        _mod_kernel,
        out_shape=jax.ShapeDtypeStruct((L, B, M), F32),
        grid=(L, M // tn),
        in_specs=[pl.BlockSpec((B, D), lambda l, j: (0, 0)),
                  pl.BlockSpec((1, D, tn), lambda l, j: (l, 0, j)),
                  pl.BlockSpec((1, 1, tn), lambda l, j: (l, 0, j))],
        out_specs=pl.BlockSpec((1, B, tn), lambda l, j: (l, 0, j)),
        compiler_params=_cparams(("arbitrary", "arbitrary")),
        name="adaln_mod",
    )(c, ada_w, ada_b.reshape(L, 1, M))


def _inproj_kernel(x_ref, sc_ref, sh_ref, g_ref, w_ref, o_ref):
    h = _rms_mod(x_ref[0], g_ref[...], sc_ref[0], sh_ref[0])
    o_ref[0] = _mm(h, w_ref[...])


def _in_projection(x, sc, sh, gain, w, tm):
    B, T, D = x.shape
    n = w.shape[1]
    return pl.pallas_call(
        _inproj_kernel,
        out_shape=jax.ShapeDtypeStruct((B, T, n), F32),
        grid=(B, T // tm),
        in_specs=[pl.BlockSpec((1, tm, D), lambda b, i: (b, i, 0)),
                  pl.BlockSpec((1, 1, D), lambda b, i: (b, 0, 0)),
                  pl.BlockSpec((1, 1, D), lambda b, i: (b, 0, 0)),
                  pl.BlockSpec((1, D), lambda b, i: (0, 0)),
                  pl.BlockSpec((D, n), lambda b, i: (0, 0))],
        out_specs=pl.BlockSpec((1, tm, n), lambda b, i: (b, i, 0)),
        compiler_params=_cparams(("arbitrary", "arbitrary")),
        name="in_proj",
    )(x, sc, sh, gain, w)


def _gla_kernel(qk_ref, v_ref, og_ref, gz_ref, w2_ref, gb_ref, ng_ref, o_ref, s_ref, *, n_chunk):
    C = GLA_CHUNK
    hk = qk_ref.shape[2] // 2
    hv = v_ref.shape[2]
    dk = hk // GLA_HEADS
    dv = hv // GLA_HEADS

    @pl.when(pl.program_id(1) == 0)
    def _():
        s_ref[...] = jnp.zeros_like(s_ref)

    ti = lax.broadcasted_iota(jnp.int32, (C, C), 0)
    tj = lax.broadcasted_iota(jnp.int32, (C, C), 1)
    tril = (ti >= tj).astype(F32)
    pr = lax.broadcasted_iota(jnp.int32, (C * C, C), 0)
    pc = lax.broadcasted_iota(jnp.int32, (C * C, C), 1)
    row_sel = (pr // C == pc).astype(F32)
    pi = lax.broadcasted_iota(jnp.int32, (C * C, 1), 0)
    causal = (pi % C) <= (pi // C)
    er = lax.broadcasted_iota(jnp.int32, (hk, hv), 0) // dk
    ec = lax.broadcasted_iota(jnp.int32, (hk, hv), 1) // dv
    expand = (er == ec).astype(BF16)
    sr = lax.broadcasted_iota(jnp.int32, (hv, hk), 0) // dv
    scol = lax.broadcasted_iota(jnp.int32, (hv, hk), 1) // dk
    state_mask = sr == scol
    head_mean = _group_matrix(hv, dv).astype(F32) / dv

    def chunk(ci, carry):
        r0 = pl.multiple_of(ci * C, C)
        rows = pl.ds(r0, C)
        q = qk_ref[0, rows, 0:hk] * (dk ** -0.5)
        k = qk_ref[0, rows, hk:2 * hk]
        v = v_ref[0, rows, :]
        z = _mm_f32(gz_ref[0, rows, :], w2_ref[...]) + gb_ref[...]
        gk = (jnp.minimum(z, 0.0) - jnp.log(1.0 + jnp.exp(-jnp.abs(z)))) / GLA_GATE_NORMALIZER
        G = _mm_f32(tril, gk)
        gq_i = _mm_f32(row_sel, jnp.concatenate([G, q], axis=1))
        G_j = jnp.broadcast_to(G[None], (C, C, hk)).reshape(C * C, hk)
        k_j = jnp.broadcast_to(k[None], (C, C, hk)).reshape(C * C, hk)
        decay = jnp.exp(jnp.where(causal, gq_i[:, 0:hk] - G_j, -jnp.inf))
        p = gq_i[:, hk:2 * hk] * k_j * decay
        a_exp = jnp.dot(p.astype(BF16), expand, preferred_element_type=F32)
        v_j = jnp.broadcast_to(v[None], (C, C, hv))
        o_intra = jnp.sum(a_exp.reshape(C, C, hv) * v_j, axis=1)
        g_last = G[C - 1:C, :]
        s = s_ref[...]
        o_inter = _mm_nt(q * jnp.exp(G), s)
        kv = _mm_tn(v, k * jnp.exp(g_last - G))
        s_ref[...] = s * jnp.exp(g_last) + jnp.where(state_mask, kv, 0.0)
        o = o_intra + o_inter
        ms = _mm_f32(o * o, head_mean)
        o = o * lax.rsqrt(ms + RMS_EPS) * ng_ref[...] * _silu(og_ref[0, rows, :])
        o_ref[0, rows, :] = o.astype(o_ref.dtype)
        return carry

    lax.fori_loop(0, n_chunk, chunk, 0)


def _gla(proj, w2p, gb, ng, col_qk, col_v, col_og, col_gz, tb):
    B, T, _ = proj.shape
    hv = ng.shape[1]
    hk = gb.shape[1]
    return pl.pallas_call(
        functools.partial(_gla_kernel, n_chunk=tb // GLA_CHUNK),
        out_shape=jax.ShapeDtypeStruct((B, T, hv), BF16),
        grid=(B, T // tb),
        in_specs=[pl.BlockSpec((1, tb, 2 * hk), lambda b, j: (b, j, col_qk)),
                  pl.BlockSpec((1, tb, hv), lambda b, j: (b, j, col_v)),
                  pl.BlockSpec((1, tb, hv), lambda b, j: (b, j, col_og)),
                  pl.BlockSpec((1, tb, LANES), lambda b, j: (b, j, col_gz)),
                  pl.BlockSpec(w2p.shape, lambda b, j: (0, 0)),
                  pl.BlockSpec(gb.shape, lambda b, j: (0, 0)),
                  pl.BlockSpec(ng.shape, lambda b, j: (0, 0))],
        out_specs=pl.BlockSpec((1, tb, hv), lambda b, j: (b, j, 0)),
        scratch_shapes=[pltpu.VMEM((hv, hk), F32)],
        compiler_params=_cparams(("arbitrary", "arbitrary")),
        name="gla_mixer",
    )(proj, proj, proj, proj, w2p, gb, ng)


def _rope(t, cos, sin_signed):
    d = 32
    half = d // 2
    out = []
    for s in range(t.shape[1] // LANES):
        x = t[:, s * LANES:(s + 1) * LANES]
        lane = lax.broadcasted_iota(jnp.int32, x.shape, 1)
        up = pltpu.roll(x, LANES - half, 1)
        down = pltpu.roll(x, half, 1)
        rot = jnp.where((lane % d) < half, up, down)
        out.append(x * cos[:, s * LANES:(s + 1) * LANES] + rot * sin_signed[:, s * LANES:(s + 1) * LANES])
    return jnp.concatenate(out, axis=1)


def _diff_kernel(q_ref, k_ref, v_ref, cosq_ref, sinq_ref, cosk_ref, sink_ref, lam_ref, g_ref, o_ref,
                 ks, vs, *, tq, lam_init):
    H = DIFF_HEADS
    d = q_ref.shape[2] // (2 * H)
    dv = v_ref.shape[2] // H
    j = pl.program_id(1)

    @pl.when(j == 0)
    def _():
        k = _rope(k_ref[0], cosk_ref[0], sink_ref[0])
        v = v_ref[0]
        for hc in range(2 * H):
            ks[hc] = k[:, hc * d:(hc + 1) * d].astype(BF16)
        for h in range(H):
            vs[h] = v[:, h * dv:(h + 1) * dv].astype(BF16)

    q = _rope(q_ref[0], cosq_ref[0], sinq_ref[0]) * (d ** -0.5)
    lp = lam_ref[...]
    lam = (jnp.exp(jnp.sum(lp[0:1] * lp[1:2], axis=-1, keepdims=True))
           - jnp.exp(jnp.sum(lp[2:3] * lp[3:4], axis=-1, keepdims=True)) + lam_init)
    row = lax.broadcasted_iota(jnp.int32, (tq, tq), 0) + j * tq
    col0 = lax.broadcasted_iota(jnp.int32, (tq, tq), 1)

    def softmax_pv(qh, hc, h):
        def kv_step(kb, carry):
            m, l, acc = carry
            k0 = pl.multiple_of(kb * tq, tq)
            s = lax.dot_general(qh, ks[hc, pl.ds(k0, tq), :], (((1,), (1,)), ((), ())),
                                preferred_element_type=F32)
            s = jnp.where(col0 + k0 <= row, s, -jnp.inf)
            m_new = jnp.maximum(m, jnp.max(s, axis=-1, keepdims=True))
            alpha = jnp.exp(m - m_new)
            p = jnp.exp(s - m_new)
            l = alpha * l + jnp.sum(p, axis=-1, keepdims=True)
            acc = alpha * acc + jnp.dot(p.astype(BF16), vs[h, pl.ds(k0, tq), :],
                                        preferred_element_type=F32)
            return m_new, l, acc

        init = (jnp.full((tq, 1), -jnp.inf, F32), jnp.zeros((tq, 1), F32), jnp.zeros((tq, dv), F32))
        _, l, acc = lax.fori_loop(0, j + 1, kv_step, init)
        return acc / l

    for h in range(H):
        comp = [softmax_pv(q[:, (2 * h + c) * d:(2 * h + c + 1) * d].astype(BF16), 2 * h + c, h)
                for c in range(2)]
        o = comp[0] - lam * comp[1]
        o = o * lax.rsqrt(jnp.mean(o * o, axis=-1, keepdims=True) + RMS_EPS)
        o = o * g_ref[...] * (1.0 - lam_init)
        o_ref[0, :, h * dv:(h + 1) * dv] = o.astype(o_ref.dtype)


def _diff_attention(proj, cos, sin, lam_p, g, col_q, col_k, col_v, layer_idx):
    B, T, _ = proj.shape
    H = DIFF_HEADS
    w = cos.shape[2]
    d = w // (2 * H)
    dv = 2 * d
    tq = min(256, T)
    lam_init = 0.8 - 0.6 * math.exp(-0.3 * layer_idx)
    blk = lambda col: pl.BlockSpec((1, tq, w), lambda b, j: (b, j, col))
    full = lambda col: pl.BlockSpec((1, T, w), lambda b, j: (b, 0, col))
    return pl.pallas_call(
        functools.partial(_diff_kernel, tq=tq, lam_init=lam_init),
        out_shape=jax.ShapeDtypeStruct((B, T, H * dv), BF16),
        grid=(B, T // tq),
        in_specs=[blk(col_q), full(col_k), full(col_v), blk(0), blk(0), full(0), full(0),
                  pl.BlockSpec(lam_p.shape, lambda b, j: (0, 0)),
                  pl.BlockSpec(g.shape, lambda b, j: (0, 0))],
        out_specs=pl.BlockSpec((1, tq, H * dv), lambda b, j: (b, j, 0)),
        scratch_shapes=[pltpu.VMEM((2 * H, T, d), BF16),
                        pltpu.VMEM((H, T, dv), BF16)],
        compiler_params=_cparams(("arbitrary", "arbitrary")),
        name="diff_attention",
    )(proj, proj, proj, cos, sin, cos, sin, lam_p, g)


def _conv_kernel(a_ref, b_ref, w_ref, cb_ref, lg_ref, lb_ref, o_ref, u_ref, *, rb):
    T = a_ref.shape[1]
    pad = u_ref.shape[0] - T
    u_ref[0:pad, :] = jnp.zeros((pad, u_ref.shape[1]), F32)
    u_ref[pad:pad + T, :] = a_ref[0] * _sigmoid(b_ref[0])
    first = pad - (CONV_WIDTH - 1)

    def block(i, carry):
        r0 = pl.multiple_of(i * rb, rb)
        win = u_ref[pl.ds(r0, rb + pad), :]
        acc = jnp.zeros((rb, u_ref.shape[1]), F32)
        for j in range(CONV_WIDTH):
            acc = acc + w_ref[j:j + 1, :] * win[first + j:first + j + rb, :]
        y = acc + cb_ref[...]
        mu = jnp.mean(y, axis=-1, keepdims=True)
        yc = y - mu
        var = jnp.mean(yc * yc, axis=-1, keepdims=True)
        y = yc * lax.rsqrt(var + LN_EPS) * lg_ref[...] + lb_ref[...]
        o_ref[0, pl.ds(r0, rb), :] = _silu(y).astype(o_ref.dtype)
        return carry

    lax.fori_loop(0, T // rb, block, 0)


def _conformer_conv(proj, w, cb, lg, lb, col_a, col_b):
    B, T, _ = proj.shape
    ch = w.shape[1]
    rb = min(128, T)
    return pl.pallas_call(
        functools.partial(_conv_kernel, rb=rb),
        out_shape=jax.ShapeDtypeStruct((B, T, ch), BF16),
        grid=(B,),
        in_specs=[pl.BlockSpec((1, T, ch), lambda b: (b, 0, col_a)),
                  pl.BlockSpec((1, T, ch), lambda b: (b, 0, col_b)),
                  pl.BlockSpec(w.shape, lambda b: (0, 0)),
                  pl.BlockSpec(cb.shape, lambda b: (0, 0)),
                  pl.BlockSpec(lg.shape, lambda b: (0, 0)),
                  pl.BlockSpec(lb.shape, lambda b: (0, 0))],
        out_specs=pl.BlockSpec((1, T, ch), lambda b: (b, 0, 0)),
        scratch_shapes=[pltpu.VMEM((T + 32, ch), F32)],
        compiler_params=_cparams(("arbitrary",)),
        name="conformer_conv",
    )(proj, proj, w, cb, lg, lb)


def _rwkv_kernel(x_ref, mu_ref, w0_ref, w2_ref, a0_ref, a2_ref, g2_ref, kk_ref, ka_ref, rk_ref,
                 lng_ref, lnb_ref, o_ref, s_ref, prev_ref, *, n_chunk):
    C = RWKV_CHUNK
    H = RWKV_HEADS
    W = o_ref.shape[2]
    N = W // H

    @pl.when(pl.program_id(1) == 0)
    def _():
        s_ref[...] = jnp.zeros_like(s_ref)
        prev_ref[...] = jnp.zeros_like(prev_ref)

    lane = lax.broadcasted_iota(jnp.int32, (1, W), 1)
    head_mask = [(lane // N == h).astype(F32) for h in range(H)]
    block_diag = _group_matrix(W, N)
    head_sum = block_diag.astype(F32)
    ti = lax.broadcasted_iota(jnp.int32, (C, C), 0)
    tj = lax.broadcasted_iota(jnp.int32, (C, C), 1)
    tril_incl = ti >= tj
    tril_strict = ti > tj
    same_sub = (ti // RWKV_SUB) == (tj // RWKV_SUB)
    eye = (ti == tj).astype(F32)
    cum = tril_incl.astype(F32)
    first_row = lax.broadcasted_iota(jnp.int32, (C, 1), 0) == 0

    def per_head(mats, x):
        reps = x.shape[1] // W
        acc = None
        for h in range(H):
            m = head_mask[h] if reps == 1 else jnp.concatenate([head_mask[h]] * reps, axis=1)
            t = m * _mm(mats[h], x)
            acc = t if acc is None else acc + t
        return acc

    def chunk(ci, carry):
        r0 = pl.multiple_of(ci * C, C)
        rows = pl.ds(r0, C)
        x = x_ref[0, rows, :]
        prev = jnp.where(first_row, prev_ref[...], pltpu.roll(x, 1, 0))
        prev_ref[...] = x[C - 1:C, :]
        xm = x + (prev - x) * mu_ref[...]
        r = xm[:, 0:W]
        k = xm[:, W:2 * W]
        v = xm[:, 2 * W:3 * W]
        zz = xm[:, 3 * W:3 * W + LANES]
        zg = xm[:, 3 * W + LANES:]
        w = -_softplus(-(w0_ref[...] + _mm(jnp.tanh(zz), w2_ref[...]))) - 0.5
        lw = -jnp.exp(w)
        a = _sigmoid(a0_ref[...] + _mm(zz, a2_ref[...]))
        g = _mm(_sigmoid(zg), g2_ref[...])
        kk = k * kk_ref[...]
        kk = kk / jnp.maximum(jnp.sqrt(_mm_f32(kk * kk, head_sum)), 1e-12)
        k = k * (1.0 + (a - 1.0) * ka_ref[...])
        b = kk * a

        G = _mm_f32(cum, lw)
        Gp = G - lw
        g_end = G[C - 1:C, :]
        kap = kk * jnp.exp(Gp)
        rho = r * jnp.exp(G)
        inv = jnp.exp(-G)
        bet = b * inv
        kt = k * inv
        to_end = jnp.exp(g_end - G)
        bet_c = b * to_end
        k_c = k * to_end

        lhs = jnp.concatenate([kap * head_mask[h] for h in range(H)]
                              + [rho * head_mask[h] for h in range(H)], axis=0)
        pb = _mm_nt(lhs, bet)
        pk = _mm_nt(lhs, kt)
        A_b = jnp.where(tril_strict, pb[0:H * C].reshape(H, C, C), 0.0)
        A_k = jnp.where(tril_strict, pk[0:H * C].reshape(H, C, C), 0.0)
        B_b = jnp.where(tril_incl, pb[H * C:].reshape(H, C, C), 0.0)
        B_k = jnp.where(tril_incl, pk[H * C:].reshape(H, C, C), 0.0)

        Dg = jnp.where(same_sub, A_b, 0.0)
        Lo = A_b - Dg
        D2 = _bmm(Dg, Dg)
        D4 = _bmm(D2, D2)
        D8 = _bmm(D4, D4)
        Dinv = _bmm(_bmm(_bmm(eye - Dg, eye + D2), eye + D4), eye + D8)
        Nn = _bmm(Dinv, Lo)
        N2 = _bmm(Nn, Nn)
        Tm = _bmm(_bmm(eye - Nn, eye + N2), Dinv)

        akv = per_head(A_k, v)
        tk = per_head(Tm, jnp.concatenate([kap, akv], axis=1))
        kap_p = tk[:, 0:W]
        v_p = tk[:, W:2 * W]
        bb = per_head(B_b, jnp.concatenate([kap_p, v_p], axis=1))
        q_eff = rho - bb[:, 0:W]
        y_loc = per_head(B_k, v) - bb[:, W:2 * W]
        m_low = jnp.where(block_diag, _mm_tn(kap_p, bet_c), 0.0)
        n_c = jnp.where(block_diag, _mm_tn(v, k_c) - _mm_tn(v_p, bet_c), 0.0)

        s = s_ref[...]
        y = _mm_nt(q_eff, s) + y_loc
        s_ref[...] = s * jnp.exp(g_end) - _mm(s, m_low) + n_c

        mean = _mm_f32(y, head_sum) / N
        yc = y - mean
        var = _mm_f32(yc * yc, head_sum) / N
        yn = yc * lax.rsqrt(var + RWKV_GN_EPS) * lng_ref[...] + lnb_ref[...]
        bonus = _mm_f32(r * k * rk_ref[...], head_sum) * v
        o_ref[0, rows, :] = ((yn + bonus) * g).astype(o_ref.dtype)
        return carry

    lax.fori_loop(0, n_chunk, chunk, 0)


def _rwkv7(proj, vecs, w2p, a2p, g2, col, tb):
    B, T, _ = proj.shape
    W = g2.shape[1]
    cols = vecs["mu"].shape[1]
    names = ("mu", "w0", "w2", "a0", "a2", "g2", "kk", "ka", "rk", "lng", "lnb")
    params = dict(vecs, w2=w2p, a2=a2p, g2=g2)
    const = lambda arr: pl.BlockSpec(arr.shape, lambda b, j: (0, 0))
    return pl.pallas_call(
        functools.partial(_rwkv_kernel, n_chunk=tb // RWKV_CHUNK),
        out_shape=jax.ShapeDtypeStruct((B, T, W), BF16),
        grid=(B, T // tb),
        in_specs=[pl.BlockSpec((1, tb, cols), lambda b, j: (b, j, col))]
                 + [const(params[n]) for n in names],
        out_specs=pl.BlockSpec((1, tb, W), lambda b, j: (b, j, 0)),
        scratch_shapes=[pltpu.VMEM((W, W), F32), pltpu.VMEM((1, cols), F32)],
        compiler_params=_cparams(("arbitrary", "arbitrary")),
        name="rwkv7_mixer",
    )(proj, *[params[n] for n in names])


def _merge_kernel(x_ref, sc_ref, sh_ref, gt_ref, gpre_ref, gpost_ref, wg_ref, b0_ref, b1_ref, b2_ref,
                  b3_ref, wb_ref, wo_ref, o_ref):
    x = x_ref[0]
    D = x.shape[1]
    h = _rms_mod(x, gpre_ref[...], sc_ref[0], sh_ref[0]).astype(BF16)
    merged = None
    for g, br in enumerate((b0_ref, b1_ref, b2_ref, b3_ref)):
        gate = _sigmoid(jnp.dot(h, wg_ref[:, g * D:(g + 1) * D], preferred_element_type=F32))
        t = gate * jnp.dot(br[0], wb_ref[g], preferred_element_type=F32)
        merged = t if merged is None else merged + t
    y = _mm(merged, wo_ref[...])
    y = y * lax.rsqrt(jnp.mean(y * y, axis=-1, keepdims=True) + RMS_EPS) * gpost_ref[...]
    o_ref[0] = x + gt_ref[0] * y


def _merge(x, sc, sh, gt, gpre, gpost, w_gate, branches, w_branch, w_out, tm):
    B, T, D = x.shape
    bw = branches[0].shape[2]
    tok = lambda n: pl.BlockSpec((1, tm, n), lambda b, i: (b, i, 0))
    vec = pl.BlockSpec((1, 1, D), lambda b, i: (b, 0, 0))
    const2 = lambda arr: pl.BlockSpec(arr.shape, lambda b, i: (0, 0))
    return pl.pallas_call(
        _merge_kernel,
        out_shape=jax.ShapeDtypeStruct((B, T, D), F32),
        grid=(B, T // tm),
        in_specs=[tok(D), vec, vec, vec, const2(gpre), const2(gpost), const2(w_gate),
                  tok(bw), tok(bw), tok(bw), tok(bw),
                  pl.BlockSpec(w_branch.shape, lambda b, i: (0, 0, 0)), const2(w_out)],
        out_specs=tok(D),
        compiler_params=_cparams(("arbitrary", "arbitrary")),
        name="merge_out_proj",
    )(x, sc, sh, gt, gpre, gpost, w_gate, *branches, w_branch, w_out)


def _ffn_kernel(x_ref, sc_ref, sh_ref, gt_ref, gpre_ref, gpost_ref, wg_ref, wu_ref, wd_ref, o_ref, *, tf):
    x = x_ref[0]
    h = _rms_mod(x, gpre_ref[...], sc_ref[0], sh_ref[0]).astype(BF16)
    F = wg_ref.shape[1]
    acc = None
    for f0 in range(0, F, tf):
        f1 = min(f0 + tf, F)
        gate = jnp.dot(h, wg_ref[:, f0:f1], preferred_element_type=F32)
        up = jnp.dot(h, wu_ref[:, f0:f1], preferred_element_type=F32)
        t = jnp.dot((_silu(gate) * up).astype(BF16), wd_ref[f0:f1, :], preferred_element_type=F32)
        acc = t if acc is None else acc + t
    y = acc * lax.rsqrt(jnp.mean(acc * acc, axis=-1, keepdims=True) + RMS_EPS) * gpost_ref[...]
    o_ref[0] = x + gt_ref[0] * y


def _dense_ffn(x, sc, sh, gt, gpre, gpost, wg, wu, wd, tm):
    B, T, D = x.shape
    tok = pl.BlockSpec((1, tm, D), lambda b, i: (b, i, 0))
    vec = pl.BlockSpec((1, 1, D), lambda b, i: (b, 0, 0))
    const2 = lambda arr: pl.BlockSpec(arr.shape, lambda b, i: (0, 0))
    return pl.pallas_call(
        functools.partial(_ffn_kernel, tf=512),
        out_shape=jax.ShapeDtypeStruct((B, T, D), F32),
        grid=(B, T // tm),
        in_specs=[tok, vec, vec, vec, const2(gpre), const2(gpost), const2(wg), const2(wu), const2(wd)],
        out_specs=tok,
        compiler_params=_cparams(("arbitrary", "arbitrary")),
        name="dense_swiglu",
    )(x, sc, sh, gt, gpre, gpost, wg, wu, wd)


def _moe_kernel(x_ref, sc_ref, sh_ref, gt_ref, gpre_ref, gpost_ref, rw_ref, rb_ref, wg_ref, wu_ref,
                wd_ref, o_ref, h_ref, comb_ref, acc_ref):
    e = pl.program_id(2)
    f = pl.program_id(3)
    last = jnp.logical_and(e == pl.num_programs(2) - 1, f == pl.num_programs(3) - 1)

    @pl.when(jnp.logical_and(e == 0, f == 0))
    def _():
        h = _rms_mod(x_ref[0], gpre_ref[...], sc_ref[0], sh_ref[0])
        h_ref[...] = h.astype(BF16)
        logits = _mm_f32(h, rw_ref[...]) + rb_ref[...]
        lane = lax.broadcasted_iota(jnp.int32, logits.shape, 1)
        v1 = jnp.max(logits, axis=-1, keepdims=True)
        i1 = jnp.min(jnp.where(logits == v1, lane, LANES), axis=-1, keepdims=True)
        rest = jnp.where(lane == i1, -jnp.inf, logits)
        v2 = jnp.max(rest, axis=-1, keepdims=True)
        i2 = jnp.min(jnp.where(rest == v2, lane, LANES), axis=-1, keepdims=True)
        e2 = jnp.exp(v2 - v1)
        p1 = 1.0 / (1.0 + e2)
        p2 = e2 / (1.0 + e2)
        comb_ref[...] = jnp.where(lane == i1, p1, 0.0) + jnp.where(lane == i2, p2, 0.0)
        acc_ref[...] = jnp.zeros_like(acc_ref)

    lane = lax.broadcasted_iota(jnp.int32, comb_ref.shape, 1)
    w_e = jnp.sum(jnp.where(lane == e, comb_ref[...], 0.0), axis=-1, keepdims=True)
    h = h_ref[...]
    gate = jnp.dot(h, wg_ref[0], preferred_element_type=F32)
    up = jnp.dot(h, wu_ref[0], preferred_element_type=F32)
    act = (_silu(gate) * up * w_e).astype(BF16)
    acc_ref[...] += jnp.dot(act, wd_ref[0], preferred_element_type=F32)

    @pl.when(last)
    def _():
        acc = acc_ref[...]
        y = acc * lax.rsqrt(jnp.mean(acc * acc, axis=-1, keepdims=True) + RMS_EPS) * gpost_ref[...]
        o_ref[0] = x_ref[0] + gt_ref[0] * y


def _moe_ffn(x, sc, sh, gt, gpre, gpost, rw, rb, wg, wu, wd, tm, tf):
    B, T, D = x.shape
    E, _, F = wg.shape
    tok = pl.BlockSpec((1, tm, D), lambda b, i, e, f: (b, i, 0))
    vec = pl.BlockSpec((1, 1, D), lambda b, i, e, f: (b, 0, 0))
    const2 = lambda arr: pl.BlockSpec(arr.shape, lambda b, i, e, f: (0, 0))
    return pl.pallas_call(
        _moe_kernel,
        out_shape=jax.ShapeDtypeStruct((B, T, D), F32),
        grid=(B, T // tm, E, F // tf),
        in_specs=[tok, vec, vec, vec, const2(gpre), const2(gpost), const2(rw), const2(rb),
                  pl.BlockSpec((1, D, tf), lambda b, i, e, f: (e, 0, f)),
                  pl.BlockSpec((1, D, tf), lambda b, i, e, f: (e, 0, f)),
                  pl.BlockSpec((1, tf, D), lambda b, i, e, f: (e, f, 0))],
        out_specs=tok,
        scratch_shapes=[pltpu.VMEM((tm, D), BF16), pltpu.VMEM((tm, LANES), F32),
                        pltpu.VMEM((tm, D), F32)],
        compiler_params=_cparams(("arbitrary",) * 4),
        name="moe_swiglu",
    )(x, sc, sh, gt, gpre, gpost, rw, rb, wg, wu, wd)


def _rope_tables(positions, groups):
    d = 32
    inv = 1.0 / (ROPE_THETA ** (jnp.arange(0, d, 2, dtype=F32) / d))
    ang = positions.astype(F32)[..., None] * inv
    cos, sin = jnp.cos(ang), jnp.sin(ang)
    cos = jnp.tile(jnp.concatenate([cos, cos], axis=-1), (1, 1, groups))
    sin = jnp.tile(jnp.concatenate([-sin, sin], axis=-1), (1, 1, groups))
    return cos, sin


def _pad_rows(w, rows, offset):
    out = jnp.zeros((rows, w.shape[1]), w.dtype)
    return out.at[offset:offset + w.shape[0]].set(w)


def kernel(x, c, positions, ada_w, ada_b, norm_mix_pre, norm_mix_post, norm_ffn_pre, norm_ffn_post, w_in, gla_gate_w2, gla_gate_b, gla_norm, diff_lambda, diff_subln, conv_w, conv_b, conv_ln_g, conv_ln_b, rwkv_mu, rwkv_w0, rwkv_w2, rwkv_a0, rwkv_a2, rwkv_g2, rwkv_k_k, rwkv_k_a, rwkv_r_k, rwkv_ln_g, rwkv_ln_b, w_branch, w_out, ffn_w_gate, ffn_w_up, ffn_w_down, router_w, router_b, moe_w_gate, moe_w_up, moe_w_down):
    B, T, D = x.shape
    L = ada_w.shape[0]
    W = D // N_BRANCH
    hk = gla_gate_b.shape[1]
    decay_rank = rwkv_w2.shape[1]
    a_rank = rwkv_a2.shape[1]
    gate_rank = rwkv_g2.shape[1]
    assert decay_rank + a_rank == LANES and 2 * hk == W and gate_rank == LANES
    n_mix = 3 * W + 3 * W + 2 * W + (3 * W + decay_rank + a_rank + gate_rank)
    sizes = (hk, hk, W, W, GLA_GATE_RANK, W, W, W, W, W, 3 * W + LANES + gate_rank, N_BRANCH * D)
    offs = [0]
    for s in sizes:
        offs.append(offs[-1] + s)
    assert offs[-1] == w_in.shape[2]
    tm = min(512, T)
    tb = min(512, T)

    mod = _modulation(c, ada_w, ada_b)
    cos, sin = _rope_tables(positions, W // 32)

    for l in range(L):
        m = mod[l].reshape(B, 1, 6 * D)
        sh_m, sc_m, gt_m, sh_f, sc_f, gt_f = [m[:, :, i * D:(i + 1) * D] for i in range(6)]

        wl = w_in[l]
        gz_cols = jnp.zeros((D, LANES), F32).at[:, :GLA_GATE_RANK].set(wl[:, offs[4]:offs[5]])
        w_mix = jnp.concatenate([wl[:, offs[0]:offs[4]], wl[:, offs[5]:offs[11]], gz_cols], axis=1)
        w_mix = w_mix.astype(BF16)
        w_gate = wl[:, offs[11]:offs[12]].astype(BF16)
        proj = _in_projection(x, sc_m, sh_m, norm_mix_pre[l][None], w_mix, tm)

        w2p = _pad_rows(gla_gate_w2[l], LANES, 0)
        o_gla = _gla(proj, w2p, gla_gate_b[l][None], jnp.tile(gla_norm[l], GLA_HEADS)[None],
                     col_qk=0, col_v=1, col_og=2, col_gz=(n_mix // LANES), tb=tb)
        o_diff = _diff_attention(proj, cos, sin, diff_lambda[l],
                                 diff_subln[l][None], col_q=3, col_k=4, col_v=5, layer_idx=l)
        o_conv = _conformer_conv(proj, _pad_rows(conv_w[l], 32, 0), conv_b[l][None],
                                 conv_ln_g[l][None], conv_ln_b[l][None], col_a=6, col_b=7)
        vecs = dict(mu=rwkv_mu[l][None], w0=rwkv_w0[l][None], a0=rwkv_a0[l][None],
                    kk=rwkv_k_k[l][None], ka=rwkv_k_a[l][None], rk=rwkv_r_k[l].reshape(1, W),
                    lng=rwkv_ln_g[l][None], lnb=rwkv_ln_b[l][None])
        o_rwkv = _rwkv7(proj, vecs, _pad_rows(rwkv_w2[l], LANES, 0).astype(BF16),
                        _pad_rows(rwkv_a2[l], LANES, decay_rank).astype(BF16),
                        rwkv_g2[l].astype(BF16), col=2, tb=tb)
        x = _merge(x, sc_m, sh_m, gt_m, norm_mix_pre[l][None], norm_mix_post[l][None], w_gate,
                   (o_gla, o_diff, o_conv, o_rwkv), w_branch[l].astype(BF16), w_out[l].astype(BF16), tm)

        i = l // 2
        if l % 2 == 0:
            x = _dense_ffn(x, sc_f, sh_f, gt_f, norm_ffn_pre[l][None], norm_ffn_post[l][None],
                           ffn_w_gate[i].astype(BF16), ffn_w_up[i].astype(BF16),
                           ffn_w_down[i].astype(BF16), tm)
        else:
            rw = jnp.zeros((D, LANES), F32).at[:, :N_EXPERTS].set(router_w[i])
            rb = jnp.full((1, LANES), -jnp.inf, F32).at[0, :N_EXPERTS].set(router_b[i])
            x = _moe_ffn(x, sc_f, sh_f, gt_f, norm_ffn_pre[l][None], norm_ffn_post[l][None], rw, rb,
                         moe_w_gate[i].astype(BF16), moe_w_up[i].astype(BF16),
                         moe_w_down[i].astype(BF16), tm=min(1024, T), tf=512)
    return x
```

```python
import functools
import math

import jax
import jax.numpy as jnp
from jax import lax
from jax.experimental import pallas as pl
from jax.experimental.pallas import tpu as pltpu

F32 = jnp.float32
BF16 = jnp.bfloat16
HIGHEST = lax.Precision.HIGHEST

N_BRANCH = 4
GLA_HEADS = 4
GLA_GATE_RANK = 16
GLA_GATE_NORMALIZER = 16.0
GLA_CHUNK = 32
DIFF_HEADS = 4
ROPE_THETA = 10000.0
CONV_WIDTH = 31
RWKV_HEADS = 4
RWKV_CHUNK = 64
RWKV_SUB = 16
N_EXPERTS = 8
RMS_EPS = 1e-6
LN_EPS = 1e-5
RWKV_GN_EPS = 64e-5
LANES = 128
VMEM_LIMIT = 56 * 1024 * 1024


def _cparams(sem):
    return pltpu.CompilerParams(dimension_semantics=sem, vmem_limit_bytes=VMEM_LIMIT)


def _mm(a, b):
    return jnp.dot(a.astype(BF16), b.astype(BF16), preferred_element_type=F32)


def _mm_nt(a, b):
    return lax.dot_general(a.astype(BF16), b.astype(BF16), (((1,), (1,)), ((), ())),
                           preferred_element_type=F32)


def _mm_tn(a, b):
    return lax.dot_general(a.astype(BF16), b.astype(BF16), (((0,), (0,)), ((), ())),
                           preferred_element_type=F32)


def _bmm(a, b):
    return lax.dot_general(a.astype(BF16), b.astype(BF16), (((2,), (1,)), ((0,), (0,))),
                           preferred_element_type=F32)


def _mm_f32(a, b):
    return jnp.dot(a, b, precision=HIGHEST, preferred_element_type=F32)


def _hi_lo(x):
    hi = x.astype(BF16)
    return jnp.concatenate([hi, (x - hi.astype(F32)).astype(BF16)], axis=1)


def _sigmoid(x):
    return 1.0 / (1.0 + jnp.exp(-x))


def _silu(x):
    return x * _sigmoid(x)


def _softplus(x):
    return jnp.maximum(x, 0.0) + jnp.log(1.0 + jnp.exp(-jnp.abs(x)))


def _group_matrix(n, group):
    r = lax.broadcasted_iota(jnp.int32, (n, n), 0) // group
    c = lax.broadcasted_iota(jnp.int32, (n, n), 1) // group
    return r == c


def _rms_mod(x, gain, scale, shift):
    y = x * lax.rsqrt(jnp.mean(x * x, axis=-1, keepdims=True) + RMS_EPS)
    return y * gain * (1.0 + scale) + shift


def _mod_kernel(c_ref, w_ref, b_ref, o_ref):
    o_ref[0] = _mm_f32(_silu(c_ref[...]), w_ref[0]) + b_ref[0]


def _modulation(c, ada_w, ada_b):
    L, D, M = ada_w.shape
    B = c.shape[0]
    tn = M // 4
    return pl.pallas_call(
        _mod_kernel,
        out_shape=jax.ShapeDtypeStruct((L, B, M), F32),
        grid=(L, M // tn),
        in_specs=[pl.BlockSpec((B, D), lambda l, j: (0, 0)),
                  pl.BlockSpec((1, D, tn), lambda l, j: (l, 0, j)),
                  pl.BlockSpec((1, 1, tn), lambda l, j: (l, 0, j))],
        out_specs=pl.BlockSpec((1, B, tn), lambda l, j: (l, 0, j)),
        compiler_params=_cparams(("arbitrary", "arbitrary")),
        name="adaln_mod",
    )(c, ada_w, ada_b.reshape(L, 1, M))


def _inproj_kernel(x_ref, sc_ref, sh_ref, g_ref, w_ref, o_ref):
    h = _rms_mod(x_ref[0], g_ref[...], sc_ref[0], sh_ref[0])
    o_ref[0] = _mm(h, w_ref[...])


def _in_projection(x, sc, sh, gain, w, tm):
    B, T, D = x.shape
    n = w.shape[1]
    return pl.pallas_call(
        _inproj_kernel,
        out_shape=jax.ShapeDtypeStruct((B, T, n), F32),
        grid=(B, T // tm),
        in_specs=[pl.BlockSpec((1, tm, D), lambda b, i: (b, i, 0)),
                  pl.BlockSpec((1, 1, D), lambda b, i: (b, 0, 0)),
                  pl.BlockSpec((1, 1, D), lambda b, i: (b, 0, 0)),
                  pl.BlockSpec((1, D), lambda b, i: (0, 0)),
                  pl.BlockSpec((D, n), lambda b, i: (0, 0))],
        out_specs=pl.BlockSpec((1, tm, n), lambda b, i: (b, i, 0)),
        compiler_params=_cparams(("arbitrary", "arbitrary")),
        name="in_proj",
    )(x, sc, sh, gain, w)


def _gla_kernel(qk_ref, v_ref, og_ref, gz_ref, w2_ref, gb_ref, ng_ref, o_ref, s_ref, g_ref, r_ref,
                qd_ref, kd_ref, oacc_ref, *, n_chunk):
    C = GLA_CHUNK
    hk = qk_ref.shape[2] // 2
    hv = v_ref.shape[2]
    dk = hk // GLA_HEADS
    dv = hv // GLA_HEADS

    tb = n_chunk * C

    @pl.when(pl.program_id(1) == 0)
    def _():
        s_ref[...] = jnp.zeros_like(s_ref)

    z = _mm(gz_ref[0], w2_ref[...]) + gb_ref[...]
    gk = (jnp.minimum(z, 0.0) - jnp.log(1.0 + jnp.exp(-jnp.abs(z)))) / GLA_GATE_NORMALIZER
    bi = lax.broadcasted_iota(jnp.int32, (tb, tb), 0)
    bj = lax.broadcasted_iota(jnp.int32, (tb, tb), 1)
    same_chunk = (bi // C) == (bj // C)
    prefix = jnp.logical_and(same_chunk, bj <= bi)
    suffix = jnp.logical_and(same_chunk, bj > bi)
    sums = jnp.dot(jnp.concatenate([prefix, suffix], axis=0).astype(BF16), _hi_lo(gk),
                   preferred_element_type=F32)
    G_all = sums[0:tb, 0:hk] + sums[0:tb, hk:2 * hk]
    R_all = sums[tb:2 * tb, 0:hk] + sums[tb:2 * tb, hk:2 * hk]
    g_ref[...] = G_all
    r_ref[...] = R_all
    qd_ref[...] = (qk_ref[0, :, 0:hk] * (dk ** -0.5) * jnp.exp(G_all)).astype(BF16)
    kd_ref[...] = (qk_ref[0, :, hk:2 * hk] * jnp.exp(R_all)).astype(BF16)

    pr = lax.broadcasted_iota(jnp.int32, (C * C, C), 0)
    pc = lax.broadcasted_iota(jnp.int32, (C * C, C), 1)
    row_sel = (pr // C == pc).astype(BF16)
    pi = lax.broadcasted_iota(jnp.int32, (C * C, 1), 0)
    causal = (pi % C) <= (pi // C)
    er = lax.broadcasted_iota(jnp.int32, (hk, hv), 0) // dk
    ec = lax.broadcasted_iota(jnp.int32, (hk, hv), 1) // dv
    expand = (er == ec).astype(BF16)
    sr = lax.broadcasted_iota(jnp.int32, (hv, hk), 0) // dv
    scol = lax.broadcasted_iota(jnp.int32, (hv, hk), 1) // dk
    state_mask = sr == scol
    head_mean = (_group_matrix(hv, dv).astype(F32) / dv).astype(BF16)

    def chunk(ci, carry):
        r0 = pl.multiple_of(ci * C, C)
        rows = pl.ds(r0, C)
        q = qk_ref[0, rows, 0:hk] * (dk ** -0.5)
        k = qk_ref[0, rows, hk:2 * hk]
        v = v_ref[0, rows, :]
        G = g_ref[rows, :]
        sel = jnp.dot(row_sel, jnp.concatenate([_hi_lo(G), q.astype(BF16)], axis=1),
                      preferred_element_type=F32)
        G_i = sel[:, 0:hk] + sel[:, hk:2 * hk]
        q_i = sel[:, 2 * hk:3 * hk]
        G_j = jnp.broadcast_to(G[None], (C, C, hk)).reshape(C * C, hk)
        k_j = jnp.broadcast_to(k[None], (C, C, hk)).reshape(C * C, hk)
        decay = jnp.exp(jnp.where(causal, G_i - G_j, -jnp.inf))
        p = q_i * k_j * decay
        a_exp = jnp.dot(p.astype(BF16), expand, preferred_element_type=F32)
        v_j = jnp.broadcast_to(v[None], (C, C, hv))
        o_intra = jnp.sum(a_exp.reshape(C, C, hv) * v_j, axis=1)
        g_total = G[0:1, :] + r_ref[pl.ds(r0, 8), :][0:1, :]
        s = s_ref[...]
        o_inter = lax.dot_general(qd_ref[rows, :], s.astype(BF16), (((1,), (1,)), ((), ())),
                                  preferred_element_type=F32)
        kv = lax.dot_general(v.astype(BF16), kd_ref[rows, :], (((0,), (0,)), ((), ())),
                             preferred_element_type=F32)
        s_ref[...] = s * jnp.exp(g_total) + jnp.where(state_mask, kv, 0.0)
        oacc_ref[rows, :] = o_intra + o_inter
        return carry

    lax.fori_loop(0, n_chunk, chunk, 0)

    o = oacc_ref[...]
    ms = jnp.dot(_hi_lo(o * o), jnp.concatenate([head_mean, head_mean], axis=0),
                 preferred_element_type=F32)
    o = o * lax.rsqrt(ms + RMS_EPS) * ng_ref[...] * _silu(og_ref[0])
    o_ref[0] = o.astype(o_ref.dtype)


def _gla(proj, w2p, gb, ng, col_qk, col_v, col_og, col_gz, tb):
    B, T, _ = proj.shape
    hv = ng.shape[1]
    hk = gb.shape[1]
    return pl.pallas_call(
        functools.partial(_gla_kernel, n_chunk=tb // GLA_CHUNK),
        out_shape=jax.ShapeDtypeStruct((B, T, hv), BF16),
        grid=(B, T // tb),
        in_specs=[pl.BlockSpec((1, tb, 2 * hk), lambda b, j: (b, j, col_qk)),
                  pl.BlockSpec((1, tb, hv), lambda b, j: (b, j, col_v)),
                  pl.BlockSpec((1, tb, hv), lambda b, j: (b, j, col_og)),
                  pl.BlockSpec((1, tb, LANES), lambda b, j: (b, j, col_gz)),
                  pl.BlockSpec(w2p.shape, lambda b, j: (0, 0)),
                  pl.BlockSpec(gb.shape, lambda b, j: (0, 0)),
                  pl.BlockSpec(ng.shape, lambda b, j: (0, 0))],
        out_specs=pl.BlockSpec((1, tb, hv), lambda b, j: (b, j, 0)),
        scratch_shapes=[pltpu.VMEM((hv, hk), F32), pltpu.VMEM((tb, hk), F32), pltpu.VMEM((tb, hk), F32),
                        pltpu.VMEM((tb, hk), BF16), pltpu.VMEM((tb, hk), BF16), pltpu.VMEM((tb, hv), F32)],
        compiler_params=_cparams(("arbitrary", "arbitrary")),
        name="gla_mixer",
    )(proj, proj, proj, proj, w2p, gb, ng)


def _rope(t, cos, sin_signed):
    d = 32
    half = d // 2
    out = []
    for s in range(t.shape[1] // LANES):
        x = t[:, s * LANES:(s + 1) * LANES]
        lane = lax.broadcasted_iota(jnp.int32, x.shape, 1)
        up = pltpu.roll(x, LANES - half, 1)
        down = pltpu.roll(x, half, 1)
        rot = jnp.where((lane % d) < half, up, down)
        out.append(x * cos[:, s * LANES:(s + 1) * LANES] + rot * sin_signed[:, s * LANES:(s + 1) * LANES])
    return jnp.concatenate(out, axis=1)


def _diff_kernel(q_ref, k_ref, v_ref, cosq_ref, sinq_ref, cosk_ref, sink_ref, lam_ref, g_ref, o_ref,
                 ks, vs, *, tq, lam_init):
    H = DIFF_HEADS
    d = q_ref.shape[2] // (2 * H)
    dv = v_ref.shape[2] // H
    j = pl.program_id(1)

    @pl.when(j == 0)
    def _():
        k = _rope(k_ref[0], cosk_ref[0], sink_ref[0])
        v = v_ref[0]
        for hc in range(2 * H):
            ks[hc] = k[:, hc * d:(hc + 1) * d].astype(BF16)
        ones_col = (lax.broadcasted_iota(jnp.int32, (v.shape[0], dv), 1) == 0).astype(BF16)
        for h in range(H):
            vs[h] = jnp.concatenate([v[:, h * dv:(h + 1) * dv].astype(BF16), ones_col], axis=1)

    q = _rope(q_ref[0], cosq_ref[0], sinq_ref[0]) * (d ** -0.5)
    lp = lam_ref[...]
    lam = (jnp.exp(jnp.sum(lp[0:1] * lp[1:2], axis=-1, keepdims=True))
           - jnp.exp(jnp.sum(lp[2:3] * lp[3:4], axis=-1, keepdims=True)) + lam_init)
    on_or_below_diag = (lax.broadcasted_iota(jnp.int32, (tq, tq), 1)
                        <= lax.broadcasted_iota(jnp.int32, (tq, tq), 0))

    def update(qh, kh, vh, m, acc, masked):
        s = lax.dot_general(qh, kh, (((1,), (1,)), ((), ())), preferred_element_type=F32)
        if masked:
            s = jnp.where(on_or_below_diag, s, -jnp.inf)
        m_new = jnp.maximum(m, jnp.max(s, axis=-1, keepdims=True))
        p = jnp.exp(s - m_new).astype(BF16)
        acc = jnp.exp(m - m_new) * acc + jnp.dot(p, vh, preferred_element_type=F32)
        return m_new, acc

    def two_softmax_pv(h):
        q1 = q[:, (2 * h) * d:(2 * h + 1) * d].astype(BF16)
        q2 = q[:, (2 * h + 1) * d:(2 * h + 2) * d].astype(BF16)

        def kv_step(kb, carry):
            m1, a1, m2, a2 = carry
            rows = pl.ds(pl.multiple_of(kb * tq, tq), tq)
            vh = vs[h, rows, :]
            m1, a1 = update(q1, ks[2 * h, rows, :], vh, m1, a1, False)
            m2, a2 = update(q2, ks[2 * h + 1, rows, :], vh, m2, a2, False)
            return m1, a1, m2, a2

        m0 = jnp.full((tq, 1), -jnp.inf, F32)
        a0 = jnp.zeros((tq, 2 * dv), F32)
        m1, a1, m2, a2 = lax.fori_loop(0, j, kv_step, (m0, a0, m0, a0))
        rows = pl.ds(pl.multiple_of(j * tq, tq), tq)
        vh = vs[h, rows, :]
        _, a1 = update(q1, ks[2 * h, rows, :], vh, m1, a1, True)
        _, a2 = update(q2, ks[2 * h + 1, rows, :], vh, m2, a2, True)
        return a1, a2

    for h in range(H):
        a1, a2 = two_softmax_pv(h)
        comp = [a[:, 0:dv] / a[:, dv:dv + 1] for a in (a1, a2)]
        o = comp[0] - lam * comp[1]
        o = o * lax.rsqrt(jnp.mean(o * o, axis=-1, keepdims=True) + RMS_EPS)
        o = o * g_ref[...] * (1.0 - lam_init)
        o_ref[0, :, h * dv:(h + 1) * dv] = o.astype(o_ref.dtype)


def _diff_attention(proj, cos, sin, lam_p, g, col_q, col_k, col_v, layer_idx):
    B, T, _ = proj.shape
    H = DIFF_HEADS
    w = cos.shape[2]
    d = w // (2 * H)
    dv = 2 * d
    tq = min(512, T)
    lam_init = 0.8 - 0.6 * math.exp(-0.3 * layer_idx)
    blk = lambda col: pl.BlockSpec((1, tq, w), lambda b, j: (b, j, col))
    full = lambda col: pl.BlockSpec((1, T, w), lambda b, j: (b, 0, col))
    return pl.pallas_call(
        functools.partial(_diff_kernel, tq=tq, lam_init=lam_init),
        out_shape=jax.ShapeDtypeStruct((B, T, H * dv), BF16),
        grid=(B, T // tq),
        in_specs=[blk(col_q), full(col_k), full(col_v), blk(0), blk(0), full(0), full(0),
                  pl.BlockSpec(lam_p.shape, lambda b, j: (0, 0)),
                  pl.BlockSpec(g.shape, lambda b, j: (0, 0))],
        out_specs=pl.BlockSpec((1, tq, H * dv), lambda b, j: (b, j, 0)),
        scratch_shapes=[pltpu.VMEM((2 * H, T, d), BF16),
                        pltpu.VMEM((H, T, 2 * dv), BF16)],
        compiler_params=_cparams(("arbitrary", "arbitrary")),
        name="diff_attention",
    )(proj, proj, proj, cos, sin, cos, sin, lam_p, g)


def _conv_kernel(a_ref, b_ref, w_ref, cb_ref, lg_ref, lb_ref, o_ref, u_ref, *, rb):
    T = a_ref.shape[1]
    pad = u_ref.shape[0] - T
    u_ref[0:pad, :] = jnp.zeros((pad, u_ref.shape[1]), F32)
    u_ref[pad:pad + T, :] = a_ref[0] * _sigmoid(b_ref[0])
    first = pad - (CONV_WIDTH - 1)

    def block(i, carry):
        r0 = pl.multiple_of(i * rb, rb)
        win = u_ref[pl.ds(r0, rb + pad), :]
        acc = jnp.zeros((rb, u_ref.shape[1]), F32)
        for j in range(CONV_WIDTH):
            acc = acc + w_ref[j:j + 1, :] * win[first + j:first + j + rb, :]
        y = acc + cb_ref[...]
        mu = jnp.mean(y, axis=-1, keepdims=True)
        yc = y - mu
        var = jnp.mean(yc * yc, axis=-1, keepdims=True)
        y = yc * lax.rsqrt(var + LN_EPS) * lg_ref[...] + lb_ref[...]
        o_ref[0, pl.ds(r0, rb), :] = _silu(y).astype(o_ref.dtype)
        return carry

    lax.fori_loop(0, T // rb, block, 0)


def _conformer_conv(proj, w, cb, lg, lb, col_a, col_b):
    B, T, _ = proj.shape
    ch = w.shape[1]
    rb = min(128, T)
    return pl.pallas_call(
        functools.partial(_conv_kernel, rb=rb),
        out_shape=jax.ShapeDtypeStruct((B, T, ch), BF16),
        grid=(B,),
        in_specs=[pl.BlockSpec((1, T, ch), lambda b: (b, 0, col_a)),
                  pl.BlockSpec((1, T, ch), lambda b: (b, 0, col_b)),
                  pl.BlockSpec(w.shape, lambda b: (0, 0)),
                  pl.BlockSpec(cb.shape, lambda b: (0, 0)),
                  pl.BlockSpec(lg.shape, lambda b: (0, 0)),
                  pl.BlockSpec(lb.shape, lambda b: (0, 0))],
        out_specs=pl.BlockSpec((1, T, ch), lambda b: (b, 0, 0)),
        scratch_shapes=[pltpu.VMEM((T + 32, ch), F32)],
        compiler_params=_cparams(("arbitrary",)),
        name="conformer_conv",
    )(proj, proj, w, cb, lg, lb)


def _rwkv_kernel(x_ref, mu_ref, w0_ref, w2_ref, a0_ref, a2_ref, g2_ref, kk_ref, ka_ref, rk_ref,
                 lng_ref, lnb_ref, o_ref, s_ref, prev_ref, *, n_chunk):
    C = RWKV_CHUNK
    H = RWKV_HEADS
    W = o_ref.shape[2]
    N = W // H

    @pl.when(pl.program_id(1) == 0)
    def _():
        s_ref[...] = jnp.zeros_like(s_ref)
        prev_ref[...] = jnp.zeros_like(prev_ref)

    lane = lax.broadcasted_iota(jnp.int32, (1, W), 1)
    head_mask = [(lane // N == h).astype(F32) for h in range(H)]
    block_diag = _group_matrix(W, N)
    head_sum = block_diag.astype(F32)
    ti = lax.broadcasted_iota(jnp.int32, (C, C), 0)
    tj = lax.broadcasted_iota(jnp.int32, (C, C), 1)
    tril_incl = ti >= tj
    tril_strict = ti > tj
    same_sub = (ti // RWKV_SUB) == (tj // RWKV_SUB)
    eye = (ti == tj).astype(F32)
    cum = tril_incl.astype(F32)
    first_row = lax.broadcasted_iota(jnp.int32, (C, 1), 0) == 0

    def per_head(mats, x):
        reps = x.shape[1] // W
        acc = None
        for h in range(H):
            m = head_mask[h] if reps == 1 else jnp.concatenate([head_mask[h]] * reps, axis=1)
            t = m * _mm(mats[h], x)
            acc = t if acc is None else acc + t
        return acc

    def chunk(ci, carry):
        r0 = pl.multiple_of(ci * C, C)
        rows = pl.ds(r0, C)
        x = x_ref[0, rows, :]
        prev = jnp.where(first_row, prev_ref[...], pltpu.roll(x, 1, 0))
        prev_ref[...] = x[C - 1:C, :]
        xm = x + (prev - x) * mu_ref[...]
        r = xm[:, 0:W]
        k = xm[:, W:2 * W]
        v = xm[:, 2 * W:3 * W]
        zz = xm[:, 3 * W:3 * W + LANES]
        zg = xm[:, 3 * W + LANES:]
        w = -_softplus(-(w0_ref[...] + _mm(jnp.tanh(zz), w2_ref[...]))) - 0.5
        lw = -jnp.exp(w)
        a = _sigmoid(a0_ref[...] + _mm(zz, a2_ref[...]))
        g = _mm(_sigmoid(zg), g2_ref[...])
        kk = k * kk_ref[...]
        kk = kk / jnp.maximum(jnp.sqrt(_mm_f32(kk * kk, head_sum)), 1e-12)
        k = k * (1.0 + (a - 1.0) * ka_ref[...])
        b = kk * a

        G = _mm_f32(cum, lw)
        Gp = G - lw
        g_end = G[C - 1:C, :]
        kap = kk * jnp.exp(Gp)
        rho = r * jnp.exp(G)
        inv = jnp.exp(-G)
        bet = b * inv
        kt = k * inv
        to_end = jnp.exp(g_end - G)
        bet_c = b * to_end
        k_c = k * to_end

        lhs = jnp.concatenate([kap * head_mask[h] for h in range(H)]
                              + [rho * head_mask[h] for h in range(H)], axis=0)
        pb = _mm_nt(lhs, bet)
        pk = _mm_nt(lhs, kt)
        A_b = jnp.where(tril_strict, pb[0:H * C].reshape(H, C, C), 0.0)
        A_k = jnp.where(tril_strict, pk[0:H * C].reshape(H, C, C), 0.0)
        B_b = jnp.where(tril_incl, pb[H * C:].reshape(H, C, C), 0.0)
        B_k = jnp.where(tril_incl, pk[H * C:].reshape(H, C, C), 0.0)

        Dg = jnp.where(same_sub, A_b, 0.0)
        Lo = A_b - Dg
        D2 = _bmm(Dg, Dg)
        D4 = _bmm(D2, D2)
        D8 = _bmm(D4, D4)
        Dinv = _bmm(_bmm(_bmm(eye - Dg, eye + D2), eye + D4), eye + D8)
        Nn = _bmm(Dinv, Lo)
        N2 = _bmm(Nn, Nn)
        Tm = _bmm(_bmm(eye - Nn, eye + N2), Dinv)

        akv = per_head(A_k, v)
        tk = per_head(Tm, jnp.concatenate([kap, akv], axis=1))
        kap_p = tk[:, 0:W]
        v_p = tk[:, W:2 * W]
        bb = per_head(B_b, jnp.concatenate([kap_p, v_p], axis=1))
        q_eff = rho - bb[:, 0:W]
        y_loc = per_head(B_k, v) - bb[:, W:2 * W]
        m_low = jnp.where(block_diag, _mm_tn(kap_p, bet_c), 0.0)
        n_c = jnp.where(block_diag, _mm_tn(v, k_c) - _mm_tn(v_p, bet_c), 0.0)

        s = s_ref[...]
        y = _mm_nt(q_eff, s) + y_loc
        s_ref[...] = s * jnp.exp(g_end) - _mm(s, m_low) + n_c

        mean = _mm_f32(y, head_sum) / N
        yc = y - mean
        var = _mm_f32(yc * yc, head_sum) / N
        yn = yc * lax.rsqrt(var + RWKV_GN_EPS) * lng_ref[...] + lnb_ref[...]
        bonus = _mm_f32(r * k * rk_ref[...], head_sum) * v
        o_ref[0, rows, :] = ((yn + bonus) * g).astype(o_ref.dtype)
        return carry

    lax.fori_loop(0, n_chunk, chunk, 0)


def _rwkv7(proj, vecs, w2p, a2p, g2, col, tb):
    B, T, _ = proj.shape
    W = g2.shape[1]
    cols = vecs["mu"].shape[1]
    names = ("mu", "w0", "w2", "a0", "a2", "g2", "kk", "ka", "rk", "lng", "lnb")
    params = dict(vecs, w2=w2p, a2=a2p, g2=g2)
    const = lambda arr: pl.BlockSpec(arr.shape, lambda b, j: (0, 0))
    return pl.pallas_call(
        functools.partial(_rwkv_kernel, n_chunk=tb // RWKV_CHUNK),
        out_shape=jax.ShapeDtypeStruct((B, T, W), BF16),
        grid=(B, T // tb),
        in_specs=[pl.BlockSpec((1, tb, cols), lambda b, j: (b, j, col))]
                 + [const(params[n]) for n in names],
        out_specs=pl.BlockSpec((1, tb, W), lambda b, j: (b, j, 0)),
        scratch_shapes=[pltpu.VMEM((W, W), F32), pltpu.VMEM((1, cols), F32)],
        compiler_params=_cparams(("arbitrary", "arbitrary")),
        name="rwkv7_mixer",
    )(proj, *[params[n] for n in names])


def _merge_kernel(x_ref, sc_ref, sh_ref, gt_ref, gpre_ref, gpost_ref, wg_ref, b0_ref, b1_ref, b2_ref,
                  b3_ref, wb_ref, wo_ref, o_ref):
    x = x_ref[0]
    D = x.shape[1]
    h = _rms_mod(x, gpre_ref[...], sc_ref[0], sh_ref[0]).astype(BF16)
    merged = None
    for g, br in enumerate((b0_ref, b1_ref, b2_ref, b3_ref)):
        gate = _sigmoid(jnp.dot(h, wg_ref[:, g * D:(g + 1) * D], preferred_element_type=F32))
        t = gate * jnp.dot(br[0], wb_ref[g], preferred_element_type=F32)
        merged = t if merged is None else merged + t
    y = _mm(merged, wo_ref[...])
    y = y * lax.rsqrt(jnp.mean(y * y, axis=-1, keepdims=True) + RMS_EPS) * gpost_ref[...]
    o_ref[0] = x + gt_ref[0] * y


def _merge(x, sc, sh, gt, gpre, gpost, w_gate, branches, w_branch, w_out, tm):
    B, T, D = x.shape
    bw = branches[0].shape[2]
    tok = lambda n: pl.BlockSpec((1, tm, n), lambda b, i: (b, i, 0))
    vec = pl.BlockSpec((1, 1, D), lambda b, i: (b, 0, 0))
    const2 = lambda arr: pl.BlockSpec(arr.shape, lambda b, i: (0, 0))
    return pl.pallas_call(
        _merge_kernel,
        out_shape=jax.ShapeDtypeStruct((B, T, D), F32),
        grid=(B, T // tm),
        in_specs=[tok(D), vec, vec, vec, const2(gpre), const2(gpost), const2(w_gate),
                  tok(bw), tok(bw), tok(bw), tok(bw),
                  pl.BlockSpec(w_branch.shape, lambda b, i: (0, 0, 0)), const2(w_out)],
        out_specs=tok(D),
        compiler_params=_cparams(("arbitrary", "arbitrary")),
        name="merge_out_proj",
    )(x, sc, sh, gt, gpre, gpost, w_gate, *branches, w_branch, w_out)


def _ffn_kernel(x_ref, sc_ref, sh_ref, gt_ref, gpre_ref, gpost_ref, wg_ref, wu_ref, wd_ref, o_ref, *, tf):
    x = x_ref[0]
    h = _rms_mod(x, gpre_ref[...], sc_ref[0], sh_ref[0]).astype(BF16)
    F = wg_ref.shape[1]
    acc = None
    for f0 in range(0, F, tf):
        f1 = min(f0 + tf, F)
        gate = jnp.dot(h, wg_ref[:, f0:f1], preferred_element_type=F32)
        up = jnp.dot(h, wu_ref[:, f0:f1], preferred_element_type=F32)
        t = jnp.dot((_silu(gate) * up).astype(BF16), wd_ref[f0:f1, :], preferred_element_type=F32)
        acc = t if acc is None else acc + t
    y = acc * lax.rsqrt(jnp.mean(acc * acc, axis=-1, keepdims=True) + RMS_EPS) * gpost_ref[...]
    o_ref[0] = x + gt_ref[0] * y


def _dense_ffn(x, sc, sh, gt, gpre, gpost, wg, wu, wd, tm):
    B, T, D = x.shape
    tok = pl.BlockSpec((1, tm, D), lambda b, i: (b, i, 0))
    vec = pl.BlockSpec((1, 1, D), lambda b, i: (b, 0, 0))
    const2 = lambda arr: pl.BlockSpec(arr.shape, lambda b, i: (0, 0))
    return pl.pallas_call(
        functools.partial(_ffn_kernel, tf=512),
        out_shape=jax.ShapeDtypeStruct((B, T, D), F32),
        grid=(B, T // tm),
        in_specs=[tok, vec, vec, vec, const2(gpre), const2(gpost), const2(wg), const2(wu), const2(wd)],
        out_specs=tok,
        compiler_params=_cparams(("arbitrary", "arbitrary")),
        name="dense_swiglu",
    )(x, sc, sh, gt, gpre, gpost, wg, wu, wd)


def _moe_kernel(x_ref, sc_ref, sh_ref, gt_ref, gpre_ref, gpost_ref, rw_ref, rb_ref, wg_ref, wu_ref,
                wd_ref, o_ref, h_ref, comb_ref, acc_ref):
    e = pl.program_id(2)
    f = pl.program_id(3)
    last = jnp.logical_and(e == pl.num_programs(2) - 1, f == pl.num_programs(3) - 1)

    @pl.when(jnp.logical_and(e == 0, f == 0))
    def _():
        h = _rms_mod(x_ref[0], gpre_ref[...], sc_ref[0], sh_ref[0])
        h_ref[...] = h.astype(BF16)
        logits = _mm_f32(h, rw_ref[...]) + rb_ref[...]
        lane = lax.broadcasted_iota(jnp.int32, logits.shape, 1)
        v1 = jnp.max(logits, axis=-1, keepdims=True)
        i1 = jnp.min(jnp.where(logits == v1, lane, LANES), axis=-1, keepdims=True)
        rest = jnp.where(lane == i1, -jnp.inf, logits)
        v2 = jnp.max(rest, axis=-1, keepdims=True)
        i2 = jnp.min(jnp.where(rest == v2, lane, LANES), axis=-1, keepdims=True)
        e2 = jnp.exp(v2 - v1)
        p1 = 1.0 / (1.0 + e2)
        p2 = e2 / (1.0 + e2)
        comb_ref[...] = jnp.where(lane == i1, p1, 0.0) + jnp.where(lane == i2, p2, 0.0)
        acc_ref[...] = jnp.zeros_like(acc_ref)

    lane = lax.broadcasted_iota(jnp.int32, comb_ref.shape, 1)
    w_e = jnp.sum(jnp.where(lane == e, comb_ref[...], 0.0), axis=-1, keepdims=True)
    h = h_ref[...]
    gate = jnp.dot(h, wg_ref[0], preferred_element_type=F32)
    up = jnp.dot(h, wu_ref[0], preferred_element_type=F32)
    act = (_silu(gate) * up * w_e).astype(BF16)
    acc_ref[...] += jnp.dot(act, wd_ref[0], preferred_element_type=F32)

    @pl.when(last)
    def _():
        acc = acc_ref[...]
        y = acc * lax.rsqrt(jnp.mean(acc * acc, axis=-1, keepdims=True) + RMS_EPS) * gpost_ref[...]
        o_ref[0] = x_ref[0] + gt_ref[0] * y


def _moe_ffn(x, sc, sh, gt, gpre, gpost, rw, rb, wg, wu, wd, tm, tf):
    B, T, D = x.shape
    E, _, F = wg.shape
    tok = pl.BlockSpec((1, tm, D), lambda b, i, e, f: (b, i, 0))
    vec = pl.BlockSpec((1, 1, D), lambda b, i, e, f: (b, 0, 0))
    const2 = lambda arr: pl.BlockSpec(arr.shape, lambda b, i, e, f: (0, 0))
    return pl.pallas_call(
        _moe_kernel,
        out_shape=jax.ShapeDtypeStruct((B, T, D), F32),
        grid=(B, T // tm, E, F // tf),
        in_specs=[tok, vec, vec, vec, const2(gpre), const2(gpost), const2(rw), const2(rb),
                  pl.BlockSpec((1, D, tf), lambda b, i, e, f: (e, 0, f)),
                  pl.BlockSpec((1, D, tf), lambda b, i, e, f: (e, 0, f)),
                  pl.BlockSpec((1, tf, D), lambda b, i, e, f: (e, f, 0))],
        out_specs=tok,
        scratch_shapes=[pltpu.VMEM((tm, D), BF16), pltpu.VMEM((tm, LANES), F32),
                        pltpu.VMEM((tm, D), F32)],
        compiler_params=_cparams(("arbitrary",) * 4),
        name="moe_swiglu",
    )(x, sc, sh, gt, gpre, gpost, rw, rb, wg, wu, wd)


def _rope_tables(positions, groups):
    d = 32
    inv = 1.0 / (ROPE_THETA ** (jnp.arange(0, d, 2, dtype=F32) / d))
    ang = positions.astype(F32)[..., None] * inv
    cos, sin = jnp.cos(ang), jnp.sin(ang)
    cos = jnp.tile(jnp.concatenate([cos, cos], axis=-1), (1, 1, groups))
    sin = jnp.tile(jnp.concatenate([-sin, sin], axis=-1), (1, 1, groups))
    return cos, sin


def _pad_rows(w, rows, offset):
    out = jnp.zeros((rows, w.shape[1]), w.dtype)
    return out.at[offset:offset + w.shape[0]].set(w)


def kernel(x, c, positions, ada_w, ada_b, norm_mix_pre, norm_mix_post, norm_ffn_pre, norm_ffn_post, w_in, gla_gate_w2, gla_gate_b, gla_norm, diff_lambda, diff_subln, conv_w, conv_b, conv_ln_g, conv_ln_b, rwkv_mu, rwkv_w0, rwkv_w2, rwkv_a0, rwkv_a2, rwkv_g2, rwkv_k_k, rwkv_k_a, rwkv_r_k, rwkv_ln_g, rwkv_ln_b, w_branch, w_out, ffn_w_gate, ffn_w_up, ffn_w_down, router_w, router_b, moe_w_gate, moe_w_up, moe_w_down):
    B, T, D = x.shape
    L = ada_w.shape[0]
    W = D // N_BRANCH
    hk = gla_gate_b.shape[1]
    decay_rank = rwkv_w2.shape[1]
    a_rank = rwkv_a2.shape[1]
    gate_rank = rwkv_g2.shape[1]
    assert decay_rank + a_rank == LANES and 2 * hk == W and gate_rank == LANES
    n_mix = 3 * W + 3 * W + 2 * W + (3 * W + decay_rank + a_rank + gate_rank)
    sizes = (hk, hk, W, W, GLA_GATE_RANK, W, W, W, W, W, 3 * W + LANES + gate_rank, N_BRANCH * D)
    offs = [0]
    for s in sizes:
        offs.append(offs[-1] + s)
    assert offs[-1] == w_in.shape[2]
    tm = min(512, T)
    tb = min(512, T)

    mod = _modulation(c, ada_w, ada_b)
    cos, sin = _rope_tables(positions, W // 32)

    for l in range(L):
        m = mod[l].reshape(B, 1, 6 * D)
        sh_m, sc_m, gt_m, sh_f, sc_f, gt_f = [m[:, :, i * D:(i + 1) * D] for i in range(6)]

        wl = w_in[l]
        gz_cols = jnp.zeros((D, LANES), F32).at[:, :GLA_GATE_RANK].set(wl[:, offs[4]:offs[5]])
        w_mix = jnp.concatenate([wl[:, offs[0]:offs[4]], wl[:, offs[5]:offs[11]], gz_cols], axis=1)
        w_mix = w_mix.astype(BF16)
        w_gate = wl[:, offs[11]:offs[12]].astype(BF16)
        proj = _in_projection(x, sc_m, sh_m, norm_mix_pre[l][None], w_mix, tm)

        w2p = _pad_rows(gla_gate_w2[l], LANES, 0)
        o_gla = _gla(proj, w2p, gla_gate_b[l][None], jnp.tile(gla_norm[l], GLA_HEADS)[None],
                     col_qk=0, col_v=1, col_og=2, col_gz=(n_mix // LANES), tb=tb)
        o_diff = _diff_attention(proj, cos, sin, diff_lambda[l],
                                 diff_subln[l][None], col_q=3, col_k=4, col_v=5, layer_idx=l)
        o_conv = _conformer_conv(proj, _pad_rows(conv_w[l], 32, 0), conv_b[l][None],
                                 conv_ln_g[l][None], conv_ln_b[l][None], col_a=6, col_b=7)
        vecs = dict(mu=rwkv_mu[l][None], w0=rwkv_w0[l][None], a0=rwkv_a0[l][None],
                    kk=rwkv_k_k[l][None], ka=rwkv_k_a[l][None], rk=rwkv_r_k[l].reshape(1, W),
                    lng=rwkv_ln_g[l][None], lnb=rwkv_ln_b[l][None])
        o_rwkv = _rwkv7(proj, vecs, _pad_rows(rwkv_w2[l], LANES, 0).astype(BF16),
                        _pad_rows(rwkv_a2[l], LANES, decay_rank).astype(BF16),
                        rwkv_g2[l].astype(BF16), col=2, tb=tb)
        x = _merge(x, sc_m, sh_m, gt_m, norm_mix_pre[l][None], norm_mix_post[l][None], w_gate,
                   (o_gla, o_diff, o_conv, o_rwkv), w_branch[l].astype(BF16), w_out[l].astype(BF16), tm)

        i = l // 2
        if l % 2 == 0:
            x = _dense_ffn(x, sc_f, sh_f, gt_f, norm_ffn_pre[l][None], norm_ffn_post[l][None],
                           ffn_w_gate[i].astype(BF16), ffn_w_up[i].astype(BF16),
                           ffn_w_down[i].astype(BF16), tm)
        else:
            rw = jnp.zeros((D, LANES), F32).at[:, :N_EXPERTS].set(router_w[i])
            rb = jnp.full((1, LANES), -jnp.inf, F32).at[0, :N_EXPERTS].set(router_b[i])
            x = _moe_ffn(x, sc_f, sh_f, gt_f, norm_ffn_pre[l][None], norm_ffn_post[l][None], rw, rb,
                         moe_w_gate[i].astype(BF16), moe_w_up[i].astype(BF16),
                         moe_w_down[i].astype(BF16), tm=min(1024, T), tf=512)
    return x
```

```python
import functools
import math

import jax
import jax.numpy as jnp
from jax import lax
from jax.experimental import pallas as pl
from jax.experimental.pallas import tpu as pltpu

F32 = jnp.float32
BF16 = jnp.bfloat16
HIGHEST = lax.Precision.HIGHEST

N_BRANCH = 4
GLA_HEADS = 4
GLA_GATE_RANK = 16
GLA_GATE_NORMALIZER = 16.0
GLA_CHUNK = 32
DIFF_HEADS = 4
ROPE_THETA = 10000.0
CONV_WIDTH = 31
RWKV_HEADS = 4
RWKV_CHUNK = 64
RWKV_SUB = 16
N_EXPERTS = 8
RMS_EPS = 1e-6
LN_EPS = 1e-5
RWKV_GN_EPS = 64e-5
LANES = 128
VMEM_LIMIT = 56 * 1024 * 1024


def _cparams(sem):
    return pltpu.CompilerParams(dimension_semantics=sem, vmem_limit_bytes=VMEM_LIMIT)


def _mm(a, b):
    return jnp.dot(a.astype(BF16), b.astype(BF16), preferred_element_type=F32)


def _mm_nt(a, b):
    return lax.dot_general(a.astype(BF16), b.astype(BF16), (((1,), (1,)), ((), ())),
                           preferred_element_type=F32)


def _mm_tn(a, b):
    return lax.dot_general(a.astype(BF16), b.astype(BF16), (((0,), (0,)), ((), ())),
                           preferred_element_type=F32)


def _bmm(a, b):
    return lax.dot_general(a.astype(BF16), b.astype(BF16), (((2,), (1,)), ((0,), (0,))),
                           preferred_element_type=F32)


def _mm_f32(a, b):
    return jnp.dot(a, b, precision=HIGHEST, preferred_element_type=F32)


def _hi_lo(x):
    hi = x.astype(BF16)
    return jnp.concatenate([hi, (x - hi.astype(F32)).astype(BF16)], axis=1)


def _sigmoid(x):
    return 1.0 / (1.0 + jnp.exp(-x))


def _silu(x):
    return x * _sigmoid(x)


def _softplus(x):
    return jnp.maximum(x, 0.0) + jnp.log(1.0 + jnp.exp(-jnp.abs(x)))


def _group_matrix(n, group):
    r = lax.broadcasted_iota(jnp.int32, (n, n), 0) // group
    c = lax.broadcasted_iota(jnp.int32, (n, n), 1) // group
    return r == c


def _rms_mod(x, gain, scale, shift):
    y = x * lax.rsqrt(jnp.mean(x * x, axis=-1, keepdims=True) + RMS_EPS)
    return y * gain * (1.0 + scale) + shift


def _mod_kernel(c_ref, w_ref, b_ref, o_ref):
    o_ref[0] = _mm_f32(_silu(c_ref[...]), w_ref[0]) + b_ref[0]


def _modulation(c, ada_w, ada_b):
    L, D, M = ada_w.shape
    B = c.shape[0]
    tn = M // 4
    return pl.pallas_call(
        _mod_kernel,
        out_shape=jax.ShapeDtypeStruct((L, B, M), F32),
        grid=(L, M // tn),
        in_specs=[pl.BlockSpec((B, D), lambda l, j: (0, 0)),
                  pl.BlockSpec((1, D, tn), lambda l, j: (l, 0, j)),
                  pl.BlockSpec((1, 1, tn), lambda l, j: (l, 0, j))],
        out_specs=pl.BlockSpec((1, B, tn), lambda l, j: (l, 0, j)),
        compiler_params=_cparams(("arbitrary", "arbitrary")),
        name="adaln_mod",
    )(c, ada_w, ada_b.reshape(L, 1, M))


def _inproj_kernel(x_ref, sc_ref, sh_ref, g_ref, w_ref, o_ref):
    h = _rms_mod(x_ref[0], g_ref[...], sc_ref[0], sh_ref[0])
    o_ref[0] = _mm(h, w_ref[...])


def _in_projection(x, sc, sh, gain, w, tm):
    B, T, D = x.shape
    n = w.shape[1]
    return pl.pallas_call(
        _inproj_kernel,
        out_shape=jax.ShapeDtypeStruct((B, T, n), F32),
        grid=(B, T // tm),
        in_specs=[pl.BlockSpec((1, tm, D), lambda b, i: (b, i, 0)),
                  pl.BlockSpec((1, 1, D), lambda b, i: (b, 0, 0)),
                  pl.BlockSpec((1, 1, D), lambda b, i: (b, 0, 0)),
                  pl.BlockSpec((1, D), lambda b, i: (0, 0)),
                  pl.BlockSpec((D, n), lambda b, i: (0, 0))],
        out_specs=pl.BlockSpec((1, tm, n), lambda b, i: (b, i, 0)),
        compiler_params=_cparams(("arbitrary", "arbitrary")),
        name="in_proj",
    )(x, sc, sh, gain, w)


def _gla_kernel(qk_ref, v_ref, og_ref, gz_ref, w2_ref, gb_ref, ng_ref, o_ref, s_ref, g_ref, r_ref,
                qd_ref, kd_ref, oacc_ref, *, n_chunk):
    C = GLA_CHUNK
    hk = qk_ref.shape[2] // 2
    hv = v_ref.shape[2]
    dk = hk // GLA_HEADS
    dv = hv // GLA_HEADS

    tb = n_chunk * C

    @pl.when(pl.program_id(1) == 0)
    def _():
        s_ref[...] = jnp.zeros_like(s_ref)

    z = _mm(gz_ref[0], w2_ref[...]) + gb_ref[...]
    gk = (jnp.minimum(z, 0.0) - jnp.log(1.0 + jnp.exp(-jnp.abs(z)))) / GLA_GATE_NORMALIZER
    bi = lax.broadcasted_iota(jnp.int32, (tb, tb), 0)
    bj = lax.broadcasted_iota(jnp.int32, (tb, tb), 1)
    same_chunk = (bi // C) == (bj // C)
    prefix = jnp.logical_and(same_chunk, bj <= bi)
    suffix = jnp.logical_and(same_chunk, bj > bi)
    sums = jnp.dot(jnp.concatenate([prefix, suffix], axis=0).astype(BF16), _hi_lo(gk),
                   preferred_element_type=F32)
    G_all = sums[0:tb, 0:hk] + sums[0:tb, hk:2 * hk]
    R_all = sums[tb:2 * tb, 0:hk] + sums[tb:2 * tb, hk:2 * hk]
    g_ref[...] = G_all
    r_ref[...] = R_all
    qd_ref[...] = (qk_ref[0, :, 0:hk] * (dk ** -0.5) * jnp.exp(G_all)).astype(BF16)
    kd_ref[...] = (qk_ref[0, :, hk:2 * hk] * jnp.exp(R_all)).astype(BF16)

    pr = lax.broadcasted_iota(jnp.int32, (C * C, C), 0)
    pc = lax.broadcasted_iota(jnp.int32, (C * C, C), 1)
    row_sel = (pr // C == pc).astype(BF16)
    pi = lax.broadcasted_iota(jnp.int32, (C * C, 1), 0)
    causal = (pi % C) <= (pi // C)
    er = lax.broadcasted_iota(jnp.int32, (hk, hv), 0) // dk
    ec = lax.broadcasted_iota(jnp.int32, (hk, hv), 1) // dv
    expand = (er == ec).astype(BF16)
    sr = lax.broadcasted_iota(jnp.int32, (hv, hk), 0) // dv
    scol = lax.broadcasted_iota(jnp.int32, (hv, hk), 1) // dk
    state_mask = sr == scol
    head_mean = (_group_matrix(hv, dv).astype(F32) / dv).astype(BF16)

    def chunk(ci, carry):
        r0 = pl.multiple_of(ci * C, C)
        rows = pl.ds(r0, C)
        q = qk_ref[0, rows, 0:hk] * (dk ** -0.5)
        k = qk_ref[0, rows, hk:2 * hk]
        v = v_ref[0, rows, :]
        G = g_ref[rows, :]
        sel = jnp.dot(row_sel, jnp.concatenate([_hi_lo(G), q.astype(BF16)], axis=1),
                      preferred_element_type=F32)
        G_i = sel[:, 0:hk] + sel[:, hk:2 * hk]
        q_i = sel[:, 2 * hk:3 * hk]
        G_j = jnp.broadcast_to(G[None], (C, C, hk)).reshape(C * C, hk)
        k_j = jnp.broadcast_to(k[None], (C, C, hk)).reshape(C * C, hk)
        decay = jnp.exp(jnp.where(causal, G_i - G_j, -jnp.inf))
        p = q_i * k_j * decay
        a_exp = jnp.dot(p.astype(BF16), expand, preferred_element_type=F32)
        v_j = jnp.broadcast_to(v[None], (C, C, hv))
        o_intra = jnp.sum(a_exp.reshape(C, C, hv) * v_j, axis=1)
        g_total = G[0:1, :] + r_ref[pl.ds(r0, 8), :][0:1, :]
        s = s_ref[...]
        o_inter = lax.dot_general(qd_ref[rows, :], s.astype(BF16), (((1,), (1,)), ((), ())),
                                  preferred_element_type=F32)
        kv = lax.dot_general(v.astype(BF16), kd_ref[rows, :], (((0,), (0,)), ((), ())),
                             preferred_element_type=F32)
        s_ref[...] = s * jnp.exp(g_total) + jnp.where(state_mask, kv, 0.0)
        oacc_ref[rows, :] = o_intra + o_inter
        return carry

    lax.fori_loop(0, n_chunk, chunk, 0)

    o = oacc_ref[...]
    ms = jnp.dot(_hi_lo(o * o), jnp.concatenate([head_mean, head_mean], axis=0),
                 preferred_element_type=F32)
    o = o * lax.rsqrt(ms + RMS_EPS) * ng_ref[...] * _silu(og_ref[0])
    o_ref[0] = o.astype(o_ref.dtype)


def _gla(proj, w2p, gb, ng, col_qk, col_v, col_og, col_gz, tb):
    B, T, _ = proj.shape
    hv = ng.shape[1]
    hk = gb.shape[1]
    return pl.pallas_call(
        functools.partial(_gla_kernel, n_chunk=tb // GLA_CHUNK),
        out_shape=jax.ShapeDtypeStruct((B, T, hv), BF16),
        grid=(B, T // tb),
        in_specs=[pl.BlockSpec((1, tb, 2 * hk), lambda b, j: (b, j, col_qk)),
                  pl.BlockSpec((1, tb, hv), lambda b, j: (b, j, col_v)),
                  pl.BlockSpec((1, tb, hv), lambda b, j: (b, j, col_og)),
                  pl.BlockSpec((1, tb, LANES), lambda b, j: (b, j, col_gz)),
                  pl.BlockSpec(w2p.shape, lambda b, j: (0, 0)),
                  pl.BlockSpec(gb.shape, lambda b, j: (0, 0)),
                  pl.BlockSpec(ng.shape, lambda b, j: (0, 0))],
        out_specs=pl.BlockSpec((1, tb, hv), lambda b, j: (b, j, 0)),
        scratch_shapes=[pltpu.VMEM((hv, hk), F32), pltpu.VMEM((tb, hk), F32), pltpu.VMEM((tb, hk), F32),
                        pltpu.VMEM((tb, hk), BF16), pltpu.VMEM((tb, hk), BF16), pltpu.VMEM((tb, hv), F32)],
        compiler_params=_cparams(("arbitrary", "arbitrary")),
        name="gla_mixer",
    )(proj, proj, proj, proj, w2p, gb, ng)


def _rope(t, cos, sin_signed):
    d = 32
    half = d // 2
    out = []
    for s in range(t.shape[1] // LANES):
        x = t[:, s * LANES:(s + 1) * LANES]
        lane = lax.broadcasted_iota(jnp.int32, x.shape, 1)
        up = pltpu.roll(x, LANES - half, 1)
        down = pltpu.roll(x, half, 1)
        rot = jnp.where((lane % d) < half, up, down)
        out.append(x * cos[:, s * LANES:(s + 1) * LANES] + rot * sin_signed[:, s * LANES:(s + 1) * LANES])
    return jnp.concatenate(out, axis=1)


def _diff_kernel(q_ref, k_ref, v_ref, cosq_ref, sinq_ref, cosk_ref, sink_ref, lam_ref, g_ref, o_ref,
                 ks, vs, *, tq, lam_init):
    H = DIFF_HEADS
    d = q_ref.shape[2] // (2 * H)
    dv = v_ref.shape[2] // H
    j = pl.program_id(1)

    @pl.when(j == 0)
    def _():
        k = _rope(k_ref[0], cosk_ref[0], sink_ref[0])
        v = v_ref[0]
        for hc in range(2 * H):
            ks[hc] = k[:, hc * d:(hc + 1) * d].astype(BF16)
        ones_col = (lax.broadcasted_iota(jnp.int32, (v.shape[0], dv), 1) == 0).astype(BF16)
        for h in range(H):
            vs[h] = jnp.concatenate([v[:, h * dv:(h + 1) * dv].astype(BF16), ones_col], axis=1)

    q = _rope(q_ref[0], cosq_ref[0], sinq_ref[0]) * (d ** -0.5)
    lp = lam_ref[...]
    lam = (jnp.exp(jnp.sum(lp[0:1] * lp[1:2], axis=-1, keepdims=True))
           - jnp.exp(jnp.sum(lp[2:3] * lp[3:4], axis=-1, keepdims=True)) + lam_init)
    on_or_below_diag = (lax.broadcasted_iota(jnp.int32, (tq, tq), 1)
                        <= lax.broadcasted_iota(jnp.int32, (tq, tq), 0))

    def update(qh, kh, vh, m, acc, masked):
        s = lax.dot_general(qh, kh, (((1,), (1,)), ((), ())), preferred_element_type=F32)
        if masked:
            s = jnp.where(on_or_below_diag, s, -jnp.inf)
        m_new = jnp.maximum(m, jnp.max(s, axis=-1, keepdims=True))
        p = jnp.exp(s - m_new).astype(BF16)
        acc = jnp.exp(m - m_new) * acc + jnp.dot(p, vh, preferred_element_type=F32)
        return m_new, acc

    def two_softmax_pv(h):
        q1 = q[:, (2 * h) * d:(2 * h + 1) * d].astype(BF16)
        q2 = q[:, (2 * h + 1) * d:(2 * h + 2) * d].astype(BF16)

        def kv_step(kb, carry):
            m1, a1, m2, a2 = carry
            rows = pl.ds(pl.multiple_of(kb * tq, tq), tq)
            vh = vs[h, rows, :]
            m1, a1 = update(q1, ks[2 * h, rows, :], vh, m1, a1, False)
            m2, a2 = update(q2, ks[2 * h + 1, rows, :], vh, m2, a2, False)
            return m1, a1, m2, a2

        m0 = jnp.full((tq, 1), -jnp.inf, F32)
        a0 = jnp.zeros((tq, 2 * dv), F32)
        m1, a1, m2, a2 = lax.fori_loop(0, j, kv_step, (m0, a0, m0, a0))
        rows = pl.ds(pl.multiple_of(j * tq, tq), tq)
        vh = vs[h, rows, :]
        _, a1 = update(q1, ks[2 * h, rows, :], vh, m1, a1, True)
        _, a2 = update(q2, ks[2 * h + 1, rows, :], vh, m2, a2, True)
        return a1, a2

    for h in range(H):
        a1, a2 = two_softmax_pv(h)
        comp = [a[:, 0:dv] / a[:, dv:dv + 1] for a in (a1, a2)]
        o = comp[0] - lam * comp[1]
        o = o * lax.rsqrt(jnp.mean(o * o, axis=-1, keepdims=True) + RMS_EPS)
        o = o * g_ref[...] * (1.0 - lam_init)
        o_ref[0, :, h * dv:(h + 1) * dv] = o.astype(o_ref.dtype)


def _diff_attention(proj, cos, sin, lam_p, g, col_q, col_k, col_v, layer_idx):
    B, T, _ = proj.shape
    H = DIFF_HEADS
    w = cos.shape[2]
    d = w // (2 * H)
    dv = 2 * d
    tq = min(512, T)
    lam_init = 0.8 - 0.6 * math.exp(-0.3 * layer_idx)
    blk = lambda col: pl.BlockSpec((1, tq, w), lambda b, j: (b, j, col))
    full = lambda col: pl.BlockSpec((1, T, w), lambda b, j: (b, 0, col))
    return pl.pallas_call(
        functools.partial(_diff_kernel, tq=tq, lam_init=lam_init),
        out_shape=jax.ShapeDtypeStruct((B, T, H * dv), BF16),
        grid=(B, T // tq),
        in_specs=[blk(col_q), full(col_k), full(col_v), blk(0), blk(0), full(0), full(0),
                  pl.BlockSpec(lam_p.shape, lambda b, j: (0, 0)),
                  pl.BlockSpec(g.shape, lambda b, j: (0, 0))],
        out_specs=pl.BlockSpec((1, tq, H * dv), lambda b, j: (b, j, 0)),
        scratch_shapes=[pltpu.VMEM((2 * H, T, d), BF16),
                        pltpu.VMEM((H, T, 2 * dv), BF16)],
        compiler_params=_cparams(("arbitrary", "arbitrary")),
        name="diff_attention",
    )(proj, proj, proj, cos, sin, cos, sin, lam_p, g)


def _conv_kernel(a_ref, b_ref, w_ref, cb_ref, lg_ref, lb_ref, o_ref, u_ref, *, rb):
    T = a_ref.shape[1]
    pad = u_ref.shape[0] - T
    u_ref[0:pad, :] = jnp.zeros((pad, u_ref.shape[1]), F32)
    u_ref[pad:pad + T, :] = a_ref[0] * _sigmoid(b_ref[0])
    first = pad - (CONV_WIDTH - 1)

    def block(i, carry):
        r0 = pl.multiple_of(i * rb, rb)
        win = u_ref[pl.ds(r0, rb + pad), :]
        acc = jnp.zeros((rb, u_ref.shape[1]), F32)
        for j in range(CONV_WIDTH):
            acc = acc + w_ref[j:j + 1, :] * win[first + j:first + j + rb, :]
        y = acc + cb_ref[...]
        mu = jnp.mean(y, axis=-1, keepdims=True)
        yc = y - mu
        var = jnp.mean(yc * yc, axis=-1, keepdims=True)
        y = yc * lax.rsqrt(var + LN_EPS) * lg_ref[...] + lb_ref[...]
        o_ref[0, pl.ds(r0, rb), :] = _silu(y).astype(o_ref.dtype)
        return carry

    lax.fori_loop(0, T // rb, block, 0)


def _conformer_conv(proj, w, cb, lg, lb, col_a, col_b):
    B, T, _ = proj.shape
    ch = w.shape[1]
    rb = min(128, T)
    return pl.pallas_call(
        functools.partial(_conv_kernel, rb=rb),
        out_shape=jax.ShapeDtypeStruct((B, T, ch), BF16),
        grid=(B,),
        in_specs=[pl.BlockSpec((1, T, ch), lambda b: (b, 0, col_a)),
                  pl.BlockSpec((1, T, ch), lambda b: (b, 0, col_b)),
                  pl.BlockSpec(w.shape, lambda b: (0, 0)),
                  pl.BlockSpec(cb.shape, lambda b: (0, 0)),
                  pl.BlockSpec(lg.shape, lambda b: (0, 0)),
                  pl.BlockSpec(lb.shape, lambda b: (0, 0))],
        out_specs=pl.BlockSpec((1, T, ch), lambda b: (b, 0, 0)),
        scratch_shapes=[pltpu.VMEM((T + 32, ch), F32)],
        compiler_params=_cparams(("arbitrary",)),
        name="conformer_conv",
    )(proj, proj, w, cb, lg, lb)


def _rwkv_kernel(x_ref, mu_ref, w0_ref, w2_ref, a0_ref, a2_ref, g2_ref, kk_ref, ka_ref, rk_ref,
                 lng_ref, lnb_ref, o_ref, s_ref, prev_ref, *, n_chunk):
    C = RWKV_CHUNK
    H = RWKV_HEADS
    W = o_ref.shape[2]
    N = W // H

    @pl.when(pl.program_id(1) == 0)
    def _():
        s_ref[...] = jnp.zeros_like(s_ref)
        prev_ref[...] = jnp.zeros_like(prev_ref)

    lane = lax.broadcasted_iota(jnp.int32, (1, W), 1)
    head_mask = [(lane // N == h).astype(F32) for h in range(H)]
    block_diag = _group_matrix(W, N)
    head_sum = block_diag.astype(F32)
    ti = lax.broadcasted_iota(jnp.int32, (C, C), 0)
    tj = lax.broadcasted_iota(jnp.int32, (C, C), 1)
    tril_incl = ti >= tj
    tril_strict = ti > tj
    same_sub = (ti // RWKV_SUB) == (tj // RWKV_SUB)
    eye = (ti == tj).astype(F32)
    cum = tril_incl.astype(F32)
    first_row = lax.broadcasted_iota(jnp.int32, (C, 1), 0) == 0

    def per_head(mats, x):
        reps = x.shape[1] // W
        acc = None
        for h in range(H):
            m = head_mask[h] if reps == 1 else jnp.concatenate([head_mask[h]] * reps, axis=1)
            t = m * _mm(mats[h], x)
            acc = t if acc is None else acc + t
        return acc

    def chunk(ci, carry):
        r0 = pl.multiple_of(ci * C, C)
        rows = pl.ds(r0, C)
        x = x_ref[0, rows, :]
        prev = jnp.where(first_row, prev_ref[...], pltpu.roll(x, 1, 0))
        prev_ref[...] = x[C - 1:C, :]
        xm = x + (prev - x) * mu_ref[...]
        r = xm[:, 0:W]
        k = xm[:, W:2 * W]
        v = xm[:, 2 * W:3 * W]
        zz = xm[:, 3 * W:3 * W + LANES]
        zg = xm[:, 3 * W + LANES:]
        w = -_softplus(-(w0_ref[...] + _mm(jnp.tanh(zz), w2_ref[...]))) - 0.5
        lw = -jnp.exp(w)
        a = _sigmoid(a0_ref[...] + _mm(zz, a2_ref[...]))
        g = _mm(_sigmoid(zg), g2_ref[...])
        kk = k * kk_ref[...]
        kk = kk / jnp.maximum(jnp.sqrt(_mm_f32(kk * kk, head_sum)), 1e-12)
        k = k * (1.0 + (a - 1.0) * ka_ref[...])
        b = kk * a

        G = _mm_f32(cum, lw)
        Gp = G - lw
        g_end = G[C - 1:C, :]
        kap = kk * jnp.exp(Gp)
        rho = r * jnp.exp(G)
        inv = jnp.exp(-G)
        bet = b * inv
        kt = k * inv
        to_end = jnp.exp(g_end - G)
        bet_c = b * to_end
        k_c = k * to_end

        lhs = jnp.concatenate([kap * head_mask[h] for h in range(H)]
                              + [rho * head_mask[h] for h in range(H)], axis=0)
        pb = _mm_nt(lhs, bet)
        pk = _mm_nt(lhs, kt)
        A_b = jnp.where(tril_strict, pb[0:H * C].reshape(H, C, C), 0.0)
        A_k = jnp.where(tril_strict, pk[0:H * C].reshape(H, C, C), 0.0)
        B_b = jnp.where(tril_incl, pb[H * C:].reshape(H, C, C), 0.0)
        B_k = jnp.where(tril_incl, pk[H * C:].reshape(H, C, C), 0.0)

        Dg = jnp.where(same_sub, A_b, 0.0)
        Lo = A_b - Dg
        D2 = _bmm(Dg, Dg)
        D4 = _bmm(D2, D2)
        D8 = _bmm(D4, D4)
        Dinv = _bmm(_bmm(_bmm(eye - Dg, eye + D2), eye + D4), eye + D8)
        Nn = _bmm(Dinv, Lo)
        N2 = _bmm(Nn, Nn)
        Tm = _bmm(_bmm(eye - Nn, eye + N2), Dinv)

        akv = per_head(A_k, v)
        tk = per_head(Tm, jnp.concatenate([kap, akv], axis=1))
        kap_p = tk[:, 0:W]
        v_p = tk[:, W:2 * W]
        bb = per_head(B_b, jnp.concatenate([kap_p, v_p], axis=1))
        q_eff = rho - bb[:, 0:W]
        y_loc = per_head(B_k, v) - bb[:, W:2 * W]
        m_low = jnp.where(block_diag, _mm_tn(kap_p, bet_c), 0.0)
        n_c = jnp.where(block_diag, _mm_tn(v, k_c) - _mm_tn(v_p, bet_c), 0.0)

        s = s_ref[...]
        y = _mm_nt(q_eff, s) + y_loc
        s_ref[...] = s * jnp.exp(g_end) - _mm(s, m_low) + n_c

        mean = _mm_f32(y, head_sum) / N
        yc = y - mean
        var = _mm_f32(yc * yc, head_sum) / N
        yn = yc * lax.rsqrt(var + RWKV_GN_EPS) * lng_ref[...] + lnb_ref[...]
        bonus = _mm_f32(r * k * rk_ref[...], head_sum) * v
        o_ref[0, rows, :] = ((yn + bonus) * g).astype(o_ref.dtype)
        return carry

    lax.fori_loop(0, n_chunk, chunk, 0)


def _rwkv7(proj, vecs, w2p, a2p, g2, col, tb):
    B, T, _ = proj.shape
    W = g2.shape[1]
    cols = vecs["mu"].shape[1]
    names = ("mu", "w0", "w2", "a0", "a2", "g2", "kk", "ka", "rk", "lng", "lnb")
    params = dict(vecs, w2=w2p, a2=a2p, g2=g2)
    const = lambda arr: pl.BlockSpec(arr.shape, lambda b, j: (0, 0))
    return pl.pallas_call(
        functools.partial(_rwkv_kernel, n_chunk=tb // RWKV_CHUNK),
        out_shape=jax.ShapeDtypeStruct((B, T, W), BF16),
        grid=(B, T // tb),
        in_specs=[pl.BlockSpec((1, tb, cols), lambda b, j: (b, j, col))]
                 + [const(params[n]) for n in names],
        out_specs=pl.BlockSpec((1, tb, W), lambda b, j: (b, j, 0)),
        scratch_shapes=[pltpu.VMEM((W, W), F32), pltpu.VMEM((1, cols), F32)],
        compiler_params=_cparams(("arbitrary", "arbitrary")),
        name="rwkv7_mixer",
    )(proj, *[params[n] for n in names])


def _merge_kernel(x_ref, sc_ref, sh_ref, gt_ref, gpre_ref, gpost_ref, wg_ref, b0_ref, b1_ref, b2_ref,
                  b3_ref, wb_ref, wo_ref, o_ref):
    x = x_ref[0]
    D = x.shape[1]
    h = _rms_mod(x, gpre_ref[...], sc_ref[0], sh_ref[0]).astype(BF16)
    merged = None
    for g, br in enumerate((b0_ref, b1_ref, b2_ref, b3_ref)):
        gate = _sigmoid(jnp.dot(h, wg_ref[:, g * D:(g + 1) * D], preferred_element_type=F32))
        t = gate * jnp.dot(br[0], wb_ref[g], preferred_element_type=F32)
        merged = t if merged is None else merged + t
    y = _mm(merged, wo_ref[...])
    y = y * lax.rsqrt(jnp.mean(y * y, axis=-1, keepdims=True) + RMS_EPS) * gpost_ref[...]
    o_ref[0] = x + gt_ref[0] * y


def _merge(x, sc, sh, gt, gpre, gpost, w_gate, branches, w_branch, w_out, tm):
    B, T, D = x.shape
    bw = branches[0].shape[2]
    tok = lambda n: pl.BlockSpec((1, tm, n), lambda b, i: (b, i, 0))
    vec = pl.BlockSpec((1, 1, D), lambda b, i: (b, 0, 0))
    const2 = lambda arr: pl.BlockSpec(arr.shape, lambda b, i: (0, 0))
    return pl.pallas_call(
        _merge_kernel,
        out_shape=jax.ShapeDtypeStruct((B, T, D), F32),
        grid=(B, T // tm),
        in_specs=[tok(D), vec, vec, vec, const2(gpre), const2(gpost), const2(w_gate),
                  tok(bw), tok(bw), tok(bw), tok(bw),
                  pl.BlockSpec(w_branch.shape, lambda b, i: (0, 0, 0)), const2(w_out)],
        out_specs=tok(D),
        compiler_params=_cparams(("arbitrary", "arbitrary")),
        name="merge_out_proj",
    )(x, sc, sh, gt, gpre, gpost, w_gate, *branches, w_branch, w_out)


def _ffn_kernel(x_ref, sc_ref, sh_ref, gt_ref, gpre_ref, gpost_ref, wg_ref, wu_ref, wd_ref, o_ref, *, tf):
    x = x_ref[0]
    h = _rms_mod(x, gpre_ref[...], sc_ref[0], sh_ref[0]).astype(BF16)
    F = wg_ref.shape[1]
    acc = None
    for f0 in range(0, F, tf):
        f1 = min(f0 + tf, F)
        gate = jnp.dot(h, wg_ref[:, f0:f1], preferred_element_type=F32)
        up = jnp.dot(h, wu_ref[:, f0:f1], preferred_element_type=F32)
        t = jnp.dot((_silu(gate) * up).astype(BF16), wd_ref[f0:f1, :], preferred_element_type=F32)
        acc = t if acc is None else acc + t
    y = acc * lax.rsqrt(jnp.mean(acc * acc, axis=-1, keepdims=True) + RMS_EPS) * gpost_ref[...]
    o_ref[0] = x + gt_ref[0] * y


def _dense_ffn(x, sc, sh, gt, gpre, gpost, wg, wu, wd, tm):
    B, T, D = x.shape
    tok = pl.BlockSpec((1, tm, D), lambda b, i: (b, i, 0))
    vec = pl.BlockSpec((1, 1, D), lambda b, i: (b, 0, 0))
    const2 = lambda arr: pl.BlockSpec(arr.shape, lambda b, i: (0, 0))
    return pl.pallas_call(
        functools.partial(_ffn_kernel, tf=512),
        out_shape=jax.ShapeDtypeStruct((B, T, D), F32),
        grid=(B, T // tm),
        in_specs=[tok, vec, vec, vec, const2(gpre), const2(gpost), const2(wg), const2(wu), const2(wd)],
        out_specs=tok,
        compiler_params=_cparams(("arbitrary", "arbitrary")),
        name="dense_swiglu",
    )(x, sc, sh, gt, gpre, gpost, wg, wu, wd)


MOE_TOKEN_TILE = 256
MOE_ROW_TILE = 512
SEG_ALIGN = 16
SEG_PIECES = (256, 128, 64, 32, 16)


def _route_kernel(x_ref, sc_ref, sh_ref, gpre_ref, rw_ref, rb_ref, h_ref, mi_ref, mp_ref, cnt_ref):
    tm = x_ref.shape[1]
    h = _rms_mod(x_ref[0], gpre_ref[...], sc_ref[0], sh_ref[0])
    h_ref[...] = h.astype(BF16)
    logits = _mm_f32(h, rw_ref[...]) + rb_ref[...]
    lane = lax.broadcasted_iota(jnp.int32, logits.shape, 1)
    v1 = jnp.max(logits, axis=-1, keepdims=True)
    i1 = jnp.min(jnp.where(logits == v1, lane, LANES), axis=-1, keepdims=True)
    rest = jnp.where(lane == i1, -jnp.inf, logits)
    v2 = jnp.max(rest, axis=-1, keepdims=True)
    i2 = jnp.min(jnp.where(rest == v2, lane, LANES), axis=-1, keepdims=True)
    e2 = jnp.exp(v2 - v1)
    p1 = 1.0 / (1.0 + e2)
    p2 = e2 / (1.0 + e2)
    oh1 = (lane == i1).astype(F32)
    oh2 = (lane == i2).astype(F32)
    both = oh1 + oh2
    earlier = (lax.broadcasted_iota(jnp.int32, (tm, tm), 1)
               < lax.broadcasted_iota(jnp.int32, (tm, tm), 0)).astype(BF16)
    before = jnp.dot(earlier, both.astype(BF16), preferred_element_type=F32)
    r1 = jnp.sum(oh1 * before, axis=-1, keepdims=True).astype(jnp.int32)
    r2 = jnp.sum(oh2 * before, axis=-1, keepdims=True).astype(jnp.int32)
    col = lax.broadcasted_iota(jnp.int32, mi_ref.shape, 1)
    mi_ref[...] = jnp.where(col == 0, i1, jnp.where(col == 1, i2, jnp.where(col == 2, r1,
                                                                           jnp.where(col == 3, r2, 0))))
    mp_ref[...] = jnp.where(col == 0, p1, jnp.where(col == 1, p2, 0.0))
    cnt_ref[0] = jnp.sum(both, axis=0, keepdims=True).astype(jnp.int32)


def _segment_pieces(n_rows):
    out = []
    for s in SEG_PIECES:
        if s == MOE_TOKEN_TILE:
            out.append((n_rows == s, 0, s))
        else:
            out.append(((n_rows & s) != 0, pl.multiple_of((n_rows // (2 * s)) * (2 * s), SEG_ALIGN), s))
    return out


def _dispatch_kernel(seg_ref, cnt_ref, h_ref, mit_ref, init_ref, xs_ref, buf_ref, sem):
    del init_ref
    tm = h_ref.shape[0]
    i = pl.program_id(0)
    e1, e2 = mit_ref[0:1, :], mit_ref[1:2, :]
    r1, r2 = mit_ref[2:3, :], mit_ref[3:4, :]
    row = lax.broadcasted_iota(jnp.int32, (tm, tm), 0)
    select = jnp.concatenate(
        [jnp.logical_or(jnp.logical_and(e1 == e, r1 == row), jnp.logical_and(e2 == e, r2 == row))
         for e in range(N_EXPERTS)], axis=0).astype(BF16)
    buf_ref[...] = jnp.dot(select, h_ref[...], preferred_element_type=F32).astype(BF16)

    def segment_copies(e):
        n = cnt_ref[i * N_EXPERTS + e]
        n_rows = ((n + SEG_ALIGN - 1) // SEG_ALIGN) * SEG_ALIGN
        dst = pl.multiple_of(seg_ref[i * N_EXPERTS + e], SEG_ALIGN)
        return [(cond, pltpu.make_async_copy(buf_ref.at[pl.ds(e * tm + off, s), :],
                                             xs_ref.at[pl.ds(dst + off, s), :], sem))
                for cond, off, s in _segment_pieces(n_rows)]

    for e in range(N_EXPERTS):
        for cond, cp in segment_copies(e):
            pl.when(cond)(cp.start)
    for e in range(N_EXPERTS):
        for cond, cp in segment_copies(e):
            pl.when(cond)(cp.wait)


def _expert_kernel(te_ref, nv_ref, xs_ref, wg_ref, wu_ref, wd_ref, ys_ref, acc_ref):
    del te_ref
    r = pl.program_id(0)
    f = pl.program_id(1)
    valid = r < nv_ref[0]

    @pl.when(jnp.logical_and(valid, f == 0))
    def _():
        acc_ref[...] = jnp.zeros_like(acc_ref)

    @pl.when(valid)
    def _():
        x = xs_ref[...]
        gate = jnp.dot(x, wg_ref[0], preferred_element_type=F32)
        up = jnp.dot(x, wu_ref[0], preferred_element_type=F32)
        acc_ref[...] += jnp.dot((_silu(gate) * up).astype(BF16), wd_ref[0], preferred_element_type=F32)

    @pl.when(f == pl.num_programs(1) - 1)
    def _():
        @pl.when(valid)
        def _():
            ys_ref[...] = acc_ref[...].astype(ys_ref.dtype)

        @pl.when(jnp.logical_not(valid))
        def _():
            ys_ref[...] = jnp.zeros_like(ys_ref)


def _combine_kernel(seg_ref, ys_ref, mi_ref, mp_ref, x_ref, gt_ref, gpost_ref, o_ref, win_ref, sem):
    tm = x_ref.shape[1]
    i = pl.program_id(0) * pl.num_programs(1) + pl.program_id(1)

    def window_copy(e):
        src = pl.multiple_of(seg_ref[i * N_EXPERTS + e], SEG_ALIGN)
        return pltpu.make_async_copy(ys_ref.at[pl.ds(src, tm), :], win_ref.at[pl.ds(e * tm, tm), :],
                                     sem.at[e])

    for e in range(N_EXPERTS):
        window_copy(e).start()
    e1, e2 = mi_ref[:, 0:1], mi_ref[:, 1:2]
    r1, r2 = mi_ref[:, 2:3], mi_ref[:, 3:4]
    col = lax.broadcasted_iota(jnp.int32, (tm, tm), 1)
    acc = None
    for e in range(N_EXPERTS):
        expand = jnp.concatenate([jnp.logical_and(e1 == e, r1 == col),
                                  jnp.logical_and(e2 == e, r2 == col)], axis=0).astype(BF16)
        window_copy(e).wait()
        t = jnp.dot(expand, win_ref[pl.ds(e * tm, tm), :], preferred_element_type=F32)
        acc = t if acc is None else acc + t
    y = mp_ref[:, 0:1] * acc[0:tm] + mp_ref[:, 1:2] * acc[tm:2 * tm]
    y = y * lax.rsqrt(jnp.mean(y * y, axis=-1, keepdims=True) + RMS_EPS) * gpost_ref[...]
    o_ref[0] = x_ref[0] + gt_ref[0] * y


def _moe_ffn(x, sc, sh, gt, gpre, gpost, rw, rb, wg, wu, wd, tf):
    B, T, D = x.shape
    E, _, F = wg.shape
    tm = min(MOE_TOKEN_TILE, T)
    assert tm == MOE_TOKEN_TILE and E == N_EXPERTS
    nT = T // tm
    n_tok_tiles = B * nT
    N = B * T
    max_rows = 2 * N + n_tok_tiles * E * (SEG_ALIGN - 1) + E * (MOE_ROW_TILE - SEG_ALIGN)
    n_row_tiles = -(-max_rows // MOE_ROW_TILE) + 1
    P = n_row_tiles * MOE_ROW_TILE

    vec = pl.BlockSpec((1, 1, D), lambda b, j: (b, 0, 0))
    const2 = lambda arr: pl.BlockSpec(arr.shape, lambda b, j: (0, 0))
    flat = lambda n: pl.BlockSpec((tm, n), lambda b, j: (b * nT + j, 0))
    h, mi, mp, cnt = pl.pallas_call(
        _route_kernel,
        out_shape=(jax.ShapeDtypeStruct((N, D), BF16), jax.ShapeDtypeStruct((N, 8), jnp.int32),
                   jax.ShapeDtypeStruct((N, 8), F32), jax.ShapeDtypeStruct((n_tok_tiles, 1, LANES), jnp.int32)),
        grid=(B, nT),
        in_specs=[pl.BlockSpec((1, tm, D), lambda b, j: (b, j, 0)), vec, vec, const2(gpre), const2(rw),
                  const2(rb)],
        out_specs=(flat(D), flat(8), flat(8), pl.BlockSpec((1, 1, LANES), lambda b, j: (b * nT + j, 0, 0))),
        compiler_params=_cparams(("arbitrary", "arbitrary")),
        name="moe_route",
    )(x, sc, sh, gpre, rw, rb)

    counts = cnt[:, 0, :E]
    seg_len = (counts + SEG_ALIGN - 1) // SEG_ALIGN * SEG_ALIGN
    group_len = (jnp.sum(seg_len, axis=0) + MOE_ROW_TILE - 1) // MOE_ROW_TILE * MOE_ROW_TILE
    group_end = jnp.cumsum(group_len)
    seg_start = (group_end - group_len)[None, :] + jnp.cumsum(seg_len, axis=0) - seg_len
    seg_start = seg_start.reshape(-1).astype(jnp.int32)
    counts = counts.reshape(-1)
    n_valid = (group_end[-1:] // MOE_ROW_TILE).astype(jnp.int32)
    tile_first_row = jnp.arange(n_row_tiles, dtype=jnp.int32) * MOE_ROW_TILE
    tile_expert = jnp.minimum(jnp.sum(tile_first_row[:, None] >= group_end[None, :], axis=1), E - 1)
    tile_expert = tile_expert.astype(jnp.int32)

    xs = pl.pallas_call(
        _dispatch_kernel,
        out_shape=jax.ShapeDtypeStruct((P, D), BF16),
        grid_spec=pltpu.PrefetchScalarGridSpec(
            num_scalar_prefetch=2,
            grid=(n_tok_tiles,),
            in_specs=[pl.BlockSpec((tm, D), lambda i, seg, n: (i, 0)),
                      pl.BlockSpec((8, tm), lambda i, seg, n: (0, i)),
                      pl.BlockSpec(memory_space=pl.ANY)],
            out_specs=pl.BlockSpec(memory_space=pl.ANY),
            scratch_shapes=[pltpu.VMEM((E * tm, D), BF16), pltpu.SemaphoreType.DMA(())]),
        input_output_aliases={4: 0},
        compiler_params=_cparams(("arbitrary",)),
        name="moe_dispatch",
    )(seg_start, counts, h, mi.T, jnp.zeros((P, D), BF16))

    nf = F // tf
    live = lambda r, f, nv: jnp.where(r < nv[0], f, nf - 1)
    ys = pl.pallas_call(
        _expert_kernel,
        out_shape=jax.ShapeDtypeStruct((P, D), BF16),
        grid_spec=pltpu.PrefetchScalarGridSpec(
            num_scalar_prefetch=2,
            grid=(n_row_tiles, nf),
            in_specs=[pl.BlockSpec((MOE_ROW_TILE, D), lambda r, f, te, nv: (r, 0)),
                      pl.BlockSpec((1, D, tf), lambda r, f, te, nv: (te[r], 0, live(r, f, nv))),
                      pl.BlockSpec((1, D, tf), lambda r, f, te, nv: (te[r], 0, live(r, f, nv))),
                      pl.BlockSpec((1, tf, D), lambda r, f, te, nv: (te[r], live(r, f, nv), 0))],
            out_specs=pl.BlockSpec((MOE_ROW_TILE, D), lambda r, f, te, nv: (r, 0)),
            scratch_shapes=[pltpu.VMEM((MOE_ROW_TILE, D), F32)]),
        compiler_params=_cparams(("arbitrary", "arbitrary")),
        name="moe_experts",
    )(tile_expert, n_valid, xs, wg, wu, wd)

    return pl.pallas_call(
        _combine_kernel,
        out_shape=jax.ShapeDtypeStruct((B, T, D), F32),
        grid_spec=pltpu.PrefetchScalarGridSpec(
            num_scalar_prefetch=1,
            grid=(B, nT),
            in_specs=[pl.BlockSpec(memory_space=pl.ANY),
                      pl.BlockSpec((tm, 8), lambda b, j, seg: (b * nT + j, 0)),
                      pl.BlockSpec((tm, 8), lambda b, j, seg: (b * nT + j, 0)),
                      pl.BlockSpec((1, tm, D), lambda b, j, seg: (b, j, 0)),
                      pl.BlockSpec((1, 1, D), lambda b, j, seg: (b, 0, 0)),
                      pl.BlockSpec(gpost.shape, lambda b, j, seg: (0, 0))],
            out_specs=pl.BlockSpec((1, tm, D), lambda b, j, seg: (b, j, 0)),
            scratch_shapes=[pltpu.VMEM((E * tm, D), BF16), pltpu.SemaphoreType.DMA((E,))]),
        compiler_params=_cparams(("arbitrary", "arbitrary")),
        name="moe_combine",
    )(seg_start, ys, mi, mp, x, gt, gpost)


def _rope_tables(positions, groups):
    d = 32
    inv = 1.0 / (ROPE_THETA ** (jnp.arange(0, d, 2, dtype=F32) / d))
    ang = positions.astype(F32)[..., None] * inv
    cos, sin = jnp.cos(ang), jnp.sin(ang)
    cos = jnp.tile(jnp.concatenate([cos, cos], axis=-1), (1, 1, groups))
    sin = jnp.tile(jnp.concatenate([-sin, sin], axis=-1), (1, 1, groups))
    return cos, sin


def _pad_rows(w, rows, offset):
    out = jnp.zeros((rows, w.shape[1]), w.dtype)
    return out.at[offset:offset + w.shape[0]].set(w)


def kernel(x, c, positions, ada_w, ada_b, norm_mix_pre, norm_mix_post, norm_ffn_pre, norm_ffn_post, w_in, gla_gate_w2, gla_gate_b, gla_norm, diff_lambda, diff_subln, conv_w, conv_b, conv_ln_g, conv_ln_b, rwkv_mu, rwkv_w0, rwkv_w2, rwkv_a0, rwkv_a2, rwkv_g2, rwkv_k_k, rwkv_k_a, rwkv_r_k, rwkv_ln_g, rwkv_ln_b, w_branch, w_out, ffn_w_gate, ffn_w_up, ffn_w_down, router_w, router_b, moe_w_gate, moe_w_up, moe_w_down):
    B, T, D = x.shape
    L = ada_w.shape[0]
    W = D // N_BRANCH
    hk = gla_gate_b.shape[1]
    decay_rank = rwkv_w2.shape[1]
    a_rank = rwkv_a2.shape[1]
    gate_rank = rwkv_g2.shape[1]
    assert decay_rank + a_rank == LANES and 2 * hk == W and gate_rank == LANES
    n_mix = 3 * W + 3 * W + 2 * W + (3 * W + decay_rank + a_rank + gate_rank)
    sizes = (hk, hk, W, W, GLA_GATE_RANK, W, W, W, W, W, 3 * W + LANES + gate_rank, N_BRANCH * D)
    offs = [0]
    for s in sizes:
        offs.append(offs[-1] + s)
    assert offs[-1] == w_in.shape[2]
    tm = min(512, T)
    tb = min(512, T)

    mod = _modulation(c, ada_w, ada_b)
    cos, sin = _rope_tables(positions, W // 32)

    for l in range(L):
        m = mod[l].reshape(B, 1, 6 * D)
        sh_m, sc_m, gt_m, sh_f, sc_f, gt_f = [m[:, :, i * D:(i + 1) * D] for i in range(6)]

        wl = w_in[l]
        gz_cols = jnp.zeros((D, LANES), F32).at[:, :GLA_GATE_RANK].set(wl[:, offs[4]:offs[5]])
        w_mix = jnp.concatenate([wl[:, offs[0]:offs[4]], wl[:, offs[5]:offs[11]], gz_cols], axis=1)
        w_mix = w_mix.astype(BF16)
        w_gate = wl[:, offs[11]:offs[12]].astype(BF16)
        proj = _in_projection(x, sc_m, sh_m, norm_mix_pre[l][None], w_mix, tm)

        w2p = _pad_rows(gla_gate_w2[l], LANES, 0)
        o_gla = _gla(proj, w2p, gla_gate_b[l][None], jnp.tile(gla_norm[l], GLA_HEADS)[None],
                     col_qk=0, col_v=1, col_og=2, col_gz=(n_mix // LANES), tb=tb)
        o_diff = _diff_attention(proj, cos, sin, diff_lambda[l],
                                 diff_subln[l][None], col_q=3, col_k=4, col_v=5, layer_idx=l)
        o_conv = _conformer_conv(proj, _pad_rows(conv_w[l], 32, 0), conv_b[l][None],
                                 conv_ln_g[l][None], conv_ln_b[l][None], col_a=6, col_b=7)
        vecs = dict(mu=rwkv_mu[l][None], w0=rwkv_w0[l][None], a0=rwkv_a0[l][None],
                    kk=rwkv_k_k[l][None], ka=rwkv_k_a[l][None], rk=rwkv_r_k[l].reshape(1, W),
                    lng=rwkv_ln_g[l][None], lnb=rwkv_ln_b[l][None])
        o_rwkv = _rwkv7(proj, vecs, _pad_rows(rwkv_w2[l], LANES, 0).astype(BF16),
                        _pad_rows(rwkv_a2[l], LANES, decay_rank).astype(BF16),
                        rwkv_g2[l].astype(BF16), col=2, tb=tb)
        x = _merge(x, sc_m, sh_m, gt_m, norm_mix_pre[l][None], norm_mix_post[l][None], w_gate,
                   (o_gla, o_diff, o_conv, o_rwkv), w_branch[l].astype(BF16), w_out[l].astype(BF16), tm)

        i = l // 2
        if l % 2 == 0:
            x = _dense_ffn(x, sc_f, sh_f, gt_f, norm_ffn_pre[l][None], norm_ffn_post[l][None],
                           ffn_w_gate[i].astype(BF16), ffn_w_up[i].astype(BF16),
                           ffn_w_down[i].astype(BF16), tm)
        else:
            rw = jnp.zeros((D, LANES), F32).at[:, :N_EXPERTS].set(router_w[i])
            rb = jnp.full((1, LANES), -jnp.inf, F32).at[0, :N_EXPERTS].set(router_b[i])
            x = _moe_ffn(x, sc_f, sh_f, gt_f, norm_ffn_pre[l][None], norm_ffn_post[l][None], rw, rb,
                         moe_w_gate[i].astype(BF16), moe_w_up[i].astype(BF16),
                         moe_w_down[i].astype(BF16), tf=512)
    return x
```

```python
import functools
import math

import jax
import jax.numpy as jnp
from jax import lax
from jax.experimental import pallas as pl
from jax.experimental.pallas import tpu as pltpu

F32 = jnp.float32
BF16 = jnp.bfloat16
HIGHEST = lax.Precision.HIGHEST

N_BRANCH = 4
GLA_HEADS = 4
GLA_GATE_RANK = 16
GLA_GATE_NORMALIZER = 16.0
GLA_CHUNK = 32
DIFF_HEADS = 4
ROPE_THETA = 10000.0
CONV_WIDTH = 31
RWKV_HEADS = 4
RWKV_CHUNK = 64
RWKV_SUB = 16
N_EXPERTS = 8
RMS_EPS = 1e-6
LN_EPS = 1e-5
RWKV_GN_EPS = 64e-5
LANES = 128
VMEM_LIMIT = 56 * 1024 * 1024


def _cparams(sem):
    return pltpu.CompilerParams(dimension_semantics=sem, vmem_limit_bytes=VMEM_LIMIT)


def _mm(a, b):
    return jnp.dot(a.astype(BF16), b.astype(BF16), preferred_element_type=F32)


def _mm_nt(a, b):
    return lax.dot_general(a.astype(BF16), b.astype(BF16), (((1,), (1,)), ((), ())),
                           preferred_element_type=F32)


def _mm_tn(a, b):
    return lax.dot_general(a.astype(BF16), b.astype(BF16), (((0,), (0,)), ((), ())),
                           preferred_element_type=F32)


def _bmm(a, b):
    return lax.dot_general(a.astype(BF16), b.astype(BF16), (((2,), (1,)), ((0,), (0,))),
                           preferred_element_type=F32)


def _mm_f32(a, b):
    return jnp.dot(a, b, precision=HIGHEST, preferred_element_type=F32)


def _hi_lo(x):
    hi = x.astype(BF16)
    return jnp.concatenate([hi, (x - hi.astype(F32)).astype(BF16)], axis=1)


def _sigmoid(x):
    return 1.0 / (1.0 + jnp.exp(-x))


def _silu(x):
    return x * _sigmoid(x)


def _softplus(x):
    return jnp.maximum(x, 0.0) + jnp.log(1.0 + jnp.exp(-jnp.abs(x)))


def _group_matrix(n, group):
    r = lax.broadcasted_iota(jnp.int32, (n, n), 0) // group
    c = lax.broadcasted_iota(jnp.int32, (n, n), 1) // group
    return r == c


def _rms_mod(x, gain, scale, shift):
    y = x * lax.rsqrt(jnp.mean(x * x, axis=-1, keepdims=True) + RMS_EPS)
    return y * gain * (1.0 + scale) + shift


def _mod_kernel(c_ref, w_ref, b_ref, o_ref):
    o_ref[0] = _mm_f32(_silu(c_ref[...]), w_ref[0]) + b_ref[0]


def _modulation(c, ada_w, ada_b):
    L, D, M = ada_w.shape
    B = c.shape[0]
    tn = M // 4
    return pl.pallas_call(
        _mod_kernel,
        out_shape=jax.ShapeDtypeStruct((L, B, M), F32),
        grid=(L, M // tn),
        in_specs=[pl.BlockSpec((B, D), lambda l, j: (0, 0)),
                  pl.BlockSpec((1, D, tn), lambda l, j: (l, 0, j)),
                  pl.BlockSpec((1, 1, tn), lambda l, j: (l, 0, j))],
        out_specs=pl.BlockSpec((1, B, tn), lambda l, j: (l, 0, j)),
        compiler_params=_cparams(("arbitrary", "arbitrary")),
        name="adaln_mod",
    )(c, ada_w, ada_b.reshape(L, 1, M))


def _inproj_kernel(x_ref, sc_ref, sh_ref, g_ref, w_ref, o_ref):
    h = _rms_mod(x_ref[0], g_ref[...], sc_ref[0], sh_ref[0])
    o_ref[0] = _mm(h, w_ref[...])


def _in_projection(x, sc, sh, gain, w, tm):
    B, T, D = x.shape
    n = w.shape[1]
    return pl.pallas_call(
        _inproj_kernel,
        out_shape=jax.ShapeDtypeStruct((B, T, n), F32),
        grid=(B, T // tm),
        in_specs=[pl.BlockSpec((1, tm, D), lambda b, i: (b, i, 0)),
                  pl.BlockSpec((1, 1, D), lambda b, i: (b, 0, 0)),
                  pl.BlockSpec((1, 1, D), lambda b, i: (b, 0, 0)),
                  pl.BlockSpec((1, D), lambda b, i: (0, 0)),
                  pl.BlockSpec((D, n), lambda b, i: (0, 0))],
        out_specs=pl.BlockSpec((1, tm, n), lambda b, i: (b, i, 0)),
        compiler_params=_cparams(("arbitrary", "arbitrary")),
        name="in_proj",
    )(x, sc, sh, gain, w)


def _gla_kernel(qk_ref, v_ref, og_ref, gz_ref, w2_ref, gb_ref, ng_ref, o_ref, s_ref, g_ref, r_ref,
                qd_ref, kd_ref, oacc_ref, *, n_chunk):
    C = GLA_CHUNK
    hk = qk_ref.shape[2] // 2
    hv = v_ref.shape[2]
    dk = hk // GLA_HEADS
    dv = hv // GLA_HEADS

    tb = n_chunk * C

    @pl.when(pl.program_id(1) == 0)
    def _():
        s_ref[...] = jnp.zeros_like(s_ref)

    z = _mm(gz_ref[0], w2_ref[...]) + gb_ref[...]
    gk = (jnp.minimum(z, 0.0) - jnp.log(1.0 + jnp.exp(-jnp.abs(z)))) / GLA_GATE_NORMALIZER
    bi = lax.broadcasted_iota(jnp.int32, (tb, tb), 0)
    bj = lax.broadcasted_iota(jnp.int32, (tb, tb), 1)
    same_chunk = (bi // C) == (bj // C)
    prefix = jnp.logical_and(same_chunk, bj <= bi)
    suffix = jnp.logical_and(same_chunk, bj > bi)
    sums = jnp.dot(jnp.concatenate([prefix, suffix], axis=0).astype(BF16), _hi_lo(gk),
                   preferred_element_type=F32)
    G_all = sums[0:tb, 0:hk] + sums[0:tb, hk:2 * hk]
    R_all = sums[tb:2 * tb, 0:hk] + sums[tb:2 * tb, hk:2 * hk]
    g_ref[...] = G_all
    r_ref[...] = R_all
    qd_ref[...] = (qk_ref[0, :, 0:hk] * (dk ** -0.5) * jnp.exp(G_all)).astype(BF16)
    kd_ref[...] = (qk_ref[0, :, hk:2 * hk] * jnp.exp(R_all)).astype(BF16)

    pr = lax.broadcasted_iota(jnp.int32, (C * C, C), 0)
    pc = lax.broadcasted_iota(jnp.int32, (C * C, C), 1)
    row_sel = (pr // C == pc).astype(BF16)
    pi = lax.broadcasted_iota(jnp.int32, (C * C, 1), 0)
    causal = (pi % C) <= (pi // C)
    er = lax.broadcasted_iota(jnp.int32, (hk, hv), 0) // dk
    ec = lax.broadcasted_iota(jnp.int32, (hk, hv), 1) // dv
    expand = (er == ec).astype(BF16)
    sr = lax.broadcasted_iota(jnp.int32, (hv, hk), 0) // dv
    scol = lax.broadcasted_iota(jnp.int32, (hv, hk), 1) // dk
    state_mask = sr == scol
    head_mean = (_group_matrix(hv, dv).astype(F32) / dv).astype(BF16)

    def chunk(ci, carry):
        r0 = pl.multiple_of(ci * C, C)
        rows = pl.ds(r0, C)
        q = qk_ref[0, rows, 0:hk] * (dk ** -0.5)
        k = qk_ref[0, rows, hk:2 * hk]
        v = v_ref[0, rows, :]
        G = g_ref[rows, :]
        sel = jnp.dot(row_sel, jnp.concatenate([_hi_lo(G), q.astype(BF16)], axis=1),
                      preferred_element_type=F32)
        G_i = sel[:, 0:hk] + sel[:, hk:2 * hk]
        q_i = sel[:, 2 * hk:3 * hk]
        G_j = jnp.broadcast_to(G[None], (C, C, hk)).reshape(C * C, hk)
        k_j = jnp.broadcast_to(k[None], (C, C, hk)).reshape(C * C, hk)
        decay = jnp.exp(jnp.where(causal, G_i - G_j, -jnp.inf))
        p = q_i * k_j * decay
        a_exp = jnp.dot(p.astype(BF16), expand, preferred_element_type=F32)
        v_j = jnp.broadcast_to(v[None], (C, C, hv))
        o_intra = jnp.sum(a_exp.reshape(C, C, hv) * v_j, axis=1)
        g_total = G[0:1, :] + r_ref[pl.ds(r0, 8), :][0:1, :]
        s = s_ref[...]
        o_inter = lax.dot_general(qd_ref[rows, :], s.astype(BF16), (((1,), (1,)), ((), ())),
                                  preferred_element_type=F32)
        kv = lax.dot_general(v.astype(BF16), kd_ref[rows, :], (((0,), (0,)), ((), ())),
                             preferred_element_type=F32)
        s_ref[...] = s * jnp.exp(g_total) + jnp.where(state_mask, kv, 0.0)
        oacc_ref[rows, :] = o_intra + o_inter
        return carry

    lax.fori_loop(0, n_chunk, chunk, 0)

    o = oacc_ref[...]
    ms = jnp.dot(_hi_lo(o * o), jnp.concatenate([head_mean, head_mean], axis=0),
                 preferred_element_type=F32)
    o = o * lax.rsqrt(ms + RMS_EPS) * ng_ref[...] * _silu(og_ref[0])
    o_ref[0] = o.astype(o_ref.dtype)


def _gla(proj, w2p, gb, ng, col_qk, col_v, col_og, col_gz, tb):
    B, T, _ = proj.shape
    hv = ng.shape[1]
    hk = gb.shape[1]
    return pl.pallas_call(
        functools.partial(_gla_kernel, n_chunk=tb // GLA_CHUNK),
        out_shape=jax.ShapeDtypeStruct((B, T, hv), BF16),
        grid=(B, T // tb),
        in_specs=[pl.BlockSpec((1, tb, 2 * hk), lambda b, j: (b, j, col_qk)),
                  pl.BlockSpec((1, tb, hv), lambda b, j: (b, j, col_v)),
                  pl.BlockSpec((1, tb, hv), lambda b, j: (b, j, col_og)),
                  pl.BlockSpec((1, tb, LANES), lambda b, j: (b, j, col_gz)),
                  pl.BlockSpec(w2p.shape, lambda b, j: (0, 0)),
                  pl.BlockSpec(gb.shape, lambda b, j: (0, 0)),
                  pl.BlockSpec(ng.shape, lambda b, j: (0, 0))],
        out_specs=pl.BlockSpec((1, tb, hv), lambda b, j: (b, j, 0)),
        scratch_shapes=[pltpu.VMEM((hv, hk), F32), pltpu.VMEM((tb, hk), F32), pltpu.VMEM((tb, hk), F32),
                        pltpu.VMEM((tb, hk), BF16), pltpu.VMEM((tb, hk), BF16), pltpu.VMEM((tb, hv), F32)],
        compiler_params=_cparams(("arbitrary", "arbitrary")),
        name="gla_mixer",
    )(proj, proj, proj, proj, w2p, gb, ng)


def _rope(t, cos, sin_signed):
    d = 32
    half = d // 2
    out = []
    for s in range(t.shape[1] // LANES):
        x = t[:, s * LANES:(s + 1) * LANES]
        lane = lax.broadcasted_iota(jnp.int32, x.shape, 1)
        up = pltpu.roll(x, LANES - half, 1)
        down = pltpu.roll(x, half, 1)
        rot = jnp.where((lane % d) < half, up, down)
        out.append(x * cos[:, s * LANES:(s + 1) * LANES] + rot * sin_signed[:, s * LANES:(s + 1) * LANES])
    return jnp.concatenate(out, axis=1)


def _diff_kernel(q_ref, k_ref, v_ref, cosq_ref, sinq_ref, cosk_ref, sink_ref, lam_ref, g_ref, o_ref,
                 ks, vs, *, tq, lam_init):
    H = DIFF_HEADS
    d = q_ref.shape[2] // (2 * H)
    dv = v_ref.shape[2] // H
    j = pl.program_id(1)

    @pl.when(j == 0)
    def _():
        k = _rope(k_ref[0], cosk_ref[0], sink_ref[0])
        v = v_ref[0]
        for hc in range(2 * H):
            ks[hc] = k[:, hc * d:(hc + 1) * d].astype(BF16)
        ones_col = (lax.broadcasted_iota(jnp.int32, (v.shape[0], dv), 1) == 0).astype(BF16)
        for h in range(H):
            vs[h] = jnp.concatenate([v[:, h * dv:(h + 1) * dv].astype(BF16), ones_col], axis=1)

    q = _rope(q_ref[0], cosq_ref[0], sinq_ref[0]) * (d ** -0.5)
    lp = lam_ref[...]
    lam = (jnp.exp(jnp.sum(lp[0:1] * lp[1:2], axis=-1, keepdims=True))
           - jnp.exp(jnp.sum(lp[2:3] * lp[3:4], axis=-1, keepdims=True)) + lam_init)
    on_or_below_diag = (lax.broadcasted_iota(jnp.int32, (tq, tq), 1)
                        <= lax.broadcasted_iota(jnp.int32, (tq, tq), 0))

    def update(qh, kh, vh, m, acc, masked):
        s = lax.dot_general(qh, kh, (((1,), (1,)), ((), ())), preferred_element_type=F32)
        if masked:
            s = jnp.where(on_or_below_diag, s, -jnp.inf)
        m_new = jnp.maximum(m, jnp.max(s, axis=-1, keepdims=True))
        p = jnp.exp(s - m_new).astype(BF16)
        acc = jnp.exp(m - m_new) * acc + jnp.dot(p, vh, preferred_element_type=F32)
        return m_new, acc

    def two_softmax_pv(h):
        q1 = q[:, (2 * h) * d:(2 * h + 1) * d].astype(BF16)
        q2 = q[:, (2 * h + 1) * d:(2 * h + 2) * d].astype(BF16)

        def kv_step(kb, carry):
            m1, a1, m2, a2 = carry
            rows = pl.ds(pl.multiple_of(kb * tq, tq), tq)
            vh = vs[h, rows, :]
            m1, a1 = update(q1, ks[2 * h, rows, :], vh, m1, a1, False)
            m2, a2 = update(q2, ks[2 * h + 1, rows, :], vh, m2, a2, False)
            return m1, a1, m2, a2

        m0 = jnp.full((tq, 1), -jnp.inf, F32)
        a0 = jnp.zeros((tq, 2 * dv), F32)
        m1, a1, m2, a2 = lax.fori_loop(0, j, kv_step, (m0, a0, m0, a0))
        rows = pl.ds(pl.multiple_of(j * tq, tq), tq)
        vh = vs[h, rows, :]
        _, a1 = update(q1, ks[2 * h, rows, :], vh, m1, a1, True)
        _, a2 = update(q2, ks[2 * h + 1, rows, :], vh, m2, a2, True)
        return a1, a2

    for h in range(H):
        a1, a2 = two_softmax_pv(h)
        comp = [a[:, 0:dv] / a[:, dv:dv + 1] for a in (a1, a2)]
        o = comp[0] - lam * comp[1]
        o = o * lax.rsqrt(jnp.mean(o * o, axis=-1, keepdims=True) + RMS_EPS)
        o = o * g_ref[...] * (1.0 - lam_init)
        o_ref[0, :, h * dv:(h + 1) * dv] = o.astype(o_ref.dtype)


def _diff_attention(proj, cos, sin, lam_p, g, col_q, col_k, col_v, layer_idx):
    B, T, _ = proj.shape
    H = DIFF_HEADS
    w = cos.shape[2]
    d = w // (2 * H)
    dv = 2 * d
    tq = min(512, T)
    lam_init = 0.8 - 0.6 * math.exp(-0.3 * layer_idx)
    blk = lambda col: pl.BlockSpec((1, tq, w), lambda b, j: (b, j, col))
    full = lambda col: pl.BlockSpec((1, T, w), lambda b, j: (b, 0, col))
    return pl.pallas_call(
        functools.partial(_diff_kernel, tq=tq, lam_init=lam_init),
        out_shape=jax.ShapeDtypeStruct((B, T, H * dv), BF16),
        grid=(B, T // tq),
        in_specs=[blk(col_q), full(col_k), full(col_v), blk(0), blk(0), full(0), full(0),
                  pl.BlockSpec(lam_p.shape, lambda b, j: (0, 0)),
                  pl.BlockSpec(g.shape, lambda b, j: (0, 0))],
        out_specs=pl.BlockSpec((1, tq, H * dv), lambda b, j: (b, j, 0)),
        scratch_shapes=[pltpu.VMEM((2 * H, T, d), BF16),
                        pltpu.VMEM((H, T, 2 * dv), BF16)],
        compiler_params=_cparams(("arbitrary", "arbitrary")),
        name="diff_attention",
    )(proj, proj, proj, cos, sin, cos, sin, lam_p, g)


def _conv_kernel(a_ref, b_ref, w_ref, cb_ref, lg_ref, lb_ref, o_ref, u_ref, *, rb):
    T = a_ref.shape[1]
    pad = u_ref.shape[0] - T
    u_ref[0:pad, :] = jnp.zeros((pad, u_ref.shape[1]), F32)
    u_ref[pad:pad + T, :] = a_ref[0] * _sigmoid(b_ref[0])
    first = pad - (CONV_WIDTH - 1)

    def block(i, carry):
        r0 = pl.multiple_of(i * rb, rb)
        win = u_ref[pl.ds(r0, rb + pad), :]
        acc = jnp.zeros((rb, u_ref.shape[1]), F32)
        for j in range(CONV_WIDTH):
            acc = acc + w_ref[j:j + 1, :] * win[first + j:first + j + rb, :]
        y = acc + cb_ref[...]
        mu = jnp.mean(y, axis=-1, keepdims=True)
        yc = y - mu
        var = jnp.mean(yc * yc, axis=-1, keepdims=True)
        y = yc * lax.rsqrt(var + LN_EPS) * lg_ref[...] + lb_ref[...]
        o_ref[0, pl.ds(r0, rb), :] = _silu(y).astype(o_ref.dtype)
        return carry

    lax.fori_loop(0, T // rb, block, 0)


def _conformer_conv(proj, w, cb, lg, lb, col_a, col_b):
    B, T, _ = proj.shape
    ch = w.shape[1]
    rb = min(128, T)
    return pl.pallas_call(
        functools.partial(_conv_kernel, rb=rb),
        out_shape=jax.ShapeDtypeStruct((B, T, ch), BF16),
        grid=(B,),
        in_specs=[pl.BlockSpec((1, T, ch), lambda b: (b, 0, col_a)),
                  pl.BlockSpec((1, T, ch), lambda b: (b, 0, col_b)),
                  pl.BlockSpec(w.shape, lambda b: (0, 0)),
                  pl.BlockSpec(cb.shape, lambda b: (0, 0)),
                  pl.BlockSpec(lg.shape, lambda b: (0, 0)),
                  pl.BlockSpec(lb.shape, lambda b: (0, 0))],
        out_specs=pl.BlockSpec((1, T, ch), lambda b: (b, 0, 0)),
        scratch_shapes=[pltpu.VMEM((T + 32, ch), F32)],
        compiler_params=_cparams(("arbitrary",)),
        name="conformer_conv",
    )(proj, proj, w, cb, lg, lb)


def _rwkv_kernel(x_ref, mu_ref, w0_ref, w2_ref, a0_ref, a2_ref, g2_ref, kk_ref, ka_ref, rk_ref,
                 lng_ref, lnb_ref, o_ref, s_ref, prev_ref, gate_ref, bonus_ref, dec_ref, qeff_ref, yloc_ref,
                 mlow_ref, nc_ref, y_ref, *, n_chunk):
    C = RWKV_CHUNK
    H = RWKV_HEADS
    W = o_ref.shape[2]
    N = W // H

    @pl.when(pl.program_id(1) == 0)
    def _():
        s_ref[...] = jnp.zeros_like(s_ref)
        prev_ref[...] = jnp.zeros_like(prev_ref)

    lane = lax.broadcasted_iota(jnp.int32, (1, W), 1)
    head_mask = [(lane // N == h).astype(F32) for h in range(H)]
    block_diag = _group_matrix(W, N)
    head_sum = block_diag.astype(F32)
    ti = lax.broadcasted_iota(jnp.int32, (C, C), 0)
    tj = lax.broadcasted_iota(jnp.int32, (C, C), 1)
    tril_incl = ti >= tj
    tril_strict = ti > tj
    same_sub = (ti // RWKV_SUB) == (tj // RWKV_SUB)
    eye = (ti == tj).astype(F32)
    nc = n_chunk
    tb = nc * C

    def head_total(t):
        ones = head_sum.astype(BF16)
        return jnp.dot(_hi_lo(t), jnp.concatenate([ones, ones], axis=0), preferred_element_type=F32)

    x = x_ref[0]
    first_row = lax.broadcasted_iota(jnp.int32, (tb, 1), 0) == 0
    prev = jnp.where(first_row, prev_ref[...], pltpu.roll(x, 1, 0))
    prev_ref[...] = x[tb - 1:tb, :]
    xm = x + (prev - x) * mu_ref[...]
    r = xm[:, 0:W]
    k = xm[:, W:2 * W]
    v = xm[:, 2 * W:3 * W]
    zz = xm[:, 3 * W:3 * W + LANES]
    zg = xm[:, 3 * W + LANES:]
    w = -_softplus(-(w0_ref[...] + _mm(jnp.tanh(zz), w2_ref[...]))) - 0.5
    lw = -jnp.exp(w)
    a = _sigmoid(a0_ref[...] + _mm(zz, a2_ref[...]))
    gate_ref[...] = _mm(_sigmoid(zg), g2_ref[...])
    kk = k * kk_ref[...]
    kk = kk / jnp.maximum(jnp.sqrt(head_total(kk * kk)), 1e-12)
    k = k * (1.0 + (a - 1.0) * ka_ref[...])
    b = kk * a
    bonus_ref[...] = head_total(r * k * rk_ref[...]) * v

    bi = lax.broadcasted_iota(jnp.int32, (tb, tb), 0)
    bj = lax.broadcasted_iota(jnp.int32, (tb, tb), 1)
    same_chunk = (bi // C) == (bj // C)
    sums = jnp.dot(jnp.concatenate([jnp.logical_and(same_chunk, bj <= bi),
                                    jnp.logical_and(same_chunk, bj > bi)], axis=0).astype(BF16),
                   _hi_lo(lw), preferred_element_type=F32)
    G = sums[0:tb, 0:W] + sums[0:tb, W:2 * W]
    R = sums[tb:2 * tb, 0:W] + sums[tb:2 * tb, W:2 * W]
    dec_ref[...] = jnp.exp(G + R)
    inv = jnp.exp(-G)
    to_end = jnp.exp(R)
    c3 = lambda t: t.reshape(nc, C, t.shape[1])
    kap = kk * jnp.exp(G - lw)
    rho = r * jnp.exp(G)
    kap3, rho3, v3 = c3(kap), c3(rho), c3(v)
    bet3, kt3 = c3(b * inv), c3(k * inv)
    betc3, kc3 = c3(b * to_end), c3(k * to_end)

    def bmm_nt(p, q):
        return lax.dot_general(p.astype(BF16), q.astype(BF16), (((2,), (2,)), ((0,), (0,))),
                               preferred_element_type=F32)

    def bmm_tn(p, q):
        return lax.dot_general(p.astype(BF16), q.astype(BF16), (((1,), (1,)), ((0,), (0,))),
                               preferred_element_type=F32)

    A_b, A_k, B_b, B_k = [], [], [], []
    for h in range(H):
        lhs = jnp.concatenate([c3(kap * head_mask[h]), c3(rho * head_mask[h])], axis=1)
        pb = bmm_nt(lhs, bet3)
        pk = bmm_nt(lhs, kt3)
        A_b.append(jnp.where(tril_strict, pb[:, 0:C], 0.0))
        A_k.append(jnp.where(tril_strict, pk[:, 0:C], 0.0))
        B_b.append(jnp.where(tril_incl, pb[:, C:2 * C], 0.0))
        B_k.append(jnp.where(tril_incl, pk[:, C:2 * C], 0.0))
    A_b = jnp.concatenate(A_b, axis=0)

    Dg = jnp.where(same_sub, A_b, 0.0)
    Lo = A_b - Dg
    D2 = _bmm(Dg, Dg)
    D4 = _bmm(D2, D2)
    D8 = _bmm(D4, D4)
    Dinv = _bmm(_bmm(_bmm(eye - Dg, eye + D2), eye + D4), eye + D8)
    Nn = _bmm(Dinv, Lo)
    N2 = _bmm(Nn, Nn)
    Tm = _bmm(_bmm(eye - Nn, eye + N2), Dinv)
    Tm = [Tm[h * nc:(h + 1) * nc] for h in range(H)]

    def per_head(mats, t):
        reps = t.shape[2] // W
        acc = None
        for h in range(H):
            m = head_mask[h] if reps == 1 else jnp.concatenate([head_mask[h]] * reps, axis=1)
            u = m * _bmm(mats[h], t)
            acc = u if acc is None else acc + u
        return acc

    akv = per_head(A_k, v3)
    tk = per_head(Tm, jnp.concatenate([kap3, akv], axis=2))
    kap_p = tk[:, :, 0:W]
    v_p = tk[:, :, W:2 * W]
    bb = per_head(B_b, jnp.concatenate([kap_p, v_p], axis=2))
    qeff_ref[...] = (rho3 - bb[:, :, 0:W]).astype(BF16)
    yloc_ref[...] = per_head(B_k, v3) - bb[:, :, W:2 * W]
    mlow_ref[...] = jnp.where(block_diag, bmm_tn(kap_p, betc3), 0.0).astype(BF16)
    nc_ref[...] = jnp.where(block_diag, bmm_tn(v3, kc3) - bmm_tn(v_p, betc3), 0.0)

    def chunk(ci, carry):
        r0 = pl.multiple_of(ci * C, C)
        s = s_ref[...]
        sb = s.astype(BF16)
        y = lax.dot_general(qeff_ref[ci], sb, (((1,), (1,)), ((), ())), preferred_element_type=F32)
        y_ref[pl.ds(r0, C), :] = y + yloc_ref[ci]
        s_ref[...] = (s * dec_ref[pl.ds(r0, 8), :][0:1, :]
                      - jnp.dot(sb, mlow_ref[ci], preferred_element_type=F32) + nc_ref[ci])
        return carry

    lax.fori_loop(0, nc, chunk, 0)

    y = y_ref[...]
    mean = head_total(y) / N
    yc = y - mean
    var = head_total(yc * yc) / N
    yn = yc * lax.rsqrt(var + RWKV_GN_EPS) * lng_ref[...] + lnb_ref[...]
    o_ref[0] = ((yn + bonus_ref[...]) * gate_ref[...]).astype(o_ref.dtype)


def _rwkv7(proj, vecs, w2p, a2p, g2, col, tb):
    B, T, _ = proj.shape
    W = g2.shape[1]
    cols = vecs["mu"].shape[1]
    names = ("mu", "w0", "w2", "a0", "a2", "g2", "kk", "ka", "rk", "lng", "lnb")
    params = dict(vecs, w2=w2p, a2=a2p, g2=g2)
    const = lambda arr: pl.BlockSpec(arr.shape, lambda b, j: (0, 0))
    n_chunk = tb // RWKV_CHUNK
    return pl.pallas_call(
        functools.partial(_rwkv_kernel, n_chunk=n_chunk),
        out_shape=jax.ShapeDtypeStruct((B, T, W), BF16),
        grid=(B, T // tb),
        in_specs=[pl.BlockSpec((1, tb, cols), lambda b, j: (b, j, col))]
                 + [const(params[n]) for n in names],
        out_specs=pl.BlockSpec((1, tb, W), lambda b, j: (b, j, 0)),
        scratch_shapes=[pltpu.VMEM((W, W), F32), pltpu.VMEM((1, cols), F32),
                        pltpu.VMEM((tb, W), F32), pltpu.VMEM((tb, W), F32), pltpu.VMEM((tb, W), F32),
                        pltpu.VMEM((n_chunk, RWKV_CHUNK, W), BF16), pltpu.VMEM((n_chunk, RWKV_CHUNK, W), F32),
                        pltpu.VMEM((n_chunk, W, W), BF16), pltpu.VMEM((n_chunk, W, W), F32),
                        pltpu.VMEM((tb, W), F32)],
        compiler_params=_cparams(("arbitrary", "arbitrary")),
        name="rwkv7_mixer",
    )(proj, *[params[n] for n in names])


def _merge_kernel(x_ref, sc_ref, sh_ref, gt_ref, gpre_ref, gpost_ref, wg_ref, b0_ref, b1_ref, b2_ref,
                  b3_ref, wb_ref, wo_ref, o_ref):
    x = x_ref[0]
    D = x.shape[1]
    h = _rms_mod(x, gpre_ref[...], sc_ref[0], sh_ref[0]).astype(BF16)
    merged = None
    for g, br in enumerate((b0_ref, b1_ref, b2_ref, b3_ref)):
        gate = _sigmoid(jnp.dot(h, wg_ref[:, g * D:(g + 1) * D], preferred_element_type=F32))
        t = gate * jnp.dot(br[0], wb_ref[g], preferred_element_type=F32)
        merged = t if merged is None else merged + t
    y = _mm(merged, wo_ref[...])
    y = y * lax.rsqrt(jnp.mean(y * y, axis=-1, keepdims=True) + RMS_EPS) * gpost_ref[...]
    o_ref[0] = x + gt_ref[0] * y


def _merge(x, sc, sh, gt, gpre, gpost, w_gate, branches, w_branch, w_out, tm):
    B, T, D = x.shape
    bw = branches[0].shape[2]
    tok = lambda n: pl.BlockSpec((1, tm, n), lambda b, i: (b, i, 0))
    vec = pl.BlockSpec((1, 1, D), lambda b, i: (b, 0, 0))
    const2 = lambda arr: pl.BlockSpec(arr.shape, lambda b, i: (0, 0))
    return pl.pallas_call(
        _merge_kernel,
        out_shape=jax.ShapeDtypeStruct((B, T, D), F32),
        grid=(B, T // tm),
        in_specs=[tok(D), vec, vec, vec, const2(gpre), const2(gpost), const2(w_gate),
                  tok(bw), tok(bw), tok(bw), tok(bw),
                  pl.BlockSpec(w_branch.shape, lambda b, i: (0, 0, 0)), const2(w_out)],
        out_specs=tok(D),
        compiler_params=_cparams(("arbitrary", "arbitrary")),
        name="merge_out_proj",
    )(x, sc, sh, gt, gpre, gpost, w_gate, *branches, w_branch, w_out)


def _ffn_kernel(x_ref, sc_ref, sh_ref, gt_ref, gpre_ref, gpost_ref, wg_ref, wu_ref, wd_ref, o_ref, *, tf):
    x = x_ref[0]
    h = _rms_mod(x, gpre_ref[...], sc_ref[0], sh_ref[0]).astype(BF16)
    F = wg_ref.shape[1]
    acc = None
    for f0 in range(0, F, tf):
        f1 = min(f0 + tf, F)
        gate = jnp.dot(h, wg_ref[:, f0:f1], preferred_element_type=F32)
        up = jnp.dot(h, wu_ref[:, f0:f1], preferred_element_type=F32)
        t = jnp.dot((_silu(gate) * up).astype(BF16), wd_ref[f0:f1, :], preferred_element_type=F32)
        acc = t if acc is None else acc + t
    y = acc * lax.rsqrt(jnp.mean(acc * acc, axis=-1, keepdims=True) + RMS_EPS) * gpost_ref[...]
    o_ref[0] = x + gt_ref[0] * y


def _dense_ffn(x, sc, sh, gt, gpre, gpost, wg, wu, wd, tm):
    B, T, D = x.shape
    tok = pl.BlockSpec((1, tm, D), lambda b, i: (b, i, 0))
    vec = pl.BlockSpec((1, 1, D), lambda b, i: (b, 0, 0))
    const2 = lambda arr: pl.BlockSpec(arr.shape, lambda b, i: (0, 0))
    return pl.pallas_call(
        functools.partial(_ffn_kernel, tf=512),
        out_shape=jax.ShapeDtypeStruct((B, T, D), F32),
        grid=(B, T // tm),
        in_specs=[tok, vec, vec, vec, const2(gpre), const2(gpost), const2(wg), const2(wu), const2(wd)],
        out_specs=tok,
        compiler_params=_cparams(("arbitrary", "arbitrary")),
        name="dense_swiglu",
    )(x, sc, sh, gt, gpre, gpost, wg, wu, wd)


MOE_TOKEN_TILE = 256
MOE_ROW_TILE = 512
SEG_ALIGN = 16
SEG_PIECES = (256, 128, 64, 32, 16)


def _route_kernel(x_ref, sc_ref, sh_ref, gpre_ref, rw_ref, rb_ref, h_ref, mi_ref, mp_ref, cnt_ref):
    tm = x_ref.shape[1]
    h = _rms_mod(x_ref[0], gpre_ref[...], sc_ref[0], sh_ref[0])
    h_ref[...] = h.astype(BF16)
    logits = _mm_f32(h, rw_ref[...]) + rb_ref[...]
    lane = lax.broadcasted_iota(jnp.int32, logits.shape, 1)
    v1 = jnp.max(logits, axis=-1, keepdims=True)
    i1 = jnp.min(jnp.where(logits == v1, lane, LANES), axis=-1, keepdims=True)
    rest = jnp.where(lane == i1, -jnp.inf, logits)
    v2 = jnp.max(rest, axis=-1, keepdims=True)
    i2 = jnp.min(jnp.where(rest == v2, lane, LANES), axis=-1, keepdims=True)
    e2 = jnp.exp(v2 - v1)
    p1 = 1.0 / (1.0 + e2)
    p2 = e2 / (1.0 + e2)
    oh1 = (lane == i1).astype(F32)
    oh2 = (lane == i2).astype(F32)
    both = oh1 + oh2
    earlier = (lax.broadcasted_iota(jnp.int32, (tm, tm), 1)
               < lax.broadcasted_iota(jnp.int32, (tm, tm), 0)).astype(BF16)
    before = jnp.dot(earlier, both.astype(BF16), preferred_element_type=F32)
    r1 = jnp.sum(oh1 * before, axis=-1, keepdims=True).astype(jnp.int32)
    r2 = jnp.sum(oh2 * before, axis=-1, keepdims=True).astype(jnp.int32)
    col = lax.broadcasted_iota(jnp.int32, mi_ref.shape, 1)
    mi_ref[...] = jnp.where(col == 0, i1, jnp.where(col == 1, i2, jnp.where(col == 2, r1,
                                                                           jnp.where(col == 3, r2, 0))))
    mp_ref[...] = jnp.where(col == 0, p1, jnp.where(col == 1, p2, 0.0))
    cnt_ref[0] = jnp.sum(both, axis=0, keepdims=True).astype(jnp.int32)


def _segment_pieces(n_rows):
    out = []
    for s in SEG_PIECES:
        if s == MOE_TOKEN_TILE:
            out.append((n_rows == s, 0, s))
        else:
            out.append(((n_rows & s) != 0, pl.multiple_of((n_rows // (2 * s)) * (2 * s), SEG_ALIGN), s))
    return out


def _dispatch_kernel(seg_ref, cnt_ref, h_ref, mit_ref, init_ref, xs_ref, buf_ref, sem):
    del init_ref
    tm = h_ref.shape[0]
    i = pl.program_id(0)
    e1, e2 = mit_ref[0:1, :], mit_ref[1:2, :]
    r1, r2 = mit_ref[2:3, :], mit_ref[3:4, :]
    row = lax.broadcasted_iota(jnp.int32, (tm, tm), 0)
    select = jnp.concatenate(
        [jnp.logical_or(jnp.logical_and(e1 == e, r1 == row), jnp.logical_and(e2 == e, r2 == row))
         for e in range(N_EXPERTS)], axis=0).astype(BF16)
    buf_ref[...] = jnp.dot(select, h_ref[...], preferred_element_type=F32).astype(BF16)

    def segment_copies(e):
        n = cnt_ref[i * N_EXPERTS + e]
        n_rows = ((n + SEG_ALIGN - 1) // SEG_ALIGN) * SEG_ALIGN
        dst = pl.multiple_of(seg_ref[i * N_EXPERTS + e], SEG_ALIGN)
        return [(cond, pltpu.make_async_copy(buf_ref.at[pl.ds(e * tm + off, s), :],
                                             xs_ref.at[pl.ds(dst + off, s), :], sem))
                for cond, off, s in _segment_pieces(n_rows)]

    for e in range(N_EXPERTS):
        for cond, cp in segment_copies(e):
            pl.when(cond)(cp.start)
    for e in range(N_EXPERTS):
        for cond, cp in segment_copies(e):
            pl.when(cond)(cp.wait)


def _expert_kernel(te_ref, nv_ref, xs_ref, wg_ref, wu_ref, wd_ref, ys_ref, acc_ref):
    del te_ref
    r = pl.program_id(0)
    f = pl.program_id(1)
    valid = r < nv_ref[0]

    @pl.when(jnp.logical_and(valid, f == 0))
    def _():
        acc_ref[...] = jnp.zeros_like(acc_ref)

    @pl.when(valid)
    def _():
        x = xs_ref[...]
        gate = jnp.dot(x, wg_ref[0], preferred_element_type=F32)
        up = jnp.dot(x, wu_ref[0], preferred_element_type=F32)
        acc_ref[...] += jnp.dot((_silu(gate) * up).astype(BF16), wd_ref[0], preferred_element_type=F32)

    @pl.when(f == pl.num_programs(1) - 1)
    def _():
        @pl.when(valid)
        def _():
            ys_ref[...] = acc_ref[...].astype(ys_ref.dtype)

        @pl.when(jnp.logical_not(valid))
        def _():
            ys_ref[...] = jnp.zeros_like(ys_ref)


def _combine_kernel(seg_ref, ys_ref, mi_ref, mp_ref, x_ref, gt_ref, gpost_ref, o_ref, win_ref, sem):
    tm = x_ref.shape[1]
    i = pl.program_id(0) * pl.num_programs(1) + pl.program_id(1)

    def window_copy(e):
        src = pl.multiple_of(seg_ref[i * N_EXPERTS + e], SEG_ALIGN)
        return pltpu.make_async_copy(ys_ref.at[pl.ds(src, tm), :], win_ref.at[pl.ds(e * tm, tm), :],
                                     sem.at[e])

    for e in range(N_EXPERTS):
        window_copy(e).start()
    e1, e2 = mi_ref[:, 0:1], mi_ref[:, 1:2]
    r1, r2 = mi_ref[:, 2:3], mi_ref[:, 3:4]
    col = lax.broadcasted_iota(jnp.int32, (tm, tm), 1)
    acc = None
    for e in range(N_EXPERTS):
        expand = jnp.concatenate([jnp.logical_and(e1 == e, r1 == col),
                                  jnp.logical_and(e2 == e, r2 == col)], axis=0).astype(BF16)
        window_copy(e).wait()
        t = jnp.dot(expand, win_ref[pl.ds(e * tm, tm), :], preferred_element_type=F32)
        acc = t if acc is None else acc + t
    y = mp_ref[:, 0:1] * acc[0:tm] + mp_ref[:, 1:2] * acc[tm:2 * tm]
    y = y * lax.rsqrt(jnp.mean(y * y, axis=-1, keepdims=True) + RMS_EPS) * gpost_ref[...]
    o_ref[0] = x_ref[0] + gt_ref[0] * y


def _moe_ffn(x, sc, sh, gt, gpre, gpost, rw, rb, wg, wu, wd, tf):
    B, T, D = x.shape
    E, _, F = wg.shape
    tm = min(MOE_TOKEN_TILE, T)
    assert tm == MOE_TOKEN_TILE and E == N_EXPERTS
    nT = T // tm
    n_tok_tiles = B * nT
    N = B * T
    max_rows = 2 * N + n_tok_tiles * E * (SEG_ALIGN - 1) + E * (MOE_ROW_TILE - SEG_ALIGN)
    n_row_tiles = -(-max_rows // MOE_ROW_TILE) + 1
    P = n_row_tiles * MOE_ROW_TILE

    vec = pl.BlockSpec((1, 1, D), lambda b, j: (b, 0, 0))
    const2 = lambda arr: pl.BlockSpec(arr.shape, lambda b, j: (0, 0))
    flat = lambda n: pl.BlockSpec((tm, n), lambda b, j: (b * nT + j, 0))
    h, mi, mp, cnt = pl.pallas_call(
        _route_kernel,
        out_shape=(jax.ShapeDtypeStruct((N, D), BF16), jax.ShapeDtypeStruct((N, 8), jnp.int32),
                   jax.ShapeDtypeStruct((N, 8), F32), jax.ShapeDtypeStruct((n_tok_tiles, 1, LANES), jnp.int32)),
        grid=(B, nT),
        in_specs=[pl.BlockSpec((1, tm, D), lambda b, j: (b, j, 0)), vec, vec, const2(gpre), const2(rw),
                  const2(rb)],
        out_specs=(flat(D), flat(8), flat(8), pl.BlockSpec((1, 1, LANES), lambda b, j: (b * nT + j, 0, 0))),
        compiler_params=_cparams(("arbitrary", "arbitrary")),
        name="moe_route",
    )(x, sc, sh, gpre, rw, rb)

    counts = cnt[:, 0, :E]
    seg_len = (counts + SEG_ALIGN - 1) // SEG_ALIGN * SEG_ALIGN
    group_len = (jnp.sum(seg_len, axis=0) + MOE_ROW_TILE - 1) // MOE_ROW_TILE * MOE_ROW_TILE
    group_end = jnp.cumsum(group_len)
    seg_start = (group_end - group_len)[None, :] + jnp.cumsum(seg_len, axis=0) - seg_len
    seg_start = seg_start.reshape(-1).astype(jnp.int32)
    counts = counts.reshape(-1)
    n_valid = (group_end[-1:] // MOE_ROW_TILE).astype(jnp.int32)
    tile_first_row = jnp.arange(n_row_tiles, dtype=jnp.int32) * MOE_ROW_TILE
    tile_expert = jnp.minimum(jnp.sum(tile_first_row[:, None] >= group_end[None, :], axis=1), E - 1)
    tile_expert = tile_expert.astype(jnp.int32)

    xs = pl.pallas_call(
        _dispatch_kernel,
        out_shape=jax.ShapeDtypeStruct((P, D), BF16),
        grid_spec=pltpu.PrefetchScalarGridSpec(
            num_scalar_prefetch=2,
            grid=(n_tok_tiles,),
            in_specs=[pl.BlockSpec((tm, D), lambda i, seg, n: (i, 0)),
                      pl.BlockSpec((8, tm), lambda i, seg, n: (0, i)),
                      pl.BlockSpec(memory_space=pl.ANY)],
            out_specs=pl.BlockSpec(memory_space=pl.ANY),
            scratch_shapes=[pltpu.VMEM((E * tm, D), BF16), pltpu.SemaphoreType.DMA(())]),
        input_output_aliases={4: 0},
        compiler_params=_cparams(("arbitrary",)),
        name="moe_dispatch",
    )(seg_start, counts, h, mi.T, jnp.zeros((P, D), BF16))

    nf = F // tf
    live = lambda r, f, nv: jnp.where(r < nv[0], f, nf - 1)
    ys = pl.pallas_call(
        _expert_kernel,
        out_shape=jax.ShapeDtypeStruct((P, D), BF16),
        grid_spec=pltpu.PrefetchScalarGridSpec(
            num_scalar_prefetch=2,
            grid=(n_row_tiles, nf),
            in_specs=[pl.BlockSpec((MOE_ROW_TILE, D), lambda r, f, te, nv: (r, 0)),
                      pl.BlockSpec((1, D, tf), lambda r, f, te, nv: (te[r], 0, live(r, f, nv))),
                      pl.BlockSpec((1, D, tf), lambda r, f, te, nv: (te[r], 0, live(r, f, nv))),
                      pl.BlockSpec((1, tf, D), lambda r, f, te, nv: (te[r], live(r, f, nv), 0))],
            out_specs=pl.BlockSpec((MOE_ROW_TILE, D), lambda r, f, te, nv: (r, 0)),
            scratch_shapes=[pltpu.VMEM((MOE_ROW_TILE, D), F32)]),
        compiler_params=_cparams(("arbitrary", "arbitrary")),
        name="moe_experts",
    )(tile_expert, n_valid, xs, wg, wu, wd)

    return pl.pallas_call(
        _combine_kernel,
        out_shape=jax.ShapeDtypeStruct((B, T, D), F32),
        grid_spec=pltpu.PrefetchScalarGridSpec(
            num_scalar_prefetch=1,
            grid=(B, nT),
            in_specs=[pl.BlockSpec(memory_space=pl.ANY),
                      pl.BlockSpec((tm, 8), lambda b, j, seg: (b * nT + j, 0)),
                      pl.BlockSpec((tm, 8), lambda b, j, seg: (b * nT + j, 0)),
                      pl.BlockSpec((1, tm, D), lambda b, j, seg: (b, j, 0)),
                      pl.BlockSpec((1, 1, D), lambda b, j, seg: (b, 0, 0)),
                      pl.BlockSpec(gpost.shape, lambda b, j, seg: (0, 0))],
            out_specs=pl.BlockSpec((1, tm, D), lambda b, j, seg: (b, j, 0)),
            scratch_shapes=[pltpu.VMEM((E * tm, D), BF16), pltpu.SemaphoreType.DMA((E,))]),
        compiler_params=_cparams(("arbitrary", "arbitrary")),
        name="moe_combine",
    )(seg_start, ys, mi, mp, x, gt, gpost)


def _rope_tables(positions, groups):
    d = 32
    inv = 1.0 / (ROPE_THETA ** (jnp.arange(0, d, 2, dtype=F32) / d))
    ang = positions.astype(F32)[..., None] * inv
    cos, sin = jnp.cos(ang), jnp.sin(ang)
    cos = jnp.tile(jnp.concatenate([cos, cos], axis=-1), (1, 1, groups))
    sin = jnp.tile(jnp.concatenate([-sin, sin], axis=-1), (1, 1, groups))
    return cos, sin


def _pad_rows(w, rows, offset):
    out = jnp.zeros((rows, w.shape[1]), w.dtype)
    return out.at[offset:offset + w.shape[0]].set(w)


def kernel(x, c, positions, ada_w, ada_b, norm_mix_pre, norm_mix_post, norm_ffn_pre, norm_ffn_post, w_in, gla_gate_w2, gla_gate_b, gla_norm, diff_lambda, diff_subln, conv_w, conv_b, conv_ln_g, conv_ln_b, rwkv_mu, rwkv_w0, rwkv_w2, rwkv_a0, rwkv_a2, rwkv_g2, rwkv_k_k, rwkv_k_a, rwkv_r_k, rwkv_ln_g, rwkv_ln_b, w_branch, w_out, ffn_w_gate, ffn_w_up, ffn_w_down, router_w, router_b, moe_w_gate, moe_w_up, moe_w_down):
    B, T, D = x.shape
    L = ada_w.shape[0]
    W = D // N_BRANCH
    hk = gla_gate_b.shape[1]
    decay_rank = rwkv_w2.shape[1]
    a_rank = rwkv_a2.shape[1]
    gate_rank = rwkv_g2.shape[1]
    assert decay_rank + a_rank == LANES and 2 * hk == W and gate_rank == LANES
    n_mix = 3 * W + 3 * W + 2 * W + (3 * W + decay_rank + a_rank + gate_rank)
    sizes = (hk, hk, W, W, GLA_GATE_RANK, W, W, W, W, W, 3 * W + LANES + gate_rank, N_BRANCH * D)
    offs = [0]
    for s in sizes:
        offs.append(offs[-1] + s)
    assert offs[-1] == w_in.shape[2]
    tm = min(512, T)
    tb = min(512, T)

    mod = _modulation(c, ada_w, ada_b)
    cos, sin = _rope_tables(positions, W // 32)

    for l in range(L):
        m = mod[l].reshape(B, 1, 6 * D)
        sh_m, sc_m, gt_m, sh_f, sc_f, gt_f = [m[:, :, i * D:(i + 1) * D] for i in range(6)]

        wl = w_in[l]
        gz_cols = jnp.zeros((D, LANES), F32).at[:, :GLA_GATE_RANK].set(wl[:, offs[4]:offs[5]])
        w_mix = jnp.concatenate([wl[:, offs[0]:offs[4]], wl[:, offs[5]:offs[11]], gz_cols], axis=1)
        w_mix = w_mix.astype(BF16)
        w_gate = wl[:, offs[11]:offs[12]].astype(BF16)
        proj = _in_projection(x, sc_m, sh_m, norm_mix_pre[l][None], w_mix, tm)

        w2p = _pad_rows(gla_gate_w2[l], LANES, 0)
        o_gla = _gla(proj, w2p, gla_gate_b[l][None], jnp.tile(gla_norm[l], GLA_HEADS)[None],
                     col_qk=0, col_v=1, col_og=2, col_gz=(n_mix // LANES), tb=tb)
        o_diff = _diff_attention(proj, cos, sin, diff_lambda[l],
                                 diff_subln[l][None], col_q=3, col_k=4, col_v=5, layer_idx=l)
        o_conv = _conformer_conv(proj, _pad_rows(conv_w[l], 32, 0), conv_b[l][None],
                                 conv_ln_g[l][None], conv_ln_b[l][None], col_a=6, col_b=7)
        vecs = dict(mu=rwkv_mu[l][None], w0=rwkv_w0[l][None], a0=rwkv_a0[l][None],
                    kk=rwkv_k_k[l][None], ka=rwkv_k_a[l][None], rk=rwkv_r_k[l].reshape(1, W),
                    lng=rwkv_ln_g[l][None], lnb=rwkv_ln_b[l][None])
        o_rwkv = _rwkv7(proj, vecs, _pad_rows(rwkv_w2[l], LANES, 0).astype(BF16),
                        _pad_rows(rwkv_a2[l], LANES, decay_rank).astype(BF16),
                        rwkv_g2[l].astype(BF16), col=2, tb=tb)
        x = _merge(x, sc_m, sh_m, gt_m, norm_mix_pre[l][None], norm_mix_post[l][None], w_gate,
                   (o_gla, o_diff, o_conv, o_rwkv), w_branch[l].astype(BF16), w_out[l].astype(BF16), tm)

        i = l // 2
        if l % 2 == 0:
            x = _dense_ffn(x, sc_f, sh_f, gt_f, norm_ffn_pre[l][None], norm_ffn_post[l][None],
                           ffn_w_gate[i].astype(BF16), ffn_w_up[i].astype(BF16),
                           ffn_w_down[i].astype(BF16), tm)
        else:
            rw = jnp.zeros((D, LANES), F32).at[:, :N_EXPERTS].set(router_w[i])
            rb = jnp.full((1, LANES), -jnp.inf, F32).at[0, :N_EXPERTS].set(router_b[i])
            x = _moe_ffn(x, sc_f, sh_f, gt_f, norm_ffn_pre[l][None], norm_ffn_post[l][None], rw, rb,
                         moe_w_gate[i].astype(BF16), moe_w_up[i].astype(BF16),
                         moe_w_down[i].astype(BF16), tf=512)
    return x
```

```python
import functools
import math

import jax
import jax.numpy as jnp
from jax import lax
from jax.experimental import pallas as pl
from jax.experimental.pallas import tpu as pltpu

F32 = jnp.float32
BF16 = jnp.bfloat16
HIGHEST = lax.Precision.HIGHEST

N_BRANCH = 4
GLA_HEADS = 4
GLA_GATE_RANK = 16
GLA_GATE_NORMALIZER = 16.0
GLA_CHUNK = 32
DIFF_HEADS = 4
ROPE_THETA = 10000.0
CONV_WIDTH = 31
RWKV_HEADS = 4
RWKV_CHUNK = 64
RWKV_SUB = 16
N_EXPERTS = 8
RMS_EPS = 1e-6
LN_EPS = 1e-5
RWKV_GN_EPS = 64e-5
LANES = 128
VMEM_LIMIT = 56 * 1024 * 1024


def _cparams(sem):
    return pltpu.CompilerParams(dimension_semantics=sem, vmem_limit_bytes=VMEM_LIMIT)


def _mm(a, b):
    return jnp.dot(a.astype(BF16), b.astype(BF16), preferred_element_type=F32)


def _mm_nt(a, b):
    return lax.dot_general(a.astype(BF16), b.astype(BF16), (((1,), (1,)), ((), ())),
                           preferred_element_type=F32)


def _mm_tn(a, b):
    return lax.dot_general(a.astype(BF16), b.astype(BF16), (((0,), (0,)), ((), ())),
                           preferred_element_type=F32)


def _bmm(a, b):
    return lax.dot_general(a.astype(BF16), b.astype(BF16), (((2,), (1,)), ((0,), (0,))),
                           preferred_element_type=F32)


def _mm_f32(a, b):
    return jnp.dot(a, b, precision=HIGHEST, preferred_element_type=F32)


def _hi_lo(x):
    hi = x.astype(BF16)
    return jnp.concatenate([hi, (x - hi.astype(F32)).astype(BF16)], axis=1)


def _sigmoid(x):
    return 1.0 / (1.0 + jnp.exp(-x))


def _silu(x):
    return x * _sigmoid(x)


def _softplus(x):
    return jnp.maximum(x, 0.0) + jnp.log(1.0 + jnp.exp(-jnp.abs(x)))


def _group_matrix(n, group):
    r = lax.broadcasted_iota(jnp.int32, (n, n), 0) // group
    c = lax.broadcasted_iota(jnp.int32, (n, n), 1) // group
    return r == c


def _rms_mod(x, gain, scale, shift):
    y = x * lax.rsqrt(jnp.mean(x * x, axis=-1, keepdims=True) + RMS_EPS)
    return y * gain * (1.0 + scale) + shift


def _mod_kernel(c_ref, w_ref, b_ref, o_ref):
    o_ref[0] = _mm_f32(_silu(c_ref[...]), w_ref[0]) + b_ref[0]


def _modulation(c, ada_w, ada_b):
    L, D, M = ada_w.shape
    B = c.shape[0]
    tn = M // 4
    return pl.pallas_call(
        _mod_kernel,
        out_shape=jax.ShapeDtypeStruct((L, B, M), F32),
        grid=(L, M // tn),
        in_specs=[pl.BlockSpec((B, D), lambda l, j: (0, 0)),
                  pl.BlockSpec((1, D, tn), lambda l, j: (l, 0, j)),
                  pl.BlockSpec((1, 1, tn), lambda l, j: (l, 0, j))],
        out_specs=pl.BlockSpec((1, B, tn), lambda l, j: (l, 0, j)),
        compiler_params=_cparams(("arbitrary", "arbitrary")),
        name="adaln_mod",
    )(c, ada_w, ada_b.reshape(L, 1, M))


def _inproj_kernel(x_ref, sc_ref, sh_ref, g_ref, w_ref, o_ref):
    h = _rms_mod(x_ref[0], g_ref[...], sc_ref[0], sh_ref[0])
    o_ref[0] = _mm(h, w_ref[...])


def _in_projection(x, sc, sh, gain, w, tm):
    B, T, D = x.shape
    n = w.shape[1]
    return pl.pallas_call(
        _inproj_kernel,
        out_shape=jax.ShapeDtypeStruct((B, T, n), F32),
        grid=(B, T // tm),
        in_specs=[pl.BlockSpec((1, tm, D), lambda b, i: (b, i, 0)),
                  pl.BlockSpec((1, 1, D), lambda b, i: (b, 0, 0)),
                  pl.BlockSpec((1, 1, D), lambda b, i: (b, 0, 0)),
                  pl.BlockSpec((1, D), lambda b, i: (0, 0)),
                  pl.BlockSpec((D, n), lambda b, i: (0, 0))],
        out_specs=pl.BlockSpec((1, tm, n), lambda b, i: (b, i, 0)),
        compiler_params=_cparams(("arbitrary", "arbitrary")),
        name="in_proj",
    )(x, sc, sh, gain, w)


def _gla_kernel(qk_ref, v_ref, og_ref, gz_ref, w2_ref, gb_ref, ng_ref, o_ref, s_ref, g_ref, r_ref,
                qd_ref, kd_ref, oacc_ref, *, n_chunk):
    C = GLA_CHUNK
    hk = qk_ref.shape[2] // 2
    hv = v_ref.shape[2]
    dk = hk // GLA_HEADS
    dv = hv // GLA_HEADS

    tb = n_chunk * C

    @pl.when(pl.program_id(1) == 0)
    def _():
        s_ref[...] = jnp.zeros_like(s_ref)

    z = _mm(gz_ref[0], w2_ref[...]) + gb_ref[...]
    gk = (jnp.minimum(z, 0.0) - jnp.log(1.0 + jnp.exp(-jnp.abs(z)))) / GLA_GATE_NORMALIZER
    bi = lax.broadcasted_iota(jnp.int32, (tb, tb), 0)
    bj = lax.broadcasted_iota(jnp.int32, (tb, tb), 1)
    same_chunk = (bi // C) == (bj // C)
    prefix = jnp.logical_and(same_chunk, bj <= bi)
    suffix = jnp.logical_and(same_chunk, bj > bi)
    sums = jnp.dot(jnp.concatenate([prefix, suffix], axis=0).astype(BF16), _hi_lo(gk),
                   preferred_element_type=F32)
    G_all = sums[0:tb, 0:hk] + sums[0:tb, hk:2 * hk]
    R_all = sums[tb:2 * tb, 0:hk] + sums[tb:2 * tb, hk:2 * hk]
    g_ref[...] = G_all
    r_ref[...] = R_all
    qd_ref[...] = (qk_ref[0, :, 0:hk] * (dk ** -0.5) * jnp.exp(G_all)).astype(BF16)
    kd_ref[...] = (qk_ref[0, :, hk:2 * hk] * jnp.exp(R_all)).astype(BF16)

    causal = (lax.broadcasted_iota(jnp.int32, (C, C, hk), 1)
              <= lax.broadcasted_iota(jnp.int32, (C, C, hk), 0))
    er = lax.broadcasted_iota(jnp.int32, (hk, hv), 0) // dk
    ec = lax.broadcasted_iota(jnp.int32, (hk, hv), 1) // dv
    expand = (er == ec).astype(BF16)
    sr = lax.broadcasted_iota(jnp.int32, (hv, hk), 0) // dv
    scol = lax.broadcasted_iota(jnp.int32, (hv, hk), 1) // dk
    state_mask = sr == scol
    head_mean = (_group_matrix(hv, dv).astype(F32) / dv).astype(BF16)

    def chunk(ci, carry):
        r0 = pl.multiple_of(ci * C, C)
        rows = pl.ds(r0, C)
        q = qk_ref[0, rows, 0:hk] * (dk ** -0.5)
        k = qk_ref[0, rows, hk:2 * hk]
        v = v_ref[0, rows, :]
        G = g_ref[rows, :]
        pair = (C, C, hk)
        g_diff = jnp.broadcast_to(G[:, None, :], pair) - jnp.broadcast_to(G[None], pair)
        decay = jnp.exp(jnp.where(causal, g_diff, -jnp.inf))
        p = jnp.broadcast_to(q[:, None, :], pair) * jnp.broadcast_to(k[None], pair) * decay
        a_exp = jnp.dot(p.reshape(C * C, hk).astype(BF16), expand, preferred_element_type=F32)
        v_j = jnp.broadcast_to(v[None], (C, C, hv))
        o_intra = jnp.sum(a_exp.reshape(C, C, hv) * v_j, axis=1)
        g_total = G[0:1, :] + r_ref[pl.ds(r0, 8), :][0:1, :]
        s = s_ref[...]
        o_inter = lax.dot_general(qd_ref[rows, :], s.astype(BF16), (((1,), (1,)), ((), ())),
                                  preferred_element_type=F32)
        kv = lax.dot_general(v.astype(BF16), kd_ref[rows, :], (((0,), (0,)), ((), ())),
                             preferred_element_type=F32)
        s_ref[...] = s * jnp.exp(g_total) + jnp.where(state_mask, kv, 0.0)
        oacc_ref[rows, :] = o_intra + o_inter
        return carry

    lax.fori_loop(0, n_chunk, chunk, 0)

    o = oacc_ref[...]
    ms = jnp.dot(_hi_lo(o * o), jnp.concatenate([head_mean, head_mean], axis=0),
                 preferred_element_type=F32)
    o = o * lax.rsqrt(ms + RMS_EPS) * ng_ref[...] * _silu(og_ref[0])
    o_ref[0] = o.astype(o_ref.dtype)


def _gla(proj, w2p, gb, ng, col_qk, col_v, col_og, col_gz, tb):
    B, T, _ = proj.shape
    hv = ng.shape[1]
    hk = gb.shape[1]
    return pl.pallas_call(
        functools.partial(_gla_kernel, n_chunk=tb // GLA_CHUNK),
        out_shape=jax.ShapeDtypeStruct((B, T, hv), BF16),
        grid=(B, T // tb),
        in_specs=[pl.BlockSpec((1, tb, 2 * hk), lambda b, j: (b, j, col_qk)),
                  pl.BlockSpec((1, tb, hv), lambda b, j: (b, j, col_v)),
                  pl.BlockSpec((1, tb, hv), lambda b, j: (b, j, col_og)),
                  pl.BlockSpec((1, tb, LANES), lambda b, j: (b, j, col_gz)),
                  pl.BlockSpec(w2p.shape, lambda b, j: (0, 0)),
                  pl.BlockSpec(gb.shape, lambda b, j: (0, 0)),
                  pl.BlockSpec(ng.shape, lambda b, j: (0, 0))],
        out_specs=pl.BlockSpec((1, tb, hv), lambda b, j: (b, j, 0)),
        scratch_shapes=[pltpu.VMEM((hv, hk), F32), pltpu.VMEM((tb, hk), F32), pltpu.VMEM((tb, hk), F32),
                        pltpu.VMEM((tb, hk), BF16), pltpu.VMEM((tb, hk), BF16), pltpu.VMEM((tb, hv), F32)],
        compiler_params=_cparams(("arbitrary", "arbitrary")),
        name="gla_mixer",
    )(proj, proj, proj, proj, w2p, gb, ng)


def _rope(t, cos, sin_signed):
    d = 32
    half = d // 2
    out = []
    for s in range(t.shape[1] // LANES):
        x = t[:, s * LANES:(s + 1) * LANES]
        lane = lax.broadcasted_iota(jnp.int32, x.shape, 1)
        up = pltpu.roll(x, LANES - half, 1)
        down = pltpu.roll(x, half, 1)
        rot = jnp.where((lane % d) < half, up, down)
        out.append(x * cos[:, s * LANES:(s + 1) * LANES] + rot * sin_signed[:, s * LANES:(s + 1) * LANES])
    return jnp.concatenate(out, axis=1)


def _diff_kernel(q_ref, k_ref, v_ref, cosq_ref, sinq_ref, cosk_ref, sink_ref, lam_ref, g_ref, o_ref,
                 ks, vs, *, tq, lam_init):
    H = DIFF_HEADS
    d = q_ref.shape[2] // (2 * H)
    dv = v_ref.shape[2] // H
    j = pl.program_id(1)

    @pl.when(j == 0)
    def _():
        k = _rope(k_ref[0], cosk_ref[0], sink_ref[0])
        v = v_ref[0]
        for hc in range(2 * H):
            ks[hc] = k[:, hc * d:(hc + 1) * d].astype(BF16)
        ones_col = (lax.broadcasted_iota(jnp.int32, (v.shape[0], dv), 1) == 0).astype(BF16)
        for h in range(H):
            vs[h] = jnp.concatenate([v[:, h * dv:(h + 1) * dv].astype(BF16), ones_col], axis=1)

    q = _rope(q_ref[0], cosq_ref[0], sinq_ref[0]) * (d ** -0.5)
    lp = lam_ref[...]
    lam = (jnp.exp(jnp.sum(lp[0:1] * lp[1:2], axis=-1, keepdims=True))
           - jnp.exp(jnp.sum(lp[2:3] * lp[3:4], axis=-1, keepdims=True)) + lam_init)
    on_or_below_diag = (lax.broadcasted_iota(jnp.int32, (tq, tq), 1)
                        <= lax.broadcasted_iota(jnp.int32, (tq, tq), 0))

    def update(qh, kh, vh, m, acc, masked):
        s = lax.dot_general(qh, kh, (((1,), (1,)), ((), ())), preferred_element_type=F32)
        if masked:
            s = jnp.where(on_or_below_diag, s, -jnp.inf)
        m_new = jnp.maximum(m, jnp.max(s, axis=-1, keepdims=True))
        p = jnp.exp(s - m_new).astype(BF16)
        acc = jnp.exp(m - m_new) * acc + jnp.dot(p, vh, preferred_element_type=F32)
        return m_new, acc

    qs = [q[:, hc * d:(hc + 1) * d].astype(BF16) for hc in range(2 * H)]

    def kv_block(kb, carry, masked):
        rows = pl.ds(pl.multiple_of(kb * tq, tq), tq)
        out = []
        for hc in range(2 * H):
            m, acc = carry[2 * hc], carry[2 * hc + 1]
            out.extend(update(qs[hc], ks[hc, rows, :], vs[hc // 2, rows, :], m, acc, masked))
        return tuple(out)

    m0 = jnp.full((tq, 1), -jnp.inf, F32)
    a0 = jnp.zeros((tq, 2 * dv), F32)
    carry = lax.fori_loop(0, j, lambda kb, c: kv_block(kb, c, False), (m0, a0) * (2 * H))
    carry = kv_block(j, carry, True)

    for h in range(H):
        a1, a2 = carry[4 * h + 1], carry[4 * h + 3]
        comp = [a[:, 0:dv] / a[:, dv:dv + 1] for a in (a1, a2)]
        o = comp[0] - lam * comp[1]
        o = o * lax.rsqrt(jnp.mean(o * o, axis=-1, keepdims=True) + RMS_EPS)
        o = o * g_ref[...] * (1.0 - lam_init)
        o_ref[0, :, h * dv:(h + 1) * dv] = o.astype(o_ref.dtype)


def _diff_attention(proj, cos, sin, lam_p, g, col_q, col_k, col_v, layer_idx):
    B, T, _ = proj.shape
    H = DIFF_HEADS
    w = cos.shape[2]
    d = w // (2 * H)
    dv = 2 * d
    tq = min(512, T)
    lam_init = 0.8 - 0.6 * math.exp(-0.3 * layer_idx)
    blk = lambda col: pl.BlockSpec((1, tq, w), lambda b, j: (b, j, col))
    full = lambda col: pl.BlockSpec((1, T, w), lambda b, j: (b, 0, col))
    return pl.pallas_call(
        functools.partial(_diff_kernel, tq=tq, lam_init=lam_init),
        out_shape=jax.ShapeDtypeStruct((B, T, H * dv), BF16),
        grid=(B, T // tq),
        in_specs=[blk(col_q), full(col_k), full(col_v), blk(0), blk(0), full(0), full(0),
                  pl.BlockSpec(lam_p.shape, lambda b, j: (0, 0)),
                  pl.BlockSpec(g.shape, lambda b, j: (0, 0))],
        out_specs=pl.BlockSpec((1, tq, H * dv), lambda b, j: (b, j, 0)),
        scratch_shapes=[pltpu.VMEM((2 * H, T, d), BF16),
                        pltpu.VMEM((H, T, 2 * dv), BF16)],
        compiler_params=_cparams(("arbitrary", "arbitrary")),
        name="diff_attention",
    )(proj, proj, proj, cos, sin, cos, sin, lam_p, g)


def _conv_kernel(a_ref, b_ref, w_ref, cb_ref, lg_ref, lb_ref, o_ref, u_ref, *, rb):
    T = a_ref.shape[1]
    pad = u_ref.shape[0] - T
    u_ref[0:pad, :] = jnp.zeros((pad, u_ref.shape[1]), F32)
    u_ref[pad:pad + T, :] = a_ref[0] * _sigmoid(b_ref[0])
    first = pad - (CONV_WIDTH - 1)

    def block(i, carry):
        r0 = pl.multiple_of(i * rb, rb)
        win = u_ref[pl.ds(r0, rb + pad), :]
        acc = jnp.zeros((rb, u_ref.shape[1]), F32)
        for j in range(CONV_WIDTH):
            acc = acc + w_ref[j:j + 1, :] * win[first + j:first + j + rb, :]
        y = acc + cb_ref[...]
        mu = jnp.mean(y, axis=-1, keepdims=True)
        yc = y - mu
        var = jnp.mean(yc * yc, axis=-1, keepdims=True)
        y = yc * lax.rsqrt(var + LN_EPS) * lg_ref[...] + lb_ref[...]
        o_ref[0, pl.ds(r0, rb), :] = _silu(y).astype(o_ref.dtype)
        return carry

    lax.fori_loop(0, T // rb, block, 0)


def _conformer_conv(proj, w, cb, lg, lb, col_a, col_b):
    B, T, _ = proj.shape
    ch = w.shape[1]
    rb = min(128, T)
    return pl.pallas_call(
        functools.partial(_conv_kernel, rb=rb),
        out_shape=jax.ShapeDtypeStruct((B, T, ch), BF16),
        grid=(B,),
        in_specs=[pl.BlockSpec((1, T, ch), lambda b: (b, 0, col_a)),
                  pl.BlockSpec((1, T, ch), lambda b: (b, 0, col_b)),
                  pl.BlockSpec(w.shape, lambda b: (0, 0)),
                  pl.BlockSpec(cb.shape, lambda b: (0, 0)),
                  pl.BlockSpec(lg.shape, lambda b: (0, 0)),
                  pl.BlockSpec(lb.shape, lambda b: (0, 0))],
        out_specs=pl.BlockSpec((1, T, ch), lambda b: (b, 0, 0)),
        scratch_shapes=[pltpu.VMEM((T + 32, ch), F32)],
        compiler_params=_cparams(("arbitrary",)),
        name="conformer_conv",
    )(proj, proj, w, cb, lg, lb)


def _rwkv_kernel(x_ref, mu_ref, w0_ref, w2_ref, a0_ref, a2_ref, g2_ref, kk_ref, ka_ref, rk_ref,
                 lng_ref, lnb_ref, o_ref, s_ref, prev_ref, gate_ref, bonus_ref, dec_ref, qeff_ref, yloc_ref,
                 mlow_ref, nc_ref, y_ref, *, n_chunk):
    C = RWKV_CHUNK
    H = RWKV_HEADS
    W = o_ref.shape[2]
    N = W // H

    @pl.when(pl.program_id(1) == 0)
    def _():
        s_ref[...] = jnp.zeros_like(s_ref)
        prev_ref[...] = jnp.zeros_like(prev_ref)

    lane = lax.broadcasted_iota(jnp.int32, (1, W), 1)
    head_mask = [(lane // N == h).astype(F32) for h in range(H)]
    block_diag = _group_matrix(W, N)
    head_sum = block_diag.astype(F32)
    ti = lax.broadcasted_iota(jnp.int32, (C, C), 0)
    tj = lax.broadcasted_iota(jnp.int32, (C, C), 1)
    tril_incl = ti >= tj
    tril_strict = ti > tj
    same_sub = (ti // RWKV_SUB) == (tj // RWKV_SUB)
    eye = (ti == tj).astype(F32)
    nc = n_chunk
    tb = nc * C

    def head_total(t):
        ones = head_sum.astype(BF16)
        return jnp.dot(_hi_lo(t), jnp.concatenate([ones, ones], axis=0), preferred_element_type=F32)

    x = x_ref[0]
    first_row = lax.broadcasted_iota(jnp.int32, (tb, 1), 0) == 0
    prev = jnp.where(first_row, prev_ref[...], pltpu.roll(x, 1, 0))
    prev_ref[...] = x[tb - 1:tb, :]
    xm = x + (prev - x) * mu_ref[...]
    r = xm[:, 0:W]
    k = xm[:, W:2 * W]
    v = xm[:, 2 * W:3 * W]
    zz = xm[:, 3 * W:3 * W + LANES]
    zg = xm[:, 3 * W + LANES:]
    w = -_softplus(-(w0_ref[...] + _mm(jnp.tanh(zz), w2_ref[...]))) - 0.5
    lw = -jnp.exp(w)
    a = _sigmoid(a0_ref[...] + _mm(zz, a2_ref[...]))
    gate_ref[...] = _mm(_sigmoid(zg), g2_ref[...])
    kk = k * kk_ref[...]
    kk = kk / jnp.maximum(jnp.sqrt(head_total(kk * kk)), 1e-12)
    k = k * (1.0 + (a - 1.0) * ka_ref[...])
    b = kk * a
    bonus_ref[...] = head_total(r * k * rk_ref[...]) * v

    bi = lax.broadcasted_iota(jnp.int32, (tb, tb), 0)
    bj = lax.broadcasted_iota(jnp.int32, (tb, tb), 1)
    same_chunk = (bi // C) == (bj // C)
    sums = jnp.dot(jnp.concatenate([jnp.logical_and(same_chunk, bj <= bi),
                                    jnp.logical_and(same_chunk, bj > bi)], axis=0).astype(BF16),
                   _hi_lo(lw), preferred_element_type=F32)
    G = sums[0:tb, 0:W] + sums[0:tb, W:2 * W]
    R = sums[tb:2 * tb, 0:W] + sums[tb:2 * tb, W:2 * W]
    dec_ref[...] = jnp.exp(G + R)
    inv = jnp.exp(-G)
    to_end = jnp.exp(R)
    c3 = lambda t: t.reshape(nc, C, t.shape[1])
    kap = kk * jnp.exp(G - lw)
    rho = r * jnp.exp(G)
    kap3, rho3, v3 = c3(kap), c3(rho), c3(v)
    bet3, kt3 = c3(b * inv), c3(k * inv)
    betc3, kc3 = c3(b * to_end), c3(k * to_end)

    def bmm_nt(p, q):
        return lax.dot_general(p.astype(BF16), q.astype(BF16), (((2,), (2,)), ((0,), (0,))),
                               preferred_element_type=F32)

    def bmm_tn(p, q):
        return lax.dot_general(p.astype(BF16), q.astype(BF16), (((1,), (1,)), ((0,), (0,))),
                               preferred_element_type=F32)

    A_b, A_k, B_b, B_k = [], [], [], []
    for h in range(H):
        lhs = jnp.concatenate([c3(kap * head_mask[h]), c3(rho * head_mask[h])], axis=1)
        pb = bmm_nt(lhs, bet3)
        pk = bmm_nt(lhs, kt3)
        A_b.append(jnp.where(tril_strict, pb[:, 0:C], 0.0))
        A_k.append(jnp.where(tril_strict, pk[:, 0:C], 0.0))
        B_b.append(jnp.where(tril_incl, pb[:, C:2 * C], 0.0))
        B_k.append(jnp.where(tril_incl, pk[:, C:2 * C], 0.0))
    A_b = jnp.concatenate(A_b, axis=0)

    Dg = jnp.where(same_sub, A_b, 0.0)
    Lo = A_b - Dg
    D2 = _bmm(Dg, Dg)
    D4 = _bmm(D2, D2)
    D8 = _bmm(D4, D4)
    Dinv = _bmm(_bmm(_bmm(eye - Dg, eye + D2), eye + D4), eye + D8)
    Nn = _bmm(Dinv, Lo)
    N2 = _bmm(Nn, Nn)
    Tm = _bmm(_bmm(eye - Nn, eye + N2), Dinv)
    Tm = [Tm[h * nc:(h + 1) * nc] for h in range(H)]

    def per_head(mats, t):
        reps = t.shape[2] // W
        acc = None
        for h in range(H):
            m = head_mask[h] if reps == 1 else jnp.concatenate([head_mask[h]] * reps, axis=1)
            u = m * _bmm(mats[h], t)
            acc = u if acc is None else acc + u
        return acc

    akv = per_head(A_k, v3)
    tk = per_head(Tm, jnp.concatenate([kap3, akv], axis=2))
    kap_p = tk[:, :, 0:W]
    v_p = tk[:, :, W:2 * W]
    bb = per_head(B_b, jnp.concatenate([kap_p, v_p], axis=2))
    qeff_ref[...] = (rho3 - bb[:, :, 0:W]).astype(BF16)
    yloc_ref[...] = per_head(B_k, v3) - bb[:, :, W:2 * W]
    mlow_ref[...] = jnp.where(block_diag, bmm_tn(kap_p, betc3), 0.0).astype(BF16)
    nc_ref[...] = jnp.where(block_diag, bmm_tn(v3, kc3) - bmm_tn(v_p, betc3), 0.0)

    def chunk(ci, carry):
        r0 = pl.multiple_of(ci * C, C)
        s = s_ref[...]
        sb = s.astype(BF16)
        y = lax.dot_general(qeff_ref[ci], sb, (((1,), (1,)), ((), ())), preferred_element_type=F32)
        y_ref[pl.ds(r0, C), :] = y + yloc_ref[ci]
        s_ref[...] = (s * dec_ref[pl.ds(r0, 8), :][0:1, :]
                      - jnp.dot(sb, mlow_ref[ci], preferred_element_type=F32) + nc_ref[ci])
        return carry

    lax.fori_loop(0, nc, chunk, 0)

    y = y_ref[...]
    mean = head_total(y) / N
    yc = y - mean
    var = head_total(yc * yc) / N
    yn = yc * lax.rsqrt(var + RWKV_GN_EPS) * lng_ref[...] + lnb_ref[...]
    o_ref[0] = ((yn + bonus_ref[...]) * gate_ref[...]).astype(o_ref.dtype)


def _rwkv7(proj, vecs, w2p, a2p, g2, col, tb):
    B, T, _ = proj.shape
    W = g2.shape[1]
    cols = vecs["mu"].shape[1]
    names = ("mu", "w0", "w2", "a0", "a2", "g2", "kk", "ka", "rk", "lng", "lnb")
    params = dict(vecs, w2=w2p, a2=a2p, g2=g2)
    const = lambda arr: pl.BlockSpec(arr.shape, lambda b, j: (0, 0))
    n_chunk = tb // RWKV_CHUNK
    return pl.pallas_call(
        functools.partial(_rwkv_kernel, n_chunk=n_chunk),
        out_shape=jax.ShapeDtypeStruct((B, T, W), BF16),
        grid=(B, T // tb),
        in_specs=[pl.BlockSpec((1, tb, cols), lambda b, j: (b, j, col))]
                 + [const(params[n]) for n in names],
        out_specs=pl.BlockSpec((1, tb, W), lambda b, j: (b, j, 0)),
        scratch_shapes=[pltpu.VMEM((W, W), F32), pltpu.VMEM((1, cols), F32),
                        pltpu.VMEM((tb, W), F32), pltpu.VMEM((tb, W), F32), pltpu.VMEM((tb, W), F32),
                        pltpu.VMEM((n_chunk, RWKV_CHUNK, W), BF16), pltpu.VMEM((n_chunk, RWKV_CHUNK, W), F32),
                        pltpu.VMEM((n_chunk, W, W), BF16), pltpu.VMEM((n_chunk, W, W), F32),
                        pltpu.VMEM((tb, W), F32)],
        compiler_params=_cparams(("arbitrary", "arbitrary")),
        name="rwkv7_mixer",
    )(proj, *[params[n] for n in names])


def _merge_kernel(x_ref, sc_ref, sh_ref, gt_ref, gpre_ref, gpost_ref, wg_ref, b0_ref, b1_ref, b2_ref,
                  b3_ref, wb_ref, wo_ref, o_ref):
    x = x_ref[0]
    D = x.shape[1]
    h = _rms_mod(x, gpre_ref[...], sc_ref[0], sh_ref[0]).astype(BF16)
    merged = None
    for g, br in enumerate((b0_ref, b1_ref, b2_ref, b3_ref)):
        gate = _sigmoid(jnp.dot(h, wg_ref[:, g * D:(g + 1) * D], preferred_element_type=F32))
        t = gate * jnp.dot(br[0], wb_ref[g], preferred_element_type=F32)
        merged = t if merged is None else merged + t
    y = _mm(merged, wo_ref[...])
    y = y * lax.rsqrt(jnp.mean(y * y, axis=-1, keepdims=True) + RMS_EPS) * gpost_ref[...]
    o_ref[0] = x + gt_ref[0] * y


def _merge(x, sc, sh, gt, gpre, gpost, w_gate, branches, w_branch, w_out, tm):
    B, T, D = x.shape
    bw = branches[0].shape[2]
    tok = lambda n: pl.BlockSpec((1, tm, n), lambda b, i: (b, i, 0))
    vec = pl.BlockSpec((1, 1, D), lambda b, i: (b, 0, 0))
    const2 = lambda arr: pl.BlockSpec(arr.shape, lambda b, i: (0, 0))
    return pl.pallas_call(
        _merge_kernel,
        out_shape=jax.ShapeDtypeStruct((B, T, D), F32),
        grid=(B, T // tm),
        in_specs=[tok(D), vec, vec, vec, const2(gpre), const2(gpost), const2(w_gate),
                  tok(bw), tok(bw), tok(bw), tok(bw),
                  pl.BlockSpec(w_branch.shape, lambda b, i: (0, 0, 0)), const2(w_out)],
        out_specs=tok(D),
        compiler_params=_cparams(("arbitrary", "arbitrary")),
        name="merge_out_proj",
    )(x, sc, sh, gt, gpre, gpost, w_gate, *branches, w_branch, w_out)


def _ffn_kernel(x_ref, sc_ref, sh_ref, gt_ref, gpre_ref, gpost_ref, wg_ref, wu_ref, wd_ref, o_ref, *, tf):
    x = x_ref[0]
    h = _rms_mod(x, gpre_ref[...], sc_ref[0], sh_ref[0]).astype(BF16)
    F = wg_ref.shape[1]
    acc = None
    for f0 in range(0, F, tf):
        f1 = min(f0 + tf, F)
        gate = jnp.dot(h, wg_ref[:, f0:f1], preferred_element_type=F32)
        up = jnp.dot(h, wu_ref[:, f0:f1], preferred_element_type=F32)
        t = jnp.dot((_silu(gate) * up).astype(BF16), wd_ref[f0:f1, :], preferred_element_type=F32)
        acc = t if acc is None else acc + t
    y = acc * lax.rsqrt(jnp.mean(acc * acc, axis=-1, keepdims=True) + RMS_EPS) * gpost_ref[...]
    o_ref[0] = x + gt_ref[0] * y


def _dense_ffn(x, sc, sh, gt, gpre, gpost, wg, wu, wd, tm):
    B, T, D = x.shape
    tok = pl.BlockSpec((1, tm, D), lambda b, i: (b, i, 0))
    vec = pl.BlockSpec((1, 1, D), lambda b, i: (b, 0, 0))
    const2 = lambda arr: pl.BlockSpec(arr.shape, lambda b, i: (0, 0))
    return pl.pallas_call(
        functools.partial(_ffn_kernel, tf=512),
        out_shape=jax.ShapeDtypeStruct((B, T, D), F32),
        grid=(B, T // tm),
        in_specs=[tok, vec, vec, vec, const2(gpre), const2(gpost), const2(wg), const2(wu), const2(wd)],
        out_specs=tok,
        compiler_params=_cparams(("arbitrary", "arbitrary")),
        name="dense_swiglu",
    )(x, sc, sh, gt, gpre, gpost, wg, wu, wd)


MOE_TOKEN_TILE = 256
MOE_ROW_TILE = 512
SEG_ALIGN = 16
SEG_PIECES = (256, 128, 64, 32, 16)


def _route_kernel(x_ref, sc_ref, sh_ref, gpre_ref, rw_ref, rb_ref, h_ref, mi_ref, mp_ref, cnt_ref):
    tm = x_ref.shape[1]
    h = _rms_mod(x_ref[0], gpre_ref[...], sc_ref[0], sh_ref[0])
    h_ref[...] = h.astype(BF16)
    logits = _mm_f32(h, rw_ref[...]) + rb_ref[...]
    lane = lax.broadcasted_iota(jnp.int32, logits.shape, 1)
    v1 = jnp.max(logits, axis=-1, keepdims=True)
    i1 = jnp.min(jnp.where(logits == v1, lane, LANES), axis=-1, keepdims=True)
    rest = jnp.where(lane == i1, -jnp.inf, logits)
    v2 = jnp.max(rest, axis=-1, keepdims=True)
    i2 = jnp.min(jnp.where(rest == v2, lane, LANES), axis=-1, keepdims=True)
    e2 = jnp.exp(v2 - v1)
    p1 = 1.0 / (1.0 + e2)
    p2 = e2 / (1.0 + e2)
    oh1 = (lane == i1).astype(F32)
    oh2 = (lane == i2).astype(F32)
    both = oh1 + oh2
    earlier = (lax.broadcasted_iota(jnp.int32, (tm, tm), 1)
               < lax.broadcasted_iota(jnp.int32, (tm, tm), 0)).astype(BF16)
    before = jnp.dot(earlier, both.astype(BF16), preferred_element_type=F32)
    r1 = jnp.sum(oh1 * before, axis=-1, keepdims=True).astype(jnp.int32)
    r2 = jnp.sum(oh2 * before, axis=-1, keepdims=True).astype(jnp.int32)
    col = lax.broadcasted_iota(jnp.int32, mi_ref.shape, 1)
    mi_ref[...] = jnp.where(col == 0, i1, jnp.where(col == 1, i2, jnp.where(col == 2, r1,
                                                                           jnp.where(col == 3, r2, 0))))
    mp_ref[...] = jnp.where(col == 0, p1, jnp.where(col == 1, p2, 0.0))
    cnt_ref[0] = jnp.sum(both, axis=0, keepdims=True).astype(jnp.int32)


def _segment_pieces(n_rows):
    out = []
    for s in SEG_PIECES:
        if s == MOE_TOKEN_TILE:
            out.append((n_rows == s, 0, s))
        else:
            out.append(((n_rows & s) != 0, pl.multiple_of((n_rows // (2 * s)) * (2 * s), SEG_ALIGN), s))
    return out


def _dispatch_kernel(seg_ref, cnt_ref, h_ref, mit_ref, init_ref, xs_ref, buf_ref, sem):
    del init_ref
    tm = h_ref.shape[0]
    i = pl.program_id(0)
    e1, e2 = mit_ref[0:1, :], mit_ref[1:2, :]
    r1, r2 = mit_ref[2:3, :], mit_ref[3:4, :]
    row = lax.broadcasted_iota(jnp.int32, (tm, tm), 0)
    select = jnp.concatenate(
        [jnp.logical_or(jnp.logical_and(e1 == e, r1 == row), jnp.logical_and(e2 == e, r2 == row))
         for e in range(N_EXPERTS)], axis=0).astype(BF16)
    buf_ref[...] = jnp.dot(select, h_ref[...], preferred_element_type=F32).astype(BF16)

    def segment_copies(e):
        n = cnt_ref[i * N_EXPERTS + e]
        n_rows = ((n + SEG_ALIGN - 1) // SEG_ALIGN) * SEG_ALIGN
        dst = pl.multiple_of(seg_ref[i * N_EXPERTS + e], SEG_ALIGN)
        return [(cond, pltpu.make_async_copy(buf_ref.at[pl.ds(e * tm + off, s), :],
                                             xs_ref.at[pl.ds(dst + off, s), :], sem))
                for cond, off, s in _segment_pieces(n_rows)]

    for e in range(N_EXPERTS):
        for cond, cp in segment_copies(e):
            pl.when(cond)(cp.start)
    for e in range(N_EXPERTS):
        for cond, cp in segment_copies(e):
            pl.when(cond)(cp.wait)


def _expert_kernel(te_ref, nv_ref, xs_ref, wg_ref, wu_ref, wd_ref, ys_ref, acc_ref):
    del te_ref
    r = pl.program_id(0)
    f = pl.program_id(1)
    valid = r < nv_ref[0]

    @pl.when(jnp.logical_and(valid, f == 0))
    def _():
        acc_ref[...] = jnp.zeros_like(acc_ref)

    @pl.when(valid)
    def _():
        x = xs_ref[...]
        gate = jnp.dot(x, wg_ref[0], preferred_element_type=F32)
        up = jnp.dot(x, wu_ref[0], preferred_element_type=F32)
        acc_ref[...] += jnp.dot((_silu(gate) * up).astype(BF16), wd_ref[0], preferred_element_type=F32)

    @pl.when(f == pl.num_programs(1) - 1)
    def _():
        @pl.when(valid)
        def _():
            ys_ref[...] = acc_ref[...].astype(ys_ref.dtype)

        @pl.when(jnp.logical_not(valid))
        def _():
            ys_ref[...] = jnp.zeros_like(ys_ref)


def _combine_kernel(seg_ref, ys_ref, mi_ref, mp_ref, x_ref, gt_ref, gpost_ref, o_ref, win_ref, sem):
    tm = x_ref.shape[1]
    i = pl.program_id(0) * pl.num_programs(1) + pl.program_id(1)

    def window_copy(e):
        src = pl.multiple_of(seg_ref[i * N_EXPERTS + e], SEG_ALIGN)
        return pltpu.make_async_copy(ys_ref.at[pl.ds(src, tm), :], win_ref.at[pl.ds(e * tm, tm), :],
                                     sem.at[e])

    for e in range(N_EXPERTS):
        window_copy(e).start()
    e1, e2 = mi_ref[:, 0:1], mi_ref[:, 1:2]
    r1, r2 = mi_ref[:, 2:3], mi_ref[:, 3:4]
    col = lax.broadcasted_iota(jnp.int32, (tm, tm), 1)
    acc = None
    for e in range(N_EXPERTS):
        expand = jnp.concatenate([jnp.logical_and(e1 == e, r1 == col),
                                  jnp.logical_and(e2 == e, r2 == col)], axis=0).astype(BF16)
        window_copy(e).wait()
        t = jnp.dot(expand, win_ref[pl.ds(e * tm, tm), :], preferred_element_type=F32)
        acc = t if acc is None else acc + t
    y = mp_ref[:, 0:1] * acc[0:tm] + mp_ref[:, 1:2] * acc[tm:2 * tm]
    y = y * lax.rsqrt(jnp.mean(y * y, axis=-1, keepdims=True) + RMS_EPS) * gpost_ref[...]
    o_ref[0] = x_ref[0] + gt_ref[0] * y


def _moe_ffn(x, sc, sh, gt, gpre, gpost, rw, rb, wg, wu, wd, tf):
    B, T, D = x.shape
    E, _, F = wg.shape
    tm = min(MOE_TOKEN_TILE, T)
    assert tm == MOE_TOKEN_TILE and E == N_EXPERTS
    nT = T // tm
    n_tok_tiles = B * nT
    N = B * T
    max_rows = 2 * N + n_tok_tiles * E * (SEG_ALIGN - 1) + E * (MOE_ROW_TILE - SEG_ALIGN)
    n_row_tiles = -(-max_rows // MOE_ROW_TILE) + 1
    P = n_row_tiles * MOE_ROW_TILE

    vec = pl.BlockSpec((1, 1, D), lambda b, j: (b, 0, 0))
    const2 = lambda arr: pl.BlockSpec(arr.shape, lambda b, j: (0, 0))
    flat = lambda n: pl.BlockSpec((tm, n), lambda b, j: (b * nT + j, 0))
    h, mi, mp, cnt = pl.pallas_call(
        _route_kernel,
        out_shape=(jax.ShapeDtypeStruct((N, D), BF16), jax.ShapeDtypeStruct((N, 8), jnp.int32),
                   jax.ShapeDtypeStruct((N, 8), F32), jax.ShapeDtypeStruct((n_tok_tiles, 1, LANES), jnp.int32)),
        grid=(B, nT),
        in_specs=[pl.BlockSpec((1, tm, D), lambda b, j: (b, j, 0)), vec, vec, const2(gpre), const2(rw),
                  const2(rb)],
        out_specs=(flat(D), flat(8), flat(8), pl.BlockSpec((1, 1, LANES), lambda b, j: (b * nT + j, 0, 0))),
        compiler_params=_cparams(("arbitrary", "arbitrary")),
        name="moe_route",
    )(x, sc, sh, gpre, rw, rb)

    counts = cnt[:, 0, :E]
    seg_len = (counts + SEG_ALIGN - 1) // SEG_ALIGN * SEG_ALIGN
    group_len = (jnp.sum(seg_len, axis=0) + MOE_ROW_TILE - 1) // MOE_ROW_TILE * MOE_ROW_TILE
    group_end = jnp.cumsum(group_len)
    seg_start = (group_end - group_len)[None, :] + jnp.cumsum(seg_len, axis=0) - seg_len
    seg_start = seg_start.reshape(-1).astype(jnp.int32)
    counts = counts.reshape(-1)
    n_valid = (group_end[-1:] // MOE_ROW_TILE).astype(jnp.int32)
    tile_first_row = jnp.arange(n_row_tiles, dtype=jnp.int32) * MOE_ROW_TILE
    tile_expert = jnp.minimum(jnp.sum(tile_first_row[:, None] >= group_end[None, :], axis=1), E - 1)
    tile_expert = tile_expert.astype(jnp.int32)

    xs = pl.pallas_call(
        _dispatch_kernel,
        out_shape=jax.ShapeDtypeStruct((P, D), BF16),
        grid_spec=pltpu.PrefetchScalarGridSpec(
            num_scalar_prefetch=2,
            grid=(n_tok_tiles,),
            in_specs=[pl.BlockSpec((tm, D), lambda i, seg, n: (i, 0)),
                      pl.BlockSpec((8, tm), lambda i, seg, n: (0, i)),
                      pl.BlockSpec(memory_space=pl.ANY)],
            out_specs=pl.BlockSpec(memory_space=pl.ANY),
            scratch_shapes=[pltpu.VMEM((E * tm, D), BF16), pltpu.SemaphoreType.DMA(())]),
        input_output_aliases={4: 0},
        compiler_params=_cparams(("arbitrary",)),
        name="moe_dispatch",
    )(seg_start, counts, h, mi.T, jnp.zeros((P, D), BF16))

    nf = F // tf
    live = lambda r, f, nv: jnp.where(r < nv[0], f, nf - 1)
    ys = pl.pallas_call(
        _expert_kernel,
        out_shape=jax.ShapeDtypeStruct((P, D), BF16),
        grid_spec=pltpu.PrefetchScalarGridSpec(
            num_scalar_prefetch=2,
            grid=(n_row_tiles, nf),
            in_specs=[pl.BlockSpec((MOE_ROW_TILE, D), lambda r, f, te, nv: (r, 0)),
                      pl.BlockSpec((1, D, tf), lambda r, f, te, nv: (te[r], 0, live(r, f, nv))),
                      pl.BlockSpec((1, D, tf), lambda r, f, te, nv: (te[r], 0, live(r, f, nv))),
                      pl.BlockSpec((1, tf, D), lambda r, f, te, nv: (te[r], live(r, f, nv), 0))],
            out_specs=pl.BlockSpec((MOE_ROW_TILE, D), lambda r, f, te, nv: (r, 0)),
            scratch_shapes=[pltpu.VMEM((MOE_ROW_TILE, D), F32)]),
        compiler_params=_cparams(("arbitrary", "arbitrary")),
        name="moe_experts",
    )(tile_expert, n_valid, xs, wg, wu, wd)

    return pl.pallas_call(
        _combine_kernel,
        out_shape=jax.ShapeDtypeStruct((B, T, D), F32),
        grid_spec=pltpu.PrefetchScalarGridSpec(
            num_scalar_prefetch=1,
            grid=(B, nT),
            in_specs=[pl.BlockSpec(memory_space=pl.ANY),
                      pl.BlockSpec((tm, 8), lambda b, j, seg: (b * nT + j, 0)),
                      pl.BlockSpec((tm, 8), lambda b, j, seg: (b * nT + j, 0)),
                      pl.BlockSpec((1, tm, D), lambda b, j, seg: (b, j, 0)),
                      pl.BlockSpec((1, 1, D), lambda b, j, seg: (b, 0, 0)),
                      pl.BlockSpec(gpost.shape, lambda b, j, seg: (0, 0))],
            out_specs=pl.BlockSpec((1, tm, D), lambda b, j, seg: (b, j, 0)),
            scratch_shapes=[pltpu.VMEM((E * tm, D), BF16), pltpu.SemaphoreType.DMA((E,))]),
        compiler_params=_cparams(("arbitrary", "arbitrary")),
        name="moe_combine",
    )(seg_start, ys, mi, mp, x, gt, gpost)


def _rope_tables(positions, groups):
    d = 32
    inv = 1.0 / (ROPE_THETA ** (jnp.arange(0, d, 2, dtype=F32) / d))
    ang = positions.astype(F32)[..., None] * inv
    cos, sin = jnp.cos(ang), jnp.sin(ang)
    cos = jnp.tile(jnp.concatenate([cos, cos], axis=-1), (1, 1, groups))
    sin = jnp.tile(jnp.concatenate([-sin, sin], axis=-1), (1, 1, groups))
    return cos, sin


def _pad_rows(w, rows, offset):
    out = jnp.zeros((rows, w.shape[1]), w.dtype)
    return out.at[offset:offset + w.shape[0]].set(w)


def kernel(x, c, positions, ada_w, ada_b, norm_mix_pre, norm_mix_post, norm_ffn_pre, norm_ffn_post, w_in, gla_gate_w2, gla_gate_b, gla_norm, diff_lambda, diff_subln, conv_w, conv_b, conv_ln_g, conv_ln_b, rwkv_mu, rwkv_w0, rwkv_w2, rwkv_a0, rwkv_a2, rwkv_g2, rwkv_k_k, rwkv_k_a, rwkv_r_k, rwkv_ln_g, rwkv_ln_b, w_branch, w_out, ffn_w_gate, ffn_w_up, ffn_w_down, router_w, router_b, moe_w_gate, moe_w_up, moe_w_down):
    B, T, D = x.shape
    L = ada_w.shape[0]
    W = D // N_BRANCH
    hk = gla_gate_b.shape[1]
    decay_rank = rwkv_w2.shape[1]
    a_rank = rwkv_a2.shape[1]
    gate_rank = rwkv_g2.shape[1]
    assert decay_rank + a_rank == LANES and 2 * hk == W and gate_rank == LANES
    n_mix = 3 * W + 3 * W + 2 * W + (3 * W + decay_rank + a_rank + gate_rank)
    sizes = (hk, hk, W, W, GLA_GATE_RANK, W, W, W, W, W, 3 * W + LANES + gate_rank, N_BRANCH * D)
    offs = [0]
    for s in sizes:
        offs.append(offs[-1] + s)
    assert offs[-1] == w_in.shape[2]
    tm = min(512, T)
    tb = min(512, T)

    mod = _modulation(c, ada_w, ada_b)
    cos, sin = _rope_tables(positions, W // 32)

    for l in range(L):
        m = mod[l].reshape(B, 1, 6 * D)
        sh_m, sc_m, gt_m, sh_f, sc_f, gt_f = [m[:, :, i * D:(i + 1) * D] for i in range(6)]

        wl = w_in[l]
        gz_cols = jnp.zeros((D, LANES), F32).at[:, :GLA_GATE_RANK].set(wl[:, offs[4]:offs[5]])
        w_mix = jnp.concatenate([wl[:, offs[0]:offs[4]], wl[:, offs[5]:offs[11]], gz_cols], axis=1)
        w_mix = w_mix.astype(BF16)
        w_gate = wl[:, offs[11]:offs[12]].astype(BF16)
        proj = _in_projection(x, sc_m, sh_m, norm_mix_pre[l][None], w_mix, tm)

        w2p = _pad_rows(gla_gate_w2[l], LANES, 0)
        o_gla = _gla(proj, w2p, gla_gate_b[l][None], jnp.tile(gla_norm[l], GLA_HEADS)[None],
                     col_qk=0, col_v=1, col_og=2, col_gz=(n_mix // LANES), tb=tb)
        o_diff = _diff_attention(proj, cos, sin, diff_lambda[l],
                                 diff_subln[l][None], col_q=3, col_k=4, col_v=5, layer_idx=l)
        o_conv = _conformer_conv(proj, _pad_rows(conv_w[l], 32, 0), conv_b[l][None],
                                 conv_ln_g[l][None], conv_ln_b[l][None], col_a=6, col_b=7)
        vecs = dict(mu=rwkv_mu[l][None], w0=rwkv_w0[l][None], a0=rwkv_a0[l][None],
                    kk=rwkv_k_k[l][None], ka=rwkv_k_a[l][None], rk=rwkv_r_k[l].reshape(1, W),
                    lng=rwkv_ln_g[l][None], lnb=rwkv_ln_b[l][None])
        o_rwkv = _rwkv7(proj, vecs, _pad_rows(rwkv_w2[l], LANES, 0).astype(BF16),
                        _pad_rows(rwkv_a2[l], LANES, decay_rank).astype(BF16),
                        rwkv_g2[l].astype(BF16), col=2, tb=tb)
        x = _merge(x, sc_m, sh_m, gt_m, norm_mix_pre[l][None], norm_mix_post[l][None], w_gate,
                   (o_gla, o_diff, o_conv, o_rwkv), w_branch[l].astype(BF16), w_out[l].astype(BF16), tm)

        i = l // 2
        if l % 2 == 0:
            x = _dense_ffn(x, sc_f, sh_f, gt_f, norm_ffn_pre[l][None], norm_ffn_post[l][None],
                           ffn_w_gate[i].astype(BF16), ffn_w_up[i].astype(BF16),
                           ffn_w_down[i].astype(BF16), tm)
        else:
            rw = jnp.zeros((D, LANES), F32).at[:, :N_EXPERTS].set(router_w[i])
            rb = jnp.full((1, LANES), -jnp.inf, F32).at[0, :N_EXPERTS].set(router_b[i])
            x = _moe_ffn(x, sc_f, sh_f, gt_f, norm_ffn_pre[l][None], norm_ffn_post[l][None], rw, rb,
                         moe_w_gate[i].astype(BF16), moe_w_up[i].astype(BF16),
                         moe_w_down[i].astype(BF16), tf=512)
    return x
```

```python
import functools
import math

import jax
import jax.numpy as jnp
from jax import lax
from jax.experimental import pallas as pl
from jax.experimental.pallas import tpu as pltpu

F32 = jnp.float32
BF16 = jnp.bfloat16
HIGHEST = lax.Precision.HIGHEST

N_BRANCH = 4
GLA_HEADS = 4
GLA_GATE_RANK = 16
GLA_GATE_NORMALIZER = 16.0
GLA_CHUNK = 32
DIFF_HEADS = 4
ROPE_THETA = 10000.0
CONV_WIDTH = 31
RWKV_HEADS = 4
RWKV_CHUNK = 64
RWKV_SUB = 16
N_EXPERTS = 8
RMS_EPS = 1e-6
LN_EPS = 1e-5
RWKV_GN_EPS = 64e-5
LANES = 128
SUBLANES = 8
VMEM_LIMIT = 56 * 1024 * 1024


def _cparams(sem):
    return pltpu.CompilerParams(dimension_semantics=sem, vmem_limit_bytes=VMEM_LIMIT)


def _mm(a, b):
    return jnp.dot(a.astype(BF16), b.astype(BF16), preferred_element_type=F32)


def _mm_nt(a, b):
    return lax.dot_general(a.astype(BF16), b.astype(BF16), (((1,), (1,)), ((), ())),
                           preferred_element_type=F32)


def _mm_tn(a, b):
    return lax.dot_general(a.astype(BF16), b.astype(BF16), (((0,), (0,)), ((), ())),
                           preferred_element_type=F32)


def _bmm(a, b):
    return lax.dot_general(a.astype(BF16), b.astype(BF16), (((2,), (1,)), ((0,), (0,))),
                           preferred_element_type=F32)


def _mm_f32(a, b):
    return jnp.dot(a, b, precision=HIGHEST, preferred_element_type=F32)


def _hi_lo(x):
    hi = x.astype(BF16)
    return jnp.concatenate([hi, (x - hi.astype(F32)).astype(BF16)], axis=1)


def _sigmoid(x):
    return 0.5 * jnp.tanh(0.5 * x) + 0.5


def _silu(x):
    return x * _sigmoid(x)


def _softplus(x):
    return jnp.maximum(x, 0.0) + jnp.log(1.0 + jnp.exp(-jnp.abs(x)))


def _group_matrix(n, group):
    r = lax.broadcasted_iota(jnp.int32, (n, n), 0) // group
    c = lax.broadcasted_iota(jnp.int32, (n, n), 1) // group
    return r == c


def _rms_mod(x, gain, scale, shift):
    y = x * lax.rsqrt(jnp.mean(x * x, axis=-1, keepdims=True) + RMS_EPS)
    return y * gain * (1.0 + scale) + shift


def _mod_kernel(c_ref, w_ref, b_ref, o_ref):
    o_ref[0] = _mm_f32(_silu(c_ref[...]), w_ref[0]) + b_ref[0]


def _modulation(c, ada_w, ada_b):
    L, D, M = ada_w.shape
    B = c.shape[0]
    tn = M // 4
    return pl.pallas_call(
        _mod_kernel,
        out_shape=jax.ShapeDtypeStruct((L, B, M), F32),
        grid=(L, M // tn),
        in_specs=[pl.BlockSpec((B, D), lambda l, j: (0, 0)),
                  pl.BlockSpec((1, D, tn), lambda l, j: (l, 0, j)),
                  pl.BlockSpec((1, 1, tn), lambda l, j: (l, 0, j))],
        out_specs=pl.BlockSpec((1, B, tn), lambda l, j: (l, 0, j)),
        compiler_params=_cparams(("arbitrary", "arbitrary")),
        name="adaln_mod",
    )(c, ada_w, ada_b.reshape(L, 1, M))


def _inproj_kernel(x_ref, sc_ref, sh_ref, g_ref, w_ref, o_ref):
    h = _rms_mod(x_ref[0], g_ref[...], sc_ref[0], sh_ref[0])
    o_ref[0] = _mm(h, w_ref[...])


def _in_projection(x, sc, sh, gain, w, tm):
    B, T, D = x.shape
    n = w.shape[1]
    return pl.pallas_call(
        _inproj_kernel,
        out_shape=jax.ShapeDtypeStruct((B, T, n), F32),
        grid=(B, T // tm),
        in_specs=[pl.BlockSpec((1, tm, D), lambda b, i: (b, i, 0)),
                  pl.BlockSpec((1, 1, D), lambda b, i: (b, 0, 0)),
                  pl.BlockSpec((1, 1, D), lambda b, i: (b, 0, 0)),
                  pl.BlockSpec((1, D), lambda b, i: (0, 0)),
                  pl.BlockSpec((D, n), lambda b, i: (0, 0))],
        out_specs=pl.BlockSpec((1, tm, n), lambda b, i: (b, i, 0)),
        compiler_params=_cparams(("arbitrary", "arbitrary")),
        name="in_proj",
    )(x, sc, sh, gain, w)


def _gla_kernel(qk_ref, v_ref, og_ref, gz_ref, w2_ref, gb_ref, ng_ref, o_ref, s_ref, g_ref, r_ref,
                qd_ref, kd_ref, oacc_ref, *, n_chunk):
    C = GLA_CHUNK
    hk = qk_ref.shape[2] // 2
    hv = v_ref.shape[2]
    dk = hk // GLA_HEADS
    dv = hv // GLA_HEADS

    tb = n_chunk * C

    @pl.when(pl.program_id(1) == 0)
    def _():
        s_ref[...] = jnp.zeros_like(s_ref)

    z = _mm(gz_ref[0], w2_ref[...]) + gb_ref[...]
    gk = (jnp.minimum(z, 0.0) - jnp.log(1.0 + jnp.exp(-jnp.abs(z)))) / GLA_GATE_NORMALIZER
    tri = (lax.broadcasted_iota(jnp.int32, (C, C), 1)
           <= lax.broadcasted_iota(jnp.int32, (C, C), 0)).astype(BF16)
    sums = _bmm(jnp.broadcast_to(tri[None], (n_chunk, C, C)), _hi_lo(gk).reshape(n_chunk, C, 2 * hk))
    G3 = sums[:, :, 0:hk] + sums[:, :, hk:2 * hk]
    G_all = G3.reshape(tb, hk)
    R_all = (jnp.broadcast_to(G3[:, C - 1:C, :], (n_chunk, C, hk)) - G3).reshape(tb, hk)
    g_ref[...] = G_all
    r_ref[...] = R_all
    qd_ref[...] = (qk_ref[0, :, 0:hk] * (dk ** -0.5) * jnp.exp(G_all)).astype(BF16)
    kd_ref[...] = (qk_ref[0, :, hk:2 * hk] * jnp.exp(R_all)).astype(BF16)

    causal = (lax.broadcasted_iota(jnp.int32, (C, C, hk), 1)
              <= lax.broadcasted_iota(jnp.int32, (C, C, hk), 0))
    er = lax.broadcasted_iota(jnp.int32, (hk, hv), 0) // dk
    ec = lax.broadcasted_iota(jnp.int32, (hk, hv), 1) // dv
    expand = (er == ec).astype(BF16)
    sr = lax.broadcasted_iota(jnp.int32, (hv, hk), 0) // dv
    scol = lax.broadcasted_iota(jnp.int32, (hv, hk), 1) // dk
    state_mask = sr == scol
    head_mean = (_group_matrix(hv, dv).astype(F32) / dv).astype(BF16)

    def chunk(ci, carry):
        r0 = pl.multiple_of(ci * C, C)
        rows = pl.ds(r0, C)
        q = qk_ref[0, rows, 0:hk] * (dk ** -0.5)
        k = qk_ref[0, rows, hk:2 * hk]
        v = v_ref[0, rows, :]
        G = g_ref[rows, :]
        pair = (C, C, hk)
        g_diff = jnp.broadcast_to(G[:, None, :], pair) - jnp.broadcast_to(G[None], pair)
        decay = jnp.exp(jnp.where(causal, g_diff, -jnp.inf))
        p = jnp.broadcast_to(q[:, None, :], pair) * jnp.broadcast_to(k[None], pair) * decay
        a_exp = jnp.dot(p.reshape(C * C, hk).astype(BF16), expand, preferred_element_type=F32)
        v_j = jnp.broadcast_to(v[None], (C, C, hv))
        o_intra = jnp.sum(a_exp.reshape(C, C, hv) * v_j, axis=1)
        g_total = G[0:1, :] + r_ref[pl.ds(r0, 8), :][0:1, :]
        s = s_ref[...]
        o_inter = lax.dot_general(qd_ref[rows, :], s.astype(BF16), (((1,), (1,)), ((), ())),
                                  preferred_element_type=F32)
        kv = lax.dot_general(v.astype(BF16), kd_ref[rows, :], (((0,), (0,)), ((), ())),
                             preferred_element_type=F32)
        s_ref[...] = s * jnp.exp(g_total) + jnp.where(state_mask, kv, 0.0)
        oacc_ref[rows, :] = o_intra + o_inter
        return carry

    lax.fori_loop(0, n_chunk, chunk, 0)

    o = oacc_ref[...]
    ms = jnp.dot(_hi_lo(o * o), jnp.concatenate([head_mean, head_mean], axis=0),
                 preferred_element_type=F32)
    o = o * lax.rsqrt(ms + RMS_EPS) * ng_ref[...] * _silu(og_ref[0])
    o_ref[0] = o.astype(o_ref.dtype)


def _gla(proj, w2p, gb, ng, col_qk, col_v, col_og, col_gz, tb):
    B, T, _ = proj.shape
    hv = ng.shape[1]
    hk = gb.shape[1]
    return pl.pallas_call(
        functools.partial(_gla_kernel, n_chunk=tb // GLA_CHUNK),
        out_shape=jax.ShapeDtypeStruct((B, T, hv), BF16),
        grid=(B, T // tb),
        in_specs=[pl.BlockSpec((1, tb, 2 * hk), lambda b, j: (b, j, col_qk)),
                  pl.BlockSpec((1, tb, hv), lambda b, j: (b, j, col_v)),
                  pl.BlockSpec((1, tb, hv), lambda b, j: (b, j, col_og)),
                  pl.BlockSpec((1, tb, LANES), lambda b, j: (b, j, col_gz)),
                  pl.BlockSpec(w2p.shape, lambda b, j: (0, 0)),
                  pl.BlockSpec(gb.shape, lambda b, j: (0, 0)),
                  pl.BlockSpec(ng.shape, lambda b, j: (0, 0))],
        out_specs=pl.BlockSpec((1, tb, hv), lambda b, j: (b, j, 0)),
        scratch_shapes=[pltpu.VMEM((hv, hk), F32), pltpu.VMEM((tb, hk), F32), pltpu.VMEM((tb, hk), F32),
                        pltpu.VMEM((tb, hk), BF16), pltpu.VMEM((tb, hk), BF16), pltpu.VMEM((tb, hv), F32)],
        compiler_params=_cparams(("arbitrary", "arbitrary")),
        name="gla_mixer",
    )(proj, proj, proj, proj, w2p, gb, ng)


def _rope(t, cos, sin_signed):
    d = 32
    half = d // 2
    out = []
    for s in range(t.shape[1] // LANES):
        x = t[:, s * LANES:(s + 1) * LANES]
        lane = lax.broadcasted_iota(jnp.int32, x.shape, 1)
        up = pltpu.roll(x, LANES - half, 1)
        down = pltpu.roll(x, half, 1)
        rot = jnp.where((lane % d) < half, up, down)
        out.append(x * cos[:, s * LANES:(s + 1) * LANES] + rot * sin_signed[:, s * LANES:(s + 1) * LANES])
    return jnp.concatenate(out, axis=1)


def _diff_kernel(q_ref, k_ref, v_ref, cosq_ref, sinq_ref, cosk_ref, sink_ref, lam_ref, g_ref, o_ref,
                 ks, vs, *, tq, lam_init):
    H = DIFF_HEADS
    d = q_ref.shape[2] // (2 * H)
    dv = v_ref.shape[2] // H
    j = pl.program_id(1)

    @pl.when(j == 0)
    def _():
        k = _rope(k_ref[0], cosk_ref[0], sink_ref[0])
        v = v_ref[0]
        for hc in range(2 * H):
            ks[hc] = k[:, hc * d:(hc + 1) * d].astype(BF16)
        ones_col = (lax.broadcasted_iota(jnp.int32, (v.shape[0], dv), 1) == 0).astype(BF16)
        for h in range(H):
            vs[h] = jnp.concatenate([v[:, h * dv:(h + 1) * dv].astype(BF16), ones_col], axis=1)

    q = _rope(q_ref[0], cosq_ref[0], sinq_ref[0]) * (d ** -0.5)
    lp = lam_ref[...]
    lam = (jnp.exp(jnp.sum(lp[0:1] * lp[1:2], axis=-1, keepdims=True))
           - jnp.exp(jnp.sum(lp[2:3] * lp[3:4], axis=-1, keepdims=True)) + lam_init)
    on_or_below_diag = (lax.broadcasted_iota(jnp.int32, (tq, tq), 1)
                        <= lax.broadcasted_iota(jnp.int32, (tq, tq), 0))

    def update(qh, kh, vh, m, acc, masked):
        s = lax.dot_general(qh, kh, (((1,), (1,)), ((), ())), preferred_element_type=F32)
        if masked:
            s = jnp.where(on_or_below_diag, s, -jnp.inf)
        m_new = jnp.maximum(m, jnp.max(s, axis=-1, keepdims=True))
        p = jnp.exp(s - m_new).astype(BF16)
        acc = jnp.exp(m - m_new) * acc + jnp.dot(p, vh, preferred_element_type=F32)
        return m_new, acc

    qs = [q[:, hc * d:(hc + 1) * d].astype(BF16) for hc in range(2 * H)]

    def kv_block(kb, carry, masked):
        rows = pl.ds(pl.multiple_of(kb * tq, tq), tq)
        out = []
        for hc in range(2 * H):
            m, acc = carry[2 * hc], carry[2 * hc + 1]
            out.extend(update(qs[hc], ks[hc, rows, :], vs[hc // 2, rows, :], m, acc, masked))
        return tuple(out)

    m0 = jnp.full((tq, 1), -jnp.inf, F32)
    a0 = jnp.zeros((tq, 2 * dv), F32)
    carry = lax.fori_loop(0, j, lambda kb, c: kv_block(kb, c, False), (m0, a0) * (2 * H))
    carry = kv_block(j, carry, True)

    for h in range(H):
        a1, a2 = carry[4 * h + 1], carry[4 * h + 3]
        comp = [a[:, 0:dv] / a[:, dv:dv + 1] for a in (a1, a2)]
        o = comp[0] - lam * comp[1]
        o = o * lax.rsqrt(jnp.mean(o * o, axis=-1, keepdims=True) + RMS_EPS)
        o = o * g_ref[...] * (1.0 - lam_init)
        o_ref[0, :, h * dv:(h + 1) * dv] = o.astype(o_ref.dtype)


def _diff_attention(proj, cos, sin, lam_p, g, col_q, col_k, col_v, layer_idx):
    B, T, _ = proj.shape
    H = DIFF_HEADS
    w = cos.shape[2]
    d = w // (2 * H)
    dv = 2 * d
    tq = min(512, T)
    lam_init = 0.8 - 0.6 * math.exp(-0.3 * layer_idx)
    blk = lambda col: pl.BlockSpec((1, tq, w), lambda b, j: (b, j, col))
    full = lambda col: pl.BlockSpec((1, T, w), lambda b, j: (b, 0, col))
    return pl.pallas_call(
        functools.partial(_diff_kernel, tq=tq, lam_init=lam_init),
        out_shape=jax.ShapeDtypeStruct((B, T, H * dv), BF16),
        grid=(B, T // tq),
        in_specs=[blk(col_q), full(col_k), full(col_v), blk(0), blk(0), full(0), full(0),
                  pl.BlockSpec(lam_p.shape, lambda b, j: (0, 0)),
                  pl.BlockSpec(g.shape, lambda b, j: (0, 0))],
        out_specs=pl.BlockSpec((1, tq, H * dv), lambda b, j: (b, j, 0)),
        scratch_shapes=[pltpu.VMEM((2 * H, T, d), BF16),
                        pltpu.VMEM((H, T, 2 * dv), BF16)],
        compiler_params=_cparams(("arbitrary", "arbitrary")),
        name="diff_attention",
    )(proj, proj, proj, cos, sin, cos, sin, lam_p, g)


def _conv_kernel(a_ref, b_ref, w_ref, cb_ref, lg_ref, lb_ref, o_ref, u_ref, *, rb):
    T = a_ref.shape[1]
    pad = u_ref.shape[0] - T
    u_ref[0:pad, :] = jnp.zeros((pad, u_ref.shape[1]), F32)
    u_ref[pad:pad + T, :] = a_ref[0] * _sigmoid(b_ref[0])
    first = pad - (CONV_WIDTH - 1)

    def block(i, carry):
        r0 = pl.multiple_of(i * rb, rb)
        win = u_ref[pl.ds(r0, rb + pad), :]
        acc = jnp.zeros((rb, u_ref.shape[1]), F32)
        for s in range(SUBLANES):
            taps = [j for j in range(CONV_WIDTH) if (first + j) % SUBLANES == s]
            rolled = win if s == 0 else pltpu.roll(win, rb + pad - s, 0)
            for j in taps:
                a0 = first + j - s
                acc = acc + w_ref[j:j + 1, :] * rolled[a0:a0 + rb, :]
        y = acc + cb_ref[...]
        mu = jnp.mean(y, axis=-1, keepdims=True)
        yc = y - mu
        var = jnp.mean(yc * yc, axis=-1, keepdims=True)
        y = yc * lax.rsqrt(var + LN_EPS) * lg_ref[...] + lb_ref[...]
        o_ref[0, pl.ds(r0, rb), :] = _silu(y).astype(o_ref.dtype)
        return carry

    lax.fori_loop(0, T // rb, block, 0)


def _conformer_conv(proj, w, cb, lg, lb, col_a, col_b):
    B, T, _ = proj.shape
    ch = w.shape[1]
    rb = min(128, T)
    return pl.pallas_call(
        functools.partial(_conv_kernel, rb=rb),
        out_shape=jax.ShapeDtypeStruct((B, T, ch), BF16),
        grid=(B,),
        in_specs=[pl.BlockSpec((1, T, ch), lambda b: (b, 0, col_a)),
                  pl.BlockSpec((1, T, ch), lambda b: (b, 0, col_b)),
                  pl.BlockSpec(w.shape, lambda b: (0, 0)),
                  pl.BlockSpec(cb.shape, lambda b: (0, 0)),
                  pl.BlockSpec(lg.shape, lambda b: (0, 0)),
                  pl.BlockSpec(lb.shape, lambda b: (0, 0))],
        out_specs=pl.BlockSpec((1, T, ch), lambda b: (b, 0, 0)),
        scratch_shapes=[pltpu.VMEM((T + 32, ch), F32)],
        compiler_params=_cparams(("arbitrary",)),
        name="conformer_conv",
    )(proj, proj, w, cb, lg, lb)


def _rwkv_kernel(x_ref, mu_ref, w0_ref, w2_ref, a0_ref, a2_ref, g2_ref, kk_ref, ka_ref, rk_ref,
                 lng_ref, lnb_ref, o_ref, s_ref, prev_ref, gate_ref, bonus_ref, dec_ref, qeff_ref, yloc_ref,
                 mlow_ref, nc_ref, y_ref, *, n_chunk):
    C = RWKV_CHUNK
    H = RWKV_HEADS
    W = o_ref.shape[2]
    N = W // H

    @pl.when(pl.program_id(1) == 0)
    def _():
        s_ref[...] = jnp.zeros_like(s_ref)
        prev_ref[...] = jnp.zeros_like(prev_ref)

    lane = lax.broadcasted_iota(jnp.int32, (1, W), 1)
    head_mask = [(lane // N == h).astype(F32) for h in range(H)]
    block_diag = _group_matrix(W, N)
    head_sum = block_diag.astype(F32)
    ti = lax.broadcasted_iota(jnp.int32, (C, C), 0)
    tj = lax.broadcasted_iota(jnp.int32, (C, C), 1)
    tril_incl = ti >= tj
    tril_strict = ti > tj
    same_sub = (ti // RWKV_SUB) == (tj // RWKV_SUB)
    eye = (ti == tj).astype(F32)
    nc = n_chunk
    tb = nc * C

    def head_total(t, two_term=False):
        ones = head_sum.astype(BF16)
        if two_term:
            return jnp.dot(_hi_lo(t), jnp.concatenate([ones, ones], axis=0), preferred_element_type=F32)
        return jnp.dot(t.astype(BF16), ones, preferred_element_type=F32)

    x = x_ref[0]
    first_row = lax.broadcasted_iota(jnp.int32, (tb, 1), 0) == 0
    prev = jnp.where(first_row, prev_ref[...], pltpu.roll(x, 1, 0))
    prev_ref[...] = x[tb - 1:tb, :]
    xm = x + (prev - x) * mu_ref[...]
    r = xm[:, 0:W]
    k = xm[:, W:2 * W]
    v = xm[:, 2 * W:3 * W]
    zz = xm[:, 3 * W:3 * W + LANES]
    zg = xm[:, 3 * W + LANES:]
    w = -_softplus(-(w0_ref[...] + _mm(jnp.tanh(zz), w2_ref[...]))) - 0.5
    lw = -jnp.exp(w)
    a = _sigmoid(a0_ref[...] + _mm(zz, a2_ref[...]))
    gate_ref[...] = _mm(_sigmoid(zg), g2_ref[...])
    kk = k * kk_ref[...]
    kk = kk / jnp.maximum(jnp.sqrt(head_total(kk * kk, two_term=True)), 1e-12)
    k = k * (1.0 + (a - 1.0) * ka_ref[...])
    b = kk * a
    bonus_ref[...] = head_total(r * k * rk_ref[...]) * v

    c3 = lambda t: t.reshape(nc, C, t.shape[1])
    sums = _bmm(jnp.broadcast_to(tril_incl.astype(BF16)[None], (nc, C, C)), c3(_hi_lo(lw)))
    G3 = sums[:, :, 0:W] + sums[:, :, W:2 * W]
    total3 = jnp.broadcast_to(G3[:, C - 1:C, :], (nc, C, W))
    dec_ref[...] = jnp.exp(total3).reshape(tb, W)
    G = G3.reshape(tb, W)
    inv = jnp.exp(-G)
    to_end = jnp.exp(total3 - G3).reshape(tb, W)
    kap = kk * jnp.exp(G - lw)
    rho = r * jnp.exp(G)
    kap3, rho3, v3 = c3(kap), c3(rho), c3(v)
    bet_kt3 = jnp.concatenate([c3(b * inv), c3(k * inv)], axis=1)
    betc3, kc3 = c3(b * to_end), c3(k * to_end)

    def bmm_nt(p, q):
        return lax.dot_general(p.astype(BF16), q.astype(BF16), (((2,), (2,)), ((0,), (0,))),
                               preferred_element_type=F32)

    def bmm_tn(p, q):
        return lax.dot_general(p.astype(BF16), q.astype(BF16), (((1,), (1,)), ((0,), (0,))),
                               preferred_element_type=F32)

    wi = lax.broadcasted_iota(jnp.int32, (C, 2 * C), 0)
    wj = lax.broadcasted_iota(jnp.int32, (C, 2 * C), 1) % C
    A_bk, B_bk = [], []
    for h in range(H):
        lhs = jnp.concatenate([c3(kap * head_mask[h]), c3(rho * head_mask[h])], axis=1)
        prod = bmm_nt(lhs, bet_kt3)
        A_bk.append(jnp.where(wj < wi, prod[:, 0:C], 0.0))
        B_bk.append(jnp.where(wj <= wi, prod[:, C:2 * C], 0.0))
    A_b = jnp.concatenate([t[:, :, 0:C] for t in A_bk], axis=0)
    B_b = [t[:, :, 0:C] for t in B_bk]
    v3_low = jnp.concatenate([jnp.zeros_like(v3), v3], axis=1)

    Dg = jnp.where(same_sub, A_b, 0.0)
    Lo = A_b - Dg
    D2 = _bmm(Dg, Dg)
    D4 = _bmm(D2, D2)
    D8 = _bmm(D4, D4)
    Dinv = _bmm(_bmm(_bmm(eye - Dg, eye + D2), eye + D4), eye + D8)
    Nn = _bmm(Dinv, Lo)
    N2 = _bmm(Nn, Nn)
    Tm = _bmm(_bmm(eye - Nn, eye + N2), Dinv)
    Tm = [Tm[h * nc:(h + 1) * nc] for h in range(H)]

    def per_head(mats, t):
        reps = t.shape[2] // W
        acc = None
        for h in range(H):
            m = head_mask[h] if reps == 1 else jnp.concatenate([head_mask[h]] * reps, axis=1)
            u = m * _bmm(mats[h], t)
            acc = u if acc is None else acc + u
        return acc

    akv = per_head(A_bk, v3_low)
    tk = per_head(Tm, jnp.concatenate([kap3, akv], axis=2))
    kap_p = tk[:, :, 0:W]
    v_p = tk[:, :, W:2 * W]
    bb = per_head(B_b, jnp.concatenate([kap_p, v_p], axis=2))
    qeff_ref[...] = (rho3 - bb[:, :, 0:W]).astype(BF16)
    yloc_ref[...] = per_head(B_bk, v3_low) - bb[:, :, W:2 * W]
    mlow_ref[...] = jnp.where(block_diag, bmm_tn(kap_p, betc3), 0.0).astype(BF16)
    nc_ref[...] = jnp.where(block_diag, bmm_tn(jnp.concatenate([v3, v_p], axis=1),
                                               jnp.concatenate([kc3, -betc3], axis=1)), 0.0)

    def chunk(ci, carry):
        r0 = pl.multiple_of(ci * C, C)
        s = s_ref[...]
        sb = s.astype(BF16)
        y = lax.dot_general(qeff_ref[ci], sb, (((1,), (1,)), ((), ())), preferred_element_type=F32)
        y_ref[pl.ds(r0, C), :] = y + yloc_ref[ci]
        s_ref[...] = (s * dec_ref[pl.ds(r0, 8), :][0:1, :]
                      - jnp.dot(sb, mlow_ref[ci], preferred_element_type=F32) + nc_ref[ci])
        return carry

    lax.fori_loop(0, nc, chunk, 0)

    y = y_ref[...]
    mean = head_total(y, two_term=True) / N
    yc = y - mean
    var = head_total(yc * yc) / N
    yn = yc * lax.rsqrt(var + RWKV_GN_EPS) * lng_ref[...] + lnb_ref[...]
    o_ref[0] = ((yn + bonus_ref[...]) * gate_ref[...]).astype(o_ref.dtype)


def _rwkv7(proj, vecs, w2p, a2p, g2, col, tb):
    B, T, _ = proj.shape
    W = g2.shape[1]
    cols = vecs["mu"].shape[1]
    names = ("mu", "w0", "w2", "a0", "a2", "g2", "kk", "ka", "rk", "lng", "lnb")
    params = dict(vecs, w2=w2p, a2=a2p, g2=g2)
    const = lambda arr: pl.BlockSpec(arr.shape, lambda b, j: (0, 0))
    n_chunk = tb // RWKV_CHUNK
    return pl.pallas_call(
        functools.partial(_rwkv_kernel, n_chunk=n_chunk),
        out_shape=jax.ShapeDtypeStruct((B, T, W), BF16),
        grid=(B, T // tb),
        in_specs=[pl.BlockSpec((1, tb, cols), lambda b, j: (b, j, col))]
                 + [const(params[n]) for n in names],
        out_specs=pl.BlockSpec((1, tb, W), lambda b, j: (b, j, 0)),
        scratch_shapes=[pltpu.VMEM((W, W), F32), pltpu.VMEM((1, cols), F32),
                        pltpu.VMEM((tb, W), F32), pltpu.VMEM((tb, W), F32), pltpu.VMEM((tb, W), F32),
                        pltpu.VMEM((n_chunk, RWKV_CHUNK, W), BF16), pltpu.VMEM((n_chunk, RWKV_CHUNK, W), F32),
                        pltpu.VMEM((n_chunk, W, W), BF16), pltpu.VMEM((n_chunk, W, W), F32),
                        pltpu.VMEM((tb, W), F32)],
        compiler_params=_cparams(("arbitrary", "arbitrary")),
        name="rwkv7_mixer",
    )(proj, *[params[n] for n in names])


def _merge_kernel(x_ref, sc_ref, sh_ref, gt_ref, gpre_ref, gpost_ref, wg_ref, b0_ref, b1_ref, b2_ref,
                  b3_ref, wb_ref, wo_ref, o_ref):
    x = x_ref[0]
    D = x.shape[1]
    h = _rms_mod(x, gpre_ref[...], sc_ref[0], sh_ref[0]).astype(BF16)
    merged = None
    for g, br in enumerate((b0_ref, b1_ref, b2_ref, b3_ref)):
        gate = _sigmoid(jnp.dot(h, wg_ref[:, g * D:(g + 1) * D], preferred_element_type=F32))
        t = gate * jnp.dot(br[0], wb_ref[g], preferred_element_type=F32)
        merged = t if merged is None else merged + t
    y = _mm(merged, wo_ref[...])
    y = y * lax.rsqrt(jnp.mean(y * y, axis=-1, keepdims=True) + RMS_EPS) * gpost_ref[...]
    o_ref[0] = x + gt_ref[0] * y


def _merge(x, sc, sh, gt, gpre, gpost, w_gate, branches, w_branch, w_out, tm):
    B, T, D = x.shape
    bw = branches[0].shape[2]
    tok = lambda n: pl.BlockSpec((1, tm, n), lambda b, i: (b, i, 0))
    vec = pl.BlockSpec((1, 1, D), lambda b, i: (b, 0, 0))
    const2 = lambda arr: pl.BlockSpec(arr.shape, lambda b, i: (0, 0))
    return pl.pallas_call(
        _merge_kernel,
        out_shape=jax.ShapeDtypeStruct((B, T, D), F32),
        grid=(B, T // tm),
        in_specs=[tok(D), vec, vec, vec, const2(gpre), const2(gpost), const2(w_gate),
                  tok(bw), tok(bw), tok(bw), tok(bw),
                  pl.BlockSpec(w_branch.shape, lambda b, i: (0, 0, 0)), const2(w_out)],
        out_specs=tok(D),
        compiler_params=_cparams(("arbitrary", "arbitrary")),
        name="merge_out_proj",
    )(x, sc, sh, gt, gpre, gpost, w_gate, *branches, w_branch, w_out)


def _ffn_kernel(x_ref, sc_ref, sh_ref, gt_ref, gpre_ref, gpost_ref, wg_ref, wu_ref, wd_ref, o_ref, *, tf):
    x = x_ref[0]
    h = _rms_mod(x, gpre_ref[...], sc_ref[0], sh_ref[0]).astype(BF16)
    F = wg_ref.shape[1]
    acc = None
    for f0 in range(0, F, tf):
        f1 = min(f0 + tf, F)
        gate = jnp.dot(h, wg_ref[:, f0:f1], preferred_element_type=F32)
        up = jnp.dot(h, wu_ref[:, f0:f1], preferred_element_type=F32)
        t = jnp.dot((_silu(gate) * up).astype(BF16), wd_ref[f0:f1, :], preferred_element_type=F32)
        acc = t if acc is None else acc + t
    y = acc * lax.rsqrt(jnp.mean(acc * acc, axis=-1, keepdims=True) + RMS_EPS) * gpost_ref[...]
    o_ref[0] = x + gt_ref[0] * y


def _dense_ffn(x, sc, sh, gt, gpre, gpost, wg, wu, wd, tm):
    B, T, D = x.shape
    tok = pl.BlockSpec((1, tm, D), lambda b, i: (b, i, 0))
    vec = pl.BlockSpec((1, 1, D), lambda b, i: (b, 0, 0))
    const2 = lambda arr: pl.BlockSpec(arr.shape, lambda b, i: (0, 0))
    return pl.pallas_call(
        functools.partial(_ffn_kernel, tf=512),
        out_shape=jax.ShapeDtypeStruct((B, T, D), F32),
        grid=(B, T // tm),
        in_specs=[tok, vec, vec, vec, const2(gpre), const2(gpost), const2(wg), const2(wu), const2(wd)],
        out_specs=tok,
        compiler_params=_cparams(("arbitrary", "arbitrary")),
        name="dense_swiglu",
    )(x, sc, sh, gt, gpre, gpost, wg, wu, wd)


MOE_TOKEN_TILE = 256
MOE_ROW_TILE = 512
SEG_ALIGN = 16
SEG_PIECES = (256, 128, 64, 32, 16)


def _route_kernel(x_ref, sc_ref, sh_ref, gpre_ref, rw_ref, rb_ref, h_ref, mi_ref, mp_ref, cnt_ref):
    tm = x_ref.shape[1]
    h = _rms_mod(x_ref[0], gpre_ref[...], sc_ref[0], sh_ref[0])
    h_ref[...] = h.astype(BF16)
    logits = _mm_f32(h, rw_ref[...]) + rb_ref[...]
    lane = lax.broadcasted_iota(jnp.int32, logits.shape, 1)
    v1 = jnp.max(logits, axis=-1, keepdims=True)
    i1 = jnp.min(jnp.where(logits == v1, lane, LANES), axis=-1, keepdims=True)
    rest = jnp.where(lane == i1, -jnp.inf, logits)
    v2 = jnp.max(rest, axis=-1, keepdims=True)
    i2 = jnp.min(jnp.where(rest == v2, lane, LANES), axis=-1, keepdims=True)
    e2 = jnp.exp(v2 - v1)
    p1 = 1.0 / (1.0 + e2)
    p2 = e2 / (1.0 + e2)
    oh1 = (lane == i1).astype(F32)
    oh2 = (lane == i2).astype(F32)
    both = oh1 + oh2
    earlier = (lax.broadcasted_iota(jnp.int32, (tm, tm), 1)
               < lax.broadcasted_iota(jnp.int32, (tm, tm), 0)).astype(BF16)
    before = jnp.dot(earlier, both.astype(BF16), preferred_element_type=F32)
    r1 = jnp.sum(oh1 * before, axis=-1, keepdims=True).astype(jnp.int32)
    r2 = jnp.sum(oh2 * before, axis=-1, keepdims=True).astype(jnp.int32)
    col = lax.broadcasted_iota(jnp.int32, mi_ref.shape, 1)
    mi_ref[...] = jnp.where(col == 0, i1, jnp.where(col == 1, i2, jnp.where(col == 2, r1,
                                                                           jnp.where(col == 3, r2, 0))))
    mp_ref[...] = jnp.where(col == 0, p1, jnp.where(col == 1, p2, 0.0))
    cnt_ref[0] = jnp.sum(both, axis=0, keepdims=True).astype(jnp.int32)


def _segment_pieces(n_rows):
    out = []
    for s in SEG_PIECES:
        if s == MOE_TOKEN_TILE:
            out.append((n_rows == s, 0, s))
        else:
            out.append(((n_rows & s) != 0, pl.multiple_of((n_rows // (2 * s)) * (2 * s), SEG_ALIGN), s))
    return out


def _dispatch_kernel(seg_ref, cnt_ref, h_ref, mit_ref, init_ref, xs_ref, buf_ref, sem):
    del init_ref
    tm = h_ref.shape[0]
    i = pl.program_id(0)
    e1, e2 = mit_ref[0:1, :], mit_ref[1:2, :]
    r1, r2 = mit_ref[2:3, :], mit_ref[3:4, :]
    row = lax.broadcasted_iota(jnp.int32, (tm, tm), 0)
    select = jnp.concatenate(
        [jnp.logical_or(jnp.logical_and(e1 == e, r1 == row), jnp.logical_and(e2 == e, r2 == row))
         for e in range(N_EXPERTS)], axis=0).astype(BF16)
    slot = i % 2
    buf_ref[slot] = jnp.dot(select, h_ref[...], preferred_element_type=F32).astype(BF16)

    def segment_copies(tile, buf, e):
        n = cnt_ref[tile * N_EXPERTS + e]
        n_rows = ((n + SEG_ALIGN - 1) // SEG_ALIGN) * SEG_ALIGN
        dst = pl.multiple_of(seg_ref[tile * N_EXPERTS + e], SEG_ALIGN)
        return [(cond, pltpu.make_async_copy(buf_ref.at[buf, pl.ds(e * tm + off, s), :],
                                             xs_ref.at[pl.ds(dst + off, s), :], sem))
                for cond, off, s in _segment_pieces(n_rows)]

    def for_all_segments(tile, buf, action):
        for e in range(N_EXPERTS):
            for cond, cp in segment_copies(tile, buf, e):
                pl.when(cond)(getattr(cp, action))

    @pl.when(i > 0)
    def _():
        for_all_segments(i - 1, 1 - slot, "wait")

    for_all_segments(i, slot, "start")

    @pl.when(i == pl.num_programs(0) - 1)
    def _():
        for_all_segments(i, slot, "wait")


def _expert_kernel(te_ref, nv_ref, xs_ref, wg_ref, wu_ref, wd_ref, ys_ref, acc_ref):
    del te_ref
    r = pl.program_id(0)
    f = pl.program_id(1)
    valid = r < nv_ref[0]

    @pl.when(jnp.logical_and(valid, f == 0))
    def _():
        acc_ref[...] = jnp.zeros_like(acc_ref)

    @pl.when(valid)
    def _():
        x = xs_ref[...]
        gate = jnp.dot(x, wg_ref[0], preferred_element_type=F32)
        up = jnp.dot(x, wu_ref[0], preferred_element_type=F32)
        acc_ref[...] += jnp.dot((_silu(gate) * up).astype(BF16), wd_ref[0], preferred_element_type=F32)

    @pl.when(f == pl.num_programs(1) - 1)
    def _():
        @pl.when(valid)
        def _():
            ys_ref[...] = acc_ref[...].astype(ys_ref.dtype)

        @pl.when(jnp.logical_not(valid))
        def _():
            ys_ref[...] = jnp.zeros_like(ys_ref)


def _combine_kernel(seg_ref, ys_ref, mi_ref, mp_ref, x_ref, gt_ref, gpost_ref, o_ref, win_ref, sem):
    tm = x_ref.shape[1]
    i = pl.program_id(0) * pl.num_programs(1) + pl.program_id(1)
    n_tiles = pl.num_programs(0) * pl.num_programs(1)
    slot = i % 2

    def window_copy(tile, buf, e):
        src = pl.multiple_of(seg_ref[tile * N_EXPERTS + e], SEG_ALIGN)
        return pltpu.make_async_copy(ys_ref.at[pl.ds(src, tm), :], win_ref.at[buf, pl.ds(e * tm, tm), :],
                                     sem.at[buf, e])

    @pl.when(i == 0)
    def _():
        for e in range(N_EXPERTS):
            window_copy(i, slot, e).start()

    @pl.when(i + 1 < n_tiles)
    def _():
        for e in range(N_EXPERTS):
            window_copy(i + 1, 1 - slot, e).start()

    e1, e2 = mi_ref[:, 0:1], mi_ref[:, 1:2]
    r1, r2 = mi_ref[:, 2:3], mi_ref[:, 3:4]
    col = lax.broadcasted_iota(jnp.int32, (tm, tm), 1)
    acc = None
    for e in range(N_EXPERTS):
        expand = jnp.concatenate([jnp.logical_and(e1 == e, r1 == col),
                                  jnp.logical_and(e2 == e, r2 == col)], axis=0).astype(BF16)
        window_copy(i, slot, e).wait()
        t = jnp.dot(expand, win_ref[slot, pl.ds(e * tm, tm), :], preferred_element_type=F32)
        acc = t if acc is None else acc + t
    y = mp_ref[:, 0:1] * acc[0:tm] + mp_ref[:, 1:2] * acc[tm:2 * tm]
    y = y * lax.rsqrt(jnp.mean(y * y, axis=-1, keepdims=True) + RMS_EPS) * gpost_ref[...]
    o_ref[0] = x_ref[0] + gt_ref[0] * y


def _moe_ffn(x, sc, sh, gt, gpre, gpost, rw, rb, wg, wu, wd, tf):
    B, T, D = x.shape
    E, _, F = wg.shape
    tm = min(MOE_TOKEN_TILE, T)
    assert tm == MOE_TOKEN_TILE and E == N_EXPERTS
    nT = T // tm
    n_tok_tiles = B * nT
    N = B * T
    max_rows = 2 * N + n_tok_tiles * E * (SEG_ALIGN - 1) + E * (MOE_ROW_TILE - SEG_ALIGN)
    n_row_tiles = -(-max_rows // MOE_ROW_TILE) + 1
    P = n_row_tiles * MOE_ROW_TILE

    vec = pl.BlockSpec((1, 1, D), lambda b, j: (b, 0, 0))
    const2 = lambda arr: pl.BlockSpec(arr.shape, lambda b, j: (0, 0))
    flat = lambda n: pl.BlockSpec((tm, n), lambda b, j: (b * nT + j, 0))
    h, mi, mp, cnt = pl.pallas_call(
        _route_kernel,
        out_shape=(jax.ShapeDtypeStruct((N, D), BF16), jax.ShapeDtypeStruct((N, 8), jnp.int32),
                   jax.ShapeDtypeStruct((N, 8), F32), jax.ShapeDtypeStruct((n_tok_tiles, 1, LANES), jnp.int32)),
        grid=(B, nT),
        in_specs=[pl.BlockSpec((1, tm, D), lambda b, j: (b, j, 0)), vec, vec, const2(gpre), const2(rw),
                  const2(rb)],
        out_specs=(flat(D), flat(8), flat(8), pl.BlockSpec((1, 1, LANES), lambda b, j: (b * nT + j, 0, 0))),
        compiler_params=_cparams(("arbitrary", "arbitrary")),
        name="moe_route",
    )(x, sc, sh, gpre, rw, rb)

    counts = cnt[:, 0, :E]
    seg_len = (counts + SEG_ALIGN - 1) // SEG_ALIGN * SEG_ALIGN
    group_len = (jnp.sum(seg_len, axis=0) + MOE_ROW_TILE - 1) // MOE_ROW_TILE * MOE_ROW_TILE
    group_end = jnp.cumsum(group_len)
    seg_start = (group_end - group_len)[None, :] + jnp.cumsum(seg_len, axis=0) - seg_len
    seg_start = seg_start.reshape(-1).astype(jnp.int32)
    counts = counts.reshape(-1)
    n_valid = (group_end[-1:] // MOE_ROW_TILE).astype(jnp.int32)
    tile_first_row = jnp.arange(n_row_tiles, dtype=jnp.int32) * MOE_ROW_TILE
    tile_expert = jnp.minimum(jnp.sum(tile_first_row[:, None] >= group_end[None, :], axis=1), E - 1)
    tile_expert = tile_expert.astype(jnp.int32)

    xs = pl.pallas_call(
        _dispatch_kernel,
        out_shape=jax.ShapeDtypeStruct((P, D), BF16),
        grid_spec=pltpu.PrefetchScalarGridSpec(
            num_scalar_prefetch=2,
            grid=(n_tok_tiles,),
            in_specs=[pl.BlockSpec((tm, D), lambda i, seg, n: (i, 0)),
                      pl.BlockSpec((8, tm), lambda i, seg, n: (0, i)),
                      pl.BlockSpec(memory_space=pl.ANY)],
            out_specs=pl.BlockSpec(memory_space=pl.ANY),
            scratch_shapes=[pltpu.VMEM((2, E * tm, D), BF16), pltpu.SemaphoreType.DMA(())]),
        input_output_aliases={4: 0},
        compiler_params=_cparams(("arbitrary",)),
        name="moe_dispatch",
    )(seg_start, counts, h, mi.T, jnp.zeros((P, D), BF16))

    nf = F // tf
    live = lambda r, f, nv: jnp.where(r < nv[0], f, nf - 1)
    ys = pl.pallas_call(
        _expert_kernel,
        out_shape=jax.ShapeDtypeStruct((P, D), BF16),
        grid_spec=pltpu.PrefetchScalarGridSpec(
            num_scalar_prefetch=2,
            grid=(n_row_tiles, nf),
            in_specs=[pl.BlockSpec((MOE_ROW_TILE, D), lambda r, f, te, nv: (r, 0)),
                      pl.BlockSpec((1, D, tf), lambda r, f, te, nv: (te[r], 0, live(r, f, nv))),
                      pl.BlockSpec((1, D, tf), lambda r, f, te, nv: (te[r], 0, live(r, f, nv))),
                      pl.BlockSpec((1, tf, D), lambda r, f, te, nv: (te[r], live(r, f, nv), 0))],
            out_specs=pl.BlockSpec((MOE_ROW_TILE, D), lambda r, f, te, nv: (r, 0)),
            scratch_shapes=[pltpu.VMEM((MOE_ROW_TILE, D), F32)]),
        compiler_params=_cparams(("arbitrary", "arbitrary")),
        name="moe_experts",
    )(tile_expert, n_valid, xs, wg, wu, wd)

    return pl.pallas_call(
        _combine_kernel,
        out_shape=jax.ShapeDtypeStruct((B, T, D), F32),
        grid_spec=pltpu.PrefetchScalarGridSpec(
            num_scalar_prefetch=1,
            grid=(B, nT),
            in_specs=[pl.BlockSpec(memory_space=pl.ANY),
                      pl.BlockSpec((tm, 8), lambda b, j, seg: (b * nT + j, 0)),
                      pl.BlockSpec((tm, 8), lambda b, j, seg: (b * nT + j, 0)),
                      pl.BlockSpec((1, tm, D), lambda b, j, seg: (b, j, 0)),
                      pl.BlockSpec((1, 1, D), lambda b, j, seg: (b, 0, 0)),
                      pl.BlockSpec(gpost.shape, lambda b, j, seg: (0, 0))],
            out_specs=pl.BlockSpec((1, tm, D), lambda b, j, seg: (b, j, 0)),
            scratch_shapes=[pltpu.VMEM((2, E * tm, D), BF16), pltpu.SemaphoreType.DMA((2, E))]),
        compiler_params=_cparams(("arbitrary", "arbitrary")),
        name="moe_combine",
    )(seg_start, ys, mi, mp, x, gt, gpost)


def _rope_tables(positions, groups):
    d = 32
    inv = 1.0 / (ROPE_THETA ** (jnp.arange(0, d, 2, dtype=F32) / d))
    ang = positions.astype(F32)[..., None] * inv
    cos, sin = jnp.cos(ang), jnp.sin(ang)
    cos = jnp.tile(jnp.concatenate([cos, cos], axis=-1), (1, 1, groups))
    sin = jnp.tile(jnp.concatenate([-sin, sin], axis=-1), (1, 1, groups))
    return cos, sin


def _pad_rows(w, rows, offset):
    out = jnp.zeros((rows, w.shape[1]), w.dtype)
    return out.at[offset:offset + w.shape[0]].set(w)


def kernel(x, c, positions, ada_w, ada_b, norm_mix_pre, norm_mix_post, norm_ffn_pre, norm_ffn_post, w_in, gla_gate_w2, gla_gate_b, gla_norm, diff_lambda, diff_subln, conv_w, conv_b, conv_ln_g, conv_ln_b, rwkv_mu, rwkv_w0, rwkv_w2, rwkv_a0, rwkv_a2, rwkv_g2, rwkv_k_k, rwkv_k_a, rwkv_r_k, rwkv_ln_g, rwkv_ln_b, w_branch, w_out, ffn_w_gate, ffn_w_up, ffn_w_down, router_w, router_b, moe_w_gate, moe_w_up, moe_w_down):
    B, T, D = x.shape
    L = ada_w.shape[0]
    W = D // N_BRANCH
    hk = gla_gate_b.shape[1]
    decay_rank = rwkv_w2.shape[1]
    a_rank = rwkv_a2.shape[1]
    gate_rank = rwkv_g2.shape[1]
    assert decay_rank + a_rank == LANES and 2 * hk == W and gate_rank == LANES
    n_mix = 3 * W + 3 * W + 2 * W + (3 * W + decay_rank + a_rank + gate_rank)
    sizes = (hk, hk, W, W, GLA_GATE_RANK, W, W, W, W, W, 3 * W + LANES + gate_rank, N_BRANCH * D)
    offs = [0]
    for s in sizes:
        offs.append(offs[-1] + s)
    assert offs[-1] == w_in.shape[2]
    tm = min(512, T)
    tb = min(512, T)

    mod = _modulation(c, ada_w, ada_b)
    cos, sin = _rope_tables(positions, W // 32)

    for l in range(L):
        m = mod[l].reshape(B, 1, 6 * D)
        sh_m, sc_m, gt_m, sh_f, sc_f, gt_f = [m[:, :, i * D:(i + 1) * D] for i in range(6)]

        wl = w_in[l]
        gz_cols = jnp.zeros((D, LANES), F32).at[:, :GLA_GATE_RANK].set(wl[:, offs[4]:offs[5]])
        w_mix = jnp.concatenate([wl[:, offs[0]:offs[4]], wl[:, offs[5]:offs[11]], gz_cols], axis=1)
        w_mix = w_mix.astype(BF16)
        w_gate = wl[:, offs[11]:offs[12]].astype(BF16)
        proj = _in_projection(x, sc_m, sh_m, norm_mix_pre[l][None], w_mix, tm)

        w2p = _pad_rows(gla_gate_w2[l], LANES, 0)
        o_gla = _gla(proj, w2p, gla_gate_b[l][None], jnp.tile(gla_norm[l], GLA_HEADS)[None],
                     col_qk=0, col_v=1, col_og=2, col_gz=(n_mix // LANES), tb=tb)
        o_diff = _diff_attention(proj, cos, sin, diff_lambda[l],
                                 diff_subln[l][None], col_q=3, col_k=4, col_v=5, layer_idx=l)
        o_conv = _conformer_conv(proj, _pad_rows(conv_w[l], 32, 0), conv_b[l][None],
                                 conv_ln_g[l][None], conv_ln_b[l][None], col_a=6, col_b=7)
        vecs = dict(mu=rwkv_mu[l][None], w0=rwkv_w0[l][None], a0=rwkv_a0[l][None],
                    kk=rwkv_k_k[l][None], ka=rwkv_k_a[l][None], rk=rwkv_r_k[l].reshape(1, W),
                    lng=rwkv_ln_g[l][None], lnb=rwkv_ln_b[l][None])
        o_rwkv = _rwkv7(proj, vecs, _pad_rows(rwkv_w2[l], LANES, 0).astype(BF16),
                        _pad_rows(rwkv_a2[l], LANES, decay_rank).astype(BF16),
                        rwkv_g2[l].astype(BF16), col=2, tb=tb)
        x = _merge(x, sc_m, sh_m, gt_m, norm_mix_pre[l][None], norm_mix_post[l][None], w_gate,
                   (o_gla, o_diff, o_conv, o_rwkv), w_branch[l].astype(BF16), w_out[l].astype(BF16), tm)

        i = l // 2
        if l % 2 == 0:
            x = _dense_ffn(x, sc_f, sh_f, gt_f, norm_ffn_pre[l][None], norm_ffn_post[l][None],
                           ffn_w_gate[i].astype(BF16), ffn_w_up[i].astype(BF16),
                           ffn_w_down[i].astype(BF16), tm)
        else:
            rw = jnp.zeros((D, LANES), F32).at[:, :N_EXPERTS].set(router_w[i])
            rb = jnp.full((1, LANES), -jnp.inf, F32).at[0, :N_EXPERTS].set(router_b[i])
            x = _moe_ffn(x, sc_f, sh_f, gt_f, norm_ffn_pre[l][None], norm_ffn_post[l][None], rw, rb,
                         moe_w_gate[i].astype(BF16), moe_w_up[i].astype(BF16),
                         moe_w_down[i].astype(BF16), tf=512)
    return x
```

```python
import functools
import math

import jax
import jax.numpy as jnp
from jax import lax
from jax.experimental import pallas as pl
from jax.experimental.pallas import tpu as pltpu

F32 = jnp.float32
BF16 = jnp.bfloat16
HIGHEST = lax.Precision.HIGHEST

N_BRANCH = 4
GLA_HEADS = 4
GLA_GATE_RANK = 16
GLA_GATE_NORMALIZER = 16.0
GLA_CHUNK = 32
DIFF_HEADS = 4
ROPE_THETA = 10000.0
CONV_WIDTH = 31
RWKV_HEADS = 4
RWKV_CHUNK = 64
RWKV_SUB = 16
N_EXPERTS = 8
RMS_EPS = 1e-6
LN_EPS = 1e-5
RWKV_GN_EPS = 64e-5
LANES = 128
SUBLANES = 8
VMEM_LIMIT = 56 * 1024 * 1024


def _cparams(sem):
    return pltpu.CompilerParams(dimension_semantics=sem, vmem_limit_bytes=VMEM_LIMIT)


def _mm(a, b):
    return jnp.dot(a.astype(BF16), b.astype(BF16), preferred_element_type=F32)


def _mm_nt(a, b):
    return lax.dot_general(a.astype(BF16), b.astype(BF16), (((1,), (1,)), ((), ())),
                           preferred_element_type=F32)


def _mm_tn(a, b):
    return lax.dot_general(a.astype(BF16), b.astype(BF16), (((0,), (0,)), ((), ())),
                           preferred_element_type=F32)


def _bmm(a, b):
    return lax.dot_general(a.astype(BF16), b.astype(BF16), (((2,), (1,)), ((0,), (0,))),
                           preferred_element_type=F32)


def _mm_f32(a, b):
    return jnp.dot(a, b, precision=HIGHEST, preferred_element_type=F32)


def _hi_lo(x):
    hi = x.astype(BF16)
    return jnp.concatenate([hi, (x - hi.astype(F32)).astype(BF16)], axis=1)


def _sigmoid(x):
    return 0.5 * jnp.tanh(0.5 * x) + 0.5


def _silu(x):
    return x * _sigmoid(x)


def _softplus(x):
    return jnp.maximum(x, 0.0) + jnp.log(1.0 + jnp.exp(-jnp.abs(x)))


def _group_matrix(n, group):
    r = lax.broadcasted_iota(jnp.int32, (n, n), 0) // group
    c = lax.broadcasted_iota(jnp.int32, (n, n), 1) // group
    return r == c


def _rms_mod(x, gain, scale, shift):
    y = x * lax.rsqrt(jnp.mean(x * x, axis=-1, keepdims=True) + RMS_EPS)
    return y * gain * (1.0 + scale) + shift


def _mod_kernel(c_ref, w_ref, b_ref, o_ref):
    o_ref[0] = _mm_f32(_silu(c_ref[...]), w_ref[0]) + b_ref[0]


def _modulation(c, ada_w, ada_b):
    L, D, M = ada_w.shape
    B = c.shape[0]
    tn = M // 4
    return pl.pallas_call(
        _mod_kernel,
        out_shape=jax.ShapeDtypeStruct((L, B, M), F32),
        grid=(L, M // tn),
        in_specs=[pl.BlockSpec((B, D), lambda l, j: (0, 0)),
                  pl.BlockSpec((1, D, tn), lambda l, j: (l, 0, j)),
                  pl.BlockSpec((1, 1, tn), lambda l, j: (l, 0, j))],
        out_specs=pl.BlockSpec((1, B, tn), lambda l, j: (l, 0, j)),
        compiler_params=_cparams(("arbitrary", "arbitrary")),
        name="adaln_mod",
    )(c, ada_w, ada_b.reshape(L, 1, M))


def _inproj_kernel(x_ref, sc_ref, sh_ref, g_ref, w_ref, o_ref):
    h = _rms_mod(x_ref[0], g_ref[...], sc_ref[0], sh_ref[0])
    o_ref[0] = _mm(h, w_ref[...])


def _in_projection(x, sc, sh, gain, w, tm):
    B, T, D = x.shape
    n = w.shape[1]
    return pl.pallas_call(
        _inproj_kernel,
        out_shape=jax.ShapeDtypeStruct((B, T, n), F32),
        grid=(B, T // tm),
        in_specs=[pl.BlockSpec((1, tm, D), lambda b, i: (b, i, 0)),
                  pl.BlockSpec((1, 1, D), lambda b, i: (b, 0, 0)),
                  pl.BlockSpec((1, 1, D), lambda b, i: (b, 0, 0)),
                  pl.BlockSpec((1, D), lambda b, i: (0, 0)),
                  pl.BlockSpec((D, n), lambda b, i: (0, 0))],
        out_specs=pl.BlockSpec((1, tm, n), lambda b, i: (b, i, 0)),
        compiler_params=_cparams(("arbitrary", "arbitrary")),
        name="in_proj",
    )(x, sc, sh, gain, w)


def _gla_kernel(qk_ref, v_ref, og_ref, gz_ref, w2_ref, gb_ref, ng_ref, o_ref, s_ref, g_ref, r_ref,
                qd_ref, kd_ref, oacc_ref, *, n_chunk):
    C = GLA_CHUNK
    hk = qk_ref.shape[2] // 2
    hv = v_ref.shape[2]
    dk = hk // GLA_HEADS
    dv = hv // GLA_HEADS

    tb = n_chunk * C

    @pl.when(pl.program_id(1) == 0)
    def _():
        s_ref[...] = jnp.zeros_like(s_ref)

    z = _mm(gz_ref[0], w2_ref[...]) + gb_ref[...]
    gk = (jnp.minimum(z, 0.0) - jnp.log(1.0 + jnp.exp(-jnp.abs(z)))) / GLA_GATE_NORMALIZER
    tri = (lax.broadcasted_iota(jnp.int32, (C, C), 1)
           <= lax.broadcasted_iota(jnp.int32, (C, C), 0)).astype(BF16)
    sums = _bmm(jnp.broadcast_to(tri[None], (n_chunk, C, C)), _hi_lo(gk).reshape(n_chunk, C, 2 * hk))
    G3 = sums[:, :, 0:hk] + sums[:, :, hk:2 * hk]
    G_all = G3.reshape(tb, hk)
    R_all = (jnp.broadcast_to(G3[:, C - 1:C, :], (n_chunk, C, hk)) - G3).reshape(tb, hk)
    g_ref[...] = G_all
    r_ref[...] = R_all
    qd_ref[...] = (qk_ref[0, :, 0:hk] * (dk ** -0.5) * jnp.exp(G_all)).astype(BF16)
    kd_ref[...] = (qk_ref[0, :, hk:2 * hk] * jnp.exp(R_all)).astype(BF16)

    causal = (lax.broadcasted_iota(jnp.int32, (C, C, hk), 1)
              <= lax.broadcasted_iota(jnp.int32, (C, C, hk), 0))
    er = lax.broadcasted_iota(jnp.int32, (hk, hv), 0) // dk
    ec = lax.broadcasted_iota(jnp.int32, (hk, hv), 1) // dv
    expand = (er == ec).astype(BF16)
    sr = lax.broadcasted_iota(jnp.int32, (hv, hk), 0) // dv
    scol = lax.broadcasted_iota(jnp.int32, (hv, hk), 1) // dk
    state_mask = sr == scol
    head_mean = (_group_matrix(hv, dv).astype(F32) / dv).astype(BF16)

    def chunk(ci, carry):
        r0 = pl.multiple_of(ci * C, C)
        rows = pl.ds(r0, C)
        q = qk_ref[0, rows, 0:hk] * (dk ** -0.5)
        k = qk_ref[0, rows, hk:2 * hk]
        v = v_ref[0, rows, :]
        G = g_ref[rows, :]
        pair = (C, C, hk)
        g_diff = jnp.broadcast_to(G[:, None, :], pair) - jnp.broadcast_to(G[None], pair)
        decay = jnp.exp(jnp.where(causal, g_diff, -jnp.inf))
        p = jnp.broadcast_to(q[:, None, :], pair) * jnp.broadcast_to(k[None], pair) * decay
        a_exp = jnp.dot(p.reshape(C * C, hk).astype(BF16), expand, preferred_element_type=F32)
        v_j = jnp.broadcast_to(v[None], (C, C, hv))
        o_intra = jnp.sum(a_exp.reshape(C, C, hv) * v_j, axis=1)
        g_total = G[0:1, :] + r_ref[pl.ds(r0, 8), :][0:1, :]
        s = s_ref[...]
        o_inter = lax.dot_general(qd_ref[rows, :], s.astype(BF16), (((1,), (1,)), ((), ())),
                                  preferred_element_type=F32)
        kv = lax.dot_general(v.astype(BF16), kd_ref[rows, :], (((0,), (0,)), ((), ())),
                             preferred_element_type=F32)
        s_ref[...] = s * jnp.exp(g_total) + jnp.where(state_mask, kv, 0.0)
        oacc_ref[rows, :] = o_intra + o_inter
        return carry

    lax.fori_loop(0, n_chunk, chunk, 0)

    o = oacc_ref[...]
    ms = jnp.dot(_hi_lo(o * o), jnp.concatenate([head_mean, head_mean], axis=0),
                 preferred_element_type=F32)
    o = o * lax.rsqrt(ms + RMS_EPS) * ng_ref[...] * _silu(og_ref[0])
    o_ref[0] = o.astype(o_ref.dtype)


def _gla(proj, w2p, gb, ng, col_qk, col_v, col_og, col_gz, tb):
    B, T, _ = proj.shape
    hv = ng.shape[1]
    hk = gb.shape[1]
    return pl.pallas_call(
        functools.partial(_gla_kernel, n_chunk=tb // GLA_CHUNK),
        out_shape=jax.ShapeDtypeStruct((B, T, hv), BF16),
        grid=(B, T // tb),
        in_specs=[pl.BlockSpec((1, tb, 2 * hk), lambda b, j: (b, j, col_qk)),
                  pl.BlockSpec((1, tb, hv), lambda b, j: (b, j, col_v)),
                  pl.BlockSpec((1, tb, hv), lambda b, j: (b, j, col_og)),
                  pl.BlockSpec((1, tb, LANES), lambda b, j: (b, j, col_gz)),
                  pl.BlockSpec(w2p.shape, lambda b, j: (0, 0)),
                  pl.BlockSpec(gb.shape, lambda b, j: (0, 0)),
                  pl.BlockSpec(ng.shape, lambda b, j: (0, 0))],
        out_specs=pl.BlockSpec((1, tb, hv), lambda b, j: (b, j, 0)),
        scratch_shapes=[pltpu.VMEM((hv, hk), F32), pltpu.VMEM((tb, hk), F32), pltpu.VMEM((tb, hk), F32),
                        pltpu.VMEM((tb, hk), BF16), pltpu.VMEM((tb, hk), BF16), pltpu.VMEM((tb, hv), F32)],
        compiler_params=_cparams(("arbitrary", "arbitrary")),
        name="gla_mixer",
    )(proj, proj, proj, proj, w2p, gb, ng)


def _rope(t, cos, sin_signed):
    d = 32
    half = d // 2
    out = []
    for s in range(t.shape[1] // LANES):
        x = t[:, s * LANES:(s + 1) * LANES]
        lane = lax.broadcasted_iota(jnp.int32, x.shape, 1)
        up = pltpu.roll(x, LANES - half, 1)
        down = pltpu.roll(x, half, 1)
        rot = jnp.where((lane % d) < half, up, down)
        out.append(x * cos + rot * sin_signed)
    return jnp.concatenate(out, axis=1)


def _diff_kernel(q_ref, k_ref, v_ref, cosq_ref, sinq_ref, cosk_ref, sink_ref, lam_ref, g_ref, o_ref,
                 ks, vs, *, tq, lam_init):
    H = DIFF_HEADS
    d = q_ref.shape[2] // (2 * H)
    dv = v_ref.shape[2] // H
    j = pl.program_id(1)

    @pl.when(j == 0)
    def _():
        k = _rope(k_ref[0], cosk_ref[0], sink_ref[0])
        v = v_ref[0]
        for hc in range(2 * H):
            ks[hc] = k[:, hc * d:(hc + 1) * d].astype(BF16)
        ones_col = (lax.broadcasted_iota(jnp.int32, (v.shape[0], dv), 1) == 0).astype(BF16)
        for h in range(H):
            vs[h] = jnp.concatenate([v[:, h * dv:(h + 1) * dv].astype(BF16), ones_col], axis=1)

    q = _rope(q_ref[0], cosq_ref[0], sinq_ref[0]) * (d ** -0.5)
    lp = lam_ref[...]
    lam = (jnp.exp(jnp.sum(lp[0:1] * lp[1:2], axis=-1, keepdims=True))
           - jnp.exp(jnp.sum(lp[2:3] * lp[3:4], axis=-1, keepdims=True)) + lam_init)
    on_or_below_diag = (lax.broadcasted_iota(jnp.int32, (tq, tq), 1)
                        <= lax.broadcasted_iota(jnp.int32, (tq, tq), 0))

    def update(qh, kh, vh, m, acc, masked):
        s = lax.dot_general(qh, kh, (((1,), (1,)), ((), ())), preferred_element_type=F32)
        if masked:
            s = jnp.where(on_or_below_diag, s, -jnp.inf)
        m_new = jnp.maximum(m, jnp.max(s, axis=-1, keepdims=True))
        p = jnp.exp(s - m_new).astype(BF16)
        acc = jnp.exp(m - m_new) * acc + jnp.dot(p, vh, preferred_element_type=F32)
        return m_new, acc

    qs = [q[:, hc * d:(hc + 1) * d].astype(BF16) for hc in range(2 * H)]

    def kv_block(kb, carry, masked):
        rows = pl.ds(pl.multiple_of(kb * tq, tq), tq)
        out = []
        for hc in range(2 * H):
            m, acc = carry[2 * hc], carry[2 * hc + 1]
            out.extend(update(qs[hc], ks[hc, rows, :], vs[hc // 2, rows, :], m, acc, masked))
        return tuple(out)

    m0 = jnp.full((tq, 1), -jnp.inf, F32)
    a0 = jnp.zeros((tq, 2 * dv), F32)
    carry = lax.fori_loop(0, j, lambda kb, c: kv_block(kb, c, False), (m0, a0) * (2 * H))
    carry = kv_block(j, carry, True)

    for h in range(H):
        a1, a2 = carry[4 * h + 1], carry[4 * h + 3]
        comp = [a[:, 0:dv] / a[:, dv:dv + 1] for a in (a1, a2)]
        o = comp[0] - lam * comp[1]
        o = o * lax.rsqrt(jnp.mean(o * o, axis=-1, keepdims=True) + RMS_EPS)
        o = o * g_ref[...] * (1.0 - lam_init)
        o_ref[0, :, h * dv:(h + 1) * dv] = o.astype(o_ref.dtype)


def _diff_attention(proj, cos, sin, lam_p, g, col_q, col_k, col_v, layer_idx):
    B, T, _ = proj.shape
    H = DIFF_HEADS
    dv = g.shape[1]
    d = dv // 2
    w = 2 * H * d
    assert cos.shape[2] == LANES
    tq = min(512, T)
    lam_init = 0.8 - 0.6 * math.exp(-0.3 * layer_idx)
    blk = lambda col: pl.BlockSpec((1, tq, w), lambda b, j: (b, j, col))
    full = lambda col: pl.BlockSpec((1, T, w), lambda b, j: (b, 0, col))
    tab_blk = pl.BlockSpec((1, tq, LANES), lambda b, j: (b, j, 0))
    tab_full = pl.BlockSpec((1, T, LANES), lambda b, j: (b, 0, 0))
    return pl.pallas_call(
        functools.partial(_diff_kernel, tq=tq, lam_init=lam_init),
        out_shape=jax.ShapeDtypeStruct((B, T, H * dv), BF16),
        grid=(B, T // tq),
        in_specs=[blk(col_q), full(col_k), full(col_v), tab_blk, tab_blk, tab_full, tab_full,
                  pl.BlockSpec(lam_p.shape, lambda b, j: (0, 0)),
                  pl.BlockSpec(g.shape, lambda b, j: (0, 0))],
        out_specs=pl.BlockSpec((1, tq, H * dv), lambda b, j: (b, j, 0)),
        scratch_shapes=[pltpu.VMEM((2 * H, T, d), BF16),
                        pltpu.VMEM((H, T, 2 * dv), BF16)],
        compiler_params=_cparams(("arbitrary", "arbitrary")),
        name="diff_attention",
    )(proj, proj, proj, cos, sin, cos, sin, lam_p, g)


def _conv_kernel(a_ref, b_ref, w_ref, cb_ref, lg_ref, lb_ref, o_ref, u_ref, *, rb):
    T = a_ref.shape[1]
    pad = u_ref.shape[0] - T
    u_ref[0:pad, :] = jnp.zeros((pad, u_ref.shape[1]), F32)
    u_ref[pad:pad + T, :] = a_ref[0] * _sigmoid(b_ref[0])
    first = pad - (CONV_WIDTH - 1)

    def block(i, carry):
        r0 = pl.multiple_of(i * rb, rb)
        win = u_ref[pl.ds(r0, rb + pad), :]
        acc = jnp.zeros((rb, u_ref.shape[1]), F32)
        for s in range(SUBLANES):
            taps = [j for j in range(CONV_WIDTH) if (first + j) % SUBLANES == s]
            rolled = win if s == 0 else pltpu.roll(win, rb + pad - s, 0)
            for j in taps:
                a0 = first + j - s
                acc = acc + w_ref[j:j + 1, :] * rolled[a0:a0 + rb, :]
        y = acc + cb_ref[...]
        mu = jnp.mean(y, axis=-1, keepdims=True)
        yc = y - mu
        var = jnp.mean(yc * yc, axis=-1, keepdims=True)
        y = yc * lax.rsqrt(var + LN_EPS) * lg_ref[...] + lb_ref[...]
        o_ref[0, pl.ds(r0, rb), :] = _silu(y).astype(o_ref.dtype)
        return carry

    lax.fori_loop(0, T // rb, block, 0)


def _conformer_conv(proj, w, cb, lg, lb, col_a, col_b):
    B, T, _ = proj.shape
    ch = w.shape[1]
    rb = min(128, T)
    return pl.pallas_call(
        functools.partial(_conv_kernel, rb=rb),
        out_shape=jax.ShapeDtypeStruct((B, T, ch), BF16),
        grid=(B,),
        in_specs=[pl.BlockSpec((1, T, ch), lambda b: (b, 0, col_a)),
                  pl.BlockSpec((1, T, ch), lambda b: (b, 0, col_b)),
                  pl.BlockSpec(w.shape, lambda b: (0, 0)),
                  pl.BlockSpec(cb.shape, lambda b: (0, 0)),
                  pl.BlockSpec(lg.shape, lambda b: (0, 0)),
                  pl.BlockSpec(lb.shape, lambda b: (0, 0))],
        out_specs=pl.BlockSpec((1, T, ch), lambda b: (b, 0, 0)),
        scratch_shapes=[pltpu.VMEM((T + 32, ch), F32)],
        compiler_params=_cparams(("arbitrary",)),
        name="conformer_conv",
    )(proj, proj, w, cb, lg, lb)


def _rwkv_kernel(x_ref, mu_ref, w0_ref, w2_ref, a0_ref, a2_ref, g2_ref, kk_ref, ka_ref, rk_ref,
                 lng_ref, lnb_ref, o_ref, s_ref, prev_ref, gate_ref, bonus_ref, dec_ref, qeff_ref, yloc_ref,
                 mlow_ref, nc_ref, y_ref, *, n_chunk):
    C = RWKV_CHUNK
    H = RWKV_HEADS
    W = o_ref.shape[2]
    N = W // H

    @pl.when(pl.program_id(1) == 0)
    def _():
        s_ref[...] = jnp.zeros_like(s_ref)
        prev_ref[...] = jnp.zeros_like(prev_ref)

    lane = lax.broadcasted_iota(jnp.int32, (1, W), 1)
    head_mask = [(lane // N == h).astype(F32) for h in range(H)]
    block_diag = _group_matrix(W, N)
    head_sum = block_diag.astype(F32)
    ti = lax.broadcasted_iota(jnp.int32, (C, C), 0)
    tj = lax.broadcasted_iota(jnp.int32, (C, C), 1)
    tril_incl = ti >= tj
    tril_strict = ti > tj
    same_sub = (ti // RWKV_SUB) == (tj // RWKV_SUB)
    eye = (ti == tj).astype(F32)
    nc = n_chunk
    tb = nc * C

    def head_total(t, two_term=False):
        ones = head_sum.astype(BF16)
        if two_term:
            return jnp.dot(_hi_lo(t), jnp.concatenate([ones, ones], axis=0), preferred_element_type=F32)
        return jnp.dot(t.astype(BF16), ones, preferred_element_type=F32)

    x = x_ref[0]
    first_row = lax.broadcasted_iota(jnp.int32, (tb, 1), 0) == 0
    prev = jnp.where(first_row, prev_ref[...], pltpu.roll(x, 1, 0))
    prev_ref[...] = x[tb - 1:tb, :]
    xm = x + (prev - x) * mu_ref[...]
    r = xm[:, 0:W]
    k = xm[:, W:2 * W]
    v = xm[:, 2 * W:3 * W]
    zz = xm[:, 3 * W:3 * W + LANES]
    zg = xm[:, 3 * W + LANES:]
    w = -_softplus(-(w0_ref[...] + _mm(jnp.tanh(zz), w2_ref[...]))) - 0.5
    lw = -jnp.exp(w)
    a = _sigmoid(a0_ref[...] + _mm(zz, a2_ref[...]))
    gate_ref[...] = _mm(_sigmoid(zg), g2_ref[...])
    kk = k * kk_ref[...]
    kk = kk / jnp.maximum(jnp.sqrt(head_total(kk * kk, two_term=True)), 1e-12)
    k = k * (1.0 + (a - 1.0) * ka_ref[...])
    b = kk * a
    bonus_ref[...] = head_total(r * k * rk_ref[...]) * v

    c3 = lambda t: t.reshape(nc, C, t.shape[1])
    sums = _bmm(jnp.broadcast_to(tril_incl.astype(BF16)[None], (nc, C, C)), c3(_hi_lo(lw)))
    G3 = sums[:, :, 0:W] + sums[:, :, W:2 * W]
    total3 = jnp.broadcast_to(G3[:, C - 1:C, :], (nc, C, W))
    dec_ref[...] = jnp.exp(total3).reshape(tb, W)
    G = G3.reshape(tb, W)
    inv = jnp.exp(-G)
    to_end = jnp.exp(total3 - G3).reshape(tb, W)
    kap = kk * jnp.exp(G - lw)
    rho = r * jnp.exp(G)
    kap3, rho3, v3 = c3(kap), c3(rho), c3(v)
    bet_kt3 = jnp.concatenate([c3(b * inv), c3(k * inv)], axis=1)
    betc3, kc3 = c3(b * to_end), c3(k * to_end)

    def bmm_nt(p, q):
        return lax.dot_general(p.astype(BF16), q.astype(BF16), (((2,), (2,)), ((0,), (0,))),
                               preferred_element_type=F32)

    def bmm_tn(p, q):
        return lax.dot_general(p.astype(BF16), q.astype(BF16), (((1,), (1,)), ((0,), (0,))),
                               preferred_element_type=F32)

    wi = lax.broadcasted_iota(jnp.int32, (C, 2 * C), 0)
    wj = lax.broadcasted_iota(jnp.int32, (C, 2 * C), 1) % C
    A_bk, B_bk = [], []
    for h in range(H):
        lhs = jnp.concatenate([c3(kap * head_mask[h]), c3(rho * head_mask[h])], axis=1)
        prod = bmm_nt(lhs, bet_kt3)
        A_bk.append(jnp.where(wj < wi, prod[:, 0:C], 0.0))
        B_bk.append(jnp.where(wj <= wi, prod[:, C:2 * C], 0.0))
    A_b = jnp.concatenate([t[:, :, 0:C] for t in A_bk], axis=0)
    B_b = [t[:, :, 0:C] for t in B_bk]
    v3_low = jnp.concatenate([jnp.zeros_like(v3), v3], axis=1)

    Dg = jnp.where(same_sub, A_b, 0.0)
    Lo = A_b - Dg
    D2 = _bmm(Dg, Dg)
    D4 = _bmm(D2, D2)
    D8 = _bmm(D4, D4)
    Dinv = _bmm(_bmm(_bmm(eye - Dg, eye + D2), eye + D4), eye + D8)
    Nn = _bmm(Dinv, Lo)
    N2 = _bmm(Nn, Nn)
    Tm = _bmm(_bmm(eye - Nn, eye + N2), Dinv)
    Tm = [Tm[h * nc:(h + 1) * nc] for h in range(H)]

    def per_head(mats, t):
        reps = t.shape[2] // W
        acc = None
        for h in range(H):
            m = head_mask[h] if reps == 1 else jnp.concatenate([head_mask[h]] * reps, axis=1)
            u = m * _bmm(mats[h], t)
            acc = u if acc is None else acc + u
        return acc

    akv = per_head(A_bk, v3_low)
    tk = per_head(Tm, jnp.concatenate([kap3, akv], axis=2))
    kap_p = tk[:, :, 0:W]
    v_p = tk[:, :, W:2 * W]
    bb = per_head(B_b, jnp.concatenate([kap_p, v_p], axis=2))
    qeff_ref[...] = (rho3 - bb[:, :, 0:W]).astype(BF16)
    yloc_ref[...] = per_head(B_bk, v3_low) - bb[:, :, W:2 * W]
    mlow_ref[...] = jnp.where(block_diag, bmm_tn(kap_p, betc3), 0.0).astype(BF16)
    nc_ref[...] = jnp.where(block_diag, bmm_tn(jnp.concatenate([v3, v_p], axis=1),
                                               jnp.concatenate([kc3, -betc3], axis=1)), 0.0)

    def chunk(ci, carry):
        r0 = pl.multiple_of(ci * C, C)
        s = s_ref[...]
        sb = s.astype(BF16)
        y = lax.dot_general(qeff_ref[ci], sb, (((1,), (1,)), ((), ())), preferred_element_type=F32)
        y_ref[pl.ds(r0, C), :] = y + yloc_ref[ci]
        s_ref[...] = (s * dec_ref[pl.ds(r0, 8), :][0:1, :]
                      - jnp.dot(sb, mlow_ref[ci], preferred_element_type=F32) + nc_ref[ci])
        return carry

    lax.fori_loop(0, nc, chunk, 0)

    y = y_ref[...]
    mean = head_total(y, two_term=True) / N
    yc = y - mean
    var = head_total(yc * yc) / N
    yn = yc * lax.rsqrt(var + RWKV_GN_EPS) * lng_ref[...] + lnb_ref[...]
    o_ref[0] = ((yn + bonus_ref[...]) * gate_ref[...]).astype(o_ref.dtype)


def _rwkv7(proj, vecs, w2p, a2p, g2, col, tb):
    B, T, _ = proj.shape
    W = g2.shape[1]
    cols = vecs["mu"].shape[1]
    names = ("mu", "w0", "w2", "a0", "a2", "g2", "kk", "ka", "rk", "lng", "lnb")
    params = dict(vecs, w2=w2p, a2=a2p, g2=g2)
    const = lambda arr: pl.BlockSpec(arr.shape, lambda b, j: (0, 0))
    n_chunk = tb // RWKV_CHUNK
    return pl.pallas_call(
        functools.partial(_rwkv_kernel, n_chunk=n_chunk),
        out_shape=jax.ShapeDtypeStruct((B, T, W), BF16),
        grid=(B, T // tb),
        in_specs=[pl.BlockSpec((1, tb, cols), lambda b, j: (b, j, col))]
                 + [const(params[n]) for n in names],
        out_specs=pl.BlockSpec((1, tb, W), lambda b, j: (b, j, 0)),
        scratch_shapes=[pltpu.VMEM((W, W), F32), pltpu.VMEM((1, cols), F32),
                        pltpu.VMEM((tb, W), F32), pltpu.VMEM((tb, W), F32), pltpu.VMEM((tb, W), F32),
                        pltpu.VMEM((n_chunk, RWKV_CHUNK, W), BF16), pltpu.VMEM((n_chunk, RWKV_CHUNK, W), F32),
                        pltpu.VMEM((n_chunk, W, W), BF16), pltpu.VMEM((n_chunk, W, W), F32),
                        pltpu.VMEM((tb, W), F32)],
        compiler_params=_cparams(("arbitrary", "arbitrary")),
        name="rwkv7_mixer",
    )(proj, *[params[n] for n in names])


def _merge_kernel(x_ref, sc_ref, sh_ref, gt_ref, gpre_ref, gpost_ref, wg_ref, b0_ref, b1_ref, b2_ref,
                  b3_ref, wb_ref, wo_ref, o_ref):
    x = x_ref[0]
    D = x.shape[1]
    h = _rms_mod(x, gpre_ref[...], sc_ref[0], sh_ref[0]).astype(BF16)
    merged = None
    for g, br in enumerate((b0_ref, b1_ref, b2_ref, b3_ref)):
        gate = _sigmoid(jnp.dot(h, wg_ref[:, g * D:(g + 1) * D], preferred_element_type=F32))
        t = gate * jnp.dot(br[0], wb_ref[g], preferred_element_type=F32)
        merged = t if merged is None else merged + t
    y = _mm(merged, wo_ref[...])
    y = y * lax.rsqrt(jnp.mean(y * y, axis=-1, keepdims=True) + RMS_EPS) * gpost_ref[...]
    o_ref[0] = x + gt_ref[0] * y


def _merge(x, sc, sh, gt, gpre, gpost, w_gate, branches, w_branch, w_out, tm):
    B, T, D = x.shape
    bw = branches[0].shape[2]
    tok = lambda n: pl.BlockSpec((1, tm, n), lambda b, i: (b, i, 0))
    vec = pl.BlockSpec((1, 1, D), lambda b, i: (b, 0, 0))
    const2 = lambda arr: pl.BlockSpec(arr.shape, lambda b, i: (0, 0))
    return pl.pallas_call(
        _merge_kernel,
        out_shape=jax.ShapeDtypeStruct((B, T, D), F32),
        grid=(B, T // tm),
        in_specs=[tok(D), vec, vec, vec, const2(gpre), const2(gpost), const2(w_gate),
                  tok(bw), tok(bw), tok(bw), tok(bw),
                  pl.BlockSpec(w_branch.shape, lambda b, i: (0, 0, 0)), const2(w_out)],
        out_specs=tok(D),
        compiler_params=_cparams(("arbitrary", "arbitrary")),
        name="merge_out_proj",
    )(x, sc, sh, gt, gpre, gpost, w_gate, *branches, w_branch, w_out)


def _ffn_kernel(x_ref, sc_ref, sh_ref, gt_ref, gpre_ref, gpost_ref, wg_ref, wu_ref, wd_ref, o_ref, *, tf):
    x = x_ref[0]
    h = _rms_mod(x, gpre_ref[...], sc_ref[0], sh_ref[0]).astype(BF16)
    F = wg_ref.shape[1]
    acc = None
    for f0 in range(0, F, tf):
        f1 = min(f0 + tf, F)
        gate = jnp.dot(h, wg_ref[:, f0:f1], preferred_element_type=F32)
        up = jnp.dot(h, wu_ref[:, f0:f1], preferred_element_type=F32)
        t = jnp.dot((_silu(gate) * up).astype(BF16), wd_ref[f0:f1, :], preferred_element_type=F32)
        acc = t if acc is None else acc + t
    y = acc * lax.rsqrt(jnp.mean(acc * acc, axis=-1, keepdims=True) + RMS_EPS) * gpost_ref[...]
    o_ref[0] = x + gt_ref[0] * y


def _dense_ffn(x, sc, sh, gt, gpre, gpost, wg, wu, wd, tm):
    B, T, D = x.shape
    tok = pl.BlockSpec((1, tm, D), lambda b, i: (b, i, 0))
    vec = pl.BlockSpec((1, 1, D), lambda b, i: (b, 0, 0))
    const2 = lambda arr: pl.BlockSpec(arr.shape, lambda b, i: (0, 0))
    return pl.pallas_call(
        functools.partial(_ffn_kernel, tf=512),
        out_shape=jax.ShapeDtypeStruct((B, T, D), F32),
        grid=(B, T // tm),
        in_specs=[tok, vec, vec, vec, const2(gpre), const2(gpost), const2(wg), const2(wu), const2(wd)],
        out_specs=tok,
        compiler_params=_cparams(("arbitrary", "arbitrary")),
        name="dense_swiglu",
    )(x, sc, sh, gt, gpre, gpost, wg, wu, wd)


MOE_TOKEN_TILE = 256
MOE_ROW_TILE = 512
SEG_ALIGN = 16
SEG_PIECES = (256, 128, 64, 32, 16)


def _route_kernel(x_ref, sc_ref, sh_ref, gpre_ref, rw_ref, rb_ref, h_ref, mi_ref, mp_ref, cnt_ref):
    tm = x_ref.shape[1]
    h = _rms_mod(x_ref[0], gpre_ref[...], sc_ref[0], sh_ref[0])
    h_ref[...] = h.astype(BF16)
    logits = _mm_f32(h, rw_ref[...]) + rb_ref[...]
    lane = lax.broadcasted_iota(jnp.int32, logits.shape, 1)
    v1 = jnp.max(logits, axis=-1, keepdims=True)
    i1 = jnp.min(jnp.where(logits == v1, lane, LANES), axis=-1, keepdims=True)
    rest = jnp.where(lane == i1, -jnp.inf, logits)
    v2 = jnp.max(rest, axis=-1, keepdims=True)
    i2 = jnp.min(jnp.where(rest == v2, lane, LANES), axis=-1, keepdims=True)
    e2 = jnp.exp(v2 - v1)
    p1 = 1.0 / (1.0 + e2)
    p2 = e2 / (1.0 + e2)
    oh1 = (lane == i1).astype(F32)
    oh2 = (lane == i2).astype(F32)
    both = oh1 + oh2
    earlier = (lax.broadcasted_iota(jnp.int32, (tm, tm), 1)
               < lax.broadcasted_iota(jnp.int32, (tm, tm), 0)).astype(BF16)
    before = jnp.dot(earlier, both.astype(BF16), preferred_element_type=F32)
    r1 = jnp.sum(oh1 * before, axis=-1, keepdims=True).astype(jnp.int32)
    r2 = jnp.sum(oh2 * before, axis=-1, keepdims=True).astype(jnp.int32)
    col = lax.broadcasted_iota(jnp.int32, mi_ref.shape, 1)
    mi_ref[...] = jnp.where(col == 0, i1, jnp.where(col == 1, i2, jnp.where(col == 2, r1,
                                                                           jnp.where(col == 3, r2, 0))))
    mp_ref[...] = jnp.where(col == 0, p1, jnp.where(col == 1, p2, 0.0))
    cnt_ref[0] = jnp.sum(both, axis=0, keepdims=True).astype(jnp.int32)


def _segment_pieces(n_rows):
    out = []
    for s in SEG_PIECES:
        if s == MOE_TOKEN_TILE:
            out.append((n_rows == s, 0, s))
        else:
            out.append(((n_rows & s) != 0, pl.multiple_of((n_rows // (2 * s)) * (2 * s), SEG_ALIGN), s))
    return out


def _dispatch_kernel(seg_ref, cnt_ref, h_ref, mit_ref, init_ref, xs_ref, buf_ref, sem):
    del init_ref
    tm = h_ref.shape[0]
    i = pl.program_id(0)
    e1, e2 = mit_ref[0:1, :], mit_ref[1:2, :]
    r1, r2 = mit_ref[2:3, :], mit_ref[3:4, :]
    row = lax.broadcasted_iota(jnp.int32, (tm, tm), 0)
    select = jnp.concatenate(
        [jnp.logical_or(jnp.logical_and(e1 == e, r1 == row), jnp.logical_and(e2 == e, r2 == row))
         for e in range(N_EXPERTS)], axis=0).astype(BF16)
    slot = i % 2
    buf_ref[slot] = jnp.dot(select, h_ref[...], preferred_element_type=F32).astype(BF16)

    def segment_copies(tile, buf, e):
        n = cnt_ref[tile * N_EXPERTS + e]
        n_rows = ((n + SEG_ALIGN - 1) // SEG_ALIGN) * SEG_ALIGN
        dst = pl.multiple_of(seg_ref[tile * N_EXPERTS + e], SEG_ALIGN)
        return [(cond, pltpu.make_async_copy(buf_ref.at[buf, pl.ds(e * tm + off, s), :],
                                             xs_ref.at[pl.ds(dst + off, s), :], sem))
                for cond, off, s in _segment_pieces(n_rows)]

    def for_all_segments(tile, buf, action):
        for e in range(N_EXPERTS):
            for cond, cp in segment_copies(tile, buf, e):
                pl.when(cond)(getattr(cp, action))

    @pl.when(i > 0)
    def _():
        for_all_segments(i - 1, 1 - slot, "wait")

    for_all_segments(i, slot, "start")

    @pl.when(i == pl.num_programs(0) - 1)
    def _():
        for_all_segments(i, slot, "wait")


def _expert_kernel(te_ref, nv_ref, xs_ref, wg_ref, wu_ref, wd_ref, ys_ref, acc_ref):
    del te_ref
    r = pl.program_id(0)
    f = pl.program_id(1)
    valid = r < nv_ref[0]

    @pl.when(jnp.logical_and(valid, f == 0))
    def _():
        acc_ref[...] = jnp.zeros_like(acc_ref)

    @pl.when(valid)
    def _():
        x = xs_ref[...]
        tf = wg_ref.shape[2]
        acc = acc_ref[...]
        for f0 in range(0, tf, 512):
            f1 = min(f0 + 512, tf)
            gate = jnp.dot(x, wg_ref[0, :, f0:f1], preferred_element_type=F32)
            up = jnp.dot(x, wu_ref[0, :, f0:f1], preferred_element_type=F32)
            acc = acc + jnp.dot((_silu(gate) * up).astype(BF16), wd_ref[0, f0:f1, :],
                                preferred_element_type=F32)
        acc_ref[...] = acc

    @pl.when(f == pl.num_programs(1) - 1)
    def _():
        @pl.when(valid)
        def _():
            ys_ref[...] = acc_ref[...].astype(ys_ref.dtype)

        @pl.when(jnp.logical_not(valid))
        def _():
            ys_ref[...] = jnp.zeros_like(ys_ref)


def _combine_kernel(seg_ref, ys_ref, mi_ref, mp_ref, x_ref, gt_ref, gpost_ref, o_ref, win_ref, sem):
    tm = x_ref.shape[1]
    i = pl.program_id(0) * pl.num_programs(1) + pl.program_id(1)
    n_tiles = pl.num_programs(0) * pl.num_programs(1)
    slot = i % 2

    def window_copy(tile, buf, e):
        src = pl.multiple_of(seg_ref[tile * N_EXPERTS + e], SEG_ALIGN)
        return pltpu.make_async_copy(ys_ref.at[pl.ds(src, tm), :], win_ref.at[buf, pl.ds(e * tm, tm), :],
                                     sem.at[buf, e])

    @pl.when(i == 0)
    def _():
        for e in range(N_EXPERTS):
            window_copy(i, slot, e).start()

    @pl.when(i + 1 < n_tiles)
    def _():
        for e in range(N_EXPERTS):
            window_copy(i + 1, 1 - slot, e).start()

    e1, e2 = mi_ref[:, 0:1], mi_ref[:, 1:2]
    r1, r2 = mi_ref[:, 2:3], mi_ref[:, 3:4]
    col = lax.broadcasted_iota(jnp.int32, (tm, tm), 1)
    acc = None
    for e in range(N_EXPERTS):
        expand = jnp.concatenate([jnp.logical_and(e1 == e, r1 == col),
                                  jnp.logical_and(e2 == e, r2 == col)], axis=0).astype(BF16)
        window_copy(i, slot, e).wait()
        t = jnp.dot(expand, win_ref[slot, pl.ds(e * tm, tm), :], preferred_element_type=F32)
        acc = t if acc is None else acc + t
    y = mp_ref[:, 0:1] * acc[0:tm] + mp_ref[:, 1:2] * acc[tm:2 * tm]
    y = y * lax.rsqrt(jnp.mean(y * y, axis=-1, keepdims=True) + RMS_EPS) * gpost_ref[...]
    o_ref[0] = x_ref[0] + gt_ref[0] * y


def _moe_ffn(x, sc, sh, gt, gpre, gpost, rw, rb, wg, wu, wd, tf):
    B, T, D = x.shape
    E, _, F = wg.shape
    tm = min(MOE_TOKEN_TILE, T)
    assert tm == MOE_TOKEN_TILE and E == N_EXPERTS
    nT = T // tm
    n_tok_tiles = B * nT
    N = B * T
    max_rows = 2 * N + n_tok_tiles * E * (SEG_ALIGN - 1) + E * (MOE_ROW_TILE - SEG_ALIGN)
    n_row_tiles = -(-max_rows // MOE_ROW_TILE) + 1
    P = n_row_tiles * MOE_ROW_TILE

    vec = pl.BlockSpec((1, 1, D), lambda b, j: (b, 0, 0))
    const2 = lambda arr: pl.BlockSpec(arr.shape, lambda b, j: (0, 0))
    flat = lambda n: pl.BlockSpec((tm, n), lambda b, j: (b * nT + j, 0))
    h, mi, mp, cnt = pl.pallas_call(
        _route_kernel,
        out_shape=(jax.ShapeDtypeStruct((N, D), BF16), jax.ShapeDtypeStruct((N, 8), jnp.int32),
                   jax.ShapeDtypeStruct((N, 8), F32), jax.ShapeDtypeStruct((n_tok_tiles, 1, LANES), jnp.int32)),
        grid=(B, nT),
        in_specs=[pl.BlockSpec((1, tm, D), lambda b, j: (b, j, 0)), vec, vec, const2(gpre), const2(rw),
                  const2(rb)],
        out_specs=(flat(D), flat(8), flat(8), pl.BlockSpec((1, 1, LANES), lambda b, j: (b * nT + j, 0, 0))),
        compiler_params=_cparams(("arbitrary", "arbitrary")),
        name="moe_route",
    )(x, sc, sh, gpre, rw, rb)

    counts = cnt[:, 0, :E]
    seg_len = (counts + SEG_ALIGN - 1) // SEG_ALIGN * SEG_ALIGN
    group_len = (jnp.sum(seg_len, axis=0) + MOE_ROW_TILE - 1) // MOE_ROW_TILE * MOE_ROW_TILE
    group_end = jnp.cumsum(group_len)
    seg_start = (group_end - group_len)[None, :] + jnp.cumsum(seg_len, axis=0) - seg_len
    seg_start = seg_start.reshape(-1).astype(jnp.int32)
    counts = counts.reshape(-1)
    n_valid = (group_end[-1:] // MOE_ROW_TILE).astype(jnp.int32)
    tile_first_row = jnp.arange(n_row_tiles, dtype=jnp.int32) * MOE_ROW_TILE
    tile_expert = jnp.minimum(jnp.sum(tile_first_row[:, None] >= group_end[None, :], axis=1), E - 1)
    tile_expert = tile_expert.astype(jnp.int32)

    xs = pl.pallas_call(
        _dispatch_kernel,
        out_shape=jax.ShapeDtypeStruct((P, D), BF16),
        grid_spec=pltpu.PrefetchScalarGridSpec(
            num_scalar_prefetch=2,
            grid=(n_tok_tiles,),
            in_specs=[pl.BlockSpec((tm, D), lambda i, seg, n: (i, 0)),
                      pl.BlockSpec((8, tm), lambda i, seg, n: (0, i)),
                      pl.BlockSpec(memory_space=pl.ANY)],
            out_specs=pl.BlockSpec(memory_space=pl.ANY),
            scratch_shapes=[pltpu.VMEM((2, E * tm, D), BF16), pltpu.SemaphoreType.DMA(())]),
        input_output_aliases={4: 0},
        compiler_params=_cparams(("arbitrary",)),
        name="moe_dispatch",
    )(seg_start, counts, h, mi.T, jnp.zeros((P, D), BF16))

    nf = F // tf
    live = lambda r, f, nv: jnp.where(r < nv[0], f, nf - 1)
    ys = pl.pallas_call(
        _expert_kernel,
        out_shape=jax.ShapeDtypeStruct((P, D), BF16),
        grid_spec=pltpu.PrefetchScalarGridSpec(
            num_scalar_prefetch=2,
            grid=(n_row_tiles, nf),
            in_specs=[pl.BlockSpec((MOE_ROW_TILE, D), lambda r, f, te, nv: (r, 0)),
                      pl.BlockSpec((1, D, tf), lambda r, f, te, nv: (te[r], 0, live(r, f, nv))),
                      pl.BlockSpec((1, D, tf), lambda r, f, te, nv: (te[r], 0, live(r, f, nv))),
                      pl.BlockSpec((1, tf, D), lambda r, f, te, nv: (te[r], live(r, f, nv), 0))],
            out_specs=pl.BlockSpec((MOE_ROW_TILE, D), lambda r, f, te, nv: (r, 0)),
            scratch_shapes=[pltpu.VMEM((MOE_ROW_TILE, D), F32)]),
        compiler_params=_cparams(("arbitrary", "arbitrary")),
        name="moe_experts",
    )(tile_expert, n_valid, xs, wg, wu, wd)

    return pl.pallas_call(
        _combine_kernel,
        out_shape=jax.ShapeDtypeStruct((B, T, D), F32),
        grid_spec=pltpu.PrefetchScalarGridSpec(
            num_scalar_prefetch=1,
            grid=(B, nT),
            in_specs=[pl.BlockSpec(memory_space=pl.ANY),
                      pl.BlockSpec((tm, 8), lambda b, j, seg: (b * nT + j, 0)),
                      pl.BlockSpec((tm, 8), lambda b, j, seg: (b * nT + j, 0)),
                      pl.BlockSpec((1, tm, D), lambda b, j, seg: (b, j, 0)),
                      pl.BlockSpec((1, 1, D), lambda b, j, seg: (b, 0, 0)),
                      pl.BlockSpec(gpost.shape, lambda b, j, seg: (0, 0))],
            out_specs=pl.BlockSpec((1, tm, D), lambda b, j, seg: (b, j, 0)),
            scratch_shapes=[pltpu.VMEM((2, E * tm, D), BF16), pltpu.SemaphoreType.DMA((2, E))]),
        compiler_params=_cparams(("arbitrary", "arbitrary")),
        name="moe_combine",
    )(seg_start, ys, mi, mp, x, gt, gpost)


def _rope_tables(positions, groups):
    d = 32
    inv = 1.0 / (ROPE_THETA ** (jnp.arange(0, d, 2, dtype=F32) / d))
    ang = positions.astype(F32)[..., None] * inv
    cos, sin = jnp.cos(ang), jnp.sin(ang)
    cos = jnp.tile(jnp.concatenate([cos, cos], axis=-1), (1, 1, groups))
    sin = jnp.tile(jnp.concatenate([-sin, sin], axis=-1), (1, 1, groups))
    return cos, sin


def _pad_rows(w, rows, offset):
    out = jnp.zeros((rows, w.shape[1]), w.dtype)
    return out.at[offset:offset + w.shape[0]].set(w)


def kernel(x, c, positions, ada_w, ada_b, norm_mix_pre, norm_mix_post, norm_ffn_pre, norm_ffn_post, w_in, gla_gate_w2, gla_gate_b, gla_norm, diff_lambda, diff_subln, conv_w, conv_b, conv_ln_g, conv_ln_b, rwkv_mu, rwkv_w0, rwkv_w2, rwkv_a0, rwkv_a2, rwkv_g2, rwkv_k_k, rwkv_k_a, rwkv_r_k, rwkv_ln_g, rwkv_ln_b, w_branch, w_out, ffn_w_gate, ffn_w_up, ffn_w_down, router_w, router_b, moe_w_gate, moe_w_up, moe_w_down):
    B, T, D = x.shape
    L = ada_w.shape[0]
    W = D // N_BRANCH
    hk = gla_gate_b.shape[1]
    decay_rank = rwkv_w2.shape[1]
    a_rank = rwkv_a2.shape[1]
    gate_rank = rwkv_g2.shape[1]
    assert decay_rank + a_rank == LANES and 2 * hk == W and gate_rank == LANES
    n_mix = 3 * W + 3 * W + 2 * W + (3 * W + decay_rank + a_rank + gate_rank)
    sizes = (hk, hk, W, W, GLA_GATE_RANK, W, W, W, W, W, 3 * W + LANES + gate_rank, N_BRANCH * D)
    offs = [0]
    for s in sizes:
        offs.append(offs[-1] + s)
    assert offs[-1] == w_in.shape[2]
    tm = min(512, T)
    tb = min(512, T)

    mod = _modulation(c, ada_w, ada_b)
    cos, sin = _rope_tables(positions, LANES // 32)

    for l in range(L):
        m = mod[l].reshape(B, 1, 6 * D)
        sh_m, sc_m, gt_m, sh_f, sc_f, gt_f = [m[:, :, i * D:(i + 1) * D] for i in range(6)]

        wl = w_in[l]
        gz_cols = jnp.zeros((D, LANES), F32).at[:, :GLA_GATE_RANK].set(wl[:, offs[4]:offs[5]])
        w_mix = jnp.concatenate([wl[:, offs[0]:offs[4]], wl[:, offs[5]:offs[11]], gz_cols], axis=1)
        w_mix = w_mix.astype(BF16)
        w_gate = wl[:, offs[11]:offs[12]].astype(BF16)
        proj = _in_projection(x, sc_m, sh_m, norm_mix_pre[l][None], w_mix, tm)

        w2p = _pad_rows(gla_gate_w2[l], LANES, 0)
        o_gla = _gla(proj, w2p, gla_gate_b[l][None], jnp.tile(gla_norm[l], GLA_HEADS)[None],
                     col_qk=0, col_v=1, col_og=2, col_gz=(n_mix // LANES), tb=tb)
        o_diff = _diff_attention(proj, cos, sin, diff_lambda[l],
                                 diff_subln[l][None], col_q=3, col_k=4, col_v=5, layer_idx=l)
        o_conv = _conformer_conv(proj, _pad_rows(conv_w[l], 32, 0), conv_b[l][None],
                                 conv_ln_g[l][None], conv_ln_b[l][None], col_a=6, col_b=7)
        vecs = dict(mu=rwkv_mu[l][None], w0=rwkv_w0[l][None], a0=rwkv_a0[l][None],
                    kk=rwkv_k_k[l][None], ka=rwkv_k_a[l][None], rk=rwkv_r_k[l].reshape(1, W),
                    lng=rwkv_ln_g[l][None], lnb=rwkv_ln_b[l][None])
        o_rwkv = _rwkv7(proj, vecs, _pad_rows(rwkv_w2[l], LANES, 0).astype(BF16),
                        _pad_rows(rwkv_a2[l], LANES, decay_rank).astype(BF16),
                        rwkv_g2[l].astype(BF16), col=2, tb=tb)
        x = _merge(x, sc_m, sh_m, gt_m, norm_mix_pre[l][None], norm_mix_post[l][None], w_gate,
                   (o_gla, o_diff, o_conv, o_rwkv), w_branch[l].astype(BF16), w_out[l].astype(BF16), tm)

        i = l // 2
        if l % 2 == 0:
            x = _dense_ffn(x, sc_f, sh_f, gt_f, norm_ffn_pre[l][None], norm_ffn_post[l][None],
                           ffn_w_gate[i].astype(BF16), ffn_w_up[i].astype(BF16),
                           ffn_w_down[i].astype(BF16), tm)
        else:
            rw = jnp.zeros((D, LANES), F32).at[:, :N_EXPERTS].set(router_w[i])
            rb = jnp.full((1, LANES), -jnp.inf, F32).at[0, :N_EXPERTS].set(router_b[i])
            x = _moe_ffn(x, sc_f, sh_f, gt_f, norm_ffn_pre[l][None], norm_ffn_post[l][None], rw, rb,
                         moe_w_gate[i].astype(BF16), moe_w_up[i].astype(BF16),
                         moe_w_down[i].astype(BF16), tf=1792)
    return x
```

```python
import functools
import math

import jax
import jax.numpy as jnp
from jax import lax
from jax.experimental import pallas as pl
from jax.experimental.pallas import tpu as pltpu

F32 = jnp.float32
BF16 = jnp.bfloat16
HIGHEST = lax.Precision.HIGHEST

N_BRANCH = 4
GLA_HEADS = 4
GLA_GATE_RANK = 16
GLA_GATE_NORMALIZER = 16.0
GLA_CHUNK = 32
DIFF_HEADS = 4
ROPE_THETA = 10000.0
CONV_WIDTH = 31
RWKV_HEADS = 4
RWKV_CHUNK = 64
RWKV_SUB = 16
N_EXPERTS = 8
RMS_EPS = 1e-6
LN_EPS = 1e-5
RWKV_GN_EPS = 64e-5
LANES = 128
SUBLANES = 8
VMEM_LIMIT = 56 * 1024 * 1024


def _cparams(sem):
    return pltpu.CompilerParams(dimension_semantics=sem, vmem_limit_bytes=VMEM_LIMIT)


def _mm(a, b):
    return jnp.dot(a.astype(BF16), b.astype(BF16), preferred_element_type=F32)


def _mm_nt(a, b):
    return lax.dot_general(a.astype(BF16), b.astype(BF16), (((1,), (1,)), ((), ())),
                           preferred_element_type=F32)


def _mm_tn(a, b):
    return lax.dot_general(a.astype(BF16), b.astype(BF16), (((0,), (0,)), ((), ())),
                           preferred_element_type=F32)


def _bmm(a, b):
    return lax.dot_general(a.astype(BF16), b.astype(BF16), (((2,), (1,)), ((0,), (0,))),
                           preferred_element_type=F32)


def _mm_f32(a, b):
    return jnp.dot(a, b, precision=HIGHEST, preferred_element_type=F32)


def _hi_lo(x):
    hi = x.astype(BF16)
    return jnp.concatenate([hi, (x - hi.astype(F32)).astype(BF16)], axis=1)


def _sigmoid(x):
    return 0.5 * jnp.tanh(0.5 * x) + 0.5


def _silu(x):
    return x * _sigmoid(x)


def _softplus(x):
    return jnp.maximum(x, 0.0) + jnp.log(1.0 + jnp.exp(-jnp.abs(x)))


def _group_matrix(n, group):
    r = lax.broadcasted_iota(jnp.int32, (n, n), 0) // group
    c = lax.broadcasted_iota(jnp.int32, (n, n), 1) // group
    return r == c


def _rms_mod(x, gain, scale, shift):
    y = x * lax.rsqrt(jnp.mean(x * x, axis=-1, keepdims=True) + RMS_EPS)
    return y * gain * (1.0 + scale) + shift


def _mod_kernel(c_ref, w_ref, b_ref, o_ref):
    o_ref[0] = _mm_f32(_silu(c_ref[...]), w_ref[0]) + b_ref[0]


def _modulation(c, ada_w, ada_b):
    L, D, M = ada_w.shape
    B = c.shape[0]
    tn = M // 4
    return pl.pallas_call(
        _mod_kernel,
        out_shape=jax.ShapeDtypeStruct((L, B, M), F32),
        grid=(L, M // tn),
        in_specs=[pl.BlockSpec((B, D), lambda l, j: (0, 0)),
                  pl.BlockSpec((1, D, tn), lambda l, j: (l, 0, j)),
                  pl.BlockSpec((1, 1, tn), lambda l, j: (l, 0, j))],
        out_specs=pl.BlockSpec((1, B, tn), lambda l, j: (l, 0, j)),
        compiler_params=_cparams(("arbitrary", "arbitrary")),
        name="adaln_mod",
    )(c, ada_w, ada_b.reshape(L, 1, M))


def _inproj_kernel(x_ref, sc_ref, sh_ref, g_ref, w_ref, o_ref):
    h = _rms_mod(x_ref[0], g_ref[...], sc_ref[0], sh_ref[0])
    o_ref[0] = _mm(h, w_ref[...])


def _in_projection(x, sc, sh, gain, w, tm):
    B, T, D = x.shape
    n = w.shape[1]
    return pl.pallas_call(
        _inproj_kernel,
        out_shape=jax.ShapeDtypeStruct((B, T, n), F32),
        grid=(B, T // tm),
        in_specs=[pl.BlockSpec((1, tm, D), lambda b, i: (b, i, 0)),
                  pl.BlockSpec((1, 1, D), lambda b, i: (b, 0, 0)),
                  pl.BlockSpec((1, 1, D), lambda b, i: (b, 0, 0)),
                  pl.BlockSpec((1, D), lambda b, i: (0, 0)),
                  pl.BlockSpec((D, n), lambda b, i: (0, 0))],
        out_specs=pl.BlockSpec((1, tm, n), lambda b, i: (b, i, 0)),
        compiler_params=_cparams(("arbitrary", "arbitrary")),
        name="in_proj",
    )(x, sc, sh, gain, w)


def _gla_kernel(qk_ref, v_ref, og_ref, gz_ref, w2_ref, gb_ref, ng_ref, o_ref, s_ref, g_ref, r_ref,
                qd_ref, kd_ref, oacc_ref, *, n_chunk):
    C = GLA_CHUNK
    hk = qk_ref.shape[2] // 2
    hv = v_ref.shape[2]
    dk = hk // GLA_HEADS
    dv = hv // GLA_HEADS

    tb = n_chunk * C

    @pl.when(pl.program_id(1) == 0)
    def _():
        s_ref[...] = jnp.zeros_like(s_ref)

    z = _mm(gz_ref[0], w2_ref[...]) + gb_ref[...]
    gk = (jnp.minimum(z, 0.0) - jnp.log(1.0 + jnp.exp(-jnp.abs(z)))) / GLA_GATE_NORMALIZER
    tri = (lax.broadcasted_iota(jnp.int32, (C, C), 1)
           <= lax.broadcasted_iota(jnp.int32, (C, C), 0)).astype(BF16)
    sums = _bmm(jnp.broadcast_to(tri[None], (n_chunk, C, C)), _hi_lo(gk).reshape(n_chunk, C, 2 * hk))
    G3 = sums[:, :, 0:hk] + sums[:, :, hk:2 * hk]
    G_all = G3.reshape(tb, hk)
    R_all = (jnp.broadcast_to(G3[:, C - 1:C, :], (n_chunk, C, hk)) - G3).reshape(tb, hk)
    g_ref[...] = G_all
    r_ref[...] = R_all
    qd_ref[...] = (qk_ref[0, :, 0:hk] * (dk ** -0.5) * jnp.exp(G_all)).astype(BF16)
    kd_ref[...] = (qk_ref[0, :, hk:2 * hk] * jnp.exp(R_all)).astype(BF16)

    causal = (lax.broadcasted_iota(jnp.int32, (C, C, hk), 1)
              <= lax.broadcasted_iota(jnp.int32, (C, C, hk), 0))
    er = lax.broadcasted_iota(jnp.int32, (hk, hv), 0) // dk
    ec = lax.broadcasted_iota(jnp.int32, (hk, hv), 1) // dv
    expand = (er == ec).astype(BF16)
    sr = lax.broadcasted_iota(jnp.int32, (hv, hk), 0) // dv
    scol = lax.broadcasted_iota(jnp.int32, (hv, hk), 1) // dk
    state_mask = sr == scol
    head_mean = (_group_matrix(hv, dv).astype(F32) / dv).astype(BF16)

    def chunk(ci, carry):
        r0 = pl.multiple_of(ci * C, C)
        rows = pl.ds(r0, C)
        q = qk_ref[0, rows, 0:hk] * (dk ** -0.5)
        k = qk_ref[0, rows, hk:2 * hk]
        v = v_ref[0, rows, :]
        G = g_ref[rows, :]
        pair = (C, C, hk)
        g_diff = jnp.broadcast_to(G[:, None, :], pair) - jnp.broadcast_to(G[None], pair)
        decay = jnp.exp(jnp.where(causal, g_diff, -jnp.inf))
        p = jnp.broadcast_to(q[:, None, :], pair) * jnp.broadcast_to(k[None], pair) * decay
        a_exp = jnp.dot(p.reshape(C * C, hk).astype(BF16), expand, preferred_element_type=F32)
        v_j = jnp.broadcast_to(v[None], (C, C, hv))
        o_intra = jnp.sum(a_exp.reshape(C, C, hv) * v_j, axis=1)
        g_total = G[0:1, :] + r_ref[pl.ds(r0, 8), :][0:1, :]
        s = s_ref[...]
        o_inter = lax.dot_general(qd_ref[rows, :], s.astype(BF16), (((1,), (1,)), ((), ())),
                                  preferred_element_type=F32)
        kv = lax.dot_general(v.astype(BF16), kd_ref[rows, :], (((0,), (0,)), ((), ())),
                             preferred_element_type=F32)
        s_ref[...] = s * jnp.exp(g_total) + jnp.where(state_mask, kv, 0.0)
        oacc_ref[rows, :] = o_intra + o_inter
        return carry

    lax.fori_loop(0, n_chunk, chunk, 0)

    o = oacc_ref[...]
    ms = jnp.dot(_hi_lo(o * o), jnp.concatenate([head_mean, head_mean], axis=0),
                 preferred_element_type=F32)
    o = o * lax.rsqrt(ms + RMS_EPS) * ng_ref[...] * _silu(og_ref[0])
    o_ref[0] = o.astype(o_ref.dtype)


def _gla(proj, w2p, gb, ng, col_qk, col_v, col_og, col_gz, tb):
    B, T, _ = proj.shape
    hv = ng.shape[1]
    hk = gb.shape[1]
    return pl.pallas_call(
        functools.partial(_gla_kernel, n_chunk=tb // GLA_CHUNK),
        out_shape=jax.ShapeDtypeStruct((B, T, hv), BF16),
        grid=(B, T // tb),
        in_specs=[pl.BlockSpec((1, tb, 2 * hk), lambda b, j: (b, j, col_qk)),
                  pl.BlockSpec((1, tb, hv), lambda b, j: (b, j, col_v)),
                  pl.BlockSpec((1, tb, hv), lambda b, j: (b, j, col_og)),
                  pl.BlockSpec((1, tb, LANES), lambda b, j: (b, j, col_gz)),
                  pl.BlockSpec(w2p.shape, lambda b, j: (0, 0)),
                  pl.BlockSpec(gb.shape, lambda b, j: (0, 0)),
                  pl.BlockSpec(ng.shape, lambda b, j: (0, 0))],
        out_specs=pl.BlockSpec((1, tb, hv), lambda b, j: (b, j, 0)),
        scratch_shapes=[pltpu.VMEM((hv, hk), F32), pltpu.VMEM((tb, hk), F32), pltpu.VMEM((tb, hk), F32),
                        pltpu.VMEM((tb, hk), BF16), pltpu.VMEM((tb, hk), BF16), pltpu.VMEM((tb, hv), F32)],
        compiler_params=_cparams(("arbitrary", "arbitrary")),
        name="gla_mixer",
    )(proj, proj, proj, proj, w2p, gb, ng)


def _rope(t, cos, sin_signed):
    d = 32
    half = d // 2
    out = []
    for s in range(t.shape[1] // LANES):
        x = t[:, s * LANES:(s + 1) * LANES]
        lane = lax.broadcasted_iota(jnp.int32, x.shape, 1)
        up = pltpu.roll(x, LANES - half, 1)
        down = pltpu.roll(x, half, 1)
        rot = jnp.where((lane % d) < half, up, down)
        out.append(x * cos + rot * sin_signed)
    return jnp.concatenate(out, axis=1)


def _diff_kernel(q_ref, k_ref, v_ref, cosq_ref, sinq_ref, cosk_ref, sink_ref, lam_ref, g_ref, o_ref,
                 ks, vs, *, tq, lam_init):
    H = DIFF_HEADS
    d = q_ref.shape[2] // (2 * H)
    dv = v_ref.shape[2] // H
    j = pl.program_id(1)

    @pl.when(j == 0)
    def _():
        k = _rope(k_ref[0], cosk_ref[0], sink_ref[0])
        v = v_ref[0]
        for hc in range(2 * H):
            ks[hc] = k[:, hc * d:(hc + 1) * d].astype(BF16)
        ones_col = (lax.broadcasted_iota(jnp.int32, (v.shape[0], dv), 1) == 0).astype(BF16)
        for h in range(H):
            vs[h] = jnp.concatenate([v[:, h * dv:(h + 1) * dv].astype(BF16), ones_col], axis=1)

    q = _rope(q_ref[0], cosq_ref[0], sinq_ref[0]) * (d ** -0.5)
    lp = lam_ref[...]
    lam = (jnp.exp(jnp.sum(lp[0:1] * lp[1:2], axis=-1, keepdims=True))
           - jnp.exp(jnp.sum(lp[2:3] * lp[3:4], axis=-1, keepdims=True)) + lam_init)
    on_or_below_diag = (lax.broadcasted_iota(jnp.int32, (tq, tq), 1)
                        <= lax.broadcasted_iota(jnp.int32, (tq, tq), 0))

    def update(qh, kh, vh, m, acc, masked):
        s = lax.dot_general(qh, kh, (((1,), (1,)), ((), ())), preferred_element_type=F32)
        if masked:
            s = jnp.where(on_or_below_diag, s, -jnp.inf)
        m_new = jnp.maximum(m, jnp.max(s, axis=-1, keepdims=True))
        p = jnp.exp(s - m_new).astype(BF16)
        acc = jnp.exp(m - m_new) * acc + jnp.dot(p, vh, preferred_element_type=F32)
        return m_new, acc

    qs = [q[:, hc * d:(hc + 1) * d].astype(BF16) for hc in range(2 * H)]

    def kv_block(kb, carry, masked):
        rows = pl.ds(pl.multiple_of(kb * tq, tq), tq)
        out = []
        for hc in range(2 * H):
            m, acc = carry[2 * hc], carry[2 * hc + 1]
            out.extend(update(qs[hc], ks[hc, rows, :], vs[hc // 2, rows, :], m, acc, masked))
        return tuple(out)

    m0 = jnp.full((tq, 1), -jnp.inf, F32)
    a0 = jnp.zeros((tq, 2 * dv), F32)
    carry = lax.fori_loop(0, j, lambda kb, c: kv_block(kb, c, False), (m0, a0) * (2 * H))
    carry = kv_block(j, carry, True)

    for h in range(H):
        a1, a2 = carry[4 * h + 1], carry[4 * h + 3]
        comp = [a[:, 0:dv] / a[:, dv:dv + 1] for a in (a1, a2)]
        o = comp[0] - lam * comp[1]
        o = o * lax.rsqrt(jnp.mean(o * o, axis=-1, keepdims=True) + RMS_EPS)
        o = o * g_ref[...] * (1.0 - lam_init)
        o_ref[0, :, h * dv:(h + 1) * dv] = o.astype(o_ref.dtype)


def _diff_attention(proj, cos, sin, lam_p, g, col_q, col_k, col_v, layer_idx):
    B, T, _ = proj.shape
    H = DIFF_HEADS
    dv = g.shape[1]
    d = dv // 2
    w = 2 * H * d
    assert cos.shape[2] == LANES
    tq = min(512, T)
    lam_init = 0.8 - 0.6 * math.exp(-0.3 * layer_idx)
    blk = lambda col: pl.BlockSpec((1, tq, w), lambda b, j: (b, j, col))
    full = lambda col: pl.BlockSpec((1, T, w), lambda b, j: (b, 0, col))
    tab_blk = pl.BlockSpec((1, tq, LANES), lambda b, j: (b, j, 0))
    tab_full = pl.BlockSpec((1, T, LANES), lambda b, j: (b, 0, 0))
    return pl.pallas_call(
        functools.partial(_diff_kernel, tq=tq, lam_init=lam_init),
        out_shape=jax.ShapeDtypeStruct((B, T, H * dv), BF16),
        grid=(B, T // tq),
        in_specs=[blk(col_q), full(col_k), full(col_v), tab_blk, tab_blk, tab_full, tab_full,
                  pl.BlockSpec(lam_p.shape, lambda b, j: (0, 0)),
                  pl.BlockSpec(g.shape, lambda b, j: (0, 0))],
        out_specs=pl.BlockSpec((1, tq, H * dv), lambda b, j: (b, j, 0)),
        scratch_shapes=[pltpu.VMEM((2 * H, T, d), BF16),
                        pltpu.VMEM((H, T, 2 * dv), BF16)],
        compiler_params=_cparams(("arbitrary", "arbitrary")),
        name="diff_attention",
    )(proj, proj, proj, cos, sin, cos, sin, lam_p, g)


def _conv_kernel(a_ref, b_ref, w_ref, cb_ref, lg_ref, lb_ref, o_ref, u_ref, *, rb):
    T = a_ref.shape[1]
    pad = u_ref.shape[0] - T
    u_ref[0:pad, :] = jnp.zeros((pad, u_ref.shape[1]), F32)
    u_ref[pad:pad + T, :] = a_ref[0] * _sigmoid(b_ref[0])
    first = pad - (CONV_WIDTH - 1)

    def block(i, carry):
        r0 = pl.multiple_of(i * rb, rb)
        win = u_ref[pl.ds(r0, rb + pad), :]
        acc = jnp.zeros((rb, u_ref.shape[1]), F32)
        for s in range(SUBLANES):
            taps = [j for j in range(CONV_WIDTH) if (first + j) % SUBLANES == s]
            rolled = win if s == 0 else pltpu.roll(win, rb + pad - s, 0)
            for j in taps:
                a0 = first + j - s
                acc = acc + w_ref[j:j + 1, :] * rolled[a0:a0 + rb, :]
        y = acc + cb_ref[...]
        mu = jnp.mean(y, axis=-1, keepdims=True)
        yc = y - mu
        var = jnp.mean(yc * yc, axis=-1, keepdims=True)
        y = yc * lax.rsqrt(var + LN_EPS) * lg_ref[...] + lb_ref[...]
        o_ref[0, pl.ds(r0, rb), :] = _silu(y).astype(o_ref.dtype)
        return carry

    lax.fori_loop(0, T // rb, block, 0)


def _conformer_conv(proj, w, cb, lg, lb, col_a, col_b):
    B, T, _ = proj.shape
    ch = w.shape[1]
    rb = min(128, T)
    return pl.pallas_call(
        functools.partial(_conv_kernel, rb=rb),
        out_shape=jax.ShapeDtypeStruct((B, T, ch), BF16),
        grid=(B,),
        in_specs=[pl.BlockSpec((1, T, ch), lambda b: (b, 0, col_a)),
                  pl.BlockSpec((1, T, ch), lambda b: (b, 0, col_b)),
                  pl.BlockSpec(w.shape, lambda b: (0, 0)),
                  pl.BlockSpec(cb.shape, lambda b: (0, 0)),
                  pl.BlockSpec(lg.shape, lambda b: (0, 0)),
                  pl.BlockSpec(lb.shape, lambda b: (0, 0))],
        out_specs=pl.BlockSpec((1, T, ch), lambda b: (b, 0, 0)),
        scratch_shapes=[pltpu.VMEM((T + 32, ch), F32)],
        compiler_params=_cparams(("arbitrary",)),
        name="conformer_conv",
    )(proj, proj, w, cb, lg, lb)


def _rwkv_kernel(x_ref, mu_ref, w0_ref, w2_ref, a0_ref, a2_ref, g2_ref, kk_ref, ka_ref, rk_ref,
                 lng_ref, lnb_ref, o_ref, s_ref, prev_ref, gate_ref, bonus_ref, dec_ref, qeff_ref, yloc_ref,
                 mlow_ref, nc_ref, y_ref, *, n_chunk):
    C = RWKV_CHUNK
    H = RWKV_HEADS
    W = o_ref.shape[2]
    N = W // H

    @pl.when(pl.program_id(1) == 0)
    def _():
        s_ref[...] = jnp.zeros_like(s_ref)
        prev_ref[...] = jnp.zeros_like(prev_ref)

    lane = lax.broadcasted_iota(jnp.int32, (1, W), 1)
    head_mask = [(lane // N == h).astype(F32) for h in range(H)]
    block_diag = _group_matrix(W, N)
    head_sum = block_diag.astype(F32)
    ti = lax.broadcasted_iota(jnp.int32, (C, C), 0)
    tj = lax.broadcasted_iota(jnp.int32, (C, C), 1)
    tril_incl = ti >= tj
    tril_strict = ti > tj
    same_sub = (ti // RWKV_SUB) == (tj // RWKV_SUB)
    eye = (ti == tj).astype(F32)
    nc = n_chunk
    tb = nc * C

    def head_total(t, two_term=False):
        ones = head_sum.astype(BF16)
        if two_term:
            return jnp.dot(_hi_lo(t), jnp.concatenate([ones, ones], axis=0), preferred_element_type=F32)
        return jnp.dot(t.astype(BF16), ones, preferred_element_type=F32)

    x = x_ref[0]
    first_row = lax.broadcasted_iota(jnp.int32, (tb, 1), 0) == 0
    prev = jnp.where(first_row, prev_ref[...], pltpu.roll(x, 1, 0))
    prev_ref[...] = x[tb - 1:tb, :]
    xm = x + (prev - x) * mu_ref[...]
    r = xm[:, 0:W]
    k = xm[:, W:2 * W]
    v = xm[:, 2 * W:3 * W]
    zz = xm[:, 3 * W:3 * W + LANES]
    zg = xm[:, 3 * W + LANES:]
    w = -_softplus(-(w0_ref[...] + _mm(jnp.tanh(zz), w2_ref[...]))) - 0.5
    lw = -jnp.exp(w)
    a = _sigmoid(a0_ref[...] + _mm(zz, a2_ref[...]))
    gate_ref[...] = _mm(_sigmoid(zg), g2_ref[...])
    kk = k * kk_ref[...]
    kk = kk / jnp.maximum(jnp.sqrt(head_total(kk * kk, two_term=True)), 1e-12)
    k = k * (1.0 + (a - 1.0) * ka_ref[...])
    b = kk * a
    bonus_ref[...] = head_total(r * k * rk_ref[...]) * v

    c3 = lambda t: t.reshape(nc, C, t.shape[1])
    sums = _bmm(jnp.broadcast_to(tril_incl.astype(BF16)[None], (nc, C, C)), c3(_hi_lo(lw)))
    G3 = sums[:, :, 0:W] + sums[:, :, W:2 * W]
    total3 = jnp.broadcast_to(G3[:, C - 1:C, :], (nc, C, W))
    dec_ref[...] = jnp.exp(total3).reshape(tb, W)
    G = G3.reshape(tb, W)
    inv = jnp.exp(-G)
    to_end = jnp.exp(total3 - G3).reshape(tb, W)
    kap = kk * jnp.exp(G - lw)
    rho = r * jnp.exp(G)
    kap3, rho3, v3 = c3(kap), c3(rho), c3(v)
    bet_kt3 = jnp.concatenate([c3(b * inv), c3(k * inv)], axis=1)
    betc3, kc3 = c3(b * to_end), c3(k * to_end)

    def bmm_nt(p, q):
        return lax.dot_general(p.astype(BF16), q.astype(BF16), (((2,), (2,)), ((0,), (0,))),
                               preferred_element_type=F32)

    def bmm_tn(p, q):
        return lax.dot_general(p.astype(BF16), q.astype(BF16), (((1,), (1,)), ((0,), (0,))),
                               preferred_element_type=F32)

    wi = lax.broadcasted_iota(jnp.int32, (C, 2 * C), 0)
    wj = lax.broadcasted_iota(jnp.int32, (C, 2 * C), 1) % C
    A_bk, B_bk = [], []
    for h in range(H):
        lhs = jnp.concatenate([c3(kap * head_mask[h]), c3(rho * head_mask[h])], axis=1)
        prod = bmm_nt(lhs, bet_kt3)
        A_bk.append(jnp.where(wj < wi, prod[:, 0:C], 0.0))
        B_bk.append(jnp.where(wj <= wi, prod[:, C:2 * C], 0.0))
    A_b = jnp.concatenate([t[:, :, 0:C] for t in A_bk], axis=0)
    B_b = [t[:, :, 0:C] for t in B_bk]
    v3_low = jnp.concatenate([jnp.zeros_like(v3), v3], axis=1)

    Dg = jnp.where(same_sub, A_b, 0.0)
    Lo = A_b - Dg
    D2 = _bmm(Dg, Dg)
    D4 = _bmm(D2, D2)
    D8 = _bmm(D4, D4)
    Dinv = _bmm(_bmm(_bmm(eye - Dg, eye + D2), eye + D4), eye + D8)
    Nn = _bmm(Dinv, Lo)
    N2 = _bmm(Nn, Nn)
    Tm = _bmm(_bmm(eye - Nn, eye + N2), Dinv)
    Tm = [Tm[h * nc:(h + 1) * nc] for h in range(H)]

    def per_head(mats, t):
        reps = t.shape[2] // W
        acc = None
        for h in range(H):
            m = head_mask[h] if reps == 1 else jnp.concatenate([head_mask[h]] * reps, axis=1)
            u = m * _bmm(mats[h], t)
            acc = u if acc is None else acc + u
        return acc

    akv = per_head(A_bk, v3_low)
    tk = per_head(Tm, jnp.concatenate([kap3, akv], axis=2))
    kap_p = tk[:, :, 0:W]
    v_p = tk[:, :, W:2 * W]
    bb = per_head(B_b, jnp.concatenate([kap_p, v_p], axis=2))
    qeff_ref[...] = (rho3 - bb[:, :, 0:W]).astype(BF16)
    yloc_ref[...] = per_head(B_bk, v3_low) - bb[:, :, W:2 * W]
    mlow_ref[...] = jnp.where(block_diag, bmm_tn(kap_p, betc3), 0.0).astype(BF16)
    nc_ref[...] = jnp.where(block_diag, bmm_tn(jnp.concatenate([v3, v_p], axis=1),
                                               jnp.concatenate([kc3, -betc3], axis=1)), 0.0)

    def chunk(ci, carry):
        r0 = pl.multiple_of(ci * C, C)
        s = s_ref[...]
        sb = s.astype(BF16)
        y = lax.dot_general(qeff_ref[ci], sb, (((1,), (1,)), ((), ())), preferred_element_type=F32)
        y_ref[pl.ds(r0, C), :] = y + yloc_ref[ci]
        s_ref[...] = (s * dec_ref[pl.ds(r0, 8), :][0:1, :]
                      - jnp.dot(sb, mlow_ref[ci], preferred_element_type=F32) + nc_ref[ci])
        return carry

    lax.fori_loop(0, nc, chunk, 0)

    y = y_ref[...]
    mean = head_total(y, two_term=True) / N
    yc = y - mean
    var = head_total(yc * yc) / N
    yn = yc * lax.rsqrt(var + RWKV_GN_EPS) * lng_ref[...] + lnb_ref[...]
    o_ref[0] = ((yn + bonus_ref[...]) * gate_ref[...]).astype(o_ref.dtype)


def _rwkv7(proj, vecs, w2p, a2p, g2, col, tb):
    B, T, _ = proj.shape
    W = g2.shape[1]
    cols = vecs["mu"].shape[1]
    names = ("mu", "w0", "w2", "a0", "a2", "g2", "kk", "ka", "rk", "lng", "lnb")
    params = dict(vecs, w2=w2p, a2=a2p, g2=g2)
    const = lambda arr: pl.BlockSpec(arr.shape, lambda b, j: (0, 0))
    n_chunk = tb // RWKV_CHUNK
    return pl.pallas_call(
        functools.partial(_rwkv_kernel, n_chunk=n_chunk),
        out_shape=jax.ShapeDtypeStruct((B, T, W), BF16),
        grid=(B, T // tb),
        in_specs=[pl.BlockSpec((1, tb, cols), lambda b, j: (b, j, col))]
                 + [const(params[n]) for n in names],
        out_specs=pl.BlockSpec((1, tb, W), lambda b, j: (b, j, 0)),
        scratch_shapes=[pltpu.VMEM((W, W), F32), pltpu.VMEM((1, cols), F32),
                        pltpu.VMEM((tb, W), F32), pltpu.VMEM((tb, W), F32), pltpu.VMEM((tb, W), F32),
                        pltpu.VMEM((n_chunk, RWKV_CHUNK, W), BF16), pltpu.VMEM((n_chunk, RWKV_CHUNK, W), F32),
                        pltpu.VMEM((n_chunk, W, W), BF16), pltpu.VMEM((n_chunk, W, W), F32),
                        pltpu.VMEM((tb, W), F32)],
        compiler_params=_cparams(("arbitrary", "arbitrary")),
        name="rwkv7_mixer",
    )(proj, *[params[n] for n in names])


def _merge_kernel(x_ref, sc_ref, sh_ref, gt_ref, gpre_ref, gpost_ref, wg_ref, b0_ref, b1_ref, b2_ref,
                  b3_ref, wb_ref, wo_ref, o_ref):
    x = x_ref[0]
    D = x.shape[1]
    h = _rms_mod(x, gpre_ref[...], sc_ref[0], sh_ref[0]).astype(BF16)
    merged = None
    for g, br in enumerate((b0_ref, b1_ref, b2_ref, b3_ref)):
        gate = _sigmoid(jnp.dot(h, wg_ref[:, g * D:(g + 1) * D], preferred_element_type=F32))
        t = gate * jnp.dot(br[0], wb_ref[g], preferred_element_type=F32)
        merged = t if merged is None else merged + t
    y = _mm(merged, wo_ref[...])
    y = y * lax.rsqrt(jnp.mean(y * y, axis=-1, keepdims=True) + RMS_EPS) * gpost_ref[...]
    o_ref[0] = x + gt_ref[0] * y


def _merge(x, sc, sh, gt, gpre, gpost, w_gate, branches, w_branch, w_out, tm):
    B, T, D = x.shape
    bw = branches[0].shape[2]
    tok = lambda n: pl.BlockSpec((1, tm, n), lambda b, i: (b, i, 0))
    vec = pl.BlockSpec((1, 1, D), lambda b, i: (b, 0, 0))
    const2 = lambda arr: pl.BlockSpec(arr.shape, lambda b, i: (0, 0))
    return pl.pallas_call(
        _merge_kernel,
        out_shape=jax.ShapeDtypeStruct((B, T, D), F32),
        grid=(B, T // tm),
        in_specs=[tok(D), vec, vec, vec, const2(gpre), const2(gpost), const2(w_gate),
                  tok(bw), tok(bw), tok(bw), tok(bw),
                  pl.BlockSpec(w_branch.shape, lambda b, i: (0, 0, 0)), const2(w_out)],
        out_specs=tok(D),
        compiler_params=_cparams(("arbitrary", "arbitrary")),
        name="merge_out_proj",
    )(x, sc, sh, gt, gpre, gpost, w_gate, *branches, w_branch, w_out)


def _ffn_kernel(x_ref, sc_ref, sh_ref, gt_ref, gpre_ref, gpost_ref, wg_ref, wu_ref, wd_ref, o_ref, *, tf):
    x = x_ref[0]
    h = _rms_mod(x, gpre_ref[...], sc_ref[0], sh_ref[0]).astype(BF16)
    F = wg_ref.shape[1]
    acc = None
    for f0 in range(0, F, tf):
        f1 = min(f0 + tf, F)
        gate = jnp.dot(h, wg_ref[:, f0:f1], preferred_element_type=F32)
        up = jnp.dot(h, wu_ref[:, f0:f1], preferred_element_type=F32)
        t = jnp.dot((_silu(gate) * up).astype(BF16), wd_ref[f0:f1, :], preferred_element_type=F32)
        acc = t if acc is None else acc + t
    y = acc * lax.rsqrt(jnp.mean(acc * acc, axis=-1, keepdims=True) + RMS_EPS) * gpost_ref[...]
    o_ref[0] = x + gt_ref[0] * y


def _dense_ffn(x, sc, sh, gt, gpre, gpost, wg, wu, wd, tm):
    B, T, D = x.shape
    tok = pl.BlockSpec((1, tm, D), lambda b, i: (b, i, 0))
    vec = pl.BlockSpec((1, 1, D), lambda b, i: (b, 0, 0))
    const2 = lambda arr: pl.BlockSpec(arr.shape, lambda b, i: (0, 0))
    return pl.pallas_call(
        functools.partial(_ffn_kernel, tf=512),
        out_shape=jax.ShapeDtypeStruct((B, T, D), F32),
        grid=(B, T // tm),
        in_specs=[tok, vec, vec, vec, const2(gpre), const2(gpost), const2(wg), const2(wu), const2(wd)],
        out_specs=tok,
        compiler_params=_cparams(("arbitrary", "arbitrary")),
        name="dense_swiglu",
    )(x, sc, sh, gt, gpre, gpost, wg, wu, wd)


MOE_TOKEN_TILE = 256
MOE_ROW_TILE = 512
SEG_ALIGN = 16
SEG_PIECES = (256, 128, 64, 32, 16)
MOE_SMALL_SEG = 128


def _route_kernel(x_ref, sc_ref, sh_ref, gpre_ref, rw_ref, rb_ref, h_ref, mi_ref, mp_ref, cnt_ref):
    tm = x_ref.shape[1]
    h = _rms_mod(x_ref[0], gpre_ref[...], sc_ref[0], sh_ref[0])
    h_ref[...] = h.astype(BF16)
    logits = _mm_f32(h, rw_ref[...]) + rb_ref[...]
    lane = lax.broadcasted_iota(jnp.int32, logits.shape, 1)
    v1 = jnp.max(logits, axis=-1, keepdims=True)
    i1 = jnp.min(jnp.where(logits == v1, lane, LANES), axis=-1, keepdims=True)
    rest = jnp.where(lane == i1, -jnp.inf, logits)
    v2 = jnp.max(rest, axis=-1, keepdims=True)
    i2 = jnp.min(jnp.where(rest == v2, lane, LANES), axis=-1, keepdims=True)
    e2 = jnp.exp(v2 - v1)
    p1 = 1.0 / (1.0 + e2)
    p2 = e2 / (1.0 + e2)
    oh1 = (lane == i1).astype(F32)
    oh2 = (lane == i2).astype(F32)
    both = oh1 + oh2
    earlier = (lax.broadcasted_iota(jnp.int32, (tm, tm), 1)
               < lax.broadcasted_iota(jnp.int32, (tm, tm), 0)).astype(BF16)
    before = jnp.dot(earlier, both.astype(BF16), preferred_element_type=F32)
    r1 = jnp.sum(oh1 * before, axis=-1, keepdims=True).astype(jnp.int32)
    r2 = jnp.sum(oh2 * before, axis=-1, keepdims=True).astype(jnp.int32)
    col = lax.broadcasted_iota(jnp.int32, mi_ref.shape, 1)
    mi_ref[...] = jnp.where(col == 0, i1, jnp.where(col == 1, i2, jnp.where(col == 2, r1,
                                                                           jnp.where(col == 3, r2, 0))))
    mp_ref[...] = jnp.where(col == 0, p1, jnp.where(col == 1, p2, 0.0))
    cnt_ref[0] = jnp.sum(both, axis=0, keepdims=True).astype(jnp.int32)


def _segment_pieces(n_rows):
    out = []
    for s in SEG_PIECES:
        if s == MOE_TOKEN_TILE:
            out.append((n_rows == s, 0, s))
        else:
            out.append(((n_rows & s) != 0, pl.multiple_of((n_rows // (2 * s)) * (2 * s), SEG_ALIGN), s))
    return out


def _all_segments_small(cnt_ref, tile):
    most = cnt_ref[tile * N_EXPERTS]
    for e in range(1, N_EXPERTS):
        most = jnp.maximum(most, cnt_ref[tile * N_EXPERTS + e])
    return most <= MOE_SMALL_SEG


def _dispatch_kernel(seg_ref, cnt_ref, h_ref, mit_ref, init_ref, xs_ref, buf_ref, sem):
    del init_ref
    tm = h_ref.shape[0]
    i = pl.program_id(0)
    e1, e2 = mit_ref[0:1, :], mit_ref[1:2, :]
    r1, r2 = mit_ref[2:3, :], mit_ref[3:4, :]
    slot = i % 2

    def compact(cap):
        row = lax.broadcasted_iota(jnp.int32, (cap, tm), 0)
        select = jnp.concatenate(
            [jnp.logical_or(jnp.logical_and(e1 == e, r1 == row), jnp.logical_and(e2 == e, r2 == row))
             for e in range(N_EXPERTS)], axis=0).astype(BF16)
        rows = jnp.dot(select, h_ref[...], preferred_element_type=F32).astype(BF16)
        for e in range(N_EXPERTS):
            buf_ref[slot, e * tm:e * tm + cap, :] = rows[e * cap:(e + 1) * cap]

    small = _all_segments_small(cnt_ref, i)
    pl.when(small)(functools.partial(compact, MOE_SMALL_SEG))
    pl.when(jnp.logical_not(small))(functools.partial(compact, tm))

    def segment_copies(tile, buf, e):
        n = cnt_ref[tile * N_EXPERTS + e]
        n_rows = ((n + SEG_ALIGN - 1) // SEG_ALIGN) * SEG_ALIGN
        dst = pl.multiple_of(seg_ref[tile * N_EXPERTS + e], SEG_ALIGN)
        return [(cond, pltpu.make_async_copy(buf_ref.at[buf, pl.ds(e * tm + off, s), :],
                                             xs_ref.at[pl.ds(dst + off, s), :], sem))
                for cond, off, s in _segment_pieces(n_rows)]

    def for_all_segments(tile, buf, action):
        for e in range(N_EXPERTS):
            for cond, cp in segment_copies(tile, buf, e):
                pl.when(cond)(getattr(cp, action))

    @pl.when(i > 0)
    def _():
        for_all_segments(i - 1, 1 - slot, "wait")

    for_all_segments(i, slot, "start")

    @pl.when(i == pl.num_programs(0) - 1)
    def _():
        for_all_segments(i, slot, "wait")


def _expert_kernel(te_ref, nv_ref, xs_ref, wg_ref, wu_ref, wd_ref, ys_ref, acc_ref):
    del te_ref
    r = pl.program_id(0)
    f = pl.program_id(1)
    valid = r < nv_ref[0]

    @pl.when(jnp.logical_and(valid, f == 0))
    def _():
        acc_ref[...] = jnp.zeros_like(acc_ref)

    @pl.when(valid)
    def _():
        x = xs_ref[...]
        tf = wg_ref.shape[2]
        acc = acc_ref[...]
        for f0 in range(0, tf, 512):
            f1 = min(f0 + 512, tf)
            gate = jnp.dot(x, wg_ref[0, :, f0:f1], preferred_element_type=F32)
            up = jnp.dot(x, wu_ref[0, :, f0:f1], preferred_element_type=F32)
            acc = acc + jnp.dot((_silu(gate) * up).astype(BF16), wd_ref[0, f0:f1, :],
                                preferred_element_type=F32)
        acc_ref[...] = acc

    @pl.when(f == pl.num_programs(1) - 1)
    def _():
        @pl.when(valid)
        def _():
            ys_ref[...] = acc_ref[...].astype(ys_ref.dtype)

        @pl.when(jnp.logical_not(valid))
        def _():
            ys_ref[...] = jnp.zeros_like(ys_ref)


def _combine_kernel(seg_ref, cnt_ref, ys_ref, mi_ref, mp_ref, x_ref, gt_ref, gpost_ref, o_ref, win_ref,
                    pair_ref, sem):
    tm = x_ref.shape[1]
    i = pl.program_id(0) * pl.num_programs(1) + pl.program_id(1)
    n_tiles = pl.num_programs(0) * pl.num_programs(1)
    slot = i % 2

    def window_copy(tile, buf, e, rows):
        src = pl.multiple_of(seg_ref[tile * N_EXPERTS + e], SEG_ALIGN)
        return pltpu.make_async_copy(ys_ref.at[pl.ds(src, rows), :], win_ref.at[buf, pl.ds(e * tm, rows), :],
                                     sem.at[buf, e])

    def for_all_windows(tile, buf, action):
        small = _all_segments_small(cnt_ref, tile)
        for rows, cond in ((MOE_SMALL_SEG, small), (tm, jnp.logical_not(small))):
            @pl.when(cond)
            def _():
                for e in range(N_EXPERTS):
                    getattr(window_copy(tile, buf, e, rows), action)()

    @pl.when(i == 0)
    def _():
        for_all_windows(i, slot, "start")

    @pl.when(i + 1 < n_tiles)
    def _():
        for_all_windows(i + 1, 1 - slot, "start")

    for_all_windows(i, slot, "wait")
    e1, e2 = mi_ref[:, 0:1], mi_ref[:, 1:2]
    r1, r2 = mi_ref[:, 2:3], mi_ref[:, 3:4]

    def expand_rows(cap):
        col = lax.broadcasted_iota(jnp.int32, (tm, cap), 1)
        acc = None
        for e in range(N_EXPERTS):
            expand = jnp.concatenate([jnp.logical_and(e1 == e, r1 == col),
                                      jnp.logical_and(e2 == e, r2 == col)], axis=0).astype(BF16)
            t = jnp.dot(expand, win_ref[slot, pl.ds(e * tm, cap), :], preferred_element_type=F32)
            acc = t if acc is None else acc + t
        pair_ref[...] = acc

    small = _all_segments_small(cnt_ref, i)
    pl.when(small)(functools.partial(expand_rows, MOE_SMALL_SEG))
    pl.when(jnp.logical_not(small))(functools.partial(expand_rows, tm))
    y = mp_ref[:, 0:1] * pair_ref[0:tm, :] + mp_ref[:, 1:2] * pair_ref[tm:2 * tm, :]
    y = y * lax.rsqrt(jnp.mean(y * y, axis=-1, keepdims=True) + RMS_EPS) * gpost_ref[...]
    o_ref[0] = x_ref[0] + gt_ref[0] * y


def _moe_ffn(x, sc, sh, gt, gpre, gpost, rw, rb, wg, wu, wd, tf):
    B, T, D = x.shape
    E, _, F = wg.shape
    tm = min(MOE_TOKEN_TILE, T)
    assert tm == MOE_TOKEN_TILE and E == N_EXPERTS
    nT = T // tm
    n_tok_tiles = B * nT
    N = B * T
    max_rows = 2 * N + n_tok_tiles * E * (SEG_ALIGN - 1) + E * (MOE_ROW_TILE - SEG_ALIGN)
    n_row_tiles = -(-max_rows // MOE_ROW_TILE) + 1
    P = n_row_tiles * MOE_ROW_TILE

    vec = pl.BlockSpec((1, 1, D), lambda b, j: (b, 0, 0))
    const2 = lambda arr: pl.BlockSpec(arr.shape, lambda b, j: (0, 0))
    flat = lambda n: pl.BlockSpec((tm, n), lambda b, j: (b * nT + j, 0))
    h, mi, mp, cnt = pl.pallas_call(
        _route_kernel,
        out_shape=(jax.ShapeDtypeStruct((N, D), BF16), jax.ShapeDtypeStruct((N, 8), jnp.int32),
                   jax.ShapeDtypeStruct((N, 8), F32), jax.ShapeDtypeStruct((n_tok_tiles, 1, LANES), jnp.int32)),
        grid=(B, nT),
        in_specs=[pl.BlockSpec((1, tm, D), lambda b, j: (b, j, 0)), vec, vec, const2(gpre), const2(rw),
                  const2(rb)],
        out_specs=(flat(D), flat(8), flat(8), pl.BlockSpec((1, 1, LANES), lambda b, j: (b * nT + j, 0, 0))),
        compiler_params=_cparams(("arbitrary", "arbitrary")),
        name="moe_route",
    )(x, sc, sh, gpre, rw, rb)

    counts = cnt[:, 0, :E]
    seg_len = (counts + SEG_ALIGN - 1) // SEG_ALIGN * SEG_ALIGN
    group_len = (jnp.sum(seg_len, axis=0) + MOE_ROW_TILE - 1) // MOE_ROW_TILE * MOE_ROW_TILE
    group_end = jnp.cumsum(group_len)
    seg_start = (group_end - group_len)[None, :] + jnp.cumsum(seg_len, axis=0) - seg_len
    seg_start = seg_start.reshape(-1).astype(jnp.int32)
    counts = counts.reshape(-1)
    n_valid = (group_end[-1:] // MOE_ROW_TILE).astype(jnp.int32)
    tile_first_row = jnp.arange(n_row_tiles, dtype=jnp.int32) * MOE_ROW_TILE
    tile_expert = jnp.minimum(jnp.sum(tile_first_row[:, None] >= group_end[None, :], axis=1), E - 1)
    tile_expert = tile_expert.astype(jnp.int32)

    xs = pl.pallas_call(
        _dispatch_kernel,
        out_shape=jax.ShapeDtypeStruct((P, D), BF16),
        grid_spec=pltpu.PrefetchScalarGridSpec(
            num_scalar_prefetch=2,
            grid=(n_tok_tiles,),
            in_specs=[pl.BlockSpec((tm, D), lambda i, seg, n: (i, 0)),
                      pl.BlockSpec((8, tm), lambda i, seg, n: (0, i)),
                      pl.BlockSpec(memory_space=pl.ANY)],
            out_specs=pl.BlockSpec(memory_space=pl.ANY),
            scratch_shapes=[pltpu.VMEM((2, E * tm, D), BF16), pltpu.SemaphoreType.DMA(())]),
        input_output_aliases={4: 0},
        compiler_params=_cparams(("arbitrary",)),
        name="moe_dispatch",
    )(seg_start, counts, h, mi.T, jnp.zeros((P, D), BF16))

    nf = F // tf
    live = lambda r, f, nv: jnp.where(r < nv[0], f, nf - 1)
    ys = pl.pallas_call(
        _expert_kernel,
        out_shape=jax.ShapeDtypeStruct((P, D), BF16),
        grid_spec=pltpu.PrefetchScalarGridSpec(
            num_scalar_prefetch=2,
            grid=(n_row_tiles, nf),
            in_specs=[pl.BlockSpec((MOE_ROW_TILE, D), lambda r, f, te, nv: (r, 0)),
                      pl.BlockSpec((1, D, tf), lambda r, f, te, nv: (te[r], 0, live(r, f, nv))),
                      pl.BlockSpec((1, D, tf), lambda r, f, te, nv: (te[r], 0, live(r, f, nv))),
                      pl.BlockSpec((1, tf, D), lambda r, f, te, nv: (te[r], live(r, f, nv), 0))],
            out_specs=pl.BlockSpec((MOE_ROW_TILE, D), lambda r, f, te, nv: (r, 0)),
            scratch_shapes=[pltpu.VMEM((MOE_ROW_TILE, D), F32)]),
        compiler_params=_cparams(("arbitrary", "arbitrary")),
        name="moe_experts",
    )(tile_expert, n_valid, xs, wg, wu, wd)

    return pl.pallas_call(
        _combine_kernel,
        out_shape=jax.ShapeDtypeStruct((B, T, D), F32),
        grid_spec=pltpu.PrefetchScalarGridSpec(
            num_scalar_prefetch=2,
            grid=(B, nT),
            in_specs=[pl.BlockSpec(memory_space=pl.ANY),
                      pl.BlockSpec((tm, 8), lambda b, j, seg, n: (b * nT + j, 0)),
                      pl.BlockSpec((tm, 8), lambda b, j, seg, n: (b * nT + j, 0)),
                      pl.BlockSpec((1, tm, D), lambda b, j, seg, n: (b, j, 0)),
                      pl.BlockSpec((1, 1, D), lambda b, j, seg, n: (b, 0, 0)),
                      pl.BlockSpec(gpost.shape, lambda b, j, seg, n: (0, 0))],
            out_specs=pl.BlockSpec((1, tm, D), lambda b, j, seg, n: (b, j, 0)),
            scratch_shapes=[pltpu.VMEM((2, E * tm, D), BF16), pltpu.VMEM((2 * tm, D), F32),
                            pltpu.SemaphoreType.DMA((2, E))]),
        compiler_params=_cparams(("arbitrary", "arbitrary")),
        name="moe_combine",
    )(seg_start, counts, ys, mi, mp, x, gt, gpost)


def _rope_tables(positions, groups):
    d = 32
    inv = 1.0 / (ROPE_THETA ** (jnp.arange(0, d, 2, dtype=F32) / d))
    ang = positions.astype(F32)[..., None] * inv
    cos, sin = jnp.cos(ang), jnp.sin(ang)
    cos = jnp.tile(jnp.concatenate([cos, cos], axis=-1), (1, 1, groups))
    sin = jnp.tile(jnp.concatenate([-sin, sin], axis=-1), (1, 1, groups))
    return cos, sin


def _pad_rows(w, rows, offset):
    out = jnp.zeros((rows, w.shape[1]), w.dtype)
    return out.at[offset:offset + w.shape[0]].set(w)


def kernel(x, c, positions, ada_w, ada_b, norm_mix_pre, norm_mix_post, norm_ffn_pre, norm_ffn_post, w_in, gla_gate_w2, gla_gate_b, gla_norm, diff_lambda, diff_subln, conv_w, conv_b, conv_ln_g, conv_ln_b, rwkv_mu, rwkv_w0, rwkv_w2, rwkv_a0, rwkv_a2, rwkv_g2, rwkv_k_k, rwkv_k_a, rwkv_r_k, rwkv_ln_g, rwkv_ln_b, w_branch, w_out, ffn_w_gate, ffn_w_up, ffn_w_down, router_w, router_b, moe_w_gate, moe_w_up, moe_w_down):
    B, T, D = x.shape
    L = ada_w.shape[0]
    W = D // N_BRANCH
    hk = gla_gate_b.shape[1]
    decay_rank = rwkv_w2.shape[1]
    a_rank = rwkv_a2.shape[1]
    gate_rank = rwkv_g2.shape[1]
    assert decay_rank + a_rank == LANES and 2 * hk == W and gate_rank == LANES
    n_mix = 3 * W + 3 * W + 2 * W + (3 * W + decay_rank + a_rank + gate_rank)
    sizes = (hk, hk, W, W, GLA_GATE_RANK, W, W, W, W, W, 3 * W + LANES + gate_rank, N_BRANCH * D)
    offs = [0]
    for s in sizes:
        offs.append(offs[-1] + s)
    assert offs[-1] == w_in.shape[2]
    tm = min(512, T)
    tb = min(512, T)

    mod = _modulation(c, ada_w, ada_b)
    cos, sin = _rope_tables(positions, LANES // 32)

    for l in range(L):
        m = mod[l].reshape(B, 1, 6 * D)
        sh_m, sc_m, gt_m, sh_f, sc_f, gt_f = [m[:, :, i * D:(i + 1) * D] for i in range(6)]

        wl = w_in[l]
        gz_cols = jnp.zeros((D, LANES), F32).at[:, :GLA_GATE_RANK].set(wl[:, offs[4]:offs[5]])
        w_mix = jnp.concatenate([wl[:, offs[0]:offs[4]], wl[:, offs[5]:offs[11]], gz_cols], axis=1)
        w_mix = w_mix.astype(BF16)
        w_gate = wl[:, offs[11]:offs[12]].astype(BF16)
        proj = _in_projection(x, sc_m, sh_m, norm_mix_pre[l][None], w_mix, tm)

        w2p = _pad_rows(gla_gate_w2[l], LANES, 0)
        o_gla = _gla(proj, w2p, gla_gate_b[l][None], jnp.tile(gla_norm[l], GLA_HEADS)[None],
                     col_qk=0, col_v=1, col_og=2, col_gz=(n_mix // LANES), tb=tb)
        o_diff = _diff_attention(proj, cos, sin, diff_lambda[l],
                                 diff_subln[l][None], col_q=3, col_k=4, col_v=5, layer_idx=l)
        o_conv = _conformer_conv(proj, _pad_rows(conv_w[l], 32, 0), conv_b[l][None],
                                 conv_ln_g[l][None], conv_ln_b[l][None], col_a=6, col_b=7)
        vecs = dict(mu=rwkv_mu[l][None], w0=rwkv_w0[l][None], a0=rwkv_a0[l][None],
                    kk=rwkv_k_k[l][None], ka=rwkv_k_a[l][None], rk=rwkv_r_k[l].reshape(1, W),
                    lng=rwkv_ln_g[l][None], lnb=rwkv_ln_b[l][None])
        o_rwkv = _rwkv7(proj, vecs, _pad_rows(rwkv_w2[l], LANES, 0).astype(BF16),
                        _pad_rows(rwkv_a2[l], LANES, decay_rank).astype(BF16),
                        rwkv_g2[l].astype(BF16), col=2, tb=tb)
        x = _merge(x, sc_m, sh_m, gt_m, norm_mix_pre[l][None], norm_mix_post[l][None], w_gate,
                   (o_gla, o_diff, o_conv, o_rwkv), w_branch[l].astype(BF16), w_out[l].astype(BF16), tm)

        i = l // 2
        if l % 2 == 0:
            x = _dense_ffn(x, sc_f, sh_f, gt_f, norm_ffn_pre[l][None], norm_ffn_post[l][None],
                           ffn_w_gate[i].astype(BF16), ffn_w_up[i].astype(BF16),
                           ffn_w_down[i].astype(BF16), tm)
        else:
            rw = jnp.zeros((D, LANES), F32).at[:, :N_EXPERTS].set(router_w[i])
            rb = jnp.full((1, LANES), -jnp.inf, F32).at[0, :N_EXPERTS].set(router_b[i])
            x = _moe_ffn(x, sc_f, sh_f, gt_f, norm_ffn_pre[l][None], norm_ffn_post[l][None], rw, rb,
                         moe_w_gate[i].astype(BF16), moe_w_up[i].astype(BF16),
                         moe_w_down[i].astype(BF16), tf=moe_w_gate.shape[3])
    return x
```

```python
import functools
import math

import jax
import jax.numpy as jnp
from jax import lax
from jax.experimental import pallas as pl
from jax.experimental.pallas import tpu as pltpu

F32 = jnp.float32
BF16 = jnp.bfloat16
HIGHEST = lax.Precision.HIGHEST

N_BRANCH = 4
GLA_HEADS = 4
GLA_GATE_RANK = 16
GLA_GATE_NORMALIZER = 16.0
GLA_CHUNK = 32
DIFF_HEADS = 4
ROPE_THETA = 10000.0
CONV_WIDTH = 31
RWKV_HEADS = 4
RWKV_CHUNK = 64
RWKV_SUB = 16
N_EXPERTS = 8
RMS_EPS = 1e-6
LN_EPS = 1e-5
RWKV_GN_EPS = 64e-5
LANES = 128
SUBLANES = 8
VMEM_LIMIT = 56 * 1024 * 1024


def _cparams(sem):
    return pltpu.CompilerParams(dimension_semantics=sem, vmem_limit_bytes=VMEM_LIMIT)


def _mm(a, b):
    return jnp.dot(a.astype(BF16), b.astype(BF16), preferred_element_type=F32)


def _mm_nt(a, b):
    return lax.dot_general(a.astype(BF16), b.astype(BF16), (((1,), (1,)), ((), ())),
                           preferred_element_type=F32)


def _mm_tn(a, b):
    return lax.dot_general(a.astype(BF16), b.astype(BF16), (((0,), (0,)), ((), ())),
                           preferred_element_type=F32)


def _bmm(a, b):
    return lax.dot_general(a.astype(BF16), b.astype(BF16), (((2,), (1,)), ((0,), (0,))),
                           preferred_element_type=F32)


def _mm_f32(a, b):
    return jnp.dot(a, b, precision=HIGHEST, preferred_element_type=F32)


def _hi_lo(x):
    hi = x.astype(BF16)
    return jnp.concatenate([hi, (x - hi.astype(F32)).astype(BF16)], axis=1)


def _sigmoid(x):
    return 0.5 * jnp.tanh(0.5 * x) + 0.5


def _silu(x):
    return x * _sigmoid(x)


def _softplus(x):
    return jnp.maximum(x, 0.0) + jnp.log(1.0 + jnp.exp(-jnp.abs(x)))


def _group_matrix(n, group):
    r = lax.broadcasted_iota(jnp.int32, (n, n), 0) // group
    c = lax.broadcasted_iota(jnp.int32, (n, n), 1) // group
    return r == c


def _rms_mod(x, gain, scale, shift):
    y = x * lax.rsqrt(jnp.mean(x * x, axis=-1, keepdims=True) + RMS_EPS)
    return y * gain * (1.0 + scale) + shift


def _mod_kernel(c_ref, w_ref, b_ref, o_ref):
    o_ref[0] = _mm_f32(_silu(c_ref[...]), w_ref[0]) + b_ref[0]


def _modulation(c, ada_w, ada_b):
    L, D, M = ada_w.shape
    B = c.shape[0]
    tn = M // 4
    return pl.pallas_call(
        _mod_kernel,
        out_shape=jax.ShapeDtypeStruct((L, B, M), F32),
        grid=(L, M // tn),
        in_specs=[pl.BlockSpec((B, D), lambda l, j: (0, 0)),
                  pl.BlockSpec((1, D, tn), lambda l, j: (l, 0, j)),
                  pl.BlockSpec((1, 1, tn), lambda l, j: (l, 0, j))],
        out_specs=pl.BlockSpec((1, B, tn), lambda l, j: (l, 0, j)),
        compiler_params=_cparams(("arbitrary", "arbitrary")),
        name="adaln_mod",
    )(c, ada_w, ada_b.reshape(L, 1, M))


def _inproj_kernel(x_ref, sc_ref, sh_ref, g_ref, w_ref, o_ref):
    h = _rms_mod(x_ref[0], g_ref[...], sc_ref[0], sh_ref[0])
    o_ref[0] = _mm(h, w_ref[...])


def _in_projection(x, sc, sh, gain, w, tm):
    B, T, D = x.shape
    n = w.shape[1]
    return pl.pallas_call(
        _inproj_kernel,
        out_shape=jax.ShapeDtypeStruct((B, T, n), F32),
        grid=(B, T // tm),
        in_specs=[pl.BlockSpec((1, tm, D), lambda b, i: (b, i, 0)),
                  pl.BlockSpec((1, 1, D), lambda b, i: (b, 0, 0)),
                  pl.BlockSpec((1, 1, D), lambda b, i: (b, 0, 0)),
                  pl.BlockSpec((1, D), lambda b, i: (0, 0)),
                  pl.BlockSpec((D, n), lambda b, i: (0, 0))],
        out_specs=pl.BlockSpec((1, tm, n), lambda b, i: (b, i, 0)),
        compiler_params=_cparams(("arbitrary", "arbitrary")),
        name="in_proj",
    )(x, sc, sh, gain, w)


def _gla_kernel(qk_ref, v_ref, og_ref, gz_ref, w2_ref, gb_ref, ng_ref, o_ref, s_ref, g_ref, r_ref,
                qd_ref, kd_ref, oacc_ref, *, n_chunk):
    C = GLA_CHUNK
    hk = qk_ref.shape[2] // 2
    hv = v_ref.shape[2]
    dk = hk // GLA_HEADS
    dv = hv // GLA_HEADS

    tb = n_chunk * C

    @pl.when(pl.program_id(1) == 0)
    def _():
        s_ref[...] = jnp.zeros_like(s_ref)

    z = _mm(gz_ref[0], w2_ref[...]) + gb_ref[...]
    gk = (jnp.minimum(z, 0.0) - jnp.log(1.0 + jnp.exp(-jnp.abs(z)))) / GLA_GATE_NORMALIZER
    tri = (lax.broadcasted_iota(jnp.int32, (C, C), 1)
           <= lax.broadcasted_iota(jnp.int32, (C, C), 0)).astype(BF16)
    sums = _bmm(jnp.broadcast_to(tri[None], (n_chunk, C, C)), _hi_lo(gk).reshape(n_chunk, C, 2 * hk))
    G3 = sums[:, :, 0:hk] + sums[:, :, hk:2 * hk]
    G_all = G3.reshape(tb, hk)
    R_all = (jnp.broadcast_to(G3[:, C - 1:C, :], (n_chunk, C, hk)) - G3).reshape(tb, hk)
    g_ref[...] = G_all
    r_ref[...] = R_all
    qd_ref[...] = (qk_ref[0, :, 0:hk] * (dk ** -0.5) * jnp.exp(G_all)).astype(BF16)
    kd_ref[...] = (qk_ref[0, :, hk:2 * hk] * jnp.exp(R_all)).astype(BF16)

    causal = (lax.broadcasted_iota(jnp.int32, (C, C, hk), 1)
              <= lax.broadcasted_iota(jnp.int32, (C, C, hk), 0))
    er = lax.broadcasted_iota(jnp.int32, (hk, hv), 0) // dk
    ec = lax.broadcasted_iota(jnp.int32, (hk, hv), 1) // dv
    expand = (er == ec).astype(BF16)
    sr = lax.broadcasted_iota(jnp.int32, (hv, hk), 0) // dv
    scol = lax.broadcasted_iota(jnp.int32, (hv, hk), 1) // dk
    state_mask = sr == scol
    head_mean = (_group_matrix(hv, dv).astype(F32) / dv).astype(BF16)

    def chunk(ci, carry):
        r0 = pl.multiple_of(ci * C, C)
        rows = pl.ds(r0, C)
        q = qk_ref[0, rows, 0:hk] * (dk ** -0.5)
        k = qk_ref[0, rows, hk:2 * hk]
        v = v_ref[0, rows, :]
        G = g_ref[rows, :]
        pair = (C, C, hk)
        g_diff = jnp.broadcast_to(G[:, None, :], pair) - jnp.broadcast_to(G[None], pair)
        decay = jnp.exp(jnp.where(causal, g_diff, -jnp.inf))
        p = jnp.broadcast_to(q[:, None, :], pair) * jnp.broadcast_to(k[None], pair) * decay
        a_exp = jnp.dot(p.reshape(C * C, hk).astype(BF16), expand, preferred_element_type=F32)
        v_j = jnp.broadcast_to(v[None], (C, C, hv))
        o_intra = jnp.sum(a_exp.reshape(C, C, hv) * v_j, axis=1)
        g_total = G[0:1, :] + r_ref[pl.ds(r0, 8), :][0:1, :]
        s = s_ref[...]
        o_inter = lax.dot_general(qd_ref[rows, :], s.astype(BF16), (((1,), (1,)), ((), ())),
                                  preferred_element_type=F32)
        kv = lax.dot_general(v.astype(BF16), kd_ref[rows, :], (((0,), (0,)), ((), ())),
                             preferred_element_type=F32)
        s_ref[...] = s * jnp.exp(g_total) + jnp.where(state_mask, kv, 0.0)
        oacc_ref[rows, :] = o_intra + o_inter
        return carry

    lax.fori_loop(0, n_chunk, chunk, 0)

    o = oacc_ref[...]
    ms = jnp.dot(_hi_lo(o * o), jnp.concatenate([head_mean, head_mean], axis=0),
                 preferred_element_type=F32)
    o = o * lax.rsqrt(ms + RMS_EPS) * ng_ref[...] * _silu(og_ref[0])
    o_ref[0] = o.astype(o_ref.dtype)


def _gla(proj, w2p, gb, ng, col_qk, col_v, col_og, col_gz, tb):
    B, T, _ = proj.shape
    hv = ng.shape[1]
    hk = gb.shape[1]
    return pl.pallas_call(
        functools.partial(_gla_kernel, n_chunk=tb // GLA_CHUNK),
        out_shape=jax.ShapeDtypeStruct((B, T, hv), BF16),
        grid=(B, T // tb),
        in_specs=[pl.BlockSpec((1, tb, 2 * hk), lambda b, j: (b, j, col_qk)),
                  pl.BlockSpec((1, tb, hv), lambda b, j: (b, j, col_v)),
                  pl.BlockSpec((1, tb, hv), lambda b, j: (b, j, col_og)),
                  pl.BlockSpec((1, tb, LANES), lambda b, j: (b, j, col_gz)),
                  pl.BlockSpec(w2p.shape, lambda b, j: (0, 0)),
                  pl.BlockSpec(gb.shape, lambda b, j: (0, 0)),
                  pl.BlockSpec(ng.shape, lambda b, j: (0, 0))],
        out_specs=pl.BlockSpec((1, tb, hv), lambda b, j: (b, j, 0)),
        scratch_shapes=[pltpu.VMEM((hv, hk), F32), pltpu.VMEM((tb, hk), F32), pltpu.VMEM((tb, hk), F32),
                        pltpu.VMEM((tb, hk), BF16), pltpu.VMEM((tb, hk), BF16), pltpu.VMEM((tb, hv), F32)],
        compiler_params=_cparams(("arbitrary", "arbitrary")),
        name="gla_mixer",
    )(proj, proj, proj, proj, w2p, gb, ng)


def _rope(t, cos, sin_signed):
    d = 32
    half = d // 2
    out = []
    for s in range(t.shape[1] // LANES):
        x = t[:, s * LANES:(s + 1) * LANES]
        lane = lax.broadcasted_iota(jnp.int32, x.shape, 1)
        up = pltpu.roll(x, LANES - half, 1)
        down = pltpu.roll(x, half, 1)
        rot = jnp.where((lane % d) < half, up, down)
        out.append(x * cos + rot * sin_signed)
    return jnp.concatenate(out, axis=1)


def _diff_kernel(q_ref, k_ref, v_ref, cosq_ref, sinq_ref, cosk_ref, sink_ref, lam_ref, g_ref, o_ref,
                 ks, vs, *, tq, lam_init):
    H = DIFF_HEADS
    d = q_ref.shape[2] // (2 * H)
    dv = v_ref.shape[2] // H
    j = pl.program_id(1)

    @pl.when(j == 0)
    def _():
        k = _rope(k_ref[0], cosk_ref[0], sink_ref[0])
        v = v_ref[0]
        for hc in range(2 * H):
            ks[hc] = k[:, hc * d:(hc + 1) * d].astype(BF16)
        ones_col = (lax.broadcasted_iota(jnp.int32, (v.shape[0], dv), 1) == 0).astype(BF16)
        for h in range(H):
            vs[h] = jnp.concatenate([v[:, h * dv:(h + 1) * dv].astype(BF16), ones_col], axis=1)

    q = _rope(q_ref[0], cosq_ref[0], sinq_ref[0]) * (d ** -0.5)
    lp = lam_ref[...]
    lam = (jnp.exp(jnp.sum(lp[0:1] * lp[1:2], axis=-1, keepdims=True))
           - jnp.exp(jnp.sum(lp[2:3] * lp[3:4], axis=-1, keepdims=True)) + lam_init)
    on_or_below_diag = (lax.broadcasted_iota(jnp.int32, (tq, tq), 1)
                        <= lax.broadcasted_iota(jnp.int32, (tq, tq), 0))

    def update(qh, kh, vh, m, acc, masked):
        s = lax.dot_general(qh, kh, (((1,), (1,)), ((), ())), preferred_element_type=F32)
        if masked:
            s = jnp.where(on_or_below_diag, s, -jnp.inf)
        m_new = jnp.maximum(m, jnp.max(s, axis=-1, keepdims=True))
        p = jnp.exp(s - m_new).astype(BF16)
        acc = jnp.exp(m - m_new) * acc + jnp.dot(p, vh, preferred_element_type=F32)
        return m_new, acc

    qs = [q[:, hc * d:(hc + 1) * d].astype(BF16) for hc in range(2 * H)]

    def kv_block(kb, carry, masked):
        rows = pl.ds(pl.multiple_of(kb * tq, tq), tq)
        out = []
        for hc in range(2 * H):
            m, acc = carry[2 * hc], carry[2 * hc + 1]
            out.extend(update(qs[hc], ks[hc, rows, :], vs[hc // 2, rows, :], m, acc, masked))
        return tuple(out)

    m0 = jnp.full((tq, 1), -jnp.inf, F32)
    a0 = jnp.zeros((tq, 2 * dv), F32)
    carry = lax.fori_loop(0, j, lambda kb, c: kv_block(kb, c, False), (m0, a0) * (2 * H))
    carry = kv_block(j, carry, True)

    for h in range(H):
        a1, a2 = carry[4 * h + 1], carry[4 * h + 3]
        comp = [a[:, 0:dv] / a[:, dv:dv + 1] for a in (a1, a2)]
        o = comp[0] - lam * comp[1]
        o = o * lax.rsqrt(jnp.mean(o * o, axis=-1, keepdims=True) + RMS_EPS)
        o = o * g_ref[...] * (1.0 - lam_init)
        o_ref[0, :, h * dv:(h + 1) * dv] = o.astype(o_ref.dtype)


def _diff_attention(proj, cos, sin, lam_p, g, col_q, col_k, col_v, layer_idx):
    B, T, _ = proj.shape
    H = DIFF_HEADS
    dv = g.shape[1]
    d = dv // 2
    w = 2 * H * d
    assert cos.shape[2] == LANES
    tq = min(512, T)
    lam_init = 0.8 - 0.6 * math.exp(-0.3 * layer_idx)
    blk = lambda col: pl.BlockSpec((1, tq, w), lambda b, j: (b, j, col))
    full = lambda col: pl.BlockSpec((1, T, w), lambda b, j: (b, 0, col))
    tab_blk = pl.BlockSpec((1, tq, LANES), lambda b, j: (b, j, 0))
    tab_full = pl.BlockSpec((1, T, LANES), lambda b, j: (b, 0, 0))
    return pl.pallas_call(
        functools.partial(_diff_kernel, tq=tq, lam_init=lam_init),
        out_shape=jax.ShapeDtypeStruct((B, T, H * dv), BF16),
        grid=(B, T // tq),
        in_specs=[blk(col_q), full(col_k), full(col_v), tab_blk, tab_blk, tab_full, tab_full,
                  pl.BlockSpec(lam_p.shape, lambda b, j: (0, 0)),
                  pl.BlockSpec(g.shape, lambda b, j: (0, 0))],
        out_specs=pl.BlockSpec((1, tq, H * dv), lambda b, j: (b, j, 0)),
        scratch_shapes=[pltpu.VMEM((2 * H, T, d), BF16),
                        pltpu.VMEM((H, T, 2 * dv), BF16)],
        compiler_params=_cparams(("arbitrary", "arbitrary")),
        name="diff_attention",
    )(proj, proj, proj, cos, sin, cos, sin, lam_p, g)


def _conv_kernel(a_ref, b_ref, w_ref, cb_ref, lg_ref, lb_ref, o_ref, u_ref, *, rb):
    T = a_ref.shape[1]
    pad = u_ref.shape[0] - T
    u_ref[0:pad, :] = jnp.zeros((pad, u_ref.shape[1]), F32)
    u_ref[pad:pad + T, :] = a_ref[0] * _sigmoid(b_ref[0])
    first = pad - (CONV_WIDTH - 1)

    def block(i, carry):
        r0 = pl.multiple_of(i * rb, rb)
        win = u_ref[pl.ds(r0, rb + pad), :]
        acc = jnp.zeros((rb, u_ref.shape[1]), F32)
        for s in range(SUBLANES):
            taps = [j for j in range(CONV_WIDTH) if (first + j) % SUBLANES == s]
            rolled = win if s == 0 else pltpu.roll(win, rb + pad - s, 0)
            for j in taps:
                a0 = first + j - s
                acc = acc + w_ref[j:j + 1, :] * rolled[a0:a0 + rb, :]
        y = acc + cb_ref[...]
        mu = jnp.mean(y, axis=-1, keepdims=True)
        yc = y - mu
        var = jnp.mean(yc * yc, axis=-1, keepdims=True)
        y = yc * lax.rsqrt(var + LN_EPS) * lg_ref[...] + lb_ref[...]
        o_ref[0, pl.ds(r0, rb), :] = _silu(y).astype(o_ref.dtype)
        return carry

    lax.fori_loop(0, T // rb, block, 0)


def _conformer_conv(proj, w, cb, lg, lb, col_a, col_b):
    B, T, _ = proj.shape
    ch = w.shape[1]
    rb = min(128, T)
    return pl.pallas_call(
        functools.partial(_conv_kernel, rb=rb),
        out_shape=jax.ShapeDtypeStruct((B, T, ch), BF16),
        grid=(B,),
        in_specs=[pl.BlockSpec((1, T, ch), lambda b: (b, 0, col_a)),
                  pl.BlockSpec((1, T, ch), lambda b: (b, 0, col_b)),
                  pl.BlockSpec(w.shape, lambda b: (0, 0)),
                  pl.BlockSpec(cb.shape, lambda b: (0, 0)),
                  pl.BlockSpec(lg.shape, lambda b: (0, 0)),
                  pl.BlockSpec(lb.shape, lambda b: (0, 0))],
        out_specs=pl.BlockSpec((1, T, ch), lambda b: (b, 0, 0)),
        scratch_shapes=[pltpu.VMEM((T + 32, ch), F32)],
        compiler_params=_cparams(("arbitrary",)),
        name="conformer_conv",
    )(proj, proj, w, cb, lg, lb)


def _rwkv_kernel(x_ref, mu_ref, w0_ref, w2_ref, a0_ref, a2_ref, g2_ref, kk_ref, ka_ref, rk_ref,
                 lng_ref, lnb_ref, o_ref, s_ref, prev_ref, gate_ref, bonus_ref, dec_ref, qeff_ref, yloc_ref,
                 mlow_ref, nc_ref, y_ref, *, n_chunk):
    C = RWKV_CHUNK
    H = RWKV_HEADS
    W = o_ref.shape[2]
    N = W // H

    @pl.when(pl.program_id(1) == 0)
    def _():
        s_ref[...] = jnp.zeros_like(s_ref)
        prev_ref[...] = jnp.zeros_like(prev_ref)

    lane = lax.broadcasted_iota(jnp.int32, (1, W), 1)
    head_mask = [(lane // N == h).astype(F32) for h in range(H)]
    block_diag = _group_matrix(W, N)
    head_sum = block_diag.astype(F32)
    ti = lax.broadcasted_iota(jnp.int32, (C, C), 0)
    tj = lax.broadcasted_iota(jnp.int32, (C, C), 1)
    tril_incl = ti >= tj
    tril_strict = ti > tj
    same_sub = (ti // RWKV_SUB) == (tj // RWKV_SUB)
    eye = (ti == tj).astype(F32)
    nc = n_chunk
    tb = nc * C

    def head_total(t, two_term=False):
        ones = head_sum.astype(BF16)
        if two_term:
            return jnp.dot(_hi_lo(t), jnp.concatenate([ones, ones], axis=0), preferred_element_type=F32)
        return jnp.dot(t.astype(BF16), ones, preferred_element_type=F32)

    x = x_ref[0]
    first_row = lax.broadcasted_iota(jnp.int32, (tb, 1), 0) == 0
    prev = jnp.where(first_row, prev_ref[...], pltpu.roll(x, 1, 0))
    prev_ref[...] = x[tb - 1:tb, :]
    xm = x + (prev - x) * mu_ref[...]
    r = xm[:, 0:W]
    k = xm[:, W:2 * W]
    v = xm[:, 2 * W:3 * W]
    zz = xm[:, 3 * W:3 * W + LANES]
    zg = xm[:, 3 * W + LANES:]
    w = -_softplus(-(w0_ref[...] + _mm(jnp.tanh(zz), w2_ref[...]))) - 0.5
    lw = -jnp.exp(w)
    a = _sigmoid(a0_ref[...] + _mm(zz, a2_ref[...]))
    gate_ref[...] = _mm(_sigmoid(zg), g2_ref[...])
    kk = k * kk_ref[...]
    kk = kk / jnp.maximum(jnp.sqrt(head_total(kk * kk, two_term=True)), 1e-12)
    k = k * (1.0 + (a - 1.0) * ka_ref[...])
    b = kk * a
    bonus_ref[...] = head_total(r * k * rk_ref[...]) * v

    c3 = lambda t: t.reshape(nc, C, t.shape[1])
    sums = _bmm(jnp.broadcast_to(tril_incl.astype(BF16)[None], (nc, C, C)), c3(_hi_lo(lw)))
    G3 = sums[:, :, 0:W] + sums[:, :, W:2 * W]
    total3 = jnp.broadcast_to(G3[:, C - 1:C, :], (nc, C, W))
    dec_ref[...] = jnp.exp(total3).reshape(tb, W)
    G = G3.reshape(tb, W)
    inv = jnp.exp(-G)
    to_end = jnp.exp(total3 - G3).reshape(tb, W)
    kap = kk * jnp.exp(G - lw)
    rho = r * jnp.exp(G)
    kap3, rho3, v3 = c3(kap), c3(rho), c3(v)
    bet_kt3 = jnp.concatenate([c3(b * inv), c3(k * inv)], axis=1)
    betc3, kc3 = c3(b * to_end), c3(k * to_end)

    def bmm_nt(p, q):
        return lax.dot_general(p.astype(BF16), q.astype(BF16), (((2,), (2,)), ((0,), (0,))),
                               preferred_element_type=F32)

    def bmm_tn(p, q):
        return lax.dot_general(p.astype(BF16), q.astype(BF16), (((1,), (1,)), ((0,), (0,))),
                               preferred_element_type=F32)

    wi = lax.broadcasted_iota(jnp.int32, (C, 2 * C), 0)
    wj = lax.broadcasted_iota(jnp.int32, (C, 2 * C), 1) % C
    A_bk, B_bk = [], []
    for h in range(H):
        lhs = jnp.concatenate([c3(kap * head_mask[h]), c3(rho * head_mask[h])], axis=1)
        prod = bmm_nt(lhs, bet_kt3)
        A_bk.append(jnp.where(wj < wi, prod[:, 0:C], 0.0))
        B_bk.append(jnp.where(wj <= wi, prod[:, C:2 * C], 0.0))
    A_b = jnp.concatenate([t[:, :, 0:C] for t in A_bk], axis=0)
    B_b = [t[:, :, 0:C] for t in B_bk]
    v3_low = jnp.concatenate([jnp.zeros_like(v3), v3], axis=1)

    Dg = jnp.where(same_sub, A_b, 0.0)
    Lo = A_b - Dg
    D2 = _bmm(Dg, Dg)
    D4 = _bmm(D2, D2)
    D8 = _bmm(D4, D4)
    Dinv = _bmm(_bmm(_bmm(eye - Dg, eye + D2), eye + D4), eye + D8)
    Nn = _bmm(Dinv, Lo)
    N2 = _bmm(Nn, Nn)
    Tm = _bmm(_bmm(eye - Nn, eye + N2), Dinv)
    Tm = [Tm[h * nc:(h + 1) * nc] for h in range(H)]

    def per_head(mats, t):
        reps = t.shape[2] // W
        acc = None
        for h in range(H):
            m = head_mask[h] if reps == 1 else jnp.concatenate([head_mask[h]] * reps, axis=1)
            u = m * _bmm(mats[h], t)
            acc = u if acc is None else acc + u
        return acc

    akv = per_head(A_bk, v3_low)
    tk = per_head(Tm, jnp.concatenate([kap3, akv], axis=2))
    kap_p = tk[:, :, 0:W]
    v_p = tk[:, :, W:2 * W]
    bb = per_head(B_b, jnp.concatenate([kap_p, v_p], axis=2))
    qeff_ref[...] = (rho3 - bb[:, :, 0:W]).astype(BF16)
    yloc_ref[...] = per_head(B_bk, v3_low) - bb[:, :, W:2 * W]
    mlow_ref[...] = jnp.where(block_diag, bmm_tn(kap_p, betc3), 0.0).astype(BF16)
    nc_ref[...] = jnp.where(block_diag, bmm_tn(jnp.concatenate([v3, v_p], axis=1),
                                               jnp.concatenate([kc3, -betc3], axis=1)), 0.0)

    def chunk(ci, carry):
        r0 = pl.multiple_of(ci * C, C)
        s = s_ref[...]
        sb = s.astype(BF16)
        y = lax.dot_general(qeff_ref[ci], sb, (((1,), (1,)), ((), ())), preferred_element_type=F32)
        y_ref[pl.ds(r0, C), :] = y + yloc_ref[ci]
        s_ref[...] = (s * dec_ref[pl.ds(r0, 8), :][0:1, :]
                      - jnp.dot(sb, mlow_ref[ci], preferred_element_type=F32) + nc_ref[ci])
        return carry

    lax.fori_loop(0, nc, chunk, 0)

    y = y_ref[...]
    mean = head_total(y, two_term=True) / N
    yc = y - mean
    var = head_total(yc * yc) / N
    yn = yc * lax.rsqrt(var + RWKV_GN_EPS) * lng_ref[...] + lnb_ref[...]
    o_ref[0] = ((yn + bonus_ref[...]) * gate_ref[...]).astype(o_ref.dtype)


def _rwkv7(proj, vecs, w2p, a2p, g2, col, tb):
    B, T, _ = proj.shape
    W = g2.shape[1]
    cols = vecs["mu"].shape[1]
    names = ("mu", "w0", "w2", "a0", "a2", "g2", "kk", "ka", "rk", "lng", "lnb")
    params = dict(vecs, w2=w2p, a2=a2p, g2=g2)
    const = lambda arr: pl.BlockSpec(arr.shape, lambda b, j: (0, 0))
    n_chunk = tb // RWKV_CHUNK
    return pl.pallas_call(
        functools.partial(_rwkv_kernel, n_chunk=n_chunk),
        out_shape=jax.ShapeDtypeStruct((B, T, W), BF16),
        grid=(B, T // tb),
        in_specs=[pl.BlockSpec((1, tb, cols), lambda b, j: (b, j, col))]
                 + [const(params[n]) for n in names],
        out_specs=pl.BlockSpec((1, tb, W), lambda b, j: (b, j, 0)),
        scratch_shapes=[pltpu.VMEM((W, W), F32), pltpu.VMEM((1, cols), F32),
                        pltpu.VMEM((tb, W), F32), pltpu.VMEM((tb, W), F32), pltpu.VMEM((tb, W), F32),
                        pltpu.VMEM((n_chunk, RWKV_CHUNK, W), BF16), pltpu.VMEM((n_chunk, RWKV_CHUNK, W), F32),
                        pltpu.VMEM((n_chunk, W, W), BF16), pltpu.VMEM((n_chunk, W, W), F32),
                        pltpu.VMEM((tb, W), F32)],
        compiler_params=_cparams(("arbitrary", "arbitrary")),
        name="rwkv7_mixer",
    )(proj, *[params[n] for n in names])


def _merge_kernel(x_ref, sc_ref, sh_ref, gt_ref, gpre_ref, gpost_ref, wg_ref, b0_ref, b1_ref, b2_ref,
                  b3_ref, wb_ref, wo_ref, o_ref):
    x = x_ref[0]
    D = x.shape[1]
    h = _rms_mod(x, gpre_ref[...], sc_ref[0], sh_ref[0]).astype(BF16)
    merged = None
    for g, br in enumerate((b0_ref, b1_ref, b2_ref, b3_ref)):
        gate = _sigmoid(jnp.dot(h, wg_ref[:, g * D:(g + 1) * D], preferred_element_type=F32))
        t = gate * jnp.dot(br[0], wb_ref[g], preferred_element_type=F32)
        merged = t if merged is None else merged + t
    y = _mm(merged, wo_ref[...])
    y = y * lax.rsqrt(jnp.mean(y * y, axis=-1, keepdims=True) + RMS_EPS) * gpost_ref[...]
    o_ref[0] = x + gt_ref[0] * y


def _merge(x, sc, sh, gt, gpre, gpost, w_gate, branches, w_branch, w_out, tm):
    B, T, D = x.shape
    bw = branches[0].shape[2]
    tok = lambda n: pl.BlockSpec((1, tm, n), lambda b, i: (b, i, 0))
    vec = pl.BlockSpec((1, 1, D), lambda b, i: (b, 0, 0))
    const2 = lambda arr: pl.BlockSpec(arr.shape, lambda b, i: (0, 0))
    return pl.pallas_call(
        _merge_kernel,
        out_shape=jax.ShapeDtypeStruct((B, T, D), F32),
        grid=(B, T // tm),
        in_specs=[tok(D), vec, vec, vec, const2(gpre), const2(gpost), const2(w_gate),
                  tok(bw), tok(bw), tok(bw), tok(bw),
                  pl.BlockSpec(w_branch.shape, lambda b, i: (0, 0, 0)), const2(w_out)],
        out_specs=tok(D),
        compiler_params=_cparams(("arbitrary", "arbitrary")),
        name="merge_out_proj",
    )(x, sc, sh, gt, gpre, gpost, w_gate, *branches, w_branch, w_out)


def _ffn_kernel(x_ref, sc_ref, sh_ref, gt_ref, gpre_ref, gpost_ref, wg_ref, wu_ref, wd_ref, o_ref, *, tf):
    x = x_ref[0]
    h = _rms_mod(x, gpre_ref[...], sc_ref[0], sh_ref[0]).astype(BF16)
    F = wg_ref.shape[1]
    acc = None
    for f0 in range(0, F, tf):
        f1 = min(f0 + tf, F)
        gate = jnp.dot(h, wg_ref[:, f0:f1], preferred_element_type=F32)
        up = jnp.dot(h, wu_ref[:, f0:f1], preferred_element_type=F32)
        t = jnp.dot((_silu(gate) * up).astype(BF16), wd_ref[f0:f1, :], preferred_element_type=F32)
        acc = t if acc is None else acc + t
    y = acc * lax.rsqrt(jnp.mean(acc * acc, axis=-1, keepdims=True) + RMS_EPS) * gpost_ref[...]
    o_ref[0] = x + gt_ref[0] * y


def _dense_ffn(x, sc, sh, gt, gpre, gpost, wg, wu, wd, tm):
    B, T, D = x.shape
    tok = pl.BlockSpec((1, tm, D), lambda b, i: (b, i, 0))
    vec = pl.BlockSpec((1, 1, D), lambda b, i: (b, 0, 0))
    const2 = lambda arr: pl.BlockSpec(arr.shape, lambda b, i: (0, 0))
    return pl.pallas_call(
        functools.partial(_ffn_kernel, tf=512),
        out_shape=jax.ShapeDtypeStruct((B, T, D), F32),
        grid=(B, T // tm),
        in_specs=[tok, vec, vec, vec, const2(gpre), const2(gpost), const2(wg), const2(wu), const2(wd)],
        out_specs=tok,
        compiler_params=_cparams(("arbitrary", "arbitrary")),
        name="dense_swiglu",
    )(x, sc, sh, gt, gpre, gpost, wg, wu, wd)


MOE_TOKEN_TILE = 256
MOE_ROW_TILE = 512
SEG_ALIGN = 16
SEG_PIECES = (256, 128, 64, 32, 16)
MOE_SMALL_SEG = 128


def _route_kernel(x_ref, sc_ref, sh_ref, gpre_ref, rw_ref, rb_ref, h_ref, mi_ref, mp_ref, cnt_ref):
    tm = x_ref.shape[1]
    h = _rms_mod(x_ref[0], gpre_ref[...], sc_ref[0], sh_ref[0])
    h_ref[...] = h.astype(BF16)
    h_hi = h.astype(BF16)
    h_lo = (h - h_hi.astype(F32)).astype(BF16)
    w = rw_ref[...]
    w_hi = w.astype(BF16)
    w_lo = (w - w_hi.astype(F32)).astype(BF16)
    logits = jnp.dot(jnp.concatenate([h_hi, h_lo, h_hi], axis=1), jnp.concatenate([w_hi, w_hi, w_lo], axis=0),
                     preferred_element_type=F32) + rb_ref[...]
    lane = lax.broadcasted_iota(jnp.int32, logits.shape, 1)
    v1 = jnp.max(logits, axis=-1, keepdims=True)
    i1 = jnp.min(jnp.where(logits == v1, lane, LANES), axis=-1, keepdims=True)
    rest = jnp.where(lane == i1, -jnp.inf, logits)
    v2 = jnp.max(rest, axis=-1, keepdims=True)
    i2 = jnp.min(jnp.where(rest == v2, lane, LANES), axis=-1, keepdims=True)
    e2 = jnp.exp(v2 - v1)
    p1 = 1.0 / (1.0 + e2)
    p2 = e2 / (1.0 + e2)
    oh1 = (lane == i1).astype(F32)
    oh2 = (lane == i2).astype(F32)
    both = oh1 + oh2
    earlier = (lax.broadcasted_iota(jnp.int32, (tm, tm), 1)
               < lax.broadcasted_iota(jnp.int32, (tm, tm), 0)).astype(BF16)
    before = jnp.dot(earlier, both.astype(BF16), preferred_element_type=F32)
    r1 = jnp.sum(oh1 * before, axis=-1, keepdims=True).astype(jnp.int32)
    r2 = jnp.sum(oh2 * before, axis=-1, keepdims=True).astype(jnp.int32)
    col = lax.broadcasted_iota(jnp.int32, mi_ref.shape, 1)
    mi_ref[...] = jnp.where(col == 0, i1, jnp.where(col == 1, i2, jnp.where(col == 2, r1,
                                                                           jnp.where(col == 3, r2, 0))))
    mp_ref[...] = jnp.where(col == 0, p1, jnp.where(col == 1, p2, 0.0))
    cnt_ref[0] = jnp.sum(both, axis=0, keepdims=True).astype(jnp.int32)


def _segment_pieces(n_rows):
    out = []
    for s in SEG_PIECES:
        if s == MOE_TOKEN_TILE:
            out.append((n_rows == s, 0, s))
        else:
            out.append(((n_rows & s) != 0, pl.multiple_of((n_rows // (2 * s)) * (2 * s), SEG_ALIGN), s))
    return out


def _all_segments_small(cnt_ref, tile):
    most = cnt_ref[tile * N_EXPERTS]
    for e in range(1, N_EXPERTS):
        most = jnp.maximum(most, cnt_ref[tile * N_EXPERTS + e])
    return most <= MOE_SMALL_SEG


def _dispatch_kernel(seg_ref, cnt_ref, h_ref, mit_ref, init_ref, xs_ref, buf_ref, sem):
    del init_ref
    tm = h_ref.shape[0]
    i = pl.program_id(0)
    e1, e2 = mit_ref[0:1, :], mit_ref[1:2, :]
    r1, r2 = mit_ref[2:3, :], mit_ref[3:4, :]
    slot = i % 2

    def compact(cap):
        row = lax.broadcasted_iota(jnp.int32, (cap, tm), 0)
        select = jnp.concatenate(
            [jnp.logical_or(jnp.logical_and(e1 == e, r1 == row), jnp.logical_and(e2 == e, r2 == row))
             for e in range(N_EXPERTS)], axis=0).astype(BF16)
        rows = jnp.dot(select, h_ref[...], preferred_element_type=F32).astype(BF16)
        for e in range(N_EXPERTS):
            buf_ref[slot, e * tm:e * tm + cap, :] = rows[e * cap:(e + 1) * cap]

    small = _all_segments_small(cnt_ref, i)
    pl.when(small)(functools.partial(compact, MOE_SMALL_SEG))
    pl.when(jnp.logical_not(small))(functools.partial(compact, tm))

    def segment_copies(tile, buf, e):
        n = cnt_ref[tile * N_EXPERTS + e]
        n_rows = ((n + SEG_ALIGN - 1) // SEG_ALIGN) * SEG_ALIGN
        dst = pl.multiple_of(seg_ref[tile * N_EXPERTS + e], SEG_ALIGN)
        return [(cond, pltpu.make_async_copy(buf_ref.at[buf, pl.ds(e * tm + off, s), :],
                                             xs_ref.at[pl.ds(dst + off, s), :], sem))
                for cond, off, s in _segment_pieces(n_rows)]

    def for_all_segments(tile, buf, action):
        for e in range(N_EXPERTS):
            for cond, cp in segment_copies(tile, buf, e):
                pl.when(cond)(getattr(cp, action))

    @pl.when(i > 0)
    def _():
        for_all_segments(i - 1, 1 - slot, "wait")

    for_all_segments(i, slot, "start")

    @pl.when(i == pl.num_programs(0) - 1)
    def _():
        for_all_segments(i, slot, "wait")


def _expert_kernel(te_ref, nv_ref, xs_ref, wg_ref, wu_ref, wd_ref, ys_ref, acc_ref):
    del te_ref
    r = pl.program_id(0)
    f = pl.program_id(1)
    valid = r < nv_ref[0]

    @pl.when(jnp.logical_and(valid, f == 0))
    def _():
        acc_ref[...] = jnp.zeros_like(acc_ref)

    @pl.when(valid)
    def _():
        x = xs_ref[...]
        tf = wg_ref.shape[2]
        acc = acc_ref[...]
        for f0 in range(0, tf, 512):
            f1 = min(f0 + 512, tf)
            gate = jnp.dot(x, wg_ref[0, :, f0:f1], preferred_element_type=F32)
            up = jnp.dot(x, wu_ref[0, :, f0:f1], preferred_element_type=F32)
            acc = acc + jnp.dot((_silu(gate) * up).astype(BF16), wd_ref[0, f0:f1, :],
                                preferred_element_type=F32)
        acc_ref[...] = acc

    @pl.when(f == pl.num_programs(1) - 1)
    def _():
        @pl.when(valid)
        def _():
            ys_ref[...] = acc_ref[...].astype(ys_ref.dtype)

        @pl.when(jnp.logical_not(valid))
        def _():
            ys_ref[...] = jnp.zeros_like(ys_ref)


def _combine_kernel(seg_ref, cnt_ref, ys_ref, mi_ref, mp_ref, x_ref, gt_ref, gpost_ref, o_ref, win_ref,
                    pair_ref, sem):
    tm = x_ref.shape[1]
    i = pl.program_id(0) * pl.num_programs(1) + pl.program_id(1)
    n_tiles = pl.num_programs(0) * pl.num_programs(1)
    slot = i % 2

    def window_copy(tile, buf, e, rows):
        src = pl.multiple_of(seg_ref[tile * N_EXPERTS + e], SEG_ALIGN)
        return pltpu.make_async_copy(ys_ref.at[pl.ds(src, rows), :], win_ref.at[buf, pl.ds(e * rows, rows), :],
                                     sem.at[buf, e])

    def for_all_windows(tile, buf, action):
        small = _all_segments_small(cnt_ref, tile)
        for rows, cond in ((MOE_SMALL_SEG, small), (tm, jnp.logical_not(small))):
            @pl.when(cond)
            def _():
                for e in range(N_EXPERTS):
                    getattr(window_copy(tile, buf, e, rows), action)()

    @pl.when(i == 0)
    def _():
        for_all_windows(i, slot, "start")

    @pl.when(i + 1 < n_tiles)
    def _():
        for_all_windows(i + 1, 1 - slot, "start")

    for_all_windows(i, slot, "wait")
    e1, e2 = mi_ref[:, 0:1], mi_ref[:, 1:2]
    r1, r2 = mi_ref[:, 2:3], mi_ref[:, 3:4]

    def expand_rows(cap):
        col = lax.broadcasted_iota(jnp.int32, (tm, cap), 1)
        expand = jnp.concatenate(
            [jnp.concatenate([jnp.logical_and(e1 == e, r1 == col), jnp.logical_and(e2 == e, r2 == col)], axis=0)
             for e in range(N_EXPERTS)], axis=1).astype(BF16)
        pair_ref[...] = jnp.dot(expand, win_ref[slot, 0:N_EXPERTS * cap, :], preferred_element_type=F32)

    small = _all_segments_small(cnt_ref, i)
    pl.when(small)(functools.partial(expand_rows, MOE_SMALL_SEG))
    pl.when(jnp.logical_not(small))(functools.partial(expand_rows, tm))
    y = mp_ref[:, 0:1] * pair_ref[0:tm, :] + mp_ref[:, 1:2] * pair_ref[tm:2 * tm, :]
    y = y * lax.rsqrt(jnp.mean(y * y, axis=-1, keepdims=True) + RMS_EPS) * gpost_ref[...]
    o_ref[0] = x_ref[0] + gt_ref[0] * y


def _moe_ffn(x, sc, sh, gt, gpre, gpost, rw, rb, wg, wu, wd, tf):
    B, T, D = x.shape
    E, _, F = wg.shape
    tm = min(MOE_TOKEN_TILE, T)
    assert tm == MOE_TOKEN_TILE and E == N_EXPERTS
    nT = T // tm
    n_tok_tiles = B * nT
    N = B * T
    max_rows = 2 * N + n_tok_tiles * E * (SEG_ALIGN - 1) + E * (MOE_ROW_TILE - SEG_ALIGN)
    n_row_tiles = -(-max_rows // MOE_ROW_TILE) + 1
    P = n_row_tiles * MOE_ROW_TILE

    vec = pl.BlockSpec((1, 1, D), lambda b, j: (b, 0, 0))
    const2 = lambda arr: pl.BlockSpec(arr.shape, lambda b, j: (0, 0))
    flat = lambda n: pl.BlockSpec((tm, n), lambda b, j: (b * nT + j, 0))
    h, mi, mp, cnt = pl.pallas_call(
        _route_kernel,
        out_shape=(jax.ShapeDtypeStruct((N, D), BF16), jax.ShapeDtypeStruct((N, 8), jnp.int32),
                   jax.ShapeDtypeStruct((N, 8), F32), jax.ShapeDtypeStruct((n_tok_tiles, 1, LANES), jnp.int32)),
        grid=(B, nT),
        in_specs=[pl.BlockSpec((1, tm, D), lambda b, j: (b, j, 0)), vec, vec, const2(gpre), const2(rw),
                  const2(rb)],
        out_specs=(flat(D), flat(8), flat(8), pl.BlockSpec((1, 1, LANES), lambda b, j: (b * nT + j, 0, 0))),
        compiler_params=_cparams(("arbitrary", "arbitrary")),
        name="moe_route",
    )(x, sc, sh, gpre, rw, rb)

    counts = cnt[:, 0, :E]
    seg_len = (counts + SEG_ALIGN - 1) // SEG_ALIGN * SEG_ALIGN
    group_len = (jnp.sum(seg_len, axis=0) + MOE_ROW_TILE - 1) // MOE_ROW_TILE * MOE_ROW_TILE
    group_end = jnp.cumsum(group_len)
    seg_start = (group_end - group_len)[None, :] + jnp.cumsum(seg_len, axis=0) - seg_len
    seg_start = seg_start.reshape(-1).astype(jnp.int32)
    counts = counts.reshape(-1)
    n_valid = (group_end[-1:] // MOE_ROW_TILE).astype(jnp.int32)
    tile_first_row = jnp.arange(n_row_tiles, dtype=jnp.int32) * MOE_ROW_TILE
    tile_expert = jnp.minimum(jnp.sum(tile_first_row[:, None] >= group_end[None, :], axis=1), E - 1)
    tile_expert = tile_expert.astype(jnp.int32)

    xs = pl.pallas_call(
        _dispatch_kernel,
        out_shape=jax.ShapeDtypeStruct((P, D), BF16),
        grid_spec=pltpu.PrefetchScalarGridSpec(
            num_scalar_prefetch=2,
            grid=(n_tok_tiles,),
            in_specs=[pl.BlockSpec((tm, D), lambda i, seg, n: (i, 0)),
                      pl.BlockSpec((8, tm), lambda i, seg, n: (0, i)),
                      pl.BlockSpec(memory_space=pl.ANY)],
            out_specs=pl.BlockSpec(memory_space=pl.ANY),
            scratch_shapes=[pltpu.VMEM((2, E * tm, D), BF16), pltpu.SemaphoreType.DMA(())]),
        input_output_aliases={4: 0},
        compiler_params=_cparams(("arbitrary",)),
        name="moe_dispatch",
    )(seg_start, counts, h, mi.T, jnp.zeros((P, D), BF16))

    nf = F // tf
    live = lambda r, f, nv: jnp.where(r < nv[0], f, nf - 1)
    ys = pl.pallas_call(
        _expert_kernel,
        out_shape=jax.ShapeDtypeStruct((P, D), BF16),
        grid_spec=pltpu.PrefetchScalarGridSpec(
            num_scalar_prefetch=2,
            grid=(n_row_tiles, nf),
            in_specs=[pl.BlockSpec((MOE_ROW_TILE, D), lambda r, f, te, nv: (r, 0)),
                      pl.BlockSpec((1, D, tf), lambda r, f, te, nv: (te[r], 0, live(r, f, nv))),
                      pl.BlockSpec((1, D, tf), lambda r, f, te, nv: (te[r], 0, live(r, f, nv))),
                      pl.BlockSpec((1, tf, D), lambda r, f, te, nv: (te[r], live(r, f, nv), 0))],
            out_specs=pl.BlockSpec((MOE_ROW_TILE, D), lambda r, f, te, nv: (r, 0)),
            scratch_shapes=[pltpu.VMEM((MOE_ROW_TILE, D), F32)]),
        compiler_params=_cparams(("arbitrary", "arbitrary")),
        name="moe_experts",
    )(tile_expert, n_valid, xs, wg, wu, wd)

    return pl.pallas_call(
        _combine_kernel,
        out_shape=jax.ShapeDtypeStruct((B, T, D), F32),
        grid_spec=pltpu.PrefetchScalarGridSpec(
            num_scalar_prefetch=2,
            grid=(B, nT),
            in_specs=[pl.BlockSpec(memory_space=pl.ANY),
                      pl.BlockSpec((tm, 8), lambda b, j, seg, n: (b * nT + j, 0)),
                      pl.BlockSpec((tm, 8), lambda b, j, seg, n: (b * nT + j, 0)),
                      pl.BlockSpec((1, tm, D), lambda b, j, seg, n: (b, j, 0)),
                      pl.BlockSpec((1, 1, D), lambda b, j, seg, n: (b, 0, 0)),
                      pl.BlockSpec(gpost.shape, lambda b, j, seg, n: (0, 0))],
            out_specs=pl.BlockSpec((1, tm, D), lambda b, j, seg, n: (b, j, 0)),
            scratch_shapes=[pltpu.VMEM((2, E * tm, D), BF16), pltpu.VMEM((2 * tm, D), F32),
                            pltpu.SemaphoreType.DMA((2, E))]),
        compiler_params=_cparams(("arbitrary", "arbitrary")),
        name="moe_combine",
    )(seg_start, counts, ys, mi, mp, x, gt, gpost)


def _rope_tables(positions, groups):
    d = 32
    inv = 1.0 / (ROPE_THETA ** (jnp.arange(0, d, 2, dtype=F32) / d))
    ang = positions.astype(F32)[..., None] * inv
    cos, sin = jnp.cos(ang), jnp.sin(ang)
    cos = jnp.tile(jnp.concatenate([cos, cos], axis=-1), (1, 1, groups))
    sin = jnp.tile(jnp.concatenate([-sin, sin], axis=-1), (1, 1, groups))
    return cos, sin


def _pad_rows(w, rows, offset):
    out = jnp.zeros((rows, w.shape[1]), w.dtype)
    return out.at[offset:offset + w.shape[0]].set(w)


def kernel(x, c, positions, ada_w, ada_b, norm_mix_pre, norm_mix_post, norm_ffn_pre, norm_ffn_post, w_in, gla_gate_w2, gla_gate_b, gla_norm, diff_lambda, diff_subln, conv_w, conv_b, conv_ln_g, conv_ln_b, rwkv_mu, rwkv_w0, rwkv_w2, rwkv_a0, rwkv_a2, rwkv_g2, rwkv_k_k, rwkv_k_a, rwkv_r_k, rwkv_ln_g, rwkv_ln_b, w_branch, w_out, ffn_w_gate, ffn_w_up, ffn_w_down, router_w, router_b, moe_w_gate, moe_w_up, moe_w_down):
    B, T, D = x.shape
    L = ada_w.shape[0]
    W = D // N_BRANCH
    hk = gla_gate_b.shape[1]
    decay_rank = rwkv_w2.shape[1]
    a_rank = rwkv_a2.shape[1]
    gate_rank = rwkv_g2.shape[1]
    assert decay_rank + a_rank == LANES and 2 * hk == W and gate_rank == LANES
    n_mix = 3 * W + 3 * W + 2 * W + (3 * W + decay_rank + a_rank + gate_rank)
    sizes = (hk, hk, W, W, GLA_GATE_RANK, W, W, W, W, W, 3 * W + LANES + gate_rank, N_BRANCH * D)
    offs = [0]
    for s in sizes:
        offs.append(offs[-1] + s)
    assert offs[-1] == w_in.shape[2]
    tm = min(512, T)
    tb = min(512, T)

    mod = _modulation(c, ada_w, ada_b)
    cos, sin = _rope_tables(positions, LANES // 32)

    for l in range(L):
        m = mod[l].reshape(B, 1, 6 * D)
        sh_m, sc_m, gt_m, sh_f, sc_f, gt_f = [m[:, :, i * D:(i + 1) * D] for i in range(6)]

        wl = w_in[l]
        gz_cols = jnp.zeros((D, LANES), F32).at[:, :GLA_GATE_RANK].set(wl[:, offs[4]:offs[5]])
        w_mix = jnp.concatenate([wl[:, offs[0]:offs[4]], wl[:, offs[5]:offs[11]], gz_cols], axis=1)
        w_mix = w_mix.astype(BF16)
        w_gate = wl[:, offs[11]:offs[12]].astype(BF16)
        proj = _in_projection(x, sc_m, sh_m, norm_mix_pre[l][None], w_mix, tm)

        w2p = _pad_rows(gla_gate_w2[l], LANES, 0)
        o_gla = _gla(proj, w2p, gla_gate_b[l][None], jnp.tile(gla_norm[l], GLA_HEADS)[None],
                     col_qk=0, col_v=1, col_og=2, col_gz=(n_mix // LANES), tb=tb)
        o_diff = _diff_attention(proj, cos, sin, diff_lambda[l],
                                 diff_subln[l][None], col_q=3, col_k=4, col_v=5, layer_idx=l)
        o_conv = _conformer_conv(proj, _pad_rows(conv_w[l], 32, 0), conv_b[l][None],
                                 conv_ln_g[l][None], conv_ln_b[l][None], col_a=6, col_b=7)
        vecs = dict(mu=rwkv_mu[l][None], w0=rwkv_w0[l][None], a0=rwkv_a0[l][None],
                    kk=rwkv_k_k[l][None], ka=rwkv_k_a[l][None], rk=rwkv_r_k[l].reshape(1, W),
                    lng=rwkv_ln_g[l][None], lnb=rwkv_ln_b[l][None])
        o_rwkv = _rwkv7(proj, vecs, _pad_rows(rwkv_w2[l], LANES, 0).astype(BF16),
                        _pad_rows(rwkv_a2[l], LANES, decay_rank).astype(BF16),
                        rwkv_g2[l].astype(BF16), col=2, tb=tb)
        x = _merge(x, sc_m, sh_m, gt_m, norm_mix_pre[l][None], norm_mix_post[l][None], w_gate,
                   (o_gla, o_diff, o_conv, o_rwkv), w_branch[l].astype(BF16), w_out[l].astype(BF16), tm)

        i = l // 2
        if l % 2 == 0:
            x = _dense_ffn(x, sc_f, sh_f, gt_f, norm_ffn_pre[l][None], norm_ffn_post[l][None],
                           ffn_w_gate[i].astype(BF16), ffn_w_up[i].astype(BF16),
                           ffn_w_down[i].astype(BF16), tm)
        else:
            rw = jnp.zeros((D, LANES), F32).at[:, :N_EXPERTS].set(router_w[i])
            rb = jnp.full((1, LANES), -jnp.inf, F32).at[0, :N_EXPERTS].set(router_b[i])
            x = _moe_ffn(x, sc_f, sh_f, gt_f, norm_ffn_pre[l][None], norm_ffn_post[l][None], rw, rb,
                         moe_w_gate[i].astype(BF16), moe_w_up[i].astype(BF16),
                         moe_w_down[i].astype(BF16), tf=moe_w_gate.shape[3])
    return x
```

```python
import functools
import math

import jax
import jax.numpy as jnp
from jax import lax
from jax.experimental import pallas as pl
from jax.experimental.pallas import tpu as pltpu

F32 = jnp.float32
BF16 = jnp.bfloat16
HIGHEST = lax.Precision.HIGHEST

N_BRANCH = 4
GLA_HEADS = 4
GLA_GATE_RANK = 16
GLA_GATE_NORMALIZER = 16.0
GLA_CHUNK = 32
DIFF_HEADS = 4
ROPE_THETA = 10000.0
CONV_WIDTH = 31
RWKV_HEADS = 4
RWKV_CHUNK = 64
RWKV_SUB = 16
N_EXPERTS = 8
RMS_EPS = 1e-6
LN_EPS = 1e-5
RWKV_GN_EPS = 64e-5
LANES = 128
SUBLANES = 8
VMEM_LIMIT = 56 * 1024 * 1024

TOKEN_TILE = 512
SEQ_BLOCK = 512
ATTN_BLOCK = 512
CONV_ROWS = 128
FFN_CHUNK = 512


def _cparams(sem):
    return pltpu.CompilerParams(dimension_semantics=sem, vmem_limit_bytes=VMEM_LIMIT)


def _mm(a, b):
    return jnp.dot(a.astype(BF16), b.astype(BF16), preferred_element_type=F32)


def _mm_nt(a, b):
    return lax.dot_general(a.astype(BF16), b.astype(BF16), (((1,), (1,)), ((), ())),
                           preferred_element_type=F32)


def _mm_tn(a, b):
    return lax.dot_general(a.astype(BF16), b.astype(BF16), (((0,), (0,)), ((), ())),
                           preferred_element_type=F32)


def _bmm(a, b):
    return lax.dot_general(a.astype(BF16), b.astype(BF16), (((2,), (1,)), ((0,), (0,))),
                           preferred_element_type=F32)


def _mm_f32(a, b):
    return jnp.dot(a, b, precision=HIGHEST, preferred_element_type=F32)


def _hi_lo(x):
    hi = x.astype(BF16)
    return jnp.concatenate([hi, (x - hi.astype(F32)).astype(BF16)], axis=1)


def _sigmoid(x):
    return 0.5 * jnp.tanh(0.5 * x) + 0.5


def _silu(x):
    return x * _sigmoid(x)


def _softplus(x):
    return jnp.maximum(x, 0.0) + jnp.log(1.0 + jnp.exp(-jnp.abs(x)))


def _group_matrix(n, group):
    r = lax.broadcasted_iota(jnp.int32, (n, n), 0) // group
    c = lax.broadcasted_iota(jnp.int32, (n, n), 1) // group
    return r == c


def _rms_mod(x, gain, scale, shift):
    y = x * lax.rsqrt(jnp.mean(x * x, axis=-1, keepdims=True) + RMS_EPS)
    return y * gain * (1.0 + scale) + shift


def _mod_kernel(c_ref, w_ref, b_ref, o_ref):
    o_ref[0] = _mm_f32(_silu(c_ref[...]), w_ref[0]) + b_ref[0]


def _modulation(c, ada_w, ada_b):
    L, D, M = ada_w.shape
    B = c.shape[0]
    tn = M // 4
    return pl.pallas_call(
        _mod_kernel,
        out_shape=jax.ShapeDtypeStruct((L, B, M), F32),
        grid=(L, M // tn),
        in_specs=[pl.BlockSpec((B, D), lambda l, j: (0, 0)),
                  pl.BlockSpec((1, D, tn), lambda l, j: (l, 0, j)),
                  pl.BlockSpec((1, 1, tn), lambda l, j: (l, 0, j))],
        out_specs=pl.BlockSpec((1, B, tn), lambda l, j: (l, 0, j)),
        compiler_params=_cparams(("arbitrary", "arbitrary")),
        name="adaln_mod",
    )(c, ada_w, ada_b.reshape(L, 1, M))


def _inproj_kernel(x_ref, sc_ref, sh_ref, g_ref, w_ref, o_ref):
    h = _rms_mod(x_ref[0], g_ref[...], sc_ref[0], sh_ref[0])
    o_ref[0] = _mm(h, w_ref[...])


def _in_projection(x, sc, sh, gain, w, tm):
    B, T, D = x.shape
    n = w.shape[1]
    return pl.pallas_call(
        _inproj_kernel,
        out_shape=jax.ShapeDtypeStruct((B, T, n), F32),
        grid=(B, T // tm),
        in_specs=[pl.BlockSpec((1, tm, D), lambda b, i: (b, i, 0)),
                  pl.BlockSpec((1, 1, D), lambda b, i: (b, 0, 0)),
                  pl.BlockSpec((1, 1, D), lambda b, i: (b, 0, 0)),
                  pl.BlockSpec((1, D), lambda b, i: (0, 0)),
                  pl.BlockSpec((D, n), lambda b, i: (0, 0))],
        out_specs=pl.BlockSpec((1, tm, n), lambda b, i: (b, i, 0)),
        compiler_params=_cparams(("arbitrary", "arbitrary")),
        name="in_proj",
    )(x, sc, sh, gain, w)


def _gla_kernel(qk_ref, v_ref, og_ref, gz_ref, w2_ref, gb_ref, ng_ref, o_ref, s_ref, g_ref, r_ref,
                qd_ref, kd_ref, oacc_ref, *, n_chunk):
    C = GLA_CHUNK
    hk = qk_ref.shape[2] // 2
    hv = v_ref.shape[2]
    dk = hk // GLA_HEADS
    dv = hv // GLA_HEADS

    tb = n_chunk * C

    @pl.when(pl.program_id(1) == 0)
    def _():
        s_ref[...] = jnp.zeros_like(s_ref)

    z = _mm(gz_ref[0], w2_ref[...]) + gb_ref[...]
    gk = (jnp.minimum(z, 0.0) - jnp.log(1.0 + jnp.exp(-jnp.abs(z)))) / GLA_GATE_NORMALIZER
    tri = (lax.broadcasted_iota(jnp.int32, (C, C), 1)
           <= lax.broadcasted_iota(jnp.int32, (C, C), 0)).astype(BF16)
    sums = _bmm(jnp.broadcast_to(tri[None], (n_chunk, C, C)), _hi_lo(gk).reshape(n_chunk, C, 2 * hk))
    G3 = sums[:, :, 0:hk] + sums[:, :, hk:2 * hk]
    G_all = G3.reshape(tb, hk)
    R_all = (jnp.broadcast_to(G3[:, C - 1:C, :], (n_chunk, C, hk)) - G3).reshape(tb, hk)
    g_ref[...] = G_all
    r_ref[...] = R_all
    qd_ref[...] = (qk_ref[0, :, 0:hk] * (dk ** -0.5) * jnp.exp(G_all)).astype(BF16)
    kd_ref[...] = (qk_ref[0, :, hk:2 * hk] * jnp.exp(R_all)).astype(BF16)

    causal = (lax.broadcasted_iota(jnp.int32, (C, C, hk), 1)
              <= lax.broadcasted_iota(jnp.int32, (C, C, hk), 0))
    er = lax.broadcasted_iota(jnp.int32, (hk, hv), 0) // dk
    ec = lax.broadcasted_iota(jnp.int32, (hk, hv), 1) // dv
    expand = (er == ec).astype(BF16)
    sr = lax.broadcasted_iota(jnp.int32, (hv, hk), 0) // dv
    scol = lax.broadcasted_iota(jnp.int32, (hv, hk), 1) // dk
    state_mask = sr == scol
    head_mean = (_group_matrix(hv, dv).astype(F32) / dv).astype(BF16)

    def chunk(ci, carry):
        r0 = pl.multiple_of(ci * C, C)
        rows = pl.ds(r0, C)
        q = qk_ref[0, rows, 0:hk] * (dk ** -0.5)
        k = qk_ref[0, rows, hk:2 * hk]
        v = v_ref[0, rows, :]
        G = g_ref[rows, :]
        pair = (C, C, hk)
        g_diff = jnp.broadcast_to(G[:, None, :], pair) - jnp.broadcast_to(G[None], pair)
        decay = jnp.exp(jnp.where(causal, g_diff, -jnp.inf))
        p = jnp.broadcast_to(q[:, None, :], pair) * jnp.broadcast_to(k[None], pair) * decay
        a_exp = jnp.dot(p.reshape(C * C, hk).astype(BF16), expand, preferred_element_type=F32)
        v_j = jnp.broadcast_to(v[None], (C, C, hv))
        o_intra = jnp.sum(a_exp.reshape(C, C, hv) * v_j, axis=1)
        g_total = G[0:1, :] + r_ref[pl.ds(r0, 8), :][0:1, :]
        s = s_ref[...]
        o_inter = lax.dot_general(qd_ref[rows, :], s.astype(BF16), (((1,), (1,)), ((), ())),
                                  preferred_element_type=F32)
        kv = lax.dot_general(v.astype(BF16), kd_ref[rows, :], (((0,), (0,)), ((), ())),
                             preferred_element_type=F32)
        s_ref[...] = s * jnp.exp(g_total) + jnp.where(state_mask, kv, 0.0)
        oacc_ref[rows, :] = o_intra + o_inter
        return carry

    lax.fori_loop(0, n_chunk, chunk, 0)

    o = oacc_ref[...]
    ms = jnp.dot(_hi_lo(o * o), jnp.concatenate([head_mean, head_mean], axis=0),
                 preferred_element_type=F32)
    o = o * lax.rsqrt(ms + RMS_EPS) * ng_ref[...] * _silu(og_ref[0])
    o_ref[0] = o.astype(o_ref.dtype)


def _gla(proj, w2p, gb, ng, col_qk, col_v, col_og, col_gz, tb):
    B, T, _ = proj.shape
    hv = ng.shape[1]
    hk = gb.shape[1]
    return pl.pallas_call(
        functools.partial(_gla_kernel, n_chunk=tb // GLA_CHUNK),
        out_shape=jax.ShapeDtypeStruct((B, T, hv), BF16),
        grid=(B, T // tb),
        in_specs=[pl.BlockSpec((1, tb, 2 * hk), lambda b, j: (b, j, col_qk)),
                  pl.BlockSpec((1, tb, hv), lambda b, j: (b, j, col_v)),
                  pl.BlockSpec((1, tb, hv), lambda b, j: (b, j, col_og)),
                  pl.BlockSpec((1, tb, LANES), lambda b, j: (b, j, col_gz)),
                  pl.BlockSpec(w2p.shape, lambda b, j: (0, 0)),
                  pl.BlockSpec(gb.shape, lambda b, j: (0, 0)),
                  pl.BlockSpec(ng.shape, lambda b, j: (0, 0))],
        out_specs=pl.BlockSpec((1, tb, hv), lambda b, j: (b, j, 0)),
        scratch_shapes=[pltpu.VMEM((hv, hk), F32), pltpu.VMEM((tb, hk), F32), pltpu.VMEM((tb, hk), F32),
                        pltpu.VMEM((tb, hk), BF16), pltpu.VMEM((tb, hk), BF16), pltpu.VMEM((tb, hv), F32)],
        compiler_params=_cparams(("arbitrary", "arbitrary")),
        name="gla_mixer",
    )(proj, proj, proj, proj, w2p, gb, ng)


def _rope(t, cos, sin_signed):
    d = 32
    half = d // 2
    out = []
    for s in range(t.shape[1] // LANES):
        x = t[:, s * LANES:(s + 1) * LANES]
        lane = lax.broadcasted_iota(jnp.int32, x.shape, 1)
        up = pltpu.roll(x, LANES - half, 1)
        down = pltpu.roll(x, half, 1)
        rot = jnp.where((lane % d) < half, up, down)
        out.append(x * cos + rot * sin_signed)
    return jnp.concatenate(out, axis=1)


def _diff_kernel(q_ref, k_ref, v_ref, cosq_ref, sinq_ref, cosk_ref, sink_ref, lam_ref, g_ref, o_ref,
                 ks, vs, *, tq, lam_init):
    H = DIFF_HEADS
    d = q_ref.shape[2] // (2 * H)
    dv = v_ref.shape[2] // H
    j = pl.program_id(1)

    @pl.when(j == 0)
    def _():
        k = _rope(k_ref[0], cosk_ref[0], sink_ref[0])
        v = v_ref[0]
        for hc in range(2 * H):
            ks[hc] = k[:, hc * d:(hc + 1) * d].astype(BF16)
        ones_col = (lax.broadcasted_iota(jnp.int32, (v.shape[0], dv), 1) == 0).astype(BF16)
        for h in range(H):
            vs[h] = jnp.concatenate([v[:, h * dv:(h + 1) * dv].astype(BF16), ones_col], axis=1)

    q = _rope(q_ref[0], cosq_ref[0], sinq_ref[0]) * (d ** -0.5 * math.log2(math.e))
    lp = lam_ref[...]
    lam = (jnp.exp(jnp.sum(lp[0:1] * lp[1:2], axis=-1, keepdims=True))
           - jnp.exp(jnp.sum(lp[2:3] * lp[3:4], axis=-1, keepdims=True)) + lam_init)
    on_or_below_diag = (lax.broadcasted_iota(jnp.int32, (tq, tq), 1)
                        <= lax.broadcasted_iota(jnp.int32, (tq, tq), 0))

    def update(qh, kh, vh, m, acc, masked):
        s = lax.dot_general(qh, kh, (((1,), (1,)), ((), ())), preferred_element_type=F32)
        if masked:
            s = jnp.where(on_or_below_diag, s, -jnp.inf)
        m_new = jnp.maximum(m, jnp.max(s, axis=-1, keepdims=True))
        p = jnp.exp2(s - m_new).astype(BF16)
        acc = jnp.exp2(m - m_new) * acc + jnp.dot(p, vh, preferred_element_type=F32)
        return m_new, acc

    qs = [q[:, hc * d:(hc + 1) * d].astype(BF16) for hc in range(2 * H)]

    def kv_block(kb, carry, masked):
        rows = pl.ds(pl.multiple_of(kb * tq, tq), tq)
        out = []
        for hc in range(2 * H):
            m, acc = carry[2 * hc], carry[2 * hc + 1]
            out.extend(update(qs[hc], ks[hc, rows, :], vs[hc // 2, rows, :], m, acc, masked))
        return tuple(out)

    m0 = jnp.full((tq, 1), -jnp.inf, F32)
    a0 = jnp.zeros((tq, 2 * dv), F32)
    carry = lax.fori_loop(0, j, lambda kb, c: kv_block(kb, c, False), (m0, a0) * (2 * H))
    carry = kv_block(j, carry, True)

    for h in range(H):
        a1, a2 = carry[4 * h + 1], carry[4 * h + 3]
        comp = [a[:, 0:dv] / a[:, dv:dv + 1] for a in (a1, a2)]
        o = comp[0] - lam * comp[1]
        o = o * lax.rsqrt(jnp.mean(o * o, axis=-1, keepdims=True) + RMS_EPS)
        o = o * g_ref[...] * (1.0 - lam_init)
        o_ref[0, :, h * dv:(h + 1) * dv] = o.astype(o_ref.dtype)


def _diff_attention(proj, cos, sin, lam_p, g, col_q, col_k, col_v, layer_idx):
    B, T, _ = proj.shape
    H = DIFF_HEADS
    dv = g.shape[1]
    d = dv // 2
    w = 2 * H * d
    assert cos.shape[2] == LANES
    tq = min(ATTN_BLOCK, T)
    lam_init = 0.8 - 0.6 * math.exp(-0.3 * layer_idx)
    blk = lambda col: pl.BlockSpec((1, tq, w), lambda b, j: (b, j, col))
    full = lambda col: pl.BlockSpec((1, T, w), lambda b, j: (b, 0, col))
    tab_blk = pl.BlockSpec((1, tq, LANES), lambda b, j: (b, j, 0))
    tab_full = pl.BlockSpec((1, T, LANES), lambda b, j: (b, 0, 0))
    return pl.pallas_call(
        functools.partial(_diff_kernel, tq=tq, lam_init=lam_init),
        out_shape=jax.ShapeDtypeStruct((B, T, H * dv), BF16),
        grid=(B, T // tq),
        in_specs=[blk(col_q), full(col_k), full(col_v), tab_blk, tab_blk, tab_full, tab_full,
                  pl.BlockSpec(lam_p.shape, lambda b, j: (0, 0)),
                  pl.BlockSpec(g.shape, lambda b, j: (0, 0))],
        out_specs=pl.BlockSpec((1, tq, H * dv), lambda b, j: (b, j, 0)),
        scratch_shapes=[pltpu.VMEM((2 * H, T, d), BF16),
                        pltpu.VMEM((H, T, 2 * dv), BF16)],
        compiler_params=_cparams(("arbitrary", "arbitrary")),
        name="diff_attention",
    )(proj, proj, proj, cos, sin, cos, sin, lam_p, g)


def _conv_kernel(a_ref, b_ref, w_ref, cb_ref, lg_ref, lb_ref, o_ref, u_ref, *, rb):
    T = a_ref.shape[1]
    pad = u_ref.shape[0] - T
    u_ref[0:pad, :] = jnp.zeros((pad, u_ref.shape[1]), F32)
    u_ref[pad:pad + T, :] = a_ref[0] * _sigmoid(b_ref[0])
    first = pad - (CONV_WIDTH - 1)

    def block(i, carry):
        r0 = pl.multiple_of(i * rb, rb)
        win = u_ref[pl.ds(r0, rb + pad), :]
        acc = jnp.zeros((rb, u_ref.shape[1]), F32)
        for s in range(SUBLANES):
            taps = [j for j in range(CONV_WIDTH) if (first + j) % SUBLANES == s]
            rolled = win if s == 0 else pltpu.roll(win, rb + pad - s, 0)
            for j in taps:
                a0 = first + j - s
                acc = acc + w_ref[j:j + 1, :] * rolled[a0:a0 + rb, :]
        y = acc + cb_ref[...]
        mu = jnp.mean(y, axis=-1, keepdims=True)
        yc = y - mu
        var = jnp.mean(yc * yc, axis=-1, keepdims=True)
        y = yc * lax.rsqrt(var + LN_EPS) * lg_ref[...] + lb_ref[...]
        o_ref[0, pl.ds(r0, rb), :] = _silu(y).astype(o_ref.dtype)
        return carry

    lax.fori_loop(0, T // rb, block, 0)


def _conformer_conv(proj, w, cb, lg, lb, col_a, col_b):
    B, T, _ = proj.shape
    ch = w.shape[1]
    rb = min(CONV_ROWS, T)
    return pl.pallas_call(
        functools.partial(_conv_kernel, rb=rb),
        out_shape=jax.ShapeDtypeStruct((B, T, ch), BF16),
        grid=(B,),
        in_specs=[pl.BlockSpec((1, T, ch), lambda b: (b, 0, col_a)),
                  pl.BlockSpec((1, T, ch), lambda b: (b, 0, col_b)),
                  pl.BlockSpec(w.shape, lambda b: (0, 0)),
                  pl.BlockSpec(cb.shape, lambda b: (0, 0)),
                  pl.BlockSpec(lg.shape, lambda b: (0, 0)),
                  pl.BlockSpec(lb.shape, lambda b: (0, 0))],
        out_specs=pl.BlockSpec((1, T, ch), lambda b: (b, 0, 0)),
        scratch_shapes=[pltpu.VMEM((T + 32, ch), F32)],
        compiler_params=_cparams(("arbitrary",)),
        name="conformer_conv",
    )(proj, proj, w, cb, lg, lb)


def _rwkv_kernel(x_ref, mu_ref, w0_ref, w2_ref, a0_ref, a2_ref, g2_ref, kk_ref, ka_ref, rk_ref,
                 lng_ref, lnb_ref, o_ref, s_ref, prev_ref, gate_ref, bonus_ref, dec_ref, qeff_ref, yloc_ref,
                 mlow_ref, nc_ref, y_ref, *, n_chunk):
    C = RWKV_CHUNK
    H = RWKV_HEADS
    W = o_ref.shape[2]
    N = W // H

    @pl.when(pl.program_id(1) == 0)
    def _():
        s_ref[...] = jnp.zeros_like(s_ref)
        prev_ref[...] = jnp.zeros_like(prev_ref)

    lane = lax.broadcasted_iota(jnp.int32, (1, W), 1)
    head_mask = [(lane // N == h).astype(F32) for h in range(H)]
    block_diag = _group_matrix(W, N)
    head_sum = block_diag.astype(F32)
    ti = lax.broadcasted_iota(jnp.int32, (C, C), 0)
    tj = lax.broadcasted_iota(jnp.int32, (C, C), 1)
    tril_incl = ti >= tj
    tril_strict = ti > tj
    same_sub = (ti // RWKV_SUB) == (tj // RWKV_SUB)
    eye = (ti == tj).astype(F32)
    nc = n_chunk
    tb = nc * C

    def head_total(t, two_term=False):
        ones = head_sum.astype(BF16)
        if two_term:
            return jnp.dot(_hi_lo(t), jnp.concatenate([ones, ones], axis=0), preferred_element_type=F32)
        return jnp.dot(t.astype(BF16), ones, preferred_element_type=F32)

    x = x_ref[0]
    first_row = lax.broadcasted_iota(jnp.int32, (tb, 1), 0) == 0
    prev = jnp.where(first_row, prev_ref[...], pltpu.roll(x, 1, 0))
    prev_ref[...] = x[tb - 1:tb, :]
    xm = x + (prev - x) * mu_ref[...]
    r = xm[:, 0:W]
    k = xm[:, W:2 * W]
    v = xm[:, 2 * W:3 * W]
    zz = xm[:, 3 * W:3 * W + LANES]
    zg = xm[:, 3 * W + LANES:]
    w = -_softplus(-(w0_ref[...] + _mm(jnp.tanh(zz), w2_ref[...]))) - 0.5
    lw = -jnp.exp(w)
    a = _sigmoid(a0_ref[...] + _mm(zz, a2_ref[...]))
    gate_ref[...] = _mm(_sigmoid(zg), g2_ref[...])
    kk = k * kk_ref[...]
    kk = kk / jnp.maximum(jnp.sqrt(head_total(kk * kk, two_term=True)), 1e-12)
    k = k * (1.0 + (a - 1.0) * ka_ref[...])
    b = kk * a
    bonus_ref[...] = head_total(r * k * rk_ref[...]) * v

    c3 = lambda t: t.reshape(nc, C, t.shape[1])
    sums = _bmm(jnp.broadcast_to(tril_incl.astype(BF16)[None], (nc, C, C)), c3(_hi_lo(lw)))
    G3 = sums[:, :, 0:W] + sums[:, :, W:2 * W]
    total3 = jnp.broadcast_to(G3[:, C - 1:C, :], (nc, C, W))
    dec_ref[...] = jnp.exp(total3).reshape(tb, W)
    G = G3.reshape(tb, W)
    inv = jnp.exp(-G)
    to_end = jnp.exp(total3 - G3).reshape(tb, W)
    kap = kk * jnp.exp(G - lw)
    rho = r * jnp.exp(G)
    kap3, rho3, v3 = c3(kap), c3(rho), c3(v)
    bet_kt3 = jnp.concatenate([c3(b * inv), c3(k * inv)], axis=1)
    betc3, kc3 = c3(b * to_end), c3(k * to_end)

    def bmm_nt(p, q):
        return lax.dot_general(p.astype(BF16), q.astype(BF16), (((2,), (2,)), ((0,), (0,))),
                               preferred_element_type=F32)

    def bmm_tn(p, q):
        return lax.dot_general(p.astype(BF16), q.astype(BF16), (((1,), (1,)), ((0,), (0,))),
                               preferred_element_type=F32)

    wi = lax.broadcasted_iota(jnp.int32, (C, 2 * C), 0)
    wj = lax.broadcasted_iota(jnp.int32, (C, 2 * C), 1) % C
    A_bk, B_bk = [], []
    for h in range(H):
        lhs = jnp.concatenate([c3(kap * head_mask[h]), c3(rho * head_mask[h])], axis=1)
        prod = bmm_nt(lhs, bet_kt3)
        A_bk.append(jnp.where(wj < wi, prod[:, 0:C], 0.0))
        B_bk.append(jnp.where(wj <= wi, prod[:, C:2 * C], 0.0))
    A_b = jnp.concatenate([t[:, :, 0:C] for t in A_bk], axis=0)
    B_b = [t[:, :, 0:C] for t in B_bk]
    v3_low = jnp.concatenate([jnp.zeros_like(v3), v3], axis=1)

    Dg = jnp.where(same_sub, A_b, 0.0)
    Lo = A_b - Dg
    D2 = _bmm(Dg, Dg)
    D4 = _bmm(D2, D2)
    D8 = _bmm(D4, D4)
    Dinv = _bmm(_bmm(_bmm(eye - Dg, eye + D2), eye + D4), eye + D8)
    Nn = _bmm(Dinv, Lo)
    N2 = _bmm(Nn, Nn)
    Tm = _bmm(_bmm(eye - Nn, eye + N2), Dinv)
    Tm = [Tm[h * nc:(h + 1) * nc] for h in range(H)]

    def per_head(mats, t):
        reps = t.shape[2] // W
        acc = None
        for h in range(H):
            m = head_mask[h] if reps == 1 else jnp.concatenate([head_mask[h]] * reps, axis=1)
            u = m * _bmm(mats[h], t)
            acc = u if acc is None else acc + u
        return acc

    akv = per_head(A_bk, v3_low)
    tk = per_head(Tm, jnp.concatenate([kap3, akv], axis=2))
    kap_p = tk[:, :, 0:W]
    v_p = tk[:, :, W:2 * W]
    bb = per_head(B_b, jnp.concatenate([kap_p, v_p], axis=2))
    qeff_ref[...] = (rho3 - bb[:, :, 0:W]).astype(BF16)
    yloc_ref[...] = per_head(B_bk, v3_low) - bb[:, :, W:2 * W]
    mlow_ref[...] = jnp.where(block_diag, bmm_tn(kap_p, betc3), 0.0).astype(BF16)
    nc_ref[...] = jnp.where(block_diag, bmm_tn(jnp.concatenate([v3, v_p], axis=1),
                                               jnp.concatenate([kc3, -betc3], axis=1)), 0.0)

    def chunk(ci, carry):
        r0 = pl.multiple_of(ci * C, C)
        s = s_ref[...]
        sb = s.astype(BF16)
        y = lax.dot_general(qeff_ref[ci], sb, (((1,), (1,)), ((), ())), preferred_element_type=F32)
        y_ref[pl.ds(r0, C), :] = y + yloc_ref[ci]
        s_ref[...] = (s * dec_ref[pl.ds(r0, 8), :][0:1, :]
                      - jnp.dot(sb, mlow_ref[ci], preferred_element_type=F32) + nc_ref[ci])
        return carry

    lax.fori_loop(0, nc, chunk, 0)

    y = y_ref[...]
    mean = head_total(y, two_term=True) / N
    yc = y - mean
    var = head_total(yc * yc) / N
    yn = yc * lax.rsqrt(var + RWKV_GN_EPS) * lng_ref[...] + lnb_ref[...]
    o_ref[0] = ((yn + bonus_ref[...]) * gate_ref[...]).astype(o_ref.dtype)


def _rwkv7(proj, vecs, w2p, a2p, g2, col, tb):
    B, T, _ = proj.shape
    W = g2.shape[1]
    cols = vecs["mu"].shape[1]
    names = ("mu", "w0", "w2", "a0", "a2", "g2", "kk", "ka", "rk", "lng", "lnb")
    params = dict(vecs, w2=w2p, a2=a2p, g2=g2)
    const = lambda arr: pl.BlockSpec(arr.shape, lambda b, j: (0, 0))
    n_chunk = tb // RWKV_CHUNK
    return pl.pallas_call(
        functools.partial(_rwkv_kernel, n_chunk=n_chunk),
        out_shape=jax.ShapeDtypeStruct((B, T, W), BF16),
        grid=(B, T // tb),
        in_specs=[pl.BlockSpec((1, tb, cols), lambda b, j: (b, j, col))]
                 + [const(params[n]) for n in names],
        out_specs=pl.BlockSpec((1, tb, W), lambda b, j: (b, j, 0)),
        scratch_shapes=[pltpu.VMEM((W, W), F32), pltpu.VMEM((1, cols), F32),
                        pltpu.VMEM((tb, W), F32), pltpu.VMEM((tb, W), F32), pltpu.VMEM((tb, W), F32),
                        pltpu.VMEM((n_chunk, RWKV_CHUNK, W), BF16), pltpu.VMEM((n_chunk, RWKV_CHUNK, W), F32),
                        pltpu.VMEM((n_chunk, W, W), BF16), pltpu.VMEM((n_chunk, W, W), F32),
                        pltpu.VMEM((tb, W), F32)],
        compiler_params=_cparams(("arbitrary", "arbitrary")),
        name="rwkv7_mixer",
    )(proj, *[params[n] for n in names])


def _merge_kernel(x_ref, sc_ref, sh_ref, gt_ref, gpre_ref, gpost_ref, wg_ref, b0_ref, b1_ref, b2_ref,
                  b3_ref, wb_ref, wo_ref, o_ref):
    x = x_ref[0]
    D = x.shape[1]
    h = _rms_mod(x, gpre_ref[...], sc_ref[0], sh_ref[0]).astype(BF16)
    merged = None
    for g, br in enumerate((b0_ref, b1_ref, b2_ref, b3_ref)):
        gate = _sigmoid(jnp.dot(h, wg_ref[:, g * D:(g + 1) * D], preferred_element_type=F32))
        t = gate * jnp.dot(br[0], wb_ref[g], preferred_element_type=F32)
        merged = t if merged is None else merged + t
    y = _mm(merged, wo_ref[...])
    y = y * lax.rsqrt(jnp.mean(y * y, axis=-1, keepdims=True) + RMS_EPS) * gpost_ref[...]
    o_ref[0] = x + gt_ref[0] * y


def _merge(x, sc, sh, gt, gpre, gpost, w_gate, branches, w_branch, w_out, tm):
    B, T, D = x.shape
    bw = branches[0].shape[2]
    tok = lambda n: pl.BlockSpec((1, tm, n), lambda b, i: (b, i, 0))
    vec = pl.BlockSpec((1, 1, D), lambda b, i: (b, 0, 0))
    const2 = lambda arr: pl.BlockSpec(arr.shape, lambda b, i: (0, 0))
    return pl.pallas_call(
        _merge_kernel,
        out_shape=jax.ShapeDtypeStruct((B, T, D), F32),
        grid=(B, T // tm),
        in_specs=[tok(D), vec, vec, vec, const2(gpre), const2(gpost), const2(w_gate),
                  tok(bw), tok(bw), tok(bw), tok(bw),
                  pl.BlockSpec(w_branch.shape, lambda b, i: (0, 0, 0)), const2(w_out)],
        out_specs=tok(D),
        compiler_params=_cparams(("arbitrary", "arbitrary")),
        name="merge_out_proj",
    )(x, sc, sh, gt, gpre, gpost, w_gate, *branches, w_branch, w_out)


def _ffn_kernel(x_ref, sc_ref, sh_ref, gt_ref, gpre_ref, gpost_ref, wg_ref, wu_ref, wd_ref, o_ref, *, tf):
    x = x_ref[0]
    h = _rms_mod(x, gpre_ref[...], sc_ref[0], sh_ref[0]).astype(BF16)
    F = wg_ref.shape[1]
    acc = None
    for f0 in range(0, F, tf):
        f1 = min(f0 + tf, F)
        gate = jnp.dot(h, wg_ref[:, f0:f1], preferred_element_type=F32)
        up = jnp.dot(h, wu_ref[:, f0:f1], preferred_element_type=F32)
        t = jnp.dot((_silu(gate) * up).astype(BF16), wd_ref[f0:f1, :], preferred_element_type=F32)
        acc = t if acc is None else acc + t
    y = acc * lax.rsqrt(jnp.mean(acc * acc, axis=-1, keepdims=True) + RMS_EPS) * gpost_ref[...]
    o_ref[0] = x + gt_ref[0] * y


def _dense_ffn(x, sc, sh, gt, gpre, gpost, wg, wu, wd, tm):
    B, T, D = x.shape
    tok = pl.BlockSpec((1, tm, D), lambda b, i: (b, i, 0))
    vec = pl.BlockSpec((1, 1, D), lambda b, i: (b, 0, 0))
    const2 = lambda arr: pl.BlockSpec(arr.shape, lambda b, i: (0, 0))
    return pl.pallas_call(
        functools.partial(_ffn_kernel, tf=FFN_CHUNK),
        out_shape=jax.ShapeDtypeStruct((B, T, D), F32),
        grid=(B, T // tm),
        in_specs=[tok, vec, vec, vec, const2(gpre), const2(gpost), const2(wg), const2(wu), const2(wd)],
        out_specs=tok,
        compiler_params=_cparams(("arbitrary", "arbitrary")),
        name="dense_swiglu",
    )(x, sc, sh, gt, gpre, gpost, wg, wu, wd)


MOE_TOKEN_TILE = 256
MOE_ROW_TILE = 512
SEG_ALIGN = 16
SEG_PIECES = (256, 128, 64, 32, 16)
MOE_SMALL_SEG = 128


def _route_kernel(x_ref, sc_ref, sh_ref, gpre_ref, rw_ref, rb_ref, h_ref, mi_ref, mp_ref, cnt_ref):
    tm = x_ref.shape[1]
    h = _rms_mod(x_ref[0], gpre_ref[...], sc_ref[0], sh_ref[0])
    h_ref[...] = h.astype(BF16)
    h_hi = h.astype(BF16)
    h_lo = (h - h_hi.astype(F32)).astype(BF16)
    w = rw_ref[...]
    w_hi = w.astype(BF16)
    w_lo = (w - w_hi.astype(F32)).astype(BF16)
    logits = jnp.dot(jnp.concatenate([h_hi, h_lo, h_hi], axis=1), jnp.concatenate([w_hi, w_hi, w_lo], axis=0),
                     preferred_element_type=F32) + rb_ref[...]
    lane = lax.broadcasted_iota(jnp.int32, logits.shape, 1)
    v1 = jnp.max(logits, axis=-1, keepdims=True)
    i1 = jnp.min(jnp.where(logits == v1, lane, LANES), axis=-1, keepdims=True)
    rest = jnp.where(lane == i1, -jnp.inf, logits)
    v2 = jnp.max(rest, axis=-1, keepdims=True)
    i2 = jnp.min(jnp.where(rest == v2, lane, LANES), axis=-1, keepdims=True)
    e2 = jnp.exp(v2 - v1)
    p1 = 1.0 / (1.0 + e2)
    p2 = e2 / (1.0 + e2)
    oh1 = (lane == i1).astype(F32)
    oh2 = (lane == i2).astype(F32)
    both = oh1 + oh2
    earlier = (lax.broadcasted_iota(jnp.int32, (tm, tm), 1)
               < lax.broadcasted_iota(jnp.int32, (tm, tm), 0)).astype(BF16)
    before = jnp.dot(earlier, both.astype(BF16), preferred_element_type=F32)
    r1 = jnp.sum(oh1 * before, axis=-1, keepdims=True).astype(jnp.int32)
    r2 = jnp.sum(oh2 * before, axis=-1, keepdims=True).astype(jnp.int32)
    col = lax.broadcasted_iota(jnp.int32, mi_ref.shape, 1)
    mi_ref[...] = jnp.where(col == 0, i1, jnp.where(col == 1, i2, jnp.where(col == 2, r1,
                                                                           jnp.where(col == 3, r2, 0))))
    mp_ref[...] = jnp.where(col == 0, p1, jnp.where(col == 1, p2, 0.0))
    cnt_ref[0] = jnp.sum(both, axis=0, keepdims=True).astype(jnp.int32)


def _segment_pieces(n_rows):
    out = []
    for s in SEG_PIECES:
        if s == MOE_TOKEN_TILE:
            out.append((n_rows == s, 0, s))
        else:
            out.append(((n_rows & s) != 0, pl.multiple_of((n_rows // (2 * s)) * (2 * s), SEG_ALIGN), s))
    return out


def _all_segments_small(cnt_ref, tile):
    most = cnt_ref[tile * N_EXPERTS]
    for e in range(1, N_EXPERTS):
        most = jnp.maximum(most, cnt_ref[tile * N_EXPERTS + e])
    return most <= MOE_SMALL_SEG


def _dispatch_kernel(seg_ref, cnt_ref, h_ref, mit_ref, init_ref, xs_ref, buf_ref, sem):
    del init_ref
    tm = h_ref.shape[0]
    i = pl.program_id(0)
    e1, e2 = mit_ref[0:1, :], mit_ref[1:2, :]
    r1, r2 = mit_ref[2:3, :], mit_ref[3:4, :]
    slot = i % 2

    def compact(cap):
        row = lax.broadcasted_iota(jnp.int32, (cap, tm), 0)
        select = jnp.concatenate(
            [jnp.logical_or(jnp.logical_and(e1 == e, r1 == row), jnp.logical_and(e2 == e, r2 == row))
             for e in range(N_EXPERTS)], axis=0).astype(BF16)
        rows = jnp.dot(select, h_ref[...], preferred_element_type=F32).astype(BF16)
        for e in range(N_EXPERTS):
            buf_ref[slot, e * tm:e * tm + cap, :] = rows[e * cap:(e + 1) * cap]

    small = _all_segments_small(cnt_ref, i)
    pl.when(small)(functools.partial(compact, MOE_SMALL_SEG))
    pl.when(jnp.logical_not(small))(functools.partial(compact, tm))

    def segment_copies(tile, buf, e):
        n = cnt_ref[tile * N_EXPERTS + e]
        n_rows = ((n + SEG_ALIGN - 1) // SEG_ALIGN) * SEG_ALIGN
        dst = pl.multiple_of(seg_ref[tile * N_EXPERTS + e], SEG_ALIGN)
        return [(cond, pltpu.make_async_copy(buf_ref.at[buf, pl.ds(e * tm + off, s), :],
                                             xs_ref.at[pl.ds(dst + off, s), :], sem))
                for cond, off, s in _segment_pieces(n_rows)]

    def for_all_segments(tile, buf, action):
        for e in range(N_EXPERTS):
            for cond, cp in segment_copies(tile, buf, e):
                pl.when(cond)(getattr(cp, action))

    @pl.when(i > 0)
    def _():
        for_all_segments(i - 1, 1 - slot, "wait")

    for_all_segments(i, slot, "start")

    @pl.when(i == pl.num_programs(0) - 1)
    def _():
        for_all_segments(i, slot, "wait")


def _expert_kernel(te_ref, nv_ref, xs_ref, wg_ref, wu_ref, wd_ref, ys_ref, acc_ref):
    del te_ref
    r = pl.program_id(0)
    f = pl.program_id(1)
    valid = r < nv_ref[0]

    @pl.when(jnp.logical_and(valid, f == 0))
    def _():
        acc_ref[...] = jnp.zeros_like(acc_ref)

    @pl.when(valid)
    def _():
        x = xs_ref[...]
        tf = wg_ref.shape[2]
        acc = acc_ref[...]
        for f0 in range(0, tf, FFN_CHUNK):
            f1 = min(f0 + FFN_CHUNK, tf)
            gate = jnp.dot(x, wg_ref[0, :, f0:f1], preferred_element_type=F32)
            up = jnp.dot(x, wu_ref[0, :, f0:f1], preferred_element_type=F32)
            acc = acc + jnp.dot((_silu(gate) * up).astype(BF16), wd_ref[0, f0:f1, :],
                                preferred_element_type=F32)
        acc_ref[...] = acc

    @pl.when(f == pl.num_programs(1) - 1)
    def _():
        @pl.when(valid)
        def _():
            ys_ref[...] = acc_ref[...].astype(ys_ref.dtype)

        @pl.when(jnp.logical_not(valid))
        def _():
            ys_ref[...] = jnp.zeros_like(ys_ref)


def _combine_kernel(seg_ref, cnt_ref, ys_ref, mi_ref, mp_ref, x_ref, gt_ref, gpost_ref, o_ref, win_ref,
                    pair_ref, sem):
    tm = x_ref.shape[1]
    i = pl.program_id(0) * pl.num_programs(1) + pl.program_id(1)
    n_tiles = pl.num_programs(0) * pl.num_programs(1)
    slot = i % 2

    def window_copy(tile, buf, e, rows):
        src = pl.multiple_of(seg_ref[tile * N_EXPERTS + e], SEG_ALIGN)
        return pltpu.make_async_copy(ys_ref.at[pl.ds(src, rows), :], win_ref.at[buf, pl.ds(e * rows, rows), :],
                                     sem.at[buf, e])

    def for_all_windows(tile, buf, action):
        small = _all_segments_small(cnt_ref, tile)
        for rows, cond in ((MOE_SMALL_SEG, small), (tm, jnp.logical_not(small))):
            @pl.when(cond)
            def _():
                for e in range(N_EXPERTS):
                    getattr(window_copy(tile, buf, e, rows), action)()

    @pl.when(i == 0)
    def _():
        for_all_windows(i, slot, "start")

    @pl.when(i + 1 < n_tiles)
    def _():
        for_all_windows(i + 1, 1 - slot, "start")

    for_all_windows(i, slot, "wait")
    e1, e2 = mi_ref[:, 0:1], mi_ref[:, 1:2]
    r1, r2 = mi_ref[:, 2:3], mi_ref[:, 3:4]

    def expand_rows(cap):
        col = lax.broadcasted_iota(jnp.int32, (tm, cap), 1)
        expand = jnp.concatenate(
            [jnp.concatenate([jnp.logical_and(e1 == e, r1 == col), jnp.logical_and(e2 == e, r2 == col)], axis=0)
             for e in range(N_EXPERTS)], axis=1).astype(BF16)
        pair_ref[...] = jnp.dot(expand, win_ref[slot, 0:N_EXPERTS * cap, :], preferred_element_type=F32)

    small = _all_segments_small(cnt_ref, i)
    pl.when(small)(functools.partial(expand_rows, MOE_SMALL_SEG))
    pl.when(jnp.logical_not(small))(functools.partial(expand_rows, tm))
    y = mp_ref[:, 0:1] * pair_ref[0:tm, :] + mp_ref[:, 1:2] * pair_ref[tm:2 * tm, :]
    y = y * lax.rsqrt(jnp.mean(y * y, axis=-1, keepdims=True) + RMS_EPS) * gpost_ref[...]
    o_ref[0] = x_ref[0] + gt_ref[0] * y


def _moe_ffn(x, sc, sh, gt, gpre, gpost, rw, rb, wg, wu, wd, tf):
    B, T, D = x.shape
    E, _, F = wg.shape
    tm = min(MOE_TOKEN_TILE, T)
    assert tm == MOE_TOKEN_TILE and E == N_EXPERTS
    nT = T // tm
    n_tok_tiles = B * nT
    N = B * T
    max_rows = 2 * N + n_tok_tiles * E * (SEG_ALIGN - 1) + E * (MOE_ROW_TILE - SEG_ALIGN)
    n_row_tiles = -(-max_rows // MOE_ROW_TILE) + 1
    P = n_row_tiles * MOE_ROW_TILE

    vec = pl.BlockSpec((1, 1, D), lambda b, j: (b, 0, 0))
    const2 = lambda arr: pl.BlockSpec(arr.shape, lambda b, j: (0, 0))
    flat = lambda n: pl.BlockSpec((tm, n), lambda b, j: (b * nT + j, 0))
    h, mi, mp, cnt = pl.pallas_call(
        _route_kernel,
        out_shape=(jax.ShapeDtypeStruct((N, D), BF16), jax.ShapeDtypeStruct((N, 8), jnp.int32),
                   jax.ShapeDtypeStruct((N, 8), F32), jax.ShapeDtypeStruct((n_tok_tiles, 1, LANES), jnp.int32)),
        grid=(B, nT),
        in_specs=[pl.BlockSpec((1, tm, D), lambda b, j: (b, j, 0)), vec, vec, const2(gpre), const2(rw),
                  const2(rb)],
        out_specs=(flat(D), flat(8), flat(8), pl.BlockSpec((1, 1, LANES), lambda b, j: (b * nT + j, 0, 0))),
        compiler_params=_cparams(("arbitrary", "arbitrary")),
        name="moe_route",
    )(x, sc, sh, gpre, rw, rb)

    counts = cnt[:, 0, :E]
    seg_len = (counts + SEG_ALIGN - 1) // SEG_ALIGN * SEG_ALIGN
    group_len = (jnp.sum(seg_len, axis=0) + MOE_ROW_TILE - 1) // MOE_ROW_TILE * MOE_ROW_TILE
    group_end = jnp.cumsum(group_len)
    seg_start = (group_end - group_len)[None, :] + jnp.cumsum(seg_len, axis=0) - seg_len
    seg_start = seg_start.reshape(-1).astype(jnp.int32)
    counts = counts.reshape(-1)
    n_valid = (group_end[-1:] // MOE_ROW_TILE).astype(jnp.int32)
    tile_first_row = jnp.arange(n_row_tiles, dtype=jnp.int32) * MOE_ROW_TILE
    tile_expert = jnp.minimum(jnp.sum(tile_first_row[:, None] >= group_end[None, :], axis=1), E - 1)
    tile_expert = tile_expert.astype(jnp.int32)

    xs = pl.pallas_call(
        _dispatch_kernel,
        out_shape=jax.ShapeDtypeStruct((P, D), BF16),
        grid_spec=pltpu.PrefetchScalarGridSpec(
            num_scalar_prefetch=2,
            grid=(n_tok_tiles,),
            in_specs=[pl.BlockSpec((tm, D), lambda i, seg, n: (i, 0)),
                      pl.BlockSpec((8, tm), lambda i, seg, n: (0, i)),
                      pl.BlockSpec(memory_space=pl.ANY)],
            out_specs=pl.BlockSpec(memory_space=pl.ANY),
            scratch_shapes=[pltpu.VMEM((2, E * tm, D), BF16), pltpu.SemaphoreType.DMA(())]),
        input_output_aliases={4: 0},
        compiler_params=_cparams(("arbitrary",)),
        name="moe_dispatch",
    )(seg_start, counts, h, mi.T, jnp.zeros((P, D), BF16))

    nf = F // tf
    live = lambda r, f, nv: jnp.where(r < nv[0], f, nf - 1)
    ys = pl.pallas_call(
        _expert_kernel,
        out_shape=jax.ShapeDtypeStruct((P, D), BF16),
        grid_spec=pltpu.PrefetchScalarGridSpec(
            num_scalar_prefetch=2,
            grid=(n_row_tiles, nf),
            in_specs=[pl.BlockSpec((MOE_ROW_TILE, D), lambda r, f, te, nv: (r, 0)),
                      pl.BlockSpec((1, D, tf), lambda r, f, te, nv: (te[r], 0, live(r, f, nv))),
                      pl.BlockSpec((1, D, tf), lambda r, f, te, nv: (te[r], 0, live(r, f, nv))),
                      pl.BlockSpec((1, tf, D), lambda r, f, te, nv: (te[r], live(r, f, nv), 0))],
            out_specs=pl.BlockSpec((MOE_ROW_TILE, D), lambda r, f, te, nv: (r, 0)),
            scratch_shapes=[pltpu.VMEM((MOE_ROW_TILE, D), F32)]),
        compiler_params=_cparams(("arbitrary", "arbitrary")),
        name="moe_experts",
    )(tile_expert, n_valid, xs, wg, wu, wd)

    return pl.pallas_call(
        _combine_kernel,
        out_shape=jax.ShapeDtypeStruct((B, T, D), F32),
        grid_spec=pltpu.PrefetchScalarGridSpec(
            num_scalar_prefetch=2,
            grid=(B, nT),
            in_specs=[pl.BlockSpec(memory_space=pl.ANY),
                      pl.BlockSpec((tm, 8), lambda b, j, seg, n: (b * nT + j, 0)),
                      pl.BlockSpec((tm, 8), lambda b, j, seg, n: (b * nT + j, 0)),
                      pl.BlockSpec((1, tm, D), lambda b, j, seg, n: (b, j, 0)),
                      pl.BlockSpec((1, 1, D), lambda b, j, seg, n: (b, 0, 0)),
                      pl.BlockSpec(gpost.shape, lambda b, j, seg, n: (0, 0))],
            out_specs=pl.BlockSpec((1, tm, D), lambda b, j, seg, n: (b, j, 0)),
            scratch_shapes=[pltpu.VMEM((2, E * tm, D), BF16), pltpu.VMEM((2 * tm, D), F32),
                            pltpu.SemaphoreType.DMA((2, E))]),
        compiler_params=_cparams(("arbitrary", "arbitrary")),
        name="moe_combine",
    )(seg_start, counts, ys, mi, mp, x, gt, gpost)


def _rope_tables(positions, groups):
    d = 32
    inv = 1.0 / (ROPE_THETA ** (jnp.arange(0, d, 2, dtype=F32) / d))
    ang = positions.astype(F32)[..., None] * inv
    cos, sin = jnp.cos(ang), jnp.sin(ang)
    cos = jnp.tile(jnp.concatenate([cos, cos], axis=-1), (1, 1, groups))
    sin = jnp.tile(jnp.concatenate([-sin, sin], axis=-1), (1, 1, groups))
    return cos, sin


def _pad_rows(w, rows, offset):
    out = jnp.zeros((rows, w.shape[1]), w.dtype)
    return out.at[offset:offset + w.shape[0]].set(w)


def kernel(x, c, positions, ada_w, ada_b, norm_mix_pre, norm_mix_post, norm_ffn_pre, norm_ffn_post, w_in, gla_gate_w2, gla_gate_b, gla_norm, diff_lambda, diff_subln, conv_w, conv_b, conv_ln_g, conv_ln_b, rwkv_mu, rwkv_w0, rwkv_w2, rwkv_a0, rwkv_a2, rwkv_g2, rwkv_k_k, rwkv_k_a, rwkv_r_k, rwkv_ln_g, rwkv_ln_b, w_branch, w_out, ffn_w_gate, ffn_w_up, ffn_w_down, router_w, router_b, moe_w_gate, moe_w_up, moe_w_down):
    B, T, D = x.shape
    L = ada_w.shape[0]
    W = D // N_BRANCH
    hk = gla_gate_b.shape[1]
    decay_rank = rwkv_w2.shape[1]
    a_rank = rwkv_a2.shape[1]
    gate_rank = rwkv_g2.shape[1]
    assert decay_rank + a_rank == LANES and 2 * hk == W and gate_rank == LANES
    n_mix = 3 * W + 3 * W + 2 * W + (3 * W + decay_rank + a_rank + gate_rank)
    sizes = (hk, hk, W, W, GLA_GATE_RANK, W, W, W, W, W, 3 * W + LANES + gate_rank, N_BRANCH * D)
    offs = [0]
    for s in sizes:
        offs.append(offs[-1] + s)
    assert offs[-1] == w_in.shape[2]
    tm = min(TOKEN_TILE, T)
    tb = min(SEQ_BLOCK, T)

    mod = _modulation(c, ada_w, ada_b)
    cos, sin = _rope_tables(positions, LANES // 32)

    for l in range(L):
        m = mod[l].reshape(B, 1, 6 * D)
        sh_m, sc_m, gt_m, sh_f, sc_f, gt_f = [m[:, :, i * D:(i + 1) * D] for i in range(6)]

        wl = w_in[l]
        gz_cols = jnp.zeros((D, LANES), F32).at[:, :GLA_GATE_RANK].set(wl[:, offs[4]:offs[5]])
        w_mix = jnp.concatenate([wl[:, offs[0]:offs[4]], wl[:, offs[5]:offs[11]], gz_cols], axis=1)
        w_mix = w_mix.astype(BF16)
        w_gate = wl[:, offs[11]:offs[12]].astype(BF16)
        proj = _in_projection(x, sc_m, sh_m, norm_mix_pre[l][None], w_mix, tm)

        w2p = _pad_rows(gla_gate_w2[l], LANES, 0)
        o_gla = _gla(proj, w2p, gla_gate_b[l][None], jnp.tile(gla_norm[l], GLA_HEADS)[None],
                     col_qk=0, col_v=1, col_og=2, col_gz=(n_mix // LANES), tb=tb)
        o_diff = _diff_attention(proj, cos, sin, diff_lambda[l],
                                 diff_subln[l][None], col_q=3, col_k=4, col_v=5, layer_idx=l)
        o_conv = _conformer_conv(proj, _pad_rows(conv_w[l], 32, 0), conv_b[l][None],
                                 conv_ln_g[l][None], conv_ln_b[l][None], col_a=6, col_b=7)
        vecs = dict(mu=rwkv_mu[l][None], w0=rwkv_w0[l][None], a0=rwkv_a0[l][None],
                    kk=rwkv_k_k[l][None], ka=rwkv_k_a[l][None], rk=rwkv_r_k[l].reshape(1, W),
                    lng=rwkv_ln_g[l][None], lnb=rwkv_ln_b[l][None])
        o_rwkv = _rwkv7(proj, vecs, _pad_rows(rwkv_w2[l], LANES, 0).astype(BF16),
                        _pad_rows(rwkv_a2[l], LANES, decay_rank).astype(BF16),
                        rwkv_g2[l].astype(BF16), col=2, tb=tb)
        x = _merge(x, sc_m, sh_m, gt_m, norm_mix_pre[l][None], norm_mix_post[l][None], w_gate,
                   (o_gla, o_diff, o_conv, o_rwkv), w_branch[l].astype(BF16), w_out[l].astype(BF16), tm)

        i = l // 2
        if l % 2 == 0:
            x = _dense_ffn(x, sc_f, sh_f, gt_f, norm_ffn_pre[l][None], norm_ffn_post[l][None],
                           ffn_w_gate[i].astype(BF16), ffn_w_up[i].astype(BF16),
                           ffn_w_down[i].astype(BF16), tm)
        else:
            rw = jnp.zeros((D, LANES), F32).at[:, :N_EXPERTS].set(router_w[i])
            rb = jnp.full((1, LANES), -jnp.inf, F32).at[0, :N_EXPERTS].set(router_b[i])
            x = _moe_ffn(x, sc_f, sh_f, gt_f, norm_ffn_pre[l][None], norm_ffn_post[l][None], rw, rb,
                         moe_w_gate[i].astype(BF16), moe_w_up[i].astype(BF16),
                         moe_w_down[i].astype(BF16), tf=moe_w_gate.shape[3])
    return x
```

```python
import functools
import math

import jax
import jax.numpy as jnp
from jax import lax
from jax.experimental import pallas as pl
from jax.experimental.pallas import tpu as pltpu

F32 = jnp.float32
BF16 = jnp.bfloat16
HIGHEST = lax.Precision.HIGHEST

N_BRANCH = 4
GLA_HEADS = 4
GLA_GATE_RANK = 16
GLA_GATE_NORMALIZER = 16.0
GLA_CHUNK = 32
DIFF_HEADS = 4
ROPE_THETA = 10000.0
CONV_WIDTH = 31
RWKV_HEADS = 4
RWKV_CHUNK = 64
RWKV_SUB = 16
N_EXPERTS = 8
RMS_EPS = 1e-6
LN_EPS = 1e-5
RWKV_GN_EPS = 64e-5
LANES = 128
SUBLANES = 8
VMEM_LIMIT = 56 * 1024 * 1024

TOKEN_TILE = 512
SEQ_BLOCK = 512
ATTN_BLOCK = 512
CONV_ROWS = 128
FFN_CHUNK = 512


def _cparams(sem):
    return pltpu.CompilerParams(dimension_semantics=sem, vmem_limit_bytes=VMEM_LIMIT)


def _mm(a, b):
    return jnp.dot(a.astype(BF16), b.astype(BF16), preferred_element_type=F32)


def _mm_nt(a, b):
    return lax.dot_general(a.astype(BF16), b.astype(BF16), (((1,), (1,)), ((), ())),
                           preferred_element_type=F32)


def _mm_tn(a, b):
    return lax.dot_general(a.astype(BF16), b.astype(BF16), (((0,), (0,)), ((), ())),
                           preferred_element_type=F32)


def _bmm(a, b):
    return lax.dot_general(a.astype(BF16), b.astype(BF16), (((2,), (1,)), ((0,), (0,))),
                           preferred_element_type=F32)


def _mm_f32(a, b):
    return jnp.dot(a, b, precision=HIGHEST, preferred_element_type=F32)


def _hi_lo(x):
    hi = x.astype(BF16)
    return jnp.concatenate([hi, (x - hi.astype(F32)).astype(BF16)], axis=1)


def _sigmoid(x):
    return 0.5 * jnp.tanh(0.5 * x) + 0.5


def _silu(x):
    return x * _sigmoid(x)


def _softplus(x):
    return jnp.maximum(x, 0.0) + jnp.log(1.0 + jnp.exp(-jnp.abs(x)))


def _group_matrix(n, group):
    r = lax.broadcasted_iota(jnp.int32, (n, n), 0) // group
    c = lax.broadcasted_iota(jnp.int32, (n, n), 1) // group
    return r == c


def _rms_mod(x, gain, scale, shift):
    y = x * lax.rsqrt(jnp.mean(x * x, axis=-1, keepdims=True) + RMS_EPS)
    return y * gain * (1.0 + scale) + shift


def _mod_kernel(c_ref, w_ref, b_ref, o_ref):
    o_ref[0] = _mm_f32(_silu(c_ref[...]), w_ref[0]) + b_ref[0]


def _modulation(c, ada_w, ada_b):
    L, D, M = ada_w.shape
    B = c.shape[0]
    tn = M // 4
    return pl.pallas_call(
        _mod_kernel,
        out_shape=jax.ShapeDtypeStruct((L, B, M), F32),
        grid=(L, M // tn),
        in_specs=[pl.BlockSpec((B, D), lambda l, j: (0, 0)),
                  pl.BlockSpec((1, D, tn), lambda l, j: (l, 0, j)),
                  pl.BlockSpec((1, 1, tn), lambda l, j: (l, 0, j))],
        out_specs=pl.BlockSpec((1, B, tn), lambda l, j: (l, 0, j)),
        compiler_params=_cparams(("arbitrary", "arbitrary")),
        name="adaln_mod",
    )(c, ada_w, ada_b.reshape(L, 1, M))


def _inproj_kernel(x_ref, sc_ref, sh_ref, g_ref, w_ref, o_ref):
    h = _rms_mod(x_ref[0], g_ref[...], sc_ref[0], sh_ref[0])
    o_ref[0] = _mm(h, w_ref[...])


def _in_projection(x, sc, sh, gain, w, tm):
    B, T, D = x.shape
    n = w.shape[1]
    return pl.pallas_call(
        _inproj_kernel,
        out_shape=jax.ShapeDtypeStruct((B, T, n), F32),
        grid=(B, T // tm),
        in_specs=[pl.BlockSpec((1, tm, D), lambda b, i: (b, i, 0)),
                  pl.BlockSpec((1, 1, D), lambda b, i: (b, 0, 0)),
                  pl.BlockSpec((1, 1, D), lambda b, i: (b, 0, 0)),
                  pl.BlockSpec((1, D), lambda b, i: (0, 0)),
                  pl.BlockSpec((D, n), lambda b, i: (0, 0))],
        out_specs=pl.BlockSpec((1, tm, n), lambda b, i: (b, i, 0)),
        compiler_params=_cparams(("arbitrary", "arbitrary")),
        name="in_proj",
    )(x, sc, sh, gain, w)


def _gla_kernel(qk_ref, v_ref, og_ref, gz_ref, w2_ref, gb_ref, ng_ref, o_ref, s_ref, g_ref, r_ref,
                qd_ref, kd_ref, oacc_ref, *, n_chunk, unroll=False):
    C = GLA_CHUNK
    hk = qk_ref.shape[2] // 2
    hv = v_ref.shape[2]
    dk = hk // GLA_HEADS
    dv = hv // GLA_HEADS

    tb = n_chunk * C

    @pl.when(pl.program_id(1) == 0)
    def _():
        s_ref[...] = jnp.zeros_like(s_ref)

    z = _mm(gz_ref[0], w2_ref[...]) + gb_ref[...]
    gk = (jnp.minimum(z, 0.0) - jnp.log(1.0 + jnp.exp(-jnp.abs(z)))) / GLA_GATE_NORMALIZER
    tri = (lax.broadcasted_iota(jnp.int32, (C, C), 1)
           <= lax.broadcasted_iota(jnp.int32, (C, C), 0)).astype(BF16)
    sums = _bmm(jnp.broadcast_to(tri[None], (n_chunk, C, C)), _hi_lo(gk).reshape(n_chunk, C, 2 * hk))
    G3 = sums[:, :, 0:hk] + sums[:, :, hk:2 * hk]
    G_all = G3.reshape(tb, hk)
    R_all = (jnp.broadcast_to(G3[:, C - 1:C, :], (n_chunk, C, hk)) - G3).reshape(tb, hk)
    g_ref[...] = G_all
    r_ref[...] = R_all
    qd_ref[...] = (qk_ref[0, :, 0:hk] * (dk ** -0.5) * jnp.exp(G_all)).astype(BF16)
    kd_ref[...] = (qk_ref[0, :, hk:2 * hk] * jnp.exp(R_all)).astype(BF16)

    causal = (lax.broadcasted_iota(jnp.int32, (C, C, hk), 1)
              <= lax.broadcasted_iota(jnp.int32, (C, C, hk), 0))
    er = lax.broadcasted_iota(jnp.int32, (hk, hv), 0) // dk
    ec = lax.broadcasted_iota(jnp.int32, (hk, hv), 1) // dv
    expand = (er == ec).astype(BF16)
    sr = lax.broadcasted_iota(jnp.int32, (hv, hk), 0) // dv
    scol = lax.broadcasted_iota(jnp.int32, (hv, hk), 1) // dk
    state_mask = sr == scol
    head_mean = (_group_matrix(hv, dv).astype(F32) / dv).astype(BF16)

    def chunk(ci, carry):
        r0 = pl.multiple_of(ci * C, C)
        rows = pl.ds(r0, C)
        q = qk_ref[0, rows, 0:hk] * (dk ** -0.5)
        k = qk_ref[0, rows, hk:2 * hk]
        v = v_ref[0, rows, :]
        G = g_ref[rows, :]
        pair = (C, C, hk)
        g_diff = jnp.broadcast_to(G[:, None, :], pair) - jnp.broadcast_to(G[None], pair)
        decay = jnp.exp(jnp.where(causal, g_diff, -jnp.inf))
        p = jnp.broadcast_to(q[:, None, :], pair) * jnp.broadcast_to(k[None], pair) * decay
        a_exp = jnp.dot(p.reshape(C * C, hk).astype(BF16), expand, preferred_element_type=F32)
        v_j = jnp.broadcast_to(v[None], (C, C, hv))
        o_intra = jnp.sum(a_exp.reshape(C, C, hv) * v_j, axis=1)
        g_total = G[0:1, :] + r_ref[pl.ds(r0, 8), :][0:1, :]
        s = s_ref[...]
        o_inter = lax.dot_general(qd_ref[rows, :], s.astype(BF16), (((1,), (1,)), ((), ())),
                                  preferred_element_type=F32)
        kv = lax.dot_general(v.astype(BF16), kd_ref[rows, :], (((0,), (0,)), ((), ())),
                             preferred_element_type=F32)
        s_ref[...] = s * jnp.exp(g_total) + jnp.where(state_mask, kv, 0.0)
        oacc_ref[rows, :] = o_intra + o_inter
        return carry

    lax.fori_loop(0, n_chunk, chunk, 0, unroll=unroll)

    o = oacc_ref[...]
    ms = jnp.dot(_hi_lo(o * o), jnp.concatenate([head_mean, head_mean], axis=0),
                 preferred_element_type=F32)
    o = o * lax.rsqrt(ms + RMS_EPS) * ng_ref[...] * _silu(og_ref[0])
    o_ref[0] = o.astype(o_ref.dtype)


def _rope(t, cos, sin_signed):
    d = 32
    half = d // 2
    out = []
    for s in range(t.shape[1] // LANES):
        x = t[:, s * LANES:(s + 1) * LANES]
        lane = lax.broadcasted_iota(jnp.int32, x.shape, 1)
        up = pltpu.roll(x, LANES - half, 1)
        down = pltpu.roll(x, half, 1)
        rot = jnp.where((lane % d) < half, up, down)
        out.append(x * cos + rot * sin_signed)
    return jnp.concatenate(out, axis=1)


def _diff_kernel(q_ref, k_ref, v_ref, cosq_ref, sinq_ref, cosk_ref, sink_ref, lam_ref, g_ref, o_ref,
                 ks, vs, *, tq, lam_init):
    H = DIFF_HEADS
    d = q_ref.shape[2] // (2 * H)
    dv = v_ref.shape[2] // H
    j = pl.program_id(1)

    @pl.when(j == 0)
    def _():
        k = _rope(k_ref[0], cosk_ref[0], sink_ref[0])
        v = v_ref[0]
        for hc in range(2 * H):
            ks[hc] = k[:, hc * d:(hc + 1) * d].astype(BF16)
        ones_col = (lax.broadcasted_iota(jnp.int32, (v.shape[0], dv), 1) == 0).astype(BF16)
        for h in range(H):
            vs[h] = jnp.concatenate([v[:, h * dv:(h + 1) * dv].astype(BF16), ones_col], axis=1)

    q = _rope(q_ref[0], cosq_ref[0], sinq_ref[0]) * (d ** -0.5 * math.log2(math.e))
    lp = lam_ref[...]
    lam = (jnp.exp(jnp.sum(lp[0:1] * lp[1:2], axis=-1, keepdims=True))
           - jnp.exp(jnp.sum(lp[2:3] * lp[3:4], axis=-1, keepdims=True)) + lam_init)
    on_or_below_diag = (lax.broadcasted_iota(jnp.int32, (tq, tq), 1)
                        <= lax.broadcasted_iota(jnp.int32, (tq, tq), 0))

    def update(qh, kh, vh, m, acc, masked):
        s = lax.dot_general(qh, kh, (((1,), (1,)), ((), ())), preferred_element_type=F32)
        if masked:
            s = jnp.where(on_or_below_diag, s, -jnp.inf)
        m_new = jnp.maximum(m, jnp.max(s, axis=-1, keepdims=True))
        p = jnp.exp2(s - m_new).astype(BF16)
        acc = jnp.exp2(m - m_new) * acc + jnp.dot(p, vh, preferred_element_type=F32)
        return m_new, acc

    qs = [q[:, hc * d:(hc + 1) * d].astype(BF16) for hc in range(2 * H)]

    def kv_block(kb, carry, masked):
        rows = pl.ds(pl.multiple_of(kb * tq, tq), tq)
        out = []
        for hc in range(2 * H):
            m, acc = carry[2 * hc], carry[2 * hc + 1]
            out.extend(update(qs[hc], ks[hc, rows, :], vs[hc // 2, rows, :], m, acc, masked))
        return tuple(out)

    m0 = jnp.full((tq, 1), -jnp.inf, F32)
    a0 = jnp.zeros((tq, 2 * dv), F32)
    carry = lax.fori_loop(0, j, lambda kb, c: kv_block(kb, c, False), (m0, a0) * (2 * H))
    carry = kv_block(j, carry, True)

    for h in range(H):
        a1, a2 = carry[4 * h + 1], carry[4 * h + 3]
        comp = [a[:, 0:dv] / a[:, dv:dv + 1] for a in (a1, a2)]
        o = comp[0] - lam * comp[1]
        o = o * lax.rsqrt(jnp.mean(o * o, axis=-1, keepdims=True) + RMS_EPS)
        o = o * g_ref[...] * (1.0 - lam_init)
        o_ref[0, :, h * dv:(h + 1) * dv] = o.astype(o_ref.dtype)


def _diff_attention(proj, cos, sin, lam_p, g, col_q, col_k, col_v, layer_idx):
    B, T, _ = proj.shape
    H = DIFF_HEADS
    dv = g.shape[1]
    d = dv // 2
    w = 2 * H * d
    assert cos.shape[2] == LANES
    tq = min(ATTN_BLOCK, T)
    lam_init = 0.8 - 0.6 * math.exp(-0.3 * layer_idx)
    blk = lambda col: pl.BlockSpec((1, tq, w), lambda b, j: (b, j, col))
    full = lambda col: pl.BlockSpec((1, T, w), lambda b, j: (b, 0, col))
    tab_blk = pl.BlockSpec((1, tq, LANES), lambda b, j: (b, j, 0))
    tab_full = pl.BlockSpec((1, T, LANES), lambda b, j: (b, 0, 0))
    return pl.pallas_call(
        functools.partial(_diff_kernel, tq=tq, lam_init=lam_init),
        out_shape=jax.ShapeDtypeStruct((B, T, H * dv), BF16),
        grid=(B, T // tq),
        in_specs=[blk(col_q), full(col_k), full(col_v), tab_blk, tab_blk, tab_full, tab_full,
                  pl.BlockSpec(lam_p.shape, lambda b, j: (0, 0)),
                  pl.BlockSpec(g.shape, lambda b, j: (0, 0))],
        out_specs=pl.BlockSpec((1, tq, H * dv), lambda b, j: (b, j, 0)),
        scratch_shapes=[pltpu.VMEM((2 * H, T, d), BF16),
                        pltpu.VMEM((H, T, 2 * dv), BF16)],
        compiler_params=_cparams(("arbitrary", "arbitrary")),
        name="diff_attention",
    )(proj, proj, proj, cos, sin, cos, sin, lam_p, g)


def _conv_kernel(a_ref, b_ref, w_ref, cb_ref, lg_ref, lb_ref, o_ref, u_ref, *, rb):
    T = a_ref.shape[1]
    pad = u_ref.shape[0] - T
    u_ref[0:pad, :] = jnp.zeros((pad, u_ref.shape[1]), F32)
    u_ref[pad:pad + T, :] = a_ref[0] * _sigmoid(b_ref[0])
    first = pad - (CONV_WIDTH - 1)

    def block(i, carry):
        r0 = pl.multiple_of(i * rb, rb)
        win = u_ref[pl.ds(r0, rb + pad), :]
        acc = jnp.zeros((rb, u_ref.shape[1]), F32)
        for s in range(SUBLANES):
            taps = [j for j in range(CONV_WIDTH) if (first + j) % SUBLANES == s]
            rolled = win if s == 0 else pltpu.roll(win, rb + pad - s, 0)
            for j in taps:
                a0 = first + j - s
                acc = acc + w_ref[j:j + 1, :] * rolled[a0:a0 + rb, :]
        y = acc + cb_ref[...]
        mu = jnp.mean(y, axis=-1, keepdims=True)
        yc = y - mu
        var = jnp.mean(yc * yc, axis=-1, keepdims=True)
        y = yc * lax.rsqrt(var + LN_EPS) * lg_ref[...] + lb_ref[...]
        o_ref[0, pl.ds(r0, rb), :] = _silu(y).astype(o_ref.dtype)
        return carry

    lax.fori_loop(0, T // rb, block, 0)


def _conformer_conv(proj, w, cb, lg, lb, col_a, col_b):
    B, T, _ = proj.shape
    ch = w.shape[1]
    rb = min(CONV_ROWS, T)
    return pl.pallas_call(
        functools.partial(_conv_kernel, rb=rb),
        out_shape=jax.ShapeDtypeStruct((B, T, ch), BF16),
        grid=(B,),
        in_specs=[pl.BlockSpec((1, T, ch), lambda b: (b, 0, col_a)),
                  pl.BlockSpec((1, T, ch), lambda b: (b, 0, col_b)),
                  pl.BlockSpec(w.shape, lambda b: (0, 0)),
                  pl.BlockSpec(cb.shape, lambda b: (0, 0)),
                  pl.BlockSpec(lg.shape, lambda b: (0, 0)),
                  pl.BlockSpec(lb.shape, lambda b: (0, 0))],
        out_specs=pl.BlockSpec((1, T, ch), lambda b: (b, 0, 0)),
        scratch_shapes=[pltpu.VMEM((T + 32, ch), F32)],
        compiler_params=_cparams(("arbitrary",)),
        name="conformer_conv",
    )(proj, proj, w, cb, lg, lb)


def _rwkv_kernel(x_ref, mu_ref, w0_ref, w2_ref, a0_ref, a2_ref, g2_ref, kk_ref, ka_ref, rk_ref,
                 lng_ref, lnb_ref, o_ref, s_ref, prev_ref, gate_ref, bonus_ref, dec_ref, qeff_ref, yloc_ref,
                 mlow_ref, nc_ref, y_ref, *, n_chunk, unroll=False):
    C = RWKV_CHUNK
    H = RWKV_HEADS
    W = o_ref.shape[2]
    N = W // H

    @pl.when(pl.program_id(1) == 0)
    def _():
        s_ref[...] = jnp.zeros_like(s_ref)
        prev_ref[...] = jnp.zeros_like(prev_ref)

    lane = lax.broadcasted_iota(jnp.int32, (1, W), 1)
    head_mask = [(lane // N == h).astype(F32) for h in range(H)]
    block_diag = _group_matrix(W, N)
    head_sum = block_diag.astype(F32)
    ti = lax.broadcasted_iota(jnp.int32, (C, C), 0)
    tj = lax.broadcasted_iota(jnp.int32, (C, C), 1)
    tril_incl = ti >= tj
    tril_strict = ti > tj
    same_sub = (ti // RWKV_SUB) == (tj // RWKV_SUB)
    eye = (ti == tj).astype(F32)
    nc = n_chunk
    tb = nc * C

    def head_total(t, two_term=False):
        ones = head_sum.astype(BF16)
        if two_term:
            return jnp.dot(_hi_lo(t), jnp.concatenate([ones, ones], axis=0), preferred_element_type=F32)
        return jnp.dot(t.astype(BF16), ones, preferred_element_type=F32)

    x = x_ref[0]
    first_row = lax.broadcasted_iota(jnp.int32, (tb, 1), 0) == 0
    prev = jnp.where(first_row, prev_ref[...], pltpu.roll(x, 1, 0))
    prev_ref[...] = x[tb - 1:tb, :]
    xm = x + (prev - x) * mu_ref[...]
    r = xm[:, 0:W]
    k = xm[:, W:2 * W]
    v = xm[:, 2 * W:3 * W]
    zz = xm[:, 3 * W:3 * W + LANES]
    zg = xm[:, 3 * W + LANES:]
    w = -_softplus(-(w0_ref[...] + _mm(jnp.tanh(zz), w2_ref[...]))) - 0.5
    lw = -jnp.exp(w)
    a = _sigmoid(a0_ref[...] + _mm(zz, a2_ref[...]))
    gate_ref[...] = _mm(_sigmoid(zg), g2_ref[...])
    kk = k * kk_ref[...]
    kk = kk / jnp.maximum(jnp.sqrt(head_total(kk * kk, two_term=True)), 1e-12)
    k = k * (1.0 + (a - 1.0) * ka_ref[...])
    b = kk * a
    bonus_ref[...] = head_total(r * k * rk_ref[...]) * v

    c3 = lambda t: t.reshape(nc, C, t.shape[1])
    sums = _bmm(jnp.broadcast_to(tril_incl.astype(BF16)[None], (nc, C, C)), c3(_hi_lo(lw)))
    G3 = sums[:, :, 0:W] + sums[:, :, W:2 * W]
    total3 = jnp.broadcast_to(G3[:, C - 1:C, :], (nc, C, W))
    dec_ref[...] = jnp.exp(total3).reshape(tb, W)
    G = G3.reshape(tb, W)
    inv = jnp.exp(-G)
    to_end = jnp.exp(total3 - G3).reshape(tb, W)
    kap = kk * jnp.exp(G - lw)
    rho = r * jnp.exp(G)
    kap3, rho3, v3 = c3(kap), c3(rho), c3(v)
    bet_kt3 = jnp.concatenate([c3(b * inv), c3(k * inv)], axis=1)
    betc3, kc3 = c3(b * to_end), c3(k * to_end)

    def bmm_nt(p, q):
        return lax.dot_general(p.astype(BF16), q.astype(BF16), (((2,), (2,)), ((0,), (0,))),
                               preferred_element_type=F32)

    def bmm_tn(p, q):
        return lax.dot_general(p.astype(BF16), q.astype(BF16), (((1,), (1,)), ((0,), (0,))),
                               preferred_element_type=F32)

    wi = lax.broadcasted_iota(jnp.int32, (C, 2 * C), 0)
    wj = lax.broadcasted_iota(jnp.int32, (C, 2 * C), 1) % C
    A_bk, B_bk = [], []
    for h in range(H):
        lhs = jnp.concatenate([c3(kap * head_mask[h]), c3(rho * head_mask[h])], axis=1)
        prod = bmm_nt(lhs, bet_kt3)
        A_bk.append(jnp.where(wj < wi, prod[:, 0:C], 0.0))
        B_bk.append(jnp.where(wj <= wi, prod[:, C:2 * C], 0.0))
    A_b = jnp.concatenate([t[:, :, 0:C] for t in A_bk], axis=0)
    B_b = [t[:, :, 0:C] for t in B_bk]
    v3_low = jnp.concatenate([jnp.zeros_like(v3), v3], axis=1)

    Dg = jnp.where(same_sub, A_b, 0.0)
    Lo = A_b - Dg
    D2 = _bmm(Dg, Dg)
    D4 = _bmm(D2, D2)
    D8 = _bmm(D4, D4)
    Dinv = _bmm(_bmm(_bmm(eye - Dg, eye + D2), eye + D4), eye + D8)
    Nn = _bmm(Dinv, Lo)
    N2 = _bmm(Nn, Nn)
    Tm = _bmm(_bmm(eye - Nn, eye + N2), Dinv)
    Tm = [Tm[h * nc:(h + 1) * nc] for h in range(H)]

    def per_head(mats, t):
        reps = t.shape[2] // W
        acc = None
        for h in range(H):
            m = head_mask[h] if reps == 1 else jnp.concatenate([head_mask[h]] * reps, axis=1)
            u = m * _bmm(mats[h], t)
            acc = u if acc is None else acc + u
        return acc

    akv = per_head(A_bk, v3_low)
    tk = per_head(Tm, jnp.concatenate([kap3, akv], axis=2))
    kap_p = tk[:, :, 0:W]
    v_p = tk[:, :, W:2 * W]
    bb = per_head(B_b, jnp.concatenate([kap_p, v_p], axis=2))
    qeff_ref[...] = (rho3 - bb[:, :, 0:W]).astype(BF16)
    yloc_ref[...] = per_head(B_bk, v3_low) - bb[:, :, W:2 * W]
    mlow_ref[...] = jnp.where(block_diag, bmm_tn(kap_p, betc3), 0.0).astype(BF16)
    nc_ref[...] = jnp.where(block_diag, bmm_tn(jnp.concatenate([v3, v_p], axis=1),
                                               jnp.concatenate([kc3, -betc3], axis=1)), 0.0)

    def chunk(ci, carry):
        r0 = pl.multiple_of(ci * C, C)
        s = s_ref[...]
        sb = s.astype(BF16)
        y = lax.dot_general(qeff_ref[ci], sb, (((1,), (1,)), ((), ())), preferred_element_type=F32)
        y_ref[pl.ds(r0, C), :] = y + yloc_ref[ci]
        s_ref[...] = (s * dec_ref[pl.ds(r0, 8), :][0:1, :]
                      - jnp.dot(sb, mlow_ref[ci], preferred_element_type=F32) + nc_ref[ci])
        return carry

    lax.fori_loop(0, nc, chunk, 0, unroll=unroll)

    y = y_ref[...]
    mean = head_total(y, two_term=True) / N
    yc = y - mean
    var = head_total(yc * yc) / N
    yn = yc * lax.rsqrt(var + RWKV_GN_EPS) * lng_ref[...] + lnb_ref[...]
    o_ref[0] = ((yn + bonus_ref[...]) * gate_ref[...]).astype(o_ref.dtype)


N_GLA_INPUTS = 7
N_GLA_SCRATCH = 6
N_RWKV_INPUTS = 12


def _gla_rwkv_kernel(*refs, gla_chunks, rwkv_chunks):
    n_in = N_GLA_INPUTS + N_RWKV_INPUTS
    gla_in, rwkv_in = refs[0:N_GLA_INPUTS], refs[N_GLA_INPUTS:n_in]
    gla_out, rwkv_out = refs[n_in], refs[n_in + 1]
    gla_scratch = refs[n_in + 2:n_in + 2 + N_GLA_SCRATCH]
    rwkv_scratch = refs[n_in + 2 + N_GLA_SCRATCH:]
    _gla_kernel(*gla_in, gla_out, *gla_scratch, n_chunk=gla_chunks, unroll=True)
    _rwkv_kernel(*rwkv_in, rwkv_out, *rwkv_scratch, n_chunk=rwkv_chunks, unroll=True)


def _gla_rwkv7(proj, gla_w2p, gla_gb, gla_ng, col_qk, col_v, col_og, col_gz, vecs, w2p, a2p, g2, col_rwkv, tb):
    B, T, _ = proj.shape
    hv = gla_ng.shape[1]
    hk = gla_gb.shape[1]
    W = g2.shape[1]
    cols = vecs["mu"].shape[1]
    names = ("mu", "w0", "w2", "a0", "a2", "g2", "kk", "ka", "rk", "lng", "lnb")
    params = dict(vecs, w2=w2p, a2=a2p, g2=g2)
    assert len(names) + 1 == N_RWKV_INPUTS
    const = lambda arr: pl.BlockSpec(arr.shape, lambda b, j: (0, 0))
    rows = lambda n, col: pl.BlockSpec((1, tb, n), lambda b, j: (b, j, col))
    n_chunk = tb // RWKV_CHUNK
    return pl.pallas_call(
        functools.partial(_gla_rwkv_kernel, gla_chunks=tb // GLA_CHUNK, rwkv_chunks=n_chunk),
        out_shape=(jax.ShapeDtypeStruct((B, T, hv), BF16), jax.ShapeDtypeStruct((B, T, W), BF16)),
        grid=(B, T // tb),
        in_specs=[rows(2 * hk, col_qk), rows(hv, col_v), rows(hv, col_og), rows(LANES, col_gz),
                  const(gla_w2p), const(gla_gb), const(gla_ng), rows(cols, col_rwkv)]
                 + [const(params[n]) for n in names],
        out_specs=(rows(hv, 0), rows(W, 0)),
        scratch_shapes=[pltpu.VMEM((hv, hk), F32), pltpu.VMEM((tb, hk), F32), pltpu.VMEM((tb, hk), F32),
                        pltpu.VMEM((tb, hk), BF16), pltpu.VMEM((tb, hk), BF16), pltpu.VMEM((tb, hv), F32),
                        pltpu.VMEM((W, W), F32), pltpu.VMEM((1, cols), F32),
                        pltpu.VMEM((tb, W), F32), pltpu.VMEM((tb, W), F32), pltpu.VMEM((tb, W), F32),
                        pltpu.VMEM((n_chunk, RWKV_CHUNK, W), BF16), pltpu.VMEM((n_chunk, RWKV_CHUNK, W), F32),
                        pltpu.VMEM((n_chunk, W, W), BF16), pltpu.VMEM((n_chunk, W, W), F32),
                        pltpu.VMEM((tb, W), F32)],
        compiler_params=_cparams(("arbitrary", "arbitrary")),
        name="gla_rwkv7_mixers",
    )(proj, proj, proj, proj, gla_w2p, gla_gb, gla_ng, proj, *[params[n] for n in names])


def _merge_kernel(x_ref, sc_ref, sh_ref, gt_ref, gpre_ref, gpost_ref, wg_ref, b0_ref, b1_ref, b2_ref,
                  b3_ref, wb_ref, wo_ref, o_ref):
    x = x_ref[0]
    D = x.shape[1]
    h = _rms_mod(x, gpre_ref[...], sc_ref[0], sh_ref[0]).astype(BF16)
    merged = None
    for g, br in enumerate((b0_ref, b1_ref, b2_ref, b3_ref)):
        gate = _sigmoid(jnp.dot(h, wg_ref[:, g * D:(g + 1) * D], preferred_element_type=F32))
        t = gate * jnp.dot(br[0], wb_ref[g], preferred_element_type=F32)
        merged = t if merged is None else merged + t
    y = _mm(merged, wo_ref[...])
    y = y * lax.rsqrt(jnp.mean(y * y, axis=-1, keepdims=True) + RMS_EPS) * gpost_ref[...]
    o_ref[0] = x + gt_ref[0] * y


def _merge(x, sc, sh, gt, gpre, gpost, w_gate, branches, w_branch, w_out, tm):
    B, T, D = x.shape
    bw = branches[0].shape[2]
    tok = lambda n: pl.BlockSpec((1, tm, n), lambda b, i: (b, i, 0))
    vec = pl.BlockSpec((1, 1, D), lambda b, i: (b, 0, 0))
    const2 = lambda arr: pl.BlockSpec(arr.shape, lambda b, i: (0, 0))
    return pl.pallas_call(
        _merge_kernel,
        out_shape=jax.ShapeDtypeStruct((B, T, D), F32),
        grid=(B, T // tm),
        in_specs=[tok(D), vec, vec, vec, const2(gpre), const2(gpost), const2(w_gate),
                  tok(bw), tok(bw), tok(bw), tok(bw),
                  pl.BlockSpec(w_branch.shape, lambda b, i: (0, 0, 0)), const2(w_out)],
        out_specs=tok(D),
        compiler_params=_cparams(("arbitrary", "arbitrary")),
        name="merge_out_proj",
    )(x, sc, sh, gt, gpre, gpost, w_gate, *branches, w_branch, w_out)


def _ffn_kernel(x_ref, sc_ref, sh_ref, gt_ref, gpre_ref, gpost_ref, wg_ref, wu_ref, wd_ref, o_ref, *, tf):
    x = x_ref[0]
    h = _rms_mod(x, gpre_ref[...], sc_ref[0], sh_ref[0]).astype(BF16)
    F = wg_ref.shape[1]
    acc = None
    for f0 in range(0, F, tf):
        f1 = min(f0 + tf, F)
        gate = jnp.dot(h, wg_ref[:, f0:f1], preferred_element_type=F32)
        up = jnp.dot(h, wu_ref[:, f0:f1], preferred_element_type=F32)
        t = jnp.dot((_silu(gate) * up).astype(BF16), wd_ref[f0:f1, :], preferred_element_type=F32)
        acc = t if acc is None else acc + t
    y = acc * lax.rsqrt(jnp.mean(acc * acc, axis=-1, keepdims=True) + RMS_EPS) * gpost_ref[...]
    o_ref[0] = x + gt_ref[0] * y


def _dense_ffn(x, sc, sh, gt, gpre, gpost, wg, wu, wd, tm):
    B, T, D = x.shape
    tok = pl.BlockSpec((1, tm, D), lambda b, i: (b, i, 0))
    vec = pl.BlockSpec((1, 1, D), lambda b, i: (b, 0, 0))
    const2 = lambda arr: pl.BlockSpec(arr.shape, lambda b, i: (0, 0))
    return pl.pallas_call(
        functools.partial(_ffn_kernel, tf=FFN_CHUNK),
        out_shape=jax.ShapeDtypeStruct((B, T, D), F32),
        grid=(B, T // tm),
        in_specs=[tok, vec, vec, vec, const2(gpre), const2(gpost), const2(wg), const2(wu), const2(wd)],
        out_specs=tok,
        compiler_params=_cparams(("arbitrary", "arbitrary")),
        name="dense_swiglu",
    )(x, sc, sh, gt, gpre, gpost, wg, wu, wd)


MOE_TOKEN_TILE = 256
MOE_ROW_TILE = 512
SEG_ALIGN = 16
SEG_PIECES = (256, 128, 64, 32, 16)
MOE_SMALL_SEG = 128


def _route_kernel(x_ref, sc_ref, sh_ref, gpre_ref, rw_ref, rb_ref, h_ref, mi_ref, mp_ref, cnt_ref):
    tm = x_ref.shape[1]
    h = _rms_mod(x_ref[0], gpre_ref[...], sc_ref[0], sh_ref[0])
    h_ref[...] = h.astype(BF16)
    h_hi = h.astype(BF16)
    h_lo = (h - h_hi.astype(F32)).astype(BF16)
    w = rw_ref[...]
    w_hi = w.astype(BF16)
    w_lo = (w - w_hi.astype(F32)).astype(BF16)
    logits = jnp.dot(jnp.concatenate([h_hi, h_lo, h_hi], axis=1), jnp.concatenate([w_hi, w_hi, w_lo], axis=0),
                     preferred_element_type=F32) + rb_ref[...]
    lane = lax.broadcasted_iota(jnp.int32, logits.shape, 1)
    v1 = jnp.max(logits, axis=-1, keepdims=True)
    i1 = jnp.min(jnp.where(logits == v1, lane, LANES), axis=-1, keepdims=True)
    rest = jnp.where(lane == i1, -jnp.inf, logits)
    v2 = jnp.max(rest, axis=-1, keepdims=True)
    i2 = jnp.min(jnp.where(rest == v2, lane, LANES), axis=-1, keepdims=True)
    e2 = jnp.exp(v2 - v1)
    p1 = 1.0 / (1.0 + e2)
    p2 = e2 / (1.0 + e2)
    oh1 = (lane == i1).astype(F32)
    oh2 = (lane == i2).astype(F32)
    both = oh1 + oh2
    earlier = (lax.broadcasted_iota(jnp.int32, (tm, tm), 1)
               < lax.broadcasted_iota(jnp.int32, (tm, tm), 0)).astype(BF16)
    before = jnp.dot(earlier, both.astype(BF16), preferred_element_type=F32)
    r1 = jnp.sum(oh1 * before, axis=-1, keepdims=True).astype(jnp.int32)
    r2 = jnp.sum(oh2 * before, axis=-1, keepdims=True).astype(jnp.int32)
    col = lax.broadcasted_iota(jnp.int32, mi_ref.shape, 1)
    mi_ref[...] = jnp.where(col == 0, i1, jnp.where(col == 1, i2, jnp.where(col == 2, r1,
                                                                           jnp.where(col == 3, r2, 0))))
    mp_ref[...] = jnp.where(col == 0, p1, jnp.where(col == 1, p2, 0.0))
    cnt_ref[0] = jnp.sum(both, axis=0, keepdims=True).astype(jnp.int32)


def _segment_pieces(n_rows):
    out = []
    for s in SEG_PIECES:
        if s == MOE_TOKEN_TILE:
            out.append((n_rows == s, 0, s))
        else:
            out.append(((n_rows & s) != 0, pl.multiple_of((n_rows // (2 * s)) * (2 * s), SEG_ALIGN), s))
    return out


def _all_segments_small(cnt_ref, tile):
    most = cnt_ref[tile * N_EXPERTS]
    for e in range(1, N_EXPERTS):
        most = jnp.maximum(most, cnt_ref[tile * N_EXPERTS + e])
    return most <= MOE_SMALL_SEG


def _dispatch_kernel(seg_ref, cnt_ref, h_ref, mit_ref, init_ref, xs_ref, buf_ref, sem):
    del init_ref
    tm = h_ref.shape[0]
    i = pl.program_id(0)
    e1, e2 = mit_ref[0:1, :], mit_ref[1:2, :]
    r1, r2 = mit_ref[2:3, :], mit_ref[3:4, :]
    slot = i % 2

    def compact(cap):
        row = lax.broadcasted_iota(jnp.int32, (cap, tm), 0)
        select = jnp.concatenate(
            [jnp.logical_or(jnp.logical_and(e1 == e, r1 == row), jnp.logical_and(e2 == e, r2 == row))
             for e in range(N_EXPERTS)], axis=0).astype(BF16)
        rows = jnp.dot(select, h_ref[...], preferred_element_type=F32).astype(BF16)
        for e in range(N_EXPERTS):
            buf_ref[slot, e * tm:e * tm + cap, :] = rows[e * cap:(e + 1) * cap]

    small = _all_segments_small(cnt_ref, i)
    pl.when(small)(functools.partial(compact, MOE_SMALL_SEG))
    pl.when(jnp.logical_not(small))(functools.partial(compact, tm))

    def segment_copies(tile, buf, e):
        n = cnt_ref[tile * N_EXPERTS + e]
        n_rows = ((n + SEG_ALIGN - 1) // SEG_ALIGN) * SEG_ALIGN
        dst = pl.multiple_of(seg_ref[tile * N_EXPERTS + e], SEG_ALIGN)
        return [(cond, pltpu.make_async_copy(buf_ref.at[buf, pl.ds(e * tm + off, s), :],
                                             xs_ref.at[pl.ds(dst + off, s), :], sem))
                for cond, off, s in _segment_pieces(n_rows)]

    def for_all_segments(tile, buf, action):
        for e in range(N_EXPERTS):
            for cond, cp in segment_copies(tile, buf, e):
                pl.when(cond)(getattr(cp, action))

    @pl.when(i > 0)
    def _():
        for_all_segments(i - 1, 1 - slot, "wait")

    for_all_segments(i, slot, "start")

    @pl.when(i == pl.num_programs(0) - 1)
    def _():
        for_all_segments(i, slot, "wait")


def _expert_kernel(te_ref, nv_ref, xs_ref, wg_ref, wu_ref, wd_ref, ys_ref, acc_ref):
    del te_ref
    r = pl.program_id(0)
    f = pl.program_id(1)
    valid = r < nv_ref[0]

    @pl.when(jnp.logical_and(valid, f == 0))
    def _():
        acc_ref[...] = jnp.zeros_like(acc_ref)

    @pl.when(valid)
    def _():
        x = xs_ref[...]
        tf = wg_ref.shape[2]
        acc = acc_ref[...]
        for f0 in range(0, tf, FFN_CHUNK):
            f1 = min(f0 + FFN_CHUNK, tf)
            gate = jnp.dot(x, wg_ref[0, :, f0:f1], preferred_element_type=F32)
            up = jnp.dot(x, wu_ref[0, :, f0:f1], preferred_element_type=F32)
            acc = acc + jnp.dot((_silu(gate) * up).astype(BF16), wd_ref[0, f0:f1, :],
                                preferred_element_type=F32)
        acc_ref[...] = acc

    @pl.when(f == pl.num_programs(1) - 1)
    def _():
        @pl.when(valid)
        def _():
            ys_ref[...] = acc_ref[...].astype(ys_ref.dtype)

        @pl.when(jnp.logical_not(valid))
        def _():
            ys_ref[...] = jnp.zeros_like(ys_ref)


def _combine_kernel(seg_ref, cnt_ref, ys_ref, mi_ref, mp_ref, x_ref, gt_ref, gpost_ref, o_ref, win_ref,
                    pair_ref, sem):
    tm = x_ref.shape[1]
    i = pl.program_id(0) * pl.num_programs(1) + pl.program_id(1)
    n_tiles = pl.num_programs(0) * pl.num_programs(1)
    slot = i % 2

    def window_copy(tile, buf, e, rows):
        src = pl.multiple_of(seg_ref[tile * N_EXPERTS + e], SEG_ALIGN)
        return pltpu.make_async_copy(ys_ref.at[pl.ds(src, rows), :], win_ref.at[buf, pl.ds(e * rows, rows), :],
                                     sem.at[buf, e])

    def for_all_windows(tile, buf, action):
        small = _all_segments_small(cnt_ref, tile)
        for rows, cond in ((MOE_SMALL_SEG, small), (tm, jnp.logical_not(small))):
            @pl.when(cond)
            def _():
                for e in range(N_EXPERTS):
                    getattr(window_copy(tile, buf, e, rows), action)()

    @pl.when(i == 0)
    def _():
        for_all_windows(i, slot, "start")

    @pl.when(i + 1 < n_tiles)
    def _():
        for_all_windows(i + 1, 1 - slot, "start")

    for_all_windows(i, slot, "wait")
    e1, e2 = mi_ref[:, 0:1], mi_ref[:, 1:2]
    r1, r2 = mi_ref[:, 2:3], mi_ref[:, 3:4]

    def expand_rows(cap):
        col = lax.broadcasted_iota(jnp.int32, (tm, cap), 1)
        expand = jnp.concatenate(
            [jnp.concatenate([jnp.logical_and(e1 == e, r1 == col), jnp.logical_and(e2 == e, r2 == col)], axis=0)
             for e in range(N_EXPERTS)], axis=1).astype(BF16)
        pair_ref[...] = jnp.dot(expand, win_ref[slot, 0:N_EXPERTS * cap, :], preferred_element_type=F32)

    small = _all_segments_small(cnt_ref, i)
    pl.when(small)(functools.partial(expand_rows, MOE_SMALL_SEG))
    pl.when(jnp.logical_not(small))(functools.partial(expand_rows, tm))
    y = mp_ref[:, 0:1] * pair_ref[0:tm, :] + mp_ref[:, 1:2] * pair_ref[tm:2 * tm, :]
    y = y * lax.rsqrt(jnp.mean(y * y, axis=-1, keepdims=True) + RMS_EPS) * gpost_ref[...]
    o_ref[0] = x_ref[0] + gt_ref[0] * y


def _moe_ffn(x, sc, sh, gt, gpre, gpost, rw, rb, wg, wu, wd, tf):
    B, T, D = x.shape
    E, _, F = wg.shape
    tm = min(MOE_TOKEN_TILE, T)
    assert tm == MOE_TOKEN_TILE and E == N_EXPERTS
    nT = T // tm
    n_tok_tiles = B * nT
    N = B * T
    max_rows = 2 * N + n_tok_tiles * E * (SEG_ALIGN - 1) + E * (MOE_ROW_TILE - SEG_ALIGN)
    n_row_tiles = -(-max_rows // MOE_ROW_TILE) + 1
    P = n_row_tiles * MOE_ROW_TILE

    vec = pl.BlockSpec((1, 1, D), lambda b, j: (b, 0, 0))
    const2 = lambda arr: pl.BlockSpec(arr.shape, lambda b, j: (0, 0))
    flat = lambda n: pl.BlockSpec((tm, n), lambda b, j: (b * nT + j, 0))
    h, mi, mp, cnt = pl.pallas_call(
        _route_kernel,
        out_shape=(jax.ShapeDtypeStruct((N, D), BF16), jax.ShapeDtypeStruct((N, 8), jnp.int32),
                   jax.ShapeDtypeStruct((N, 8), F32), jax.ShapeDtypeStruct((n_tok_tiles, 1, LANES), jnp.int32)),
        grid=(B, nT),
        in_specs=[pl.BlockSpec((1, tm, D), lambda b, j: (b, j, 0)), vec, vec, const2(gpre), const2(rw),
                  const2(rb)],
        out_specs=(flat(D), flat(8), flat(8), pl.BlockSpec((1, 1, LANES), lambda b, j: (b * nT + j, 0, 0))),
        compiler_params=_cparams(("arbitrary", "arbitrary")),
        name="moe_route",
    )(x, sc, sh, gpre, rw, rb)

    counts = cnt[:, 0, :E]
    seg_len = (counts + SEG_ALIGN - 1) // SEG_ALIGN * SEG_ALIGN
    group_len = (jnp.sum(seg_len, axis=0) + MOE_ROW_TILE - 1) // MOE_ROW_TILE * MOE_ROW_TILE
    group_end = jnp.cumsum(group_len)
    seg_start = (group_end - group_len)[None, :] + jnp.cumsum(seg_len, axis=0) - seg_len
    seg_start = seg_start.reshape(-1).astype(jnp.int32)
    counts = counts.reshape(-1)
    n_valid = (group_end[-1:] // MOE_ROW_TILE).astype(jnp.int32)
    tile_first_row = jnp.arange(n_row_tiles, dtype=jnp.int32) * MOE_ROW_TILE
    tile_expert = jnp.minimum(jnp.sum(tile_first_row[:, None] >= group_end[None, :], axis=1), E - 1)
    tile_expert = tile_expert.astype(jnp.int32)

    xs = pl.pallas_call(
        _dispatch_kernel,
        out_shape=jax.ShapeDtypeStruct((P, D), BF16),
        grid_spec=pltpu.PrefetchScalarGridSpec(
            num_scalar_prefetch=2,
            grid=(n_tok_tiles,),
            in_specs=[pl.BlockSpec((tm, D), lambda i, seg, n: (i, 0)),
                      pl.BlockSpec((8, tm), lambda i, seg, n: (0, i)),
                      pl.BlockSpec(memory_space=pl.ANY)],
            out_specs=pl.BlockSpec(memory_space=pl.ANY),
            scratch_shapes=[pltpu.VMEM((2, E * tm, D), BF16), pltpu.SemaphoreType.DMA(())]),
        input_output_aliases={4: 0},
        compiler_params=_cparams(("arbitrary",)),
        name="moe_dispatch",
    )(seg_start, counts, h, mi.T, jnp.zeros((P, D), BF16))

    nf = F // tf
    live = lambda r, f, nv: jnp.where(r < nv[0], f, nf - 1)
    ys = pl.pallas_call(
        _expert_kernel,
        out_shape=jax.ShapeDtypeStruct((P, D), BF16),
        grid_spec=pltpu.PrefetchScalarGridSpec(
            num_scalar_prefetch=2,
            grid=(n_row_tiles, nf),
            in_specs=[pl.BlockSpec((MOE_ROW_TILE, D), lambda r, f, te, nv: (r, 0)),
                      pl.BlockSpec((1, D, tf), lambda r, f, te, nv: (te[r], 0, live(r, f, nv))),
                      pl.BlockSpec((1, D, tf), lambda r, f, te, nv: (te[r], 0, live(r, f, nv))),
                      pl.BlockSpec((1, tf, D), lambda r, f, te, nv: (te[r], live(r, f, nv), 0))],
            out_specs=pl.BlockSpec((MOE_ROW_TILE, D), lambda r, f, te, nv: (r, 0)),
            scratch_shapes=[pltpu.VMEM((MOE_ROW_TILE, D), F32)]),
        compiler_params=_cparams(("arbitrary", "arbitrary")),
        name="moe_experts",
    )(tile_expert, n_valid, xs, wg, wu, wd)

    return pl.pallas_call(
        _combine_kernel,
        out_shape=jax.ShapeDtypeStruct((B, T, D), F32),
        grid_spec=pltpu.PrefetchScalarGridSpec(
            num_scalar_prefetch=2,
            grid=(B, nT),
            in_specs=[pl.BlockSpec(memory_space=pl.ANY),
                      pl.BlockSpec((tm, 8), lambda b, j, seg, n: (b * nT + j, 0)),
                      pl.BlockSpec((tm, 8), lambda b, j, seg, n: (b * nT + j, 0)),
                      pl.BlockSpec((1, tm, D), lambda b, j, seg, n: (b, j, 0)),
                      pl.BlockSpec((1, 1, D), lambda b, j, seg, n: (b, 0, 0)),
                      pl.BlockSpec(gpost.shape, lambda b, j, seg, n: (0, 0))],
            out_specs=pl.BlockSpec((1, tm, D), lambda b, j, seg, n: (b, j, 0)),
            scratch_shapes=[pltpu.VMEM((2, E * tm, D), BF16), pltpu.VMEM((2 * tm, D), F32),
                            pltpu.SemaphoreType.DMA((2, E))]),
        compiler_params=_cparams(("arbitrary", "arbitrary")),
        name="moe_combine",
    )(seg_start, counts, ys, mi, mp, x, gt, gpost)


def _rope_tables(positions, groups):
    d = 32
    inv = 1.0 / (ROPE_THETA ** (jnp.arange(0, d, 2, dtype=F32) / d))
    ang = positions.astype(F32)[..., None] * inv
    cos, sin = jnp.cos(ang), jnp.sin(ang)
    cos = jnp.tile(jnp.concatenate([cos, cos], axis=-1), (1, 1, groups))
    sin = jnp.tile(jnp.concatenate([-sin, sin], axis=-1), (1, 1, groups))
    return cos, sin


def _pad_rows(w, rows, offset):
    out = jnp.zeros((rows, w.shape[1]), w.dtype)
    return out.at[offset:offset + w.shape[0]].set(w)


def kernel(x, c, positions, ada_w, ada_b, norm_mix_pre, norm_mix_post, norm_ffn_pre, norm_ffn_post, w_in, gla_gate_w2, gla_gate_b, gla_norm, diff_lambda, diff_subln, conv_w, conv_b, conv_ln_g, conv_ln_b, rwkv_mu, rwkv_w0, rwkv_w2, rwkv_a0, rwkv_a2, rwkv_g2, rwkv_k_k, rwkv_k_a, rwkv_r_k, rwkv_ln_g, rwkv_ln_b, w_branch, w_out, ffn_w_gate, ffn_w_up, ffn_w_down, router_w, router_b, moe_w_gate, moe_w_up, moe_w_down):
    B, T, D = x.shape
    L = ada_w.shape[0]
    W = D // N_BRANCH
    hk = gla_gate_b.shape[1]
    decay_rank = rwkv_w2.shape[1]
    a_rank = rwkv_a2.shape[1]
    gate_rank = rwkv_g2.shape[1]
    assert decay_rank + a_rank == LANES and 2 * hk == W and gate_rank == LANES
    n_mix = 3 * W + 3 * W + 2 * W + (3 * W + decay_rank + a_rank + gate_rank)
    sizes = (hk, hk, W, W, GLA_GATE_RANK, W, W, W, W, W, 3 * W + LANES + gate_rank, N_BRANCH * D)
    offs = [0]
    for s in sizes:
        offs.append(offs[-1] + s)
    assert offs[-1] == w_in.shape[2]
    tm = min(TOKEN_TILE, T)
    tb = min(SEQ_BLOCK, T)

    mod = _modulation(c, ada_w, ada_b)
    cos, sin = _rope_tables(positions, LANES // 32)

    for l in range(L):
        m = mod[l].reshape(B, 1, 6 * D)
        sh_m, sc_m, gt_m, sh_f, sc_f, gt_f = [m[:, :, i * D:(i + 1) * D] for i in range(6)]

        wl = w_in[l]
        gz_cols = jnp.zeros((D, LANES), F32).at[:, :GLA_GATE_RANK].set(wl[:, offs[4]:offs[5]])
        w_mix = jnp.concatenate([wl[:, offs[0]:offs[4]], wl[:, offs[5]:offs[11]], gz_cols], axis=1)
        w_mix = w_mix.astype(BF16)
        w_gate = wl[:, offs[11]:offs[12]].astype(BF16)
        proj = _in_projection(x, sc_m, sh_m, norm_mix_pre[l][None], w_mix, tm)

        o_diff = _diff_attention(proj, cos, sin, diff_lambda[l],
                                 diff_subln[l][None], col_q=3, col_k=4, col_v=5, layer_idx=l)
        o_conv = _conformer_conv(proj, _pad_rows(conv_w[l], 32, 0), conv_b[l][None],
                                 conv_ln_g[l][None], conv_ln_b[l][None], col_a=6, col_b=7)
        vecs = dict(mu=rwkv_mu[l][None], w0=rwkv_w0[l][None], a0=rwkv_a0[l][None],
                    kk=rwkv_k_k[l][None], ka=rwkv_k_a[l][None], rk=rwkv_r_k[l].reshape(1, W),
                    lng=rwkv_ln_g[l][None], lnb=rwkv_ln_b[l][None])
        o_gla, o_rwkv = _gla_rwkv7(
            proj, _pad_rows(gla_gate_w2[l], LANES, 0), gla_gate_b[l][None],
            jnp.tile(gla_norm[l], GLA_HEADS)[None], 0, 1, 2, n_mix // LANES,
            vecs, _pad_rows(rwkv_w2[l], LANES, 0).astype(BF16),
            _pad_rows(rwkv_a2[l], LANES, decay_rank).astype(BF16), rwkv_g2[l].astype(BF16), 2, tb)
        x = _merge(x, sc_m, sh_m, gt_m, norm_mix_pre[l][None], norm_mix_post[l][None], w_gate,
                   (o_gla, o_diff, o_conv, o_rwkv), w_branch[l].astype(BF16), w_out[l].astype(BF16), tm)

        i = l // 2
        if l % 2 == 0:
            x = _dense_ffn(x, sc_f, sh_f, gt_f, norm_ffn_pre[l][None], norm_ffn_post[l][None],
                           ffn_w_gate[i].astype(BF16), ffn_w_up[i].astype(BF16),
                           ffn_w_down[i].astype(BF16), tm)
        else:
            rw = jnp.zeros((D, LANES), F32).at[:, :N_EXPERTS].set(router_w[i])
            rb = jnp.full((1, LANES), -jnp.inf, F32).at[0, :N_EXPERTS].set(router_b[i])
            x = _moe_ffn(x, sc_f, sh_f, gt_f, norm_ffn_pre[l][None], norm_ffn_post[l][None], rw, rb,
                         moe_w_gate[i].astype(BF16), moe_w_up[i].astype(BF16),
                         moe_w_down[i].astype(BF16), tf=moe_w_gate.shape[3])
    return x
```

```python
import functools
import math

import jax
import jax.numpy as jnp
from jax import lax
from jax.experimental import pallas as pl
from jax.experimental.pallas import tpu as pltpu

F32 = jnp.float32
BF16 = jnp.bfloat16
HIGHEST = lax.Precision.HIGHEST

N_BRANCH = 4
GLA_HEADS = 4
GLA_GATE_RANK = 16
GLA_GATE_NORMALIZER = 16.0
GLA_CHUNK = 32
DIFF_HEADS = 4
ROPE_THETA = 10000.0
CONV_WIDTH = 31
RWKV_HEADS = 4
RWKV_CHUNK = 64
RWKV_SUB = 16
N_EXPERTS = 8
RMS_EPS = 1e-6
LN_EPS = 1e-5
RWKV_GN_EPS = 64e-5
LANES = 128
SUBLANES = 8
VMEM_LIMIT = 56 * 1024 * 1024

TOKEN_TILE = 512
SEQ_BLOCK = 512
ATTN_BLOCK = 512
CONV_ROWS = 128
FFN_CHUNK = 512


def _cparams(sem):
    return pltpu.CompilerParams(dimension_semantics=sem, vmem_limit_bytes=VMEM_LIMIT)


def _mm(a, b):
    return jnp.dot(a.astype(BF16), b.astype(BF16), preferred_element_type=F32)


def _mm_nt(a, b):
    return lax.dot_general(a.astype(BF16), b.astype(BF16), (((1,), (1,)), ((), ())),
                           preferred_element_type=F32)


def _mm_tn(a, b):
    return lax.dot_general(a.astype(BF16), b.astype(BF16), (((0,), (0,)), ((), ())),
                           preferred_element_type=F32)


def _bmm(a, b):
    return lax.dot_general(a.astype(BF16), b.astype(BF16), (((2,), (1,)), ((0,), (0,))),
                           preferred_element_type=F32)


def _mm_f32(a, b):
    return jnp.dot(a, b, precision=HIGHEST, preferred_element_type=F32)


def _hi_lo(x):
    hi = x.astype(BF16)
    return jnp.concatenate([hi, (x - hi.astype(F32)).astype(BF16)], axis=1)


def _sigmoid(x):
    return 0.5 * jnp.tanh(0.5 * x) + 0.5


def _silu(x):
    return x * _sigmoid(x)


def _softplus(x):
    return jnp.maximum(x, 0.0) + jnp.log(1.0 + jnp.exp(-jnp.abs(x)))


def _group_matrix(n, group):
    r = lax.broadcasted_iota(jnp.int32, (n, n), 0) // group
    c = lax.broadcasted_iota(jnp.int32, (n, n), 1) // group
    return r == c


def _rms_mod(x, gain, scale, shift):
    y = x * lax.rsqrt(jnp.mean(x * x, axis=-1, keepdims=True) + RMS_EPS)
    return y * gain * (1.0 + scale) + shift


def _mod_kernel(c_ref, w_ref, b_ref, o_ref):
    o_ref[0] = _mm_f32(_silu(c_ref[...]), w_ref[0]) + b_ref[0]


def _modulation(c, ada_w, ada_b):
    L, D, M = ada_w.shape
    B = c.shape[0]
    tn = M // 4
    return pl.pallas_call(
        _mod_kernel,
        out_shape=jax.ShapeDtypeStruct((L, B, M), F32),
        grid=(L, M // tn),
        in_specs=[pl.BlockSpec((B, D), lambda l, j: (0, 0)),
                  pl.BlockSpec((1, D, tn), lambda l, j: (l, 0, j)),
                  pl.BlockSpec((1, 1, tn), lambda l, j: (l, 0, j))],
        out_specs=pl.BlockSpec((1, B, tn), lambda l, j: (l, 0, j)),
        compiler_params=_cparams(("arbitrary", "arbitrary")),
        name="adaln_mod",
    )(c, ada_w, ada_b.reshape(L, 1, M))


CONV_HALO = 32


def _inproj_conv_kernel(x_ref, sc_ref, sh_ref, g_ref, wab_ref, w_ref, cw_ref, cb_ref, lg_ref, lb_ref,
                        o_ref, oc_ref, u_ref, *, rb):
    tm = x_ref.shape[1]
    ch = oc_ref.shape[2]
    h = _rms_mod(x_ref[0], g_ref[...], sc_ref[0], sh_ref[0]).astype(BF16)

    @pl.when(pl.program_id(1) == 0)
    def _():
        u_ref[0:CONV_HALO, :] = jnp.zeros((CONV_HALO, ch), F32)

    ab = jnp.dot(h, wab_ref[...], preferred_element_type=F32)
    u_ref[CONV_HALO:CONV_HALO + tm, :] = ab[:, 0:ch] * _sigmoid(ab[:, ch:2 * ch])
    o_ref[0] = jnp.dot(h, w_ref[...], preferred_element_type=F32)

    first = CONV_HALO - (CONV_WIDTH - 1)
    for i in range(tm // rb):
        r0 = i * rb
        win = u_ref[r0:r0 + rb + CONV_HALO, :]
        acc = jnp.zeros((rb, ch), F32)
        for s in range(SUBLANES):
            taps = [j for j in range(CONV_WIDTH) if (first + j) % SUBLANES == s]
            rolled = win if s == 0 else pltpu.roll(win, rb + CONV_HALO - s, 0)
            for j in taps:
                a0 = first + j - s
                acc = acc + cw_ref[j:j + 1, :] * rolled[a0:a0 + rb, :]
        y = acc + cb_ref[...]
        mu = jnp.mean(y, axis=-1, keepdims=True)
        yc = y - mu
        var = jnp.mean(yc * yc, axis=-1, keepdims=True)
        y = yc * lax.rsqrt(var + LN_EPS) * lg_ref[...] + lb_ref[...]
        oc_ref[0, r0:r0 + rb, :] = _silu(y).astype(oc_ref.dtype)
    u_ref[0:CONV_HALO, :] = u_ref[tm:tm + CONV_HALO, :]


def _in_projection_conv(x, sc, sh, gain, w_ab, w, conv_w, conv_b, ln_g, ln_b, tm):
    B, T, D = x.shape
    n = w.shape[1]
    ch = conv_w.shape[1]
    const = lambda arr: pl.BlockSpec(arr.shape, lambda b, i: (0, 0))
    return pl.pallas_call(
        functools.partial(_inproj_conv_kernel, rb=min(CONV_ROWS, tm)),
        out_shape=(jax.ShapeDtypeStruct((B, T, n), F32), jax.ShapeDtypeStruct((B, T, ch), BF16)),
        grid=(B, T // tm),
        in_specs=[pl.BlockSpec((1, tm, D), lambda b, i: (b, i, 0)),
                  pl.BlockSpec((1, 1, D), lambda b, i: (b, 0, 0)),
                  pl.BlockSpec((1, 1, D), lambda b, i: (b, 0, 0)),
                  const(gain), const(w_ab), const(w), const(conv_w), const(conv_b), const(ln_g), const(ln_b)],
        out_specs=(pl.BlockSpec((1, tm, n), lambda b, i: (b, i, 0)),
                   pl.BlockSpec((1, tm, ch), lambda b, i: (b, i, 0))),
        scratch_shapes=[pltpu.VMEM((tm + CONV_HALO, ch), F32)],
        compiler_params=_cparams(("arbitrary", "arbitrary")),
        name="in_proj_conv",
    )(x, sc, sh, gain, w_ab, w, conv_w, conv_b, ln_g, ln_b)


def _gla_kernel(qk_ref, v_ref, og_ref, gz_ref, w2_ref, gb_ref, ng_ref, o_ref, s_ref, g_ref, r_ref,
                qd_ref, kd_ref, oacc_ref, *, n_chunk, unroll=False):
    C = GLA_CHUNK
    hk = qk_ref.shape[2] // 2
    hv = v_ref.shape[2]
    dk = hk // GLA_HEADS
    dv = hv // GLA_HEADS

    tb = n_chunk * C

    @pl.when(pl.program_id(1) == 0)
    def _():
        s_ref[...] = jnp.zeros_like(s_ref)

    z = _mm(gz_ref[0], w2_ref[...]) + gb_ref[...]
    gk = (jnp.minimum(z, 0.0) - jnp.log(1.0 + jnp.exp(-jnp.abs(z)))) / GLA_GATE_NORMALIZER
    tri = (lax.broadcasted_iota(jnp.int32, (C, C), 1)
           <= lax.broadcasted_iota(jnp.int32, (C, C), 0)).astype(BF16)
    sums = _bmm(jnp.broadcast_to(tri[None], (n_chunk, C, C)), _hi_lo(gk).reshape(n_chunk, C, 2 * hk))
    G3 = sums[:, :, 0:hk] + sums[:, :, hk:2 * hk]
    G_all = G3.reshape(tb, hk)
    R_all = (jnp.broadcast_to(G3[:, C - 1:C, :], (n_chunk, C, hk)) - G3).reshape(tb, hk)
    g_ref[...] = G_all
    r_ref[...] = R_all
    qd_ref[...] = (qk_ref[0, :, 0:hk] * (dk ** -0.5) * jnp.exp(G_all)).astype(BF16)
    kd_ref[...] = (qk_ref[0, :, hk:2 * hk] * jnp.exp(R_all)).astype(BF16)

    causal = (lax.broadcasted_iota(jnp.int32, (C, C, hk), 1)
              <= lax.broadcasted_iota(jnp.int32, (C, C, hk), 0))
    er = lax.broadcasted_iota(jnp.int32, (hk, hv), 0) // dk
    ec = lax.broadcasted_iota(jnp.int32, (hk, hv), 1) // dv
    expand = (er == ec).astype(BF16)
    sr = lax.broadcasted_iota(jnp.int32, (hv, hk), 0) // dv
    scol = lax.broadcasted_iota(jnp.int32, (hv, hk), 1) // dk
    state_mask = sr == scol
    head_mean = (_group_matrix(hv, dv).astype(F32) / dv).astype(BF16)

    def chunk(ci, carry):
        r0 = pl.multiple_of(ci * C, C)
        rows = pl.ds(r0, C)
        q = qk_ref[0, rows, 0:hk] * (dk ** -0.5)
        k = qk_ref[0, rows, hk:2 * hk]
        v = v_ref[0, rows, :]
        G = g_ref[rows, :]
        pair = (C, C, hk)
        g_diff = jnp.broadcast_to(G[:, None, :], pair) - jnp.broadcast_to(G[None], pair)
        decay = jnp.exp(jnp.where(causal, g_diff, -jnp.inf))
        p = jnp.broadcast_to(q[:, None, :], pair) * jnp.broadcast_to(k[None], pair) * decay
        a_exp = jnp.dot(p.reshape(C * C, hk).astype(BF16), expand, preferred_element_type=F32)
        v_j = jnp.broadcast_to(v[None], (C, C, hv))
        o_intra = jnp.sum(a_exp.reshape(C, C, hv) * v_j, axis=1)
        g_total = G[0:1, :] + r_ref[pl.ds(r0, 8), :][0:1, :]
        s = s_ref[...]
        o_inter = lax.dot_general(qd_ref[rows, :], s.astype(BF16), (((1,), (1,)), ((), ())),
                                  preferred_element_type=F32)
        kv = lax.dot_general(v.astype(BF16), kd_ref[rows, :], (((0,), (0,)), ((), ())),
                             preferred_element_type=F32)
        s_ref[...] = s * jnp.exp(g_total) + jnp.where(state_mask, kv, 0.0)
        oacc_ref[rows, :] = o_intra + o_inter
        return carry

    lax.fori_loop(0, n_chunk, chunk, 0, unroll=unroll)

    o = oacc_ref[...]
    ms = jnp.dot(_hi_lo(o * o), jnp.concatenate([head_mean, head_mean], axis=0),
                 preferred_element_type=F32)
    o = o * lax.rsqrt(ms + RMS_EPS) * ng_ref[...] * _silu(og_ref[0])
    o_ref[0] = o.astype(o_ref.dtype)


def _rope(t, cos, sin_signed):
    d = 32
    half = d // 2
    out = []
    for s in range(t.shape[1] // LANES):
        x = t[:, s * LANES:(s + 1) * LANES]
        lane = lax.broadcasted_iota(jnp.int32, x.shape, 1)
        up = pltpu.roll(x, LANES - half, 1)
        down = pltpu.roll(x, half, 1)
        rot = jnp.where((lane % d) < half, up, down)
        out.append(x * cos + rot * sin_signed)
    return jnp.concatenate(out, axis=1)


def _diff_kernel(q_ref, k_ref, v_ref, cosq_ref, sinq_ref, cosk_ref, sink_ref, lam_ref, g_ref, o_ref,
                 ks, vs, *, tq, lam_init):
    H = DIFF_HEADS
    d = q_ref.shape[2] // (2 * H)
    dv = v_ref.shape[2] // H
    j = pl.program_id(1)

    @pl.when(j == 0)
    def _():
        k = _rope(k_ref[0], cosk_ref[0], sink_ref[0])
        v = v_ref[0]
        for hc in range(2 * H):
            ks[hc] = k[:, hc * d:(hc + 1) * d].astype(BF16)
        ones_col = (lax.broadcasted_iota(jnp.int32, (v.shape[0], dv), 1) == 0).astype(BF16)
        for h in range(H):
            vs[h] = jnp.concatenate([v[:, h * dv:(h + 1) * dv].astype(BF16), ones_col], axis=1)

    q = _rope(q_ref[0], cosq_ref[0], sinq_ref[0]) * (d ** -0.5 * math.log2(math.e))
    lp = lam_ref[...]
    lam = (jnp.exp(jnp.sum(lp[0:1] * lp[1:2], axis=-1, keepdims=True))
           - jnp.exp(jnp.sum(lp[2:3] * lp[3:4], axis=-1, keepdims=True)) + lam_init)
    on_or_below_diag = (lax.broadcasted_iota(jnp.int32, (tq, tq), 1)
                        <= lax.broadcasted_iota(jnp.int32, (tq, tq), 0))

    def update(qh, kh, vh, m, acc, masked):
        s = lax.dot_general(qh, kh, (((1,), (1,)), ((), ())), preferred_element_type=F32)
        if masked:
            s = jnp.where(on_or_below_diag, s, -jnp.inf)
        m_new = jnp.maximum(m, jnp.max(s, axis=-1, keepdims=True))
        p = jnp.exp2(s - m_new).astype(BF16)
        acc = jnp.exp2(m - m_new) * acc + jnp.dot(p, vh, preferred_element_type=F32)
        return m_new, acc

    qs = [q[:, hc * d:(hc + 1) * d].astype(BF16) for hc in range(2 * H)]

    def kv_block(kb, carry, masked):
        rows = pl.ds(pl.multiple_of(kb * tq, tq), tq)
        out = []
        for hc in range(2 * H):
            m, acc = carry[2 * hc], carry[2 * hc + 1]
            out.extend(update(qs[hc], ks[hc, rows, :], vs[hc // 2, rows, :], m, acc, masked))
        return tuple(out)

    m0 = jnp.full((tq, 1), -jnp.inf, F32)
    a0 = jnp.zeros((tq, 2 * dv), F32)
    carry = lax.fori_loop(0, j, lambda kb, c: kv_block(kb, c, False), (m0, a0) * (2 * H))
    carry = kv_block(j, carry, True)

    for h in range(H):
        a1, a2 = carry[4 * h + 1], carry[4 * h + 3]
        comp = [a[:, 0:dv] / a[:, dv:dv + 1] for a in (a1, a2)]
        o = comp[0] - lam * comp[1]
        o = o * lax.rsqrt(jnp.mean(o * o, axis=-1, keepdims=True) + RMS_EPS)
        o = o * g_ref[...] * (1.0 - lam_init)
        o_ref[0, :, h * dv:(h + 1) * dv] = o.astype(o_ref.dtype)


def _diff_attention(proj, cos, sin, lam_p, g, col_q, col_k, col_v, layer_idx):
    B, T, _ = proj.shape
    H = DIFF_HEADS
    dv = g.shape[1]
    d = dv // 2
    w = 2 * H * d
    assert cos.shape[2] == LANES
    tq = min(ATTN_BLOCK, T)
    lam_init = 0.8 - 0.6 * math.exp(-0.3 * layer_idx)
    blk = lambda col: pl.BlockSpec((1, tq, w), lambda b, j: (b, j, col))
    full = lambda col: pl.BlockSpec((1, T, w), lambda b, j: (b, 0, col))
    tab_blk = pl.BlockSpec((1, tq, LANES), lambda b, j: (b, j, 0))
    tab_full = pl.BlockSpec((1, T, LANES), lambda b, j: (b, 0, 0))
    return pl.pallas_call(
        functools.partial(_diff_kernel, tq=tq, lam_init=lam_init),
        out_shape=jax.ShapeDtypeStruct((B, T, H * dv), BF16),
        grid=(B, T // tq),
        in_specs=[blk(col_q), full(col_k), full(col_v), tab_blk, tab_blk, tab_full, tab_full,
                  pl.BlockSpec(lam_p.shape, lambda b, j: (0, 0)),
                  pl.BlockSpec(g.shape, lambda b, j: (0, 0))],
        out_specs=pl.BlockSpec((1, tq, H * dv), lambda b, j: (b, j, 0)),
        scratch_shapes=[pltpu.VMEM((2 * H, T, d), BF16),
                        pltpu.VMEM((H, T, 2 * dv), BF16)],
        compiler_params=_cparams(("arbitrary", "arbitrary")),
        name="diff_attention",
    )(proj, proj, proj, cos, sin, cos, sin, lam_p, g)


def _rwkv_kernel(x_ref, mu_ref, w0_ref, w2_ref, a0_ref, a2_ref, g2_ref, kk_ref, ka_ref, rk_ref,
                 lng_ref, lnb_ref, o_ref, s_ref, prev_ref, gate_ref, bonus_ref, dec_ref, qeff_ref, yloc_ref,
                 mlow_ref, nc_ref, y_ref, *, n_chunk, unroll=False):
    C = RWKV_CHUNK
    H = RWKV_HEADS
    W = o_ref.shape[2]
    N = W // H

    @pl.when(pl.program_id(1) == 0)
    def _():
        s_ref[...] = jnp.zeros_like(s_ref)
        prev_ref[...] = jnp.zeros_like(prev_ref)

    lane = lax.broadcasted_iota(jnp.int32, (1, W), 1)
    head_mask = [(lane // N == h).astype(F32) for h in range(H)]
    block_diag = _group_matrix(W, N)
    head_sum = block_diag.astype(F32)
    ti = lax.broadcasted_iota(jnp.int32, (C, C), 0)
    tj = lax.broadcasted_iota(jnp.int32, (C, C), 1)
    tril_incl = ti >= tj
    tril_strict = ti > tj
    same_sub = (ti // RWKV_SUB) == (tj // RWKV_SUB)
    eye = (ti == tj).astype(F32)
    nc = n_chunk
    tb = nc * C

    def head_total(t, two_term=False):
        ones = head_sum.astype(BF16)
        if two_term:
            return jnp.dot(_hi_lo(t), jnp.concatenate([ones, ones], axis=0), preferred_element_type=F32)
        return jnp.dot(t.astype(BF16), ones, preferred_element_type=F32)

    x = x_ref[0]
    first_row = lax.broadcasted_iota(jnp.int32, (tb, 1), 0) == 0
    prev = jnp.where(first_row, prev_ref[...], pltpu.roll(x, 1, 0))
    prev_ref[...] = x[tb - 1:tb, :]
    xm = x + (prev - x) * mu_ref[...]
    r = xm[:, 0:W]
    k = xm[:, W:2 * W]
    v = xm[:, 2 * W:3 * W]
    zz = xm[:, 3 * W:3 * W + LANES]
    zg = xm[:, 3 * W + LANES:]
    w = -_softplus(-(w0_ref[...] + _mm(jnp.tanh(zz), w2_ref[...]))) - 0.5
    lw = -jnp.exp(w)
    a = _sigmoid(a0_ref[...] + _mm(zz, a2_ref[...]))
    gate_ref[...] = _mm(_sigmoid(zg), g2_ref[...])
    kk = k * kk_ref[...]
    kk = kk / jnp.maximum(jnp.sqrt(head_total(kk * kk, two_term=True)), 1e-12)
    k = k * (1.0 + (a - 1.0) * ka_ref[...])
    b = kk * a
    bonus_ref[...] = head_total(r * k * rk_ref[...]) * v

    c3 = lambda t: t.reshape(nc, C, t.shape[1])
    sums = _bmm(jnp.broadcast_to(tril_incl.astype(BF16)[None], (nc, C, C)), c3(_hi_lo(lw)))
    G3 = sums[:, :, 0:W] + sums[:, :, W:2 * W]
    total3 = jnp.broadcast_to(G3[:, C - 1:C, :], (nc, C, W))
    dec_ref[...] = jnp.exp(total3).reshape(tb, W)
    G = G3.reshape(tb, W)
    inv = jnp.exp(-G)
    to_end = jnp.exp(total3 - G3).reshape(tb, W)
    kap = kk * jnp.exp(G - lw)
    rho = r * jnp.exp(G)
    kap3, rho3, v3 = c3(kap), c3(rho), c3(v)
    bet_kt3 = jnp.concatenate([c3(b * inv), c3(k * inv)], axis=1)
    betc3, kc3 = c3(b * to_end), c3(k * to_end)

    def bmm_nt(p, q):
        return lax.dot_general(p.astype(BF16), q.astype(BF16), (((2,), (2,)), ((0,), (0,))),
                               preferred_element_type=F32)

    def bmm_tn(p, q):
        return lax.dot_general(p.astype(BF16), q.astype(BF16), (((1,), (1,)), ((0,), (0,))),
                               preferred_element_type=F32)

    wi = lax.broadcasted_iota(jnp.int32, (C, 2 * C), 0)
    wj = lax.broadcasted_iota(jnp.int32, (C, 2 * C), 1) % C
    A_bk, B_bk = [], []
    for h in range(H):
        lhs = jnp.concatenate([c3(kap * head_mask[h]), c3(rho * head_mask[h])], axis=1)
        prod = bmm_nt(lhs, bet_kt3)
        A_bk.append(jnp.where(wj < wi, prod[:, 0:C], 0.0))
        B_bk.append(jnp.where(wj <= wi, prod[:, C:2 * C], 0.0))
    A_b = jnp.concatenate([t[:, :, 0:C] for t in A_bk], axis=0)
    B_b = [t[:, :, 0:C] for t in B_bk]
    v3_low = jnp.concatenate([jnp.zeros_like(v3), v3], axis=1)

    Dg = jnp.where(same_sub, A_b, 0.0)
    Lo = A_b - Dg
    D2 = _bmm(Dg, Dg)
    D4 = _bmm(D2, D2)
    D8 = _bmm(D4, D4)
    Dinv = _bmm(_bmm(_bmm(eye - Dg, eye + D2), eye + D4), eye + D8)
    Nn = _bmm(Dinv, Lo)
    N2 = _bmm(Nn, Nn)
    Tm = _bmm(_bmm(eye - Nn, eye + N2), Dinv)
    Tm = [Tm[h * nc:(h + 1) * nc] for h in range(H)]

    def per_head(mats, t):
        reps = t.shape[2] // W
        acc = None
        for h in range(H):
            m = head_mask[h] if reps == 1 else jnp.concatenate([head_mask[h]] * reps, axis=1)
            u = m * _bmm(mats[h], t)
            acc = u if acc is None else acc + u
        return acc

    akv = per_head(A_bk, v3_low)
    tk = per_head(Tm, jnp.concatenate([kap3, akv], axis=2))
    kap_p = tk[:, :, 0:W]
    v_p = tk[:, :, W:2 * W]
    bb = per_head(B_b, jnp.concatenate([kap_p, v_p], axis=2))
    qeff_ref[...] = (rho3 - bb[:, :, 0:W]).astype(BF16)
    yloc_ref[...] = per_head(B_bk, v3_low) - bb[:, :, W:2 * W]
    mlow_ref[...] = jnp.where(block_diag, bmm_tn(kap_p, betc3), 0.0).astype(BF16)
    nc_ref[...] = jnp.where(block_diag, bmm_tn(jnp.concatenate([v3, v_p], axis=1),
                                               jnp.concatenate([kc3, -betc3], axis=1)), 0.0)

    def chunk(ci, carry):
        r0 = pl.multiple_of(ci * C, C)
        s = s_ref[...]
        sb = s.astype(BF16)
        y = lax.dot_general(qeff_ref[ci], sb, (((1,), (1,)), ((), ())), preferred_element_type=F32)
        y_ref[pl.ds(r0, C), :] = y + yloc_ref[ci]
        s_ref[...] = (s * dec_ref[pl.ds(r0, 8), :][0:1, :]
                      - jnp.dot(sb, mlow_ref[ci], preferred_element_type=F32) + nc_ref[ci])
        return carry

    lax.fori_loop(0, nc, chunk, 0, unroll=unroll)

    y = y_ref[...]
    mean = head_total(y, two_term=True) / N
    yc = y - mean
    var = head_total(yc * yc) / N
    yn = yc * lax.rsqrt(var + RWKV_GN_EPS) * lng_ref[...] + lnb_ref[...]
    o_ref[0] = ((yn + bonus_ref[...]) * gate_ref[...]).astype(o_ref.dtype)


N_GLA_INPUTS = 7
N_GLA_SCRATCH = 6
N_RWKV_INPUTS = 12


def _gla_rwkv_kernel(*refs, gla_chunks, rwkv_chunks):
    n_in = N_GLA_INPUTS + N_RWKV_INPUTS
    gla_in, rwkv_in = refs[0:N_GLA_INPUTS], refs[N_GLA_INPUTS:n_in]
    gla_out, rwkv_out = refs[n_in], refs[n_in + 1]
    gla_scratch = refs[n_in + 2:n_in + 2 + N_GLA_SCRATCH]
    rwkv_scratch = refs[n_in + 2 + N_GLA_SCRATCH:]
    _gla_kernel(*gla_in, gla_out, *gla_scratch, n_chunk=gla_chunks, unroll=True)
    _rwkv_kernel(*rwkv_in, rwkv_out, *rwkv_scratch, n_chunk=rwkv_chunks, unroll=True)


def _gla_rwkv7(proj, gla_w2p, gla_gb, gla_ng, col_qk, col_v, col_og, col_gz, vecs, w2p, a2p, g2, col_rwkv, tb):
    B, T, _ = proj.shape
    hv = gla_ng.shape[1]
    hk = gla_gb.shape[1]
    W = g2.shape[1]
    cols = vecs["mu"].shape[1]
    names = ("mu", "w0", "w2", "a0", "a2", "g2", "kk", "ka", "rk", "lng", "lnb")
    params = dict(vecs, w2=w2p, a2=a2p, g2=g2)
    assert len(names) + 1 == N_RWKV_INPUTS
    const = lambda arr: pl.BlockSpec(arr.shape, lambda b, j: (0, 0))
    rows = lambda n, col: pl.BlockSpec((1, tb, n), lambda b, j: (b, j, col))
    n_chunk = tb // RWKV_CHUNK
    return pl.pallas_call(
        functools.partial(_gla_rwkv_kernel, gla_chunks=tb // GLA_CHUNK, rwkv_chunks=n_chunk),
        out_shape=(jax.ShapeDtypeStruct((B, T, hv), BF16), jax.ShapeDtypeStruct((B, T, W), BF16)),
        grid=(B, T // tb),
        in_specs=[rows(2 * hk, col_qk), rows(hv, col_v), rows(hv, col_og), rows(LANES, col_gz),
                  const(gla_w2p), const(gla_gb), const(gla_ng), rows(cols, col_rwkv)]
                 + [const(params[n]) for n in names],
        out_specs=(rows(hv, 0), rows(W, 0)),
        scratch_shapes=[pltpu.VMEM((hv, hk), F32), pltpu.VMEM((tb, hk), F32), pltpu.VMEM((tb, hk), F32),
                        pltpu.VMEM((tb, hk), BF16), pltpu.VMEM((tb, hk), BF16), pltpu.VMEM((tb, hv), F32),
                        pltpu.VMEM((W, W), F32), pltpu.VMEM((1, cols), F32),
                        pltpu.VMEM((tb, W), F32), pltpu.VMEM((tb, W), F32), pltpu.VMEM((tb, W), F32),
                        pltpu.VMEM((n_chunk, RWKV_CHUNK, W), BF16), pltpu.VMEM((n_chunk, RWKV_CHUNK, W), F32),
                        pltpu.VMEM((n_chunk, W, W), BF16), pltpu.VMEM((n_chunk, W, W), F32),
                        pltpu.VMEM((tb, W), F32)],
        compiler_params=_cparams(("arbitrary", "arbitrary")),
        name="gla_rwkv7_mixers",
    )(proj, proj, proj, proj, gla_w2p, gla_gb, gla_ng, proj, *[params[n] for n in names])


def _merge_kernel(x_ref, sc_ref, sh_ref, gt_ref, gpre_ref, gpost_ref, wg_ref, b0_ref, b1_ref, b2_ref,
                  b3_ref, wb_ref, wo_ref, o_ref):
    x = x_ref[0]
    D = x.shape[1]
    h = _rms_mod(x, gpre_ref[...], sc_ref[0], sh_ref[0]).astype(BF16)
    merged = None
    for g, br in enumerate((b0_ref, b1_ref, b2_ref, b3_ref)):
        gate = _sigmoid(jnp.dot(h, wg_ref[:, g * D:(g + 1) * D], preferred_element_type=F32))
        t = gate * jnp.dot(br[0], wb_ref[g], preferred_element_type=F32)
        merged = t if merged is None else merged + t
    y = _mm(merged, wo_ref[...])
    y = y * lax.rsqrt(jnp.mean(y * y, axis=-1, keepdims=True) + RMS_EPS) * gpost_ref[...]
    o_ref[0] = x + gt_ref[0] * y


def _merge(x, sc, sh, gt, gpre, gpost, w_gate, branches, w_branch, w_out, tm):
    B, T, D = x.shape
    bw = branches[0].shape[2]
    tok = lambda n: pl.BlockSpec((1, tm, n), lambda b, i: (b, i, 0))
    vec = pl.BlockSpec((1, 1, D), lambda b, i: (b, 0, 0))
    const2 = lambda arr: pl.BlockSpec(arr.shape, lambda b, i: (0, 0))
    return pl.pallas_call(
        _merge_kernel,
        out_shape=jax.ShapeDtypeStruct((B, T, D), F32),
        grid=(B, T // tm),
        in_specs=[tok(D), vec, vec, vec, const2(gpre), const2(gpost), const2(w_gate),
                  tok(bw), tok(bw), tok(bw), tok(bw),
                  pl.BlockSpec(w_branch.shape, lambda b, i: (0, 0, 0)), const2(w_out)],
        out_specs=tok(D),
        compiler_params=_cparams(("arbitrary", "arbitrary")),
        name="merge_out_proj",
    )(x, sc, sh, gt, gpre, gpost, w_gate, *branches, w_branch, w_out)


def _ffn_kernel(x_ref, sc_ref, sh_ref, gt_ref, gpre_ref, gpost_ref, wg_ref, wu_ref, wd_ref, o_ref, *, tf):
    x = x_ref[0]
    h = _rms_mod(x, gpre_ref[...], sc_ref[0], sh_ref[0]).astype(BF16)
    F = wg_ref.shape[1]
    acc = None
    for f0 in range(0, F, tf):
        f1 = min(f0 + tf, F)
        gate = jnp.dot(h, wg_ref[:, f0:f1], preferred_element_type=F32)
        up = jnp.dot(h, wu_ref[:, f0:f1], preferred_element_type=F32)
        t = jnp.dot((_silu(gate) * up).astype(BF16), wd_ref[f0:f1, :], preferred_element_type=F32)
        acc = t if acc is None else acc + t
    y = acc * lax.rsqrt(jnp.mean(acc * acc, axis=-1, keepdims=True) + RMS_EPS) * gpost_ref[...]
    o_ref[0] = x + gt_ref[0] * y


def _dense_ffn(x, sc, sh, gt, gpre, gpost, wg, wu, wd, tm):
    B, T, D = x.shape
    tok = pl.BlockSpec((1, tm, D), lambda b, i: (b, i, 0))
    vec = pl.BlockSpec((1, 1, D), lambda b, i: (b, 0, 0))
    const2 = lambda arr: pl.BlockSpec(arr.shape, lambda b, i: (0, 0))
    return pl.pallas_call(
        functools.partial(_ffn_kernel, tf=FFN_CHUNK),
        out_shape=jax.ShapeDtypeStruct((B, T, D), F32),
        grid=(B, T // tm),
        in_specs=[tok, vec, vec, vec, const2(gpre), const2(gpost), const2(wg), const2(wu), const2(wd)],
        out_specs=tok,
        compiler_params=_cparams(("arbitrary", "arbitrary")),
        name="dense_swiglu",
    )(x, sc, sh, gt, gpre, gpost, wg, wu, wd)


MOE_TOKEN_TILE = 256
MOE_ROW_TILE = 512
SEG_ALIGN = 16
SEG_PIECES = (256, 128, 64, 32, 16)
MOE_SMALL_SEG = 128


def _route_kernel(x_ref, sc_ref, sh_ref, gpre_ref, rw_ref, rb_ref, h_ref, mi_ref, mp_ref, cnt_ref):
    tm = x_ref.shape[1]
    h = _rms_mod(x_ref[0], gpre_ref[...], sc_ref[0], sh_ref[0])
    h_ref[...] = h.astype(BF16)
    h_hi = h.astype(BF16)
    h_lo = (h - h_hi.astype(F32)).astype(BF16)
    w = rw_ref[...]
    w_hi = w.astype(BF16)
    w_lo = (w - w_hi.astype(F32)).astype(BF16)
    logits = jnp.dot(jnp.concatenate([h_hi, h_lo, h_hi], axis=1), jnp.concatenate([w_hi, w_hi, w_lo], axis=0),
                     preferred_element_type=F32) + rb_ref[...]
    lane = lax.broadcasted_iota(jnp.int32, logits.shape, 1)
    v1 = jnp.max(logits, axis=-1, keepdims=True)
    i1 = jnp.min(jnp.where(logits == v1, lane, LANES), axis=-1, keepdims=True)
    rest = jnp.where(lane == i1, -jnp.inf, logits)
    v2 = jnp.max(rest, axis=-1, keepdims=True)
    i2 = jnp.min(jnp.where(rest == v2, lane, LANES), axis=-1, keepdims=True)
    e2 = jnp.exp(v2 - v1)
    p1 = 1.0 / (1.0 + e2)
    p2 = e2 / (1.0 + e2)
    oh1 = (lane == i1).astype(F32)
    oh2 = (lane == i2).astype(F32)
    both = oh1 + oh2
    earlier = (lax.broadcasted_iota(jnp.int32, (tm, tm), 1)
               < lax.broadcasted_iota(jnp.int32, (tm, tm), 0)).astype(BF16)
    before = jnp.dot(earlier, both.astype(BF16), preferred_element_type=F32)
    r1 = jnp.sum(oh1 * before, axis=-1, keepdims=True).astype(jnp.int32)
    r2 = jnp.sum(oh2 * before, axis=-1, keepdims=True).astype(jnp.int32)
    col = lax.broadcasted_iota(jnp.int32, mi_ref.shape, 1)
    mi_ref[...] = jnp.where(col == 0, i1, jnp.where(col == 1, i2, jnp.where(col == 2, r1,
                                                                           jnp.where(col == 3, r2, 0))))
    mp_ref[...] = jnp.where(col == 0, p1, jnp.where(col == 1, p2, 0.0))
    cnt_ref[0] = jnp.sum(both, axis=0, keepdims=True).astype(jnp.int32)


def _segment_pieces(n_rows):
    out = []
    for s in SEG_PIECES:
        if s == MOE_TOKEN_TILE:
            out.append((n_rows == s, 0, s))
        else:
            out.append(((n_rows & s) != 0, pl.multiple_of((n_rows // (2 * s)) * (2 * s), SEG_ALIGN), s))
    return out


def _all_segments_small(cnt_ref, tile):
    most = cnt_ref[tile * N_EXPERTS]
    for e in range(1, N_EXPERTS):
        most = jnp.maximum(most, cnt_ref[tile * N_EXPERTS + e])
    return most <= MOE_SMALL_SEG


def _dispatch_kernel(seg_ref, cnt_ref, h_ref, mit_ref, init_ref, xs_ref, buf_ref, sem):
    del init_ref
    tm = h_ref.shape[0]
    i = pl.program_id(0)
    e1, e2 = mit_ref[0:1, :], mit_ref[1:2, :]
    r1, r2 = mit_ref[2:3, :], mit_ref[3:4, :]
    slot = i % 2

    def compact(cap):
        row = lax.broadcasted_iota(jnp.int32, (cap, tm), 0)
        select = jnp.concatenate(
            [jnp.logical_or(jnp.logical_and(e1 == e, r1 == row), jnp.logical_and(e2 == e, r2 == row))
             for e in range(N_EXPERTS)], axis=0).astype(BF16)
        rows = jnp.dot(select, h_ref[...], preferred_element_type=F32).astype(BF16)
        for e in range(N_EXPERTS):
            buf_ref[slot, e * tm:e * tm + cap, :] = rows[e * cap:(e + 1) * cap]

    small = _all_segments_small(cnt_ref, i)
    pl.when(small)(functools.partial(compact, MOE_SMALL_SEG))
    pl.when(jnp.logical_not(small))(functools.partial(compact, tm))

    def segment_copies(tile, buf, e):
        n = cnt_ref[tile * N_EXPERTS + e]
        n_rows = ((n + SEG_ALIGN - 1) // SEG_ALIGN) * SEG_ALIGN
        dst = pl.multiple_of(seg_ref[tile * N_EXPERTS + e], SEG_ALIGN)
        return [(cond, pltpu.make_async_copy(buf_ref.at[buf, pl.ds(e * tm + off, s), :],
                                             xs_ref.at[pl.ds(dst + off, s), :], sem))
                for cond, off, s in _segment_pieces(n_rows)]

    def for_all_segments(tile, buf, action):
        for e in range(N_EXPERTS):
            for cond, cp in segment_copies(tile, buf, e):
                pl.when(cond)(getattr(cp, action))

    @pl.when(i > 0)
    def _():
        for_all_segments(i - 1, 1 - slot, "wait")

    for_all_segments(i, slot, "start")

    @pl.when(i == pl.num_programs(0) - 1)
    def _():
        for_all_segments(i, slot, "wait")


def _expert_kernel(te_ref, nv_ref, xs_ref, wg_ref, wu_ref, wd_ref, ys_ref, acc_ref):
    del te_ref
    r = pl.program_id(0)
    f = pl.program_id(1)
    valid = r < nv_ref[0]

    @pl.when(jnp.logical_and(valid, f == 0))
    def _():
        acc_ref[...] = jnp.zeros_like(acc_ref)

    @pl.when(valid)
    def _():
        x = xs_ref[...]
        tf = wg_ref.shape[2]
        acc = acc_ref[...]
        for f0 in range(0, tf, FFN_CHUNK):
            f1 = min(f0 + FFN_CHUNK, tf)
            gate = jnp.dot(x, wg_ref[0, :, f0:f1], preferred_element_type=F32)
            up = jnp.dot(x, wu_ref[0, :, f0:f1], preferred_element_type=F32)
            acc = acc + jnp.dot((_silu(gate) * up).astype(BF16), wd_ref[0, f0:f1, :],
                                preferred_element_type=F32)
        acc_ref[...] = acc

    @pl.when(f == pl.num_programs(1) - 1)
    def _():
        @pl.when(valid)
        def _():
            ys_ref[...] = acc_ref[...].astype(ys_ref.dtype)

        @pl.when(jnp.logical_not(valid))
        def _():
            ys_ref[...] = jnp.zeros_like(ys_ref)


def _combine_kernel(seg_ref, cnt_ref, ys_ref, mi_ref, mp_ref, x_ref, gt_ref, gpost_ref, o_ref, win_ref,
                    pair_ref, sem):
    tm = x_ref.shape[1]
    i = pl.program_id(0) * pl.num_programs(1) + pl.program_id(1)
    n_tiles = pl.num_programs(0) * pl.num_programs(1)
    slot = i % 2

    def window_copy(tile, buf, e, rows):
        src = pl.multiple_of(seg_ref[tile * N_EXPERTS + e], SEG_ALIGN)
        return pltpu.make_async_copy(ys_ref.at[pl.ds(src, rows), :], win_ref.at[buf, pl.ds(e * rows, rows), :],
                                     sem.at[buf, e])

    def for_all_windows(tile, buf, action):
        small = _all_segments_small(cnt_ref, tile)
        for rows, cond in ((MOE_SMALL_SEG, small), (tm, jnp.logical_not(small))):
            @pl.when(cond)
            def _():
                for e in range(N_EXPERTS):
                    getattr(window_copy(tile, buf, e, rows), action)()

    @pl.when(i == 0)
    def _():
        for_all_windows(i, slot, "start")

    @pl.when(i + 1 < n_tiles)
    def _():
        for_all_windows(i + 1, 1 - slot, "start")

    for_all_windows(i, slot, "wait")
    e1, e2 = mi_ref[:, 0:1], mi_ref[:, 1:2]
    r1, r2 = mi_ref[:, 2:3], mi_ref[:, 3:4]

    def expand_rows(cap):
        col = lax.broadcasted_iota(jnp.int32, (tm, cap), 1)
        expand = jnp.concatenate(
            [jnp.concatenate([jnp.logical_and(e1 == e, r1 == col), jnp.logical_and(e2 == e, r2 == col)], axis=0)
             for e in range(N_EXPERTS)], axis=1).astype(BF16)
        pair_ref[...] = jnp.dot(expand, win_ref[slot, 0:N_EXPERTS * cap, :], preferred_element_type=F32)

    small = _all_segments_small(cnt_ref, i)
    pl.when(small)(functools.partial(expand_rows, MOE_SMALL_SEG))
    pl.when(jnp.logical_not(small))(functools.partial(expand_rows, tm))
    y = mp_ref[:, 0:1] * pair_ref[0:tm, :] + mp_ref[:, 1:2] * pair_ref[tm:2 * tm, :]
    y = y * lax.rsqrt(jnp.mean(y * y, axis=-1, keepdims=True) + RMS_EPS) * gpost_ref[...]
    o_ref[0] = x_ref[0] + gt_ref[0] * y


def _moe_ffn(x, sc, sh, gt, gpre, gpost, rw, rb, wg, wu, wd, tf):
    B, T, D = x.shape
    E, _, F = wg.shape
    tm = min(MOE_TOKEN_TILE, T)
    assert tm == MOE_TOKEN_TILE and E == N_EXPERTS
    nT = T // tm
    n_tok_tiles = B * nT
    N = B * T
    max_rows = 2 * N + n_tok_tiles * E * (SEG_ALIGN - 1) + E * (MOE_ROW_TILE - SEG_ALIGN)
    n_row_tiles = -(-max_rows // MOE_ROW_TILE) + 1
    P = n_row_tiles * MOE_ROW_TILE

    vec = pl.BlockSpec((1, 1, D), lambda b, j: (b, 0, 0))
    const2 = lambda arr: pl.BlockSpec(arr.shape, lambda b, j: (0, 0))
    flat = lambda n: pl.BlockSpec((tm, n), lambda b, j: (b * nT + j, 0))
    h, mi, mp, cnt = pl.pallas_call(
        _route_kernel,
        out_shape=(jax.ShapeDtypeStruct((N, D), BF16), jax.ShapeDtypeStruct((N, 8), jnp.int32),
                   jax.ShapeDtypeStruct((N, 8), F32), jax.ShapeDtypeStruct((n_tok_tiles, 1, LANES), jnp.int32)),
        grid=(B, nT),
        in_specs=[pl.BlockSpec((1, tm, D), lambda b, j: (b, j, 0)), vec, vec, const2(gpre), const2(rw),
                  const2(rb)],
        out_specs=(flat(D), flat(8), flat(8), pl.BlockSpec((1, 1, LANES), lambda b, j: (b * nT + j, 0, 0))),
        compiler_params=_cparams(("arbitrary", "arbitrary")),
        name="moe_route",
    )(x, sc, sh, gpre, rw, rb)

    counts = cnt[:, 0, :E]
    seg_len = (counts + SEG_ALIGN - 1) // SEG_ALIGN * SEG_ALIGN
    group_len = (jnp.sum(seg_len, axis=0) + MOE_ROW_TILE - 1) // MOE_ROW_TILE * MOE_ROW_TILE
    group_end = jnp.cumsum(group_len)
    seg_start = (group_end - group_len)[None, :] + jnp.cumsum(seg_len, axis=0) - seg_len
    seg_start = seg_start.reshape(-1).astype(jnp.int32)
    counts = counts.reshape(-1)
    n_valid = (group_end[-1:] // MOE_ROW_TILE).astype(jnp.int32)
    tile_first_row = jnp.arange(n_row_tiles, dtype=jnp.int32) * MOE_ROW_TILE
    tile_expert = jnp.minimum(jnp.sum(tile_first_row[:, None] >= group_end[None, :], axis=1), E - 1)
    tile_expert = tile_expert.astype(jnp.int32)

    xs = pl.pallas_call(
        _dispatch_kernel,
        out_shape=jax.ShapeDtypeStruct((P, D), BF16),
        grid_spec=pltpu.PrefetchScalarGridSpec(
            num_scalar_prefetch=2,
            grid=(n_tok_tiles,),
            in_specs=[pl.BlockSpec((tm, D), lambda i, seg, n: (i, 0)),
                      pl.BlockSpec((8, tm), lambda i, seg, n: (0, i)),
                      pl.BlockSpec(memory_space=pl.ANY)],
            out_specs=pl.BlockSpec(memory_space=pl.ANY),
            scratch_shapes=[pltpu.VMEM((2, E * tm, D), BF16), pltpu.SemaphoreType.DMA(())]),
        input_output_aliases={4: 0},
        compiler_params=_cparams(("arbitrary",)),
        name="moe_dispatch",
    )(seg_start, counts, h, mi.T, jnp.zeros((P, D), BF16))

    nf = F // tf
    live = lambda r, f, nv: jnp.where(r < nv[0], f, nf - 1)
    ys = pl.pallas_call(
        _expert_kernel,
        out_shape=jax.ShapeDtypeStruct((P, D), BF16),
        grid_spec=pltpu.PrefetchScalarGridSpec(
            num_scalar_prefetch=2,
            grid=(n_row_tiles, nf),
            in_specs=[pl.BlockSpec((MOE_ROW_TILE, D), lambda r, f, te, nv: (r, 0)),
                      pl.BlockSpec((1, D, tf), lambda r, f, te, nv: (te[r], 0, live(r, f, nv))),
                      pl.BlockSpec((1, D, tf), lambda r, f, te, nv: (te[r], 0, live(r, f, nv))),
                      pl.BlockSpec((1, tf, D), lambda r, f, te, nv: (te[r], live(r, f, nv), 0))],
            out_specs=pl.BlockSpec((MOE_ROW_TILE, D), lambda r, f, te, nv: (r, 0)),
            scratch_shapes=[pltpu.VMEM((MOE_ROW_TILE, D), F32)]),
        compiler_params=_cparams(("arbitrary", "arbitrary")),
        name="moe_experts",
    )(tile_expert, n_valid, xs, wg, wu, wd)

    return pl.pallas_call(
        _combine_kernel,
        out_shape=jax.ShapeDtypeStruct((B, T, D), F32),
        grid_spec=pltpu.PrefetchScalarGridSpec(
            num_scalar_prefetch=2,
            grid=(B, nT),
            in_specs=[pl.BlockSpec(memory_space=pl.ANY),
                      pl.BlockSpec((tm, 8), lambda b, j, seg, n: (b * nT + j, 0)),
                      pl.BlockSpec((tm, 8), lambda b, j, seg, n: (b * nT + j, 0)),
                      pl.BlockSpec((1, tm, D), lambda b, j, seg, n: (b, j, 0)),
                      pl.BlockSpec((1, 1, D), lambda b, j, seg, n: (b, 0, 0)),
                      pl.BlockSpec(gpost.shape, lambda b, j, seg, n: (0, 0))],
            out_specs=pl.BlockSpec((1, tm, D), lambda b, j, seg, n: (b, j, 0)),
            scratch_shapes=[pltpu.VMEM((2, E * tm, D), BF16), pltpu.VMEM((2 * tm, D), F32),
                            pltpu.SemaphoreType.DMA((2, E))]),
        compiler_params=_cparams(("arbitrary", "arbitrary")),
        name="moe_combine",
    )(seg_start, counts, ys, mi, mp, x, gt, gpost)


def _rope_tables(positions, groups):
    d = 32
    inv = 1.0 / (ROPE_THETA ** (jnp.arange(0, d, 2, dtype=F32) / d))
    ang = positions.astype(F32)[..., None] * inv
    cos, sin = jnp.cos(ang), jnp.sin(ang)
    cos = jnp.tile(jnp.concatenate([cos, cos], axis=-1), (1, 1, groups))
    sin = jnp.tile(jnp.concatenate([-sin, sin], axis=-1), (1, 1, groups))
    return cos, sin


def _pad_rows(w, rows, offset):
    out = jnp.zeros((rows, w.shape[1]), w.dtype)
    return out.at[offset:offset + w.shape[0]].set(w)


def kernel(x, c, positions, ada_w, ada_b, norm_mix_pre, norm_mix_post, norm_ffn_pre, norm_ffn_post, w_in, gla_gate_w2, gla_gate_b, gla_norm, diff_lambda, diff_subln, conv_w, conv_b, conv_ln_g, conv_ln_b, rwkv_mu, rwkv_w0, rwkv_w2, rwkv_a0, rwkv_a2, rwkv_g2, rwkv_k_k, rwkv_k_a, rwkv_r_k, rwkv_ln_g, rwkv_ln_b, w_branch, w_out, ffn_w_gate, ffn_w_up, ffn_w_down, router_w, router_b, moe_w_gate, moe_w_up, moe_w_down):
    B, T, D = x.shape
    L = ada_w.shape[0]
    W = D // N_BRANCH
    hk = gla_gate_b.shape[1]
    decay_rank = rwkv_w2.shape[1]
    a_rank = rwkv_a2.shape[1]
    gate_rank = rwkv_g2.shape[1]
    assert decay_rank + a_rank == LANES and 2 * hk == W and gate_rank == LANES
    n_mix = (3 * W + decay_rank + a_rank + gate_rank) + 3 * W + 3 * W
    sizes = (hk, hk, W, W, GLA_GATE_RANK, W, W, W, W, W, 3 * W + LANES + gate_rank, N_BRANCH * D)
    offs = [0]
    for s in sizes:
        offs.append(offs[-1] + s)
    assert offs[-1] == w_in.shape[2]
    tm = min(TOKEN_TILE, T)
    tb = min(SEQ_BLOCK, T)

    mod = _modulation(c, ada_w, ada_b)
    cos, sin = _rope_tables(positions, LANES // 32)

    for l in range(L):
        m = mod[l].reshape(B, 1, 6 * D)
        sh_m, sc_m, gt_m, sh_f, sc_f, gt_f = [m[:, :, i * D:(i + 1) * D] for i in range(6)]

        wl = w_in[l]
        gz_cols = jnp.zeros((D, LANES), F32).at[:, :GLA_GATE_RANK].set(wl[:, offs[4]:offs[5]])
        w_mix = jnp.concatenate([wl[:, offs[10]:offs[11]], wl[:, offs[0]:offs[4]], wl[:, offs[5]:offs[8]],
                                 gz_cols], axis=1).astype(BF16)
        w_conv = wl[:, offs[8]:offs[10]].astype(BF16)
        w_gate = wl[:, offs[11]:offs[12]].astype(BF16)
        proj, o_conv = _in_projection_conv(x, sc_m, sh_m, norm_mix_pre[l][None], w_conv, w_mix,
                                           _pad_rows(conv_w[l], 32, 0), conv_b[l][None],
                                           conv_ln_g[l][None], conv_ln_b[l][None], tm)

        o_diff = _diff_attention(proj, cos, sin, diff_lambda[l],
                                 diff_subln[l][None], col_q=7, col_k=8, col_v=9, layer_idx=l)
        vecs = dict(mu=rwkv_mu[l][None], w0=rwkv_w0[l][None], a0=rwkv_a0[l][None],
                    kk=rwkv_k_k[l][None], ka=rwkv_k_a[l][None], rk=rwkv_r_k[l].reshape(1, W),
                    lng=rwkv_ln_g[l][None], lnb=rwkv_ln_b[l][None])
        o_gla, o_rwkv = _gla_rwkv7(
            proj, _pad_rows(gla_gate_w2[l], LANES, 0), gla_gate_b[l][None],
            jnp.tile(gla_norm[l], GLA_HEADS)[None], 4, 5, 6, n_mix // LANES,
            vecs, _pad_rows(rwkv_w2[l], LANES, 0).astype(BF16),
            _pad_rows(rwkv_a2[l], LANES, decay_rank).astype(BF16), rwkv_g2[l].astype(BF16), 0, tb)
        x = _merge(x, sc_m, sh_m, gt_m, norm_mix_pre[l][None], norm_mix_post[l][None], w_gate,
                   (o_gla, o_diff, o_conv, o_rwkv), w_branch[l].astype(BF16), w_out[l].astype(BF16), tm)

        i = l // 2
        if l % 2 == 0:
            x = _dense_ffn(x, sc_f, sh_f, gt_f, norm_ffn_pre[l][None], norm_ffn_post[l][None],
                           ffn_w_gate[i].astype(BF16), ffn_w_up[i].astype(BF16),
                           ffn_w_down[i].astype(BF16), tm)
        else:
            rw = jnp.zeros((D, LANES), F32).at[:, :N_EXPERTS].set(router_w[i])
            rb = jnp.full((1, LANES), -jnp.inf, F32).at[0, :N_EXPERTS].set(router_b[i])
            x = _moe_ffn(x, sc_f, sh_f, gt_f, norm_ffn_pre[l][None], norm_ffn_post[l][None], rw, rb,
                         moe_w_gate[i].astype(BF16), moe_w_up[i].astype(BF16),
                         moe_w_down[i].astype(BF16), tf=moe_w_gate.shape[3])
    return x
```

```python
import functools
import math

import jax
import jax.numpy as jnp
from jax import lax
from jax.experimental import pallas as pl
from jax.experimental.pallas import tpu as pltpu

F32 = jnp.float32
BF16 = jnp.bfloat16
HIGHEST = lax.Precision.HIGHEST
LOG2_E = math.log2(math.e)

N_BRANCH = 4
GLA_HEADS = 4
GLA_GATE_RANK = 16
GLA_GATE_NORMALIZER = 16.0
GLA_CHUNK = 32
DIFF_HEADS = 4
ROPE_THETA = 10000.0
CONV_WIDTH = 31
RWKV_HEADS = 4
RWKV_CHUNK = 64
RWKV_SUB = 16
N_EXPERTS = 8
RMS_EPS = 1e-6
LN_EPS = 1e-5
RWKV_GN_EPS = 64e-5
LANES = 128
SUBLANES = 8
VMEM_LIMIT = 56 * 1024 * 1024

TOKEN_TILE = 512
SEQ_BLOCK = 512
ATTN_BLOCK = 512
CONV_ROWS = 128
FFN_CHUNK = 512


def _cparams(sem):
    return pltpu.CompilerParams(dimension_semantics=sem, vmem_limit_bytes=VMEM_LIMIT)


def _mm(a, b):
    return jnp.dot(a.astype(BF16), b.astype(BF16), preferred_element_type=F32)


def _mm_nt(a, b):
    return lax.dot_general(a.astype(BF16), b.astype(BF16), (((1,), (1,)), ((), ())),
                           preferred_element_type=F32)


def _mm_tn(a, b):
    return lax.dot_general(a.astype(BF16), b.astype(BF16), (((0,), (0,)), ((), ())),
                           preferred_element_type=F32)


def _bmm(a, b):
    return lax.dot_general(a.astype(BF16), b.astype(BF16), (((2,), (1,)), ((0,), (0,))),
                           preferred_element_type=F32)


def _mm_f32(a, b):
    return jnp.dot(a, b, precision=HIGHEST, preferred_element_type=F32)


def _hi_lo(x):
    hi = x.astype(BF16)
    return jnp.concatenate([hi, (x - hi.astype(F32)).astype(BF16)], axis=1)


def _sigmoid(x):
    return 0.5 * jnp.tanh(0.5 * x) + 0.5


def _silu(x):
    return x * _sigmoid(x)


def _softplus(x):
    return jnp.maximum(x, 0.0) + jnp.log(1.0 + jnp.exp(-jnp.abs(x)))


def _group_matrix(n, group):
    r = lax.broadcasted_iota(jnp.int32, (n, n), 0) // group
    c = lax.broadcasted_iota(jnp.int32, (n, n), 1) // group
    return r == c


def _rms_mod(x, gain, scale, shift):
    y = x * lax.rsqrt(jnp.mean(x * x, axis=-1, keepdims=True) + RMS_EPS)
    return y * gain * (1.0 + scale) + shift


def _mod_kernel(c_ref, w_ref, b_ref, o_ref):
    o_ref[0] = _mm_f32(_silu(c_ref[...]), w_ref[0]) + b_ref[0]


def _modulation(c, ada_w, ada_b):
    L, D, M = ada_w.shape
    B = c.shape[0]
    tn = M // 4
    return pl.pallas_call(
        _mod_kernel,
        out_shape=jax.ShapeDtypeStruct((L, B, M), F32),
        grid=(L, M // tn),
        in_specs=[pl.BlockSpec((B, D), lambda l, j: (0, 0)),
                  pl.BlockSpec((1, D, tn), lambda l, j: (l, 0, j)),
                  pl.BlockSpec((1, 1, tn), lambda l, j: (l, 0, j))],
        out_specs=pl.BlockSpec((1, B, tn), lambda l, j: (l, 0, j)),
        compiler_params=_cparams(("arbitrary", "arbitrary")),
        name="adaln_mod",
    )(c, ada_w, ada_b.reshape(L, 1, M))


CONV_HALO = 32


def _inproj_conv_kernel(x_ref, sc_ref, sh_ref, g_ref, wab_ref, w_ref, cw_ref, cb_ref, lg_ref, lb_ref,
                        o_ref, oc_ref, u_ref, *, rb):
    tm = x_ref.shape[1]
    ch = oc_ref.shape[2]
    h = _rms_mod(x_ref[0], g_ref[...], sc_ref[0], sh_ref[0]).astype(BF16)

    @pl.when(pl.program_id(1) == 0)
    def _():
        u_ref[0:CONV_HALO, :] = jnp.zeros((CONV_HALO, ch), F32)

    ab = jnp.dot(h, wab_ref[...], preferred_element_type=F32)
    u_ref[CONV_HALO:CONV_HALO + tm, :] = ab[:, 0:ch] * _sigmoid(ab[:, ch:2 * ch])
    o_ref[0] = jnp.dot(h, w_ref[...], preferred_element_type=F32)

    first = CONV_HALO - (CONV_WIDTH - 1)
    for i in range(tm // rb):
        r0 = i * rb
        win = u_ref[r0:r0 + rb + CONV_HALO, :]
        acc = jnp.zeros((rb, ch), F32)
        for s in range(SUBLANES):
            taps = [j for j in range(CONV_WIDTH) if (first + j) % SUBLANES == s]
            rolled = win if s == 0 else pltpu.roll(win, rb + CONV_HALO - s, 0)
            for j in taps:
                a0 = first + j - s
                acc = acc + cw_ref[j:j + 1, :] * rolled[a0:a0 + rb, :]
        y = acc + cb_ref[...]
        mu = jnp.mean(y, axis=-1, keepdims=True)
        yc = y - mu
        var = jnp.mean(yc * yc, axis=-1, keepdims=True)
        y = yc * lax.rsqrt(var + LN_EPS) * lg_ref[...] + lb_ref[...]
        oc_ref[0, r0:r0 + rb, :] = _silu(y).astype(oc_ref.dtype)
    u_ref[0:CONV_HALO, :] = u_ref[tm:tm + CONV_HALO, :]


def _in_projection_conv(x, sc, sh, gain, w_ab, w, conv_w, conv_b, ln_g, ln_b, tm):
    B, T, D = x.shape
    n = w.shape[1]
    ch = conv_w.shape[1]
    const = lambda arr: pl.BlockSpec(arr.shape, lambda b, i: (0, 0))
    return pl.pallas_call(
        functools.partial(_inproj_conv_kernel, rb=min(CONV_ROWS, tm)),
        out_shape=(jax.ShapeDtypeStruct((B, T, n), F32), jax.ShapeDtypeStruct((B, T, ch), BF16)),
        grid=(B, T // tm),
        in_specs=[pl.BlockSpec((1, tm, D), lambda b, i: (b, i, 0)),
                  pl.BlockSpec((1, 1, D), lambda b, i: (b, 0, 0)),
                  pl.BlockSpec((1, 1, D), lambda b, i: (b, 0, 0)),
                  const(gain), const(w_ab), const(w), const(conv_w), const(conv_b), const(ln_g), const(ln_b)],
        out_specs=(pl.BlockSpec((1, tm, n), lambda b, i: (b, i, 0)),
                   pl.BlockSpec((1, tm, ch), lambda b, i: (b, i, 0))),
        scratch_shapes=[pltpu.VMEM((tm + CONV_HALO, ch), F32)],
        compiler_params=_cparams(("arbitrary", "arbitrary")),
        name="in_proj_conv",
    )(x, sc, sh, gain, w_ab, w, conv_w, conv_b, ln_g, ln_b)


def _gla_kernel(qk_ref, v_ref, og_ref, gz_ref, w2_ref, gb_ref, ng_ref, o_ref, s_ref, g_ref, r_ref,
                qd_ref, kd_ref, oacc_ref, *, n_chunk, unroll=False):
    C = GLA_CHUNK
    hk = qk_ref.shape[2] // 2
    hv = v_ref.shape[2]
    dk = hk // GLA_HEADS
    dv = hv // GLA_HEADS

    tb = n_chunk * C

    @pl.when(pl.program_id(1) == 0)
    def _():
        s_ref[...] = jnp.zeros_like(s_ref)

    z = _mm(gz_ref[0], w2_ref[...]) + gb_ref[...]
    gk = (jnp.minimum(z, 0.0) - jnp.log(1.0 + jnp.exp(-jnp.abs(z)))) * (LOG2_E / GLA_GATE_NORMALIZER)
    tri = (lax.broadcasted_iota(jnp.int32, (C, C), 1)
           <= lax.broadcasted_iota(jnp.int32, (C, C), 0)).astype(BF16)
    sums = _bmm(jnp.broadcast_to(tri[None], (n_chunk, C, C)), _hi_lo(gk).reshape(n_chunk, C, 2 * hk))
    G3 = sums[:, :, 0:hk] + sums[:, :, hk:2 * hk]
    G_all = G3.reshape(tb, hk)
    R_all = (jnp.broadcast_to(G3[:, C - 1:C, :], (n_chunk, C, hk)) - G3).reshape(tb, hk)
    g_ref[...] = G_all
    r_ref[...] = R_all
    qd_ref[...] = (qk_ref[0, :, 0:hk] * (dk ** -0.5) * jnp.exp2(G_all)).astype(BF16)
    kd_ref[...] = (qk_ref[0, :, hk:2 * hk] * jnp.exp2(R_all)).astype(BF16)

    causal = (lax.broadcasted_iota(jnp.int32, (C, C, hk), 1)
              <= lax.broadcasted_iota(jnp.int32, (C, C, hk), 0))
    er = lax.broadcasted_iota(jnp.int32, (hk, hv), 0) // dk
    ec = lax.broadcasted_iota(jnp.int32, (hk, hv), 1) // dv
    expand = (er == ec).astype(BF16)
    sr = lax.broadcasted_iota(jnp.int32, (hv, hk), 0) // dv
    scol = lax.broadcasted_iota(jnp.int32, (hv, hk), 1) // dk
    state_mask = sr == scol
    head_mean = (_group_matrix(hv, dv).astype(F32) / dv).astype(BF16)

    def chunk(ci, carry):
        r0 = pl.multiple_of(ci * C, C)
        rows = pl.ds(r0, C)
        q = qk_ref[0, rows, 0:hk] * (dk ** -0.5)
        k = qk_ref[0, rows, hk:2 * hk]
        v = v_ref[0, rows, :]
        G = g_ref[rows, :]
        pair = (C, C, hk)
        g_diff = jnp.broadcast_to(G[:, None, :], pair) - jnp.broadcast_to(G[None], pair)
        decay = jnp.exp2(jnp.where(causal, g_diff, -jnp.inf))
        p = jnp.broadcast_to(q[:, None, :], pair) * jnp.broadcast_to(k[None], pair) * decay
        a_exp = jnp.dot(p.reshape(C * C, hk).astype(BF16), expand, preferred_element_type=F32)
        v_j = jnp.broadcast_to(v[None], (C, C, hv))
        o_intra = jnp.sum(a_exp.reshape(C, C, hv) * v_j, axis=1)
        g_total = G[0:1, :] + r_ref[pl.ds(r0, 8), :][0:1, :]
        s = s_ref[...]
        o_inter = lax.dot_general(qd_ref[rows, :], s.astype(BF16), (((1,), (1,)), ((), ())),
                                  preferred_element_type=F32)
        kv = lax.dot_general(v.astype(BF16), kd_ref[rows, :], (((0,), (0,)), ((), ())),
                             preferred_element_type=F32)
        s_ref[...] = s * jnp.exp2(g_total) + jnp.where(state_mask, kv, 0.0)
        oacc_ref[rows, :] = o_intra + o_inter
        return carry

    lax.fori_loop(0, n_chunk, chunk, 0, unroll=unroll)

    o = oacc_ref[...]
    ms = jnp.dot(_hi_lo(o * o), jnp.concatenate([head_mean, head_mean], axis=0),
                 preferred_element_type=F32)
    o = o * lax.rsqrt(ms + RMS_EPS) * ng_ref[...] * _silu(og_ref[0])
    o_ref[0] = o.astype(o_ref.dtype)


def _rope(t, cos, sin_signed):
    d = 32
    half = d // 2
    out = []
    for s in range(t.shape[1] // LANES):
        x = t[:, s * LANES:(s + 1) * LANES]
        lane = lax.broadcasted_iota(jnp.int32, x.shape, 1)
        up = pltpu.roll(x, LANES - half, 1)
        down = pltpu.roll(x, half, 1)
        rot = jnp.where((lane % d) < half, up, down)
        out.append(x * cos + rot * sin_signed)
    return jnp.concatenate(out, axis=1)


def _diff_kernel(q_ref, k_ref, v_ref, cosq_ref, sinq_ref, cosk_ref, sink_ref, lam_ref, g_ref, o_ref,
                 ks, vs, *, tq, lam_init):
    H = DIFF_HEADS
    d = q_ref.shape[2] // (2 * H)
    dv = v_ref.shape[2] // H
    j = pl.program_id(1)

    @pl.when(j == 0)
    def _():
        k = _rope(k_ref[0], cosk_ref[0], sink_ref[0])
        v = v_ref[0]
        for hc in range(2 * H):
            ks[hc] = k[:, hc * d:(hc + 1) * d].astype(BF16)
        ones_col = (lax.broadcasted_iota(jnp.int32, (v.shape[0], dv), 1) == 0).astype(BF16)
        for h in range(H):
            vs[h] = jnp.concatenate([v[:, h * dv:(h + 1) * dv].astype(BF16), ones_col], axis=1)

    q = _rope(q_ref[0], cosq_ref[0], sinq_ref[0]) * (d ** -0.5 * LOG2_E)
    lp = lam_ref[...]
    lam = (jnp.exp(jnp.sum(lp[0:1] * lp[1:2], axis=-1, keepdims=True))
           - jnp.exp(jnp.sum(lp[2:3] * lp[3:4], axis=-1, keepdims=True)) + lam_init)
    on_or_below_diag = (lax.broadcasted_iota(jnp.int32, (tq, tq), 1)
                        <= lax.broadcasted_iota(jnp.int32, (tq, tq), 0))

    def update(qh, kh, vh, m, acc, masked):
        s = lax.dot_general(qh, kh, (((1,), (1,)), ((), ())), preferred_element_type=F32)
        if masked:
            s = jnp.where(on_or_below_diag, s, -jnp.inf)
        m_new = jnp.maximum(m, jnp.max(s, axis=-1, keepdims=True))
        p = jnp.exp2(s - m_new).astype(BF16)
        acc = jnp.exp2(m - m_new) * acc + jnp.dot(p, vh, preferred_element_type=F32)
        return m_new, acc

    qs = [q[:, hc * d:(hc + 1) * d].astype(BF16) for hc in range(2 * H)]

    def kv_block(kb, carry, masked):
        rows = pl.ds(pl.multiple_of(kb * tq, tq), tq)
        out = []
        for hc in range(2 * H):
            m, acc = carry[2 * hc], carry[2 * hc + 1]
            out.extend(update(qs[hc], ks[hc, rows, :], vs[hc // 2, rows, :], m, acc, masked))
        return tuple(out)

    m0 = jnp.full((tq, 1), -jnp.inf, F32)
    a0 = jnp.zeros((tq, 2 * dv), F32)
    carry = lax.fori_loop(0, j, lambda kb, c: kv_block(kb, c, False), (m0, a0) * (2 * H))
    carry = kv_block(j, carry, True)

    for h in range(H):
        a1, a2 = carry[4 * h + 1], carry[4 * h + 3]
        comp = [a[:, 0:dv] / a[:, dv:dv + 1] for a in (a1, a2)]
        o = comp[0] - lam * comp[1]
        o = o * lax.rsqrt(jnp.mean(o * o, axis=-1, keepdims=True) + RMS_EPS)
        o = o * g_ref[...] * (1.0 - lam_init)
        o_ref[0, :, h * dv:(h + 1) * dv] = o.astype(o_ref.dtype)


def _diff_attention(proj, cos, sin, lam_p, g, col_q, col_k, col_v, layer_idx):
    B, T, _ = proj.shape
    H = DIFF_HEADS
    dv = g.shape[1]
    d = dv // 2
    w = 2 * H * d
    assert cos.shape[2] == LANES
    tq = min(ATTN_BLOCK, T)
    lam_init = 0.8 - 0.6 * math.exp(-0.3 * layer_idx)
    blk = lambda col: pl.BlockSpec((1, tq, w), lambda b, j: (b, j, col))
    full = lambda col: pl.BlockSpec((1, T, w), lambda b, j: (b, 0, col))
    tab_blk = pl.BlockSpec((1, tq, LANES), lambda b, j: (b, j, 0))
    tab_full = pl.BlockSpec((1, T, LANES), lambda b, j: (b, 0, 0))
    return pl.pallas_call(
        functools.partial(_diff_kernel, tq=tq, lam_init=lam_init),
        out_shape=jax.ShapeDtypeStruct((B, T, H * dv), BF16),
        grid=(B, T // tq),
        in_specs=[blk(col_q), full(col_k), full(col_v), tab_blk, tab_blk, tab_full, tab_full,
                  pl.BlockSpec(lam_p.shape, lambda b, j: (0, 0)),
                  pl.BlockSpec(g.shape, lambda b, j: (0, 0))],
        out_specs=pl.BlockSpec((1, tq, H * dv), lambda b, j: (b, j, 0)),
        scratch_shapes=[pltpu.VMEM((2 * H, T, d), BF16),
                        pltpu.VMEM((H, T, 2 * dv), BF16)],
        compiler_params=_cparams(("arbitrary", "arbitrary")),
        name="diff_attention",
    )(proj, proj, proj, cos, sin, cos, sin, lam_p, g)


def _rwkv_kernel(x_ref, mu_ref, w0_ref, w2_ref, a0_ref, a2_ref, g2_ref, kk_ref, ka_ref, rk_ref,
                 lng_ref, lnb_ref, o_ref, s_ref, prev_ref, gate_ref, bonus_ref, dec_ref, qeff_ref, yloc_ref,
                 mlow_ref, nc_ref, y_ref, *, n_chunk, unroll=False):
    C = RWKV_CHUNK
    H = RWKV_HEADS
    W = o_ref.shape[2]
    N = W // H

    @pl.when(pl.program_id(1) == 0)
    def _():
        s_ref[...] = jnp.zeros_like(s_ref)
        prev_ref[...] = jnp.zeros_like(prev_ref)

    lane = lax.broadcasted_iota(jnp.int32, (1, W), 1)
    head_mask = [(lane // N == h).astype(F32) for h in range(H)]
    block_diag = _group_matrix(W, N)
    head_sum = block_diag.astype(F32)
    ti = lax.broadcasted_iota(jnp.int32, (C, C), 0)
    tj = lax.broadcasted_iota(jnp.int32, (C, C), 1)
    tril_incl = ti >= tj
    tril_strict = ti > tj
    same_sub = (ti // RWKV_SUB) == (tj // RWKV_SUB)
    eye = (ti == tj).astype(F32)
    nc = n_chunk
    tb = nc * C

    def head_total(t, two_term=False):
        ones = head_sum.astype(BF16)
        if two_term:
            return jnp.dot(_hi_lo(t), jnp.concatenate([ones, ones], axis=0), preferred_element_type=F32)
        return jnp.dot(t.astype(BF16), ones, preferred_element_type=F32)

    x = x_ref[0]
    first_row = lax.broadcasted_iota(jnp.int32, (tb, 1), 0) == 0
    prev = jnp.where(first_row, prev_ref[...], pltpu.roll(x, 1, 0))
    prev_ref[...] = x[tb - 1:tb, :]
    xm = x + (prev - x) * mu_ref[...]
    r = xm[:, 0:W]
    k = xm[:, W:2 * W]
    v = xm[:, 2 * W:3 * W]
    zz = xm[:, 3 * W:3 * W + LANES]
    zg = xm[:, 3 * W + LANES:]
    w = -_softplus(-(w0_ref[...] + _mm(jnp.tanh(zz), w2_ref[...]))) - 0.5
    lw = -jnp.exp(w) * LOG2_E
    a = _sigmoid(a0_ref[...] + _mm(zz, a2_ref[...]))
    gate_ref[...] = _mm(_sigmoid(zg), g2_ref[...])
    kk = k * kk_ref[...]
    kk = kk / jnp.maximum(jnp.sqrt(head_total(kk * kk, two_term=True)), 1e-12)
    k = k * (1.0 + (a - 1.0) * ka_ref[...])
    b = kk * a
    bonus_ref[...] = head_total(r * k * rk_ref[...]) * v

    c3 = lambda t: t.reshape(nc, C, t.shape[1])
    sums = _bmm(jnp.broadcast_to(tril_incl.astype(BF16)[None], (nc, C, C)), c3(_hi_lo(lw)))
    G3 = sums[:, :, 0:W] + sums[:, :, W:2 * W]
    total3 = jnp.broadcast_to(G3[:, C - 1:C, :], (nc, C, W))
    dec_ref[...] = jnp.exp2(total3).reshape(tb, W)
    G = G3.reshape(tb, W)
    inv = jnp.exp2(-G)
    to_end = jnp.exp2(total3 - G3).reshape(tb, W)
    kap = kk * jnp.exp2(G - lw)
    rho = r * jnp.exp2(G)
    kap3, rho3, v3 = c3(kap), c3(rho), c3(v)
    bet_kt3 = jnp.concatenate([c3(b * inv), c3(k * inv)], axis=1)
    betc3, kc3 = c3(b * to_end), c3(k * to_end)

    def bmm_nt(p, q):
        return lax.dot_general(p.astype(BF16), q.astype(BF16), (((2,), (2,)), ((0,), (0,))),
                               preferred_element_type=F32)

    def bmm_tn(p, q):
        return lax.dot_general(p.astype(BF16), q.astype(BF16), (((1,), (1,)), ((0,), (0,))),
                               preferred_element_type=F32)

    wi = lax.broadcasted_iota(jnp.int32, (C, 2 * C), 0)
    wj = lax.broadcasted_iota(jnp.int32, (C, 2 * C), 1) % C
    A_bk, B_bk = [], []
    for h in range(H):
        lhs = jnp.concatenate([c3(kap * head_mask[h]), c3(rho * head_mask[h])], axis=1)
        prod = bmm_nt(lhs, bet_kt3)
        A_bk.append(jnp.where(wj < wi, prod[:, 0:C], 0.0))
        B_bk.append(jnp.where(wj <= wi, prod[:, C:2 * C], 0.0))
    A_b = jnp.concatenate([t[:, :, 0:C] for t in A_bk], axis=0)
    B_b = [t[:, :, 0:C] for t in B_bk]
    v3_low = jnp.concatenate([jnp.zeros_like(v3), v3], axis=1)

    Dg = jnp.where(same_sub, A_b, 0.0)
    Lo = A_b - Dg
    D2 = _bmm(Dg, Dg)
    D4 = _bmm(D2, D2)
    D8 = _bmm(D4, D4)
    Dinv = _bmm(_bmm(_bmm(eye - Dg, eye + D2), eye + D4), eye + D8)
    Nn = _bmm(Dinv, Lo)
    N2 = _bmm(Nn, Nn)
    Tm = _bmm(_bmm(eye - Nn, eye + N2), Dinv)
    Tm = [Tm[h * nc:(h + 1) * nc] for h in range(H)]

    def per_head(mats, t):
        reps = t.shape[2] // W
        acc = None
        for h in range(H):
            m = head_mask[h] if reps == 1 else jnp.concatenate([head_mask[h]] * reps, axis=1)
            u = m * _bmm(mats[h], t)
            acc = u if acc is None else acc + u
        return acc

    akv = per_head(A_bk, v3_low)
    tk = per_head(Tm, jnp.concatenate([kap3, akv], axis=2))
    kap_p = tk[:, :, 0:W]
    v_p = tk[:, :, W:2 * W]
    bb = per_head(B_b, jnp.concatenate([kap_p, v_p], axis=2))
    qeff_ref[...] = (rho3 - bb[:, :, 0:W]).astype(BF16)
    yloc_ref[...] = per_head(B_bk, v3_low) - bb[:, :, W:2 * W]
    mlow_ref[...] = jnp.where(block_diag, bmm_tn(kap_p, betc3), 0.0).astype(BF16)
    nc_ref[...] = jnp.where(block_diag, bmm_tn(jnp.concatenate([v3, v_p], axis=1),
                                               jnp.concatenate([kc3, -betc3], axis=1)), 0.0)

    def chunk(ci, carry):
        r0 = pl.multiple_of(ci * C, C)
        s = s_ref[...]
        sb = s.astype(BF16)
        y = lax.dot_general(qeff_ref[ci], sb, (((1,), (1,)), ((), ())), preferred_element_type=F32)
        y_ref[pl.ds(r0, C), :] = y + yloc_ref[ci]
        s_ref[...] = (s * dec_ref[pl.ds(r0, 8), :][0:1, :]
                      - jnp.dot(sb, mlow_ref[ci], preferred_element_type=F32) + nc_ref[ci])
        return carry

    lax.fori_loop(0, nc, chunk, 0, unroll=unroll)

    y = y_ref[...]
    mean = head_total(y, two_term=True) / N
    yc = y - mean
    var = head_total(yc * yc) / N
    yn = yc * lax.rsqrt(var + RWKV_GN_EPS) * lng_ref[...] + lnb_ref[...]
    o_ref[0] = ((yn + bonus_ref[...]) * gate_ref[...]).astype(o_ref.dtype)


N_GLA_INPUTS = 7
N_GLA_SCRATCH = 6
N_RWKV_INPUTS = 12


def _gla_rwkv_kernel(*refs, gla_chunks, rwkv_chunks):
    n_in = N_GLA_INPUTS + N_RWKV_INPUTS
    gla_in, rwkv_in = refs[0:N_GLA_INPUTS], refs[N_GLA_INPUTS:n_in]
    gla_out, rwkv_out = refs[n_in], refs[n_in + 1]
    gla_scratch = refs[n_in + 2:n_in + 2 + N_GLA_SCRATCH]
    rwkv_scratch = refs[n_in + 2 + N_GLA_SCRATCH:]
    _gla_kernel(*gla_in, gla_out, *gla_scratch, n_chunk=gla_chunks, unroll=True)
    _rwkv_kernel(*rwkv_in, rwkv_out, *rwkv_scratch, n_chunk=rwkv_chunks, unroll=True)


def _gla_rwkv7(proj, gla_w2p, gla_gb, gla_ng, col_qk, col_v, col_og, col_gz, vecs, w2p, a2p, g2, col_rwkv, tb):
    B, T, _ = proj.shape
    hv = gla_ng.shape[1]
    hk = gla_gb.shape[1]
    W = g2.shape[1]
    cols = vecs["mu"].shape[1]
    names = ("mu", "w0", "w2", "a0", "a2", "g2", "kk", "ka", "rk", "lng", "lnb")
    params = dict(vecs, w2=w2p, a2=a2p, g2=g2)
    assert len(names) + 1 == N_RWKV_INPUTS
    const = lambda arr: pl.BlockSpec(arr.shape, lambda b, j: (0, 0))
    rows = lambda n, col: pl.BlockSpec((1, tb, n), lambda b, j: (b, j, col))
    n_chunk = tb // RWKV_CHUNK
    return pl.pallas_call(
        functools.partial(_gla_rwkv_kernel, gla_chunks=tb // GLA_CHUNK, rwkv_chunks=n_chunk),
        out_shape=(jax.ShapeDtypeStruct((B, T, hv), BF16), jax.ShapeDtypeStruct((B, T, W), BF16)),
        grid=(B, T // tb),
        in_specs=[rows(2 * hk, col_qk), rows(hv, col_v), rows(hv, col_og), rows(LANES, col_gz),
                  const(gla_w2p), const(gla_gb), const(gla_ng), rows(cols, col_rwkv)]
                 + [const(params[n]) for n in names],
        out_specs=(rows(hv, 0), rows(W, 0)),
        scratch_shapes=[pltpu.VMEM((hv, hk), F32), pltpu.VMEM((tb, hk), F32), pltpu.VMEM((tb, hk), F32),
                        pltpu.VMEM((tb, hk), BF16), pltpu.VMEM((tb, hk), BF16), pltpu.VMEM((tb, hv), F32),
                        pltpu.VMEM((W, W), F32), pltpu.VMEM((1, cols), F32),
                        pltpu.VMEM((tb, W), F32), pltpu.VMEM((tb, W), F32), pltpu.VMEM((tb, W), F32),
                        pltpu.VMEM((n_chunk, RWKV_CHUNK, W), BF16), pltpu.VMEM((n_chunk, RWKV_CHUNK, W), F32),
                        pltpu.VMEM((n_chunk, W, W), BF16), pltpu.VMEM((n_chunk, W, W), F32),
                        pltpu.VMEM((tb, W), F32)],
        compiler_params=_cparams(("arbitrary", "arbitrary")),
        name="gla_rwkv7_mixers",
    )(proj, proj, proj, proj, gla_w2p, gla_gb, gla_ng, proj, *[params[n] for n in names])


def _merge_kernel(x_ref, sc_ref, sh_ref, gt_ref, gpre_ref, gpost_ref, wg_ref, b0_ref, b1_ref, b2_ref,
                  b3_ref, wb_ref, wo_ref, o_ref):
    x = x_ref[0]
    D = x.shape[1]
    h = _rms_mod(x, gpre_ref[...], sc_ref[0], sh_ref[0]).astype(BF16)
    merged = None
    for g, br in enumerate((b0_ref, b1_ref, b2_ref, b3_ref)):
        gate = _sigmoid(jnp.dot(h, wg_ref[:, g * D:(g + 1) * D], preferred_element_type=F32))
        t = gate * jnp.dot(br[0], wb_ref[g], preferred_element_type=F32)
        merged = t if merged is None else merged + t
    y = _mm(merged, wo_ref[...])
    y = y * lax.rsqrt(jnp.mean(y * y, axis=-1, keepdims=True) + RMS_EPS) * gpost_ref[...]
    o_ref[0] = x + gt_ref[0] * y


def _merge(x, sc, sh, gt, gpre, gpost, w_gate, branches, w_branch, w_out, tm):
    B, T, D = x.shape
    bw = branches[0].shape[2]
    tok = lambda n: pl.BlockSpec((1, tm, n), lambda b, i: (b, i, 0))
    vec = pl.BlockSpec((1, 1, D), lambda b, i: (b, 0, 0))
    const2 = lambda arr: pl.BlockSpec(arr.shape, lambda b, i: (0, 0))
    return pl.pallas_call(
        _merge_kernel,
        out_shape=jax.ShapeDtypeStruct((B, T, D), F32),
        grid=(B, T // tm),
        in_specs=[tok(D), vec, vec, vec, const2(gpre), const2(gpost), const2(w_gate),
                  tok(bw), tok(bw), tok(bw), tok(bw),
                  pl.BlockSpec(w_branch.shape, lambda b, i: (0, 0, 0)), const2(w_out)],
        out_specs=tok(D),
        compiler_params=_cparams(("arbitrary", "arbitrary")),
        name="merge_out_proj",
    )(x, sc, sh, gt, gpre, gpost, w_gate, *branches, w_branch, w_out)


def _ffn_kernel(x_ref, sc_ref, sh_ref, gt_ref, gpre_ref, gpost_ref, wg_ref, wu_ref, wd_ref, o_ref, *, tf):
    x = x_ref[0]
    h = _rms_mod(x, gpre_ref[...], sc_ref[0], sh_ref[0]).astype(BF16)
    F = wg_ref.shape[1]
    acc = None
    for f0 in range(0, F, tf):
        f1 = min(f0 + tf, F)
        gate = jnp.dot(h, wg_ref[:, f0:f1], preferred_element_type=F32)
        up = jnp.dot(h, wu_ref[:, f0:f1], preferred_element_type=F32)
        t = jnp.dot((_silu(gate) * up).astype(BF16), wd_ref[f0:f1, :], preferred_element_type=F32)
        acc = t if acc is None else acc + t
    y = acc * lax.rsqrt(jnp.mean(acc * acc, axis=-1, keepdims=True) + RMS_EPS) * gpost_ref[...]
    o_ref[0] = x + gt_ref[0] * y


def _dense_ffn(x, sc, sh, gt, gpre, gpost, wg, wu, wd, tm):
    B, T, D = x.shape
    tok = pl.BlockSpec((1, tm, D), lambda b, i: (b, i, 0))
    vec = pl.BlockSpec((1, 1, D), lambda b, i: (b, 0, 0))
    const2 = lambda arr: pl.BlockSpec(arr.shape, lambda b, i: (0, 0))
    return pl.pallas_call(
        functools.partial(_ffn_kernel, tf=FFN_CHUNK),
        out_shape=jax.ShapeDtypeStruct((B, T, D), F32),
        grid=(B, T // tm),
        in_specs=[tok, vec, vec, vec, const2(gpre), const2(gpost), const2(wg), const2(wu), const2(wd)],
        out_specs=tok,
        compiler_params=_cparams(("arbitrary", "arbitrary")),
        name="dense_swiglu",
    )(x, sc, sh, gt, gpre, gpost, wg, wu, wd)


MOE_TOKEN_TILE = 256
MOE_ROW_TILE = 512
SEG_ALIGN = 16
SEG_PIECES = (256, 128, 64, 32, 16)
MOE_SMALL_SEG = 128


def _route_kernel(x_ref, sc_ref, sh_ref, gpre_ref, rw_ref, rb_ref, h_ref, mi_ref, mp_ref, cnt_ref):
    tm = x_ref.shape[1]
    h = _rms_mod(x_ref[0], gpre_ref[...], sc_ref[0], sh_ref[0])
    h_ref[...] = h.astype(BF16)
    h_hi = h.astype(BF16)
    h_lo = (h - h_hi.astype(F32)).astype(BF16)
    w = rw_ref[...]
    w_hi = w.astype(BF16)
    w_lo = (w - w_hi.astype(F32)).astype(BF16)
    logits = jnp.dot(jnp.concatenate([h_hi, h_lo, h_hi], axis=1), jnp.concatenate([w_hi, w_hi, w_lo], axis=0),
                     preferred_element_type=F32) + rb_ref[...]
    lane = lax.broadcasted_iota(jnp.int32, logits.shape, 1)
    v1 = jnp.max(logits, axis=-1, keepdims=True)
    i1 = jnp.min(jnp.where(logits == v1, lane, LANES), axis=-1, keepdims=True)
    rest = jnp.where(lane == i1, -jnp.inf, logits)
    v2 = jnp.max(rest, axis=-1, keepdims=True)
    i2 = jnp.min(jnp.where(rest == v2, lane, LANES), axis=-1, keepdims=True)
    e2 = jnp.exp(v2 - v1)
    p1 = 1.0 / (1.0 + e2)
    p2 = e2 / (1.0 + e2)
    oh1 = (lane == i1).astype(F32)
    oh2 = (lane == i2).astype(F32)
    both = oh1 + oh2
    earlier = (lax.broadcasted_iota(jnp.int32, (tm, tm), 1)
               < lax.broadcasted_iota(jnp.int32, (tm, tm), 0)).astype(BF16)
    before = jnp.dot(earlier, both.astype(BF16), preferred_element_type=F32)
    r1 = jnp.sum(oh1 * before, axis=-1, keepdims=True).astype(jnp.int32)
    r2 = jnp.sum(oh2 * before, axis=-1, keepdims=True).astype(jnp.int32)
    col = lax.broadcasted_iota(jnp.int32, mi_ref.shape, 1)
    mi_ref[...] = jnp.where(col == 0, i1, jnp.where(col == 1, i2, jnp.where(col == 2, r1,
                                                                           jnp.where(col == 3, r2, 0))))
    mp_ref[...] = jnp.where(col == 0, p1, jnp.where(col == 1, p2, 0.0))
    cnt_ref[0] = jnp.sum(both, axis=0, keepdims=True).astype(jnp.int32)


def _segment_pieces(n_rows):
    out = []
    for s in SEG_PIECES:
        if s == MOE_TOKEN_TILE:
            out.append((n_rows == s, 0, s))
        else:
            out.append(((n_rows & s) != 0, pl.multiple_of((n_rows // (2 * s)) * (2 * s), SEG_ALIGN), s))
    return out


def _all_segments_small(cnt_ref, tile):
    most = cnt_ref[tile * N_EXPERTS]
    for e in range(1, N_EXPERTS):
        most = jnp.maximum(most, cnt_ref[tile * N_EXPERTS + e])
    return most <= MOE_SMALL_SEG


def _dispatch_kernel(seg_ref, cnt_ref, h_ref, mit_ref, init_ref, xs_ref, buf_ref, sem):
    del init_ref
    tm = h_ref.shape[0]
    i = pl.program_id(0)
    e1, e2 = mit_ref[0:1, :], mit_ref[1:2, :]
    r1, r2 = mit_ref[2:3, :], mit_ref[3:4, :]
    slot = i % 2

    def compact(cap):
        row = lax.broadcasted_iota(jnp.int32, (cap, tm), 0)
        select = jnp.concatenate(
            [jnp.logical_or(jnp.logical_and(e1 == e, r1 == row), jnp.logical_and(e2 == e, r2 == row))
             for e in range(N_EXPERTS)], axis=0).astype(BF16)
        rows = jnp.dot(select, h_ref[...], preferred_element_type=F32).astype(BF16)
        for e in range(N_EXPERTS):
            buf_ref[slot, e * tm:e * tm + cap, :] = rows[e * cap:(e + 1) * cap]

    small = _all_segments_small(cnt_ref, i)
    pl.when(small)(functools.partial(compact, MOE_SMALL_SEG))
    pl.when(jnp.logical_not(small))(functools.partial(compact, tm))

    def segment_copies(tile, buf, e):
        n = cnt_ref[tile * N_EXPERTS + e]
        n_rows = ((n + SEG_ALIGN - 1) // SEG_ALIGN) * SEG_ALIGN
        dst = pl.multiple_of(seg_ref[tile * N_EXPERTS + e], SEG_ALIGN)
        return [(cond, pltpu.make_async_copy(buf_ref.at[buf, pl.ds(e * tm + off, s), :],
                                             xs_ref.at[pl.ds(dst + off, s), :], sem))
                for cond, off, s in _segment_pieces(n_rows)]

    def for_all_segments(tile, buf, action):
        for e in range(N_EXPERTS):
            for cond, cp in segment_copies(tile, buf, e):
                pl.when(cond)(getattr(cp, action))

    @pl.when(i > 0)
    def _():
        for_all_segments(i - 1, 1 - slot, "wait")

    for_all_segments(i, slot, "start")

    @pl.when(i == pl.num_programs(0) - 1)
    def _():
        for_all_segments(i, slot, "wait")


def _expert_kernel(te_ref, nv_ref, xs_ref, wg_ref, wu_ref, wd_ref, ys_ref, acc_ref):
    del te_ref
    r = pl.program_id(0)
    f = pl.program_id(1)
    valid = r < nv_ref[0]

    @pl.when(jnp.logical_and(valid, f == 0))
    def _():
        acc_ref[...] = jnp.zeros_like(acc_ref)

    @pl.when(valid)
    def _():
        x = xs_ref[...]
        tf = wg_ref.shape[2]
        acc = acc_ref[...]
        for f0 in range(0, tf, FFN_CHUNK):
            f1 = min(f0 + FFN_CHUNK, tf)
            gate = jnp.dot(x, wg_ref[0, :, f0:f1], preferred_element_type=F32)
            up = jnp.dot(x, wu_ref[0, :, f0:f1], preferred_element_type=F32)
            acc = acc + jnp.dot((_silu(gate) * up).astype(BF16), wd_ref[0, f0:f1, :],
                                preferred_element_type=F32)
        acc_ref[...] = acc

    @pl.when(f == pl.num_programs(1) - 1)
    def _():
        @pl.when(valid)
        def _():
            ys_ref[...] = acc_ref[...].astype(ys_ref.dtype)

        @pl.when(jnp.logical_not(valid))
        def _():
            ys_ref[...] = jnp.zeros_like(ys_ref)


def _combine_kernel(seg_ref, cnt_ref, ys_ref, mi_ref, mp_ref, x_ref, gt_ref, gpost_ref, o_ref, win_ref,
                    pair_ref, sem):
    tm = x_ref.shape[1]
    i = pl.program_id(0) * pl.num_programs(1) + pl.program_id(1)
    n_tiles = pl.num_programs(0) * pl.num_programs(1)
    slot = i % 2

    def window_copy(tile, buf, e, rows):
        src = pl.multiple_of(seg_ref[tile * N_EXPERTS + e], SEG_ALIGN)
        return pltpu.make_async_copy(ys_ref.at[pl.ds(src, rows), :], win_ref.at[buf, pl.ds(e * rows, rows), :],
                                     sem.at[buf, e])

    def for_all_windows(tile, buf, action):
        small = _all_segments_small(cnt_ref, tile)
        for rows, cond in ((MOE_SMALL_SEG, small), (tm, jnp.logical_not(small))):
            @pl.when(cond)
            def _():
                for e in range(N_EXPERTS):
                    getattr(window_copy(tile, buf, e, rows), action)()

    @pl.when(i == 0)
    def _():
        for_all_windows(i, slot, "start")

    @pl.when(i + 1 < n_tiles)
    def _():
        for_all_windows(i + 1, 1 - slot, "start")

    for_all_windows(i, slot, "wait")
    e1, e2 = mi_ref[:, 0:1], mi_ref[:, 1:2]
    r1, r2 = mi_ref[:, 2:3], mi_ref[:, 3:4]

    def expand_rows(cap):
        col = lax.broadcasted_iota(jnp.int32, (tm, cap), 1)
        expand = jnp.concatenate(
            [jnp.concatenate([jnp.logical_and(e1 == e, r1 == col), jnp.logical_and(e2 == e, r2 == col)], axis=0)
             for e in range(N_EXPERTS)], axis=1).astype(BF16)
        pair_ref[...] = jnp.dot(expand, win_ref[slot, 0:N_EXPERTS * cap, :], preferred_element_type=F32)

    small = _all_segments_small(cnt_ref, i)
    pl.when(small)(functools.partial(expand_rows, MOE_SMALL_SEG))
    pl.when(jnp.logical_not(small))(functools.partial(expand_rows, tm))
    y = mp_ref[:, 0:1] * pair_ref[0:tm, :] + mp_ref[:, 1:2] * pair_ref[tm:2 * tm, :]
    y = y * lax.rsqrt(jnp.mean(y * y, axis=-1, keepdims=True) + RMS_EPS) * gpost_ref[...]
    o_ref[0] = x_ref[0] + gt_ref[0] * y


def _moe_ffn(x, sc, sh, gt, gpre, gpost, rw, rb, wg, wu, wd, tf):
    B, T, D = x.shape
    E, _, F = wg.shape
    tm = min(MOE_TOKEN_TILE, T)
    assert tm == MOE_TOKEN_TILE and E == N_EXPERTS
    nT = T // tm
    n_tok_tiles = B * nT
    N = B * T
    max_rows = 2 * N + n_tok_tiles * E * (SEG_ALIGN - 1) + E * (MOE_ROW_TILE - SEG_ALIGN)
    n_row_tiles = -(-max_rows // MOE_ROW_TILE) + 1
    P = n_row_tiles * MOE_ROW_TILE

    vec = pl.BlockSpec((1, 1, D), lambda b, j: (b, 0, 0))
    const2 = lambda arr: pl.BlockSpec(arr.shape, lambda b, j: (0, 0))
    flat = lambda n: pl.BlockSpec((tm, n), lambda b, j: (b * nT + j, 0))
    h, mi, mp, cnt = pl.pallas_call(
        _route_kernel,
        out_shape=(jax.ShapeDtypeStruct((N, D), BF16), jax.ShapeDtypeStruct((N, 8), jnp.int32),
                   jax.ShapeDtypeStruct((N, 8), F32), jax.ShapeDtypeStruct((n_tok_tiles, 1, LANES), jnp.int32)),
        grid=(B, nT),
        in_specs=[pl.BlockSpec((1, tm, D), lambda b, j: (b, j, 0)), vec, vec, const2(gpre), const2(rw),
                  const2(rb)],
        out_specs=(flat(D), flat(8), flat(8), pl.BlockSpec((1, 1, LANES), lambda b, j: (b * nT + j, 0, 0))),
        compiler_params=_cparams(("arbitrary", "arbitrary")),
        name="moe_route",
    )(x, sc, sh, gpre, rw, rb)

    counts = cnt[:, 0, :E]
    seg_len = (counts + SEG_ALIGN - 1) // SEG_ALIGN * SEG_ALIGN
    group_len = (jnp.sum(seg_len, axis=0) + MOE_ROW_TILE - 1) // MOE_ROW_TILE * MOE_ROW_TILE
    group_end = jnp.cumsum(group_len)
    seg_start = (group_end - group_len)[None, :] + jnp.cumsum(seg_len, axis=0) - seg_len
    seg_start = seg_start.reshape(-1).astype(jnp.int32)
    counts = counts.reshape(-1)
    n_valid = (group_end[-1:] // MOE_ROW_TILE).astype(jnp.int32)
    tile_first_row = jnp.arange(n_row_tiles, dtype=jnp.int32) * MOE_ROW_TILE
    tile_expert = jnp.minimum(jnp.sum(tile_first_row[:, None] >= group_end[None, :], axis=1), E - 1)
    tile_expert = tile_expert.astype(jnp.int32)

    xs = pl.pallas_call(
        _dispatch_kernel,
        out_shape=jax.ShapeDtypeStruct((P, D), BF16),
        grid_spec=pltpu.PrefetchScalarGridSpec(
            num_scalar_prefetch=2,
            grid=(n_tok_tiles,),
            in_specs=[pl.BlockSpec((tm, D), lambda i, seg, n: (i, 0)),
                      pl.BlockSpec((8, tm), lambda i, seg, n: (0, i)),
                      pl.BlockSpec(memory_space=pl.ANY)],
            out_specs=pl.BlockSpec(memory_space=pl.ANY),
            scratch_shapes=[pltpu.VMEM((2, E * tm, D), BF16), pltpu.SemaphoreType.DMA(())]),
        input_output_aliases={4: 0},
        compiler_params=_cparams(("arbitrary",)),
        name="moe_dispatch",
    )(seg_start, counts, h, mi.T, jnp.zeros((P, D), BF16))

    nf = F // tf
    live = lambda r, f, nv: jnp.where(r < nv[0], f, nf - 1)
    ys = pl.pallas_call(
        _expert_kernel,
        out_shape=jax.ShapeDtypeStruct((P, D), BF16),
        grid_spec=pltpu.PrefetchScalarGridSpec(
            num_scalar_prefetch=2,
            grid=(n_row_tiles, nf),
            in_specs=[pl.BlockSpec((MOE_ROW_TILE, D), lambda r, f, te, nv: (r, 0)),
                      pl.BlockSpec((1, D, tf), lambda r, f, te, nv: (te[r], 0, live(r, f, nv))),
                      pl.BlockSpec((1, D, tf), lambda r, f, te, nv: (te[r], 0, live(r, f, nv))),
                      pl.BlockSpec((1, tf, D), lambda r, f, te, nv: (te[r], live(r, f, nv), 0))],
            out_specs=pl.BlockSpec((MOE_ROW_TILE, D), lambda r, f, te, nv: (r, 0)),
            scratch_shapes=[pltpu.VMEM((MOE_ROW_TILE, D), F32)]),
        compiler_params=_cparams(("arbitrary", "arbitrary")),
        name="moe_experts",
    )(tile_expert, n_valid, xs, wg, wu, wd)

    return pl.pallas_call(
        _combine_kernel,
        out_shape=jax.ShapeDtypeStruct((B, T, D), F32),
        grid_spec=pltpu.PrefetchScalarGridSpec(
            num_scalar_prefetch=2,
            grid=(B, nT),
            in_specs=[pl.BlockSpec(memory_space=pl.ANY),
                      pl.BlockSpec((tm, 8), lambda b, j, seg, n: (b * nT + j, 0)),
                      pl.BlockSpec((tm, 8), lambda b, j, seg, n: (b * nT + j, 0)),
                      pl.BlockSpec((1, tm, D), lambda b, j, seg, n: (b, j, 0)),
                      pl.BlockSpec((1, 1, D), lambda b, j, seg, n: (b, 0, 0)),
                      pl.BlockSpec(gpost.shape, lambda b, j, seg, n: (0, 0))],
            out_specs=pl.BlockSpec((1, tm, D), lambda b, j, seg, n: (b, j, 0)),
            scratch_shapes=[pltpu.VMEM((2, E * tm, D), BF16), pltpu.VMEM((2 * tm, D), F32),
                            pltpu.SemaphoreType.DMA((2, E))]),
        compiler_params=_cparams(("arbitrary", "arbitrary")),
        name="moe_combine",
    )(seg_start, counts, ys, mi, mp, x, gt, gpost)


def _rope_tables(positions, groups):
    d = 32
    inv = 1.0 / (ROPE_THETA ** (jnp.arange(0, d, 2, dtype=F32) / d))
    ang = positions.astype(F32)[..., None] * inv
    cos, sin = jnp.cos(ang), jnp.sin(ang)
    cos = jnp.tile(jnp.concatenate([cos, cos], axis=-1), (1, 1, groups))
    sin = jnp.tile(jnp.concatenate([-sin, sin], axis=-1), (1, 1, groups))
    return cos, sin


def _pad_rows(w, rows, offset):
    out = jnp.zeros((rows, w.shape[1]), w.dtype)
    return out.at[offset:offset + w.shape[0]].set(w)


def kernel(x, c, positions, ada_w, ada_b, norm_mix_pre, norm_mix_post, norm_ffn_pre, norm_ffn_post, w_in, gla_gate_w2, gla_gate_b, gla_norm, diff_lambda, diff_subln, conv_w, conv_b, conv_ln_g, conv_ln_b, rwkv_mu, rwkv_w0, rwkv_w2, rwkv_a0, rwkv_a2, rwkv_g2, rwkv_k_k, rwkv_k_a, rwkv_r_k, rwkv_ln_g, rwkv_ln_b, w_branch, w_out, ffn_w_gate, ffn_w_up, ffn_w_down, router_w, router_b, moe_w_gate, moe_w_up, moe_w_down):
    B, T, D = x.shape
    L = ada_w.shape[0]
    W = D // N_BRANCH
    hk = gla_gate_b.shape[1]
    decay_rank = rwkv_w2.shape[1]
    a_rank = rwkv_a2.shape[1]
    gate_rank = rwkv_g2.shape[1]
    assert decay_rank + a_rank == LANES and 2 * hk == W and gate_rank == LANES
    n_mix = (3 * W + decay_rank + a_rank + gate_rank) + 3 * W + 3 * W
    sizes = (hk, hk, W, W, GLA_GATE_RANK, W, W, W, W, W, 3 * W + LANES + gate_rank, N_BRANCH * D)
    offs = [0]
    for s in sizes:
        offs.append(offs[-1] + s)
    assert offs[-1] == w_in.shape[2]
    tm = min(TOKEN_TILE, T)
    tb = min(SEQ_BLOCK, T)

    mod = _modulation(c, ada_w, ada_b)
    cos, sin = _rope_tables(positions, LANES // 32)

    for l in range(L):
        m = mod[l].reshape(B, 1, 6 * D)
        sh_m, sc_m, gt_m, sh_f, sc_f, gt_f = [m[:, :, i * D:(i + 1) * D] for i in range(6)]

        wl = w_in[l]
        gz_cols = jnp.zeros((D, LANES), F32).at[:, :GLA_GATE_RANK].set(wl[:, offs[4]:offs[5]])
        w_mix = jnp.concatenate([wl[:, offs[10]:offs[11]], wl[:, offs[0]:offs[4]], wl[:, offs[5]:offs[8]],
                                 gz_cols], axis=1).astype(BF16)
        w_conv = wl[:, offs[8]:offs[10]].astype(BF16)
        w_gate = wl[:, offs[11]:offs[12]].astype(BF16)
        proj, o_conv = _in_projection_conv(x, sc_m, sh_m, norm_mix_pre[l][None], w_conv, w_mix,
                                           _pad_rows(conv_w[l], 32, 0), conv_b[l][None],
                                           conv_ln_g[l][None], conv_ln_b[l][None], tm)

        o_diff = _diff_attention(proj, cos, sin, diff_lambda[l],
                                 diff_subln[l][None], col_q=7, col_k=8, col_v=9, layer_idx=l)
        vecs = dict(mu=rwkv_mu[l][None], w0=rwkv_w0[l][None], a0=rwkv_a0[l][None],
                    kk=rwkv_k_k[l][None], ka=rwkv_k_a[l][None], rk=rwkv_r_k[l].reshape(1, W),
                    lng=rwkv_ln_g[l][None], lnb=rwkv_ln_b[l][None])
        o_gla, o_rwkv = _gla_rwkv7(
            proj, _pad_rows(gla_gate_w2[l], LANES, 0), gla_gate_b[l][None],
            jnp.tile(gla_norm[l], GLA_HEADS)[None], 4, 5, 6, n_mix // LANES,
            vecs, _pad_rows(rwkv_w2[l], LANES, 0).astype(BF16),
            _pad_rows(rwkv_a2[l], LANES, decay_rank).astype(BF16), rwkv_g2[l].astype(BF16), 0, tb)
        x = _merge(x, sc_m, sh_m, gt_m, norm_mix_pre[l][None], norm_mix_post[l][None], w_gate,
                   (o_gla, o_diff, o_conv, o_rwkv), w_branch[l].astype(BF16), w_out[l].astype(BF16), tm)

        i = l // 2
        if l % 2 == 0:
            x = _dense_ffn(x, sc_f, sh_f, gt_f, norm_ffn_pre[l][None], norm_ffn_post[l][None],
                           ffn_w_gate[i].astype(BF16), ffn_w_up[i].astype(BF16),
                           ffn_w_down[i].astype(BF16), tm)
        else:
            rw = jnp.zeros((D, LANES), F32).at[:, :N_EXPERTS].set(router_w[i])
            rb = jnp.full((1, LANES), -jnp.inf, F32).at[0, :N_EXPERTS].set(router_b[i])
            x = _moe_ffn(x, sc_f, sh_f, gt_f, norm_ffn_pre[l][None], norm_ffn_post[l][None], rw, rb,
                         moe_w_gate[i].astype(BF16), moe_w_up[i].astype(BF16),
                         moe_w_down[i].astype(BF16), tf=moe_w_gate.shape[3])
    return x
```

```python
import functools
import math

import jax
import jax.numpy as jnp
from jax import lax
from jax.experimental import pallas as pl
from jax.experimental.pallas import tpu as pltpu

F32 = jnp.float32
BF16 = jnp.bfloat16
HIGHEST = lax.Precision.HIGHEST
LOG2_E = math.log2(math.e)

N_BRANCH = 4
GLA_HEADS = 4
GLA_GATE_RANK = 16
GLA_GATE_NORMALIZER = 16.0
GLA_CHUNK = 32
DIFF_HEADS = 4
ROPE_THETA = 10000.0
CONV_WIDTH = 31
RWKV_HEADS = 4
RWKV_CHUNK = 64
RWKV_SUB = 16
N_EXPERTS = 8
RMS_EPS = 1e-6
LN_EPS = 1e-5
RWKV_GN_EPS = 64e-5
LANES = 128
SUBLANES = 8
VMEM_LIMIT = 56 * 1024 * 1024

TOKEN_TILE = 512
SEQ_BLOCK = 512
ATTN_BLOCK = 512
CONV_ROWS = 128
FFN_CHUNK = 512


def _cparams(sem):
    return pltpu.CompilerParams(dimension_semantics=sem, vmem_limit_bytes=VMEM_LIMIT)


def _mm(a, b):
    return jnp.dot(a.astype(BF16), b.astype(BF16), preferred_element_type=F32)


def _bmm(a, b):
    return lax.dot_general(a.astype(BF16), b.astype(BF16), (((2,), (1,)), ((0,), (0,))),
                           preferred_element_type=F32)


def _mm_f32(a, b):
    return jnp.dot(a, b, precision=HIGHEST, preferred_element_type=F32)


def _hi_lo(x):
    hi = x.astype(BF16)
    return jnp.concatenate([hi, (x - hi.astype(F32)).astype(BF16)], axis=1)


def _sigmoid(x):
    return 0.5 * jnp.tanh(0.5 * x) + 0.5


def _silu(x):
    return x * _sigmoid(x)


def _softplus(x):
    return jnp.maximum(x, 0.0) + jnp.log(1.0 + jnp.exp(-jnp.abs(x)))


def _group_matrix(n, group):
    r = lax.broadcasted_iota(jnp.int32, (n, n), 0) // group
    c = lax.broadcasted_iota(jnp.int32, (n, n), 1) // group
    return r == c


def _rms_mod(x, gain, scale, shift):
    y = x * lax.rsqrt(jnp.mean(x * x, axis=-1, keepdims=True) + RMS_EPS)
    return y * gain * (1.0 + scale) + shift


def _mod_kernel(c_ref, w_ref, b_ref, o_ref):
    o_ref[0] = _mm_f32(_silu(c_ref[...]), w_ref[0]) + b_ref[0]


def _modulation(c, ada_w, ada_b):
    L, D, M = ada_w.shape
    B = c.shape[0]
    tn = M // 4
    return pl.pallas_call(
        _mod_kernel,
        out_shape=jax.ShapeDtypeStruct((L, B, M), F32),
        grid=(L, M // tn),
        in_specs=[pl.BlockSpec((B, D), lambda l, j: (0, 0)),
                  pl.BlockSpec((1, D, tn), lambda l, j: (l, 0, j)),
                  pl.BlockSpec((1, 1, tn), lambda l, j: (l, 0, j))],
        out_specs=pl.BlockSpec((1, B, tn), lambda l, j: (l, 0, j)),
        compiler_params=_cparams(("arbitrary", "arbitrary")),
        name="adaln_mod",
    )(c, ada_w, ada_b.reshape(L, 1, M))


CONV_HALO = 32


def _inproj_conv_kernel(x_ref, sc_ref, sh_ref, g_ref, wab_ref, w_ref, cw_ref, cb_ref, lg_ref, lb_ref,
                        o_ref, oc_ref, u_ref, *, rb):
    tm = x_ref.shape[1]
    ch = oc_ref.shape[2]
    h = _rms_mod(x_ref[0], g_ref[...], sc_ref[0], sh_ref[0]).astype(BF16)

    @pl.when(pl.program_id(1) == 0)
    def _():
        u_ref[0:CONV_HALO, :] = jnp.zeros((CONV_HALO, ch), F32)

    ab = jnp.dot(h, wab_ref[...], preferred_element_type=F32)
    u_ref[CONV_HALO:CONV_HALO + tm, :] = ab[:, 0:ch] * _sigmoid(ab[:, ch:2 * ch])
    o_ref[0] = jnp.dot(h, w_ref[...], preferred_element_type=F32)

    first = CONV_HALO - (CONV_WIDTH - 1)
    for i in range(tm // rb):
        r0 = i * rb
        win = u_ref[r0:r0 + rb + CONV_HALO, :]
        acc = jnp.zeros((rb, ch), F32)
        for s in range(SUBLANES):
            taps = [j for j in range(CONV_WIDTH) if (first + j) % SUBLANES == s]
            rolled = win if s == 0 else pltpu.roll(win, rb + CONV_HALO - s, 0)
            for j in taps:
                a0 = first + j - s
                acc = acc + cw_ref[j:j + 1, :] * rolled[a0:a0 + rb, :]
        y = acc + cb_ref[...]
        mu = jnp.mean(y, axis=-1, keepdims=True)
        yc = y - mu
        var = jnp.mean(yc * yc, axis=-1, keepdims=True)
        y = yc * lax.rsqrt(var + LN_EPS) * lg_ref[...] + lb_ref[...]
        oc_ref[0, r0:r0 + rb, :] = _silu(y).astype(oc_ref.dtype)
    u_ref[0:CONV_HALO, :] = u_ref[tm:tm + CONV_HALO, :]


def _in_projection_conv(x, sc, sh, gain, w_ab, w, conv_w, conv_b, ln_g, ln_b, tm):
    B, T, D = x.shape
    n = w.shape[1]
    ch = conv_w.shape[1]
    const = lambda arr: pl.BlockSpec(arr.shape, lambda b, i: (0, 0))
    return pl.pallas_call(
        functools.partial(_inproj_conv_kernel, rb=min(CONV_ROWS, tm)),
        out_shape=(jax.ShapeDtypeStruct((B, T, n), F32), jax.ShapeDtypeStruct((B, T, ch), BF16)),
        grid=(B, T // tm),
        in_specs=[pl.BlockSpec((1, tm, D), lambda b, i: (b, i, 0)),
                  pl.BlockSpec((1, 1, D), lambda b, i: (b, 0, 0)),
                  pl.BlockSpec((1, 1, D), lambda b, i: (b, 0, 0)),
                  const(gain), const(w_ab), const(w), const(conv_w), const(conv_b), const(ln_g), const(ln_b)],
        out_specs=(pl.BlockSpec((1, tm, n), lambda b, i: (b, i, 0)),
                   pl.BlockSpec((1, tm, ch), lambda b, i: (b, i, 0))),
        scratch_shapes=[pltpu.VMEM((tm + CONV_HALO, ch), F32)],
        compiler_params=_cparams(("arbitrary", "arbitrary")),
        name="in_proj_conv",
    )(x, sc, sh, gain, w_ab, w, conv_w, conv_b, ln_g, ln_b)


def _gla_kernel(qk_ref, v_ref, og_ref, gz_ref, w2_ref, gb_ref, ng_ref, o_ref, s_ref, g_ref, r_ref,
                qd_ref, kd_ref, oacc_ref, *, n_chunk):
    C = GLA_CHUNK
    hk = qk_ref.shape[2] // 2
    hv = v_ref.shape[2]
    dk = hk // GLA_HEADS
    dv = hv // GLA_HEADS

    tb = n_chunk * C

    @pl.when(pl.program_id(1) == 0)
    def _():
        s_ref[...] = jnp.zeros_like(s_ref)

    z = _mm(gz_ref[0], w2_ref[...]) + gb_ref[...]
    gk = (jnp.minimum(z, 0.0) - jnp.log(1.0 + jnp.exp(-jnp.abs(z)))) * (LOG2_E / GLA_GATE_NORMALIZER)
    tri = (lax.broadcasted_iota(jnp.int32, (C, C), 1)
           <= lax.broadcasted_iota(jnp.int32, (C, C), 0)).astype(BF16)
    sums = _bmm(jnp.broadcast_to(tri[None], (n_chunk, C, C)), _hi_lo(gk).reshape(n_chunk, C, 2 * hk))
    G3 = sums[:, :, 0:hk] + sums[:, :, hk:2 * hk]
    G_all = G3.reshape(tb, hk)
    R_all = (jnp.broadcast_to(G3[:, C - 1:C, :], (n_chunk, C, hk)) - G3).reshape(tb, hk)
    g_ref[...] = G_all
    r_ref[...] = R_all
    qd_ref[...] = (qk_ref[0, :, 0:hk] * (dk ** -0.5) * jnp.exp2(G_all)).astype(BF16)
    kd_ref[...] = (qk_ref[0, :, hk:2 * hk] * jnp.exp2(R_all)).astype(BF16)

    causal = (lax.broadcasted_iota(jnp.int32, (C, C, hk), 0)
              <= lax.broadcasted_iota(jnp.int32, (C, C, hk), 1))
    er = lax.broadcasted_iota(jnp.int32, (hk, hv), 0) // dk
    ec = lax.broadcasted_iota(jnp.int32, (hk, hv), 1) // dv
    expand = (er == ec).astype(BF16)
    sr = lax.broadcasted_iota(jnp.int32, (hv, hk), 0) // dv
    scol = lax.broadcasted_iota(jnp.int32, (hv, hk), 1) // dk
    state_mask = sr == scol
    head_mean = (_group_matrix(hv, dv).astype(F32) / dv).astype(BF16)

    def chunk(ci, carry):
        r0 = pl.multiple_of(ci * C, C)
        rows = pl.ds(r0, C)
        q = qk_ref[0, rows, 0:hk] * (dk ** -0.5)
        k = qk_ref[0, rows, hk:2 * hk]
        v = v_ref[0, rows, :]
        G = g_ref[rows, :]
        pair = (C, C, hk)
        g_diff = jnp.broadcast_to(G[None], pair) - jnp.broadcast_to(G[:, None, :], pair)
        decay = jnp.exp2(jnp.where(causal, g_diff, -jnp.inf))
        p = jnp.broadcast_to(q[None], pair) * jnp.broadcast_to(k[:, None, :], pair) * decay
        a_exp = jnp.dot(p.reshape(C * C, hk).astype(BF16), expand, preferred_element_type=F32)
        v_j = jnp.broadcast_to(v[:, None, :], (C, C, hv))
        o_intra = jnp.sum(a_exp.reshape(C, C, hv) * v_j, axis=0)
        g_total = G[0:1, :] + r_ref[pl.ds(r0, 8), :][0:1, :]
        s = s_ref[...]
        o_inter = lax.dot_general(qd_ref[rows, :], s.astype(BF16), (((1,), (1,)), ((), ())),
                                  preferred_element_type=F32)
        kv = lax.dot_general(v.astype(BF16), kd_ref[rows, :], (((0,), (0,)), ((), ())),
                             preferred_element_type=F32)
        s_ref[...] = s * jnp.exp2(g_total) + jnp.where(state_mask, kv, 0.0)
        oacc_ref[rows, :] = o_intra + o_inter
        return carry

    lax.fori_loop(0, n_chunk, chunk, 0, unroll=True)

    o = oacc_ref[...]
    ms = jnp.dot(_hi_lo(o * o), jnp.concatenate([head_mean, head_mean], axis=0),
                 preferred_element_type=F32)
    o = o * lax.rsqrt(ms + RMS_EPS) * ng_ref[...] * _silu(og_ref[0])
    o_ref[0] = o.astype(o_ref.dtype)


def _rope(t, cos, sin_signed):
    d = 32
    half = d // 2
    out = []
    for s in range(t.shape[1] // LANES):
        x = t[:, s * LANES:(s + 1) * LANES]
        lane = lax.broadcasted_iota(jnp.int32, x.shape, 1)
        up = pltpu.roll(x, LANES - half, 1)
        down = pltpu.roll(x, half, 1)
        rot = jnp.where((lane % d) < half, up, down)
        out.append(x * cos + rot * sin_signed)
    return jnp.concatenate(out, axis=1)


def _diff_kernel(q_ref, k_ref, v_ref, cosq_ref, sinq_ref, cosk_ref, sink_ref, lam_ref, g_ref, o_ref,
                 ks, vs, *, tq, lam_init):
    H = DIFF_HEADS
    d = q_ref.shape[2] // (2 * H)
    dv = v_ref.shape[2] // H
    j = pl.program_id(1)

    @pl.when(j == 0)
    def _():
        k = _rope(k_ref[0], cosk_ref[0], sink_ref[0])
        v = v_ref[0]
        for hc in range(2 * H):
            ks[hc] = k[:, hc * d:(hc + 1) * d].astype(BF16)
        ones_col = (lax.broadcasted_iota(jnp.int32, (v.shape[0], dv), 1) == 0).astype(BF16)
        for h in range(H):
            vs[h] = jnp.concatenate([v[:, h * dv:(h + 1) * dv].astype(BF16), ones_col], axis=1)

    q = _rope(q_ref[0], cosq_ref[0], sinq_ref[0]) * (d ** -0.5 * LOG2_E)
    lp = lam_ref[...]
    lam = (jnp.exp(jnp.sum(lp[0:1] * lp[1:2], axis=-1, keepdims=True))
           - jnp.exp(jnp.sum(lp[2:3] * lp[3:4], axis=-1, keepdims=True)) + lam_init)
    on_or_below_diag = (lax.broadcasted_iota(jnp.int32, (tq, tq), 1)
                        <= lax.broadcasted_iota(jnp.int32, (tq, tq), 0))

    def update(qh, kh, vh, m, acc, masked):
        s = lax.dot_general(qh, kh, (((1,), (1,)), ((), ())), preferred_element_type=F32)
        if masked:
            s = jnp.where(on_or_below_diag, s, -jnp.inf)
        m_new = jnp.maximum(m, jnp.max(s, axis=-1, keepdims=True))
        p = jnp.exp2(s - m_new).astype(BF16)
        acc = jnp.exp2(m - m_new) * acc + jnp.dot(p, vh, preferred_element_type=F32)
        return m_new, acc

    qs = [q[:, hc * d:(hc + 1) * d].astype(BF16) for hc in range(2 * H)]

    def kv_block(kb, carry, masked):
        rows = pl.ds(pl.multiple_of(kb * tq, tq), tq)
        out = []
        for hc in range(2 * H):
            m, acc = carry[2 * hc], carry[2 * hc + 1]
            out.extend(update(qs[hc], ks[hc, rows, :], vs[hc // 2, rows, :], m, acc, masked))
        return tuple(out)

    m0 = jnp.full((tq, 1), -jnp.inf, F32)
    a0 = jnp.zeros((tq, 2 * dv), F32)
    carry = lax.fori_loop(0, j, lambda kb, c: kv_block(kb, c, False), (m0, a0) * (2 * H))
    carry = kv_block(j, carry, True)

    for h in range(H):
        a1, a2 = carry[4 * h + 1], carry[4 * h + 3]
        comp = [a[:, 0:dv] / a[:, dv:dv + 1] for a in (a1, a2)]
        o = comp[0] - lam * comp[1]
        o = o * lax.rsqrt(jnp.mean(o * o, axis=-1, keepdims=True) + RMS_EPS)
        o = o * g_ref[...] * (1.0 - lam_init)
        o_ref[0, :, h * dv:(h + 1) * dv] = o.astype(o_ref.dtype)


def _diff_attention(proj, cos, sin, lam_p, g, col_q, col_k, col_v, layer_idx):
    B, T, _ = proj.shape
    H = DIFF_HEADS
    dv = g.shape[1]
    d = dv // 2
    w = 2 * H * d
    assert cos.shape[2] == LANES
    tq = min(ATTN_BLOCK, T)
    lam_init = 0.8 - 0.6 * math.exp(-0.3 * layer_idx)
    blk = lambda col: pl.BlockSpec((1, tq, w), lambda b, j: (b, j, col))
    full = lambda col: pl.BlockSpec((1, T, w), lambda b, j: (b, 0, col))
    tab_blk = pl.BlockSpec((1, tq, LANES), lambda b, j: (b, j, 0))
    tab_full = pl.BlockSpec((1, T, LANES), lambda b, j: (b, 0, 0))
    return pl.pallas_call(
        functools.partial(_diff_kernel, tq=tq, lam_init=lam_init),
        out_shape=jax.ShapeDtypeStruct((B, T, H * dv), BF16),
        grid=(B, T // tq),
        in_specs=[blk(col_q), full(col_k), full(col_v), tab_blk, tab_blk, tab_full, tab_full,
                  pl.BlockSpec(lam_p.shape, lambda b, j: (0, 0)),
                  pl.BlockSpec(g.shape, lambda b, j: (0, 0))],
        out_specs=pl.BlockSpec((1, tq, H * dv), lambda b, j: (b, j, 0)),
        scratch_shapes=[pltpu.VMEM((2 * H, T, d), BF16),
                        pltpu.VMEM((H, T, 2 * dv), BF16)],
        compiler_params=_cparams(("arbitrary", "arbitrary")),
        name="diff_attention",
    )(proj, proj, proj, cos, sin, cos, sin, lam_p, g)


def _rwkv_kernel(x_ref, mu_ref, w0_ref, w2_ref, a0_ref, a2_ref, g2_ref, kk_ref, ka_ref, rk_ref,
                 lng_ref, lnb_ref, o_ref, s_ref, prev_ref, gate_ref, bonus_ref, dec_ref, qeff_ref, yloc_ref,
                 mlow_ref, nc_ref, y_ref, *, n_chunk):
    C = RWKV_CHUNK
    H = RWKV_HEADS
    W = o_ref.shape[2]
    N = W // H

    @pl.when(pl.program_id(1) == 0)
    def _():
        s_ref[...] = jnp.zeros_like(s_ref)
        prev_ref[...] = jnp.zeros_like(prev_ref)

    lane = lax.broadcasted_iota(jnp.int32, (1, W), 1)
    head_mask = [(lane // N == h).astype(F32) for h in range(H)]
    block_diag = _group_matrix(W, N)
    head_sum = block_diag.astype(F32)
    ti = lax.broadcasted_iota(jnp.int32, (C, C), 0)
    tj = lax.broadcasted_iota(jnp.int32, (C, C), 1)
    tril_incl = ti >= tj
    tril_strict = ti > tj
    same_sub = (ti // RWKV_SUB) == (tj // RWKV_SUB)
    eye = (ti == tj).astype(F32)
    nc = n_chunk
    tb = nc * C

    def head_total(t, two_term=False):
        ones = head_sum.astype(BF16)
        if two_term:
            return jnp.dot(_hi_lo(t), jnp.concatenate([ones, ones], axis=0), preferred_element_type=F32)
        return jnp.dot(t.astype(BF16), ones, preferred_element_type=F32)

    x = x_ref[0]
    first_row = lax.broadcasted_iota(jnp.int32, (tb, 1), 0) == 0
    prev = jnp.where(first_row, prev_ref[...], pltpu.roll(x, 1, 0))
    prev_ref[...] = x[tb - 1:tb, :]
    xm = x + (prev - x) * mu_ref[...]
    r = xm[:, 0:W]
    k = xm[:, W:2 * W]
    v = xm[:, 2 * W:3 * W]
    zz = xm[:, 3 * W:3 * W + LANES]
    zg = xm[:, 3 * W + LANES:]
    w = -_softplus(-(w0_ref[...] + _mm(jnp.tanh(zz), w2_ref[...]))) - 0.5
    lw = -jnp.exp(w) * LOG2_E
    a = _sigmoid(a0_ref[...] + _mm(zz, a2_ref[...]))
    gate_ref[...] = _mm(_sigmoid(zg), g2_ref[...])
    kk = k * kk_ref[...]
    kk = kk / jnp.maximum(jnp.sqrt(head_total(kk * kk, two_term=True)), 1e-12)
    k = k * (1.0 + (a - 1.0) * ka_ref[...])
    b = kk * a
    bonus_ref[...] = head_total(r * k * rk_ref[...]) * v

    c3 = lambda t: t.reshape(nc, C, t.shape[1])
    sums = _bmm(jnp.broadcast_to(tril_incl.astype(BF16)[None], (nc, C, C)), c3(_hi_lo(lw)))
    G3 = sums[:, :, 0:W] + sums[:, :, W:2 * W]
    total3 = jnp.broadcast_to(G3[:, C - 1:C, :], (nc, C, W))
    dec_ref[...] = jnp.exp2(total3).reshape(tb, W)
    G = G3.reshape(tb, W)
    inv = jnp.exp2(-G)
    to_end = jnp.exp2(total3 - G3).reshape(tb, W)
    kap = kk * jnp.exp2(G - lw)
    rho = r * jnp.exp2(G)
    kap3, rho3, v3 = c3(kap), c3(rho), c3(v)
    bet_kt3 = jnp.concatenate([c3(b * inv), c3(k * inv)], axis=1)
    betc3, kc3 = c3(b * to_end), c3(k * to_end)

    def bmm_nt(p, q):
        return lax.dot_general(p.astype(BF16), q.astype(BF16), (((2,), (2,)), ((0,), (0,))),
                               preferred_element_type=F32)

    def bmm_tn(p, q):
        return lax.dot_general(p.astype(BF16), q.astype(BF16), (((1,), (1,)), ((0,), (0,))),
                               preferred_element_type=F32)

    wi = lax.broadcasted_iota(jnp.int32, (C, 2 * C), 0)
    wj = lax.broadcasted_iota(jnp.int32, (C, 2 * C), 1) % C
    A_bk, B_bk = [], []
    kap_b, rho_b = kap.astype(BF16), rho.astype(BF16)
    for h in range(H):
        mask_b = head_mask[h].astype(BF16)
        lhs = jnp.concatenate([c3(kap_b * mask_b), c3(rho_b * mask_b)], axis=1)
        prod = bmm_nt(lhs, bet_kt3)
        A_bk.append(jnp.where(wj < wi, prod[:, 0:C], 0.0))
        B_bk.append(jnp.where(wj <= wi, prod[:, C:2 * C], 0.0))
    A_b = jnp.concatenate([t[:, :, 0:C] for t in A_bk], axis=0)
    B_b = [t[:, :, 0:C] for t in B_bk]
    v3_low = jnp.concatenate([jnp.zeros_like(v3), v3], axis=1)

    Dg = jnp.where(same_sub, A_b, 0.0)
    Lo = A_b - Dg
    D2 = _bmm(Dg, Dg)
    D4 = _bmm(D2, D2)
    D8 = _bmm(D4, D4)
    Dinv = _bmm(_bmm(_bmm(eye - Dg, eye + D2), eye + D4), eye + D8)
    Nn = _bmm(Dinv, Lo)
    N2 = _bmm(Nn, Nn)
    Tm = _bmm(_bmm(eye - Nn, eye + N2), Dinv)
    Tm = [Tm[h * nc:(h + 1) * nc] for h in range(H)]

    def per_head(mats, t):
        reps = t.shape[2] // W
        acc = None
        for h in range(H):
            m = head_mask[h] if reps == 1 else jnp.concatenate([head_mask[h]] * reps, axis=1)
            u = m * _bmm(mats[h], t)
            acc = u if acc is None else acc + u
        return acc

    akv = per_head(A_bk, v3_low)
    tk = per_head(Tm, jnp.concatenate([kap3, akv], axis=2))
    kap_p = tk[:, :, 0:W]
    v_p = tk[:, :, W:2 * W]
    bb = per_head(B_b, jnp.concatenate([kap_p, v_p], axis=2))
    qeff_ref[...] = (rho3 - bb[:, :, 0:W]).astype(BF16)
    yloc_ref[...] = per_head(B_bk, v3_low) - bb[:, :, W:2 * W]
    mlow_ref[...] = jnp.where(block_diag, bmm_tn(kap_p, betc3), 0.0).astype(BF16)
    nc_ref[...] = jnp.where(block_diag, bmm_tn(jnp.concatenate([v3, v_p], axis=1),
                                               jnp.concatenate([kc3, -betc3], axis=1)), 0.0)

    def chunk(ci, carry):
        r0 = pl.multiple_of(ci * C, C)
        s = s_ref[...]
        sb = s.astype(BF16)
        y = lax.dot_general(qeff_ref[ci], sb, (((1,), (1,)), ((), ())), preferred_element_type=F32)
        y_ref[pl.ds(r0, C), :] = y + yloc_ref[ci]
        s_ref[...] = (s * dec_ref[pl.ds(r0, 8), :][0:1, :]
                      - jnp.dot(sb, mlow_ref[ci], preferred_element_type=F32) + nc_ref[ci])
        return carry

    lax.fori_loop(0, nc, chunk, 0, unroll=True)

    y = y_ref[...]
    mean = head_total(y, two_term=True) / N
    yc = y - mean
    var = head_total(yc * yc) / N
    yn = yc * lax.rsqrt(var + RWKV_GN_EPS) * lng_ref[...] + lnb_ref[...]
    o_ref[0] = ((yn + bonus_ref[...]) * gate_ref[...]).astype(o_ref.dtype)


N_GLA_INPUTS = 7
N_GLA_SCRATCH = 6
N_RWKV_INPUTS = 12


def _gla_rwkv_kernel(*refs, gla_chunks, rwkv_chunks):
    n_in = N_GLA_INPUTS + N_RWKV_INPUTS
    gla_in, rwkv_in = refs[0:N_GLA_INPUTS], refs[N_GLA_INPUTS:n_in]
    gla_out, rwkv_out = refs[n_in], refs[n_in + 1]
    gla_scratch = refs[n_in + 2:n_in + 2 + N_GLA_SCRATCH]
    rwkv_scratch = refs[n_in + 2 + N_GLA_SCRATCH:]
    _gla_kernel(*gla_in, gla_out, *gla_scratch, n_chunk=gla_chunks)
    _rwkv_kernel(*rwkv_in, rwkv_out, *rwkv_scratch, n_chunk=rwkv_chunks)


def _gla_rwkv7(proj, gla_w2p, gla_gb, gla_ng, col_qk, col_v, col_og, col_gz, vecs, w2p, a2p, g2, col_rwkv, tb):
    B, T, _ = proj.shape
    hv = gla_ng.shape[1]
    hk = gla_gb.shape[1]
    W = g2.shape[1]
    cols = vecs["mu"].shape[1]
    names = ("mu", "w0", "w2", "a0", "a2", "g2", "kk", "ka", "rk", "lng", "lnb")
    params = dict(vecs, w2=w2p, a2=a2p, g2=g2)
    assert len(names) + 1 == N_RWKV_INPUTS
    const = lambda arr: pl.BlockSpec(arr.shape, lambda b, j: (0, 0))
    rows = lambda n, col: pl.BlockSpec((1, tb, n), lambda b, j: (b, j, col))
    n_chunk = tb // RWKV_CHUNK
    return pl.pallas_call(
        functools.partial(_gla_rwkv_kernel, gla_chunks=tb // GLA_CHUNK, rwkv_chunks=n_chunk),
        out_shape=(jax.ShapeDtypeStruct((B, T, hv), BF16), jax.ShapeDtypeStruct((B, T, W), BF16)),
        grid=(B, T // tb),
        in_specs=[rows(2 * hk, col_qk), rows(hv, col_v), rows(hv, col_og), rows(LANES, col_gz),
                  const(gla_w2p), const(gla_gb), const(gla_ng), rows(cols, col_rwkv)]
                 + [const(params[n]) for n in names],
        out_specs=(rows(hv, 0), rows(W, 0)),
        scratch_shapes=[pltpu.VMEM((hv, hk), F32), pltpu.VMEM((tb, hk), F32), pltpu.VMEM((tb, hk), F32),
                        pltpu.VMEM((tb, hk), BF16), pltpu.VMEM((tb, hk), BF16), pltpu.VMEM((tb, hv), F32),
                        pltpu.VMEM((W, W), F32), pltpu.VMEM((1, cols), F32),
                        pltpu.VMEM((tb, W), F32), pltpu.VMEM((tb, W), F32), pltpu.VMEM((tb, W), F32),
                        pltpu.VMEM((n_chunk, RWKV_CHUNK, W), BF16), pltpu.VMEM((n_chunk, RWKV_CHUNK, W), F32),
                        pltpu.VMEM((n_chunk, W, W), BF16), pltpu.VMEM((n_chunk, W, W), F32),
                        pltpu.VMEM((tb, W), F32)],
        compiler_params=_cparams(("arbitrary", "arbitrary")),
        name="gla_rwkv7_mixers",
    )(proj, proj, proj, proj, gla_w2p, gla_gb, gla_ng, proj, *[params[n] for n in names])


def _merge_kernel(x_ref, sc_ref, sh_ref, gt_ref, gpre_ref, gpost_ref, wg_ref, b0_ref, b1_ref, b2_ref,
                  b3_ref, wb_ref, wo_ref, o_ref):
    x = x_ref[0]
    D = x.shape[1]
    h = _rms_mod(x, gpre_ref[...], sc_ref[0], sh_ref[0]).astype(BF16)
    merged = None
    for g, br in enumerate((b0_ref, b1_ref, b2_ref, b3_ref)):
        gate = _sigmoid(jnp.dot(h, wg_ref[:, g * D:(g + 1) * D], preferred_element_type=F32))
        t = gate * jnp.dot(br[0], wb_ref[g], preferred_element_type=F32)
        merged = t if merged is None else merged + t
    y = _mm(merged, wo_ref[...])
    y = y * lax.rsqrt(jnp.mean(y * y, axis=-1, keepdims=True) + RMS_EPS) * gpost_ref[...]
    o_ref[0] = x + gt_ref[0] * y


def _merge(x, sc, sh, gt, gpre, gpost, w_gate, branches, w_branch, w_out, tm):
    B, T, D = x.shape
    bw = branches[0].shape[2]
    tok = lambda n: pl.BlockSpec((1, tm, n), lambda b, i: (b, i, 0))
    vec = pl.BlockSpec((1, 1, D), lambda b, i: (b, 0, 0))
    const2 = lambda arr: pl.BlockSpec(arr.shape, lambda b, i: (0, 0))
    return pl.pallas_call(
        _merge_kernel,
        out_shape=jax.ShapeDtypeStruct((B, T, D), F32),
        grid=(B, T // tm),
        in_specs=[tok(D), vec, vec, vec, const2(gpre), const2(gpost), const2(w_gate),
                  tok(bw), tok(bw), tok(bw), tok(bw),
                  pl.BlockSpec(w_branch.shape, lambda b, i: (0, 0, 0)), const2(w_out)],
        out_specs=tok(D),
        compiler_params=_cparams(("arbitrary", "arbitrary")),
        name="merge_out_proj",
    )(x, sc, sh, gt, gpre, gpost, w_gate, *branches, w_branch, w_out)


def _ffn_kernel(x_ref, sc_ref, sh_ref, gt_ref, gpre_ref, gpost_ref, wg_ref, wu_ref, wd_ref, o_ref, *, tf):
    x = x_ref[0]
    h = _rms_mod(x, gpre_ref[...], sc_ref[0], sh_ref[0]).astype(BF16)
    F = wg_ref.shape[1]
    acc = None
    for f0 in range(0, F, tf):
        f1 = min(f0 + tf, F)
        gate = jnp.dot(h, wg_ref[:, f0:f1], preferred_element_type=F32)
        up = jnp.dot(h, wu_ref[:, f0:f1], preferred_element_type=F32)
        t = jnp.dot((_silu(gate) * up).astype(BF16), wd_ref[f0:f1, :], preferred_element_type=F32)
        acc = t if acc is None else acc + t
    y = acc * lax.rsqrt(jnp.mean(acc * acc, axis=-1, keepdims=True) + RMS_EPS) * gpost_ref[...]
    o_ref[0] = x + gt_ref[0] * y


def _dense_ffn(x, sc, sh, gt, gpre, gpost, wg, wu, wd, tm):
    B, T, D = x.shape
    tok = pl.BlockSpec((1, tm, D), lambda b, i: (b, i, 0))
    vec = pl.BlockSpec((1, 1, D), lambda b, i: (b, 0, 0))
    const2 = lambda arr: pl.BlockSpec(arr.shape, lambda b, i: (0, 0))
    return pl.pallas_call(
        functools.partial(_ffn_kernel, tf=FFN_CHUNK),
        out_shape=jax.ShapeDtypeStruct((B, T, D), F32),
        grid=(B, T // tm),
        in_specs=[tok, vec, vec, vec, const2(gpre), const2(gpost), const2(wg), const2(wu), const2(wd)],
        out_specs=tok,
        compiler_params=_cparams(("arbitrary", "arbitrary")),
        name="dense_swiglu",
    )(x, sc, sh, gt, gpre, gpost, wg, wu, wd)


MOE_TOKEN_TILE = 256
MOE_ROW_TILE = 512
SEG_ALIGN = 16
SEG_PIECES = (256, 128, 64, 32, 16)
MOE_SMALL_SEG = 128


def _route_kernel(x_ref, sc_ref, sh_ref, gpre_ref, rw_ref, rb_ref, h_ref, mi_ref, mp_ref, cnt_ref):
    tm = x_ref.shape[1]
    h = _rms_mod(x_ref[0], gpre_ref[...], sc_ref[0], sh_ref[0])
    h_ref[...] = h.astype(BF16)
    h_hi = h.astype(BF16)
    h_lo = (h - h_hi.astype(F32)).astype(BF16)
    w = rw_ref[...]
    w_hi = w.astype(BF16)
    w_lo = (w - w_hi.astype(F32)).astype(BF16)
    logits = jnp.dot(jnp.concatenate([h_hi, h_lo, h_hi], axis=1), jnp.concatenate([w_hi, w_hi, w_lo], axis=0),
                     preferred_element_type=F32) + rb_ref[...]
    lane = lax.broadcasted_iota(jnp.int32, logits.shape, 1)
    v1 = jnp.max(logits, axis=-1, keepdims=True)
    i1 = jnp.min(jnp.where(logits == v1, lane, LANES), axis=-1, keepdims=True)
    rest = jnp.where(lane == i1, -jnp.inf, logits)
    v2 = jnp.max(rest, axis=-1, keepdims=True)
    i2 = jnp.min(jnp.where(rest == v2, lane, LANES), axis=-1, keepdims=True)
    e2 = jnp.exp(v2 - v1)
    p1 = 1.0 / (1.0 + e2)
    p2 = e2 / (1.0 + e2)
    oh1 = (lane == i1).astype(F32)
    oh2 = (lane == i2).astype(F32)
    both = oh1 + oh2
    earlier = (lax.broadcasted_iota(jnp.int32, (tm, tm), 1)
               < lax.broadcasted_iota(jnp.int32, (tm, tm), 0)).astype(BF16)
    before = jnp.dot(earlier, both.astype(BF16), preferred_element_type=F32)
    r1 = jnp.sum(oh1 * before, axis=-1, keepdims=True).astype(jnp.int32)
    r2 = jnp.sum(oh2 * before, axis=-1, keepdims=True).astype(jnp.int32)
    col = lax.broadcasted_iota(jnp.int32, mi_ref.shape, 1)
    mi_ref[...] = jnp.where(col == 0, i1, jnp.where(col == 1, i2, jnp.where(col == 2, r1,
                                                                           jnp.where(col == 3, r2, 0))))
    mp_ref[...] = jnp.where(col == 0, p1, jnp.where(col == 1, p2, 0.0))
    cnt_ref[0] = jnp.sum(both, axis=0, keepdims=True).astype(jnp.int32)


def _segment_pieces(n_rows):
    out = []
    for s in SEG_PIECES:
        if s == MOE_TOKEN_TILE:
            out.append((n_rows == s, 0, s))
        else:
            out.append(((n_rows & s) != 0, pl.multiple_of((n_rows // (2 * s)) * (2 * s), SEG_ALIGN), s))
    return out


def _all_segments_small(cnt_ref, tile):
    most = cnt_ref[tile * N_EXPERTS]
    for e in range(1, N_EXPERTS):
        most = jnp.maximum(most, cnt_ref[tile * N_EXPERTS + e])
    return most <= MOE_SMALL_SEG


def _dispatch_kernel(seg_ref, cnt_ref, h_ref, mit_ref, init_ref, xs_ref, buf_ref, sem):
    del init_ref
    tm = h_ref.shape[0]
    i = pl.program_id(0)
    e1, e2 = mit_ref[0:1, :], mit_ref[1:2, :]
    r1, r2 = mit_ref[2:3, :], mit_ref[3:4, :]
    slot = i % 2

    def compact(cap):
        row = lax.broadcasted_iota(jnp.int32, (cap, tm), 0)
        select = jnp.concatenate(
            [jnp.logical_or(jnp.logical_and(e1 == e, r1 == row), jnp.logical_and(e2 == e, r2 == row))
             for e in range(N_EXPERTS)], axis=0).astype(BF16)
        rows = jnp.dot(select, h_ref[...], preferred_element_type=F32).astype(BF16)
        for e in range(N_EXPERTS):
            buf_ref[slot, e * tm:e * tm + cap, :] = rows[e * cap:(e + 1) * cap]

    small = _all_segments_small(cnt_ref, i)
    pl.when(small)(functools.partial(compact, MOE_SMALL_SEG))
    pl.when(jnp.logical_not(small))(functools.partial(compact, tm))

    def segment_copies(tile, buf, e):
        n = cnt_ref[tile * N_EXPERTS + e]
        n_rows = ((n + SEG_ALIGN - 1) // SEG_ALIGN) * SEG_ALIGN
        dst = pl.multiple_of(seg_ref[tile * N_EXPERTS + e], SEG_ALIGN)
        return [(cond, pltpu.make_async_copy(buf_ref.at[buf, pl.ds(e * tm + off, s), :],
                                             xs_ref.at[pl.ds(dst + off, s), :], sem))
                for cond, off, s in _segment_pieces(n_rows)]

    def for_all_segments(tile, buf, action):
        for e in range(N_EXPERTS):
            for cond, cp in segment_copies(tile, buf, e):
                pl.when(cond)(getattr(cp, action))

    @pl.when(i > 0)
    def _():
        for_all_segments(i - 1, 1 - slot, "wait")

    for_all_segments(i, slot, "start")

    @pl.when(i == pl.num_programs(0) - 1)
    def _():
        for_all_segments(i, slot, "wait")


def _expert_kernel(te_ref, nv_ref, xs_ref, wg_ref, wu_ref, wd_ref, ys_ref, acc_ref):
    del te_ref
    r = pl.program_id(0)
    f = pl.program_id(1)
    valid = r < nv_ref[0]

    @pl.when(jnp.logical_and(valid, f == 0))
    def _():
        acc_ref[...] = jnp.zeros_like(acc_ref)

    @pl.when(valid)
    def _():
        x = xs_ref[...]
        tf = wg_ref.shape[2]
        acc = acc_ref[...]
        for f0 in range(0, tf, FFN_CHUNK):
            f1 = min(f0 + FFN_CHUNK, tf)
            gate = jnp.dot(x, wg_ref[0, :, f0:f1], preferred_element_type=F32)
            up = jnp.dot(x, wu_ref[0, :, f0:f1], preferred_element_type=F32)
            acc = acc + jnp.dot((_silu(gate) * up).astype(BF16), wd_ref[0, f0:f1, :],
                                preferred_element_type=F32)
        acc_ref[...] = acc

    @pl.when(f == pl.num_programs(1) - 1)
    def _():
        @pl.when(valid)
        def _():
            ys_ref[...] = acc_ref[...].astype(ys_ref.dtype)

        @pl.when(jnp.logical_not(valid))
        def _():
            ys_ref[...] = jnp.zeros_like(ys_ref)


def _combine_kernel(seg_ref, cnt_ref, ys_ref, mi_ref, mp_ref, x_ref, gt_ref, gpost_ref, o_ref, win_ref,
                    pair_ref, sem):
    tm = x_ref.shape[1]
    i = pl.program_id(0) * pl.num_programs(1) + pl.program_id(1)
    n_tiles = pl.num_programs(0) * pl.num_programs(1)
    slot = i % 2

    def window_copy(tile, buf, e, rows):
        src = pl.multiple_of(seg_ref[tile * N_EXPERTS + e], SEG_ALIGN)
        return pltpu.make_async_copy(ys_ref.at[pl.ds(src, rows), :], win_ref.at[buf, pl.ds(e * rows, rows), :],
                                     sem.at[buf, e])

    def for_all_windows(tile, buf, action):
        small = _all_segments_small(cnt_ref, tile)
        for rows, cond in ((MOE_SMALL_SEG, small), (tm, jnp.logical_not(small))):
            @pl.when(cond)
            def _():
                for e in range(N_EXPERTS):
                    getattr(window_copy(tile, buf, e, rows), action)()

    @pl.when(i == 0)
    def _():
        for_all_windows(i, slot, "start")

    @pl.when(i + 1 < n_tiles)
    def _():
        for_all_windows(i + 1, 1 - slot, "start")

    for_all_windows(i, slot, "wait")
    e1, e2 = mi_ref[:, 0:1], mi_ref[:, 1:2]
    r1, r2 = mi_ref[:, 2:3], mi_ref[:, 3:4]

    def expand_rows(cap):
        col = lax.broadcasted_iota(jnp.int32, (tm, cap), 1)
        expand = jnp.concatenate(
            [jnp.concatenate([jnp.logical_and(e1 == e, r1 == col), jnp.logical_and(e2 == e, r2 == col)], axis=0)
             for e in range(N_EXPERTS)], axis=1).astype(BF16)
        pair_ref[...] = jnp.dot(expand, win_ref[slot, 0:N_EXPERTS * cap, :], preferred_element_type=F32)

    small = _all_segments_small(cnt_ref, i)
    pl.when(small)(functools.partial(expand_rows, MOE_SMALL_SEG))
    pl.when(jnp.logical_not(small))(functools.partial(expand_rows, tm))
    y = mp_ref[:, 0:1] * pair_ref[0:tm, :] + mp_ref[:, 1:2] * pair_ref[tm:2 * tm, :]
    y = y * lax.rsqrt(jnp.mean(y * y, axis=-1, keepdims=True) + RMS_EPS) * gpost_ref[...]
    o_ref[0] = x_ref[0] + gt_ref[0] * y


def _moe_ffn(x, sc, sh, gt, gpre, gpost, rw, rb, wg, wu, wd, tf):
    B, T, D = x.shape
    E, _, F = wg.shape
    tm = min(MOE_TOKEN_TILE, T)
    assert tm == MOE_TOKEN_TILE and E == N_EXPERTS
    nT = T // tm
    n_tok_tiles = B * nT
    N = B * T
    max_rows = 2 * N + n_tok_tiles * E * (SEG_ALIGN - 1) + E * (MOE_ROW_TILE - SEG_ALIGN)
    n_row_tiles = -(-max_rows // MOE_ROW_TILE) + 1
    P = n_row_tiles * MOE_ROW_TILE

    vec = pl.BlockSpec((1, 1, D), lambda b, j: (b, 0, 0))
    const2 = lambda arr: pl.BlockSpec(arr.shape, lambda b, j: (0, 0))
    flat = lambda n: pl.BlockSpec((tm, n), lambda b, j: (b * nT + j, 0))
    h, mi, mp, cnt = pl.pallas_call(
        _route_kernel,
        out_shape=(jax.ShapeDtypeStruct((N, D), BF16), jax.ShapeDtypeStruct((N, 8), jnp.int32),
                   jax.ShapeDtypeStruct((N, 8), F32), jax.ShapeDtypeStruct((n_tok_tiles, 1, LANES), jnp.int32)),
        grid=(B, nT),
        in_specs=[pl.BlockSpec((1, tm, D), lambda b, j: (b, j, 0)), vec, vec, const2(gpre), const2(rw),
                  const2(rb)],
        out_specs=(flat(D), flat(8), flat(8), pl.BlockSpec((1, 1, LANES), lambda b, j: (b * nT + j, 0, 0))),
        compiler_params=_cparams(("arbitrary", "arbitrary")),
        name="moe_route",
    )(x, sc, sh, gpre, rw, rb)

    counts = cnt[:, 0, :E]
    seg_len = (counts + SEG_ALIGN - 1) // SEG_ALIGN * SEG_ALIGN
    group_len = (jnp.sum(seg_len, axis=0) + MOE_ROW_TILE - 1) // MOE_ROW_TILE * MOE_ROW_TILE
    group_end = jnp.cumsum(group_len)
    seg_start = (group_end - group_len)[None, :] + jnp.cumsum(seg_len, axis=0) - seg_len
    seg_start = seg_start.reshape(-1).astype(jnp.int32)
    counts = counts.reshape(-1)
    n_valid = (group_end[-1:] // MOE_ROW_TILE).astype(jnp.int32)
    tile_first_row = jnp.arange(n_row_tiles, dtype=jnp.int32) * MOE_ROW_TILE
    tile_expert = jnp.minimum(jnp.sum(tile_first_row[:, None] >= group_end[None, :], axis=1), E - 1)
    tile_expert = tile_expert.astype(jnp.int32)

    xs = pl.pallas_call(
        _dispatch_kernel,
        out_shape=jax.ShapeDtypeStruct((P, D), BF16),
        grid_spec=pltpu.PrefetchScalarGridSpec(
            num_scalar_prefetch=2,
            grid=(n_tok_tiles,),
            in_specs=[pl.BlockSpec((tm, D), lambda i, seg, n: (i, 0)),
                      pl.BlockSpec((8, tm), lambda i, seg, n: (0, i)),
                      pl.BlockSpec(memory_space=pl.ANY)],
            out_specs=pl.BlockSpec(memory_space=pl.ANY),
            scratch_shapes=[pltpu.VMEM((2, E * tm, D), BF16), pltpu.SemaphoreType.DMA(())]),
        input_output_aliases={4: 0},
        compiler_params=_cparams(("arbitrary",)),
        name="moe_dispatch",
    )(seg_start, counts, h, mi.T, jnp.zeros((P, D), BF16))

    nf = F // tf
    live = lambda r, f, nv: jnp.where(r < nv[0], f, nf - 1)
    ys = pl.pallas_call(
        _expert_kernel,
        out_shape=jax.ShapeDtypeStruct((P, D), BF16),
        grid_spec=pltpu.PrefetchScalarGridSpec(
            num_scalar_prefetch=2,
            grid=(n_row_tiles, nf),
            in_specs=[pl.BlockSpec((MOE_ROW_TILE, D), lambda r, f, te, nv: (r, 0)),
                      pl.BlockSpec((1, D, tf), lambda r, f, te, nv: (te[r], 0, live(r, f, nv))),
                      pl.BlockSpec((1, D, tf), lambda r, f, te, nv: (te[r], 0, live(r, f, nv))),
                      pl.BlockSpec((1, tf, D), lambda r, f, te, nv: (te[r], live(r, f, nv), 0))],
            out_specs=pl.BlockSpec((MOE_ROW_TILE, D), lambda r, f, te, nv: (r, 0)),
            scratch_shapes=[pltpu.VMEM((MOE_ROW_TILE, D), F32)]),
        compiler_params=_cparams(("arbitrary", "arbitrary")),
        name="moe_experts",
    )(tile_expert, n_valid, xs, wg, wu, wd)

    return pl.pallas_call(
        _combine_kernel,
        out_shape=jax.ShapeDtypeStruct((B, T, D), F32),
        grid_spec=pltpu.PrefetchScalarGridSpec(
            num_scalar_prefetch=2,
            grid=(B, nT),
            in_specs=[pl.BlockSpec(memory_space=pl.ANY),
                      pl.BlockSpec((tm, 8), lambda b, j, seg, n: (b * nT + j, 0)),
                      pl.BlockSpec((tm, 8), lambda b, j, seg, n: (b * nT + j, 0)),
                      pl.BlockSpec((1, tm, D), lambda b, j, seg, n: (b, j, 0)),
                      pl.BlockSpec((1, 1, D), lambda b, j, seg, n: (b, 0, 0)),
                      pl.BlockSpec(gpost.shape, lambda b, j, seg, n: (0, 0))],
            out_specs=pl.BlockSpec((1, tm, D), lambda b, j, seg, n: (b, j, 0)),
            scratch_shapes=[pltpu.VMEM((2, E * tm, D), BF16), pltpu.VMEM((2 * tm, D), F32),
                            pltpu.SemaphoreType.DMA((2, E))]),
        compiler_params=_cparams(("arbitrary", "arbitrary")),
        name="moe_combine",
    )(seg_start, counts, ys, mi, mp, x, gt, gpost)


def _rope_tables(positions, groups):
    d = 32
    inv = 1.0 / (ROPE_THETA ** (jnp.arange(0, d, 2, dtype=F32) / d))
    ang = positions.astype(F32)[..., None] * inv
    cos, sin = jnp.cos(ang), jnp.sin(ang)
    cos = jnp.tile(jnp.concatenate([cos, cos], axis=-1), (1, 1, groups))
    sin = jnp.tile(jnp.concatenate([-sin, sin], axis=-1), (1, 1, groups))
    return cos, sin


def _pad_rows(w, rows, offset):
    out = jnp.zeros((rows, w.shape[1]), w.dtype)
    return out.at[offset:offset + w.shape[0]].set(w)


def kernel(x, c, positions, ada_w, ada_b, norm_mix_pre, norm_mix_post, norm_ffn_pre, norm_ffn_post, w_in, gla_gate_w2, gla_gate_b, gla_norm, diff_lambda, diff_subln, conv_w, conv_b, conv_ln_g, conv_ln_b, rwkv_mu, rwkv_w0, rwkv_w2, rwkv_a0, rwkv_a2, rwkv_g2, rwkv_k_k, rwkv_k_a, rwkv_r_k, rwkv_ln_g, rwkv_ln_b, w_branch, w_out, ffn_w_gate, ffn_w_up, ffn_w_down, router_w, router_b, moe_w_gate, moe_w_up, moe_w_down):
    B, T, D = x.shape
    L = ada_w.shape[0]
    W = D // N_BRANCH
    hk = gla_gate_b.shape[1]
    decay_rank = rwkv_w2.shape[1]
    a_rank = rwkv_a2.shape[1]
    gate_rank = rwkv_g2.shape[1]
    assert decay_rank + a_rank == LANES and 2 * hk == W and gate_rank == LANES
    n_mix = (3 * W + decay_rank + a_rank + gate_rank) + 3 * W + 3 * W
    sizes = (hk, hk, W, W, GLA_GATE_RANK, W, W, W, W, W, 3 * W + LANES + gate_rank, N_BRANCH * D)
    offs = [0]
    for s in sizes:
        offs.append(offs[-1] + s)
    assert offs[-1] == w_in.shape[2]
    tm = min(TOKEN_TILE, T)
    tb = min(SEQ_BLOCK, T)

    mod = _modulation(c, ada_w, ada_b)
    cos, sin = _rope_tables(positions, LANES // 32)

    for l in range(L):
        m = mod[l].reshape(B, 1, 6 * D)
        sh_m, sc_m, gt_m, sh_f, sc_f, gt_f = [m[:, :, i * D:(i + 1) * D] for i in range(6)]

        wl = w_in[l]
        gz_cols = jnp.zeros((D, LANES), F32).at[:, :GLA_GATE_RANK].set(wl[:, offs[4]:offs[5]])
        w_mix = jnp.concatenate([wl[:, offs[10]:offs[11]], wl[:, offs[0]:offs[4]], wl[:, offs[5]:offs[8]],
                                 gz_cols], axis=1).astype(BF16)
        w_conv = wl[:, offs[8]:offs[10]].astype(BF16)
        w_gate = wl[:, offs[11]:offs[12]].astype(BF16)
        proj, o_conv = _in_projection_conv(x, sc_m, sh_m, norm_mix_pre[l][None], w_conv, w_mix,
                                           _pad_rows(conv_w[l], 32, 0), conv_b[l][None],
                                           conv_ln_g[l][None], conv_ln_b[l][None], tm)

        o_diff = _diff_attention(proj, cos, sin, diff_lambda[l],
                                 diff_subln[l][None], col_q=7, col_k=8, col_v=9, layer_idx=l)
        vecs = dict(mu=rwkv_mu[l][None], w0=rwkv_w0[l][None], a0=rwkv_a0[l][None],
                    kk=rwkv_k_k[l][None], ka=rwkv_k_a[l][None], rk=rwkv_r_k[l].reshape(1, W),
                    lng=rwkv_ln_g[l][None], lnb=rwkv_ln_b[l][None])
        o_gla, o_rwkv = _gla_rwkv7(
            proj, _pad_rows(gla_gate_w2[l], LANES, 0), gla_gate_b[l][None],
            jnp.tile(gla_norm[l], GLA_HEADS)[None], 4, 5, 6, n_mix // LANES,
            vecs, _pad_rows(rwkv_w2[l], LANES, 0).astype(BF16),
            _pad_rows(rwkv_a2[l], LANES, decay_rank).astype(BF16), rwkv_g2[l].astype(BF16), 0, tb)
        x = _merge(x, sc_m, sh_m, gt_m, norm_mix_pre[l][None], norm_mix_post[l][None], w_gate,
                   (o_gla, o_diff, o_conv, o_rwkv), w_branch[l].astype(BF16), w_out[l].astype(BF16), tm)

        i = l // 2
        if l % 2 == 0:
            x = _dense_ffn(x, sc_f, sh_f, gt_f, norm_ffn_pre[l][None], norm_ffn_post[l][None],
                           ffn_w_gate[i].astype(BF16), ffn_w_up[i].astype(BF16),
                           ffn_w_down[i].astype(BF16), tm)
        else:
            rw = jnp.zeros((D, LANES), F32).at[:, :N_EXPERTS].set(router_w[i])
            rb = jnp.full((1, LANES), -jnp.inf, F32).at[0, :N_EXPERTS].set(router_b[i])
            x = _moe_ffn(x, sc_f, sh_f, gt_f, norm_ffn_pre[l][None], norm_ffn_post[l][None], rw, rb,
                         moe_w_gate[i].astype(BF16), moe_w_up[i].astype(BF16),
                         moe_w_down[i].astype(BF16), tf=moe_w_gate.shape[3])
    return x
```

```python
import functools
import math

import jax
import jax.numpy as jnp
from jax import lax
from jax.experimental import pallas as pl
from jax.experimental.pallas import tpu as pltpu

F32 = jnp.float32
BF16 = jnp.bfloat16
HIGHEST = lax.Precision.HIGHEST
LOG2_E = math.log2(math.e)

N_BRANCH = 4
GLA_HEADS = 4
GLA_GATE_RANK = 16
GLA_GATE_NORMALIZER = 16.0
GLA_CHUNK = 32
DIFF_HEADS = 4
ROPE_THETA = 10000.0
CONV_WIDTH = 31
RWKV_HEADS = 4
RWKV_CHUNK = 64
RWKV_SUB = 16
N_EXPERTS = 8
RMS_EPS = 1e-6
LN_EPS = 1e-5
RWKV_GN_EPS = 64e-5
LANES = 128
SUBLANES = 8
VMEM_LIMIT = 56 * 1024 * 1024

TOKEN_TILE = 512
SEQ_BLOCK = 512
ATTN_BLOCK = 512
CONV_ROWS = 128
FFN_CHUNK = 512


def _cparams(sem):
    return pltpu.CompilerParams(dimension_semantics=sem, vmem_limit_bytes=VMEM_LIMIT)


def _mm(a, b):
    return jnp.dot(a.astype(BF16), b.astype(BF16), preferred_element_type=F32)


def _bmm(a, b):
    return lax.dot_general(a.astype(BF16), b.astype(BF16), (((2,), (1,)), ((0,), (0,))),
                           preferred_element_type=F32)


def _mm_f32(a, b):
    return jnp.dot(a, b, precision=HIGHEST, preferred_element_type=F32)


def _hi_lo(x):
    hi = x.astype(BF16)
    return jnp.concatenate([hi, (x - hi.astype(F32)).astype(BF16)], axis=1)


def _sigmoid(x):
    return 0.5 * jnp.tanh(0.5 * x) + 0.5


def _silu(x):
    return x * _sigmoid(x)


def _softplus(x):
    return jnp.maximum(x, 0.0) + jnp.log(1.0 + jnp.exp(-jnp.abs(x)))


def _group_matrix(n, group):
    r = lax.broadcasted_iota(jnp.int32, (n, n), 0) // group
    c = lax.broadcasted_iota(jnp.int32, (n, n), 1) // group
    return r == c


def _rms_mod(x, gain, scale, shift):
    y = x * lax.rsqrt(jnp.mean(x * x, axis=-1, keepdims=True) + RMS_EPS)
    return y * gain * (1.0 + scale) + shift


def _mod_kernel(c_ref, w_ref, b_ref, o_ref):
    o_ref[0] = _mm_f32(_silu(c_ref[...]), w_ref[0]) + b_ref[0]


def _modulation(c, ada_w, ada_b):
    L, D, M = ada_w.shape
    B = c.shape[0]
    tn = M // 4
    return pl.pallas_call(
        _mod_kernel,
        out_shape=jax.ShapeDtypeStruct((L, B, M), F32),
        grid=(L, M // tn),
        in_specs=[pl.BlockSpec((B, D), lambda l, j: (0, 0)),
                  pl.BlockSpec((1, D, tn), lambda l, j: (l, 0, j)),
                  pl.BlockSpec((1, 1, tn), lambda l, j: (l, 0, j))],
        out_specs=pl.BlockSpec((1, B, tn), lambda l, j: (l, 0, j)),
        compiler_params=_cparams(("arbitrary", "arbitrary")),
        name="adaln_mod",
    )(c, ada_w, ada_b.reshape(L, 1, M))


CONV_HALO = 32


def _inproj_conv_kernel(x_ref, sc_ref, sh_ref, g_ref, wab_ref, w_ref, cw_ref, cb_ref, lg_ref, lb_ref,
                        o_ref, oc_ref, u_ref, *, rb):
    tm = x_ref.shape[1]
    ch = oc_ref.shape[2]
    h = _rms_mod(x_ref[0], g_ref[...], sc_ref[0], sh_ref[0]).astype(BF16)

    @pl.when(pl.program_id(1) == 0)
    def _():
        u_ref[0:CONV_HALO, :] = jnp.zeros((CONV_HALO, ch), F32)

    ab = jnp.dot(h, wab_ref[...], preferred_element_type=F32)
    u_ref[CONV_HALO:CONV_HALO + tm, :] = ab[:, 0:ch] * _sigmoid(ab[:, ch:2 * ch])
    o_ref[0] = jnp.dot(h, w_ref[...], preferred_element_type=F32)

    first = CONV_HALO - (CONV_WIDTH - 1)
    for i in range(tm // rb):
        r0 = i * rb
        win = u_ref[r0:r0 + rb + CONV_HALO, :]
        acc = jnp.zeros((rb, ch), F32)
        for s in range(SUBLANES):
            taps = [j for j in range(CONV_WIDTH) if (first + j) % SUBLANES == s]
            rolled = win if s == 0 else pltpu.roll(win, rb + CONV_HALO - s, 0)
            for j in taps:
                a0 = first + j - s
                acc = acc + cw_ref[j:j + 1, :] * rolled[a0:a0 + rb, :]
        y = acc + cb_ref[...]
        mu = jnp.mean(y, axis=-1, keepdims=True)
        yc = y - mu
        var = jnp.mean(yc * yc, axis=-1, keepdims=True)
        y = yc * lax.rsqrt(var + LN_EPS) * lg_ref[...] + lb_ref[...]
        oc_ref[0, r0:r0 + rb, :] = _silu(y).astype(oc_ref.dtype)
    u_ref[0:CONV_HALO, :] = u_ref[tm:tm + CONV_HALO, :]


def _in_projection_conv(x, sc, sh, gain, w_ab, w, conv_w, conv_b, ln_g, ln_b, tm):
    B, T, D = x.shape
    n = w.shape[1]
    ch = conv_w.shape[1]
    const = lambda arr: pl.BlockSpec(arr.shape, lambda b, i: (0, 0))
    return pl.pallas_call(
        functools.partial(_inproj_conv_kernel, rb=min(CONV_ROWS, tm)),
        out_shape=(jax.ShapeDtypeStruct((B, T, n), F32), jax.ShapeDtypeStruct((B, T, ch), BF16)),
        grid=(B, T // tm),
        in_specs=[pl.BlockSpec((1, tm, D), lambda b, i: (b, i, 0)),
                  pl.BlockSpec((1, 1, D), lambda b, i: (b, 0, 0)),
                  pl.BlockSpec((1, 1, D), lambda b, i: (b, 0, 0)),
                  const(gain), const(w_ab), const(w), const(conv_w), const(conv_b), const(ln_g), const(ln_b)],
        out_specs=(pl.BlockSpec((1, tm, n), lambda b, i: (b, i, 0)),
                   pl.BlockSpec((1, tm, ch), lambda b, i: (b, i, 0))),
        scratch_shapes=[pltpu.VMEM((tm + CONV_HALO, ch), F32)],
        compiler_params=_cparams(("arbitrary", "arbitrary")),
        name="in_proj_conv",
    )(x, sc, sh, gain, w_ab, w, conv_w, conv_b, ln_g, ln_b)


def _gla_kernel(qk_ref, v_ref, og_ref, gz_ref, w2_ref, gb_ref, ng_ref, o_ref, s_ref, g_ref, r_ref,
                qd_ref, kd_ref, oacc_ref, *, n_chunk):
    C = GLA_CHUNK
    hk = qk_ref.shape[2] // 2
    hv = v_ref.shape[2]
    dk = hk // GLA_HEADS
    dv = hv // GLA_HEADS

    tb = n_chunk * C

    @pl.when(pl.program_id(1) == 0)
    def _():
        s_ref[...] = jnp.zeros_like(s_ref)

    z = _mm(gz_ref[0], w2_ref[...]) + gb_ref[...]
    gk = (jnp.minimum(z, 0.0) - jnp.log(1.0 + jnp.exp(-jnp.abs(z)))) * (LOG2_E / GLA_GATE_NORMALIZER)
    tri = (lax.broadcasted_iota(jnp.int32, (C, C), 1)
           <= lax.broadcasted_iota(jnp.int32, (C, C), 0)).astype(BF16)
    sums = _bmm(jnp.broadcast_to(tri[None], (n_chunk, C, C)), _hi_lo(gk).reshape(n_chunk, C, 2 * hk))
    G3 = sums[:, :, 0:hk] + sums[:, :, hk:2 * hk]
    G_all = G3.reshape(tb, hk)
    R_all = (jnp.broadcast_to(G3[:, C - 1:C, :], (n_chunk, C, hk)) - G3).reshape(tb, hk)
    g_ref[...] = G_all
    r_ref[...] = R_all
    qd_ref[...] = (qk_ref[0, :, 0:hk] * (dk ** -0.5) * jnp.exp2(G_all)).astype(BF16)
    kd_ref[...] = (qk_ref[0, :, hk:2 * hk] * jnp.exp2(R_all)).astype(BF16)

    half = C // 2

    def key_not_after_query(nj, ni):
        return (lax.broadcasted_iota(jnp.int32, (nj, ni, hk), 0)
                <= lax.broadcasted_iota(jnp.int32, (nj, ni, hk), 1))

    causal_first, causal_second = key_not_after_query(half, C), key_not_after_query(half, half)
    er = lax.broadcasted_iota(jnp.int32, (hk, hv), 0) // dk
    ec = lax.broadcasted_iota(jnp.int32, (hk, hv), 1) // dv
    expand = (er == ec).astype(BF16)
    sr = lax.broadcasted_iota(jnp.int32, (hv, hk), 0) // dv
    scol = lax.broadcasted_iota(jnp.int32, (hv, hk), 1) // dk
    state_mask = sr == scol
    head_mean = (_group_matrix(hv, dv).astype(F32) / dv).astype(BF16)

    def chunk(ci, carry):
        r0 = pl.multiple_of(ci * C, C)
        rows = pl.ds(r0, C)
        q = qk_ref[0, rows, 0:hk] * (dk ** -0.5)
        k = qk_ref[0, rows, hk:2 * hk]
        v = v_ref[0, rows, :]
        G = g_ref[rows, :]
        def keys_to_queries(j0, i0, keep):
            nj, ni = keep.shape[0], keep.shape[1]
            pair = (nj, ni, hk)
            g_diff = (jnp.broadcast_to(G[i0:i0 + ni][None], pair)
                      - jnp.broadcast_to(G[j0:j0 + nj][:, None, :], pair))
            decay = jnp.exp2(jnp.where(keep, g_diff, -jnp.inf))
            p = (jnp.broadcast_to(q[i0:i0 + ni][None], pair)
                 * jnp.broadcast_to(k[j0:j0 + nj][:, None, :], pair) * decay)
            a_exp = jnp.dot(p.reshape(nj * ni, hk).astype(BF16), expand, preferred_element_type=F32)
            v_j = jnp.broadcast_to(v[j0:j0 + nj][:, None, :], (nj, ni, hv))
            return jnp.sum(a_exp.reshape(nj, ni, hv) * v_j, axis=0)

        o_second = keys_to_queries(half, half, causal_second)
        o_intra = keys_to_queries(0, 0, causal_first) + jnp.concatenate([jnp.zeros_like(o_second), o_second],
                                                                        axis=0)
        g_total = G[0:1, :] + r_ref[pl.ds(r0, 8), :][0:1, :]
        s = s_ref[...]
        o_inter = lax.dot_general(qd_ref[rows, :], s.astype(BF16), (((1,), (1,)), ((), ())),
                                  preferred_element_type=F32)
        kv = lax.dot_general(v.astype(BF16), kd_ref[rows, :], (((0,), (0,)), ((), ())),
                             preferred_element_type=F32)
        s_ref[...] = s * jnp.exp2(g_total) + jnp.where(state_mask, kv, 0.0)
        oacc_ref[rows, :] = o_intra + o_inter
        return carry

    lax.fori_loop(0, n_chunk, chunk, 0, unroll=True)

    o = oacc_ref[...]
    ms = jnp.dot(_hi_lo(o * o), jnp.concatenate([head_mean, head_mean], axis=0),
                 preferred_element_type=F32)
    o = o * lax.rsqrt(ms + RMS_EPS) * ng_ref[...] * _silu(og_ref[0])
    o_ref[0] = o.astype(o_ref.dtype)


def _rope(t, cos, sin_signed):
    d = 32
    half = d // 2
    out = []
    for s in range(t.shape[1] // LANES):
        x = t[:, s * LANES:(s + 1) * LANES]
        lane = lax.broadcasted_iota(jnp.int32, x.shape, 1)
        up = pltpu.roll(x, LANES - half, 1)
        down = pltpu.roll(x, half, 1)
        rot = jnp.where((lane % d) < half, up, down)
        out.append(x * cos + rot * sin_signed)
    return jnp.concatenate(out, axis=1)


def _diff_kernel(q_ref, k_ref, v_ref, cosq_ref, sinq_ref, cosk_ref, sink_ref, lam_ref, g_ref, o_ref,
                 ks, vs, *, tq, lam_init):
    H = DIFF_HEADS
    d = q_ref.shape[2] // (2 * H)
    dv = v_ref.shape[2] // H
    j = pl.program_id(1)

    @pl.when(j == 0)
    def _():
        k = _rope(k_ref[0], cosk_ref[0], sink_ref[0])
        v = v_ref[0]
        for hc in range(2 * H):
            ks[hc] = k[:, hc * d:(hc + 1) * d].astype(BF16)
        ones_col = (lax.broadcasted_iota(jnp.int32, (v.shape[0], dv), 1) == 0).astype(BF16)
        for h in range(H):
            vs[h] = jnp.concatenate([v[:, h * dv:(h + 1) * dv].astype(BF16), ones_col], axis=1)

    q = _rope(q_ref[0], cosq_ref[0], sinq_ref[0]) * (d ** -0.5 * LOG2_E)
    lp = lam_ref[...]
    lam = (jnp.exp(jnp.sum(lp[0:1] * lp[1:2], axis=-1, keepdims=True))
           - jnp.exp(jnp.sum(lp[2:3] * lp[3:4], axis=-1, keepdims=True)) + lam_init)
    on_or_below_diag = (lax.broadcasted_iota(jnp.int32, (tq, tq), 1)
                        <= lax.broadcasted_iota(jnp.int32, (tq, tq), 0))

    def update(qh, kh, vh, m, acc, masked):
        s = lax.dot_general(qh, kh, (((1,), (1,)), ((), ())), preferred_element_type=F32)
        if masked:
            s = jnp.where(on_or_below_diag, s, -jnp.inf)
        m_new = jnp.maximum(m, jnp.max(s, axis=-1, keepdims=True))
        p = jnp.exp2(s - m_new).astype(BF16)
        acc = jnp.exp2(m - m_new) * acc + jnp.dot(p, vh, preferred_element_type=F32)
        return m_new, acc

    qs = [q[:, hc * d:(hc + 1) * d].astype(BF16) for hc in range(2 * H)]

    def kv_block(kb, carry, masked):
        rows = pl.ds(pl.multiple_of(kb * tq, tq), tq)
        out = []
        for hc in range(2 * H):
            m, acc = carry[2 * hc], carry[2 * hc + 1]
            out.extend(update(qs[hc], ks[hc, rows, :], vs[hc // 2, rows, :], m, acc, masked))
        return tuple(out)

    m0 = jnp.full((tq, 1), -jnp.inf, F32)
    a0 = jnp.zeros((tq, 2 * dv), F32)
    carry = lax.fori_loop(0, j, lambda kb, c: kv_block(kb, c, False), (m0, a0) * (2 * H))
    carry = kv_block(j, carry, True)

    for h in range(H):
        a1, a2 = carry[4 * h + 1], carry[4 * h + 3]
        comp = [a[:, 0:dv] / a[:, dv:dv + 1] for a in (a1, a2)]
        o = comp[0] - lam * comp[1]
        o = o * lax.rsqrt(jnp.mean(o * o, axis=-1, keepdims=True) + RMS_EPS)
        o = o * g_ref[...] * (1.0 - lam_init)
        o_ref[0, :, h * dv:(h + 1) * dv] = o.astype(o_ref.dtype)


def _diff_attention(proj, cos, sin, lam_p, g, col_q, col_k, col_v, layer_idx):
    B, T, _ = proj.shape
    H = DIFF_HEADS
    dv = g.shape[1]
    d = dv // 2
    w = 2 * H * d
    assert cos.shape[2] == LANES
    tq = min(ATTN_BLOCK, T)
    lam_init = 0.8 - 0.6 * math.exp(-0.3 * layer_idx)
    blk = lambda col: pl.BlockSpec((1, tq, w), lambda b, j: (b, j, col))
    full = lambda col: pl.BlockSpec((1, T, w), lambda b, j: (b, 0, col))
    tab_blk = pl.BlockSpec((1, tq, LANES), lambda b, j: (b, j, 0))
    tab_full = pl.BlockSpec((1, T, LANES), lambda b, j: (b, 0, 0))
    return pl.pallas_call(
        functools.partial(_diff_kernel, tq=tq, lam_init=lam_init),
        out_shape=jax.ShapeDtypeStruct((B, T, H * dv), BF16),
        grid=(B, T // tq),
        in_specs=[blk(col_q), full(col_k), full(col_v), tab_blk, tab_blk, tab_full, tab_full,
                  pl.BlockSpec(lam_p.shape, lambda b, j: (0, 0)),
                  pl.BlockSpec(g.shape, lambda b, j: (0, 0))],
        out_specs=pl.BlockSpec((1, tq, H * dv), lambda b, j: (b, j, 0)),
        scratch_shapes=[pltpu.VMEM((2 * H, T, d), BF16),
                        pltpu.VMEM((H, T, 2 * dv), BF16)],
        compiler_params=_cparams(("arbitrary", "arbitrary")),
        name="diff_attention",
    )(proj, proj, proj, cos, sin, cos, sin, lam_p, g)


def _rwkv_kernel(x_ref, mu_ref, w0_ref, w2_ref, a0_ref, a2_ref, g2_ref, kk_ref, ka_ref, rk_ref,
                 lng_ref, lnb_ref, o_ref, s_ref, prev_ref, gate_ref, bonus_ref, dec_ref, qeff_ref, yloc_ref,
                 mlow_ref, nc_ref, y_ref, *, n_chunk):
    C = RWKV_CHUNK
    H = RWKV_HEADS
    W = o_ref.shape[2]
    N = W // H

    @pl.when(pl.program_id(1) == 0)
    def _():
        s_ref[...] = jnp.zeros_like(s_ref)
        prev_ref[...] = jnp.zeros_like(prev_ref)

    lane = lax.broadcasted_iota(jnp.int32, (1, W), 1)
    head_mask = [(lane // N == h).astype(F32) for h in range(H)]
    block_diag = _group_matrix(W, N)
    head_sum = block_diag.astype(F32)
    ti = lax.broadcasted_iota(jnp.int32, (C, C), 0)
    tj = lax.broadcasted_iota(jnp.int32, (C, C), 1)
    tril_incl = ti >= tj
    tril_strict = ti > tj
    same_sub = (ti // RWKV_SUB) == (tj // RWKV_SUB)
    eye = (ti == tj).astype(F32)
    nc = n_chunk
    tb = nc * C

    def head_total(t, two_term=False):
        ones = head_sum.astype(BF16)
        if two_term:
            return jnp.dot(_hi_lo(t), jnp.concatenate([ones, ones], axis=0), preferred_element_type=F32)
        return jnp.dot(t.astype(BF16), ones, preferred_element_type=F32)

    x = x_ref[0]
    first_row = lax.broadcasted_iota(jnp.int32, (tb, 1), 0) == 0
    prev = jnp.where(first_row, prev_ref[...], pltpu.roll(x, 1, 0))
    prev_ref[...] = x[tb - 1:tb, :]
    xm = x + (prev - x) * mu_ref[...]
    r = xm[:, 0:W]
    k = xm[:, W:2 * W]
    v = xm[:, 2 * W:3 * W]
    zz = xm[:, 3 * W:3 * W + LANES]
    zg = xm[:, 3 * W + LANES:]
    w = -_softplus(-(w0_ref[...] + _mm(jnp.tanh(zz), w2_ref[...]))) - 0.5
    lw = -jnp.exp(w) * LOG2_E
    a = _sigmoid(a0_ref[...] + _mm(zz, a2_ref[...]))
    gate_ref[...] = _mm(_sigmoid(zg), g2_ref[...])
    kk = k * kk_ref[...]
    kk = kk / jnp.maximum(jnp.sqrt(head_total(kk * kk, two_term=True)), 1e-12)
    k = k * (1.0 + (a - 1.0) * ka_ref[...])
    b = kk * a
    bonus_ref[...] = head_total(r * k * rk_ref[...]) * v

    c3 = lambda t: t.reshape(nc, C, t.shape[1])
    sums = _bmm(jnp.broadcast_to(tril_incl.astype(BF16)[None], (nc, C, C)), c3(_hi_lo(lw)))
    G3 = sums[:, :, 0:W] + sums[:, :, W:2 * W]
    total3 = jnp.broadcast_to(G3[:, C - 1:C, :], (nc, C, W))
    dec_ref[...] = jnp.exp2(total3).reshape(tb, W)
    G = G3.reshape(tb, W)
    inv = jnp.exp2(-G)
    to_end = jnp.exp2(total3 - G3).reshape(tb, W)
    kap = kk * jnp.exp2(G - lw)
    rho = r * jnp.exp2(G)
    kap3, rho3, v3 = c3(kap), c3(rho), c3(v)
    bet_kt3 = jnp.concatenate([c3(b * inv), c3(k * inv)], axis=1)
    betc3, kc3 = c3(b * to_end), c3(k * to_end)

    def bmm_nt(p, q):
        return lax.dot_general(p.astype(BF16), q.astype(BF16), (((2,), (2,)), ((0,), (0,))),
                               preferred_element_type=F32)

    def bmm_tn(p, q):
        return lax.dot_general(p.astype(BF16), q.astype(BF16), (((1,), (1,)), ((0,), (0,))),
                               preferred_element_type=F32)

    wi = lax.broadcasted_iota(jnp.int32, (C, 2 * C), 0)
    wj = lax.broadcasted_iota(jnp.int32, (C, 2 * C), 1) % C
    A_bk, B_bk = [], []
    kap_b, rho_b = kap.astype(BF16), rho.astype(BF16)
    for h in range(H):
        mask_b = head_mask[h].astype(BF16)
        lhs = jnp.concatenate([c3(kap_b * mask_b), c3(rho_b * mask_b)], axis=1)
        prod = bmm_nt(lhs, bet_kt3)
        A_bk.append(jnp.where(wj < wi, prod[:, 0:C], 0.0))
        B_bk.append(jnp.where(wj <= wi, prod[:, C:2 * C], 0.0))
    A_b = jnp.concatenate([t[:, :, 0:C] for t in A_bk], axis=0)
    B_b = [t[:, :, 0:C] for t in B_bk]
    v3_low = jnp.concatenate([jnp.zeros_like(v3), v3], axis=1)

    Dg = jnp.where(same_sub, A_b, 0.0)
    Lo = A_b - Dg
    D2 = _bmm(Dg, Dg)
    D4 = _bmm(D2, D2)
    D8 = _bmm(D4, D4)
    Dinv = _bmm(_bmm(_bmm(eye - Dg, eye + D2), eye + D4), eye + D8)
    Nn = _bmm(Dinv, Lo)
    N2 = _bmm(Nn, Nn)
    Tm = _bmm(_bmm(eye - Nn, eye + N2), Dinv)
    Tm = [Tm[h * nc:(h + 1) * nc] for h in range(H)]

    def per_head(mats, t):
        head_of_lane = (lax.broadcasted_iota(jnp.int32, (1, 1, t.shape[2]), 2) % W) // N
        acc = _bmm(mats[H - 1], t)
        for h in range(H - 1):
            acc = jnp.where(head_of_lane == h, _bmm(mats[h], t), acc)
        return acc

    akv = per_head(A_bk, v3_low)
    tk = per_head(Tm, jnp.concatenate([kap3, akv], axis=2))
    kap_p = tk[:, :, 0:W]
    v_p = tk[:, :, W:2 * W]
    bb = per_head(B_b, jnp.concatenate([kap_p, v_p], axis=2))
    qeff_ref[...] = (rho3 - bb[:, :, 0:W]).astype(BF16)
    yloc_ref[...] = per_head(B_bk, v3_low) - bb[:, :, W:2 * W]
    mlow_ref[...] = jnp.where(block_diag, bmm_tn(kap_p, betc3), 0.0).astype(BF16)
    nc_ref[...] = jnp.where(block_diag, bmm_tn(jnp.concatenate([v3, v_p], axis=1),
                                               jnp.concatenate([kc3, -betc3], axis=1)), 0.0)

    def chunk(ci, carry):
        r0 = pl.multiple_of(ci * C, C)
        s = s_ref[...]
        sb = s.astype(BF16)
        y = lax.dot_general(qeff_ref[ci], sb, (((1,), (1,)), ((), ())), preferred_element_type=F32)
        y_ref[pl.ds(r0, C), :] = y + yloc_ref[ci]
        s_ref[...] = (s * dec_ref[pl.ds(r0, 8), :][0:1, :]
                      - jnp.dot(sb, mlow_ref[ci], preferred_element_type=F32) + nc_ref[ci])
        return carry

    lax.fori_loop(0, nc, chunk, 0, unroll=True)

    y = y_ref[...]
    mean = head_total(y, two_term=True) / N
    yc = y - mean
    var = head_total(yc * yc) / N
    yn = yc * lax.rsqrt(var + RWKV_GN_EPS) * lng_ref[...] + lnb_ref[...]
    o_ref[0] = ((yn + bonus_ref[...]) * gate_ref[...]).astype(o_ref.dtype)


N_GLA_INPUTS = 7
N_GLA_SCRATCH = 6
N_RWKV_INPUTS = 12


def _gla_rwkv_kernel(*refs, gla_chunks, rwkv_chunks):
    n_in = N_GLA_INPUTS + N_RWKV_INPUTS
    gla_in, rwkv_in = refs[0:N_GLA_INPUTS], refs[N_GLA_INPUTS:n_in]
    gla_out, rwkv_out = refs[n_in], refs[n_in + 1]
    gla_scratch = refs[n_in + 2:n_in + 2 + N_GLA_SCRATCH]
    rwkv_scratch = refs[n_in + 2 + N_GLA_SCRATCH:]
    _gla_kernel(*gla_in, gla_out, *gla_scratch, n_chunk=gla_chunks)
    _rwkv_kernel(*rwkv_in, rwkv_out, *rwkv_scratch, n_chunk=rwkv_chunks)


def _gla_rwkv7(proj, gla_w2p, gla_gb, gla_ng, col_qk, col_v, col_og, col_gz, vecs, w2p, a2p, g2, col_rwkv, tb):
    B, T, _ = proj.shape
    hv = gla_ng.shape[1]
    hk = gla_gb.shape[1]
    W = g2.shape[1]
    cols = vecs["mu"].shape[1]
    names = ("mu", "w0", "w2", "a0", "a2", "g2", "kk", "ka", "rk", "lng", "lnb")
    params = dict(vecs, w2=w2p, a2=a2p, g2=g2)
    assert len(names) + 1 == N_RWKV_INPUTS
    const = lambda arr: pl.BlockSpec(arr.shape, lambda b, j: (0, 0))
    rows = lambda n, col: pl.BlockSpec((1, tb, n), lambda b, j: (b, j, col))
    n_chunk = tb // RWKV_CHUNK
    return pl.pallas_call(
        functools.partial(_gla_rwkv_kernel, gla_chunks=tb // GLA_CHUNK, rwkv_chunks=n_chunk),
        out_shape=(jax.ShapeDtypeStruct((B, T, hv), BF16), jax.ShapeDtypeStruct((B, T, W), BF16)),
        grid=(B, T // tb),
        in_specs=[rows(2 * hk, col_qk), rows(hv, col_v), rows(hv, col_og), rows(LANES, col_gz),
                  const(gla_w2p), const(gla_gb), const(gla_ng), rows(cols, col_rwkv)]
                 + [const(params[n]) for n in names],
        out_specs=(rows(hv, 0), rows(W, 0)),
        scratch_shapes=[pltpu.VMEM((hv, hk), F32), pltpu.VMEM((tb, hk), F32), pltpu.VMEM((tb, hk), F32),
                        pltpu.VMEM((tb, hk), BF16), pltpu.VMEM((tb, hk), BF16), pltpu.VMEM((tb, hv), F32),
                        pltpu.VMEM((W, W), F32), pltpu.VMEM((1, cols), F32),
                        pltpu.VMEM((tb, W), F32), pltpu.VMEM((tb, W), F32), pltpu.VMEM((tb, W), F32),
                        pltpu.VMEM((n_chunk, RWKV_CHUNK, W), BF16), pltpu.VMEM((n_chunk, RWKV_CHUNK, W), F32),
                        pltpu.VMEM((n_chunk, W, W), BF16), pltpu.VMEM((n_chunk, W, W), F32),
                        pltpu.VMEM((tb, W), F32)],
        compiler_params=_cparams(("arbitrary", "arbitrary")),
        name="gla_rwkv7_mixers",
    )(proj, proj, proj, proj, gla_w2p, gla_gb, gla_ng, proj, *[params[n] for n in names])


def _merge_kernel(x_ref, sc_ref, sh_ref, gt_ref, gpre_ref, gpost_ref, wg_ref, b0_ref, b1_ref, b2_ref,
                  b3_ref, wb_ref, wo_ref, o_ref):
    x = x_ref[0]
    D = x.shape[1]
    h = _rms_mod(x, gpre_ref[...], sc_ref[0], sh_ref[0]).astype(BF16)
    merged = None
    for g, br in enumerate((b0_ref, b1_ref, b2_ref, b3_ref)):
        gate = _sigmoid(jnp.dot(h, wg_ref[:, g * D:(g + 1) * D], preferred_element_type=F32))
        t = gate * jnp.dot(br[0], wb_ref[g], preferred_element_type=F32)
        merged = t if merged is None else merged + t
    y = _mm(merged, wo_ref[...])
    y = y * lax.rsqrt(jnp.mean(y * y, axis=-1, keepdims=True) + RMS_EPS) * gpost_ref[...]
    o_ref[0] = x + gt_ref[0] * y


def _merge(x, sc, sh, gt, gpre, gpost, w_gate, branches, w_branch, w_out, tm):
    B, T, D = x.shape
    bw = branches[0].shape[2]
    tok = lambda n: pl.BlockSpec((1, tm, n), lambda b, i: (b, i, 0))
    vec = pl.BlockSpec((1, 1, D), lambda b, i: (b, 0, 0))
    const2 = lambda arr: pl.BlockSpec(arr.shape, lambda b, i: (0, 0))
    return pl.pallas_call(
        _merge_kernel,
        out_shape=jax.ShapeDtypeStruct((B, T, D), F32),
        grid=(B, T // tm),
        in_specs=[tok(D), vec, vec, vec, const2(gpre), const2(gpost), const2(w_gate),
                  tok(bw), tok(bw), tok(bw), tok(bw),
                  pl.BlockSpec(w_branch.shape, lambda b, i: (0, 0, 0)), const2(w_out)],
        out_specs=tok(D),
        compiler_params=_cparams(("arbitrary", "arbitrary")),
        name="merge_out_proj",
    )(x, sc, sh, gt, gpre, gpost, w_gate, *branches, w_branch, w_out)


def _ffn_kernel(x_ref, sc_ref, sh_ref, gt_ref, gpre_ref, gpost_ref, wg_ref, wu_ref, wd_ref, o_ref, *, tf):
    x = x_ref[0]
    h = _rms_mod(x, gpre_ref[...], sc_ref[0], sh_ref[0]).astype(BF16)
    F = wg_ref.shape[1]
    acc = None
    for f0 in range(0, F, tf):
        f1 = min(f0 + tf, F)
        gate = jnp.dot(h, wg_ref[:, f0:f1], preferred_element_type=F32)
        up = jnp.dot(h, wu_ref[:, f0:f1], preferred_element_type=F32)
        t = jnp.dot((_silu(gate) * up).astype(BF16), wd_ref[f0:f1, :], preferred_element_type=F32)
        acc = t if acc is None else acc + t
    y = acc * lax.rsqrt(jnp.mean(acc * acc, axis=-1, keepdims=True) + RMS_EPS) * gpost_ref[...]
    o_ref[0] = x + gt_ref[0] * y


def _dense_ffn(x, sc, sh, gt, gpre, gpost, wg, wu, wd, tm):
    B, T, D = x.shape
    tok = pl.BlockSpec((1, tm, D), lambda b, i: (b, i, 0))
    vec = pl.BlockSpec((1, 1, D), lambda b, i: (b, 0, 0))
    const2 = lambda arr: pl.BlockSpec(arr.shape, lambda b, i: (0, 0))
    return pl.pallas_call(
        functools.partial(_ffn_kernel, tf=FFN_CHUNK),
        out_shape=jax.ShapeDtypeStruct((B, T, D), F32),
        grid=(B, T // tm),
        in_specs=[tok, vec, vec, vec, const2(gpre), const2(gpost), const2(wg), const2(wu), const2(wd)],
        out_specs=tok,
        compiler_params=_cparams(("arbitrary", "arbitrary")),
        name="dense_swiglu",
    )(x, sc, sh, gt, gpre, gpost, wg, wu, wd)


MOE_TOKEN_TILE = 256
MOE_ROW_TILE = 512
SEG_ALIGN = 16
SEG_PIECES = (256, 128, 64, 32, 16)
MOE_SMALL_SEG = 128


def _route_kernel(x_ref, sc_ref, sh_ref, gpre_ref, rw_ref, rb_ref, h_ref, mi_ref, mp_ref, cnt_ref):
    tm = x_ref.shape[1]
    h = _rms_mod(x_ref[0], gpre_ref[...], sc_ref[0], sh_ref[0])
    h_ref[...] = h.astype(BF16)
    h_hi = h.astype(BF16)
    h_lo = (h - h_hi.astype(F32)).astype(BF16)
    w = rw_ref[...]
    w_hi = w.astype(BF16)
    w_lo = (w - w_hi.astype(F32)).astype(BF16)
    logits = jnp.dot(jnp.concatenate([h_hi, h_lo, h_hi], axis=1), jnp.concatenate([w_hi, w_hi, w_lo], axis=0),
                     preferred_element_type=F32) + rb_ref[...]
    lane = lax.broadcasted_iota(jnp.int32, logits.shape, 1)
    v1 = jnp.max(logits, axis=-1, keepdims=True)
    i1 = jnp.min(jnp.where(logits == v1, lane, LANES), axis=-1, keepdims=True)
    rest = jnp.where(lane == i1, -jnp.inf, logits)
    v2 = jnp.max(rest, axis=-1, keepdims=True)
    i2 = jnp.min(jnp.where(rest == v2, lane, LANES), axis=-1, keepdims=True)
    e2 = jnp.exp(v2 - v1)
    p1 = 1.0 / (1.0 + e2)
    p2 = e2 / (1.0 + e2)
    oh1 = (lane == i1).astype(F32)
    oh2 = (lane == i2).astype(F32)
    both = oh1 + oh2
    earlier = (lax.broadcasted_iota(jnp.int32, (tm, tm), 1)
               < lax.broadcasted_iota(jnp.int32, (tm, tm), 0)).astype(BF16)
    before = jnp.dot(earlier, both.astype(BF16), preferred_element_type=F32)
    r1 = jnp.sum(oh1 * before, axis=-1, keepdims=True).astype(jnp.int32)
    r2 = jnp.sum(oh2 * before, axis=-1, keepdims=True).astype(jnp.int32)
    col = lax.broadcasted_iota(jnp.int32, mi_ref.shape, 1)
    mi_ref[...] = jnp.where(col == 0, i1, jnp.where(col == 1, i2, jnp.where(col == 2, r1,
                                                                           jnp.where(col == 3, r2, 0))))
    mp_ref[...] = jnp.where(col == 0, p1, jnp.where(col == 1, p2, 0.0))
    cnt_ref[0] = jnp.sum(both, axis=0, keepdims=True).astype(jnp.int32)


def _segment_pieces(n_rows):
    out = []
    for s in SEG_PIECES:
        if s == MOE_TOKEN_TILE:
            out.append((n_rows == s, 0, s))
        else:
            out.append(((n_rows & s) != 0, pl.multiple_of((n_rows // (2 * s)) * (2 * s), SEG_ALIGN), s))
    return out


def _all_segments_small(cnt_ref, tile):
    most = cnt_ref[tile * N_EXPERTS]
    for e in range(1, N_EXPERTS):
        most = jnp.maximum(most, cnt_ref[tile * N_EXPERTS + e])
    return most <= MOE_SMALL_SEG


def _dispatch_kernel(seg_ref, cnt_ref, h_ref, mit_ref, init_ref, xs_ref, buf_ref, sem):
    del init_ref
    tm = h_ref.shape[0]
    i = pl.program_id(0)
    e1, e2 = mit_ref[0:1, :], mit_ref[1:2, :]
    r1, r2 = mit_ref[2:3, :], mit_ref[3:4, :]
    slot = i % 2

    def compact(cap):
        row = lax.broadcasted_iota(jnp.int32, (cap, tm), 0)
        select = jnp.concatenate(
            [jnp.logical_or(jnp.logical_and(e1 == e, r1 == row), jnp.logical_and(e2 == e, r2 == row))
             for e in range(N_EXPERTS)], axis=0).astype(BF16)
        rows = jnp.dot(select, h_ref[...], preferred_element_type=F32).astype(BF16)
        for e in range(N_EXPERTS):
            buf_ref[slot, e * tm:e * tm + cap, :] = rows[e * cap:(e + 1) * cap]

    small = _all_segments_small(cnt_ref, i)
    pl.when(small)(functools.partial(compact, MOE_SMALL_SEG))
    pl.when(jnp.logical_not(small))(functools.partial(compact, tm))

    def segment_copies(tile, buf, e):
        n = cnt_ref[tile * N_EXPERTS + e]
        n_rows = ((n + SEG_ALIGN - 1) // SEG_ALIGN) * SEG_ALIGN
        dst = pl.multiple_of(seg_ref[tile * N_EXPERTS + e], SEG_ALIGN)
        return [(cond, pltpu.make_async_copy(buf_ref.at[buf, pl.ds(e * tm + off, s), :],
                                             xs_ref.at[pl.ds(dst + off, s), :], sem))
                for cond, off, s in _segment_pieces(n_rows)]

    def for_all_segments(tile, buf, action):
        for e in range(N_EXPERTS):
            for cond, cp in segment_copies(tile, buf, e):
                pl.when(cond)(getattr(cp, action))

    @pl.when(i > 0)
    def _():
        for_all_segments(i - 1, 1 - slot, "wait")

    for_all_segments(i, slot, "start")

    @pl.when(i == pl.num_programs(0) - 1)
    def _():
        for_all_segments(i, slot, "wait")


def _expert_kernel(te_ref, nv_ref, xs_ref, wg_ref, wu_ref, wd_ref, ys_ref, acc_ref):
    del te_ref
    r = pl.program_id(0)
    f = pl.program_id(1)
    valid = r < nv_ref[0]

    @pl.when(jnp.logical_and(valid, f == 0))
    def _():
        acc_ref[...] = jnp.zeros_like(acc_ref)

    @pl.when(valid)
    def _():
        x = xs_ref[...]
        tf = wg_ref.shape[2]
        acc = acc_ref[...]
        for f0 in range(0, tf, FFN_CHUNK):
            f1 = min(f0 + FFN_CHUNK, tf)
            gate = jnp.dot(x, wg_ref[0, :, f0:f1], preferred_element_type=F32)
            up = jnp.dot(x, wu_ref[0, :, f0:f1], preferred_element_type=F32)
            acc = acc + jnp.dot((_silu(gate) * up).astype(BF16), wd_ref[0, f0:f1, :],
                                preferred_element_type=F32)
        acc_ref[...] = acc

    @pl.when(f == pl.num_programs(1) - 1)
    def _():
        @pl.when(valid)
        def _():
            ys_ref[...] = acc_ref[...].astype(ys_ref.dtype)

        @pl.when(jnp.logical_not(valid))
        def _():
            ys_ref[...] = jnp.zeros_like(ys_ref)


def _combine_kernel(seg_ref, cnt_ref, ys_ref, mi_ref, mp_ref, x_ref, gt_ref, gpost_ref, o_ref, win_ref,
                    pair_ref, sem):
    tm = x_ref.shape[1]
    i = pl.program_id(0) * pl.num_programs(1) + pl.program_id(1)
    n_tiles = pl.num_programs(0) * pl.num_programs(1)
    slot = i % 2

    def window_copy(tile, buf, e, rows):
        src = pl.multiple_of(seg_ref[tile * N_EXPERTS + e], SEG_ALIGN)
        return pltpu.make_async_copy(ys_ref.at[pl.ds(src, rows), :], win_ref.at[buf, pl.ds(e * rows, rows), :],
                                     sem.at[buf, e])

    def for_all_windows(tile, buf, action):
        small = _all_segments_small(cnt_ref, tile)
        for rows, cond in ((MOE_SMALL_SEG, small), (tm, jnp.logical_not(small))):
            @pl.when(cond)
            def _():
                for e in range(N_EXPERTS):
                    getattr(window_copy(tile, buf, e, rows), action)()

    @pl.when(i == 0)
    def _():
        for_all_windows(i, slot, "start")

    @pl.when(i + 1 < n_tiles)
    def _():
        for_all_windows(i + 1, 1 - slot, "start")

    for_all_windows(i, slot, "wait")
    e1, e2 = mi_ref[:, 0:1], mi_ref[:, 1:2]
    r1, r2 = mi_ref[:, 2:3], mi_ref[:, 3:4]

    def expand_rows(cap):
        col = lax.broadcasted_iota(jnp.int32, (tm, cap), 1)
        expand = jnp.concatenate(
            [jnp.concatenate([jnp.logical_and(e1 == e, r1 == col), jnp.logical_and(e2 == e, r2 == col)], axis=0)
             for e in range(N_EXPERTS)], axis=1).astype(BF16)
        pair_ref[...] = jnp.dot(expand, win_ref[slot, 0:N_EXPERTS * cap, :], preferred_element_type=F32)

    small = _all_segments_small(cnt_ref, i)
    pl.when(small)(functools.partial(expand_rows, MOE_SMALL_SEG))
    pl.when(jnp.logical_not(small))(functools.partial(expand_rows, tm))
    y = mp_ref[:, 0:1] * pair_ref[0:tm, :] + mp_ref[:, 1:2] * pair_ref[tm:2 * tm, :]
    y = y * lax.rsqrt(jnp.mean(y * y, axis=-1, keepdims=True) + RMS_EPS) * gpost_ref[...]
    o_ref[0] = x_ref[0] + gt_ref[0] * y


def _moe_ffn(x, sc, sh, gt, gpre, gpost, rw, rb, wg, wu, wd, tf):
    B, T, D = x.shape
    E, _, F = wg.shape
    tm = min(MOE_TOKEN_TILE, T)
    assert tm == MOE_TOKEN_TILE and E == N_EXPERTS
    nT = T // tm
    n_tok_tiles = B * nT
    N = B * T
    max_rows = 2 * N + n_tok_tiles * E * (SEG_ALIGN - 1) + E * (MOE_ROW_TILE - SEG_ALIGN)
    n_row_tiles = -(-max_rows // MOE_ROW_TILE) + 1
    P = n_row_tiles * MOE_ROW_TILE

    vec = pl.BlockSpec((1, 1, D), lambda b, j: (b, 0, 0))
    const2 = lambda arr: pl.BlockSpec(arr.shape, lambda b, j: (0, 0))
    flat = lambda n: pl.BlockSpec((tm, n), lambda b, j: (b * nT + j, 0))
    h, mi, mp, cnt = pl.pallas_call(
        _route_kernel,
        out_shape=(jax.ShapeDtypeStruct((N, D), BF16), jax.ShapeDtypeStruct((N, 8), jnp.int32),
                   jax.ShapeDtypeStruct((N, 8), F32), jax.ShapeDtypeStruct((n_tok_tiles, 1, LANES), jnp.int32)),
        grid=(B, nT),
        in_specs=[pl.BlockSpec((1, tm, D), lambda b, j: (b, j, 0)), vec, vec, const2(gpre), const2(rw),
                  const2(rb)],
        out_specs=(flat(D), flat(8), flat(8), pl.BlockSpec((1, 1, LANES), lambda b, j: (b * nT + j, 0, 0))),
        compiler_params=_cparams(("arbitrary", "arbitrary")),
        name="moe_route",
    )(x, sc, sh, gpre, rw, rb)

    counts = cnt[:, 0, :E]
    seg_len = (counts + SEG_ALIGN - 1) // SEG_ALIGN * SEG_ALIGN
    group_len = (jnp.sum(seg_len, axis=0) + MOE_ROW_TILE - 1) // MOE_ROW_TILE * MOE_ROW_TILE
    group_end = jnp.cumsum(group_len)
    seg_start = (group_end - group_len)[None, :] + jnp.cumsum(seg_len, axis=0) - seg_len
    seg_start = seg_start.reshape(-1).astype(jnp.int32)
    counts = counts.reshape(-1)
    n_valid = (group_end[-1:] // MOE_ROW_TILE).astype(jnp.int32)
    tile_first_row = jnp.arange(n_row_tiles, dtype=jnp.int32) * MOE_ROW_TILE
    tile_expert = jnp.minimum(jnp.sum(tile_first_row[:, None] >= group_end[None, :], axis=1), E - 1)
    tile_expert = tile_expert.astype(jnp.int32)

    xs = pl.pallas_call(
        _dispatch_kernel,
        out_shape=jax.ShapeDtypeStruct((P, D), BF16),
        grid_spec=pltpu.PrefetchScalarGridSpec(
            num_scalar_prefetch=2,
            grid=(n_tok_tiles,),
            in_specs=[pl.BlockSpec((tm, D), lambda i, seg, n: (i, 0)),
                      pl.BlockSpec((8, tm), lambda i, seg, n: (0, i)),
                      pl.BlockSpec(memory_space=pl.ANY)],
            out_specs=pl.BlockSpec(memory_space=pl.ANY),
            scratch_shapes=[pltpu.VMEM((2, E * tm, D), BF16), pltpu.SemaphoreType.DMA(())]),
        input_output_aliases={4: 0},
        compiler_params=_cparams(("arbitrary",)),
        name="moe_dispatch",
    )(seg_start, counts, h, mi.T, jnp.zeros((P, D), BF16))

    nf = F // tf
    live = lambda r, f, nv: jnp.where(r < nv[0], f, nf - 1)
    ys = pl.pallas_call(
        _expert_kernel,
        out_shape=jax.ShapeDtypeStruct((P, D), BF16),
        grid_spec=pltpu.PrefetchScalarGridSpec(
            num_scalar_prefetch=2,
            grid=(n_row_tiles, nf),
            in_specs=[pl.BlockSpec((MOE_ROW_TILE, D), lambda r, f, te, nv: (r, 0)),
                      pl.BlockSpec((1, D, tf), lambda r, f, te, nv: (te[r], 0, live(r, f, nv))),
                      pl.BlockSpec((1, D, tf), lambda r, f, te, nv: (te[r], 0, live(r, f, nv))),
                      pl.BlockSpec((1, tf, D), lambda r, f, te, nv: (te[r], live(r, f, nv), 0))],
            out_specs=pl.BlockSpec((MOE_ROW_TILE, D), lambda r, f, te, nv: (r, 0)),
            scratch_shapes=[pltpu.VMEM((MOE_ROW_TILE, D), F32)]),
        compiler_params=_cparams(("arbitrary", "arbitrary")),
        name="moe_experts",
    )(tile_expert, n_valid, xs, wg, wu, wd)

    return pl.pallas_call(
        _combine_kernel,
        out_shape=jax.ShapeDtypeStruct((B, T, D), F32),
        grid_spec=pltpu.PrefetchScalarGridSpec(
            num_scalar_prefetch=2,
            grid=(B, nT),
            in_specs=[pl.BlockSpec(memory_space=pl.ANY),
                      pl.BlockSpec((tm, 8), lambda b, j, seg, n: (b * nT + j, 0)),
                      pl.BlockSpec((tm, 8), lambda b, j, seg, n: (b * nT + j, 0)),
                      pl.BlockSpec((1, tm, D), lambda b, j, seg, n: (b, j, 0)),
                      pl.BlockSpec((1, 1, D), lambda b, j, seg, n: (b, 0, 0)),
                      pl.BlockSpec(gpost.shape, lambda b, j, seg, n: (0, 0))],
            out_specs=pl.BlockSpec((1, tm, D), lambda b, j, seg, n: (b, j, 0)),
            scratch_shapes=[pltpu.VMEM((2, E * tm, D), BF16), pltpu.VMEM((2 * tm, D), F32),
                            pltpu.SemaphoreType.DMA((2, E))]),
        compiler_params=_cparams(("arbitrary", "arbitrary")),
        name="moe_combine",
    )(seg_start, counts, ys, mi, mp, x, gt, gpost)


def _rope_tables(positions, groups):
    d = 32
    inv = 1.0 / (ROPE_THETA ** (jnp.arange(0, d, 2, dtype=F32) / d))
    ang = positions.astype(F32)[..., None] * inv
    cos, sin = jnp.cos(ang), jnp.sin(ang)
    cos = jnp.tile(jnp.concatenate([cos, cos], axis=-1), (1, 1, groups))
    sin = jnp.tile(jnp.concatenate([-sin, sin], axis=-1), (1, 1, groups))
    return cos, sin


def _pad_rows(w, rows, offset):
    out = jnp.zeros((rows, w.shape[1]), w.dtype)
    return out.at[offset:offset + w.shape[0]].set(w)


def kernel(x, c, positions, ada_w, ada_b, norm_mix_pre, norm_mix_post, norm_ffn_pre, norm_ffn_post, w_in, gla_gate_w2, gla_gate_b, gla_norm, diff_lambda, diff_subln, conv_w, conv_b, conv_ln_g, conv_ln_b, rwkv_mu, rwkv_w0, rwkv_w2, rwkv_a0, rwkv_a2, rwkv_g2, rwkv_k_k, rwkv_k_a, rwkv_r_k, rwkv_ln_g, rwkv_ln_b, w_branch, w_out, ffn_w_gate, ffn_w_up, ffn_w_down, router_w, router_b, moe_w_gate, moe_w_up, moe_w_down):
    B, T, D = x.shape
    L = ada_w.shape[0]
    W = D // N_BRANCH
    hk = gla_gate_b.shape[1]
    decay_rank = rwkv_w2.shape[1]
    a_rank = rwkv_a2.shape[1]
    gate_rank = rwkv_g2.shape[1]
    assert decay_rank + a_rank == LANES and 2 * hk == W and gate_rank == LANES
    n_mix = (3 * W + decay_rank + a_rank + gate_rank) + 3 * W + 3 * W
    sizes = (hk, hk, W, W, GLA_GATE_RANK, W, W, W, W, W, 3 * W + LANES + gate_rank, N_BRANCH * D)
    offs = [0]
    for s in sizes:
        offs.append(offs[-1] + s)
    assert offs[-1] == w_in.shape[2]
    tm = min(TOKEN_TILE, T)
    tb = min(SEQ_BLOCK, T)

    mod = _modulation(c, ada_w, ada_b)
    cos, sin = _rope_tables(positions, LANES // 32)

    for l in range(L):
        m = mod[l].reshape(B, 1, 6 * D)
        sh_m, sc_m, gt_m, sh_f, sc_f, gt_f = [m[:, :, i * D:(i + 1) * D] for i in range(6)]

        wl = w_in[l]
        gz_cols = jnp.zeros((D, LANES), F32).at[:, :GLA_GATE_RANK].set(wl[:, offs[4]:offs[5]])
        w_mix = jnp.concatenate([wl[:, offs[10]:offs[11]], wl[:, offs[0]:offs[4]], wl[:, offs[5]:offs[8]],
                                 gz_cols], axis=1).astype(BF16)
        w_conv = wl[:, offs[8]:offs[10]].astype(BF16)
        w_gate = wl[:, offs[11]:offs[12]].astype(BF16)
        proj, o_conv = _in_projection_conv(x, sc_m, sh_m, norm_mix_pre[l][None], w_conv, w_mix,
                                           _pad_rows(conv_w[l], 32, 0), conv_b[l][None],
                                           conv_ln_g[l][None], conv_ln_b[l][None], tm)

        o_diff = _diff_attention(proj, cos, sin, diff_lambda[l],
                                 diff_subln[l][None], col_q=7, col_k=8, col_v=9, layer_idx=l)
        vecs = dict(mu=rwkv_mu[l][None], w0=rwkv_w0[l][None], a0=rwkv_a0[l][None],
                    kk=rwkv_k_k[l][None], ka=rwkv_k_a[l][None], rk=rwkv_r_k[l].reshape(1, W),
                    lng=rwkv_ln_g[l][None], lnb=rwkv_ln_b[l][None])
        o_gla, o_rwkv = _gla_rwkv7(
            proj, _pad_rows(gla_gate_w2[l], LANES, 0), gla_gate_b[l][None],
            jnp.tile(gla_norm[l], GLA_HEADS)[None], 4, 5, 6, n_mix // LANES,
            vecs, _pad_rows(rwkv_w2[l], LANES, 0).astype(BF16),
            _pad_rows(rwkv_a2[l], LANES, decay_rank).astype(BF16), rwkv_g2[l].astype(BF16), 0, tb)
        x = _merge(x, sc_m, sh_m, gt_m, norm_mix_pre[l][None], norm_mix_post[l][None], w_gate,
                   (o_gla, o_diff, o_conv, o_rwkv), w_branch[l].astype(BF16), w_out[l].astype(BF16), tm)

        i = l // 2
        if l % 2 == 0:
            x = _dense_ffn(x, sc_f, sh_f, gt_f, norm_ffn_pre[l][None], norm_ffn_post[l][None],
                           ffn_w_gate[i].astype(BF16), ffn_w_up[i].astype(BF16),
                           ffn_w_down[i].astype(BF16), tm)
        else:
            rw = jnp.zeros((D, LANES), F32).at[:, :N_EXPERTS].set(router_w[i])
            rb = jnp.full((1, LANES), -jnp.inf, F32).at[0, :N_EXPERTS].set(router_b[i])
            x = _moe_ffn(x, sc_f, sh_f, gt_f, norm_ffn_pre[l][None], norm_ffn_post[l][None], rw, rb,
                         moe_w_gate[i].astype(BF16), moe_w_up[i].astype(BF16),
                         moe_w_down[i].astype(BF16), tf=moe_w_gate.shape[3])
    return x
```

```python
import functools
import math

import jax
import jax.numpy as jnp
from jax import lax
from jax.experimental import pallas as pl
from jax.experimental.pallas import tpu as pltpu

F32 = jnp.float32
BF16 = jnp.bfloat16
HIGHEST = lax.Precision.HIGHEST
LOG2_E = math.log2(math.e)

N_BRANCH = 4
GLA_HEADS = 4
GLA_GATE_RANK = 16
GLA_GATE_NORMALIZER = 16.0
GLA_CHUNK = 32
GLA_KEY_GROUP = 8
DIFF_HEADS = 4
ROPE_THETA = 10000.0
CONV_WIDTH = 31
RWKV_HEADS = 4
RWKV_CHUNK = 64
RWKV_SUB = 16
N_EXPERTS = 8
RMS_EPS = 1e-6
LN_EPS = 1e-5
RWKV_GN_EPS = 64e-5
LANES = 128
SUBLANES = 8
VMEM_LIMIT = 56 * 1024 * 1024

TOKEN_TILE = 512
SEQ_BLOCK = 512
ATTN_BLOCK = 512
CONV_ROWS = 128
FFN_CHUNK = 512


def _cparams(sem):
    return pltpu.CompilerParams(dimension_semantics=sem, vmem_limit_bytes=VMEM_LIMIT)


def _mm(a, b):
    return jnp.dot(a.astype(BF16), b.astype(BF16), preferred_element_type=F32)


def _bmm(a, b):
    return lax.dot_general(a.astype(BF16), b.astype(BF16), (((2,), (1,)), ((0,), (0,))),
                           preferred_element_type=F32)


def _mm_f32(a, b):
    return jnp.dot(a, b, precision=HIGHEST, preferred_element_type=F32)


def _hi_lo(x):
    hi = x.astype(BF16)
    return jnp.concatenate([hi, (x - hi.astype(F32)).astype(BF16)], axis=1)


def _sigmoid(x):
    return 0.5 * jnp.tanh(0.5 * x) + 0.5


def _silu(x):
    return x * _sigmoid(x)


def _softplus(x):
    return jnp.maximum(x, 0.0) + jnp.log(1.0 + jnp.exp(-jnp.abs(x)))


def _group_matrix(n, group):
    r = lax.broadcasted_iota(jnp.int32, (n, n), 0) // group
    c = lax.broadcasted_iota(jnp.int32, (n, n), 1) // group
    return r == c


def _rms_mod(x, gain, scale, shift):
    y = x * lax.rsqrt(jnp.mean(x * x, axis=-1, keepdims=True) + RMS_EPS)
    return y * gain * (1.0 + scale) + shift


def _mod_kernel(c_ref, w_ref, b_ref, o_ref):
    o_ref[0] = _mm_f32(_silu(c_ref[...]), w_ref[0]) + b_ref[0]


def _modulation(c, ada_w, ada_b):
    L, D, M = ada_w.shape
    B = c.shape[0]
    tn = M // 4
    return pl.pallas_call(
        _mod_kernel,
        out_shape=jax.ShapeDtypeStruct((L, B, M), F32),
        grid=(L, M // tn),
        in_specs=[pl.BlockSpec((B, D), lambda l, j: (0, 0)),
                  pl.BlockSpec((1, D, tn), lambda l, j: (l, 0, j)),
                  pl.BlockSpec((1, 1, tn), lambda l, j: (l, 0, j))],
        out_specs=pl.BlockSpec((1, B, tn), lambda l, j: (l, 0, j)),
        compiler_params=_cparams(("arbitrary", "arbitrary")),
        name="adaln_mod",
    )(c, ada_w, ada_b.reshape(L, 1, M))


CONV_HALO = 32


def _inproj_conv_kernel(x_ref, sc_ref, sh_ref, g_ref, wab_ref, w_ref, cw_ref, cb_ref, lg_ref, lb_ref,
                        o_ref, oc_ref, u_ref, *, rb):
    tm = x_ref.shape[1]
    ch = oc_ref.shape[2]
    h = _rms_mod(x_ref[0], g_ref[...], sc_ref[0], sh_ref[0]).astype(BF16)

    @pl.when(pl.program_id(1) == 0)
    def _():
        u_ref[0:CONV_HALO, :] = jnp.zeros((CONV_HALO, ch), F32)

    ab = jnp.dot(h, wab_ref[...], preferred_element_type=F32)
    u_ref[CONV_HALO:CONV_HALO + tm, :] = ab[:, 0:ch] * _sigmoid(ab[:, ch:2 * ch])
    o_ref[0] = jnp.dot(h, w_ref[...], preferred_element_type=F32)

    first = CONV_HALO - (CONV_WIDTH - 1)
    for i in range(tm // rb):
        r0 = i * rb
        win = u_ref[r0:r0 + rb + CONV_HALO, :]
        acc = jnp.zeros((rb, ch), F32)
        for s in range(SUBLANES):
            taps = [j for j in range(CONV_WIDTH) if (first + j) % SUBLANES == s]
            rolled = win if s == 0 else pltpu.roll(win, rb + CONV_HALO - s, 0)
            for j in taps:
                a0 = first + j - s
                acc = acc + cw_ref[j:j + 1, :] * rolled[a0:a0 + rb, :]
        y = acc + cb_ref[...]
        mu = jnp.mean(y, axis=-1, keepdims=True)
        yc = y - mu
        var = jnp.mean(yc * yc, axis=-1, keepdims=True)
        y = yc * lax.rsqrt(var + LN_EPS) * lg_ref[...] + lb_ref[...]
        oc_ref[0, r0:r0 + rb, :] = _silu(y).astype(oc_ref.dtype)
    u_ref[0:CONV_HALO, :] = u_ref[tm:tm + CONV_HALO, :]


def _in_projection_conv(x, sc, sh, gain, w_ab, w, conv_w, conv_b, ln_g, ln_b, tm):
    B, T, D = x.shape
    n = w.shape[1]
    ch = conv_w.shape[1]
    const = lambda arr: pl.BlockSpec(arr.shape, lambda b, i: (0, 0))
    return pl.pallas_call(
        functools.partial(_inproj_conv_kernel, rb=min(CONV_ROWS, tm)),
        out_shape=(jax.ShapeDtypeStruct((B, T, n), F32), jax.ShapeDtypeStruct((B, T, ch), BF16)),
        grid=(B, T // tm),
        in_specs=[pl.BlockSpec((1, tm, D), lambda b, i: (b, i, 0)),
                  pl.BlockSpec((1, 1, D), lambda b, i: (b, 0, 0)),
                  pl.BlockSpec((1, 1, D), lambda b, i: (b, 0, 0)),
                  const(gain), const(w_ab), const(w), const(conv_w), const(conv_b), const(ln_g), const(ln_b)],
        out_specs=(pl.BlockSpec((1, tm, n), lambda b, i: (b, i, 0)),
                   pl.BlockSpec((1, tm, ch), lambda b, i: (b, i, 0))),
        scratch_shapes=[pltpu.VMEM((tm + CONV_HALO, ch), F32)],
        compiler_params=_cparams(("arbitrary", "arbitrary")),
        name="in_proj_conv",
    )(x, sc, sh, gain, w_ab, w, conv_w, conv_b, ln_g, ln_b)


def _gla_kernel(qk_ref, v_ref, og_ref, gz_ref, w2_ref, gb_ref, ng_ref, o_ref, s_ref, g_ref, r_ref,
                qd_ref, kd_ref, oacc_ref, *, n_chunk):
    C = GLA_CHUNK
    hk = qk_ref.shape[2] // 2
    hv = v_ref.shape[2]
    dk = hk // GLA_HEADS
    dv = hv // GLA_HEADS

    tb = n_chunk * C

    @pl.when(pl.program_id(1) == 0)
    def _():
        s_ref[...] = jnp.zeros_like(s_ref)

    z = _mm(gz_ref[0], w2_ref[...]) + gb_ref[...]
    gk = (jnp.minimum(z, 0.0) - jnp.log(1.0 + jnp.exp(-jnp.abs(z)))) * (LOG2_E / GLA_GATE_NORMALIZER)
    tri = (lax.broadcasted_iota(jnp.int32, (C, C), 1)
           <= lax.broadcasted_iota(jnp.int32, (C, C), 0)).astype(BF16)
    sums = _bmm(jnp.broadcast_to(tri[None], (n_chunk, C, C)), _hi_lo(gk).reshape(n_chunk, C, 2 * hk))
    G3 = sums[:, :, 0:hk] + sums[:, :, hk:2 * hk]
    G_all = G3.reshape(tb, hk)
    R_all = (jnp.broadcast_to(G3[:, C - 1:C, :], (n_chunk, C, hk)) - G3).reshape(tb, hk)
    g_ref[...] = G_all
    r_ref[...] = R_all
    qd_ref[...] = (qk_ref[0, :, 0:hk] * (dk ** -0.5) * jnp.exp2(G_all)).astype(BF16)
    kd_ref[...] = (qk_ref[0, :, hk:2 * hk] * jnp.exp2(R_all)).astype(BF16)

    def key_not_after_query(nj, ni):
        return (lax.broadcasted_iota(jnp.int32, (nj, ni, hk), 0)
                <= lax.broadcasted_iota(jnp.int32, (nj, ni, hk), 1))

    group_starts = range(0, C, GLA_KEY_GROUP)
    causal = {j0: key_not_after_query(GLA_KEY_GROUP, C - j0) for j0 in group_starts}
    er = lax.broadcasted_iota(jnp.int32, (hk, hv), 0) // dk
    ec = lax.broadcasted_iota(jnp.int32, (hk, hv), 1) // dv
    expand = (er == ec).astype(BF16)
    sr = lax.broadcasted_iota(jnp.int32, (hv, hk), 0) // dv
    scol = lax.broadcasted_iota(jnp.int32, (hv, hk), 1) // dk
    state_mask = sr == scol
    head_mean = (_group_matrix(hv, dv).astype(F32) / dv).astype(BF16)

    def chunk(ci, carry):
        r0 = pl.multiple_of(ci * C, C)
        rows = pl.ds(r0, C)
        q = qk_ref[0, rows, 0:hk] * (dk ** -0.5)
        k = qk_ref[0, rows, hk:2 * hk]
        v = v_ref[0, rows, :]
        G = g_ref[rows, :]
        def keys_to_queries(j0, i0, keep):
            nj, ni = keep.shape[0], keep.shape[1]
            pair = (nj, ni, hk)
            g_diff = (jnp.broadcast_to(G[i0:i0 + ni][None], pair)
                      - jnp.broadcast_to(G[j0:j0 + nj][:, None, :], pair))
            decay = jnp.exp2(jnp.where(keep, g_diff, -jnp.inf))
            p = (jnp.broadcast_to(q[i0:i0 + ni][None], pair)
                 * jnp.broadcast_to(k[j0:j0 + nj][:, None, :], pair) * decay)
            a_exp = jnp.dot(p.reshape(nj * ni, hk).astype(BF16), expand, preferred_element_type=F32)
            v_j = jnp.broadcast_to(v[j0:j0 + nj][:, None, :], (nj, ni, hv))
            return jnp.sum(a_exp.reshape(nj, ni, hv) * v_j, axis=0)

        o_intra = keys_to_queries(0, 0, causal[0])
        for j0 in group_starts[1:]:
            part = keys_to_queries(j0, j0, causal[j0])
            o_intra = o_intra + jnp.concatenate([jnp.zeros((j0, hv), F32), part], axis=0)
        g_total = G[0:1, :] + r_ref[pl.ds(r0, 8), :][0:1, :]
        s = s_ref[...]
        o_inter = lax.dot_general(qd_ref[rows, :], s.astype(BF16), (((1,), (1,)), ((), ())),
                                  preferred_element_type=F32)
        kv = lax.dot_general(v.astype(BF16), kd_ref[rows, :], (((0,), (0,)), ((), ())),
                             preferred_element_type=F32)
        s_ref[...] = s * jnp.exp2(g_total) + jnp.where(state_mask, kv, 0.0)
        oacc_ref[rows, :] = o_intra + o_inter
        return carry

    lax.fori_loop(0, n_chunk, chunk, 0, unroll=True)

    o = oacc_ref[...]
    ms = jnp.dot(_hi_lo(o * o), jnp.concatenate([head_mean, head_mean], axis=0),
                 preferred_element_type=F32)
    o = o * lax.rsqrt(ms + RMS_EPS) * ng_ref[...] * _silu(og_ref[0])
    o_ref[0] = o.astype(o_ref.dtype)


def _rope(t, cos, sin_signed):
    d = 32
    half = d // 2
    out = []
    for s in range(t.shape[1] // LANES):
        x = t[:, s * LANES:(s + 1) * LANES]
        lane = lax.broadcasted_iota(jnp.int32, x.shape, 1)
        up = pltpu.roll(x, LANES - half, 1)
        down = pltpu.roll(x, half, 1)
        rot = jnp.where((lane % d) < half, up, down)
        out.append(x * cos + rot * sin_signed)
    return jnp.concatenate(out, axis=1)


def _diff_kernel(q_ref, k_ref, v_ref, cosq_ref, sinq_ref, cosk_ref, sink_ref, lam_ref, g_ref, o_ref,
                 ks, vs, *, tq, lam_init):
    H = DIFF_HEADS
    d = q_ref.shape[2] // (2 * H)
    dv = v_ref.shape[2] // H
    j = pl.program_id(1)

    @pl.when(j == 0)
    def _():
        k = _rope(k_ref[0], cosk_ref[0], sink_ref[0])
        v = v_ref[0]
        for hc in range(2 * H):
            ks[hc] = k[:, hc * d:(hc + 1) * d].astype(BF16)
        ones_col = (lax.broadcasted_iota(jnp.int32, (v.shape[0], dv), 1) == 0).astype(BF16)
        for h in range(H):
            vs[h] = jnp.concatenate([v[:, h * dv:(h + 1) * dv].astype(BF16), ones_col], axis=1)

    q = _rope(q_ref[0], cosq_ref[0], sinq_ref[0]) * (d ** -0.5 * LOG2_E)
    lp = lam_ref[...]
    lam = (jnp.exp(jnp.sum(lp[0:1] * lp[1:2], axis=-1, keepdims=True))
           - jnp.exp(jnp.sum(lp[2:3] * lp[3:4], axis=-1, keepdims=True)) + lam_init)
    on_or_below_diag = (lax.broadcasted_iota(jnp.int32, (tq, tq), 1)
                        <= lax.broadcasted_iota(jnp.int32, (tq, tq), 0))

    def update(qh, kh, vh, m, acc, masked):
        s = lax.dot_general(qh, kh, (((1,), (1,)), ((), ())), preferred_element_type=F32)
        if masked:
            s = jnp.where(on_or_below_diag, s, -jnp.inf)
        m_new = jnp.maximum(m, jnp.max(s, axis=-1, keepdims=True))
        p = jnp.exp2(s - m_new).astype(BF16)
        acc = jnp.exp2(m - m_new) * acc + jnp.dot(p, vh, preferred_element_type=F32)
        return m_new, acc

    qs = [q[:, hc * d:(hc + 1) * d].astype(BF16) for hc in range(2 * H)]

    def kv_block(kb, carry, masked):
        rows = pl.ds(pl.multiple_of(kb * tq, tq), tq)
        out = []
        for hc in range(2 * H):
            m, acc = carry[2 * hc], carry[2 * hc + 1]
            out.extend(update(qs[hc], ks[hc, rows, :], vs[hc // 2, rows, :], m, acc, masked))
        return tuple(out)

    m0 = jnp.full((tq, 1), -jnp.inf, F32)
    a0 = jnp.zeros((tq, 2 * dv), F32)
    carry = lax.fori_loop(0, j, lambda kb, c: kv_block(kb, c, False), (m0, a0) * (2 * H))
    carry = kv_block(j, carry, True)

    for h in range(H):
        a1, a2 = carry[4 * h + 1], carry[4 * h + 3]
        comp = [a[:, 0:dv] / a[:, dv:dv + 1] for a in (a1, a2)]
        o = comp[0] - lam * comp[1]
        o = o * lax.rsqrt(jnp.mean(o * o, axis=-1, keepdims=True) + RMS_EPS)
        o = o * g_ref[...] * (1.0 - lam_init)
        o_ref[0, :, h * dv:(h + 1) * dv] = o.astype(o_ref.dtype)


def _diff_attention(proj, cos, sin, lam_p, g, col_q, col_k, col_v, layer_idx):
    B, T, _ = proj.shape
    H = DIFF_HEADS
    dv = g.shape[1]
    d = dv // 2
    w = 2 * H * d
    assert cos.shape[2] == LANES
    tq = min(ATTN_BLOCK, T)
    lam_init = 0.8 - 0.6 * math.exp(-0.3 * layer_idx)
    blk = lambda col: pl.BlockSpec((1, tq, w), lambda b, j: (b, j, col))
    full = lambda col: pl.BlockSpec((1, T, w), lambda b, j: (b, 0, col))
    tab_blk = pl.BlockSpec((1, tq, LANES), lambda b, j: (b, j, 0))
    tab_full = pl.BlockSpec((1, T, LANES), lambda b, j: (b, 0, 0))
    return pl.pallas_call(
        functools.partial(_diff_kernel, tq=tq, lam_init=lam_init),
        out_shape=jax.ShapeDtypeStruct((B, T, H * dv), BF16),
        grid=(B, T // tq),
        in_specs=[blk(col_q), full(col_k), full(col_v), tab_blk, tab_blk, tab_full, tab_full,
                  pl.BlockSpec(lam_p.shape, lambda b, j: (0, 0)),
                  pl.BlockSpec(g.shape, lambda b, j: (0, 0))],
        out_specs=pl.BlockSpec((1, tq, H * dv), lambda b, j: (b, j, 0)),
        scratch_shapes=[pltpu.VMEM((2 * H, T, d), BF16),
                        pltpu.VMEM((H, T, 2 * dv), BF16)],
        compiler_params=_cparams(("arbitrary", "arbitrary")),
        name="diff_attention",
    )(proj, proj, proj, cos, sin, cos, sin, lam_p, g)


def _rwkv_kernel(x_ref, mu_ref, w0_ref, w2_ref, a0_ref, a2_ref, g2_ref, kk_ref, ka_ref, rk_ref,
                 lng_ref, lnb_ref, o_ref, s_ref, prev_ref, gate_ref, bonus_ref, dec_ref, qeff_ref, yloc_ref,
                 mlow_ref, nc_ref, y_ref, *, n_chunk):
    C = RWKV_CHUNK
    H = RWKV_HEADS
    W = o_ref.shape[2]
    N = W // H

    @pl.when(pl.program_id(1) == 0)
    def _():
        s_ref[...] = jnp.zeros_like(s_ref)
        prev_ref[...] = jnp.zeros_like(prev_ref)

    lane = lax.broadcasted_iota(jnp.int32, (1, W), 1)
    head_mask = [(lane // N == h).astype(F32) for h in range(H)]
    block_diag = _group_matrix(W, N)
    head_sum = block_diag.astype(F32)
    ti = lax.broadcasted_iota(jnp.int32, (C, C), 0)
    tj = lax.broadcasted_iota(jnp.int32, (C, C), 1)
    tril_incl = ti >= tj
    tril_strict = ti > tj
    same_sub = (ti // RWKV_SUB) == (tj // RWKV_SUB)
    eye = (ti == tj).astype(F32)
    nc = n_chunk
    tb = nc * C

    def head_total(t, two_term=False):
        ones = head_sum.astype(BF16)
        if two_term:
            return jnp.dot(_hi_lo(t), jnp.concatenate([ones, ones], axis=0), preferred_element_type=F32)
        return jnp.dot(t.astype(BF16), ones, preferred_element_type=F32)

    x = x_ref[0]
    first_row = lax.broadcasted_iota(jnp.int32, (tb, 1), 0) == 0
    prev = jnp.where(first_row, prev_ref[...], pltpu.roll(x, 1, 0))
    prev_ref[...] = x[tb - 1:tb, :]
    xm = x + (prev - x) * mu_ref[...]
    r = xm[:, 0:W]
    k = xm[:, W:2 * W]
    v = xm[:, 2 * W:3 * W]
    zz = xm[:, 3 * W:3 * W + LANES]
    zg = xm[:, 3 * W + LANES:]
    w = -_softplus(-(w0_ref[...] + _mm(jnp.tanh(zz), w2_ref[...]))) - 0.5
    lw = -jnp.exp(w) * LOG2_E
    a = _sigmoid(a0_ref[...] + _mm(zz, a2_ref[...]))
    gate_ref[...] = _mm(_sigmoid(zg), g2_ref[...])
    kk = k * kk_ref[...]
    kk = kk / jnp.maximum(jnp.sqrt(head_total(kk * kk, two_term=True)), 1e-12)
    k = k * (1.0 + (a - 1.0) * ka_ref[...])
    b = kk * a
    bonus_ref[...] = head_total(r * k * rk_ref[...]) * v

    c3 = lambda t: t.reshape(nc, C, t.shape[1])
    sums = _bmm(jnp.broadcast_to(tril_incl.astype(BF16)[None], (nc, C, C)), c3(_hi_lo(lw)))
    G3 = sums[:, :, 0:W] + sums[:, :, W:2 * W]
    total3 = jnp.broadcast_to(G3[:, C - 1:C, :], (nc, C, W))
    dec_ref[...] = jnp.exp2(total3).reshape(tb, W)
    G = G3.reshape(tb, W)
    inv = jnp.exp2(-G)
    to_end = jnp.exp2(total3 - G3).reshape(tb, W)
    kap = kk * jnp.exp2(G - lw)
    rho = r * jnp.exp2(G)
    kap3, rho3, v3 = c3(kap), c3(rho), c3(v)
    bet_kt3 = jnp.concatenate([c3(b * inv), c3(k * inv)], axis=1)
    betc3, kc3 = c3(b * to_end), c3(k * to_end)

    def bmm_nt(p, q):
        return lax.dot_general(p.astype(BF16), q.astype(BF16), (((2,), (2,)), ((0,), (0,))),
                               preferred_element_type=F32)

    def bmm_tn(p, q):
        return lax.dot_general(p.astype(BF16), q.astype(BF16), (((1,), (1,)), ((0,), (0,))),
                               preferred_element_type=F32)

    wi = lax.broadcasted_iota(jnp.int32, (C, 2 * C), 0)
    wj = lax.broadcasted_iota(jnp.int32, (C, 2 * C), 1) % C
    A_bk, B_bk = [], []
    kap_b, rho_b = kap.astype(BF16), rho.astype(BF16)
    for h in range(H):
        mask_b = head_mask[h].astype(BF16)
        lhs = jnp.concatenate([c3(kap_b * mask_b), c3(rho_b * mask_b)], axis=1)
        prod = bmm_nt(lhs, bet_kt3)
        A_bk.append(jnp.where(wj < wi, prod[:, 0:C], 0.0))
        B_bk.append(jnp.where(wj <= wi, prod[:, C:2 * C], 0.0))
    A_b = jnp.concatenate([t[:, :, 0:C] for t in A_bk], axis=0)
    B_b = [t[:, :, 0:C] for t in B_bk]
    v3_low = jnp.concatenate([jnp.zeros_like(v3), v3], axis=1)

    Dg = jnp.where(same_sub, A_b, 0.0)
    Lo = A_b - Dg
    D2 = _bmm(Dg, Dg)
    D4 = _bmm(D2, D2)
    D8 = _bmm(D4, D4)
    Dinv = _bmm(_bmm(_bmm(eye - Dg, eye + D2), eye + D4), eye + D8)
    Nn = _bmm(Dinv, Lo)
    N2 = _bmm(Nn, Nn)
    Tm = _bmm(_bmm(eye - Nn, eye + N2), Dinv)
    Tm = [Tm[h * nc:(h + 1) * nc] for h in range(H)]

    def per_head(mats, t):
        head_of_lane = (lax.broadcasted_iota(jnp.int32, (1, 1, t.shape[2]), 2) % W) // N
        acc = _bmm(mats[H - 1], t)
        for h in range(H - 1):
            acc = jnp.where(head_of_lane == h, _bmm(mats[h], t), acc)
        return acc

    akv = per_head(A_bk, v3_low)
    tk = per_head(Tm, jnp.concatenate([kap3, akv], axis=2))
    kap_p = tk[:, :, 0:W]
    v_p = tk[:, :, W:2 * W]
    bb = per_head(B_b, jnp.concatenate([kap_p, v_p], axis=2))
    qeff_ref[...] = (rho3 - bb[:, :, 0:W]).astype(BF16)
    yloc_ref[...] = per_head(B_bk, v3_low) - bb[:, :, W:2 * W]
    mlow_ref[...] = jnp.where(block_diag, bmm_tn(kap_p, betc3), 0.0).astype(BF16)
    nc_ref[...] = jnp.where(block_diag, bmm_tn(jnp.concatenate([v3, v_p], axis=1),
                                               jnp.concatenate([kc3, -betc3], axis=1)), 0.0)

    def chunk(ci, carry):
        r0 = pl.multiple_of(ci * C, C)
        s = s_ref[...]
        sb = s.astype(BF16)
        y = lax.dot_general(qeff_ref[ci], sb, (((1,), (1,)), ((), ())), preferred_element_type=F32)
        y_ref[pl.ds(r0, C), :] = y + yloc_ref[ci]
        s_ref[...] = (s * dec_ref[pl.ds(r0, 8), :][0:1, :]
                      - jnp.dot(sb, mlow_ref[ci], preferred_element_type=F32) + nc_ref[ci])
        return carry

    lax.fori_loop(0, nc, chunk, 0, unroll=True)

    y = y_ref[...]
    mean = head_total(y, two_term=True) / N
    yc = y - mean
    var = head_total(yc * yc) / N
    yn = yc * lax.rsqrt(var + RWKV_GN_EPS) * lng_ref[...] + lnb_ref[...]
    o_ref[0] = ((yn + bonus_ref[...]) * gate_ref[...]).astype(o_ref.dtype)


N_GLA_INPUTS = 7
N_GLA_SCRATCH = 6
N_RWKV_INPUTS = 12


def _gla_rwkv_kernel(*refs, gla_chunks, rwkv_chunks):
    n_in = N_GLA_INPUTS + N_RWKV_INPUTS
    gla_in, rwkv_in = refs[0:N_GLA_INPUTS], refs[N_GLA_INPUTS:n_in]
    gla_out, rwkv_out = refs[n_in], refs[n_in + 1]
    gla_scratch = refs[n_in + 2:n_in + 2 + N_GLA_SCRATCH]
    rwkv_scratch = refs[n_in + 2 + N_GLA_SCRATCH:]
    _gla_kernel(*gla_in, gla_out, *gla_scratch, n_chunk=gla_chunks)
    _rwkv_kernel(*rwkv_in, rwkv_out, *rwkv_scratch, n_chunk=rwkv_chunks)


def _gla_rwkv7(proj, gla_w2p, gla_gb, gla_ng, col_qk, col_v, col_og, col_gz, vecs, w2p, a2p, g2, col_rwkv, tb):
    B, T, _ = proj.shape
    hv = gla_ng.shape[1]
    hk = gla_gb.shape[1]
    W = g2.shape[1]
    cols = vecs["mu"].shape[1]
    names = ("mu", "w0", "w2", "a0", "a2", "g2", "kk", "ka", "rk", "lng", "lnb")
    params = dict(vecs, w2=w2p, a2=a2p, g2=g2)
    assert len(names) + 1 == N_RWKV_INPUTS
    const = lambda arr: pl.BlockSpec(arr.shape, lambda b, j: (0, 0))
    rows = lambda n, col: pl.BlockSpec((1, tb, n), lambda b, j: (b, j, col))
    n_chunk = tb // RWKV_CHUNK
    return pl.pallas_call(
        functools.partial(_gla_rwkv_kernel, gla_chunks=tb // GLA_CHUNK, rwkv_chunks=n_chunk),
        out_shape=(jax.ShapeDtypeStruct((B, T, hv), BF16), jax.ShapeDtypeStruct((B, T, W), BF16)),
        grid=(B, T // tb),
        in_specs=[rows(2 * hk, col_qk), rows(hv, col_v), rows(hv, col_og), rows(LANES, col_gz),
                  const(gla_w2p), const(gla_gb), const(gla_ng), rows(cols, col_rwkv)]
                 + [const(params[n]) for n in names],
        out_specs=(rows(hv, 0), rows(W, 0)),
        scratch_shapes=[pltpu.VMEM((hv, hk), F32), pltpu.VMEM((tb, hk), F32), pltpu.VMEM((tb, hk), F32),
                        pltpu.VMEM((tb, hk), BF16), pltpu.VMEM((tb, hk), BF16), pltpu.VMEM((tb, hv), F32),
                        pltpu.VMEM((W, W), F32), pltpu.VMEM((1, cols), F32),
                        pltpu.VMEM((tb, W), F32), pltpu.VMEM((tb, W), F32), pltpu.VMEM((tb, W), F32),
                        pltpu.VMEM((n_chunk, RWKV_CHUNK, W), BF16), pltpu.VMEM((n_chunk, RWKV_CHUNK, W), F32),
                        pltpu.VMEM((n_chunk, W, W), BF16), pltpu.VMEM((n_chunk, W, W), F32),
                        pltpu.VMEM((tb, W), F32)],
        compiler_params=_cparams(("arbitrary", "arbitrary")),
        name="gla_rwkv7_mixers",
    )(proj, proj, proj, proj, gla_w2p, gla_gb, gla_ng, proj, *[params[n] for n in names])


def _merge_kernel(x_ref, sc_ref, sh_ref, gt_ref, gpre_ref, gpost_ref, wg_ref, b0_ref, b1_ref, b2_ref,
                  b3_ref, wb_ref, wo_ref, o_ref):
    x = x_ref[0]
    D = x.shape[1]
    h = _rms_mod(x, gpre_ref[...], sc_ref[0], sh_ref[0]).astype(BF16)
    merged = None
    for g, br in enumerate((b0_ref, b1_ref, b2_ref, b3_ref)):
        gate = _sigmoid(jnp.dot(h, wg_ref[:, g * D:(g + 1) * D], preferred_element_type=F32))
        t = gate * jnp.dot(br[0], wb_ref[g], preferred_element_type=F32)
        merged = t if merged is None else merged + t
    y = _mm(merged, wo_ref[...])
    y = y * lax.rsqrt(jnp.mean(y * y, axis=-1, keepdims=True) + RMS_EPS) * gpost_ref[...]
    o_ref[0] = x + gt_ref[0] * y


def _merge(x, sc, sh, gt, gpre, gpost, w_gate, branches, w_branch, w_out, tm):
    B, T, D = x.shape
    bw = branches[0].shape[2]
    tok = lambda n: pl.BlockSpec((1, tm, n), lambda b, i: (b, i, 0))
    vec = pl.BlockSpec((1, 1, D), lambda b, i: (b, 0, 0))
    const2 = lambda arr: pl.BlockSpec(arr.shape, lambda b, i: (0, 0))
    return pl.pallas_call(
        _merge_kernel,
        out_shape=jax.ShapeDtypeStruct((B, T, D), F32),
        grid=(B, T // tm),
        in_specs=[tok(D), vec, vec, vec, const2(gpre), const2(gpost), const2(w_gate),
                  tok(bw), tok(bw), tok(bw), tok(bw),
                  pl.BlockSpec(w_branch.shape, lambda b, i: (0, 0, 0)), const2(w_out)],
        out_specs=tok(D),
        compiler_params=_cparams(("arbitrary", "arbitrary")),
        name="merge_out_proj",
    )(x, sc, sh, gt, gpre, gpost, w_gate, *branches, w_branch, w_out)


def _ffn_kernel(x_ref, sc_ref, sh_ref, gt_ref, gpre_ref, gpost_ref, wg_ref, wu_ref, wd_ref, o_ref, *, tf):
    x = x_ref[0]
    h = _rms_mod(x, gpre_ref[...], sc_ref[0], sh_ref[0]).astype(BF16)
    F = wg_ref.shape[1]
    acc = None
    for f0 in range(0, F, tf):
        f1 = min(f0 + tf, F)
        gate = jnp.dot(h, wg_ref[:, f0:f1], preferred_element_type=F32)
        up = jnp.dot(h, wu_ref[:, f0:f1], preferred_element_type=F32)
        t = jnp.dot((_silu(gate) * up).astype(BF16), wd_ref[f0:f1, :], preferred_element_type=F32)
        acc = t if acc is None else acc + t
    y = acc * lax.rsqrt(jnp.mean(acc * acc, axis=-1, keepdims=True) + RMS_EPS) * gpost_ref[...]
    o_ref[0] = x + gt_ref[0] * y


def _dense_ffn(x, sc, sh, gt, gpre, gpost, wg, wu, wd, tm):
    B, T, D = x.shape
    tok = pl.BlockSpec((1, tm, D), lambda b, i: (b, i, 0))
    vec = pl.BlockSpec((1, 1, D), lambda b, i: (b, 0, 0))
    const2 = lambda arr: pl.BlockSpec(arr.shape, lambda b, i: (0, 0))
    return pl.pallas_call(
        functools.partial(_ffn_kernel, tf=FFN_CHUNK),
        out_shape=jax.ShapeDtypeStruct((B, T, D), F32),
        grid=(B, T // tm),
        in_specs=[tok, vec, vec, vec, const2(gpre), const2(gpost), const2(wg), const2(wu), const2(wd)],
        out_specs=tok,
        compiler_params=_cparams(("arbitrary", "arbitrary")),
        name="dense_swiglu",
    )(x, sc, sh, gt, gpre, gpost, wg, wu, wd)


MOE_TOKEN_TILE = 256
MOE_ROW_TILE = 512
SEG_ALIGN = 16
SEG_PIECES = (256, 128, 64, 32, 16)
MOE_SMALL_SEG = 128


def _route_kernel(x_ref, sc_ref, sh_ref, gpre_ref, rw_ref, rb_ref, h_ref, mi_ref, mp_ref, cnt_ref):
    tm = x_ref.shape[1]
    h = _rms_mod(x_ref[0], gpre_ref[...], sc_ref[0], sh_ref[0])
    h_ref[...] = h.astype(BF16)
    h_hi = h.astype(BF16)
    h_lo = (h - h_hi.astype(F32)).astype(BF16)
    w = rw_ref[...]
    w_hi = w.astype(BF16)
    w_lo = (w - w_hi.astype(F32)).astype(BF16)
    logits = jnp.dot(jnp.concatenate([h_hi, h_lo, h_hi], axis=1), jnp.concatenate([w_hi, w_hi, w_lo], axis=0),
                     preferred_element_type=F32) + rb_ref[...]
    lane = lax.broadcasted_iota(jnp.int32, logits.shape, 1)
    v1 = jnp.max(logits, axis=-1, keepdims=True)
    i1 = jnp.min(jnp.where(logits == v1, lane, LANES), axis=-1, keepdims=True)
    rest = jnp.where(lane == i1, -jnp.inf, logits)
    v2 = jnp.max(rest, axis=-1, keepdims=True)
    i2 = jnp.min(jnp.where(rest == v2, lane, LANES), axis=-1, keepdims=True)
    e2 = jnp.exp(v2 - v1)
    p1 = 1.0 / (1.0 + e2)
    p2 = e2 / (1.0 + e2)
    oh1 = (lane == i1).astype(F32)
    oh2 = (lane == i2).astype(F32)
    both = oh1 + oh2
    earlier = (lax.broadcasted_iota(jnp.int32, (tm, tm), 1)
               < lax.broadcasted_iota(jnp.int32, (tm, tm), 0)).astype(BF16)
    before = jnp.dot(earlier, both.astype(BF16), preferred_element_type=F32)
    r1 = jnp.sum(oh1 * before, axis=-1, keepdims=True).astype(jnp.int32)
    r2 = jnp.sum(oh2 * before, axis=-1, keepdims=True).astype(jnp.int32)
    col = lax.broadcasted_iota(jnp.int32, mi_ref.shape, 1)
    mi_ref[...] = jnp.where(col == 0, i1, jnp.where(col == 1, i2, jnp.where(col == 2, r1,
                                                                           jnp.where(col == 3, r2, 0))))
    mp_ref[...] = jnp.where(col == 0, p1, jnp.where(col == 1, p2, 0.0))
    cnt_ref[0] = jnp.sum(both, axis=0, keepdims=True).astype(jnp.int32)


def _segment_pieces(n_rows):
    out = []
    for s in SEG_PIECES:
        if s == MOE_TOKEN_TILE:
            out.append((n_rows == s, 0, s))
        else:
            out.append(((n_rows & s) != 0, pl.multiple_of((n_rows // (2 * s)) * (2 * s), SEG_ALIGN), s))
    return out


def _all_segments_small(cnt_ref, tile):
    most = cnt_ref[tile * N_EXPERTS]
    for e in range(1, N_EXPERTS):
        most = jnp.maximum(most, cnt_ref[tile * N_EXPERTS + e])
    return most <= MOE_SMALL_SEG


def _dispatch_kernel(seg_ref, cnt_ref, h_ref, mit_ref, init_ref, xs_ref, buf_ref, sem):
    del init_ref
    tm = h_ref.shape[0]
    i = pl.program_id(0)
    e1, e2 = mit_ref[0:1, :], mit_ref[1:2, :]
    r1, r2 = mit_ref[2:3, :], mit_ref[3:4, :]
    slot = i % 2

    def compact(cap):
        row = lax.broadcasted_iota(jnp.int32, (cap, tm), 0)
        select = jnp.concatenate(
            [jnp.logical_or(jnp.logical_and(e1 == e, r1 == row), jnp.logical_and(e2 == e, r2 == row))
             for e in range(N_EXPERTS)], axis=0).astype(BF16)
        rows = jnp.dot(select, h_ref[...], preferred_element_type=F32).astype(BF16)
        for e in range(N_EXPERTS):
            buf_ref[slot, e * tm:e * tm + cap, :] = rows[e * cap:(e + 1) * cap]

    small = _all_segments_small(cnt_ref, i)
    pl.when(small)(functools.partial(compact, MOE_SMALL_SEG))
    pl.when(jnp.logical_not(small))(functools.partial(compact, tm))

    def segment_copies(tile, buf, e):
        n = cnt_ref[tile * N_EXPERTS + e]
        n_rows = ((n + SEG_ALIGN - 1) // SEG_ALIGN) * SEG_ALIGN
        dst = pl.multiple_of(seg_ref[tile * N_EXPERTS + e], SEG_ALIGN)
        return [(cond, pltpu.make_async_copy(buf_ref.at[buf, pl.ds(e * tm + off, s), :],
                                             xs_ref.at[pl.ds(dst + off, s), :], sem))
                for cond, off, s in _segment_pieces(n_rows)]

    def for_all_segments(tile, buf, action):
        for e in range(N_EXPERTS):
            for cond, cp in segment_copies(tile, buf, e):
                pl.when(cond)(getattr(cp, action))

    @pl.when(i > 0)
    def _():
        for_all_segments(i - 1, 1 - slot, "wait")

    for_all_segments(i, slot, "start")

    @pl.when(i == pl.num_programs(0) - 1)
    def _():
        for_all_segments(i, slot, "wait")


def _expert_kernel(te_ref, nv_ref, xs_ref, wg_ref, wu_ref, wd_ref, ys_ref, acc_ref):
    del te_ref
    r = pl.program_id(0)
    f = pl.program_id(1)
    valid = r < nv_ref[0]

    @pl.when(jnp.logical_and(valid, f == 0))
    def _():
        acc_ref[...] = jnp.zeros_like(acc_ref)

    @pl.when(valid)
    def _():
        x = xs_ref[...]
        tf = wg_ref.shape[2]
        acc = acc_ref[...]
        for f0 in range(0, tf, FFN_CHUNK):
            f1 = min(f0 + FFN_CHUNK, tf)
            gate = jnp.dot(x, wg_ref[0, :, f0:f1], preferred_element_type=F32)
            up = jnp.dot(x, wu_ref[0, :, f0:f1], preferred_element_type=F32)
            acc = acc + jnp.dot((_silu(gate) * up).astype(BF16), wd_ref[0, f0:f1, :],
                                preferred_element_type=F32)
        acc_ref[...] = acc

    @pl.when(f == pl.num_programs(1) - 1)
    def _():
        @pl.when(valid)
        def _():
            ys_ref[...] = acc_ref[...].astype(ys_ref.dtype)

        @pl.when(jnp.logical_not(valid))
        def _():
            ys_ref[...] = jnp.zeros_like(ys_ref)


def _combine_kernel(seg_ref, cnt_ref, ys_ref, mi_ref, mp_ref, x_ref, gt_ref, gpost_ref, o_ref, win_ref,
                    pair_ref, sem):
    tm = x_ref.shape[1]
    i = pl.program_id(0) * pl.num_programs(1) + pl.program_id(1)
    n_tiles = pl.num_programs(0) * pl.num_programs(1)
    slot = i % 2

    def window_copy(tile, buf, e, rows):
        src = pl.multiple_of(seg_ref[tile * N_EXPERTS + e], SEG_ALIGN)
        return pltpu.make_async_copy(ys_ref.at[pl.ds(src, rows), :], win_ref.at[buf, pl.ds(e * rows, rows), :],
                                     sem.at[buf, e])

    def for_all_windows(tile, buf, action):
        small = _all_segments_small(cnt_ref, tile)
        for rows, cond in ((MOE_SMALL_SEG, small), (tm, jnp.logical_not(small))):
            @pl.when(cond)
            def _():
                for e in range(N_EXPERTS):
                    getattr(window_copy(tile, buf, e, rows), action)()

    @pl.when(i == 0)
    def _():
        for_all_windows(i, slot, "start")

    @pl.when(i + 1 < n_tiles)
    def _():
        for_all_windows(i + 1, 1 - slot, "start")

    for_all_windows(i, slot, "wait")
    e1, e2 = mi_ref[:, 0:1], mi_ref[:, 1:2]
    r1, r2 = mi_ref[:, 2:3], mi_ref[:, 3:4]

    def expand_rows(cap):
        col = lax.broadcasted_iota(jnp.int32, (tm, cap), 1)
        expand = jnp.concatenate(
            [jnp.concatenate([jnp.logical_and(e1 == e, r1 == col), jnp.logical_and(e2 == e, r2 == col)], axis=0)
             for e in range(N_EXPERTS)], axis=1).astype(BF16)
        pair_ref[...] = jnp.dot(expand, win_ref[slot, 0:N_EXPERTS * cap, :], preferred_element_type=F32)

    small = _all_segments_small(cnt_ref, i)
    pl.when(small)(functools.partial(expand_rows, MOE_SMALL_SEG))
    pl.when(jnp.logical_not(small))(functools.partial(expand_rows, tm))
    y = mp_ref[:, 0:1] * pair_ref[0:tm, :] + mp_ref[:, 1:2] * pair_ref[tm:2 * tm, :]
    y = y * lax.rsqrt(jnp.mean(y * y, axis=-1, keepdims=True) + RMS_EPS) * gpost_ref[...]
    o_ref[0] = x_ref[0] + gt_ref[0] * y


def _moe_ffn(x, sc, sh, gt, gpre, gpost, rw, rb, wg, wu, wd, tf):
    B, T, D = x.shape
    E, _, F = wg.shape
    tm = min(MOE_TOKEN_TILE, T)
    assert tm == MOE_TOKEN_TILE and E == N_EXPERTS
    nT = T // tm
    n_tok_tiles = B * nT
    N = B * T
    max_rows = 2 * N + n_tok_tiles * E * (SEG_ALIGN - 1) + E * (MOE_ROW_TILE - SEG_ALIGN)
    n_row_tiles = -(-max_rows // MOE_ROW_TILE) + 1
    P = n_row_tiles * MOE_ROW_TILE

    vec = pl.BlockSpec((1, 1, D), lambda b, j: (b, 0, 0))
    const2 = lambda arr: pl.BlockSpec(arr.shape, lambda b, j: (0, 0))
    flat = lambda n: pl.BlockSpec((tm, n), lambda b, j: (b * nT + j, 0))
    h, mi, mp, cnt = pl.pallas_call(
        _route_kernel,
        out_shape=(jax.ShapeDtypeStruct((N, D), BF16), jax.ShapeDtypeStruct((N, 8), jnp.int32),
                   jax.ShapeDtypeStruct((N, 8), F32), jax.ShapeDtypeStruct((n_tok_tiles, 1, LANES), jnp.int32)),
        grid=(B, nT),
        in_specs=[pl.BlockSpec((1, tm, D), lambda b, j: (b, j, 0)), vec, vec, const2(gpre), const2(rw),
                  const2(rb)],
        out_specs=(flat(D), flat(8), flat(8), pl.BlockSpec((1, 1, LANES), lambda b, j: (b * nT + j, 0, 0))),
        compiler_params=_cparams(("arbitrary", "arbitrary")),
        name="moe_route",
    )(x, sc, sh, gpre, rw, rb)

    counts = cnt[:, 0, :E]
    seg_len = (counts + SEG_ALIGN - 1) // SEG_ALIGN * SEG_ALIGN
    group_len = (jnp.sum(seg_len, axis=0) + MOE_ROW_TILE - 1) // MOE_ROW_TILE * MOE_ROW_TILE
    group_end = jnp.cumsum(group_len)
    seg_start = (group_end - group_len)[None, :] + jnp.cumsum(seg_len, axis=0) - seg_len
    seg_start = seg_start.reshape(-1).astype(jnp.int32)
    counts = counts.reshape(-1)
    n_valid = (group_end[-1:] // MOE_ROW_TILE).astype(jnp.int32)
    tile_first_row = jnp.arange(n_row_tiles, dtype=jnp.int32) * MOE_ROW_TILE
    tile_expert = jnp.minimum(jnp.sum(tile_first_row[:, None] >= group_end[None, :], axis=1), E - 1)
    tile_expert = tile_expert.astype(jnp.int32)

    xs = pl.pallas_call(
        _dispatch_kernel,
        out_shape=jax.ShapeDtypeStruct((P, D), BF16),
        grid_spec=pltpu.PrefetchScalarGridSpec(
            num_scalar_prefetch=2,
            grid=(n_tok_tiles,),
            in_specs=[pl.BlockSpec((tm, D), lambda i, seg, n: (i, 0)),
                      pl.BlockSpec((8, tm), lambda i, seg, n: (0, i)),
                      pl.BlockSpec(memory_space=pl.ANY)],
            out_specs=pl.BlockSpec(memory_space=pl.ANY),
            scratch_shapes=[pltpu.VMEM((2, E * tm, D), BF16), pltpu.SemaphoreType.DMA(())]),
        input_output_aliases={4: 0},
        compiler_params=_cparams(("arbitrary",)),
        name="moe_dispatch",
    )(seg_start, counts, h, mi.T, jnp.zeros((P, D), BF16))

    nf = F // tf
    live = lambda r, f, nv: jnp.where(r < nv[0], f, nf - 1)
    ys = pl.pallas_call(
        _expert_kernel,
        out_shape=jax.ShapeDtypeStruct((P, D), BF16),
        grid_spec=pltpu.PrefetchScalarGridSpec(
            num_scalar_prefetch=2,
            grid=(n_row_tiles, nf),
            in_specs=[pl.BlockSpec((MOE_ROW_TILE, D), lambda r, f, te, nv: (r, 0)),
                      pl.BlockSpec((1, D, tf), lambda r, f, te, nv: (te[r], 0, live(r, f, nv))),
                      pl.BlockSpec((1, D, tf), lambda r, f, te, nv: (te[r], 0, live(r, f, nv))),
                      pl.BlockSpec((1, tf, D), lambda r, f, te, nv: (te[r], live(r, f, nv), 0))],
            out_specs=pl.BlockSpec((MOE_ROW_TILE, D), lambda r, f, te, nv: (r, 0)),
            scratch_shapes=[pltpu.VMEM((MOE_ROW_TILE, D), F32)]),
        compiler_params=_cparams(("arbitrary", "arbitrary")),
        name="moe_experts",
    )(tile_expert, n_valid, xs, wg, wu, wd)

    return pl.pallas_call(
        _combine_kernel,
        out_shape=jax.ShapeDtypeStruct((B, T, D), F32),
        grid_spec=pltpu.PrefetchScalarGridSpec(
            num_scalar_prefetch=2,
            grid=(B, nT),
            in_specs=[pl.BlockSpec(memory_space=pl.ANY),
                      pl.BlockSpec((tm, 8), lambda b, j, seg, n: (b * nT + j, 0)),
                      pl.BlockSpec((tm, 8), lambda b, j, seg, n: (b * nT + j, 0)),
                      pl.BlockSpec((1, tm, D), lambda b, j, seg, n: (b, j, 0)),
                      pl.BlockSpec((1, 1, D), lambda b, j, seg, n: (b, 0, 0)),
                      pl.BlockSpec(gpost.shape, lambda b, j, seg, n: (0, 0))],
            out_specs=pl.BlockSpec((1, tm, D), lambda b, j, seg, n: (b, j, 0)),
            scratch_shapes=[pltpu.VMEM((2, E * tm, D), BF16), pltpu.VMEM((2 * tm, D), F32),
                            pltpu.SemaphoreType.DMA((2, E))]),
        compiler_params=_cparams(("arbitrary", "arbitrary")),
        name="moe_combine",
    )(seg_start, counts, ys, mi, mp, x, gt, gpost)


def _rope_tables(positions, groups):
    d = 32
    inv = 1.0 / (ROPE_THETA ** (jnp.arange(0, d, 2, dtype=F32) / d))
    ang = positions.astype(F32)[..., None] * inv
    cos, sin = jnp.cos(ang), jnp.sin(ang)
    cos = jnp.tile(jnp.concatenate([cos, cos], axis=-1), (1, 1, groups))
    sin = jnp.tile(jnp.concatenate([-sin, sin], axis=-1), (1, 1, groups))
    return cos, sin


def _pad_rows(w, rows, offset):
    out = jnp.zeros((rows, w.shape[1]), w.dtype)
    return out.at[offset:offset + w.shape[0]].set(w)


def kernel(x, c, positions, ada_w, ada_b, norm_mix_pre, norm_mix_post, norm_ffn_pre, norm_ffn_post, w_in, gla_gate_w2, gla_gate_b, gla_norm, diff_lambda, diff_subln, conv_w, conv_b, conv_ln_g, conv_ln_b, rwkv_mu, rwkv_w0, rwkv_w2, rwkv_a0, rwkv_a2, rwkv_g2, rwkv_k_k, rwkv_k_a, rwkv_r_k, rwkv_ln_g, rwkv_ln_b, w_branch, w_out, ffn_w_gate, ffn_w_up, ffn_w_down, router_w, router_b, moe_w_gate, moe_w_up, moe_w_down):
    B, T, D = x.shape
    L = ada_w.shape[0]
    W = D // N_BRANCH
    hk = gla_gate_b.shape[1]
    decay_rank = rwkv_w2.shape[1]
    a_rank = rwkv_a2.shape[1]
    gate_rank = rwkv_g2.shape[1]
    assert decay_rank + a_rank == LANES and 2 * hk == W and gate_rank == LANES
    n_mix = (3 * W + decay_rank + a_rank + gate_rank) + 3 * W + 3 * W
    sizes = (hk, hk, W, W, GLA_GATE_RANK, W, W, W, W, W, 3 * W + LANES + gate_rank, N_BRANCH * D)
    offs = [0]
    for s in sizes:
        offs.append(offs[-1] + s)
    assert offs[-1] == w_in.shape[2]
    tm = min(TOKEN_TILE, T)
    tb = min(SEQ_BLOCK, T)

    mod = _modulation(c, ada_w, ada_b)
    cos, sin = _rope_tables(positions, LANES // 32)

    for l in range(L):
        m = mod[l].reshape(B, 1, 6 * D)
        sh_m, sc_m, gt_m, sh_f, sc_f, gt_f = [m[:, :, i * D:(i + 1) * D] for i in range(6)]

        wl = w_in[l]
        gz_cols = jnp.zeros((D, LANES), F32).at[:, :GLA_GATE_RANK].set(wl[:, offs[4]:offs[5]])
        w_mix = jnp.concatenate([wl[:, offs[10]:offs[11]], wl[:, offs[0]:offs[4]], wl[:, offs[5]:offs[8]],
                                 gz_cols], axis=1).astype(BF16)
        w_conv = wl[:, offs[8]:offs[10]].astype(BF16)
        w_gate = wl[:, offs[11]:offs[12]].astype(BF16)
        proj, o_conv = _in_projection_conv(x, sc_m, sh_m, norm_mix_pre[l][None], w_conv, w_mix,
                                           _pad_rows(conv_w[l], 32, 0), conv_b[l][None],
                                           conv_ln_g[l][None], conv_ln_b[l][None], tm)

        o_diff = _diff_attention(proj, cos, sin, diff_lambda[l],
                                 diff_subln[l][None], col_q=7, col_k=8, col_v=9, layer_idx=l)
        vecs = dict(mu=rwkv_mu[l][None], w0=rwkv_w0[l][None], a0=rwkv_a0[l][None],
                    kk=rwkv_k_k[l][None], ka=rwkv_k_a[l][None], rk=rwkv_r_k[l].reshape(1, W),
                    lng=rwkv_ln_g[l][None], lnb=rwkv_ln_b[l][None])
        o_gla, o_rwkv = _gla_rwkv7(
            proj, _pad_rows(gla_gate_w2[l], LANES, 0), gla_gate_b[l][None],
            jnp.tile(gla_norm[l], GLA_HEADS)[None], 4, 5, 6, n_mix // LANES,
            vecs, _pad_rows(rwkv_w2[l], LANES, 0).astype(BF16),
            _pad_rows(rwkv_a2[l], LANES, decay_rank).astype(BF16), rwkv_g2[l].astype(BF16), 0, tb)
        x = _merge(x, sc_m, sh_m, gt_m, norm_mix_pre[l][None], norm_mix_post[l][None], w_gate,
                   (o_gla, o_diff, o_conv, o_rwkv), w_branch[l].astype(BF16), w_out[l].astype(BF16), tm)

        i = l // 2
        if l % 2 == 0:
            x = _dense_ffn(x, sc_f, sh_f, gt_f, norm_ffn_pre[l][None], norm_ffn_post[l][None],
                           ffn_w_gate[i].astype(BF16), ffn_w_up[i].astype(BF16),
                           ffn_w_down[i].astype(BF16), tm)
        else:
            rw = jnp.zeros((D, LANES), F32).at[:, :N_EXPERTS].set(router_w[i])
            rb = jnp.full((1, LANES), -jnp.inf, F32).at[0, :N_EXPERTS].set(router_b[i])
            x = _moe_ffn(x, sc_f, sh_f, gt_f, norm_ffn_pre[l][None], norm_ffn_post[l][None], rw, rb,
                         moe_w_gate[i].astype(BF16), moe_w_up[i].astype(BF16),
                         moe_w_down[i].astype(BF16), tf=moe_w_gate.shape[3])
    return x
```

```python
import functools
import math

import jax
import jax.numpy as jnp
from jax import lax
from jax.experimental import pallas as pl
from jax.experimental.pallas import tpu as pltpu

F32 = jnp.float32
BF16 = jnp.bfloat16
HIGHEST = lax.Precision.HIGHEST
LOG2_E = math.log2(math.e)

N_BRANCH = 4
GLA_HEADS = 4
GLA_GATE_RANK = 16
GLA_GATE_NORMALIZER = 16.0
GLA_CHUNK = 32
DIFF_HEADS = 4
ROPE_THETA = 10000.0
CONV_WIDTH = 31
RWKV_HEADS = 4
RWKV_CHUNK = 64
RWKV_SUB = 16
N_EXPERTS = 8
RMS_EPS = 1e-6
LN_EPS = 1e-5
RWKV_GN_EPS = 64e-5
LANES = 128
SUBLANES = 8
VMEM_LIMIT = 56 * 1024 * 1024

TOKEN_TILE = 512
SEQ_BLOCK = 1024
ATTN_BLOCK = 512
CONV_ROWS = 128
FFN_CHUNK = 512


def _cparams(sem):
    return pltpu.CompilerParams(dimension_semantics=sem, vmem_limit_bytes=VMEM_LIMIT)


def _mm(a, b):
    return jnp.dot(a.astype(BF16), b.astype(BF16), preferred_element_type=F32)


def _bmm(a, b):
    return lax.dot_general(a.astype(BF16), b.astype(BF16), (((2,), (1,)), ((0,), (0,))),
                           preferred_element_type=F32)


def _mm_f32(a, b):
    return jnp.dot(a, b, precision=HIGHEST, preferred_element_type=F32)


def _hi_lo(x):
    hi = x.astype(BF16)
    return jnp.concatenate([hi, (x - hi.astype(F32)).astype(BF16)], axis=1)


def _sigmoid(x):
    return 0.5 * jnp.tanh(0.5 * x) + 0.5


def _silu(x):
    return x * _sigmoid(x)


def _softplus(x):
    return jnp.maximum(x, 0.0) + jnp.log(1.0 + jnp.exp(-jnp.abs(x)))


def _group_matrix(n, group):
    r = lax.broadcasted_iota(jnp.int32, (n, n), 0) // group
    c = lax.broadcasted_iota(jnp.int32, (n, n), 1) // group
    return r == c


def _rms_mod(x, gain, scale, shift):
    y = x * lax.rsqrt(jnp.mean(x * x, axis=-1, keepdims=True) + RMS_EPS)
    return y * gain * (1.0 + scale) + shift


def _mod_kernel(c_ref, w_ref, b_ref, o_ref):
    o_ref[0] = _mm_f32(_silu(c_ref[...]), w_ref[0]) + b_ref[0]


def _modulation(c, ada_w, ada_b):
    L, D, M = ada_w.shape
    B = c.shape[0]
    tn = M // 4
    return pl.pallas_call(
        _mod_kernel,
        out_shape=jax.ShapeDtypeStruct((L, B, M), F32),
        grid=(L, M // tn),
        in_specs=[pl.BlockSpec((B, D), lambda l, j: (0, 0)),
                  pl.BlockSpec((1, D, tn), lambda l, j: (l, 0, j)),
                  pl.BlockSpec((1, 1, tn), lambda l, j: (l, 0, j))],
        out_specs=pl.BlockSpec((1, B, tn), lambda l, j: (l, 0, j)),
        compiler_params=_cparams(("arbitrary", "arbitrary")),
        name="adaln_mod",
    )(c, ada_w, ada_b.reshape(L, 1, M))


CONV_HALO = 32


def _inproj_conv_kernel(x_ref, sc_ref, sh_ref, g_ref, wab_ref, w_ref, cw_ref, cb_ref, lg_ref, lb_ref,
                        o_ref, oc_ref, u_ref, *, rb):
    tm = x_ref.shape[1]
    ch = oc_ref.shape[2]
    h = _rms_mod(x_ref[0], g_ref[...], sc_ref[0], sh_ref[0]).astype(BF16)

    @pl.when(pl.program_id(1) == 0)
    def _():
        u_ref[0:CONV_HALO, :] = jnp.zeros((CONV_HALO, ch), F32)

    ab = jnp.dot(h, wab_ref[...], preferred_element_type=F32)
    u_ref[CONV_HALO:CONV_HALO + tm, :] = ab[:, 0:ch] * _sigmoid(ab[:, ch:2 * ch])
    o_ref[0] = jnp.dot(h, w_ref[...], preferred_element_type=F32)

    first = CONV_HALO - (CONV_WIDTH - 1)
    for i in range(tm // rb):
        r0 = i * rb
        win = u_ref[r0:r0 + rb + CONV_HALO, :]
        acc = jnp.zeros((rb, ch), F32)
        for s in range(SUBLANES):
            taps = [j for j in range(CONV_WIDTH) if (first + j) % SUBLANES == s]
            rolled = win if s == 0 else pltpu.roll(win, rb + CONV_HALO - s, 0)
            for j in taps:
                a0 = first + j - s
                acc = acc + cw_ref[j:j + 1, :] * rolled[a0:a0 + rb, :]
        y = acc + cb_ref[...]
        mu = jnp.mean(y, axis=-1, keepdims=True)
        yc = y - mu
        var = jnp.mean(yc * yc, axis=-1, keepdims=True)
        y = yc * lax.rsqrt(var + LN_EPS) * lg_ref[...] + lb_ref[...]
        oc_ref[0, r0:r0 + rb, :] = _silu(y).astype(oc_ref.dtype)
    u_ref[0:CONV_HALO, :] = u_ref[tm:tm + CONV_HALO, :]


def _in_projection_conv(x, sc, sh, gain, w_ab, w, conv_w, conv_b, ln_g, ln_b, tm):
    B, T, D = x.shape
    n = w.shape[1]
    ch = conv_w.shape[1]
    const = lambda arr: pl.BlockSpec(arr.shape, lambda b, i: (0, 0))
    return pl.pallas_call(
        functools.partial(_inproj_conv_kernel, rb=min(CONV_ROWS, tm)),
        out_shape=(jax.ShapeDtypeStruct((B, T, n), F32), jax.ShapeDtypeStruct((B, T, ch), BF16)),
        grid=(B, T // tm),
        in_specs=[pl.BlockSpec((1, tm, D), lambda b, i: (b, i, 0)),
                  pl.BlockSpec((1, 1, D), lambda b, i: (b, 0, 0)),
                  pl.BlockSpec((1, 1, D), lambda b, i: (b, 0, 0)),
                  const(gain), const(w_ab), const(w), const(conv_w), const(conv_b), const(ln_g), const(ln_b)],
        out_specs=(pl.BlockSpec((1, tm, n), lambda b, i: (b, i, 0)),
                   pl.BlockSpec((1, tm, ch), lambda b, i: (b, i, 0))),
        scratch_shapes=[pltpu.VMEM((tm + CONV_HALO, ch), F32)],
        compiler_params=_cparams(("arbitrary", "arbitrary")),
        name="in_proj_conv",
    )(x, sc, sh, gain, w_ab, w, conv_w, conv_b, ln_g, ln_b)


def _gla_kernel(qk_ref, v_ref, og_ref, gz_ref, w2_ref, gb_ref, ng_ref, o_ref, s_ref, g_ref, r_ref,
                qd_ref, kd_ref, oacc_ref, *, n_chunk):
    C = GLA_CHUNK
    hk = qk_ref.shape[2] // 2
    hv = v_ref.shape[2]
    dk = hk // GLA_HEADS
    dv = hv // GLA_HEADS

    tb = n_chunk * C

    @pl.when(pl.program_id(1) == 0)
    def _():
        s_ref[...] = jnp.zeros_like(s_ref)

    z = _mm(gz_ref[0], w2_ref[...]) + gb_ref[...]
    gk = (jnp.minimum(z, 0.0) - jnp.log(1.0 + jnp.exp(-jnp.abs(z)))) * (LOG2_E / GLA_GATE_NORMALIZER)
    tri = (lax.broadcasted_iota(jnp.int32, (C, C), 1)
           <= lax.broadcasted_iota(jnp.int32, (C, C), 0)).astype(BF16)
    sums = _bmm(jnp.broadcast_to(tri[None], (n_chunk, C, C)), _hi_lo(gk).reshape(n_chunk, C, 2 * hk))
    G3 = sums[:, :, 0:hk] + sums[:, :, hk:2 * hk]
    G_all = G3.reshape(tb, hk)
    R_all = (jnp.broadcast_to(G3[:, C - 1:C, :], (n_chunk, C, hk)) - G3).reshape(tb, hk)
    g_ref[...] = G_all
    r_ref[...] = R_all
    qd_ref[...] = (qk_ref[0, :, 0:hk] * (dk ** -0.5) * jnp.exp2(G_all)).astype(BF16)
    kd_ref[...] = (qk_ref[0, :, hk:2 * hk] * jnp.exp2(R_all)).astype(BF16)

    half = C // 2

    def key_not_after_query(nj, ni):
        return (lax.broadcasted_iota(jnp.int32, (nj, ni, hk), 0)
                <= lax.broadcasted_iota(jnp.int32, (nj, ni, hk), 1))

    causal_first, causal_second = key_not_after_query(half, C), key_not_after_query(half, half)
    er = lax.broadcasted_iota(jnp.int32, (hk, hv), 0) // dk
    ec = lax.broadcasted_iota(jnp.int32, (hk, hv), 1) // dv
    expand = (er == ec).astype(BF16)
    sr = lax.broadcasted_iota(jnp.int32, (hv, hk), 0) // dv
    scol = lax.broadcasted_iota(jnp.int32, (hv, hk), 1) // dk
    state_mask = sr == scol
    head_mean = (_group_matrix(hv, dv).astype(F32) / dv).astype(BF16)

    def chunk(ci, carry):
        r0 = pl.multiple_of(ci * C, C)
        rows = pl.ds(r0, C)
        q = qk_ref[0, rows, 0:hk] * (dk ** -0.5)
        k = qk_ref[0, rows, hk:2 * hk]
        v = v_ref[0, rows, :]
        G = g_ref[rows, :]
        def keys_to_queries(j0, i0, keep):
            nj, ni = keep.shape[0], keep.shape[1]
            pair = (nj, ni, hk)
            g_diff = (jnp.broadcast_to(G[i0:i0 + ni][None], pair)
                      - jnp.broadcast_to(G[j0:j0 + nj][:, None, :], pair))
            decay = jnp.exp2(jnp.where(keep, g_diff, -jnp.inf))
            p = (jnp.broadcast_to(q[i0:i0 + ni][None], pair)
                 * jnp.broadcast_to(k[j0:j0 + nj][:, None, :], pair) * decay)
            a_exp = jnp.dot(p.reshape(nj * ni, hk).astype(BF16), expand, preferred_element_type=F32)
            v_j = jnp.broadcast_to(v[j0:j0 + nj][:, None, :], (nj, ni, hv))
            return jnp.sum(a_exp.reshape(nj, ni, hv) * v_j, axis=0)

        o_second = keys_to_queries(half, half, causal_second)
        o_intra = keys_to_queries(0, 0, causal_first) + jnp.concatenate([jnp.zeros_like(o_second), o_second],
                                                                        axis=0)
        g_total = G[0:1, :] + r_ref[pl.ds(r0, 8), :][0:1, :]
        s = s_ref[...]
        o_inter = lax.dot_general(qd_ref[rows, :], s.astype(BF16), (((1,), (1,)), ((), ())),
                                  preferred_element_type=F32)
        kv = lax.dot_general(v.astype(BF16), kd_ref[rows, :], (((0,), (0,)), ((), ())),
                             preferred_element_type=F32)
        s_ref[...] = s * jnp.exp2(g_total) + jnp.where(state_mask, kv, 0.0)
        oacc_ref[rows, :] = o_intra + o_inter
        return carry

    lax.fori_loop(0, n_chunk, chunk, 0, unroll=True)

    o = oacc_ref[...]
    ms = jnp.dot(_hi_lo(o * o), jnp.concatenate([head_mean, head_mean], axis=0),
                 preferred_element_type=F32)
    o = o * lax.rsqrt(ms + RMS_EPS) * ng_ref[...] * _silu(og_ref[0])
    o_ref[0] = o.astype(o_ref.dtype)


def _rope(t, cos, sin_signed):
    d = 32
    half = d // 2
    out = []
    for s in range(t.shape[1] // LANES):
        x = t[:, s * LANES:(s + 1) * LANES]
        lane = lax.broadcasted_iota(jnp.int32, x.shape, 1)
        up = pltpu.roll(x, LANES - half, 1)
        down = pltpu.roll(x, half, 1)
        rot = jnp.where((lane % d) < half, up, down)
        out.append(x * cos + rot * sin_signed)
    return jnp.concatenate(out, axis=1)


def _diff_kernel(q_ref, k_ref, v_ref, cosq_ref, sinq_ref, cosk_ref, sink_ref, lam_ref, g_ref, o_ref,
                 ks, vs, *, tq, lam_init):
    H = DIFF_HEADS
    d = q_ref.shape[2] // (2 * H)
    dv = v_ref.shape[2] // H
    j = pl.program_id(1)

    @pl.when(j == 0)
    def _():
        k = _rope(k_ref[0], cosk_ref[0], sink_ref[0])
        v = v_ref[0]
        for hc in range(2 * H):
            ks[hc] = k[:, hc * d:(hc + 1) * d].astype(BF16)
        ones_col = (lax.broadcasted_iota(jnp.int32, (v.shape[0], dv), 1) == 0).astype(BF16)
        for h in range(H):
            vs[h] = jnp.concatenate([v[:, h * dv:(h + 1) * dv].astype(BF16), ones_col], axis=1)

    q = _rope(q_ref[0], cosq_ref[0], sinq_ref[0]) * (d ** -0.5 * LOG2_E)
    lp = lam_ref[...]
    lam = (jnp.exp(jnp.sum(lp[0:1] * lp[1:2], axis=-1, keepdims=True))
           - jnp.exp(jnp.sum(lp[2:3] * lp[3:4], axis=-1, keepdims=True)) + lam_init)
    on_or_below_diag = (lax.broadcasted_iota(jnp.int32, (tq, tq), 1)
                        <= lax.broadcasted_iota(jnp.int32, (tq, tq), 0))

    def update(qh, kh, vh, m, acc, masked):
        s = lax.dot_general(qh, kh, (((1,), (1,)), ((), ())), preferred_element_type=F32)
        if masked:
            s = jnp.where(on_or_below_diag, s, -jnp.inf)
        m_new = jnp.maximum(m, jnp.max(s, axis=-1, keepdims=True))
        p = jnp.exp2(s - m_new).astype(BF16)
        acc = jnp.exp2(m - m_new) * acc + jnp.dot(p, vh, preferred_element_type=F32)
        return m_new, acc

    qs = [q[:, hc * d:(hc + 1) * d].astype(BF16) for hc in range(2 * H)]

    def kv_block(kb, carry, masked):
        rows = pl.ds(pl.multiple_of(kb * tq, tq), tq)
        out = []
        for hc in range(2 * H):
            m, acc = carry[2 * hc], carry[2 * hc + 1]
            out.extend(update(qs[hc], ks[hc, rows, :], vs[hc // 2, rows, :], m, acc, masked))
        return tuple(out)

    m0 = jnp.full((tq, 1), -jnp.inf, F32)
    a0 = jnp.zeros((tq, 2 * dv), F32)
    carry = lax.fori_loop(0, j, lambda kb, c: kv_block(kb, c, False), (m0, a0) * (2 * H))
    carry = kv_block(j, carry, True)

    for h in range(H):
        a1, a2 = carry[4 * h + 1], carry[4 * h + 3]
        comp = [a[:, 0:dv] / a[:, dv:dv + 1] for a in (a1, a2)]
        o = comp[0] - lam * comp[1]
        o = o * lax.rsqrt(jnp.mean(o * o, axis=-1, keepdims=True) + RMS_EPS)
        o = o * g_ref[...] * (1.0 - lam_init)
        o_ref[0, :, h * dv:(h + 1) * dv] = o.astype(o_ref.dtype)


def _diff_attention(proj, cos, sin, lam_p, g, col_q, col_k, col_v, layer_idx):
    B, T, _ = proj.shape
    H = DIFF_HEADS
    dv = g.shape[1]
    d = dv // 2
    w = 2 * H * d
    assert cos.shape[2] == LANES
    tq = min(ATTN_BLOCK, T)
    lam_init = 0.8 - 0.6 * math.exp(-0.3 * layer_idx)
    blk = lambda col: pl.BlockSpec((1, tq, w), lambda b, j: (b, j, col))
    full = lambda col: pl.BlockSpec((1, T, w), lambda b, j: (b, 0, col))
    tab_blk = pl.BlockSpec((1, tq, LANES), lambda b, j: (b, j, 0))
    tab_full = pl.BlockSpec((1, T, LANES), lambda b, j: (b, 0, 0))
    return pl.pallas_call(
        functools.partial(_diff_kernel, tq=tq, lam_init=lam_init),
        out_shape=jax.ShapeDtypeStruct((B, T, H * dv), BF16),
        grid=(B, T // tq),
        in_specs=[blk(col_q), full(col_k), full(col_v), tab_blk, tab_blk, tab_full, tab_full,
                  pl.BlockSpec(lam_p.shape, lambda b, j: (0, 0)),
                  pl.BlockSpec(g.shape, lambda b, j: (0, 0))],
        out_specs=pl.BlockSpec((1, tq, H * dv), lambda b, j: (b, j, 0)),
        scratch_shapes=[pltpu.VMEM((2 * H, T, d), BF16),
                        pltpu.VMEM((H, T, 2 * dv), BF16)],
        compiler_params=_cparams(("arbitrary", "arbitrary")),
        name="diff_attention",
    )(proj, proj, proj, cos, sin, cos, sin, lam_p, g)


def _rwkv_kernel(x_ref, mu_ref, w0_ref, w2_ref, a0_ref, a2_ref, g2_ref, kk_ref, ka_ref, rk_ref,
                 lng_ref, lnb_ref, o_ref, s_ref, prev_ref, gate_ref, bonus_ref, dec_ref, qeff_ref, yloc_ref,
                 mlow_ref, nc_ref, y_ref, *, n_chunk):
    C = RWKV_CHUNK
    H = RWKV_HEADS
    W = o_ref.shape[2]
    N = W // H

    @pl.when(pl.program_id(1) == 0)
    def _():
        s_ref[...] = jnp.zeros_like(s_ref)
        prev_ref[...] = jnp.zeros_like(prev_ref)

    lane = lax.broadcasted_iota(jnp.int32, (1, W), 1)
    head_mask = [(lane // N == h).astype(F32) for h in range(H)]
    block_diag = _group_matrix(W, N)
    head_sum = block_diag.astype(F32)
    ti = lax.broadcasted_iota(jnp.int32, (C, C), 0)
    tj = lax.broadcasted_iota(jnp.int32, (C, C), 1)
    tril_incl = ti >= tj
    tril_strict = ti > tj
    same_sub = (ti // RWKV_SUB) == (tj // RWKV_SUB)
    eye = (ti == tj).astype(F32)
    nc = n_chunk
    tb = nc * C

    def head_total(t, two_term=False):
        ones = head_sum.astype(BF16)
        if two_term:
            return jnp.dot(_hi_lo(t), jnp.concatenate([ones, ones], axis=0), preferred_element_type=F32)
        return jnp.dot(t.astype(BF16), ones, preferred_element_type=F32)

    x = x_ref[0]
    first_row = lax.broadcasted_iota(jnp.int32, (tb, 1), 0) == 0
    prev = jnp.where(first_row, prev_ref[...], pltpu.roll(x, 1, 0))
    prev_ref[...] = x[tb - 1:tb, :]
    xm = x + (prev - x) * mu_ref[...]
    r = xm[:, 0:W]
    k = xm[:, W:2 * W]
    v = xm[:, 2 * W:3 * W]
    zz = xm[:, 3 * W:3 * W + LANES]
    zg = xm[:, 3 * W + LANES:]
    w = -_softplus(-(w0_ref[...] + _mm(jnp.tanh(zz), w2_ref[...]))) - 0.5
    lw = -jnp.exp(w) * LOG2_E
    a = _sigmoid(a0_ref[...] + _mm(zz, a2_ref[...]))
    gate_ref[...] = _mm(_sigmoid(zg), g2_ref[...])
    kk = k * kk_ref[...]
    kk = kk / jnp.maximum(jnp.sqrt(head_total(kk * kk, two_term=True)), 1e-12)
    k = k * (1.0 + (a - 1.0) * ka_ref[...])
    b = kk * a
    bonus_ref[...] = head_total(r * k * rk_ref[...]) * v

    c3 = lambda t: t.reshape(nc, C, t.shape[1])
    sums = _bmm(jnp.broadcast_to(tril_incl.astype(BF16)[None], (nc, C, C)), c3(_hi_lo(lw)))
    G3 = sums[:, :, 0:W] + sums[:, :, W:2 * W]
    total3 = jnp.broadcast_to(G3[:, C - 1:C, :], (nc, C, W))
    dec_ref[...] = jnp.exp2(total3).reshape(tb, W)
    G = G3.reshape(tb, W)
    inv = jnp.exp2(-G)
    to_end = jnp.exp2(total3 - G3).reshape(tb, W)
    kap = kk * jnp.exp2(G - lw)
    rho = r * jnp.exp2(G)
    kap3, rho3, v3 = c3(kap), c3(rho), c3(v)
    bet_kt3 = jnp.concatenate([c3(b * inv), c3(k * inv)], axis=1)
    betc3, kc3 = c3(b * to_end), c3(k * to_end)

    def bmm_nt(p, q):
        return lax.dot_general(p.astype(BF16), q.astype(BF16), (((2,), (2,)), ((0,), (0,))),
                               preferred_element_type=F32)

    def bmm_tn(p, q):
        return lax.dot_general(p.astype(BF16), q.astype(BF16), (((1,), (1,)), ((0,), (0,))),
                               preferred_element_type=F32)

    wi = lax.broadcasted_iota(jnp.int32, (C, 2 * C), 0)
    wj = lax.broadcasted_iota(jnp.int32, (C, 2 * C), 1) % C
    A_bk, B_bk = [], []
    kap_b, rho_b = kap.astype(BF16), rho.astype(BF16)
    for h in range(H):
        mask_b = head_mask[h].astype(BF16)
        lhs = jnp.concatenate([c3(kap_b * mask_b), c3(rho_b * mask_b)], axis=1)
        prod = bmm_nt(lhs, bet_kt3)
        A_bk.append(jnp.where(wj < wi, prod[:, 0:C], 0.0))
        B_bk.append(jnp.where(wj <= wi, prod[:, C:2 * C], 0.0))
    A_b = jnp.concatenate([t[:, :, 0:C] for t in A_bk], axis=0)
    B_b = [t[:, :, 0:C] for t in B_bk]
    v3_low = jnp.concatenate([jnp.zeros_like(v3), v3], axis=1)

    Dg = jnp.where(same_sub, A_b, 0.0)
    Lo = A_b - Dg
    D2 = _bmm(Dg, Dg)
    D4 = _bmm(D2, D2)
    D8 = _bmm(D4, D4)
    Dinv = _bmm(_bmm(_bmm(eye - Dg, eye + D2), eye + D4), eye + D8)
    Nn = _bmm(Dinv, Lo)
    N2 = _bmm(Nn, Nn)
    Tm = _bmm(_bmm(eye - Nn, eye + N2), Dinv)
    Tm = [Tm[h * nc:(h + 1) * nc] for h in range(H)]

    def per_head(mats, t):
        head_of_lane = (lax.broadcasted_iota(jnp.int32, (1, 1, t.shape[2]), 2) % W) // N
        acc = _bmm(mats[H - 1], t)
        for h in range(H - 1):
            acc = jnp.where(head_of_lane == h, _bmm(mats[h], t), acc)
        return acc

    akv = per_head(A_bk, v3_low)
    tk = per_head(Tm, jnp.concatenate([kap3, akv], axis=2))
    kap_p = tk[:, :, 0:W]
    v_p = tk[:, :, W:2 * W]
    bb = per_head(B_b, jnp.concatenate([kap_p, v_p], axis=2))
    qeff_ref[...] = (rho3 - bb[:, :, 0:W]).astype(BF16)
    yloc_ref[...] = per_head(B_bk, v3_low) - bb[:, :, W:2 * W]
    mlow_ref[...] = jnp.where(block_diag, bmm_tn(kap_p, betc3), 0.0).astype(BF16)
    nc_ref[...] = jnp.where(block_diag, bmm_tn(jnp.concatenate([v3, v_p], axis=1),
                                               jnp.concatenate([kc3, -betc3], axis=1)), 0.0)

    def chunk(ci, carry):
        r0 = pl.multiple_of(ci * C, C)
        s = s_ref[...]
        sb = s.astype(BF16)
        y = lax.dot_general(qeff_ref[ci], sb, (((1,), (1,)), ((), ())), preferred_element_type=F32)
        y_ref[pl.ds(r0, C), :] = y + yloc_ref[ci]
        s_ref[...] = (s * dec_ref[pl.ds(r0, 8), :][0:1, :]
                      - jnp.dot(sb, mlow_ref[ci], preferred_element_type=F32) + nc_ref[ci])
        return carry

    lax.fori_loop(0, nc, chunk, 0, unroll=True)

    y = y_ref[...]
    mean = head_total(y, two_term=True) / N
    yc = y - mean
    var = head_total(yc * yc) / N
    yn = yc * lax.rsqrt(var + RWKV_GN_EPS) * lng_ref[...] + lnb_ref[...]
    o_ref[0] = ((yn + bonus_ref[...]) * gate_ref[...]).astype(o_ref.dtype)


N_GLA_INPUTS = 7
N_GLA_SCRATCH = 6
N_RWKV_INPUTS = 12


def _gla_rwkv_kernel(*refs, gla_chunks, rwkv_chunks):
    n_in = N_GLA_INPUTS + N_RWKV_INPUTS
    gla_in, rwkv_in = refs[0:N_GLA_INPUTS], refs[N_GLA_INPUTS:n_in]
    gla_out, rwkv_out = refs[n_in], refs[n_in + 1]
    gla_scratch = refs[n_in + 2:n_in + 2 + N_GLA_SCRATCH]
    rwkv_scratch = refs[n_in + 2 + N_GLA_SCRATCH:]
    _gla_kernel(*gla_in, gla_out, *gla_scratch, n_chunk=gla_chunks)
    _rwkv_kernel(*rwkv_in, rwkv_out, *rwkv_scratch, n_chunk=rwkv_chunks)


def _gla_rwkv7(proj, gla_w2p, gla_gb, gla_ng, col_qk, col_v, col_og, col_gz, vecs, w2p, a2p, g2, col_rwkv, tb):
    B, T, _ = proj.shape
    hv = gla_ng.shape[1]
    hk = gla_gb.shape[1]
    W = g2.shape[1]
    cols = vecs["mu"].shape[1]
    names = ("mu", "w0", "w2", "a0", "a2", "g2", "kk", "ka", "rk", "lng", "lnb")
    params = dict(vecs, w2=w2p, a2=a2p, g2=g2)
    assert len(names) + 1 == N_RWKV_INPUTS
    const = lambda arr: pl.BlockSpec(arr.shape, lambda b, j: (0, 0))
    rows = lambda n, col: pl.BlockSpec((1, tb, n), lambda b, j: (b, j, col))
    n_chunk = tb // RWKV_CHUNK
    return pl.pallas_call(
        functools.partial(_gla_rwkv_kernel, gla_chunks=tb // GLA_CHUNK, rwkv_chunks=n_chunk),
        out_shape=(jax.ShapeDtypeStruct((B, T, hv), BF16), jax.ShapeDtypeStruct((B, T, W), BF16)),
        grid=(B, T // tb),
        in_specs=[rows(2 * hk, col_qk), rows(hv, col_v), rows(hv, col_og), rows(LANES, col_gz),
                  const(gla_w2p), const(gla_gb), const(gla_ng), rows(cols, col_rwkv)]
                 + [const(params[n]) for n in names],
        out_specs=(rows(hv, 0), rows(W, 0)),
        scratch_shapes=[pltpu.VMEM((hv, hk), F32), pltpu.VMEM((tb, hk), F32), pltpu.VMEM((tb, hk), F32),
                        pltpu.VMEM((tb, hk), BF16), pltpu.VMEM((tb, hk), BF16), pltpu.VMEM((tb, hv), F32),
                        pltpu.VMEM((W, W), F32), pltpu.VMEM((1, cols), F32),
                        pltpu.VMEM((tb, W), F32), pltpu.VMEM((tb, W), F32), pltpu.VMEM((tb, W), F32),
                        pltpu.VMEM((n_chunk, RWKV_CHUNK, W), BF16), pltpu.VMEM((n_chunk, RWKV_CHUNK, W), F32),
                        pltpu.VMEM((n_chunk, W, W), BF16), pltpu.VMEM((n_chunk, W, W), F32),
                        pltpu.VMEM((tb, W), F32)],
        compiler_params=_cparams(("arbitrary", "arbitrary")),
        name="gla_rwkv7_mixers",
    )(proj, proj, proj, proj, gla_w2p, gla_gb, gla_ng, proj, *[params[n] for n in names])


def _merge_kernel(x_ref, sc_ref, sh_ref, gt_ref, gpre_ref, gpost_ref, wg_ref, b0_ref, b1_ref, b2_ref,
                  b3_ref, wb_ref, wo_ref, o_ref):
    x = x_ref[0]
    D = x.shape[1]
    h = _rms_mod(x, gpre_ref[...], sc_ref[0], sh_ref[0]).astype(BF16)
    merged = None
    for g, br in enumerate((b0_ref, b1_ref, b2_ref, b3_ref)):
        gate = _sigmoid(jnp.dot(h, wg_ref[:, g * D:(g + 1) * D], preferred_element_type=F32))
        t = gate * jnp.dot(br[0], wb_ref[g], preferred_element_type=F32)
        merged = t if merged is None else merged + t
    y = _mm(merged, wo_ref[...])
    y = y * lax.rsqrt(jnp.mean(y * y, axis=-1, keepdims=True) + RMS_EPS) * gpost_ref[...]
    o_ref[0] = x + gt_ref[0] * y


def _merge(x, sc, sh, gt, gpre, gpost, w_gate, branches, w_branch, w_out, tm):
    B, T, D = x.shape
    bw = branches[0].shape[2]
    tok = lambda n: pl.BlockSpec((1, tm, n), lambda b, i: (b, i, 0))
    vec = pl.BlockSpec((1, 1, D), lambda b, i: (b, 0, 0))
    const2 = lambda arr: pl.BlockSpec(arr.shape, lambda b, i: (0, 0))
    return pl.pallas_call(
        _merge_kernel,
        out_shape=jax.ShapeDtypeStruct((B, T, D), F32),
        grid=(B, T // tm),
        in_specs=[tok(D), vec, vec, vec, const2(gpre), const2(gpost), const2(w_gate),
                  tok(bw), tok(bw), tok(bw), tok(bw),
                  pl.BlockSpec(w_branch.shape, lambda b, i: (0, 0, 0)), const2(w_out)],
        out_specs=tok(D),
        compiler_params=_cparams(("arbitrary", "arbitrary")),
        name="merge_out_proj",
    )(x, sc, sh, gt, gpre, gpost, w_gate, *branches, w_branch, w_out)


def _ffn_kernel(x_ref, sc_ref, sh_ref, gt_ref, gpre_ref, gpost_ref, wg_ref, wu_ref, wd_ref, o_ref, *, tf):
    x = x_ref[0]
    h = _rms_mod(x, gpre_ref[...], sc_ref[0], sh_ref[0]).astype(BF16)
    F = wg_ref.shape[1]
    acc = None
    for f0 in range(0, F, tf):
        f1 = min(f0 + tf, F)
        gate = jnp.dot(h, wg_ref[:, f0:f1], preferred_element_type=F32)
        up = jnp.dot(h, wu_ref[:, f0:f1], preferred_element_type=F32)
        t = jnp.dot((_silu(gate) * up).astype(BF16), wd_ref[f0:f1, :], preferred_element_type=F32)
        acc = t if acc is None else acc + t
    y = acc * lax.rsqrt(jnp.mean(acc * acc, axis=-1, keepdims=True) + RMS_EPS) * gpost_ref[...]
    o_ref[0] = x + gt_ref[0] * y


def _dense_ffn(x, sc, sh, gt, gpre, gpost, wg, wu, wd, tm):
    B, T, D = x.shape
    tok = pl.BlockSpec((1, tm, D), lambda b, i: (b, i, 0))
    vec = pl.BlockSpec((1, 1, D), lambda b, i: (b, 0, 0))
    const2 = lambda arr: pl.BlockSpec(arr.shape, lambda b, i: (0, 0))
    return pl.pallas_call(
        functools.partial(_ffn_kernel, tf=FFN_CHUNK),
        out_shape=jax.ShapeDtypeStruct((B, T, D), F32),
        grid=(B, T // tm),
        in_specs=[tok, vec, vec, vec, const2(gpre), const2(gpost), const2(wg), const2(wu), const2(wd)],
        out_specs=tok,
        compiler_params=_cparams(("arbitrary", "arbitrary")),
        name="dense_swiglu",
    )(x, sc, sh, gt, gpre, gpost, wg, wu, wd)


MOE_TOKEN_TILE = 256
MOE_ROW_TILE = 512
SEG_ALIGN = 16
SEG_PIECES = (256, 128, 64, 32, 16)
MOE_SMALL_SEG = 128


def _route_kernel(x_ref, sc_ref, sh_ref, gpre_ref, rw_ref, rb_ref, h_ref, mi_ref, mp_ref, cnt_ref):
    tm = x_ref.shape[1]
    h = _rms_mod(x_ref[0], gpre_ref[...], sc_ref[0], sh_ref[0])
    h_ref[...] = h.astype(BF16)
    h_hi = h.astype(BF16)
    h_lo = (h - h_hi.astype(F32)).astype(BF16)
    w = rw_ref[...]
    w_hi = w.astype(BF16)
    w_lo = (w - w_hi.astype(F32)).astype(BF16)
    logits = jnp.dot(jnp.concatenate([h_hi, h_lo, h_hi], axis=1), jnp.concatenate([w_hi, w_hi, w_lo], axis=0),
                     preferred_element_type=F32) + rb_ref[...]
    lane = lax.broadcasted_iota(jnp.int32, logits.shape, 1)
    v1 = jnp.max(logits, axis=-1, keepdims=True)
    i1 = jnp.min(jnp.where(logits == v1, lane, LANES), axis=-1, keepdims=True)
    rest = jnp.where(lane == i1, -jnp.inf, logits)
    v2 = jnp.max(rest, axis=-1, keepdims=True)
    i2 = jnp.min(jnp.where(rest == v2, lane, LANES), axis=-1, keepdims=True)
    e2 = jnp.exp(v2 - v1)
    p1 = 1.0 / (1.0 + e2)
    p2 = e2 / (1.0 + e2)
    oh1 = (lane == i1).astype(F32)
    oh2 = (lane == i2).astype(F32)
    both = oh1 + oh2
    earlier = (lax.broadcasted_iota(jnp.int32, (tm, tm), 1)
               < lax.broadcasted_iota(jnp.int32, (tm, tm), 0)).astype(BF16)
    before = jnp.dot(earlier, both.astype(BF16), preferred_element_type=F32)
    r1 = jnp.sum(oh1 * before, axis=-1, keepdims=True).astype(jnp.int32)
    r2 = jnp.sum(oh2 * before, axis=-1, keepdims=True).astype(jnp.int32)
    col = lax.broadcasted_iota(jnp.int32, mi_ref.shape, 1)
    mi_ref[...] = jnp.where(col == 0, i1, jnp.where(col == 1, i2, jnp.where(col == 2, r1,
                                                                           jnp.where(col == 3, r2, 0))))
    mp_ref[...] = jnp.where(col == 0, p1, jnp.where(col == 1, p2, 0.0))
    cnt_ref[0] = jnp.sum(both, axis=0, keepdims=True).astype(jnp.int32)


def _segment_pieces(n_rows):
    out = []
    for s in SEG_PIECES:
        if s == MOE_TOKEN_TILE:
            out.append((n_rows == s, 0, s))
        else:
            out.append(((n_rows & s) != 0, pl.multiple_of((n_rows // (2 * s)) * (2 * s), SEG_ALIGN), s))
    return out


def _all_segments_small(cnt_ref, tile):
    most = cnt_ref[tile * N_EXPERTS]
    for e in range(1, N_EXPERTS):
        most = jnp.maximum(most, cnt_ref[tile * N_EXPERTS + e])
    return most <= MOE_SMALL_SEG


def _dispatch_kernel(seg_ref, cnt_ref, h_ref, mit_ref, init_ref, xs_ref, buf_ref, sem):
    del init_ref
    tm = h_ref.shape[0]
    i = pl.program_id(0)
    e1, e2 = mit_ref[0:1, :], mit_ref[1:2, :]
    r1, r2 = mit_ref[2:3, :], mit_ref[3:4, :]
    slot = i % 2

    def compact(cap):
        row = lax.broadcasted_iota(jnp.int32, (cap, tm), 0)
        select = jnp.concatenate(
            [jnp.logical_or(jnp.logical_and(e1 == e, r1 == row), jnp.logical_and(e2 == e, r2 == row))
             for e in range(N_EXPERTS)], axis=0).astype(BF16)
        rows = jnp.dot(select, h_ref[...], preferred_element_type=F32).astype(BF16)
        for e in range(N_EXPERTS):
            buf_ref[slot, e * tm:e * tm + cap, :] = rows[e * cap:(e + 1) * cap]

    small = _all_segments_small(cnt_ref, i)
    pl.when(small)(functools.partial(compact, MOE_SMALL_SEG))
    pl.when(jnp.logical_not(small))(functools.partial(compact, tm))

    def segment_copies(tile, buf, e):
        n = cnt_ref[tile * N_EXPERTS + e]
        n_rows = ((n + SEG_ALIGN - 1) // SEG_ALIGN) * SEG_ALIGN
        dst = pl.multiple_of(seg_ref[tile * N_EXPERTS + e], SEG_ALIGN)
        return [(cond, pltpu.make_async_copy(buf_ref.at[buf, pl.ds(e * tm + off, s), :],
                                             xs_ref.at[pl.ds(dst + off, s), :], sem))
                for cond, off, s in _segment_pieces(n_rows)]

    def for_all_segments(tile, buf, action):
        for e in range(N_EXPERTS):
            for cond, cp in segment_copies(tile, buf, e):
                pl.when(cond)(getattr(cp, action))

    @pl.when(i > 0)
    def _():
        for_all_segments(i - 1, 1 - slot, "wait")

    for_all_segments(i, slot, "start")

    @pl.when(i == pl.num_programs(0) - 1)
    def _():
        for_all_segments(i, slot, "wait")


def _expert_kernel(te_ref, nv_ref, xs_ref, wg_ref, wu_ref, wd_ref, ys_ref, acc_ref):
    del te_ref
    r = pl.program_id(0)
    f = pl.program_id(1)
    valid = r < nv_ref[0]

    @pl.when(jnp.logical_and(valid, f == 0))
    def _():
        acc_ref[...] = jnp.zeros_like(acc_ref)

    @pl.when(valid)
    def _():
        x = xs_ref[...]
        tf = wg_ref.shape[2]
        acc = acc_ref[...]
        for f0 in range(0, tf, FFN_CHUNK):
            f1 = min(f0 + FFN_CHUNK, tf)
            gate = jnp.dot(x, wg_ref[0, :, f0:f1], preferred_element_type=F32)
            up = jnp.dot(x, wu_ref[0, :, f0:f1], preferred_element_type=F32)
            acc = acc + jnp.dot((_silu(gate) * up).astype(BF16), wd_ref[0, f0:f1, :],
                                preferred_element_type=F32)
        acc_ref[...] = acc

    @pl.when(f == pl.num_programs(1) - 1)
    def _():
        @pl.when(valid)
        def _():
            ys_ref[...] = acc_ref[...].astype(ys_ref.dtype)

        @pl.when(jnp.logical_not(valid))
        def _():
            ys_ref[...] = jnp.zeros_like(ys_ref)


def _combine_kernel(seg_ref, cnt_ref, ys_ref, mi_ref, mp_ref, x_ref, gt_ref, gpost_ref, o_ref, win_ref,
                    pair_ref, sem):
    tm = x_ref.shape[1]
    i = pl.program_id(0) * pl.num_programs(1) + pl.program_id(1)
    n_tiles = pl.num_programs(0) * pl.num_programs(1)
    slot = i % 2

    def window_copy(tile, buf, e, rows):
        src = pl.multiple_of(seg_ref[tile * N_EXPERTS + e], SEG_ALIGN)
        return pltpu.make_async_copy(ys_ref.at[pl.ds(src, rows), :], win_ref.at[buf, pl.ds(e * rows, rows), :],
                                     sem.at[buf, e])

    def for_all_windows(tile, buf, action):
        small = _all_segments_small(cnt_ref, tile)
        for rows, cond in ((MOE_SMALL_SEG, small), (tm, jnp.logical_not(small))):
            @pl.when(cond)
            def _():
                for e in range(N_EXPERTS):
                    getattr(window_copy(tile, buf, e, rows), action)()

    @pl.when(i == 0)
    def _():
        for_all_windows(i, slot, "start")

    @pl.when(i + 1 < n_tiles)
    def _():
        for_all_windows(i + 1, 1 - slot, "start")

    for_all_windows(i, slot, "wait")
    e1, e2 = mi_ref[:, 0:1], mi_ref[:, 1:2]
    r1, r2 = mi_ref[:, 2:3], mi_ref[:, 3:4]

    def expand_rows(cap):
        col = lax.broadcasted_iota(jnp.int32, (tm, cap), 1)
        expand = jnp.concatenate(
            [jnp.concatenate([jnp.logical_and(e1 == e, r1 == col), jnp.logical_and(e2 == e, r2 == col)], axis=0)
             for e in range(N_EXPERTS)], axis=1).astype(BF16)
        pair_ref[...] = jnp.dot(expand, win_ref[slot, 0:N_EXPERTS * cap, :], preferred_element_type=F32)

    small = _all_segments_small(cnt_ref, i)
    pl.when(small)(functools.partial(expand_rows, MOE_SMALL_SEG))
    pl.when(jnp.logical_not(small))(functools.partial(expand_rows, tm))
    y = mp_ref[:, 0:1] * pair_ref[0:tm, :] + mp_ref[:, 1:2] * pair_ref[tm:2 * tm, :]
    y = y * lax.rsqrt(jnp.mean(y * y, axis=-1, keepdims=True) + RMS_EPS) * gpost_ref[...]
    o_ref[0] = x_ref[0] + gt_ref[0] * y


def _moe_ffn(x, sc, sh, gt, gpre, gpost, rw, rb, wg, wu, wd, tf):
    B, T, D = x.shape
    E, _, F = wg.shape
    tm = min(MOE_TOKEN_TILE, T)
    assert tm == MOE_TOKEN_TILE and E == N_EXPERTS
    nT = T // tm
    n_tok_tiles = B * nT
    N = B * T
    max_rows = 2 * N + n_tok_tiles * E * (SEG_ALIGN - 1) + E * (MOE_ROW_TILE - SEG_ALIGN)
    n_row_tiles = -(-max_rows // MOE_ROW_TILE) + 1
    P = n_row_tiles * MOE_ROW_TILE

    vec = pl.BlockSpec((1, 1, D), lambda b, j: (b, 0, 0))
    const2 = lambda arr: pl.BlockSpec(arr.shape, lambda b, j: (0, 0))
    flat = lambda n: pl.BlockSpec((tm, n), lambda b, j: (b * nT + j, 0))
    h, mi, mp, cnt = pl.pallas_call(
        _route_kernel,
        out_shape=(jax.ShapeDtypeStruct((N, D), BF16), jax.ShapeDtypeStruct((N, 8), jnp.int32),
                   jax.ShapeDtypeStruct((N, 8), F32), jax.ShapeDtypeStruct((n_tok_tiles, 1, LANES), jnp.int32)),
        grid=(B, nT),
        in_specs=[pl.BlockSpec((1, tm, D), lambda b, j: (b, j, 0)), vec, vec, const2(gpre), const2(rw),
                  const2(rb)],
        out_specs=(flat(D), flat(8), flat(8), pl.BlockSpec((1, 1, LANES), lambda b, j: (b * nT + j, 0, 0))),
        compiler_params=_cparams(("arbitrary", "arbitrary")),
        name="moe_route",
    )(x, sc, sh, gpre, rw, rb)

    counts = cnt[:, 0, :E]
    seg_len = (counts + SEG_ALIGN - 1) // SEG_ALIGN * SEG_ALIGN
    group_len = (jnp.sum(seg_len, axis=0) + MOE_ROW_TILE - 1) // MOE_ROW_TILE * MOE_ROW_TILE
    group_end = jnp.cumsum(group_len)
    seg_start = (group_end - group_len)[None, :] + jnp.cumsum(seg_len, axis=0) - seg_len
    seg_start = seg_start.reshape(-1).astype(jnp.int32)
    counts = counts.reshape(-1)
    n_valid = (group_end[-1:] // MOE_ROW_TILE).astype(jnp.int32)
    tile_first_row = jnp.arange(n_row_tiles, dtype=jnp.int32) * MOE_ROW_TILE
    tile_expert = jnp.minimum(jnp.sum(tile_first_row[:, None] >= group_end[None, :], axis=1), E - 1)
    tile_expert = tile_expert.astype(jnp.int32)

    xs = pl.pallas_call(
        _dispatch_kernel,
        out_shape=jax.ShapeDtypeStruct((P, D), BF16),
        grid_spec=pltpu.PrefetchScalarGridSpec(
            num_scalar_prefetch=2,
            grid=(n_tok_tiles,),
            in_specs=[pl.BlockSpec((tm, D), lambda i, seg, n: (i, 0)),
                      pl.BlockSpec((8, tm), lambda i, seg, n: (0, i)),
                      pl.BlockSpec(memory_space=pl.ANY)],
            out_specs=pl.BlockSpec(memory_space=pl.ANY),
            scratch_shapes=[pltpu.VMEM((2, E * tm, D), BF16), pltpu.SemaphoreType.DMA(())]),
        input_output_aliases={4: 0},
        compiler_params=_cparams(("arbitrary",)),
        name="moe_dispatch",
    )(seg_start, counts, h, mi.T, jnp.zeros((P, D), BF16))

    nf = F // tf
    live = lambda r, f, nv: jnp.where(r < nv[0], f, nf - 1)
    ys = pl.pallas_call(
        _expert_kernel,
        out_shape=jax.ShapeDtypeStruct((P, D), BF16),
        grid_spec=pltpu.PrefetchScalarGridSpec(
            num_scalar_prefetch=2,
            grid=(n_row_tiles, nf),
            in_specs=[pl.BlockSpec((MOE_ROW_TILE, D), lambda r, f, te, nv: (r, 0)),
                      pl.BlockSpec((1, D, tf), lambda r, f, te, nv: (te[r], 0, live(r, f, nv))),
                      pl.BlockSpec((1, D, tf), lambda r, f, te, nv: (te[r], 0, live(r, f, nv))),
                      pl.BlockSpec((1, tf, D), lambda r, f, te, nv: (te[r], live(r, f, nv), 0))],
            out_specs=pl.BlockSpec((MOE_ROW_TILE, D), lambda r, f, te, nv: (r, 0)),
            scratch_shapes=[pltpu.VMEM((MOE_ROW_TILE, D), F32)]),
        compiler_params=_cparams(("arbitrary", "arbitrary")),
        name="moe_experts",
    )(tile_expert, n_valid, xs, wg, wu, wd)

    return pl.pallas_call(
        _combine_kernel,
        out_shape=jax.ShapeDtypeStruct((B, T, D), F32),
        grid_spec=pltpu.PrefetchScalarGridSpec(
            num_scalar_prefetch=2,
            grid=(B, nT),
            in_specs=[pl.BlockSpec(memory_space=pl.ANY),
                      pl.BlockSpec((tm, 8), lambda b, j, seg, n: (b * nT + j, 0)),
                      pl.BlockSpec((tm, 8), lambda b, j, seg, n: (b * nT + j, 0)),
                      pl.BlockSpec((1, tm, D), lambda b, j, seg, n: (b, j, 0)),
                      pl.BlockSpec((1, 1, D), lambda b, j, seg, n: (b, 0, 0)),
                      pl.BlockSpec(gpost.shape, lambda b, j, seg, n: (0, 0))],
            out_specs=pl.BlockSpec((1, tm, D), lambda b, j, seg, n: (b, j, 0)),
            scratch_shapes=[pltpu.VMEM((2, E * tm, D), BF16), pltpu.VMEM((2 * tm, D), F32),
                            pltpu.SemaphoreType.DMA((2, E))]),
        compiler_params=_cparams(("arbitrary", "arbitrary")),
        name="moe_combine",
    )(seg_start, counts, ys, mi, mp, x, gt, gpost)


def _rope_tables(positions, groups):
    d = 32
    inv = 1.0 / (ROPE_THETA ** (jnp.arange(0, d, 2, dtype=F32) / d))
    ang = positions.astype(F32)[..., None] * inv
    cos, sin = jnp.cos(ang), jnp.sin(ang)
    cos = jnp.tile(jnp.concatenate([cos, cos], axis=-1), (1, 1, groups))
    sin = jnp.tile(jnp.concatenate([-sin, sin], axis=-1), (1, 1, groups))
    return cos, sin


def _pad_rows(w, rows, offset):
    out = jnp.zeros((rows, w.shape[1]), w.dtype)
    return out.at[offset:offset + w.shape[0]].set(w)


def kernel(x, c, positions, ada_w, ada_b, norm_mix_pre, norm_mix_post, norm_ffn_pre, norm_ffn_post, w_in, gla_gate_w2, gla_gate_b, gla_norm, diff_lambda, diff_subln, conv_w, conv_b, conv_ln_g, conv_ln_b, rwkv_mu, rwkv_w0, rwkv_w2, rwkv_a0, rwkv_a2, rwkv_g2, rwkv_k_k, rwkv_k_a, rwkv_r_k, rwkv_ln_g, rwkv_ln_b, w_branch, w_out, ffn_w_gate, ffn_w_up, ffn_w_down, router_w, router_b, moe_w_gate, moe_w_up, moe_w_down):
    B, T, D = x.shape
    L = ada_w.shape[0]
    W = D // N_BRANCH
    hk = gla_gate_b.shape[1]
    decay_rank = rwkv_w2.shape[1]
    a_rank = rwkv_a2.shape[1]
    gate_rank = rwkv_g2.shape[1]
    assert decay_rank + a_rank == LANES and 2 * hk == W and gate_rank == LANES
    n_mix = (3 * W + decay_rank + a_rank + gate_rank) + 3 * W + 3 * W
    sizes = (hk, hk, W, W, GLA_GATE_RANK, W, W, W, W, W, 3 * W + LANES + gate_rank, N_BRANCH * D)
    offs = [0]
    for s in sizes:
        offs.append(offs[-1] + s)
    assert offs[-1] == w_in.shape[2]
    tm = min(TOKEN_TILE, T)
    tb = min(SEQ_BLOCK, T)

    mod = _modulation(c, ada_w, ada_b)
    cos, sin = _rope_tables(positions, LANES // 32)

    for l in range(L):
        m = mod[l].reshape(B, 1, 6 * D)
        sh_m, sc_m, gt_m, sh_f, sc_f, gt_f = [m[:, :, i * D:(i + 1) * D] for i in range(6)]

        wl = w_in[l]
        gz_cols = jnp.zeros((D, LANES), F32).at[:, :GLA_GATE_RANK].set(wl[:, offs[4]:offs[5]])
        w_mix = jnp.concatenate([wl[:, offs[10]:offs[11]], wl[:, offs[0]:offs[4]], wl[:, offs[5]:offs[8]],
                                 gz_cols], axis=1).astype(BF16)
        w_conv = wl[:, offs[8]:offs[10]].astype(BF16)
        w_gate = wl[:, offs[11]:offs[12]].astype(BF16)
        proj, o_conv = _in_projection_conv(x, sc_m, sh_m, norm_mix_pre[l][None], w_conv, w_mix,
                                           _pad_rows(conv_w[l], 32, 0), conv_b[l][None],
                                           conv_ln_g[l][None], conv_ln_b[l][None], tm)

        o_diff = _diff_attention(proj, cos, sin, diff_lambda[l],
                                 diff_subln[l][None], col_q=7, col_k=8, col_v=9, layer_idx=l)
        vecs = dict(mu=rwkv_mu[l][None], w0=rwkv_w0[l][None], a0=rwkv_a0[l][None],
                    kk=rwkv_k_k[l][None], ka=rwkv_k_a[l][None], rk=rwkv_r_k[l].reshape(1, W),
                    lng=rwkv_ln_g[l][None], lnb=rwkv_ln_b[l][None])
        o_gla, o_rwkv = _gla_rwkv7(
            proj, _pad_rows(gla_gate_w2[l], LANES, 0), gla_gate_b[l][None],
            jnp.tile(gla_norm[l], GLA_HEADS)[None], 4, 5, 6, n_mix // LANES,
            vecs, _pad_rows(rwkv_w2[l], LANES, 0).astype(BF16),
            _pad_rows(rwkv_a2[l], LANES, decay_rank).astype(BF16), rwkv_g2[l].astype(BF16), 0, tb)
        x = _merge(x, sc_m, sh_m, gt_m, norm_mix_pre[l][None], norm_mix_post[l][None], w_gate,
                   (o_gla, o_diff, o_conv, o_rwkv), w_branch[l].astype(BF16), w_out[l].astype(BF16), tm)

        i = l // 2
        if l % 2 == 0:
            x = _dense_ffn(x, sc_f, sh_f, gt_f, norm_ffn_pre[l][None], norm_ffn_post[l][None],
                           ffn_w_gate[i].astype(BF16), ffn_w_up[i].astype(BF16),
                           ffn_w_down[i].astype(BF16), tm)
        else:
            rw = jnp.zeros((D, LANES), F32).at[:, :N_EXPERTS].set(router_w[i])
            rb = jnp.full((1, LANES), -jnp.inf, F32).at[0, :N_EXPERTS].set(router_b[i])
            x = _moe_ffn(x, sc_f, sh_f, gt_f, norm_ffn_pre[l][None], norm_ffn_post[l][None], rw, rb,
                         moe_w_gate[i].astype(BF16), moe_w_up[i].astype(BF16),
                         moe_w_down[i].astype(BF16), tf=moe_w_gate.shape[3])
    return x
```

```python
import functools
import math

import jax
import jax.numpy as jnp
from jax import lax
from jax.experimental import pallas as pl
from jax.experimental.pallas import tpu as pltpu

F32 = jnp.float32
BF16 = jnp.bfloat16
HIGHEST = lax.Precision.HIGHEST
LOG2_E = math.log2(math.e)

N_BRANCH = 4
GLA_HEADS = 4
GLA_GATE_RANK = 16
GLA_GATE_NORMALIZER = 16.0
GLA_CHUNK = 32
DIFF_HEADS = 4
ROPE_THETA = 10000.0
CONV_WIDTH = 31
RWKV_HEADS = 4
RWKV_CHUNK = 64
RWKV_SUB = 16
N_EXPERTS = 8
RMS_EPS = 1e-6
LN_EPS = 1e-5
RWKV_GN_EPS = 64e-5
LANES = 128
SUBLANES = 8
VMEM_LIMIT = 56 * 1024 * 1024

TOKEN_TILE = 512
SEQ_BLOCK = 1024
ATTN_BLOCK = 512
CONV_ROWS = 128
FFN_CHUNK = 512


def _cparams(sem, fuse_inputs=None):
    return pltpu.CompilerParams(dimension_semantics=sem, vmem_limit_bytes=VMEM_LIMIT,
                                allow_input_fusion=fuse_inputs)


def _mm(a, b):
    return jnp.dot(a.astype(BF16), b.astype(BF16), preferred_element_type=F32)


def _bmm(a, b):
    return lax.dot_general(a.astype(BF16), b.astype(BF16), (((2,), (1,)), ((0,), (0,))),
                           preferred_element_type=F32)


def _mm_f32(a, b):
    return jnp.dot(a, b, precision=HIGHEST, preferred_element_type=F32)


def _hi_lo(x):
    hi = x.astype(BF16)
    return jnp.concatenate([hi, (x - hi.astype(F32)).astype(BF16)], axis=1)


def _sigmoid(x):
    return 0.5 * jnp.tanh(0.5 * x) + 0.5


def _silu(x):
    return x * _sigmoid(x)


def _softplus(x):
    return jnp.maximum(x, 0.0) + jnp.log(1.0 + jnp.exp(-jnp.abs(x)))


def _group_matrix(n, group):
    r = lax.broadcasted_iota(jnp.int32, (n, n), 0) // group
    c = lax.broadcasted_iota(jnp.int32, (n, n), 1) // group
    return r == c


def _rms_mod(x, gain, scale, shift):
    y = x * lax.rsqrt(jnp.mean(x * x, axis=-1, keepdims=True) + RMS_EPS)
    return y * gain * (1.0 + scale) + shift


def _mod_kernel(c_ref, w_ref, b_ref, o_ref):
    o_ref[0] = _mm_f32(_silu(c_ref[...]), w_ref[0]) + b_ref[0]


def _modulation(c, ada_w, ada_b):
    L, D, M = ada_w.shape
    B = c.shape[0]
    tn = M // 4
    return pl.pallas_call(
        _mod_kernel,
        out_shape=jax.ShapeDtypeStruct((L, B, M), F32),
        grid=(L, M // tn),
        in_specs=[pl.BlockSpec((B, D), lambda l, j: (0, 0)),
                  pl.BlockSpec((1, D, tn), lambda l, j: (l, 0, j)),
                  pl.BlockSpec((1, 1, tn), lambda l, j: (l, 0, j))],
        out_specs=pl.BlockSpec((1, B, tn), lambda l, j: (l, 0, j)),
        compiler_params=_cparams(("arbitrary", "arbitrary")),
        name="adaln_mod",
    )(c, ada_w, ada_b.reshape(L, 1, M))


CONV_HALO = 32


def _inproj_conv_kernel(x_ref, sc_ref, sh_ref, g_ref, wab_ref, w_ref, cw_ref, cb_ref, lg_ref, lb_ref,
                        o_ref, oc_ref, u_ref, *, rb):
    tm = x_ref.shape[1]
    ch = oc_ref.shape[2]
    h = _rms_mod(x_ref[0], g_ref[...], sc_ref[0], sh_ref[0]).astype(BF16)

    @pl.when(pl.program_id(1) == 0)
    def _():
        u_ref[0:CONV_HALO, :] = jnp.zeros((CONV_HALO, ch), F32)

    ab = jnp.dot(h, wab_ref[...], preferred_element_type=F32)
    u_ref[CONV_HALO:CONV_HALO + tm, :] = ab[:, 0:ch] * _sigmoid(ab[:, ch:2 * ch])
    o_ref[0] = jnp.dot(h, w_ref[...], preferred_element_type=F32)

    first = CONV_HALO - (CONV_WIDTH - 1)
    for i in range(tm // rb):
        r0 = i * rb
        win = u_ref[r0:r0 + rb + CONV_HALO, :]
        acc = jnp.zeros((rb, ch), F32)
        for s in range(SUBLANES):
            taps = [j for j in range(CONV_WIDTH) if (first + j) % SUBLANES == s]
            rolled = win if s == 0 else pltpu.roll(win, rb + CONV_HALO - s, 0)
            for j in taps:
                a0 = first + j - s
                acc = acc + cw_ref[j:j + 1, :] * rolled[a0:a0 + rb, :]
        y = acc + cb_ref[...]
        mu = jnp.mean(y, axis=-1, keepdims=True)
        yc = y - mu
        var = jnp.mean(yc * yc, axis=-1, keepdims=True)
        y = yc * lax.rsqrt(var + LN_EPS) * lg_ref[...] + lb_ref[...]
        oc_ref[0, r0:r0 + rb, :] = _silu(y).astype(oc_ref.dtype)
    u_ref[0:CONV_HALO, :] = u_ref[tm:tm + CONV_HALO, :]


def _in_projection_conv(x, sc, sh, gain, w_ab, w, conv_w, conv_b, ln_g, ln_b, tm):
    B, T, D = x.shape
    n = w.shape[1]
    ch = conv_w.shape[1]
    const = lambda arr: pl.BlockSpec(arr.shape, lambda b, i: (0, 0))
    return pl.pallas_call(
        functools.partial(_inproj_conv_kernel, rb=min(CONV_ROWS, tm)),
        out_shape=(jax.ShapeDtypeStruct((B, T, n), F32), jax.ShapeDtypeStruct((B, T, ch), BF16)),
        grid=(B, T // tm),
        in_specs=[pl.BlockSpec((1, tm, D), lambda b, i: (b, i, 0)),
                  pl.BlockSpec((1, 1, D), lambda b, i: (b, 0, 0)),
                  pl.BlockSpec((1, 1, D), lambda b, i: (b, 0, 0)),
                  const(gain), const(w_ab), const(w), const(conv_w), const(conv_b), const(ln_g), const(ln_b)],
        out_specs=(pl.BlockSpec((1, tm, n), lambda b, i: (b, i, 0)),
                   pl.BlockSpec((1, tm, ch), lambda b, i: (b, i, 0))),
        scratch_shapes=[pltpu.VMEM((tm + CONV_HALO, ch), F32)],
        compiler_params=_cparams(("arbitrary", "arbitrary"), [i in (4, 5) for i in range(10)]),
        name="in_proj_conv",
    )(x, sc, sh, gain, w_ab, w, conv_w, conv_b, ln_g, ln_b)


def _gla_kernel(qk_ref, v_ref, og_ref, gz_ref, w2_ref, gb_ref, ng_ref, o_ref, s_ref, g_ref, r_ref,
                qd_ref, kd_ref, oacc_ref, *, n_chunk):
    C = GLA_CHUNK
    hk = qk_ref.shape[2] // 2
    hv = v_ref.shape[2]
    dk = hk // GLA_HEADS
    dv = hv // GLA_HEADS

    tb = n_chunk * C

    @pl.when(pl.program_id(1) == 0)
    def _():
        s_ref[...] = jnp.zeros_like(s_ref)

    z = _mm(gz_ref[0], w2_ref[...]) + gb_ref[...]
    gk = (jnp.minimum(z, 0.0) - jnp.log(1.0 + jnp.exp(-jnp.abs(z)))) * (LOG2_E / GLA_GATE_NORMALIZER)
    tri = (lax.broadcasted_iota(jnp.int32, (C, C), 1)
           <= lax.broadcasted_iota(jnp.int32, (C, C), 0)).astype(BF16)
    sums = _bmm(jnp.broadcast_to(tri[None], (n_chunk, C, C)), _hi_lo(gk).reshape(n_chunk, C, 2 * hk))
    G3 = sums[:, :, 0:hk] + sums[:, :, hk:2 * hk]
    G_all = G3.reshape(tb, hk)
    R_all = (jnp.broadcast_to(G3[:, C - 1:C, :], (n_chunk, C, hk)) - G3).reshape(tb, hk)
    g_ref[...] = G_all
    r_ref[...] = R_all
    qd_ref[...] = (qk_ref[0, :, 0:hk] * (dk ** -0.5) * jnp.exp2(G_all)).astype(BF16)
    kd_ref[...] = (qk_ref[0, :, hk:2 * hk] * jnp.exp2(R_all)).astype(BF16)

    half = C // 2

    def key_not_after_query(nj, ni):
        return (lax.broadcasted_iota(jnp.int32, (nj, ni, hk), 0)
                <= lax.broadcasted_iota(jnp.int32, (nj, ni, hk), 1))

    causal_first, causal_second = key_not_after_query(half, C), key_not_after_query(half, half)
    er = lax.broadcasted_iota(jnp.int32, (hk, hv), 0) // dk
    ec = lax.broadcasted_iota(jnp.int32, (hk, hv), 1) // dv
    expand = (er == ec).astype(BF16)
    sr = lax.broadcasted_iota(jnp.int32, (hv, hk), 0) // dv
    scol = lax.broadcasted_iota(jnp.int32, (hv, hk), 1) // dk
    state_mask = sr == scol
    head_mean = (_group_matrix(hv, dv).astype(F32) / dv).astype(BF16)

    def chunk(ci, carry):
        r0 = pl.multiple_of(ci * C, C)
        rows = pl.ds(r0, C)
        q = qk_ref[0, rows, 0:hk] * (dk ** -0.5)
        k = qk_ref[0, rows, hk:2 * hk]
        v = v_ref[0, rows, :]
        G = g_ref[rows, :]
        def keys_to_queries(j0, i0, keep):
            nj, ni = keep.shape[0], keep.shape[1]
            pair = (nj, ni, hk)
            g_diff = (jnp.broadcast_to(G[i0:i0 + ni][None], pair)
                      - jnp.broadcast_to(G[j0:j0 + nj][:, None, :], pair))
            decay = jnp.exp2(jnp.where(keep, g_diff, -jnp.inf))
            p = (jnp.broadcast_to(q[i0:i0 + ni][None], pair)
                 * jnp.broadcast_to(k[j0:j0 + nj][:, None, :], pair) * decay)
            a_exp = jnp.dot(p.reshape(nj * ni, hk).astype(BF16), expand, preferred_element_type=F32)
            v_j = jnp.broadcast_to(v[j0:j0 + nj][:, None, :], (nj, ni, hv))
            return jnp.sum(a_exp.reshape(nj, ni, hv) * v_j, axis=0)

        o_second = keys_to_queries(half, half, causal_second)
        o_intra = keys_to_queries(0, 0, causal_first) + jnp.concatenate([jnp.zeros_like(o_second), o_second],
                                                                        axis=0)
        g_total = G[0:1, :] + r_ref[pl.ds(r0, 8), :][0:1, :]
        s = s_ref[...]
        o_inter = lax.dot_general(qd_ref[rows, :], s.astype(BF16), (((1,), (1,)), ((), ())),
                                  preferred_element_type=F32)
        kv = lax.dot_general(v.astype(BF16), kd_ref[rows, :], (((0,), (0,)), ((), ())),
                             preferred_element_type=F32)
        s_ref[...] = s * jnp.exp2(g_total) + jnp.where(state_mask, kv, 0.0)
        oacc_ref[rows, :] = o_intra + o_inter
        return carry

    lax.fori_loop(0, n_chunk, chunk, 0, unroll=True)

    o = oacc_ref[...]
    ms = jnp.dot(_hi_lo(o * o), jnp.concatenate([head_mean, head_mean], axis=0),
                 preferred_element_type=F32)
    o = o * lax.rsqrt(ms + RMS_EPS) * ng_ref[...] * _silu(og_ref[0])
    o_ref[0] = o.astype(o_ref.dtype)


def _rope(t, cos, sin_signed):
    d = 32
    half = d // 2
    out = []
    for s in range(t.shape[1] // LANES):
        x = t[:, s * LANES:(s + 1) * LANES]
        lane = lax.broadcasted_iota(jnp.int32, x.shape, 1)
        up = pltpu.roll(x, LANES - half, 1)
        down = pltpu.roll(x, half, 1)
        rot = jnp.where((lane % d) < half, up, down)
        out.append(x * cos + rot * sin_signed)
    return jnp.concatenate(out, axis=1)


def _diff_kernel(q_ref, k_ref, v_ref, cosq_ref, sinq_ref, cosk_ref, sink_ref, lam_ref, g_ref, o_ref,
                 ks, vs, *, tq, lam_init):
    H = DIFF_HEADS
    d = q_ref.shape[2] // (2 * H)
    dv = v_ref.shape[2] // H
    j = pl.program_id(1)

    @pl.when(j == 0)
    def _():
        k = _rope(k_ref[0], cosk_ref[0], sink_ref[0])
        v = v_ref[0]
        for hc in range(2 * H):
            ks[hc] = k[:, hc * d:(hc + 1) * d].astype(BF16)
        ones_col = (lax.broadcasted_iota(jnp.int32, (v.shape[0], dv), 1) == 0).astype(BF16)
        for h in range(H):
            vs[h] = jnp.concatenate([v[:, h * dv:(h + 1) * dv].astype(BF16), ones_col], axis=1)

    q = _rope(q_ref[0], cosq_ref[0], sinq_ref[0]) * (d ** -0.5 * LOG2_E)
    lp = lam_ref[...]
    lam = (jnp.exp(jnp.sum(lp[0:1] * lp[1:2], axis=-1, keepdims=True))
           - jnp.exp(jnp.sum(lp[2:3] * lp[3:4], axis=-1, keepdims=True)) + lam_init)
    on_or_below_diag = (lax.broadcasted_iota(jnp.int32, (tq, tq), 1)
                        <= lax.broadcasted_iota(jnp.int32, (tq, tq), 0))

    def update(qh, kh, vh, m, acc, masked):
        s = lax.dot_general(qh, kh, (((1,), (1,)), ((), ())), preferred_element_type=F32)
        if masked:
            s = jnp.where(on_or_below_diag, s, -jnp.inf)
        m_new = jnp.maximum(m, jnp.max(s, axis=-1, keepdims=True))
        p = jnp.exp2(s - m_new).astype(BF16)
        acc = jnp.exp2(m - m_new) * acc + jnp.dot(p, vh, preferred_element_type=F32)
        return m_new, acc

    qs = [q[:, hc * d:(hc + 1) * d].astype(BF16) for hc in range(2 * H)]

    def kv_block(kb, carry, masked):
        rows = pl.ds(pl.multiple_of(kb * tq, tq), tq)
        out = []
        for hc in range(2 * H):
            m, acc = carry[2 * hc], carry[2 * hc + 1]
            out.extend(update(qs[hc], ks[hc, rows, :], vs[hc // 2, rows, :], m, acc, masked))
        return tuple(out)

    m0 = jnp.full((tq, 1), -jnp.inf, F32)
    a0 = jnp.zeros((tq, 2 * dv), F32)
    carry = lax.fori_loop(0, j, lambda kb, c: kv_block(kb, c, False), (m0, a0) * (2 * H))
    carry = kv_block(j, carry, True)

    for h in range(H):
        a1, a2 = carry[4 * h + 1], carry[4 * h + 3]
        comp = [a[:, 0:dv] / a[:, dv:dv + 1] for a in (a1, a2)]
        o = comp[0] - lam * comp[1]
        o = o * lax.rsqrt(jnp.mean(o * o, axis=-1, keepdims=True) + RMS_EPS)
        o = o * g_ref[...] * (1.0 - lam_init)
        o_ref[0, :, h * dv:(h + 1) * dv] = o.astype(o_ref.dtype)


def _diff_attention(proj, cos, sin, lam_p, g, col_q, col_k, col_v, layer_idx):
    B, T, _ = proj.shape
    H = DIFF_HEADS
    dv = g.shape[1]
    d = dv // 2
    w = 2 * H * d
    assert cos.shape[2] == LANES
    tq = min(ATTN_BLOCK, T)
    lam_init = 0.8 - 0.6 * math.exp(-0.3 * layer_idx)
    blk = lambda col: pl.BlockSpec((1, tq, w), lambda b, j: (b, j, col))
    full = lambda col: pl.BlockSpec((1, T, w), lambda b, j: (b, 0, col))
    tab_blk = pl.BlockSpec((1, tq, LANES), lambda b, j: (b, j, 0))
    tab_full = pl.BlockSpec((1, T, LANES), lambda b, j: (b, 0, 0))
    return pl.pallas_call(
        functools.partial(_diff_kernel, tq=tq, lam_init=lam_init),
        out_shape=jax.ShapeDtypeStruct((B, T, H * dv), BF16),
        grid=(B, T // tq),
        in_specs=[blk(col_q), full(col_k), full(col_v), tab_blk, tab_blk, tab_full, tab_full,
                  pl.BlockSpec(lam_p.shape, lambda b, j: (0, 0)),
                  pl.BlockSpec(g.shape, lambda b, j: (0, 0))],
        out_specs=pl.BlockSpec((1, tq, H * dv), lambda b, j: (b, j, 0)),
        scratch_shapes=[pltpu.VMEM((2 * H, T, d), BF16),
                        pltpu.VMEM((H, T, 2 * dv), BF16)],
        compiler_params=_cparams(("arbitrary", "arbitrary")),
        name="diff_attention",
    )(proj, proj, proj, cos, sin, cos, sin, lam_p, g)


def _rwkv_kernel(x_ref, mu_ref, w0_ref, w2_ref, a0_ref, a2_ref, g2_ref, kk_ref, ka_ref, rk_ref,
                 lng_ref, lnb_ref, o_ref, s_ref, prev_ref, gate_ref, bonus_ref, dec_ref, qeff_ref, yloc_ref,
                 mlow_ref, nc_ref, y_ref, *, n_chunk):
    C = RWKV_CHUNK
    H = RWKV_HEADS
    W = o_ref.shape[2]
    N = W // H

    @pl.when(pl.program_id(1) == 0)
    def _():
        s_ref[...] = jnp.zeros_like(s_ref)
        prev_ref[...] = jnp.zeros_like(prev_ref)

    lane = lax.broadcasted_iota(jnp.int32, (1, W), 1)
    head_mask = [(lane // N == h).astype(F32) for h in range(H)]
    block_diag = _group_matrix(W, N)
    head_sum = block_diag.astype(F32)
    ti = lax.broadcasted_iota(jnp.int32, (C, C), 0)
    tj = lax.broadcasted_iota(jnp.int32, (C, C), 1)
    tril_incl = ti >= tj
    tril_strict = ti > tj
    same_sub = (ti // RWKV_SUB) == (tj // RWKV_SUB)
    eye = (ti == tj).astype(F32)
    nc = n_chunk
    tb = nc * C

    def head_total(t, two_term=False):
        ones = head_sum.astype(BF16)
        if two_term:
            return jnp.dot(_hi_lo(t), jnp.concatenate([ones, ones], axis=0), preferred_element_type=F32)
        return jnp.dot(t.astype(BF16), ones, preferred_element_type=F32)

    x = x_ref[0]
    first_row = lax.broadcasted_iota(jnp.int32, (tb, 1), 0) == 0
    prev = jnp.where(first_row, prev_ref[...], pltpu.roll(x, 1, 0))
    prev_ref[...] = x[tb - 1:tb, :]
    xm = x + (prev - x) * mu_ref[...]
    r = xm[:, 0:W]
    k = xm[:, W:2 * W]
    v = xm[:, 2 * W:3 * W]
    zz = xm[:, 3 * W:3 * W + LANES]
    zg = xm[:, 3 * W + LANES:]
    w = -_softplus(-(w0_ref[...] + _mm(jnp.tanh(zz), w2_ref[...]))) - 0.5
    lw = -jnp.exp(w) * LOG2_E
    a = _sigmoid(a0_ref[...] + _mm(zz, a2_ref[...]))
    gate_ref[...] = _mm(_sigmoid(zg), g2_ref[...])
    kk = k * kk_ref[...]
    kk = kk / jnp.maximum(jnp.sqrt(head_total(kk * kk, two_term=True)), 1e-12)
    k = k * (1.0 + (a - 1.0) * ka_ref[...])
    b = kk * a
    bonus_ref[...] = head_total(r * k * rk_ref[...]) * v

    c3 = lambda t: t.reshape(nc, C, t.shape[1])
    sums = _bmm(jnp.broadcast_to(tril_incl.astype(BF16)[None], (nc, C, C)), c3(_hi_lo(lw)))
    G3 = sums[:, :, 0:W] + sums[:, :, W:2 * W]
    total3 = jnp.broadcast_to(G3[:, C - 1:C, :], (nc, C, W))
    dec_ref[...] = jnp.exp2(total3).reshape(tb, W)
    G = G3.reshape(tb, W)
    inv = jnp.exp2(-G)
    to_end = jnp.exp2(total3 - G3).reshape(tb, W)
    kap = kk * jnp.exp2(G - lw)
    rho = r * jnp.exp2(G)
    kap3, rho3, v3 = c3(kap), c3(rho), c3(v)
    bet_kt3 = jnp.concatenate([c3(b * inv), c3(k * inv)], axis=1)
    betc3, kc3 = c3(b * to_end), c3(k * to_end)

    def bmm_nt(p, q):
        return lax.dot_general(p.astype(BF16), q.astype(BF16), (((2,), (2,)), ((0,), (0,))),
                               preferred_element_type=F32)

    def bmm_tn(p, q):
        return lax.dot_general(p.astype(BF16), q.astype(BF16), (((1,), (1,)), ((0,), (0,))),
                               preferred_element_type=F32)

    wi = lax.broadcasted_iota(jnp.int32, (C, 2 * C), 0)
    wj = lax.broadcasted_iota(jnp.int32, (C, 2 * C), 1) % C
    A_bk, B_bk = [], []
    kap_b, rho_b = kap.astype(BF16), rho.astype(BF16)
    for h in range(H):
        mask_b = head_mask[h].astype(BF16)
        lhs = jnp.concatenate([c3(kap_b * mask_b), c3(rho_b * mask_b)], axis=1)
        prod = bmm_nt(lhs, bet_kt3)
        A_bk.append(jnp.where(wj < wi, prod[:, 0:C], 0.0))
        B_bk.append(jnp.where(wj <= wi, prod[:, C:2 * C], 0.0))
    A_b = jnp.concatenate([t[:, :, 0:C] for t in A_bk], axis=0)
    B_b = [t[:, :, 0:C] for t in B_bk]
    v3_low = jnp.concatenate([jnp.zeros_like(v3), v3], axis=1)

    Dg = jnp.where(same_sub, A_b, 0.0)
    Lo = A_b - Dg
    D2 = _bmm(Dg, Dg)
    D4 = _bmm(D2, D2)
    D8 = _bmm(D4, D4)
    Dinv = _bmm(_bmm(_bmm(eye - Dg, eye + D2), eye + D4), eye + D8)
    Nn = _bmm(Dinv, Lo)
    N2 = _bmm(Nn, Nn)
    Tm = _bmm(_bmm(eye - Nn, eye + N2), Dinv)
    Tm = [Tm[h * nc:(h + 1) * nc] for h in range(H)]

    def per_head(mats, t):
        head_of_lane = (lax.broadcasted_iota(jnp.int32, (1, 1, t.shape[2]), 2) % W) // N
        acc = _bmm(mats[H - 1], t)
        for h in range(H - 1):
            acc = jnp.where(head_of_lane == h, _bmm(mats[h], t), acc)
        return acc

    akv = per_head(A_bk, v3_low)
    tk = per_head(Tm, jnp.concatenate([kap3, akv], axis=2))
    kap_p = tk[:, :, 0:W]
    v_p = tk[:, :, W:2 * W]
    bb = per_head(B_b, jnp.concatenate([kap_p, v_p], axis=2))
    qeff_ref[...] = (rho3 - bb[:, :, 0:W]).astype(BF16)
    yloc_ref[...] = per_head(B_bk, v3_low) - bb[:, :, W:2 * W]
    mlow_ref[...] = jnp.where(block_diag, bmm_tn(kap_p, betc3), 0.0).astype(BF16)
    nc_ref[...] = jnp.where(block_diag, bmm_tn(jnp.concatenate([v3, v_p], axis=1),
                                               jnp.concatenate([kc3, -betc3], axis=1)), 0.0)

    def chunk(ci, carry):
        r0 = pl.multiple_of(ci * C, C)
        s = s_ref[...]
        sb = s.astype(BF16)
        y = lax.dot_general(qeff_ref[ci], sb, (((1,), (1,)), ((), ())), preferred_element_type=F32)
        y_ref[pl.ds(r0, C), :] = y + yloc_ref[ci]
        s_ref[...] = (s * dec_ref[pl.ds(r0, 8), :][0:1, :]
                      - jnp.dot(sb, mlow_ref[ci], preferred_element_type=F32) + nc_ref[ci])
        return carry

    lax.fori_loop(0, nc, chunk, 0, unroll=True)

    y = y_ref[...]
    mean = head_total(y, two_term=True) / N
    yc = y - mean
    var = head_total(yc * yc) / N
    yn = yc * lax.rsqrt(var + RWKV_GN_EPS) * lng_ref[...] + lnb_ref[...]
    o_ref[0] = ((yn + bonus_ref[...]) * gate_ref[...]).astype(o_ref.dtype)


N_GLA_INPUTS = 7
N_GLA_SCRATCH = 6
N_RWKV_INPUTS = 12


def _gla_rwkv_kernel(*refs, gla_chunks, rwkv_chunks):
    n_in = N_GLA_INPUTS + N_RWKV_INPUTS
    gla_in, rwkv_in = refs[0:N_GLA_INPUTS], refs[N_GLA_INPUTS:n_in]
    gla_out, rwkv_out = refs[n_in], refs[n_in + 1]
    gla_scratch = refs[n_in + 2:n_in + 2 + N_GLA_SCRATCH]
    rwkv_scratch = refs[n_in + 2 + N_GLA_SCRATCH:]
    _gla_kernel(*gla_in, gla_out, *gla_scratch, n_chunk=gla_chunks)
    _rwkv_kernel(*rwkv_in, rwkv_out, *rwkv_scratch, n_chunk=rwkv_chunks)


def _gla_rwkv7(proj, gla_w2p, gla_gb, gla_ng, col_qk, col_v, col_og, col_gz, vecs, w2p, a2p, g2, col_rwkv, tb):
    B, T, _ = proj.shape
    hv = gla_ng.shape[1]
    hk = gla_gb.shape[1]
    W = g2.shape[1]
    cols = vecs["mu"].shape[1]
    names = ("mu", "w0", "w2", "a0", "a2", "g2", "kk", "ka", "rk", "lng", "lnb")
    params = dict(vecs, w2=w2p, a2=a2p, g2=g2)
    assert len(names) + 1 == N_RWKV_INPUTS
    const = lambda arr: pl.BlockSpec(arr.shape, lambda b, j: (0, 0))
    rows = lambda n, col: pl.BlockSpec((1, tb, n), lambda b, j: (b, j, col))
    n_chunk = tb // RWKV_CHUNK
    return pl.pallas_call(
        functools.partial(_gla_rwkv_kernel, gla_chunks=tb // GLA_CHUNK, rwkv_chunks=n_chunk),
        out_shape=(jax.ShapeDtypeStruct((B, T, hv), BF16), jax.ShapeDtypeStruct((B, T, W), BF16)),
        grid=(B, T // tb),
        in_specs=[rows(2 * hk, col_qk), rows(hv, col_v), rows(hv, col_og), rows(LANES, col_gz),
                  const(gla_w2p), const(gla_gb), const(gla_ng), rows(cols, col_rwkv)]
                 + [const(params[n]) for n in names],
        out_specs=(rows(hv, 0), rows(W, 0)),
        scratch_shapes=[pltpu.VMEM((hv, hk), F32), pltpu.VMEM((tb, hk), F32), pltpu.VMEM((tb, hk), F32),
                        pltpu.VMEM((tb, hk), BF16), pltpu.VMEM((tb, hk), BF16), pltpu.VMEM((tb, hv), F32),
                        pltpu.VMEM((W, W), F32), pltpu.VMEM((1, cols), F32),
                        pltpu.VMEM((tb, W), F32), pltpu.VMEM((tb, W), F32), pltpu.VMEM((tb, W), F32),
                        pltpu.VMEM((n_chunk, RWKV_CHUNK, W), BF16), pltpu.VMEM((n_chunk, RWKV_CHUNK, W), F32),
                        pltpu.VMEM((n_chunk, W, W), BF16), pltpu.VMEM((n_chunk, W, W), F32),
                        pltpu.VMEM((tb, W), F32)],
        compiler_params=_cparams(("arbitrary", "arbitrary")),
        name="gla_rwkv7_mixers",
    )(proj, proj, proj, proj, gla_w2p, gla_gb, gla_ng, proj, *[params[n] for n in names])


def _merge_kernel(x_ref, sc_ref, sh_ref, gt_ref, gpre_ref, gpost_ref, wg_ref, b0_ref, b1_ref, b2_ref,
                  b3_ref, wb_ref, wo_ref, o_ref):
    x = x_ref[0]
    D = x.shape[1]
    h = _rms_mod(x, gpre_ref[...], sc_ref[0], sh_ref[0]).astype(BF16)
    merged = None
    for g, br in enumerate((b0_ref, b1_ref, b2_ref, b3_ref)):
        gate = _sigmoid(jnp.dot(h, wg_ref[:, g * D:(g + 1) * D], preferred_element_type=F32))
        t = gate * jnp.dot(br[0], wb_ref[g], preferred_element_type=F32)
        merged = t if merged is None else merged + t
    y = _mm(merged, wo_ref[...])
    y = y * lax.rsqrt(jnp.mean(y * y, axis=-1, keepdims=True) + RMS_EPS) * gpost_ref[...]
    o_ref[0] = x + gt_ref[0] * y


def _merge(x, sc, sh, gt, gpre, gpost, w_gate, branches, w_branch, w_out, tm):
    B, T, D = x.shape
    bw = branches[0].shape[2]
    tok = lambda n: pl.BlockSpec((1, tm, n), lambda b, i: (b, i, 0))
    vec = pl.BlockSpec((1, 1, D), lambda b, i: (b, 0, 0))
    const2 = lambda arr: pl.BlockSpec(arr.shape, lambda b, i: (0, 0))
    return pl.pallas_call(
        _merge_kernel,
        out_shape=jax.ShapeDtypeStruct((B, T, D), F32),
        grid=(B, T // tm),
        in_specs=[tok(D), vec, vec, vec, const2(gpre), const2(gpost), const2(w_gate),
                  tok(bw), tok(bw), tok(bw), tok(bw),
                  pl.BlockSpec(w_branch.shape, lambda b, i: (0, 0, 0)), const2(w_out)],
        out_specs=tok(D),
        compiler_params=_cparams(("arbitrary", "arbitrary"), [i in (6, 11, 12) for i in range(13)]),
        name="merge_out_proj",
    )(x, sc, sh, gt, gpre, gpost, w_gate, *branches, w_branch, w_out)


def _ffn_kernel(x_ref, sc_ref, sh_ref, gt_ref, gpre_ref, gpost_ref, wg_ref, wu_ref, wd_ref, o_ref, *, tf):
    x = x_ref[0]
    h = _rms_mod(x, gpre_ref[...], sc_ref[0], sh_ref[0]).astype(BF16)
    F = wg_ref.shape[1]
    acc = None
    for f0 in range(0, F, tf):
        f1 = min(f0 + tf, F)
        gate = jnp.dot(h, wg_ref[:, f0:f1], preferred_element_type=F32)
        up = jnp.dot(h, wu_ref[:, f0:f1], preferred_element_type=F32)
        t = jnp.dot((_silu(gate) * up).astype(BF16), wd_ref[f0:f1, :], preferred_element_type=F32)
        acc = t if acc is None else acc + t
    y = acc * lax.rsqrt(jnp.mean(acc * acc, axis=-1, keepdims=True) + RMS_EPS) * gpost_ref[...]
    o_ref[0] = x + gt_ref[0] * y


def _dense_ffn(x, sc, sh, gt, gpre, gpost, wg, wu, wd, tm):
    B, T, D = x.shape
    tok = pl.BlockSpec((1, tm, D), lambda b, i: (b, i, 0))
    vec = pl.BlockSpec((1, 1, D), lambda b, i: (b, 0, 0))
    const2 = lambda arr: pl.BlockSpec(arr.shape, lambda b, i: (0, 0))
    return pl.pallas_call(
        functools.partial(_ffn_kernel, tf=FFN_CHUNK),
        out_shape=jax.ShapeDtypeStruct((B, T, D), F32),
        grid=(B, T // tm),
        in_specs=[tok, vec, vec, vec, const2(gpre), const2(gpost), const2(wg), const2(wu), const2(wd)],
        out_specs=tok,
        compiler_params=_cparams(("arbitrary", "arbitrary"), [i in (6, 7, 8) for i in range(9)]),
        name="dense_swiglu",
    )(x, sc, sh, gt, gpre, gpost, wg, wu, wd)


MOE_TOKEN_TILE = 256
MOE_ROW_TILE = 512
SEG_ALIGN = 16
SEG_PIECES = (256, 128, 64, 32, 16)
MOE_SMALL_SEG = 128


def _route_kernel(x_ref, sc_ref, sh_ref, gpre_ref, rw_ref, rb_ref, h_ref, mi_ref, mp_ref, cnt_ref):
    tm = x_ref.shape[1]
    h = _rms_mod(x_ref[0], gpre_ref[...], sc_ref[0], sh_ref[0])
    h_ref[...] = h.astype(BF16)
    h_hi = h.astype(BF16)
    h_lo = (h - h_hi.astype(F32)).astype(BF16)
    w = rw_ref[...]
    w_hi = w.astype(BF16)
    w_lo = (w - w_hi.astype(F32)).astype(BF16)
    logits = jnp.dot(jnp.concatenate([h_hi, h_lo, h_hi], axis=1), jnp.concatenate([w_hi, w_hi, w_lo], axis=0),
                     preferred_element_type=F32) + rb_ref[...]
    lane = lax.broadcasted_iota(jnp.int32, logits.shape, 1)
    v1 = jnp.max(logits, axis=-1, keepdims=True)
    i1 = jnp.min(jnp.where(logits == v1, lane, LANES), axis=-1, keepdims=True)
    rest = jnp.where(lane == i1, -jnp.inf, logits)
    v2 = jnp.max(rest, axis=-1, keepdims=True)
    i2 = jnp.min(jnp.where(rest == v2, lane, LANES), axis=-1, keepdims=True)
    e2 = jnp.exp(v2 - v1)
    p1 = 1.0 / (1.0 + e2)
    p2 = e2 / (1.0 + e2)
    oh1 = (lane == i1).astype(F32)
    oh2 = (lane == i2).astype(F32)
    both = oh1 + oh2
    earlier = (lax.broadcasted_iota(jnp.int32, (tm, tm), 1)
               < lax.broadcasted_iota(jnp.int32, (tm, tm), 0)).astype(BF16)
    before = jnp.dot(earlier, both.astype(BF16), preferred_element_type=F32)
    r1 = jnp.sum(oh1 * before, axis=-1, keepdims=True).astype(jnp.int32)
    r2 = jnp.sum(oh2 * before, axis=-1, keepdims=True).astype(jnp.int32)
    col = lax.broadcasted_iota(jnp.int32, mi_ref.shape, 1)
    mi_ref[...] = jnp.where(col == 0, i1, jnp.where(col == 1, i2, jnp.where(col == 2, r1,
                                                                           jnp.where(col == 3, r2, 0))))
    mp_ref[...] = jnp.where(col == 0, p1, jnp.where(col == 1, p2, 0.0))
    cnt_ref[0] = jnp.sum(both, axis=0, keepdims=True).astype(jnp.int32)


def _segment_pieces(n_rows):
    out = []
    for s in SEG_PIECES:
        if s == MOE_TOKEN_TILE:
            out.append((n_rows == s, 0, s))
        else:
            out.append(((n_rows & s) != 0, pl.multiple_of((n_rows // (2 * s)) * (2 * s), SEG_ALIGN), s))
    return out


def _all_segments_small(cnt_ref, tile):
    most = cnt_ref[tile * N_EXPERTS]
    for e in range(1, N_EXPERTS):
        most = jnp.maximum(most, cnt_ref[tile * N_EXPERTS + e])
    return most <= MOE_SMALL_SEG


def _dispatch_kernel(seg_ref, cnt_ref, h_ref, mit_ref, init_ref, xs_ref, buf_ref, sem):
    del init_ref
    tm = h_ref.shape[0]
    i = pl.program_id(0)
    e1, e2 = mit_ref[0:1, :], mit_ref[1:2, :]
    r1, r2 = mit_ref[2:3, :], mit_ref[3:4, :]
    slot = i % 2

    def compact(cap):
        row = lax.broadcasted_iota(jnp.int32, (cap, tm), 0)
        select = jnp.concatenate(
            [jnp.logical_or(jnp.logical_and(e1 == e, r1 == row), jnp.logical_and(e2 == e, r2 == row))
             for e in range(N_EXPERTS)], axis=0).astype(BF16)
        rows = jnp.dot(select, h_ref[...], preferred_element_type=F32).astype(BF16)
        for e in range(N_EXPERTS):
            buf_ref[slot, e * tm:e * tm + cap, :] = rows[e * cap:(e + 1) * cap]

    small = _all_segments_small(cnt_ref, i)
    pl.when(small)(functools.partial(compact, MOE_SMALL_SEG))
    pl.when(jnp.logical_not(small))(functools.partial(compact, tm))

    def segment_copies(tile, buf, e):
        n = cnt_ref[tile * N_EXPERTS + e]
        n_rows = ((n + SEG_ALIGN - 1) // SEG_ALIGN) * SEG_ALIGN
        dst = pl.multiple_of(seg_ref[tile * N_EXPERTS + e], SEG_ALIGN)
        return [(cond, pltpu.make_async_copy(buf_ref.at[buf, pl.ds(e * tm + off, s), :],
                                             xs_ref.at[pl.ds(dst + off, s), :], sem))
                for cond, off, s in _segment_pieces(n_rows)]

    def for_all_segments(tile, buf, action):
        for e in range(N_EXPERTS):
            for cond, cp in segment_copies(tile, buf, e):
                pl.when(cond)(getattr(cp, action))

    @pl.when(i > 0)
    def _():
        for_all_segments(i - 1, 1 - slot, "wait")

    for_all_segments(i, slot, "start")

    @pl.when(i == pl.num_programs(0) - 1)
    def _():
        for_all_segments(i, slot, "wait")


def _expert_kernel(te_ref, nv_ref, xs_ref, wg_ref, wu_ref, wd_ref, ys_ref, acc_ref):
    del te_ref
    r = pl.program_id(0)
    f = pl.program_id(1)
    valid = r < nv_ref[0]

    @pl.when(jnp.logical_and(valid, f == 0))
    def _():
        acc_ref[...] = jnp.zeros_like(acc_ref)

    @pl.when(valid)
    def _():
        x = xs_ref[...]
        tf = wg_ref.shape[2]
        acc = acc_ref[...]
        for f0 in range(0, tf, FFN_CHUNK):
            f1 = min(f0 + FFN_CHUNK, tf)
            gate = jnp.dot(x, wg_ref[0, :, f0:f1], preferred_element_type=F32)
            up = jnp.dot(x, wu_ref[0, :, f0:f1], preferred_element_type=F32)
            acc = acc + jnp.dot((_silu(gate) * up).astype(BF16), wd_ref[0, f0:f1, :],
                                preferred_element_type=F32)
        acc_ref[...] = acc

    @pl.when(f == pl.num_programs(1) - 1)
    def _():
        @pl.when(valid)
        def _():
            ys_ref[...] = acc_ref[...].astype(ys_ref.dtype)

        @pl.when(jnp.logical_not(valid))
        def _():
            ys_ref[...] = jnp.zeros_like(ys_ref)


def _combine_kernel(seg_ref, cnt_ref, ys_ref, mi_ref, mp_ref, x_ref, gt_ref, gpost_ref, o_ref, win_ref,
                    pair_ref, sem):
    tm = x_ref.shape[1]
    i = pl.program_id(0) * pl.num_programs(1) + pl.program_id(1)
    n_tiles = pl.num_programs(0) * pl.num_programs(1)
    slot = i % 2

    def window_copy(tile, buf, e, rows):
        src = pl.multiple_of(seg_ref[tile * N_EXPERTS + e], SEG_ALIGN)
        return pltpu.make_async_copy(ys_ref.at[pl.ds(src, rows), :], win_ref.at[buf, pl.ds(e * rows, rows), :],
                                     sem.at[buf, e])

    def for_all_windows(tile, buf, action):
        small = _all_segments_small(cnt_ref, tile)
        for rows, cond in ((MOE_SMALL_SEG, small), (tm, jnp.logical_not(small))):
            @pl.when(cond)
            def _():
                for e in range(N_EXPERTS):
                    getattr(window_copy(tile, buf, e, rows), action)()

    @pl.when(i == 0)
    def _():
        for_all_windows(i, slot, "start")

    @pl.when(i + 1 < n_tiles)
    def _():
        for_all_windows(i + 1, 1 - slot, "start")

    for_all_windows(i, slot, "wait")
    e1, e2 = mi_ref[:, 0:1], mi_ref[:, 1:2]
    r1, r2 = mi_ref[:, 2:3], mi_ref[:, 3:4]

    def expand_rows(cap):
        col = lax.broadcasted_iota(jnp.int32, (tm, cap), 1)
        expand = jnp.concatenate(
            [jnp.concatenate([jnp.logical_and(e1 == e, r1 == col), jnp.logical_and(e2 == e, r2 == col)], axis=0)
             for e in range(N_EXPERTS)], axis=1).astype(BF16)
        pair_ref[...] = jnp.dot(expand, win_ref[slot, 0:N_EXPERTS * cap, :], preferred_element_type=F32)

    small = _all_segments_small(cnt_ref, i)
    pl.when(small)(functools.partial(expand_rows, MOE_SMALL_SEG))
    pl.when(jnp.logical_not(small))(functools.partial(expand_rows, tm))
    y = mp_ref[:, 0:1] * pair_ref[0:tm, :] + mp_ref[:, 1:2] * pair_ref[tm:2 * tm, :]
    y = y * lax.rsqrt(jnp.mean(y * y, axis=-1, keepdims=True) + RMS_EPS) * gpost_ref[...]
    o_ref[0] = x_ref[0] + gt_ref[0] * y


def _moe_ffn(x, sc, sh, gt, gpre, gpost, rw, rb, wg, wu, wd, tf):
    B, T, D = x.shape
    E, _, F = wg.shape
    tm = min(MOE_TOKEN_TILE, T)
    assert tm == MOE_TOKEN_TILE and E == N_EXPERTS
    nT = T // tm
    n_tok_tiles = B * nT
    N = B * T
    max_rows = 2 * N + n_tok_tiles * E * (SEG_ALIGN - 1) + E * (MOE_ROW_TILE - SEG_ALIGN)
    n_row_tiles = -(-max_rows // MOE_ROW_TILE) + 1
    P = n_row_tiles * MOE_ROW_TILE

    vec = pl.BlockSpec((1, 1, D), lambda b, j: (b, 0, 0))
    const2 = lambda arr: pl.BlockSpec(arr.shape, lambda b, j: (0, 0))
    flat = lambda n: pl.BlockSpec((tm, n), lambda b, j: (b * nT + j, 0))
    h, mi, mp, cnt = pl.pallas_call(
        _route_kernel,
        out_shape=(jax.ShapeDtypeStruct((N, D), BF16), jax.ShapeDtypeStruct((N, 8), jnp.int32),
                   jax.ShapeDtypeStruct((N, 8), F32), jax.ShapeDtypeStruct((n_tok_tiles, 1, LANES), jnp.int32)),
        grid=(B, nT),
        in_specs=[pl.BlockSpec((1, tm, D), lambda b, j: (b, j, 0)), vec, vec, const2(gpre), const2(rw),
                  const2(rb)],
        out_specs=(flat(D), flat(8), flat(8), pl.BlockSpec((1, 1, LANES), lambda b, j: (b * nT + j, 0, 0))),
        compiler_params=_cparams(("arbitrary", "arbitrary")),
        name="moe_route",
    )(x, sc, sh, gpre, rw, rb)

    counts = cnt[:, 0, :E]
    seg_len = (counts + SEG_ALIGN - 1) // SEG_ALIGN * SEG_ALIGN
    group_len = (jnp.sum(seg_len, axis=0) + MOE_ROW_TILE - 1) // MOE_ROW_TILE * MOE_ROW_TILE
    group_end = jnp.cumsum(group_len)
    seg_start = (group_end - group_len)[None, :] + jnp.cumsum(seg_len, axis=0) - seg_len
    seg_start = seg_start.reshape(-1).astype(jnp.int32)
    counts = counts.reshape(-1)
    n_valid = (group_end[-1:] // MOE_ROW_TILE).astype(jnp.int32)
    tile_first_row = jnp.arange(n_row_tiles, dtype=jnp.int32) * MOE_ROW_TILE
    tile_expert = jnp.minimum(jnp.sum(tile_first_row[:, None] >= group_end[None, :], axis=1), E - 1)
    tile_expert = tile_expert.astype(jnp.int32)

    xs = pl.pallas_call(
        _dispatch_kernel,
        out_shape=jax.ShapeDtypeStruct((P, D), BF16),
        grid_spec=pltpu.PrefetchScalarGridSpec(
            num_scalar_prefetch=2,
            grid=(n_tok_tiles,),
            in_specs=[pl.BlockSpec((tm, D), lambda i, seg, n: (i, 0)),
                      pl.BlockSpec((8, tm), lambda i, seg, n: (0, i)),
                      pl.BlockSpec(memory_space=pl.ANY)],
            out_specs=pl.BlockSpec(memory_space=pl.ANY),
            scratch_shapes=[pltpu.VMEM((2, E * tm, D), BF16), pltpu.SemaphoreType.DMA(())]),
        input_output_aliases={4: 0},
        compiler_params=_cparams(("arbitrary",)),
        name="moe_dispatch",
    )(seg_start, counts, h, mi.T, jnp.zeros((P, D), BF16))

    nf = F // tf
    live = lambda r, f, nv: jnp.where(r < nv[0], f, nf - 1)
    ys = pl.pallas_call(
        _expert_kernel,
        out_shape=jax.ShapeDtypeStruct((P, D), BF16),
        grid_spec=pltpu.PrefetchScalarGridSpec(
            num_scalar_prefetch=2,
            grid=(n_row_tiles, nf),
            in_specs=[pl.BlockSpec((MOE_ROW_TILE, D), lambda r, f, te, nv: (r, 0)),
                      pl.BlockSpec((1, D, tf), lambda r, f, te, nv: (te[r], 0, live(r, f, nv))),
                      pl.BlockSpec((1, D, tf), lambda r, f, te, nv: (te[r], 0, live(r, f, nv))),
                      pl.BlockSpec((1, tf, D), lambda r, f, te, nv: (te[r], live(r, f, nv), 0))],
            out_specs=pl.BlockSpec((MOE_ROW_TILE, D), lambda r, f, te, nv: (r, 0)),
            scratch_shapes=[pltpu.VMEM((MOE_ROW_TILE, D), F32)]),
        compiler_params=_cparams(("arbitrary", "arbitrary")),
        name="moe_experts",
    )(tile_expert, n_valid, xs, wg, wu, wd)

    return pl.pallas_call(
        _combine_kernel,
        out_shape=jax.ShapeDtypeStruct((B, T, D), F32),
        grid_spec=pltpu.PrefetchScalarGridSpec(
            num_scalar_prefetch=2,
            grid=(B, nT),
            in_specs=[pl.BlockSpec(memory_space=pl.ANY),
                      pl.BlockSpec((tm, 8), lambda b, j, seg, n: (b * nT + j, 0)),
                      pl.BlockSpec((tm, 8), lambda b, j, seg, n: (b * nT + j, 0)),
                      pl.BlockSpec((1, tm, D), lambda b, j, seg, n: (b, j, 0)),
                      pl.BlockSpec((1, 1, D), lambda b, j, seg, n: (b, 0, 0)),
                      pl.BlockSpec(gpost.shape, lambda b, j, seg, n: (0, 0))],
            out_specs=pl.BlockSpec((1, tm, D), lambda b, j, seg, n: (b, j, 0)),
            scratch_shapes=[pltpu.VMEM((2, E * tm, D), BF16), pltpu.VMEM((2 * tm, D), F32),
                            pltpu.SemaphoreType.DMA((2, E))]),
        compiler_params=_cparams(("arbitrary", "arbitrary")),
        name="moe_combine",
    )(seg_start, counts, ys, mi, mp, x, gt, gpost)


def _rope_tables(positions, groups):
    d = 32
    inv = 1.0 / (ROPE_THETA ** (jnp.arange(0, d, 2, dtype=F32) / d))
    ang = positions.astype(F32)[..., None] * inv
    cos, sin = jnp.cos(ang), jnp.sin(ang)
    cos = jnp.tile(jnp.concatenate([cos, cos], axis=-1), (1, 1, groups))
    sin = jnp.tile(jnp.concatenate([-sin, sin], axis=-1), (1, 1, groups))
    return cos, sin


def _pad_rows(w, rows, offset):
    out = jnp.zeros((rows, w.shape[1]), w.dtype)
    return out.at[offset:offset + w.shape[0]].set(w)


def kernel(x, c, positions, ada_w, ada_b, norm_mix_pre, norm_mix_post, norm_ffn_pre, norm_ffn_post, w_in, gla_gate_w2, gla_gate_b, gla_norm, diff_lambda, diff_subln, conv_w, conv_b, conv_ln_g, conv_ln_b, rwkv_mu, rwkv_w0, rwkv_w2, rwkv_a0, rwkv_a2, rwkv_g2, rwkv_k_k, rwkv_k_a, rwkv_r_k, rwkv_ln_g, rwkv_ln_b, w_branch, w_out, ffn_w_gate, ffn_w_up, ffn_w_down, router_w, router_b, moe_w_gate, moe_w_up, moe_w_down):
    B, T, D = x.shape
    L = ada_w.shape[0]
    W = D // N_BRANCH
    hk = gla_gate_b.shape[1]
    decay_rank = rwkv_w2.shape[1]
    a_rank = rwkv_a2.shape[1]
    gate_rank = rwkv_g2.shape[1]
    assert decay_rank + a_rank == LANES and 2 * hk == W and gate_rank == LANES
    n_mix = (3 * W + decay_rank + a_rank + gate_rank) + 3 * W + 3 * W
    sizes = (hk, hk, W, W, GLA_GATE_RANK, W, W, W, W, W, 3 * W + LANES + gate_rank, N_BRANCH * D)
    offs = [0]
    for s in sizes:
        offs.append(offs[-1] + s)
    assert offs[-1] == w_in.shape[2]
    tm = min(TOKEN_TILE, T)
    tb = min(SEQ_BLOCK, T)

    mod = _modulation(c, ada_w, ada_b)
    cos, sin = _rope_tables(positions, LANES // 32)

    for l in range(L):
        m = mod[l].reshape(B, 1, 6 * D)
        sh_m, sc_m, gt_m, sh_f, sc_f, gt_f = [m[:, :, i * D:(i + 1) * D] for i in range(6)]

        wl = w_in[l]
        gz_cols = jnp.zeros((D, LANES), F32).at[:, :GLA_GATE_RANK].set(wl[:, offs[4]:offs[5]])
        w_mix = jnp.concatenate([wl[:, offs[10]:offs[11]], wl[:, offs[0]:offs[4]], wl[:, offs[5]:offs[8]],
                                 gz_cols], axis=1).astype(BF16)
        w_conv = wl[:, offs[8]:offs[10]].astype(BF16)
        w_gate = wl[:, offs[11]:offs[12]].astype(BF16)
        proj, o_conv = _in_projection_conv(x, sc_m, sh_m, norm_mix_pre[l][None], w_conv, w_mix,
                                           _pad_rows(conv_w[l], 32, 0), conv_b[l][None],
                                           conv_ln_g[l][None], conv_ln_b[l][None], tm)

        o_diff = _diff_attention(proj, cos, sin, diff_lambda[l],
                                 diff_subln[l][None], col_q=7, col_k=8, col_v=9, layer_idx=l)
        vecs = dict(mu=rwkv_mu[l][None], w0=rwkv_w0[l][None], a0=rwkv_a0[l][None],
                    kk=rwkv_k_k[l][None], ka=rwkv_k_a[l][None], rk=rwkv_r_k[l].reshape(1, W),
                    lng=rwkv_ln_g[l][None], lnb=rwkv_ln_b[l][None])
        o_gla, o_rwkv = _gla_rwkv7(
            proj, _pad_rows(gla_gate_w2[l], LANES, 0), gla_gate_b[l][None],
            jnp.tile(gla_norm[l], GLA_HEADS)[None], 4, 5, 6, n_mix // LANES,
            vecs, _pad_rows(rwkv_w2[l], LANES, 0).astype(BF16),
            _pad_rows(rwkv_a2[l], LANES, decay_rank).astype(BF16), rwkv_g2[l].astype(BF16), 0, tb)
        x = _merge(x, sc_m, sh_m, gt_m, norm_mix_pre[l][None], norm_mix_post[l][None], w_gate,
                   (o_gla, o_diff, o_conv, o_rwkv), w_branch[l].astype(BF16), w_out[l].astype(BF16), tm)

        i = l // 2
        if l % 2 == 0:
            x = _dense_ffn(x, sc_f, sh_f, gt_f, norm_ffn_pre[l][None], norm_ffn_post[l][None],
                           ffn_w_gate[i].astype(BF16), ffn_w_up[i].astype(BF16),
                           ffn_w_down[i].astype(BF16), tm)
        else:
            rw = jnp.zeros((D, LANES), F32).at[:, :N_EXPERTS].set(router_w[i])
            rb = jnp.full((1, LANES), -jnp.inf, F32).at[0, :N_EXPERTS].set(router_b[i])
            x = _moe_ffn(x, sc_f, sh_f, gt_f, norm_ffn_pre[l][None], norm_ffn_post[l][None], rw, rb,
                         moe_w_gate[i].astype(BF16), moe_w_up[i].astype(BF16),
                         moe_w_down[i].astype(BF16), tf=moe_w_gate.shape[3])
    return x
```
